```python
import math
import jax, jax.numpy as jnp
from jax import lax
import numpy as np

D_MODEL = 1024
BATCH = 8
SEQ = 4096
DEPTH = 1

CHUNK = 64
D_MIX = D_MODEL
S5_WIDTH = D_MIX // 2
S5_GROUP_CH = 16
S5_GROUPS = S5_WIDTH // S5_GROUP_CH
S5_STATE = 64
CONV_WIDTH = D_MIX - S5_WIDTH
CONV_HEAD_DIM = 64
CONV_HEADS = CONV_WIDTH // CONV_HEAD_DIM
CONV_K = 31
D_FF = 2816
IN_COLS = S5_WIDTH + 2 * CONV_WIDTH
EPS = 1e-6

kernel_name = "hybrid_s5_conformer_conv_macaron"


def rms_norm(x, g):
    xf = x.astype(jnp.float32)
    y = xf * lax.rsqrt(jnp.mean(xf * xf, axis=-1, keepdims=True) + EPS)
    return (y * g.astype(jnp.float32)).astype(x.dtype)


def swiglu_ffn(h, w_gate, w_up, w_down):
    return (jax.nn.silu(h @ w_gate) * (h @ w_up)) @ w_down


def _complex_affine_combine(left, right):
    a_re_i, a_im_i, b_re_i, b_im_i = left
    a_re_j, a_im_j, b_re_j, b_im_j = right
    a_re = a_re_j * a_re_i - a_im_j * a_im_i
    a_im = a_re_j * a_im_i + a_im_j * a_re_i
    b_re = a_re_j * b_re_i - a_im_j * b_im_i + b_re_j
    b_im = a_re_j * b_im_i + a_im_j * b_re_i + b_im_j
    return (a_re, a_im, b_re, b_im)


def s5_mixer(u, lam_re, lam_im, log_dt, b_re, b_im, c_re, c_im, d_skip, w_glu, b_glu):
    bsz, seq_len, _ = u.shape
    uf = u.astype(jnp.float32).reshape(bsz, seq_len, S5_GROUPS, S5_GROUP_CH)
    lr = lam_re.astype(jnp.float32)
    li = lam_im.astype(jnp.float32)
    dt = jnp.exp(log_dt.astype(jnp.float32))[:, None]
    mag = jnp.exp(lr * dt)
    abar_re = mag * jnp.cos(li * dt)
    abar_im = mag * jnp.sin(li * dt)
    den = lr * lr + li * li
    num_re = abar_re - 1.0
    num_im = abar_im
    f_re = ((num_re * lr + num_im * li) / den)[..., None]
    f_im = ((num_im * lr - num_re * li) / den)[..., None]
    br = b_re.astype(jnp.float32)
    bi = b_im.astype(jnp.float32)
    bbar_re = f_re * br - f_im * bi
    bbar_im = f_re * bi + f_im * br
    bu_re = jnp.einsum('blgc,gpc->blgp', uf, bbar_re)
    bu_im = jnp.einsum('blgc,gpc->blgp', uf, bbar_im)
    a_re = jnp.broadcast_to(abar_re[None, None], (1, seq_len, S5_GROUPS, S5_STATE))
    a_im = jnp.broadcast_to(abar_im[None, None], (1, seq_len, S5_GROUPS, S5_STATE))
    _, _, s_re, s_im = lax.associative_scan(_complex_affine_combine,
                                            (a_re, a_im, bu_re, bu_im), axis=1)
    y = (jnp.einsum('blgp,gcp->blgc', s_re, c_re.astype(jnp.float32))
         - jnp.einsum('blgp,gcp->blgc', s_im, c_im.astype(jnp.float32)))
    y = y + d_skip.astype(jnp.float32).reshape(S5_GROUPS, S5_GROUP_CH) * uf
    y = jax.nn.gelu(y.reshape(bsz, seq_len, S5_WIDTH)).astype(u.dtype)
    return y * jax.nn.sigmoid(y @ w_glu + b_glu)


def conv_module_mixer(v, w_dw, b_dw, ln_g, ln_b):
    bsz, seq_len, _ = v.shape
    z = v[..., :CONV_WIDTH] * jax.nn.sigmoid(v[..., CONV_WIDTH:])
    z = lax.conv_general_dilated(
        z, w_dw[:, None, :], window_strides=(1,), padding=[(CONV_K - 1, 0)],
        dimension_numbers=('NWC', 'WIO', 'NWC'), feature_group_count=CONV_WIDTH) + b_dw
    zf = z.astype(jnp.float32).reshape(bsz, seq_len, CONV_HEADS, CONV_HEAD_DIM)
    mu = jnp.mean(zf, axis=-1, keepdims=True)
    var = jnp.mean(jnp.square(zf - mu), axis=-1, keepdims=True)
    zn = ((zf - mu) * lax.rsqrt(var + EPS)).reshape(bsz, seq_len, CONV_WIDTH)
    zn = zn * ln_g.astype(jnp.float32) + ln_b.astype(jnp.float32)
    return jax.nn.silu(zn).astype(v.dtype)


def _fwd_setup_inputs(seed: int = 0) -> dict:
    key = jax.random.key(seed)
    ks = jax.random.split(key, 32)
    f32 = jnp.float32

    def nrm(k, shape, scale):
        return jax.random.normal(k, shape, f32) * scale

    def gain(k, shape):
        return 1.0 + 0.02 * jax.random.normal(k, shape, f32)

    L = DEPTH
    n_idx = jnp.arange(S5_STATE, dtype=f32)
    lam_re = -0.5 * (1.0 + 0.05 * jax.random.normal(ks[10], (L, S5_GROUPS, S5_STATE), f32))
    lam_im = jnp.broadcast_to(math.pi * n_idx, (L, S5_GROUPS, S5_STATE)) \
        + 0.01 * jax.random.normal(ks[11], (L, S5_GROUPS, S5_STATE), f32)
    log_dt = jax.random.uniform(ks[12], (L, S5_GROUPS), f32, math.log(1e-3), math.log(1e-1))
    return {
        "x": jax.random.normal(ks[0], (BATCH, SEQ, D_MODEL), f32),
        "ffn1_norm": gain(ks[1], (L, D_MODEL)),
        "ffn1_w_gate": nrm(ks[2], (L, D_MODEL, D_FF), D_MODEL ** -0.5),
        "ffn1_w_up": nrm(ks[3], (L, D_MODEL, D_FF), D_MODEL ** -0.5),
        "ffn1_w_down": nrm(ks[4], (L, D_FF, D_MODEL), D_FF ** -0.5),
        "mix_norm": gain(ks[5], (L, D_MODEL)),
        "w_in": nrm(ks[6], (L, D_MODEL, IN_COLS), D_MODEL ** -0.5),
        "s5_lam_re": lam_re,
        "s5_lam_im": lam_im,
        "s5_log_dt": log_dt,
        "s5_b_re": nrm(ks[13], (L, S5_GROUPS, S5_STATE, S5_GROUP_CH), (2 * S5_GROUP_CH) ** -0.5),
        "s5_b_im": nrm(ks[14], (L, S5_GROUPS, S5_STATE, S5_GROUP_CH), (2 * S5_GROUP_CH) ** -0.5),
        "s5_c_re": nrm(ks[15], (L, S5_GROUPS, S5_GROUP_CH, S5_STATE), (2 * S5_STATE) ** -0.5),
        "s5_c_im": nrm(ks[16], (L, S5_GROUPS, S5_GROUP_CH, S5_STATE), (2 * S5_STATE) ** -0.5),
        "s5_d": gain(ks[17], (L, S5_WIDTH)),
        "s5_w_glu": nrm(ks[18], (L, S5_WIDTH, S5_WIDTH), S5_WIDTH ** -0.5),
        "s5_b_glu": nrm(ks[19], (L, S5_WIDTH), 0.02),
        "conv_w_dw": nrm(ks[20], (L, CONV_K, CONV_WIDTH), CONV_K ** -0.5),
        "conv_b_dw": nrm(ks[21], (L, CONV_WIDTH), 0.02),
        "conv_ln_g": gain(ks[22], (L, CONV_WIDTH)),
        "conv_ln_b": nrm(ks[23], (L, CONV_WIDTH), 0.02),
        "w_out": nrm(ks[24], (L, D_MIX, D_MODEL), D_MIX ** -0.5),
        "ffn2_norm": gain(ks[25], (L, D_MODEL)),
        "ffn2_w_gate": nrm(ks[26], (L, D_MODEL, D_FF), D_MODEL ** -0.5),
        "ffn2_w_up": nrm(ks[27], (L, D_MODEL, D_FF), D_MODEL ** -0.5),
        "ffn2_w_down": nrm(ks[28], (L, D_FF, D_MODEL), D_FF ** -0.5),
        "final_norm": gain(ks[29], (D_MODEL,)),
    }


def _fwd_reference(x, ffn1_norm, ffn1_w_gate, ffn1_w_up, ffn1_w_down, mix_norm, w_in,
              s5_lam_re, s5_lam_im, s5_log_dt, s5_b_re, s5_b_im, s5_c_re, s5_c_im,
              s5_d, s5_w_glu, s5_b_glu, conv_w_dw, conv_b_dw, conv_ln_g, conv_ln_b,
              w_out, ffn2_norm, ffn2_w_gate, ffn2_w_up, ffn2_w_down, final_norm):
    for l in range(DEPTH):
        h = rms_norm(x, ffn1_norm[l])
        x = x + 0.5 * swiglu_ffn(h, ffn1_w_gate[l], ffn1_w_up[l], ffn1_w_down[l])
        h = rms_norm(x, mix_norm[l])
        u = h @ w_in[l]
        y_s5 = s5_mixer(u[..., :S5_WIDTH], s5_lam_re[l], s5_lam_im[l], s5_log_dt[l],
                        s5_b_re[l], s5_b_im[l], s5_c_re[l], s5_c_im[l], s5_d[l],
                        s5_w_glu[l], s5_b_glu[l])
        y_conv = conv_module_mixer(u[..., S5_WIDTH:], conv_w_dw[l], conv_b_dw[l],
                                   conv_ln_g[l], conv_ln_b[l])
        x = x + jnp.concatenate([y_s5, y_conv], axis=-1) @ w_out[l]
        h = rms_norm(x, ffn2_norm[l])
        x = x + 0.5 * swiglu_ffn(h, ffn2_w_gate[l], ffn2_w_up[l], ffn2_w_down[l])
    return rms_norm(x, final_norm)


import jax as _jax
import jax.numpy as _jnp

TWIN_FORMAT = 'train_step'
FWD_PARAMS = ['x', 'ffn1_norm', 'ffn1_w_gate', 'ffn1_w_up', 'ffn1_w_down', 'mix_norm', 'w_in', 's5_lam_re', 's5_lam_im', 's5_log_dt', 's5_b_re', 's5_b_im', 's5_c_re', 's5_c_im', 's5_d', 's5_w_glu', 's5_b_glu', 'conv_w_dw', 'conv_b_dw', 'conv_ln_g', 'conv_ln_b', 'w_out', 'ffn2_norm', 'ffn2_w_gate', 'ffn2_w_up', 'ffn2_w_down', 'final_norm']
TWIN_WEIGHTS = ['ffn1_norm', 'ffn1_w_gate', 'ffn1_w_up', 'ffn1_w_down', 'mix_norm', 'w_in', 's5_lam_re', 's5_lam_im', 's5_log_dt', 's5_b_re', 's5_b_im', 's5_c_re', 's5_c_im', 's5_d', 's5_w_glu', 's5_b_glu', 'conv_w_dw', 'conv_b_dw', 'conv_ln_g', 'conv_ln_b', 'w_out', 'ffn2_norm', 'ffn2_w_gate', 'ffn2_w_up', 'ffn2_w_down', 'final_norm']
TWIN_DIFF_INPUT = 'x'
TWIN_INPUTS = ['x', 'ffn1_norm', 'ffn1_w_gate', 'ffn1_w_up', 'ffn1_w_down', 'mix_norm', 'w_in', 's5_lam_re', 's5_lam_im', 's5_log_dt', 's5_b_re', 's5_b_im', 's5_c_re', 's5_c_im', 's5_d', 's5_w_glu', 's5_b_glu', 'conv_w_dw', 'conv_b_dw', 'conv_ln_g', 'conv_ln_b', 'w_out', 'ffn2_norm', 'ffn2_w_gate', 'ffn2_w_up', 'ffn2_w_down', 'final_norm', 'loss_target', 'm_ffn1_norm', 'm_ffn1_w_gate', 'm_ffn1_w_up', 'm_ffn1_w_down', 'm_mix_norm', 'm_w_in', 'm_s5_lam_re', 'm_s5_lam_im', 'm_s5_log_dt', 'm_s5_b_re', 'm_s5_b_im', 'm_s5_c_re', 'm_s5_c_im', 'm_s5_d', 'm_s5_w_glu', 'm_s5_b_glu', 'm_conv_w_dw', 'm_conv_b_dw', 'm_conv_ln_g', 'm_conv_ln_b', 'm_w_out', 'm_ffn2_norm', 'm_ffn2_w_gate', 'm_ffn2_w_up', 'm_ffn2_w_down', 'm_final_norm', 'v_ffn1_norm', 'v_ffn1_w_gate', 'v_ffn1_w_up', 'v_ffn1_w_down', 'v_mix_norm', 'v_w_in', 'v_s5_lam_re', 'v_s5_lam_im', 'v_s5_log_dt', 'v_s5_b_re', 'v_s5_b_im', 'v_s5_c_re', 'v_s5_c_im', 'v_s5_d', 'v_s5_w_glu', 'v_s5_b_glu', 'v_conv_w_dw', 'v_conv_b_dw', 'v_conv_ln_g', 'v_conv_ln_b', 'v_w_out', 'v_ffn2_norm', 'v_ffn2_w_gate', 'v_ffn2_w_up', 'v_ffn2_w_down', 'v_final_norm']
TWIN_OUTPUTS = ['loss', 'grad_x', 'grad_ffn1_norm', 'grad_ffn1_w_gate', 'grad_ffn1_w_up', 'grad_ffn1_w_down', 'grad_mix_norm', 'grad_w_in', 'grad_s5_lam_re', 'grad_s5_lam_im', 'grad_s5_log_dt', 'grad_s5_b_re', 'grad_s5_b_im', 'grad_s5_c_re', 'grad_s5_c_im', 'grad_s5_d', 'grad_s5_w_glu', 'grad_s5_b_glu', 'grad_conv_w_dw', 'grad_conv_b_dw', 'grad_conv_ln_g', 'grad_conv_ln_b', 'grad_w_out', 'grad_ffn2_norm', 'grad_ffn2_w_gate', 'grad_ffn2_w_up', 'grad_ffn2_w_down', 'grad_final_norm', 'delta_ffn1_norm', 'delta_ffn1_w_gate', 'delta_ffn1_w_up', 'delta_ffn1_w_down', 'delta_mix_norm', 'delta_w_in', 'delta_s5_lam_re', 'delta_s5_lam_im', 'delta_s5_log_dt', 'delta_s5_b_re', 'delta_s5_b_im', 'delta_s5_c_re', 'delta_s5_c_im', 'delta_s5_d', 'delta_s5_w_glu', 'delta_s5_b_glu', 'delta_conv_w_dw', 'delta_conv_b_dw', 'delta_conv_ln_g', 'delta_conv_ln_b', 'delta_w_out', 'delta_ffn2_norm', 'delta_ffn2_w_gate', 'delta_ffn2_w_up', 'delta_ffn2_w_down', 'delta_final_norm', 'new_m_ffn1_norm', 'new_m_ffn1_w_gate', 'new_m_ffn1_w_up', 'new_m_ffn1_w_down', 'new_m_mix_norm', 'new_m_w_in', 'new_m_s5_lam_re', 'new_m_s5_lam_im', 'new_m_s5_log_dt', 'new_m_s5_b_re', 'new_m_s5_b_im', 'new_m_s5_c_re', 'new_m_s5_c_im', 'new_m_s5_d', 'new_m_s5_w_glu', 'new_m_s5_b_glu', 'new_m_conv_w_dw', 'new_m_conv_b_dw', 'new_m_conv_ln_g', 'new_m_conv_ln_b', 'new_m_w_out', 'new_m_ffn2_norm', 'new_m_ffn2_w_gate', 'new_m_ffn2_w_up', 'new_m_ffn2_w_down', 'new_m_final_norm', 'new_v_ffn1_norm', 'new_v_ffn1_w_gate', 'new_v_ffn1_w_up', 'new_v_ffn1_w_down', 'new_v_mix_norm', 'new_v_w_in', 'new_v_s5_lam_re', 'new_v_s5_lam_im', 'new_v_s5_log_dt', 'new_v_s5_b_re', 'new_v_s5_b_im', 'new_v_s5_c_re', 'new_v_s5_c_im', 'new_v_s5_d', 'new_v_s5_w_glu', 'new_v_s5_b_glu', 'new_v_conv_w_dw', 'new_v_conv_b_dw', 'new_v_conv_ln_g', 'new_v_conv_ln_b', 'new_v_w_out', 'new_v_ffn2_norm', 'new_v_ffn2_w_gate', 'new_v_ffn2_w_up', 'new_v_ffn2_w_down', 'new_v_final_norm']
TWIN_LEAF_KINDS = {'loss': 'loss', 'grad_x': 'grad_x', 'grad_ffn1_norm': 'grad_w', 'grad_ffn1_w_gate': 'grad_w', 'grad_ffn1_w_up': 'grad_w', 'grad_ffn1_w_down': 'grad_w', 'grad_mix_norm': 'grad_w', 'grad_w_in': 'grad_w', 'grad_s5_lam_re': 'grad_w', 'grad_s5_lam_im': 'grad_w', 'grad_s5_log_dt': 'grad_w', 'grad_s5_b_re': 'grad_w', 'grad_s5_b_im': 'grad_w', 'grad_s5_c_re': 'grad_w', 'grad_s5_c_im': 'grad_w', 'grad_s5_d': 'grad_w', 'grad_s5_w_glu': 'grad_w', 'grad_s5_b_glu': 'grad_w', 'grad_conv_w_dw': 'grad_w', 'grad_conv_b_dw': 'grad_w', 'grad_conv_ln_g': 'grad_w', 'grad_conv_ln_b': 'grad_w', 'grad_w_out': 'grad_w', 'grad_ffn2_norm': 'grad_w', 'grad_ffn2_w_gate': 'grad_w', 'grad_ffn2_w_up': 'grad_w', 'grad_ffn2_w_down': 'grad_w', 'grad_final_norm': 'grad_w', 'delta_ffn1_norm': 'delta_w', 'delta_ffn1_w_gate': 'delta_w', 'delta_ffn1_w_up': 'delta_w', 'delta_ffn1_w_down': 'delta_w', 'delta_mix_norm': 'delta_w', 'delta_w_in': 'delta_w', 'delta_s5_lam_re': 'delta_w', 'delta_s5_lam_im': 'delta_w', 'delta_s5_log_dt': 'delta_w', 'delta_s5_b_re': 'delta_w', 'delta_s5_b_im': 'delta_w', 'delta_s5_c_re': 'delta_w', 'delta_s5_c_im': 'delta_w', 'delta_s5_d': 'delta_w', 'delta_s5_w_glu': 'delta_w', 'delta_s5_b_glu': 'delta_w', 'delta_conv_w_dw': 'delta_w', 'delta_conv_b_dw': 'delta_w', 'delta_conv_ln_g': 'delta_w', 'delta_conv_ln_b': 'delta_w', 'delta_w_out': 'delta_w', 'delta_ffn2_norm': 'delta_w', 'delta_ffn2_w_gate': 'delta_w', 'delta_ffn2_w_up': 'delta_w', 'delta_ffn2_w_down': 'delta_w', 'delta_final_norm': 'delta_w', 'new_m_ffn1_norm': 'new_m', 'new_m_ffn1_w_gate': 'new_m', 'new_m_ffn1_w_up': 'new_m', 'new_m_ffn1_w_down': 'new_m', 'new_m_mix_norm': 'new_m', 'new_m_w_in': 'new_m', 'new_m_s5_lam_re': 'new_m', 'new_m_s5_lam_im': 'new_m', 'new_m_s5_log_dt': 'new_m', 'new_m_s5_b_re': 'new_m', 'new_m_s5_b_im': 'new_m', 'new_m_s5_c_re': 'new_m', 'new_m_s5_c_im': 'new_m', 'new_m_s5_d': 'new_m', 'new_m_s5_w_glu': 'new_m', 'new_m_s5_b_glu': 'new_m', 'new_m_conv_w_dw': 'new_m', 'new_m_conv_b_dw': 'new_m', 'new_m_conv_ln_g': 'new_m', 'new_m_conv_ln_b': 'new_m', 'new_m_w_out': 'new_m', 'new_m_ffn2_norm': 'new_m', 'new_m_ffn2_w_gate': 'new_m', 'new_m_ffn2_w_up': 'new_m', 'new_m_ffn2_w_down': 'new_m', 'new_m_final_norm': 'new_m', 'new_v_ffn1_norm': 'new_v', 'new_v_ffn1_w_gate': 'new_v', 'new_v_ffn1_w_up': 'new_v', 'new_v_ffn1_w_down': 'new_v', 'new_v_mix_norm': 'new_v', 'new_v_w_in': 'new_v', 'new_v_s5_lam_re': 'new_v', 'new_v_s5_lam_im': 'new_v', 'new_v_s5_log_dt': 'new_v', 'new_v_s5_b_re': 'new_v', 'new_v_s5_b_im': 'new_v', 'new_v_s5_c_re': 'new_v', 'new_v_s5_c_im': 'new_v', 'new_v_s5_d': 'new_v', 'new_v_s5_w_glu': 'new_v', 'new_v_s5_b_glu': 'new_v', 'new_v_conv_w_dw': 'new_v', 'new_v_conv_b_dw': 'new_v', 'new_v_conv_ln_g': 'new_v', 'new_v_conv_ln_b': 'new_v', 'new_v_w_out': 'new_v', 'new_v_ffn2_norm': 'new_v', 'new_v_ffn2_w_gate': 'new_v', 'new_v_ffn2_w_up': 'new_v', 'new_v_ffn2_w_down': 'new_v', 'new_v_final_norm': 'new_v'}


def _forward(args):
    return _fwd_reference(*[args[k] for k in FWD_PARAMS])


def _output_shape():
    out = _jax.eval_shape(lambda: _forward(_fwd_setup_inputs(0)))
    return out.shape, out.dtype

N_MICROBATCH = 1
ADAM_LR = 0.001
ADAM_B1 = 0.9
ADAM_B2 = 0.999
ADAM_EPS = 1e-08
ADAM_WD = 0.01
ADAM_STEP = 10
PER_EXAMPLE_BATCH_AXIS = {'x': 0, 'loss_target': 0}
SHARED_INPUTS = []
_WEIGHT_DTYPES = {'ffn1_norm': _jnp.float32, 'ffn1_w_gate': _jnp.float32, 'ffn1_w_up': _jnp.float32, 'ffn1_w_down': _jnp.float32, 'mix_norm': _jnp.float32, 'w_in': _jnp.float32, 's5_lam_re': _jnp.float32, 's5_lam_im': _jnp.float32, 's5_log_dt': _jnp.float32, 's5_b_re': _jnp.float32, 's5_b_im': _jnp.float32, 's5_c_re': _jnp.float32, 's5_c_im': _jnp.float32, 's5_d': _jnp.float32, 's5_w_glu': _jnp.float32, 's5_b_glu': _jnp.float32, 'conv_w_dw': _jnp.float32, 'conv_b_dw': _jnp.float32, 'conv_ln_g': _jnp.float32, 'conv_ln_b': _jnp.float32, 'w_out': _jnp.float32, 'ffn2_norm': _jnp.float32, 'ffn2_w_gate': _jnp.float32, 'ffn2_w_up': _jnp.float32, 'ffn2_w_down': _jnp.float32, 'final_norm': _jnp.float32}
MOMENT_SCALE = {'ffn1_norm': 8.542051e-02, 'ffn1_w_gate': 3.325983e-02, 'ffn1_w_up': 3.216567e-02, 'ffn1_w_down': 5.329673e-02, 'mix_norm': 8.372426e-02, 'w_in': 7.006865e-02, 's5_lam_re': 3.939701e-03, 's5_lam_im': 3.783870e-03, 's5_log_dt': 3.055591e+00, 's5_b_re': 2.129314e-03, 's5_b_im': 2.122634e-03, 's5_c_re': 4.298053e-03, 's5_c_im': 4.257200e-03, 's5_d': 7.064601e-02, 's5_w_glu': 1.580675e-02, 's5_b_glu': 2.458956e-02, 'conv_w_dw': 1.011542e-01, 'conv_b_dw': 2.161297e-01, 'conv_ln_g': 1.173412e-01, 'conv_ln_b': 1.066173e-01, 'w_out': 7.829839e-02, 'ffn2_norm': 6.612380e-02, 'ffn2_w_gate': 2.771636e-02, 'ffn2_w_up': 2.691061e-02, 'ffn2_w_down': 4.466622e-02, 'final_norm': 3.195239e+01}


def _to_microbatches(a, axis):
    t = _jnp.moveaxis(a, axis, 0)
    t = t.reshape((N_MICROBATCH, t.shape[0] // N_MICROBATCH) + t.shape[1:])
    return _jnp.moveaxis(t, 1, axis + 1)


def setup_inputs(seed: int = 0) -> dict:
    inp = _fwd_setup_inputs(seed)
    key = _jax.random.fold_in(_jax.random.key(seed), 7919)
    shape, _ = _output_shape()
    out = dict(inp)
    out["loss_target"] = _jax.random.normal(_jax.random.fold_in(key, 0), shape, _jnp.float32)
    for i, name in enumerate(TWIN_WEIGHTS):
        w = inp[name].astype(_jnp.float32)
        if MOMENT_SCALE is None:
            s = _jnp.sqrt(_jnp.mean(_jnp.square(w)) + 1e-30)
        else:
            s = MOMENT_SCALE[name]
        km, kv = _jax.random.split(_jax.random.fold_in(key, i + 1))
        out[name] = w
        out["m_" + name] = s * _jax.random.normal(km, w.shape, _jnp.float32)
        out["v_" + name] = (s * s) * _jax.random.uniform(kv, w.shape, _jnp.float32, 0.5, 1.5)
    if N_MICROBATCH > 1:
        for name, axis in PER_EXAMPLE_BATCH_AXIS.items():
            out[name] = _to_microbatches(out[name], axis)
    return {'x': out['x'], 'ffn1_norm': out['ffn1_norm'], 'ffn1_w_gate': out['ffn1_w_gate'], 'ffn1_w_up': out['ffn1_w_up'], 'ffn1_w_down': out['ffn1_w_down'], 'mix_norm': out['mix_norm'], 'w_in': out['w_in'], 's5_lam_re': out['s5_lam_re'], 's5_lam_im': out['s5_lam_im'], 's5_log_dt': out['s5_log_dt'], 's5_b_re': out['s5_b_re'], 's5_b_im': out['s5_b_im'], 's5_c_re': out['s5_c_re'], 's5_c_im': out['s5_c_im'], 's5_d': out['s5_d'], 's5_w_glu': out['s5_w_glu'], 's5_b_glu': out['s5_b_glu'], 'conv_w_dw': out['conv_w_dw'], 'conv_b_dw': out['conv_b_dw'], 'conv_ln_g': out['conv_ln_g'], 'conv_ln_b': out['conv_ln_b'], 'w_out': out['w_out'], 'ffn2_norm': out['ffn2_norm'], 'ffn2_w_gate': out['ffn2_w_gate'], 'ffn2_w_up': out['ffn2_w_up'], 'ffn2_w_down': out['ffn2_w_down'], 'final_norm': out['final_norm'], 'loss_target': out['loss_target'], 'm_ffn1_norm': out['m_ffn1_norm'], 'm_ffn1_w_gate': out['m_ffn1_w_gate'], 'm_ffn1_w_up': out['m_ffn1_w_up'], 'm_ffn1_w_down': out['m_ffn1_w_down'], 'm_mix_norm': out['m_mix_norm'], 'm_w_in': out['m_w_in'], 'm_s5_lam_re': out['m_s5_lam_re'], 'm_s5_lam_im': out['m_s5_lam_im'], 'm_s5_log_dt': out['m_s5_log_dt'], 'm_s5_b_re': out['m_s5_b_re'], 'm_s5_b_im': out['m_s5_b_im'], 'm_s5_c_re': out['m_s5_c_re'], 'm_s5_c_im': out['m_s5_c_im'], 'm_s5_d': out['m_s5_d'], 'm_s5_w_glu': out['m_s5_w_glu'], 'm_s5_b_glu': out['m_s5_b_glu'], 'm_conv_w_dw': out['m_conv_w_dw'], 'm_conv_b_dw': out['m_conv_b_dw'], 'm_conv_ln_g': out['m_conv_ln_g'], 'm_conv_ln_b': out['m_conv_ln_b'], 'm_w_out': out['m_w_out'], 'm_ffn2_norm': out['m_ffn2_norm'], 'm_ffn2_w_gate': out['m_ffn2_w_gate'], 'm_ffn2_w_up': out['m_ffn2_w_up'], 'm_ffn2_w_down': out['m_ffn2_w_down'], 'm_final_norm': out['m_final_norm'], 'v_ffn1_norm': out['v_ffn1_norm'], 'v_ffn1_w_gate': out['v_ffn1_w_gate'], 'v_ffn1_w_up': out['v_ffn1_w_up'], 'v_ffn1_w_down': out['v_ffn1_w_down'], 'v_mix_norm': out['v_mix_norm'], 'v_w_in': out['v_w_in'], 'v_s5_lam_re': out['v_s5_lam_re'], 'v_s5_lam_im': out['v_s5_lam_im'], 'v_s5_log_dt': out['v_s5_log_dt'], 'v_s5_b_re': out['v_s5_b_re'], 'v_s5_b_im': out['v_s5_b_im'], 'v_s5_c_re': out['v_s5_c_re'], 'v_s5_c_im': out['v_s5_c_im'], 'v_s5_d': out['v_s5_d'], 'v_s5_w_glu': out['v_s5_w_glu'], 'v_s5_b_glu': out['v_s5_b_glu'], 'v_conv_w_dw': out['v_conv_w_dw'], 'v_conv_b_dw': out['v_conv_b_dw'], 'v_conv_ln_g': out['v_conv_ln_g'], 'v_conv_ln_b': out['v_conv_ln_b'], 'v_w_out': out['v_w_out'], 'v_ffn2_norm': out['v_ffn2_norm'], 'v_ffn2_w_gate': out['v_ffn2_w_gate'], 'v_ffn2_w_up': out['v_ffn2_w_up'], 'v_ffn2_w_down': out['v_ffn2_w_down'], 'v_final_norm': out['v_final_norm']}


def _loss(weights, diff, rest, loss_target):
    with _jax.named_scope("forward"):
        args = {**rest, TWIN_DIFF_INPUT: diff, **{k: w.astype(_WEIGHT_DTYPES[k]) for k, w in weights.items()}}
        y = _forward(args)
    with _jax.named_scope("loss_head"):
        err = _jnp.square(y.astype(_jnp.float32) - loss_target)
        return 0.5 * _jnp.sum(_jnp.mean(err, axis=-1)) if err.ndim else 0.5 * err


def _adamw(w, g, m, v):
    m = ADAM_B1 * m + (1.0 - ADAM_B1) * g
    v = ADAM_B2 * v + (1.0 - ADAM_B2) * _jnp.square(g)
    m_hat = m / (1.0 - ADAM_B1 ** ADAM_STEP)
    v_hat = v / (1.0 - ADAM_B2 ** ADAM_STEP)
    delta = -ADAM_LR * (m_hat / (_jnp.sqrt(v_hat) + ADAM_EPS) + ADAM_WD * w)
    return delta, m, v


def reference(x, ffn1_norm, ffn1_w_gate, ffn1_w_up, ffn1_w_down, mix_norm, w_in, s5_lam_re, s5_lam_im, s5_log_dt, s5_b_re, s5_b_im, s5_c_re, s5_c_im, s5_d, s5_w_glu, s5_b_glu, conv_w_dw, conv_b_dw, conv_ln_g, conv_ln_b, w_out, ffn2_norm, ffn2_w_gate, ffn2_w_up, ffn2_w_down, final_norm, loss_target, m_ffn1_norm, m_ffn1_w_gate, m_ffn1_w_up, m_ffn1_w_down, m_mix_norm, m_w_in, m_s5_lam_re, m_s5_lam_im, m_s5_log_dt, m_s5_b_re, m_s5_b_im, m_s5_c_re, m_s5_c_im, m_s5_d, m_s5_w_glu, m_s5_b_glu, m_conv_w_dw, m_conv_b_dw, m_conv_ln_g, m_conv_ln_b, m_w_out, m_ffn2_norm, m_ffn2_w_gate, m_ffn2_w_up, m_ffn2_w_down, m_final_norm, v_ffn1_norm, v_ffn1_w_gate, v_ffn1_w_up, v_ffn1_w_down, v_mix_norm, v_w_in, v_s5_lam_re, v_s5_lam_im, v_s5_log_dt, v_s5_b_re, v_s5_b_im, v_s5_c_re, v_s5_c_im, v_s5_d, v_s5_w_glu, v_s5_b_glu, v_conv_w_dw, v_conv_b_dw, v_conv_ln_g, v_conv_ln_b, v_w_out, v_ffn2_norm, v_ffn2_w_gate, v_ffn2_w_up, v_ffn2_w_down, v_final_norm):
    given = dict(x=x, ffn1_norm=ffn1_norm, ffn1_w_gate=ffn1_w_gate, ffn1_w_up=ffn1_w_up, ffn1_w_down=ffn1_w_down, mix_norm=mix_norm, w_in=w_in, s5_lam_re=s5_lam_re, s5_lam_im=s5_lam_im, s5_log_dt=s5_log_dt, s5_b_re=s5_b_re, s5_b_im=s5_b_im, s5_c_re=s5_c_re, s5_c_im=s5_c_im, s5_d=s5_d, s5_w_glu=s5_w_glu, s5_b_glu=s5_b_glu, conv_w_dw=conv_w_dw, conv_b_dw=conv_b_dw, conv_ln_g=conv_ln_g, conv_ln_b=conv_ln_b, w_out=w_out, ffn2_norm=ffn2_norm, ffn2_w_gate=ffn2_w_gate, ffn2_w_up=ffn2_w_up, ffn2_w_down=ffn2_w_down, final_norm=final_norm, loss_target=loss_target, m_ffn1_norm=m_ffn1_norm, m_ffn1_w_gate=m_ffn1_w_gate, m_ffn1_w_up=m_ffn1_w_up, m_ffn1_w_down=m_ffn1_w_down, m_mix_norm=m_mix_norm, m_w_in=m_w_in, m_s5_lam_re=m_s5_lam_re, m_s5_lam_im=m_s5_lam_im, m_s5_log_dt=m_s5_log_dt, m_s5_b_re=m_s5_b_re, m_s5_b_im=m_s5_b_im, m_s5_c_re=m_s5_c_re, m_s5_c_im=m_s5_c_im, m_s5_d=m_s5_d, m_s5_w_glu=m_s5_w_glu, m_s5_b_glu=m_s5_b_glu, m_conv_w_dw=m_conv_w_dw, m_conv_b_dw=m_conv_b_dw, m_conv_ln_g=m_conv_ln_g, m_conv_ln_b=m_conv_ln_b, m_w_out=m_w_out, m_ffn2_norm=m_ffn2_norm, m_ffn2_w_gate=m_ffn2_w_gate, m_ffn2_w_up=m_ffn2_w_up, m_ffn2_w_down=m_ffn2_w_down, m_final_norm=m_final_norm, v_ffn1_norm=v_ffn1_norm, v_ffn1_w_gate=v_ffn1_w_gate, v_ffn1_w_up=v_ffn1_w_up, v_ffn1_w_down=v_ffn1_w_down, v_mix_norm=v_mix_norm, v_w_in=v_w_in, v_s5_lam_re=v_s5_lam_re, v_s5_lam_im=v_s5_lam_im, v_s5_log_dt=v_s5_log_dt, v_s5_b_re=v_s5_b_re, v_s5_b_im=v_s5_b_im, v_s5_c_re=v_s5_c_re, v_s5_c_im=v_s5_c_im, v_s5_d=v_s5_d, v_s5_w_glu=v_s5_w_glu, v_s5_b_glu=v_s5_b_glu, v_conv_w_dw=v_conv_w_dw, v_conv_b_dw=v_conv_b_dw, v_conv_ln_g=v_conv_ln_g, v_conv_ln_b=v_conv_ln_b, v_w_out=v_w_out, v_ffn2_norm=v_ffn2_norm, v_ffn2_w_gate=v_ffn2_w_gate, v_ffn2_w_up=v_ffn2_w_up, v_ffn2_w_down=v_ffn2_w_down, v_final_norm=v_final_norm)
    weights = {n: given[n] for n in TWIN_WEIGHTS}
    shared = {n: given[n] for n in SHARED_INPUTS}
    per_example = {n: given[n] for n in ['x']}
    grad_fn = _jax.value_and_grad(_loss, argnums=(0, 1))

    def one_microbatch(ex, loss_target):
        ex = dict(ex)
        diff = ex.pop(TWIN_DIFF_INPUT)
        return grad_fn(weights, diff, {**shared, **ex}, loss_target)

    if N_MICROBATCH == 1:
        loss, (grad_w, grad_x) = one_microbatch(per_example, given["loss_target"])
    else:
        def body(carry, xs):
            loss_sum, grad_sum = carry
            l_k, (gw_k, gx_k) = one_microbatch(xs[0], xs[1])
            with _jax.named_scope("update"):
                return (loss_sum + l_k, _jax.tree.map(_jnp.add, grad_sum, gw_k)), gx_k

        init = (_jnp.zeros((), _jnp.float32), _jax.tree.map(_jnp.zeros_like, weights))
        (loss, grad_w), grad_x = _jax.lax.scan(body, init, (per_example, given["loss_target"]))
    with _jax.named_scope("update"):
        delta_w, new_m, new_v = {}, {}, {}
        for n in TWIN_WEIGHTS:
            delta_w[n], new_m[n], new_v[n] = _adamw(weights[n], grad_w[n], given["m_" + n], given["v_" + n])
    return (loss, grad_x, *[grad_w[n] for n in TWIN_WEIGHTS], *[delta_w[n] for n in TWIN_WEIGHTS],
            *[new_m[n] for n in TWIN_WEIGHTS], *[new_v[n] for n in TWIN_WEIGHTS])
```

```python
import functools

import jax
import jax.numpy as jnp
from jax import lax
from jax.experimental import pallas as pl
from jax.experimental.pallas import tpu as pltpu

f32 = jnp.float32
bf16 = jnp.bfloat16
S = jax.ShapeDtypeStruct

N_DEV = 8
N_CHIP = 4
D_MODEL = 1024
D_FF = 2816
FF_SHARD = D_FF // N_DEV
FF_SHARD_PAD = 384
FF_PAD = FF_SHARD_PAD * N_DEV
S5_WIDTH = 512
S5_GROUPS = 32
S5_GROUP_CH = 16
S5_STATE = 64
S5_LANES = S5_GROUPS * S5_STATE
CONV_WIDTH = 512
CONV_K = 31
CONV_HALO = 32
CONV_HEAD = 64
IN_COLS = S5_WIDTH + 2 * CONV_WIDTH
SEGMENTS = 8
SCAN_LANES = 512
EPS = 1e-6
LR, B1, B2, ADAM_EPS, WD, STEP = 0.001, 0.9, 0.999, 1e-08, 0.01, 10
VMEM_LIMIT = 56 * 1024 * 1024

NN = (((1,), (0,)), ((), ()))
NT = (((1,), (1,)), ((), ()))
TN = (((0,), (0,)), ((), ()))


def _dot(a, b, dims=NN):
    return lax.dot_general(a, b, dims, preferred_element_type=f32)


def _cp(*sem):
    return pltpu.CompilerParams(dimension_semantics=sem, vmem_limit_bytes=VMEM_LIMIT)


def _rms(x, g):
    return x * lax.rsqrt(jnp.mean(x * x, axis=-1, keepdims=True) + EPS) * g


def _rms_bwd(x, g, dh):
    _, vjp = jax.vjp(_rms, x, g)
    return vjp(dh)


def _sigmoid(x):
    return 1.0 / (1.0 + jnp.exp(-x))


def _gelu(x):
    return 0.5 * x * (1.0 + jnp.tanh(0.7978845608028654 * (x + 0.044715 * x * x * x)))


def _rows8(x):
    t, c = x.shape
    return x.reshape(t // 8, 8, c).sum(axis=0)


def _full(shape):
    return pl.BlockSpec(shape, lambda *_: (0,) * len(shape))


def _ffn_up(x, g, wg, wu, tm, tn, tag):
    L = x.shape[0]

    def body(x_ref, g_ref, wg_ref, wu_ref, h_ref, gate_ref, up_ref, a_ref):
        @pl.when(pl.program_id(1) == 0)
        def _():
            h_ref[...] = _rms(x_ref[...], g_ref[...]).astype(bf16)

        h = h_ref[...]
        gate = _dot(h, wg_ref[...])
        up = _dot(h, wu_ref[...])
        gate_ref[...] = gate.astype(bf16)
        up_ref[...] = up.astype(bf16)
        a_ref[...] = (gate * _sigmoid(gate) * up).astype(bf16)

    row = pl.BlockSpec((tm, D_MODEL), lambda i, j: (i, 0))
    wcol = pl.BlockSpec((D_MODEL, tn), lambda i, j: (0, j))
    tile = pl.BlockSpec((tm, tn), lambda i, j: (i, j))
    return pl.pallas_call(
        body, name=tag + "_up", grid=(L // tm, FF_PAD // tn),
        in_specs=[row, _full((1, D_MODEL)), wcol, wcol],
        out_specs=[row, tile, tile, tile],
        out_shape=[S((L, D_MODEL), bf16)] + [S((L, FF_PAD), bf16)] * 3,
        compiler_params=_cp("parallel", "arbitrary"),
    )(x, g, wg, wu)


def _ffn_down(x, a, wd, tm, tag):
    L = x.shape[0]

    def body(x_ref, a_ref, wd_ref, o_ref):
        o_ref[...] = x_ref[...] + 0.5 * _dot(a_ref[...], wd_ref[...])

    return pl.pallas_call(
        body, name=tag + "_down", grid=(L // tm,),
        in_specs=[pl.BlockSpec((tm, D_MODEL), lambda i: (i, 0)), pl.BlockSpec((tm, FF_PAD), lambda i: (i, 0)),
                  _full((FF_PAD, D_MODEL))],
        out_specs=pl.BlockSpec((tm, D_MODEL), lambda i: (i, 0)),
        out_shape=S((L, D_MODEL), f32),
        compiler_params=_cp("parallel"),
    )(x, a, wd)


def _ffn_bwd_act(dxo, wd, gate, up, tm, tn, tag):
    L = dxo.shape[0]

    def body(dx_ref, wd_ref, gate_ref, up_ref, dgate_ref, dup_ref, dxh_ref):
        @pl.when(pl.program_id(1) == 0)
        def _():
            dxh_ref[...] = (0.5 * dx_ref[...]).astype(bf16)

        da = _dot(dxh_ref[...], wd_ref[...], NT)
        gate = gate_ref[...].astype(f32)
        up = up_ref[...].astype(f32)
        sig = _sigmoid(gate)
        dgate_ref[...] = (da * up * sig * (1.0 + gate * (1.0 - sig))).astype(bf16)
        dup_ref[...] = (da * gate * sig).astype(bf16)

    row = pl.BlockSpec((tm, D_MODEL), lambda i, j: (i, 0))
    tile = pl.BlockSpec((tm, tn), lambda i, j: (i, j))
    return pl.pallas_call(
        body, name=tag + "_bwd_act", grid=(L // tm, FF_PAD // tn),
        in_specs=[row, pl.BlockSpec((tn, D_MODEL), lambda i, j: (j, 0)), tile, tile],
        out_specs=[tile, tile, row],
        out_shape=[S((L, FF_PAD), bf16), S((L, FF_PAD), bf16), S((L, D_MODEL), bf16)],
        compiler_params=_cp("parallel", "arbitrary"),
    )(dxo, wd, gate, up)


def _ffn_bwd_in(dxo, x, g, dgate, dup, wg, wu, tm, tk, tag):
    L = x.shape[0]
    nk = FF_PAD // tk

    def body(dxo_ref, x_ref, g_ref, dgate_ref, dup_ref, wg_ref, wu_ref, dx_ref, dg_ref, acc_ref):
        i, k = pl.program_id(0), pl.program_id(1)

        @pl.when((i == 0) & (k == 0))
        def _():
            dg_ref[...] = jnp.zeros_like(dg_ref)

        part = _dot(dgate_ref[...], wg_ref[...], NT) + _dot(dup_ref[...], wu_ref[...], NT)

        @pl.when(k == 0)
        def _():
            acc_ref[...] = part

        @pl.when(k > 0)
        def _():
            acc_ref[...] += part

        @pl.when(k == nk - 1)
        def _():
            dx, dg = _rms_bwd(x_ref[...], g_ref[...], acc_ref[...])
            dx_ref[...] = dxo_ref[...] + dx
            dg_ref[...] += dg

    row = pl.BlockSpec((tm, D_MODEL), lambda i, k: (i, 0))
    act = pl.BlockSpec((tm, tk), lambda i, k: (i, k))
    wcol = pl.BlockSpec((D_MODEL, tk), lambda i, k: (0, k))
    return pl.pallas_call(
        body, name=tag + "_bwd_in", grid=(L // tm, nk),
        in_specs=[row, row, _full((1, D_MODEL)), act, act, wcol, wcol],
        out_specs=[row, _full((1, D_MODEL))],
        out_shape=[S((L, D_MODEL), f32), S((1, D_MODEL), f32)],
        scratch_shapes=[pltpu.VMEM((tm, D_MODEL), f32)],
        compiler_params=_cp("arbitrary", "arbitrary"),
    )(dxo, x, g, dgate, dup, wg, wu)


def _mm_tn(a, b, out_dtype, name, tm=512, tn=1024, tk=512):
    L, M = a.shape
    N = b.shape[1]
    tm, tn, tk = min(tm, M), min(tn, N), min(tk, L)
    while N % tn:
        tn //= 2
    nk = L // tk

    def body(a_ref, b_ref, o_ref, acc_ref):
        k = pl.program_id(2)
        part = _dot(a_ref[...].astype(bf16), b_ref[...].astype(bf16), TN)

        @pl.when(k == 0)
        def _():
            acc_ref[...] = part

        @pl.when(k > 0)
        def _():
            acc_ref[...] += part

        @pl.when(k == nk - 1)
        def _():
            o_ref[...] = acc_ref[...].astype(out_dtype)

    return pl.pallas_call(
        body, name=name, grid=(M // tm, N // tn, nk),
        in_specs=[pl.BlockSpec((tk, tm), lambda i, j, k: (k, i)), pl.BlockSpec((tk, tn), lambda i, j, k: (k, j))],
        out_specs=pl.BlockSpec((tm, tn), lambda i, j, k: (i, j)),
        out_shape=S((M, N), out_dtype),
        scratch_shapes=[pltpu.VMEM((tm, tn), f32)],
        compiler_params=_cp("parallel", "parallel", "arbitrary"),
    )(a, b)


def _mix_in(x, g, w_in, tm):
    L = x.shape[0]

    def body(x_ref, g_ref, w_ref, h_ref, us_ref, v_ref):
        h = _rms(x_ref[...], g_ref[...]).astype(bf16)
        h_ref[...] = h
        u = _dot(h, w_ref[...])
        us_ref[...] = u[:, :S5_WIDTH]
        v_ref[...] = u[:, S5_WIDTH:]

    row = lambda c: pl.BlockSpec((tm, c), lambda i: (i, 0))
    return pl.pallas_call(
        body, name="mix_in", grid=(L // tm,),
        in_specs=[row(D_MODEL), _full((1, D_MODEL)), _full((D_MODEL, IN_COLS))],
        out_specs=[row(D_MODEL), row(S5_WIDTH), row(2 * CONV_WIDTH)],
        out_shape=[S((L, D_MODEL), bf16), S((L, S5_WIDTH), f32), S((L, 2 * CONV_WIDTH), f32)],
        compiler_params=_cp("parallel"),
    )(x, g, w_in)


def _mix_in_bwd(dxo, x, g, du_s5, dv, w_in, tm):
    L = x.shape[0]

    def body(dxo_ref, x_ref, g_ref, dus_ref, dv_ref, w_ref, dx_ref, dg_ref, dub_ref):
        @pl.when(pl.program_id(0) == 0)
        def _():
            dg_ref[...] = jnp.zeros_like(dg_ref)

        dus = dus_ref[...].astype(bf16)
        dvb = dv_ref[...].astype(bf16)
        dub_ref[:, :S5_WIDTH] = dus
        dub_ref[:, S5_WIDTH:] = dvb
        dh = _dot(dus, w_ref[:, :S5_WIDTH], NT) + _dot(dvb, w_ref[:, S5_WIDTH:], NT)
        dx, dg = _rms_bwd(x_ref[...], g_ref[...], dh)
        dx_ref[...] = dxo_ref[...] + dx
        dg_ref[...] += dg

    row = lambda c: pl.BlockSpec((tm, c), lambda i: (i, 0))
    return pl.pallas_call(
        body, name="mix_in_bwd", grid=(L // tm,),
        in_specs=[row(D_MODEL), row(D_MODEL), _full((1, D_MODEL)), row(S5_WIDTH), row(2 * CONV_WIDTH),
                  _full((D_MODEL, IN_COLS))],
        out_specs=[row(D_MODEL), _full((1, D_MODEL)), row(IN_COLS)],
        out_shape=[S((L, D_MODEL), f32), S((1, D_MODEL), f32), S((L, IN_COLS), bf16)],
        compiler_params=_cp("arbitrary"),
    )(dxo, x, g, du_s5, dv, w_in)


def _mix_out(x, y_s5, y_conv, w_out, tm):
    L = x.shape[0]

    def body(x_ref, ys_ref, yc_ref, w_ref, o_ref):
        o_ref[...] = x_ref[...] + _dot(ys_ref[...], w_ref[:S5_WIDTH, :]) + _dot(yc_ref[...], w_ref[S5_WIDTH:, :])

    row = lambda c: pl.BlockSpec((tm, c), lambda i: (i, 0))
    return pl.pallas_call(
        body, name="mix_out", grid=(L // tm,),
        in_specs=[row(D_MODEL), row(S5_WIDTH), row(CONV_WIDTH), _full((D_MODEL, D_MODEL))],
        out_specs=row(D_MODEL), out_shape=S((L, D_MODEL), f32),
        compiler_params=_cp("parallel"),
    )(x, y_s5, y_conv, w_out)


def _mix_out_bwd(dx, w_out, tm):
    L = dx.shape[0]

    def body(dx_ref, w_ref, dys_ref, dyc_ref, dxb_ref):
        dxb = dx_ref[...].astype(bf16)
        dxb_ref[...] = dxb
        dys_ref[...] = _dot(dxb, w_ref[:S5_WIDTH, :], NT)
        dyc_ref[...] = _dot(dxb, w_ref[S5_WIDTH:, :], NT)

    row = lambda c: pl.BlockSpec((tm, c), lambda i: (i, 0))
    return pl.pallas_call(
        body, name="mix_out_bwd", grid=(L // tm,),
        in_specs=[row(D_MODEL), _full((D_MODEL, D_MODEL))],
        out_specs=[row(S5_WIDTH), row(CONV_WIDTH), row(D_MODEL)],
        out_shape=[S((L, S5_WIDTH), f32), S((L, CONV_WIDTH), f32), S((L, D_MODEL), bf16)],
        compiler_params=_cp("parallel"),
    )(dx, w_out)


def _s5_discretise(lam_re, lam_im, log_dt, b_re, b_im):
    dt = jnp.exp(log_dt)
    mag = jnp.exp(lam_re * dt)
    abar_re = mag * jnp.cos(lam_im * dt)
    abar_im = mag * jnp.sin(lam_im * dt)
    den = lam_re * lam_re + lam_im * lam_im
    num_re = abar_re - 1.0
    f_re = ((num_re * lam_re + abar_im * lam_im) / den)[:, None, :]
    f_im = ((abar_im * lam_re - num_re * lam_im) / den)[:, None, :]
    return abar_re, abar_im, f_re * b_re - f_im * b_im, f_re * b_im + f_im * b_re


def _s5_params(lam_re, lam_im, log_dt, b_re, b_im):
    def body(lr, li, ld, br, bi, ar_ref, ai_ref, bbr_ref, bbi_ref):
        ar, ai, bbr, bbi = _s5_discretise(lr[...], li[...], ld[...], br[...], bi[...])
        ar_ref[...], ai_ref[...], bbr_ref[...], bbi_ref[...] = ar, ai, bbr, bbi

    gp = S((S5_GROUPS, S5_STATE), f32)
    gcp = S((S5_GROUPS, S5_GROUP_CH, S5_STATE), f32)
    return pl.pallas_call(body, name="s5_params", out_shape=[gp, gp, gcp, gcp])(lam_re, lam_im, log_dt, b_re, b_im)


def _s5_params_bwd(lam_re, lam_im, log_dt, b_re, b_im, d_ar, d_ai, d_bbr, d_bbi):
    def body(lr, li, ld, br, bi, car, cai, cbr, cbi, o_lr, o_li, o_ld, o_br, o_bi):
        _, vjp = jax.vjp(_s5_discretise, lr[...], li[...], ld[...], br[...], bi[...])
        o_lr[...], o_li[...], o_ld[...], o_br[...], o_bi[...] = vjp((car[...], cai[...], cbr[...], cbi[...]))

    gp = S((S5_GROUPS, S5_STATE), f32)
    gcp = S((S5_GROUPS, S5_GROUP_CH, S5_STATE), f32)
    return pl.pallas_call(body, name="s5_params_bwd", out_shape=[gp, gp, S((S5_GROUPS, 1), f32), gcp, gcp])(
        lam_re, lam_im, log_dt, b_re, b_im, d_ar, d_ai, d_bbr, d_bbi)


def _cmul(ar, ai, br, bi):
    return ar * br - ai * bi, ar * bi + ai * br


def _scan_specs(ni, bi, reverse):
    nb = ni // bi
    blk = pl.BlockSpec((bi, SEGMENTS, SCAN_LANES), (lambda c, j: (nb - 1 - j, 0, c)) if reverse else (lambda c, j: (j, 0, c)))
    vec = pl.BlockSpec((1, SCAN_LANES), lambda c, j: (0, c))
    tile = pl.BlockSpec((SEGMENTS, SCAN_LANES), lambda c, j: (0, c))
    return nb, blk, vec, tile


def _scan_ends(bu_re, bu_im, a_re, a_im, reverse, bi):
    ni = bu_re.shape[0]
    nb, blk, vec, tile = _scan_specs(ni, bi, reverse)

    def body(br_ref, bi_ref, ar_ref, ai_ref, er_ref, ei_ref):
        @pl.when(pl.program_id(1) == 0)
        def _():
            er_ref[...] = jnp.zeros_like(er_ref)
            ei_ref[...] = jnp.zeros_like(ei_ref)

        ar = jnp.broadcast_to(ar_ref[...], (SEGMENTS, SCAN_LANES))
        ai = jnp.broadcast_to(ai_ref[...], (SEGMENTS, SCAN_LANES))

        def step(n, c):
            i = (bi - 1 - n) if reverse else n
            pr, pi = _cmul(ar, ai, c[0], c[1])
            return pr + br_ref[i], pi + bi_ref[i]

        er_ref[...], ei_ref[...] = lax.fori_loop(0, bi, step, (er_ref[...], ei_ref[...]), unroll=4)

    out = S((SEGMENTS, S5_LANES), f32)
    return pl.pallas_call(
        body, name="s5_rscan_ends" if reverse else "s5_scan_ends", grid=(S5_LANES // SCAN_LANES, nb),
        in_specs=[blk, blk, vec, vec], out_specs=[tile, tile], out_shape=[out, out],
        compiler_params=_cp("parallel", "arbitrary"),
    )(bu_re, bu_im, a_re, a_im)


def _segment_starts(er, ei, ar, ai, steps, reverse):
    pr, pi = ar, ai
    n = 1
    while n < steps:
        pr, pi = _cmul(pr, pi, pr, pi)
        n *= 2
    assert n == steps
    row = lax.broadcasted_iota(jnp.int32, (SEGMENTS, SCAN_LANES), 0)
    hr = jnp.zeros((1, SCAN_LANES), f32)
    hi = jnp.zeros((1, SCAN_LANES), f32)
    out_r = jnp.zeros((SEGMENTS, SCAN_LANES), f32)
    out_i = jnp.zeros((SEGMENTS, SCAN_LANES), f32)
    order = range(SEGMENTS - 1, 0, -1) if reverse else range(0, SEGMENTS - 1)
    for r in order:
        qr, qi = _cmul(pr, pi, hr, hi)
        hr, hi = qr + er[r:r + 1, :], qi + ei[r:r + 1, :]
        nxt = r - 1 if reverse else r + 1
        out_r = jnp.where(row == nxt, hr, out_r)
        out_i = jnp.where(row == nxt, hi, out_i)
    return out_r, out_i


def _scan_states(bu_re, bu_im, a_re, a_im, e_re, e_im, bi):
    ni = bu_re.shape[0]
    nb, blk, vec, tile = _scan_specs(ni, bi, False)

    def body(br_ref, bi_ref, ar_ref, ai_ref, er_ref, ei_ref, sr_ref, si_ref, cr_ref, ci_ref):
        @pl.when(pl.program_id(1) == 0)
        def _():
            cr_ref[...], ci_ref[...] = _segment_starts(er_ref[...], ei_ref[...], ar_ref[...], ai_ref[...], ni, False)

        ar = jnp.broadcast_to(ar_ref[...], (SEGMENTS, SCAN_LANES))
        ai = jnp.broadcast_to(ai_ref[...], (SEGMENTS, SCAN_LANES))

        def step(i, c):
            pr, pi = _cmul(ar, ai, c[0], c[1])
            nr, nim = pr + br_ref[i], pi + bi_ref[i]
            sr_ref[i] = nr
            si_ref[i] = nim
            return nr, nim

        cr_ref[...], ci_ref[...] = lax.fori_loop(0, bi, step, (cr_ref[...], ci_ref[...]), unroll=4)

    return pl.pallas_call(
        body, name="s5_scan_states", grid=(S5_LANES // SCAN_LANES, nb),
        in_specs=[blk, blk, vec, vec, tile, tile],
        out_specs=[blk, blk], out_shape=[S(bu_re.shape, f32)] * 2,
        scratch_shapes=[pltpu.VMEM((SEGMENTS, SCAN_LANES), f32)] * 2,
        compiler_params=_cp("parallel", "arbitrary"),
    )(bu_re, bu_im, a_re, a_im, e_re, e_im)


def _rscan_states(ds_re, ds_im, a_re, a_im, e_re, e_im, s_re, s_im, bi):
    ni = ds_re.shape[0]
    nb, blk, vec, tile = _scan_specs(ni, bi, True)

    def body(dr_ref, di_ref, ar_ref, ai_ref, er_ref, ei_ref, sr_ref, si_ref, pr_ref, pi_ref, lr_ref, li_ref,
             gr_ref, gi_ref, dar_ref, dai_ref, cr_ref, ci_ref):
        j = pl.program_id(1)

        @pl.when(j == 0)
        def _():
            cr_ref[...], ci_ref[...] = _segment_starts(er_ref[...], ei_ref[...], ar_ref[...], ai_ref[...], ni, True)
            dar_ref[...] = jnp.zeros_like(dar_ref)
            dai_ref[...] = jnp.zeros_like(dai_ref)

        ar = jnp.broadcast_to(ar_ref[...], (SEGMENTS, SCAN_LANES))
        ai = jnp.broadcast_to(ai_ref[...], (SEGMENTS, SCAN_LANES))

        def step(n, c):
            i = bi - 1 - n
            gr, gi, accr, acci = c
            qr, qi = _cmul(ar, ai, gr, gi)
            gr, gi = qr + dr_ref[i], qi + di_ref[i]
            gr_ref[i] = gr
            gi_ref[i] = gi
            sr, si = sr_ref[i - 1], si_ref[i - 1]
            return gr, gi, accr + (gr * sr + gi * si), acci + (gi * sr - gr * si)

        gr, gi, accr, acci = lax.fori_loop(0, bi - 1, step, (cr_ref[...], ci_ref[...], dar_ref[...], dai_ref[...]), unroll=3)
        qr, qi = _cmul(ar, ai, gr, gi)
        gr, gi = qr + dr_ref[0], qi + di_ref[0]
        gr_ref[0] = gr
        gi_ref[0] = gi
        cr_ref[...], ci_ref[...] = gr, gi
        row = lax.broadcasted_iota(jnp.int32, (SEGMENTS, SCAN_LANES), 0)
        first = j == nb - 1
        wrap_r = jnp.where(row == 0, 0.0, pltpu.roll(lr_ref[0], 1, 0))
        wrap_i = jnp.where(row == 0, 0.0, pltpu.roll(li_ref[0], 1, 0))
        sr = jnp.where(first, wrap_r, pr_ref[0])
        si = jnp.where(first, wrap_i, pi_ref[0])
        dar_ref[...] = accr + gr * sr + gi * si
        dai_ref[...] = acci + gi * sr - gr * si

    prev = pl.BlockSpec((1, SEGMENTS, SCAN_LANES), lambda c, j: (jnp.maximum((nb - 1 - j) * bi - 1, 0), 0, c))
    last = pl.BlockSpec((1, SEGMENTS, SCAN_LANES), lambda c, j: (ni - 1, 0, c))
    return pl.pallas_call(
        body, name="s5_rscan_states", grid=(S5_LANES // SCAN_LANES, nb),
        in_specs=[blk, blk, vec, vec, tile, tile, blk, blk, prev, prev, last, last],
        out_specs=[blk, blk, tile, tile],
        out_shape=[S(ds_re.shape, f32)] * 2 + [S((SEGMENTS, S5_LANES), f32)] * 2,
        scratch_shapes=[pltpu.VMEM((SEGMENTS, SCAN_LANES), f32)] * 2,
        compiler_params=_cp("parallel", "arbitrary"),
    )(ds_re, ds_im, a_re, a_im, e_re, e_im, s_re, s_im, s_re, s_im, s_re, s_im)


def _s5_drive(u, bb_re, bb_im, tm):
    L = u.shape[0]

    def body(u_ref, br_ref, bi_ref, or_ref, oi_ref):
        ub = u_ref[...].astype(bf16)
        or_ref[...] = _dot(ub, br_ref[...])
        oi_ref[...] = _dot(ub, bi_ref[...])

    big = pl.BlockSpec((tm, S5_LANES), lambda i: (i, 0))
    return pl.pallas_call(
        body, name="s5_drive", grid=(L // tm,),
        in_specs=[pl.BlockSpec((tm, S5_WIDTH), lambda i: (i, 0)), _full((S5_WIDTH, S5_LANES)), _full((S5_WIDTH, S5_LANES))],
        out_specs=[big, big], out_shape=[S((L, S5_LANES), f32)] * 2,
        compiler_params=_cp("parallel"),
    )(u, bb_re, bb_im)


def _s5_read(s_re, s_im, u, c_re, c_im, d_skip, w_glu, b_glu, tm):
    L = u.shape[0]

    def body(sr_ref, si_ref, u_ref, cr_ref, ci_ref, d_ref, w_ref, b_ref, yl_ref, o_ref):
        yl = _dot(sr_ref[...].astype(bf16), cr_ref[...], NT) - _dot(si_ref[...].astype(bf16), ci_ref[...], NT)
        yl_ref[...] = yl
        y = _gelu(yl + d_ref[...] * u_ref[...])
        z = _dot(y.astype(bf16), w_ref[...]) + b_ref[...]
        o_ref[...] = (y * _sigmoid(z)).astype(bf16)

    big = pl.BlockSpec((tm, S5_LANES), lambda i: (i, 0))
    row = pl.BlockSpec((tm, S5_WIDTH), lambda i: (i, 0))
    vec = _full((1, S5_WIDTH))
    return pl.pallas_call(
        body, name="s5_read", grid=(L // tm,),
        in_specs=[big, big, row, _full((S5_WIDTH, S5_LANES)), _full((S5_WIDTH, S5_LANES)), vec,
                  _full((S5_WIDTH, S5_WIDTH)), vec],
        out_specs=[row, row], out_shape=[S((L, S5_WIDTH), f32), S((L, S5_WIDTH), bf16)],
        compiler_params=_cp("parallel"),
    )(s_re, s_im, u, c_re, c_im, d_skip, w_glu, b_glu)


def _s5_read_bwd(dout, y_lin, u, d_skip, w_glu, b_glu, tm):
    L = u.shape[0]

    def body(do_ref, yl_ref, u_ref, d_ref, w_ref, b_ref, dyl_ref, du_ref, dd_ref, dw_ref, db_ref):
        @pl.when(pl.program_id(0) == 0)
        def _():
            dd_ref[...] = jnp.zeros_like(dd_ref)
            dw_ref[...] = jnp.zeros_like(dw_ref)
            db_ref[...] = jnp.zeros_like(db_ref)

        u, d, dout = u_ref[...], d_ref[...], do_ref[...]
        y, gelu_vjp = jax.vjp(_gelu, yl_ref[...] + d * u)
        yb = y.astype(bf16)
        sig = _sigmoid(_dot(yb, w_ref[...]) + b_ref[...])
        dz = dout * y * sig * (1.0 - sig)
        dzb = dz.astype(bf16)
        dy = dout * sig + _dot(dzb, w_ref[...], NT)
        (dyp,) = gelu_vjp(dy)
        dyl_ref[...] = dyp.astype(bf16)
        du_ref[...] = d * dyp
        dd_ref[...] += _rows8(dyp * u)
        db_ref[...] += _rows8(dz)
        dw_ref[...] += _dot(yb, dzb, TN)

    row = pl.BlockSpec((tm, S5_WIDTH), lambda i: (i, 0))
    vec = _full((1, S5_WIDTH))
    part = _full((8, S5_WIDTH))
    return pl.pallas_call(
        body, name="s5_read_bwd", grid=(L // tm,),
        in_specs=[row, row, row, vec, _full((S5_WIDTH, S5_WIDTH)), vec],
        out_specs=[row, row, part, _full((S5_WIDTH, S5_WIDTH)), part],
        out_shape=[S((L, S5_WIDTH), bf16), S((L, S5_WIDTH), f32), S((8, S5_WIDTH), f32),
                   S((S5_WIDTH, S5_WIDTH), f32), S((8, S5_WIDTH), f32)],
        compiler_params=_cp("arbitrary"),
    )(dout, y_lin, u, d_skip, w_glu, b_glu)


def _s5_state_grad(dy_lin, s_re, s_im, c_re, c_im, tm):
    L = dy_lin.shape[0]

    def body(dy_ref, sr_ref, si_ref, cr_ref, ci_ref, dsr_ref, dsi_ref, dcr_ref, dci_ref):
        @pl.when(pl.program_id(0) == 0)
        def _():
            dcr_ref[...] = jnp.zeros_like(dcr_ref)
            dci_ref[...] = jnp.zeros_like(dci_ref)

        dy = dy_ref[...]
        dsr_ref[...] = _dot(dy, cr_ref[...])
        dsi_ref[...] = -_dot(dy, ci_ref[...])
        dcr_ref[...] += _dot(dy, sr_ref[...].astype(bf16), TN)
        dci_ref[...] -= _dot(dy, si_ref[...].astype(bf16), TN)

    big = pl.BlockSpec((tm, S5_LANES), lambda i: (i, 0))
    mat = _full((S5_WIDTH, S5_LANES))
    return pl.pallas_call(
        body, name="s5_state_grad", grid=(L // tm,),
        in_specs=[pl.BlockSpec((tm, S5_WIDTH), lambda i: (i, 0)), big, big, mat, mat],
        out_specs=[big, big, mat, mat],
        out_shape=[S((L, S5_LANES), f32)] * 2 + [S((S5_WIDTH, S5_LANES), f32)] * 2,
        compiler_params=_cp("arbitrary"),
    )(dy_lin, s_re, s_im, c_re, c_im)


def _s5_drive_bwd(g_re, g_im, u, du_skip, bb_re, bb_im, tm):
    L = u.shape[0]

    def body(gr_ref, gi_ref, u_ref, dus_ref, br_ref, bi_ref, du_ref, dbr_ref, dbi_ref):
        @pl.when(pl.program_id(0) == 0)
        def _():
            dbr_ref[...] = jnp.zeros_like(dbr_ref)
            dbi_ref[...] = jnp.zeros_like(dbi_ref)

        gr = gr_ref[...].astype(bf16)
        gi = gi_ref[...].astype(bf16)
        ub = u_ref[...].astype(bf16)
        du_ref[...] = dus_ref[...] + _dot(gr, br_ref[...], NT) + _dot(gi, bi_ref[...], NT)
        dbr_ref[...] += _dot(ub, gr, TN)
        dbi_ref[...] += _dot(ub, gi, TN)

    big = pl.BlockSpec((tm, S5_LANES), lambda i: (i, 0))
    row = pl.BlockSpec((tm, S5_WIDTH), lambda i: (i, 0))
    mat = _full((S5_WIDTH, S5_LANES))
    return pl.pallas_call(
        body, name="s5_drive_bwd", grid=(L // tm,),
        in_specs=[big, big, row, row, mat, mat],
        out_specs=[row, mat, mat],
        out_shape=[S((L, S5_WIDTH), f32)] + [S((S5_WIDTH, S5_LANES), f32)] * 2,
        compiler_params=_cp("arbitrary"),
    )(g_re, g_im, u, du_skip, bb_re, bb_im)


def _group_mean(x, avg):
    hi = x.astype(bf16)
    lo = (x - hi.astype(f32)).astype(bf16)
    return _dot(hi, avg) + _dot(lo, avg)


def _conv_act(zn, ln_g, ln_b):
    t = zn * ln_g + ln_b
    return t * _sigmoid(t)


def _glu_and_conv(v_ref, halo_ref, w_ref, b_ref, zpad_ref, tm):
    i = pl.program_id(0)
    v = v_ref[...]
    z = v[:, :CONV_WIDTH] * _sigmoid(v[:, CONV_WIDTH:])
    vh = halo_ref[...]
    zh = vh[:, :CONV_WIDTH] * _sigmoid(vh[:, CONV_WIDTH:])
    zpad_ref[:CONV_HALO, :] = jnp.where(i > 0, zh, 0.0)
    zpad_ref[CONV_HALO:, :] = z
    acc = jnp.zeros((tm, CONV_WIDTH), f32) + b_ref[...]
    for k in range(CONV_K):
        acc = acc + w_ref[k:k + 1, :] * zpad_ref[pl.ds(CONV_HALO - (CONV_K - 1) + k, tm), :]
    return z, acc


def _conv_specs(tm):
    per = tm // CONV_HALO
    vrow = pl.BlockSpec((tm, 2 * CONV_WIDTH), lambda i: (i, 0))
    vhalo = pl.BlockSpec((CONV_HALO, 2 * CONV_WIDTH), lambda i: (jnp.maximum(i * per - 1, 0), 0))
    return vrow, vhalo


def _conv_fwd(v, w_dw, b_dw, ln_g, ln_b, avg, tm):
    L = v.shape[0]

    def body(v_ref, halo_ref, w_ref, b_ref, g_ref, bb_ref, avg_ref, o_ref, zpad_ref):
        _, zc = _glu_and_conv(v_ref, halo_ref, w_ref, b_ref, zpad_ref, tm)
        xc = zc - _group_mean(zc, avg_ref[...])
        zn = xc * lax.rsqrt(_group_mean(xc * xc, avg_ref[...]) + EPS)
        o_ref[...] = _conv_act(zn, g_ref[...], bb_ref[...]).astype(bf16)

    vrow, vhalo = _conv_specs(tm)
    vec = _full((1, CONV_WIDTH))
    return pl.pallas_call(
        body, name="conv_fwd", grid=(L // tm,),
        in_specs=[vrow, vhalo, _full((CONV_HALO, CONV_WIDTH)), vec, vec, vec, _full((CONV_WIDTH, CONV_WIDTH))],
        out_specs=pl.BlockSpec((tm, CONV_WIDTH), lambda i: (i, 0)), out_shape=S((L, CONV_WIDTH), bf16),
        scratch_shapes=[pltpu.VMEM((tm + CONV_HALO, CONV_WIDTH), f32)],
        compiler_params=_cp("arbitrary"),
    )(v, v, w_dw, b_dw, ln_g, ln_b, avg)


def _conv_bwd_norm(dout, v, w_dw, b_dw, ln_g, ln_b, avg, tm):
    L = v.shape[0]

    def body(do_ref, v_ref, halo_ref, w_ref, b_ref, g_ref, bb_ref, avg_ref, z_ref, dzc_ref, dg_ref, db_ref, dbd_ref,
             zpad_ref):
        @pl.when(pl.program_id(0) == 0)
        def _():
            dg_ref[...] = jnp.zeros_like(dg_ref)
            db_ref[...] = jnp.zeros_like(db_ref)
            dbd_ref[...] = jnp.zeros_like(dbd_ref)

        avg = avg_ref[...]
        z, zc = _glu_and_conv(v_ref, halo_ref, w_ref, b_ref, zpad_ref, tm)
        z_ref[...] = z
        xc = zc - _group_mean(zc, avg)
        rstd = lax.rsqrt(_group_mean(xc * xc, avg) + EPS)
        xhat = xc * rstd
        _, act_vjp = jax.vjp(_conv_act, xhat, g_ref[...], bb_ref[...])
        dxhat, dg, db = act_vjp(do_ref[...])
        dzc = rstd * (dxhat - _group_mean(dxhat, avg) - xhat * _group_mean(dxhat * xhat, avg))
        dzc_ref[...] = dzc
        dg_ref[0:1, :] += dg
        db_ref[0:1, :] += db
        dbd_ref[...] += _rows8(dzc)

    vrow, vhalo = _conv_specs(tm)
    vec = _full((1, CONV_WIDTH))
    row = pl.BlockSpec((tm, CONV_WIDTH), lambda i: (i, 0))
    part = _full((8, CONV_WIDTH))
    return pl.pallas_call(
        body, name="conv_bwd_norm", grid=(L // tm,),
        in_specs=[row, vrow, vhalo, _full((CONV_HALO, CONV_WIDTH)), vec, vec, vec, _full((CONV_WIDTH, CONV_WIDTH))],
        out_specs=[row, row, part, part, part],
        out_shape=[S((L, CONV_WIDTH), f32)] * 2 + [S((8, CONV_WIDTH), f32)] * 3,
        scratch_shapes=[pltpu.VMEM((tm + CONV_HALO, CONV_WIDTH), f32)],
        compiler_params=_cp("arbitrary"),
    )(dout, v, v, w_dw, b_dw, ln_g, ln_b, avg)


def _conv_bwd_taps(dzc, z, v, w_dw, tm):
    L = v.shape[0]
    nt = L // tm
    per = tm // CONV_HALO

    def body(d_ref, dn_ref, z_ref, zh_ref, v_ref, w_ref, dv_ref, dw_ref, dpad_ref, zpad_ref):
        i = pl.program_id(0)

        @pl.when(i == 0)
        def _():
            dw_ref[...] = jnp.zeros_like(dw_ref)

        d = d_ref[...]
        dpad_ref[:tm, :] = d
        dpad_ref[tm:, :] = jnp.where(i < nt - 1, dn_ref[...], 0.0)
        zpad_ref[:CONV_HALO, :] = jnp.where(i > 0, zh_ref[...], 0.0)
        zpad_ref[CONV_HALO:, :] = z_ref[...]
        dz = jnp.zeros((tm, CONV_WIDTH), f32)
        for k in range(CONV_K):
            dz = dz + w_ref[k:k + 1, :] * dpad_ref[pl.ds(CONV_K - 1 - k, tm), :]
            dw_ref[k] += _rows8(d * zpad_ref[pl.ds(CONV_HALO - (CONV_K - 1) + k, tm), :])
        v = v_ref[...]
        v1 = v[:, :CONV_WIDTH]
        sig = _sigmoid(v[:, CONV_WIDTH:])
        dv_ref[:, :CONV_WIDTH] = dz * sig
        dv_ref[:, CONV_WIDTH:] = dz * v1 * sig * (1.0 - sig)

    row = pl.BlockSpec((tm, CONV_WIDTH), lambda i: (i, 0))
    nxt = pl.BlockSpec((CONV_HALO, CONV_WIDTH), lambda i: (jnp.minimum((i + 1) * per, nt * per - 1), 0))
    prv = pl.BlockSpec((CONV_HALO, CONV_WIDTH), lambda i: (jnp.maximum(i * per - 1, 0), 0))
    return pl.pallas_call(
        body, name="conv_bwd_taps", grid=(nt,),
        in_specs=[row, nxt, row, prv, pl.BlockSpec((tm, 2 * CONV_WIDTH), lambda i: (i, 0)), _full((CONV_HALO, CONV_WIDTH))],
        out_specs=[pl.BlockSpec((tm, 2 * CONV_WIDTH), lambda i: (i, 0)), _full((CONV_HALO, 8, CONV_WIDTH))],
        out_shape=[S((L, 2 * CONV_WIDTH), f32), S((CONV_HALO, 8, CONV_WIDTH), f32)],
        scratch_shapes=[pltpu.VMEM((tm + CONV_HALO, CONV_WIDTH), f32)] * 2,
        compiler_params=_cp("arbitrary"),
    )(dzc, dzc, z, z, v, w_dw)


def _loss_head(x, target, g, tm):
    L = x.shape[0]

    def body(x_ref, t_ref, g_ref, dx_ref, dg_ref, l_ref):
        @pl.when(pl.program_id(0) == 0)
        def _():
            dg_ref[...] = jnp.zeros_like(dg_ref)
            l_ref[...] = jnp.zeros_like(l_ref)

        x, g = x_ref[...], g_ref[...]
        e = _rms(x, g) - t_ref[...]
        l_ref[...] += _rows8(e * e) * (0.5 / D_MODEL)
        dx, dg = _rms_bwd(x, g, e * (1.0 / D_MODEL))
        dx_ref[...] = dx
        dg_ref[...] += dg

    row = pl.BlockSpec((tm, D_MODEL), lambda i: (i, 0))
    return pl.pallas_call(
        body, name="loss_head", grid=(L // tm,),
        in_specs=[row, row, _full((1, D_MODEL))],
        out_specs=[row, _full((1, D_MODEL)), _full((8, D_MODEL))],
        out_shape=[S((L, D_MODEL), f32), S((1, D_MODEL), f32), S((8, D_MODEL), f32)],
        compiler_params=_cp("arbitrary"),
    )(x, target, g)


def _to_segments(a):
    L, c = a.shape
    return a.reshape(SEGMENTS, L // SEGMENTS, c).transpose(1, 0, 2).reshape(L, c)


def _from_segments(a):
    L, c = a.shape
    return a.reshape(L // SEGMENTS, SEGMENTS, c).transpose(1, 0, 2).reshape(L, c)


def _block_diag(m):
    eye = jnp.eye(S5_GROUPS, dtype=m.dtype)
    return (m[:, :, None, :] * eye[:, None, :, None]).reshape(S5_GROUPS * S5_GROUP_CH, S5_LANES)


def _diag_blocks(m):
    m4 = m.reshape(S5_GROUPS, S5_GROUP_CH, S5_GROUPS, S5_STATE)
    return jnp.einsum("gchp,gh->gcp", m4, jnp.eye(S5_GROUPS, dtype=m.dtype))


def _ffn_block(x, p, tag, tm):
    h, gate, up, a = _ffn_up(x, p[tag + "_norm"], p[tag + "_w_gate"], p[tag + "_w_up"], tm, 768, tag)
    return _ffn_down(x, a, p[tag + "_w_down"], tm, tag), (h, gate, up, a)


def _ffn_block_bwd(dxo, x, p, tag, saved, tm, grads):
    h, gate, up, a = saved
    dgate, dup, dxh = _ffn_bwd_act(dxo, p[tag + "_w_down"], gate, up, tm, 768, tag)
    dx, grads[tag + "_norm"] = _ffn_bwd_in(dxo, x, p[tag + "_norm"], dgate, dup, p[tag + "_w_gate"], p[tag + "_w_up"], tm, 1536, tag)
    grads[tag + "_w_gate"] = _mm_tn(h, dgate, bf16, tag + "_dw_gate")
    grads[tag + "_w_up"] = _mm_tn(h, dup, bf16, tag + "_dw_up")
    grads[tag + "_w_down"] = _mm_tn(a, dxh, bf16, tag + "_dw_down")
    return dx


def _local_step(x, target, p):
    L = x.shape[0]
    tm = min(512, L)
    ts = min(256, L)
    ni = L // SEGMENTS
    bi = min(64, ni)
    grads = {}

    x1, saved1 = _ffn_block(x, p, "ffn1", tm)

    h2, u_s5, v = _mix_in(x1, p["mix_norm"], p["w_in"], tm)
    b_t = lambda b: b.transpose(0, 2, 1)
    s5_in = (p["s5_lam_re"], p["s5_lam_im"], p["s5_log_dt"].reshape(S5_GROUPS, 1), b_t(p["s5_b_re"]), b_t(p["s5_b_im"]))
    abar_re, abar_im, bbar_re, bbar_im = _s5_params(*s5_in)
    a_re, a_im = abar_re.reshape(1, S5_LANES), abar_im.reshape(1, S5_LANES)
    bb_re, bb_im = _block_diag(bbar_re).astype(bf16), _block_diag(bbar_im).astype(bf16)
    cc_re, cc_im = _block_diag(p["s5_c_re"]).astype(bf16), _block_diag(p["s5_c_im"]).astype(bf16)
    u_seg = _to_segments(u_s5)
    bu_re, bu_im = _s5_drive(u_seg, bb_re, bb_im, tm)
    seg3 = lambda a: a.reshape(ni, SEGMENTS, S5_LANES)
    flat = lambda a: a.reshape(L, S5_LANES)
    e_re, e_im = _scan_ends(seg3(bu_re), seg3(bu_im), a_re, a_im, False, bi)
    s_re, s_im = _scan_states(seg3(bu_re), seg3(bu_im), a_re, a_im, e_re, e_im, bi)
    y_lin, y_s5_seg = _s5_read(flat(s_re), flat(s_im), u_seg, cc_re, cc_im, p["s5_d"], p["s5_w_glu"], p["s5_b_glu"], ts)
    y_s5 = _from_segments(y_s5_seg)
    w_dw = jnp.pad(p["conv_w_dw"], ((0, CONV_HALO - CONV_K), (0, 0)))
    heads = jnp.arange(CONV_WIDTH) // CONV_HEAD
    avg = ((heads[:, None] == heads[None, :]).astype(f32) / CONV_HEAD).astype(bf16)
    conv_args = (w_dw, p["conv_b_dw"], p["conv_ln_g"], p["conv_ln_b"], avg)
    y_conv = _conv_fwd(v, *conv_args, tm)
    x2 = _mix_out(x1, y_s5, y_conv, p["w_out"], tm)

    x3, saved2 = _ffn_block(x2, p, "ffn2", tm)
    dx3, grads["final_norm"], loss_terms = _loss_head(x3, target, p["final_norm"].reshape(1, D_MODEL), tm)

    dx2 = _ffn_block_bwd(dx3, x2, p, "ffn2", saved2, tm, grads)

    dy_s5, dy_conv, dx2b = _mix_out_bwd(dx2, p["w_out"], tm)
    grads["w_out"] = jnp.concatenate([_mm_tn(y_s5, dx2b, bf16, "dw_out_s5"), _mm_tn(y_conv, dx2b, bf16, "dw_out_conv")], axis=0)
    dy_lin, du_skip, dd8, grads["s5_w_glu"], dbg8 = _s5_read_bwd(
        _to_segments(dy_s5), y_lin, u_seg, p["s5_d"], p["s5_w_glu"], p["s5_b_glu"], tm)
    grads["s5_d"] = dd8.sum(axis=0, keepdims=True)
    grads["s5_b_glu"] = dbg8.sum(axis=0, keepdims=True)
    ds_re, ds_im, dcc_re, dcc_im = _s5_state_grad(dy_lin, flat(s_re), flat(s_im), cc_re, cc_im, ts)
    grads["s5_c_re"], grads["s5_c_im"] = _diag_blocks(dcc_re), _diag_blocks(dcc_im)
    g_e_re, g_e_im = _scan_ends(seg3(ds_re), seg3(ds_im), a_re, -a_im, True, bi)
    g_re, g_im, da_re8, da_im8 = _rscan_states(seg3(ds_re), seg3(ds_im), a_re, -a_im, g_e_re, g_e_im, s_re, s_im, bi)
    du_seg, dbb_re, dbb_im = _s5_drive_bwd(flat(g_re), flat(g_im), u_seg, du_skip, bb_re, bb_im, ts)
    d_abar = lambda a8: a8.sum(axis=0).reshape(S5_GROUPS, S5_STATE)
    d_lr, d_li, d_ld, d_br, d_bi = _s5_params_bwd(*s5_in, d_abar(da_re8), d_abar(da_im8), _diag_blocks(dbb_re), _diag_blocks(dbb_im))
    grads["s5_lam_re"], grads["s5_lam_im"], grads["s5_log_dt"] = d_lr, d_li, d_ld.reshape(1, S5_GROUPS)
    grads["s5_b_re"], grads["s5_b_im"] = b_t(d_br), b_t(d_bi)
    z, dzc, dlg8, dlb8, dbd8 = _conv_bwd_norm(dy_conv, v, *conv_args, tm)
    grads["conv_ln_g"] = dlg8.sum(axis=0, keepdims=True)
    grads["conv_ln_b"] = dlb8.sum(axis=0, keepdims=True)
    grads["conv_b_dw"] = dbd8.sum(axis=0, keepdims=True)
    dv, dw8 = _conv_bwd_taps(dzc, z, v, w_dw, tm)
    grads["conv_w_dw"] = dw8.sum(axis=1)[:CONV_K]
    dx1, grads["mix_norm"], dub = _mix_in_bwd(dx2, x1, p["mix_norm"], _from_segments(du_seg), dv, p["w_in"], tm)
    grads["w_in"] = _mm_tn(h2, dub, bf16, "dw_in")

    dx0 = _ffn_block_bwd(dx1, x, p, "ffn1", saved1, tm, grads)
    return loss_terms, dx0, grads


MESH = pl.DeviceIdType.MESH
ANY = pl.BlockSpec(memory_space=pl.ANY)


def _place():
    return lax.axis_index("x"), lax.axis_index("y"), lax.axis_index("c")


def _all_gather(arrs, name):
    n = len(arrs)

    def body(*refs):
        ins, outs = refs[:n], refs[n:2 * n]
        send_sems, recv_sems, local_sems = refs[2 * n:]
        x, y, c = _place()
        me, sibling = (x, y, c), (x, y, 1 - c)
        chips = [(1 - x, y), (x, 1 - y), (1 - x, 1 - y)]

        def copy(a, k, block, to, src=None):
            px, py, pc = block
            dst = outs[a].at[4 * px + 2 * py + pc]
            return pltpu.make_async_remote_copy(
                src_ref=dst if src is None else src, dst_ref=dst, send_sem=send_sems.at[7 * a + k],
                recv_sem=recv_sems.at[7 * a + k], device_id=to, device_id_type=MESH)

        mine = [pltpu.make_async_copy(ins[a], outs[a].at[4 * x + 2 * y + c], local_sems.at[a]) for a in range(n)]
        for cp in mine:
            cp.start()
        first = []
        for a in range(n):
            first.append(copy(a, 0, me, sibling, src=ins[a]))
            first += [copy(a, 1 + j, me, (*chip, c), src=ins[a]) for j, chip in enumerate(chips)]
        for cp in first:
            cp.start()
        passed = [[copy(a, 4 + j, (*chip, c), sibling) for j, chip in enumerate(chips)] for a in range(n)]
        for j, chip in enumerate(chips):
            for a in range(n):
                copy(a, 1 + j, (*chip, c), me).wait_recv()
                passed[a][j].start()
        for a in range(n):
            copy(a, 0, sibling, me).wait_recv()
            for j, chip in enumerate(chips):
                copy(a, 4 + j, (*chip, 1 - c), me).wait_recv()
        for cp in first + [cp for row in passed for cp in row]:
            cp.wait_send()
        for cp in mine:
            cp.wait()

    return pl.pallas_call(
        body, name=name, in_specs=[ANY] * n, out_specs=[ANY] * n,
        out_shape=[S((N_DEV, *a.shape), a.dtype) for a in arrs],
        scratch_shapes=[pltpu.SemaphoreType.DMA((7 * n,)), pltpu.SemaphoreType.DMA((7 * n,)), pltpu.SemaphoreType.DMA((n,))],
    )(*arrs)


def _swap_with_sibling(gs, name):
    n = len(gs)

    def body(*refs):
        ins, outs = refs[:n], refs[n:2 * n]
        send_sems, recv_sems = refs[2 * n:]
        x, y, c = _place()
        copies = [pltpu.make_async_remote_copy(
            src_ref=ins[a].at[:, 1 - c], dst_ref=outs[a], send_sem=send_sems.at[a], recv_sem=recv_sems.at[a],
            device_id=(x, y, 1 - c), device_id_type=MESH) for a in range(n)]
        for cp in copies:
            cp.start()
        for cp in copies:
            cp.wait()

    return pl.pallas_call(
        body, name=name, in_specs=[ANY] * n, out_specs=[ANY] * n,
        out_shape=[S((N_CHIP, *g.shape[2:]), g.dtype) for g in gs],
        scratch_shapes=[pltpu.SemaphoreType.DMA((n,)), pltpu.SemaphoreType.DMA((n,))],
    )(*gs)


def _swap_with_chips(ps, name):
    n = len(ps)

    def body(*refs):
        ins, outs = refs[:n], refs[n:2 * n]
        send_sems, recv_sems = refs[2 * n:]
        x, y, c = _place()
        q = 2 * x + y
        peers = [(x, 1 - y), (1 - x, y), (1 - x, 1 - y)]
        copies = []
        for a in range(n):
            for j, (px, py) in enumerate(peers):
                copies.append(pltpu.make_async_remote_copy(
                    src_ref=ins[a].at[2 * px + py], dst_ref=outs[a].at[q], send_sem=send_sems.at[3 * a + j],
                    recv_sem=recv_sems.at[3 * a + j], device_id=(px, py, c), device_id_type=MESH))
        for cp in copies:
            cp.start()
        for a in range(n):
            for j, (px, py) in enumerate(peers):
                pltpu.make_async_remote_copy(
                    src_ref=ins[a].at[q], dst_ref=outs[a].at[2 * px + py], send_sem=send_sems.at[3 * a + j],
                    recv_sem=recv_sems.at[3 * a + j], device_id=(px, py, c), device_id_type=MESH).wait_recv()
        for cp in copies:
            cp.wait_send()

    return pl.pallas_call(
        body, name=name, in_specs=[ANY] * n, out_specs=[ANY] * n,
        out_shape=[S(p.shape, p.dtype) for p in ps],
        scratch_shapes=[pltpu.SemaphoreType.DMA((3 * n,)), pltpu.SemaphoreType.DMA((3 * n,))],
    )(*ps)


def _row_tile(rows, cols, itemsize):
    t = rows
    while t * cols * itemsize > (1 << 20) and t % 32 == 0:
        t //= 2
    return t


def _add_sibling(g4, st, core, name):
    _, _, R, C = g4.shape
    tr = _row_tile(R, C, 4)

    def body(c_ref, g_ref, s_ref, o_ref):
        o_ref[...] = (g_ref[...].astype(f32) + s_ref[...].astype(f32)).astype(bf16)

    return pl.pallas_call(
        body, name=name,
        grid_spec=pltpu.PrefetchScalarGridSpec(
            num_scalar_prefetch=1, grid=(N_CHIP, R // tr),
            in_specs=[pl.BlockSpec((None, None, tr, C), lambda q, i, c: (q, c[0], i, 0)),
                      pl.BlockSpec((None, tr, C), lambda q, i, c: (q, i, 0))],
            out_specs=pl.BlockSpec((None, tr, C), lambda q, i, c: (q, i, 0))),
        out_shape=S((N_CHIP, R, C), bf16),
        compiler_params=_cp("parallel", "parallel"),
    )(core, g4, st)


def _adamw(w, g, m, v):
    m = B1 * m + (1.0 - B1) * g
    v = B2 * v + (1.0 - B2) * (g * g)
    m_hat = m / (1.0 - B1 ** STEP)
    v_hat = v / (1.0 - B2 ** STEP)
    return -LR * (m_hat / (jnp.sqrt(v_hat) + ADAM_EPS) + WD * w), m, v


def _adam_sharded(w, m, v, part, got, slots, name):
    R, C = w.shape
    _, Rp, Cp = part.shape
    tr = _row_tile(R, Cp, 4) if Rp == R else R

    def body(s_ref, w_ref, m_ref, v_ref, p_ref, a_ref, b_ref, c_ref, g_out, d_out, m_out, v_out):
        g = p_ref[...].astype(f32) + a_ref[...].astype(f32) + b_ref[...].astype(f32) + c_ref[...].astype(f32)
        g = g[:, :C]
        g_out[...] = g
        d_out[...], m_out[...], v_out[...] = _adamw(w_ref[...], g, m_ref[...], v_ref[...])

    shard = pl.BlockSpec((tr, C), lambda i, s: (i, 0))
    slot = lambda k: pl.BlockSpec((None, tr, Cp), lambda i, s: (s[k], i, 0))
    return pl.pallas_call(
        body, name=name,
        grid_spec=pltpu.PrefetchScalarGridSpec(
            num_scalar_prefetch=1, grid=(R // tr,),
            in_specs=[shard, shard, shard, slot(0), slot(1), slot(2), slot(3)],
            out_specs=[shard] * 4),
        out_shape=[S((R, C), f32)] * 4,
        compiler_params=_cp("parallel"),
    )(slots, w, m, v, part, got, got, got)


def _adam_replicated(w, m, v, gathered, name):
    R = w.shape[0]

    def body(w_ref, m_ref, v_ref, g_ref, g_out, d_out, m_out, v_out):
        g = g_ref[0]
        for d in range(1, N_DEV):
            g = g + g_ref[d]
        g_out[...] = g
        d_out[...], m_out[...], v_out[...] = _adamw(w_ref[...], g, m_ref[...], v_ref[...])

    return pl.pallas_call(body, name=name, out_shape=[S((R, 128), f32)] * 4,
                          compiler_params=pltpu.CompilerParams(vmem_limit_bytes=VMEM_LIMIT))(w, m, v, gathered)


WEIGHTS = ["ffn1_norm", "ffn1_w_gate", "ffn1_w_up", "ffn1_w_down", "mix_norm", "w_in", "s5_lam_re", "s5_lam_im", "s5_log_dt",
           "s5_b_re", "s5_b_im", "s5_c_re", "s5_c_im", "s5_d", "s5_w_glu", "s5_b_glu", "conv_w_dw", "conv_b_dw", "conv_ln_g",
           "conv_ln_b", "w_out", "ffn2_norm", "ffn2_w_gate", "ffn2_w_up", "ffn2_w_down", "final_norm"]
SHARDED = ["ffn1_w_gate", "ffn1_w_up", "ffn1_w_down", "w_in", "s5_w_glu", "conv_w_dw", "w_out", "ffn2_w_gate", "ffn2_w_up",
           "ffn2_w_down"]
REPLICATED = [n for n in WEIGHTS if n not in SHARDED]
PACK = 8 * 128


def _is_up(n):
    return n.endswith("w_gate") or n.endswith("w_up")


def _shard_to_wire(n, w):
    if _is_up(n):
        w = jnp.pad(w, ((0, 0), (0, FF_SHARD_PAD - FF_SHARD)))
    elif n.endswith("w_down"):
        w = jnp.pad(w, ((0, FF_SHARD_PAD - FF_SHARD), (0, 0)))
    elif n == "conv_w_dw":
        return jnp.pad(w, ((0, CONV_HALO - CONV_K), (0, 0)))
    return w.astype(bf16)


def _gathered_to_full(n, g):
    if n == "conv_w_dw":
        return g.transpose(1, 0, 2).reshape(CONV_HALO, CONV_WIDTH)[:CONV_K]
    if _is_up(n) or n == "w_in":
        return g.transpose(1, 0, 2).reshape(g.shape[1], N_DEV * g.shape[2])
    return g.reshape(N_DEV * g.shape[1], g.shape[2])


def _grad_to_blocks(n, g):
    if n == "conv_w_dw":
        g = jnp.pad(g, ((0, CONV_HALO - CONV_K), (0, 0)))
    g = g.astype(bf16)
    if _is_up(n) or n in ("w_in", "conv_w_dw"):
        g = g.reshape(g.shape[0], N_DEV, g.shape[1] // N_DEV).transpose(1, 0, 2)
    else:
        g = g.reshape(N_DEV, g.shape[0] // N_DEV, g.shape[1])
    return g.reshape(N_CHIP, 2, *g.shape[1:])


def _pack(parts):
    out = []
    for a in parts:
        a = a.reshape(-1)
        out.append(jnp.pad(a, (0, -a.size % PACK)))
    return jnp.concatenate(out).reshape(-1, 128)


def _unpack(packed, like):
    out, at = [], 0
    flat = packed.reshape(-1)
    for a in like:
        out.append(flat[at:at + a.size].reshape(a.shape))
        at += a.size + (-a.size % PACK)
    return out


def kernel(x, ffn1_norm, ffn1_w_gate, ffn1_w_up, ffn1_w_down, mix_norm, w_in, s5_lam_re, s5_lam_im, s5_log_dt, s5_b_re, s5_b_im, s5_c_re, s5_c_im, s5_d, s5_w_glu, s5_b_glu, conv_w_dw, conv_b_dw, conv_ln_g, conv_ln_b, w_out, ffn2_norm, ffn2_w_gate, ffn2_w_up, ffn2_w_down, final_norm, loss_target, m_ffn1_norm, m_ffn1_w_gate, m_ffn1_w_up, m_ffn1_w_down, m_mix_norm, m_w_in, m_s5_lam_re, m_s5_lam_im, m_s5_log_dt, m_s5_b_re, m_s5_b_im, m_s5_c_re, m_s5_c_im, m_s5_d, m_s5_w_glu, m_s5_b_glu, m_conv_w_dw, m_conv_b_dw, m_conv_ln_g, m_conv_ln_b, m_w_out, m_ffn2_norm, m_ffn2_w_gate, m_ffn2_w_up, m_ffn2_w_down, m_final_norm, v_ffn1_norm, v_ffn1_w_gate, v_ffn1_w_up, v_ffn1_w_down, v_mix_norm, v_w_in, v_s5_lam_re, v_s5_lam_im, v_s5_log_dt, v_s5_b_re, v_s5_b_im, v_s5_c_re, v_s5_c_im, v_s5_d, v_s5_w_glu, v_s5_b_glu, v_conv_w_dw, v_conv_b_dw, v_conv_ln_g, v_conv_ln_b, v_w_out, v_ffn2_norm, v_ffn2_w_gate, v_ffn2_w_up, v_ffn2_w_down, v_final_norm):
    args = locals()
    w = {n: args[n] for n in WEIGHTS}
    m = {n: args["m_" + n] for n in WEIGHTS}
    v = {n: args["v_" + n] for n in WEIGHTS}
    shard2d = lambda a: a.reshape(a.shape[-2:])
    xq, yq, cq = _place()

    wire = [_shard_to_wire(n, shard2d(w[n])) for n in SHARDED]
    gathered = _all_gather(wire, "gather_weights")
    p = {n: _gathered_to_full(n, g) for n, g in zip(SHARDED, gathered)}
    for n in REPLICATED:
        p[n] = w[n].reshape(w[n].shape[1:]) if w[n].ndim >= 3 else w[n]

    loss_terms, dx, grads = _local_step(x[0], loss_target[0], p)

    blocks = [_grad_to_blocks(n, grads[n]) for n in SHARDED]
    from_sibling = _swap_with_sibling(blocks, "reduce_sibling")
    core = jnp.reshape(cq, (1,)).astype(jnp.int32)
    partial = [_add_sibling(b, s, core, "reduce_add_" + n) for n, b, s in zip(SHARDED, blocks, from_sibling)]
    from_chips = _swap_with_chips(partial, "reduce_chips")
    q = 2 * xq + yq
    slots = jnp.stack([q, q ^ 1, q ^ 2, q ^ 3]).astype(jnp.int32)
    out = {}
    for n, part, got in zip(SHARDED, partial, from_chips):
        rows = part.shape[1] if n == "conv_w_dw" else w[n].shape[-2]
        fit = lambda a: jnp.pad(shard2d(a), ((0, rows - a.shape[-2]), (0, 0)))
        res = _adam_sharded(fit(w[n]), fit(m[n]), fit(v[n]), part, got, slots, "adam_" + n)
        out[n] = [r[:w[n].shape[-2]].reshape(w[n].shape) for r in res]

    mine = _pack([loss_terms.sum().reshape(1)] + [grads[n] for n in REPLICATED])
    everyone = _all_gather([mine], "gather_small")[0]
    zero = jnp.zeros((1,), f32)
    res = _adam_replicated(_pack([zero] + [w[n] for n in REPLICATED]), _pack([zero] + [m[n] for n in REPLICATED]),
                           _pack([zero] + [v[n] for n in REPLICATED]), everyone, "adam_replicated")
    like = [zero] + [w[n] for n in REPLICATED]
    unpacked = [_unpack(r, like) for r in res]
    loss = unpacked[0][0].reshape(())
    for i, n in enumerate(REPLICATED):
        out[n] = [u[1 + i] for u in unpacked]

    return (loss, dx.reshape(x.shape), *[out[n][0] for n in WEIGHTS], *[out[n][1] for n in WEIGHTS],
            *[out[n][2] for n in WEIGHTS], *[out[n][3] for n in WEIGHTS])
```

```python
import functools

import jax
import jax.numpy as jnp
from jax import lax
from jax.experimental import pallas as pl
from jax.experimental.pallas import tpu as pltpu

f32 = jnp.float32
bf16 = jnp.bfloat16
S = jax.ShapeDtypeStruct

N_DEV = 8
N_CHIP = 4
D_MODEL = 1024
D_FF = 2816
FF_SHARD = D_FF // N_DEV
FF_SHARD_PAD = 384
FF_PAD = FF_SHARD_PAD * N_DEV
S5_WIDTH = 512
S5_GROUPS = 32
S5_GROUP_CH = 16
S5_STATE = 64
S5_LANES = S5_GROUPS * S5_STATE
CONV_WIDTH = 512
CONV_K = 31
CONV_HALO = 32
CONV_HEAD = 64
CONV_ROWS = 32
IN_COLS = S5_WIDTH + 2 * CONV_WIDTH
SEGMENTS = 8
SCAN_LANES = 512
EPS = 1e-6
LR, B1, B2, ADAM_EPS, WD, STEP = 0.001, 0.9, 0.999, 1e-08, 0.01, 10
VMEM_LIMIT = 56 * 1024 * 1024

NN = (((1,), (0,)), ((), ()))
NT = (((1,), (1,)), ((), ()))
TN = (((0,), (0,)), ((), ()))


def _dot(a, b, dims=NN):
    return lax.dot_general(a, b, dims, preferred_element_type=f32)


def _cp(*sem):
    return pltpu.CompilerParams(dimension_semantics=sem, vmem_limit_bytes=VMEM_LIMIT)


def _rms(x, g):
    return x * lax.rsqrt(jnp.mean(x * x, axis=-1, keepdims=True) + EPS) * g


def _rms_bwd(x, g, dh):
    _, vjp = jax.vjp(_rms, x, g)
    return vjp(dh)


def _sigmoid(x):
    return 1.0 / (1.0 + jnp.exp(-x))


def _gelu(x):
    return 0.5 * x * (1.0 + jnp.tanh(0.7978845608028654 * (x + 0.044715 * x * x * x)))


def _rows8(x):
    t, c = x.shape
    return x.reshape(t // 8, 8, c).sum(axis=0)


def _full(shape):
    return pl.BlockSpec(shape, lambda *_: (0,) * len(shape))


def _resident(shape):
    return pl.BlockSpec(shape, lambda *_: (0,) * len(shape), pipeline_mode=pl.Buffered(1))


def _ffn_up(x, g, wg, wu, tm, tn, tag):
    L = x.shape[0]

    def body(x_ref, g_ref, wg_ref, wu_ref, h_ref, dadg_ref, dadu_ref, a_ref):
        h = _rms(x_ref[...], g_ref[...]).astype(bf16)
        h_ref[...] = h
        for j in range(FF_PAD // tn):
            cols = slice(j * tn, (j + 1) * tn)
            gate = _dot(h, wg_ref[:, cols])
            up = _dot(h, wu_ref[:, cols])
            sig = _sigmoid(gate)
            silu = gate * sig
            dadg_ref[:, cols] = (up * (sig + silu * (1.0 - sig))).astype(bf16)
            dadu_ref[:, cols] = silu.astype(bf16)
            a_ref[:, cols] = (silu * up).astype(bf16)

    row = pl.BlockSpec((tm, D_MODEL), lambda i: (i, 0))
    wide = pl.BlockSpec((tm, FF_PAD), lambda i: (i, 0))
    return pl.pallas_call(
        body, name=tag + "_up", grid=(L // tm,),
        in_specs=[row, _full((1, D_MODEL)), _resident((D_MODEL, FF_PAD)), _resident((D_MODEL, FF_PAD))],
        out_specs=[row, wide, wide, wide],
        out_shape=[S((L, D_MODEL), bf16)] + [S((L, FF_PAD), bf16)] * 3,
        compiler_params=_cp("parallel"),
    )(x, g, wg, wu)


def _ffn_down(x, a, wd, tm, tag):
    L = x.shape[0]

    def body(x_ref, a_ref, wd_ref, o_ref):
        o_ref[...] = x_ref[...] + 0.5 * _dot(a_ref[...], wd_ref[...])

    return pl.pallas_call(
        body, name=tag + "_down", grid=(L // tm,),
        in_specs=[pl.BlockSpec((tm, D_MODEL), lambda i: (i, 0)), pl.BlockSpec((tm, FF_PAD), lambda i: (i, 0)),
                  _resident((FF_PAD, D_MODEL))],
        out_specs=pl.BlockSpec((tm, D_MODEL), lambda i: (i, 0)),
        out_shape=S((L, D_MODEL), f32),
        compiler_params=_cp("parallel"),
    )(x, a, wd)


def _ffn_bwd_act(dxo, wd, dadg, dadu, tm, tn, tag):
    L = dxo.shape[0]

    def body(dx_ref, wd_ref, dadg_ref, dadu_ref, dgate_ref, dup_ref, dxh_ref):
        dxh = (0.5 * dx_ref[...]).astype(bf16)
        dxh_ref[...] = dxh
        for j in range(FF_PAD // tn):
            cols = slice(j * tn, (j + 1) * tn)
            da = _dot(dxh, wd_ref[cols, :], NT)
            dgate_ref[:, cols] = (da * dadg_ref[:, cols].astype(f32)).astype(bf16)
            dup_ref[:, cols] = (da * dadu_ref[:, cols].astype(f32)).astype(bf16)

    row = pl.BlockSpec((tm, D_MODEL), lambda i: (i, 0))
    wide = pl.BlockSpec((tm, FF_PAD), lambda i: (i, 0))
    return pl.pallas_call(
        body, name=tag + "_bwd_act", grid=(L // tm,),
        in_specs=[row, _resident((FF_PAD, D_MODEL)), wide, wide],
        out_specs=[wide, wide, row],
        out_shape=[S((L, FF_PAD), bf16), S((L, FF_PAD), bf16), S((L, D_MODEL), bf16)],
        compiler_params=_cp("parallel"),
    )(dxo, wd, dadg, dadu)


def _ffn_bwd_in(dxo, x, g, dgate, dup, wg, wu, tm, tag):
    L = x.shape[0]

    def body(dxo_ref, x_ref, g_ref, dgate_ref, dup_ref, wg_ref, wu_ref, dx_ref, dg_ref):
        @pl.when(pl.program_id(0) == 0)
        def _():
            dg_ref[...] = jnp.zeros_like(dg_ref)

        dh = _dot(dgate_ref[...], wg_ref[...], NT) + _dot(dup_ref[...], wu_ref[...], NT)
        dx, dg = _rms_bwd(x_ref[...], g_ref[...], dh)
        dx_ref[...] = dxo_ref[...] + dx
        dg_ref[...] += dg

    row = pl.BlockSpec((tm, D_MODEL), lambda i: (i, 0))
    wide = pl.BlockSpec((tm, FF_PAD), lambda i: (i, 0))
    return pl.pallas_call(
        body, name=tag + "_bwd_in", grid=(L // tm,),
        in_specs=[row, row, _full((1, D_MODEL)), wide, wide, _resident((D_MODEL, FF_PAD)), _resident((D_MODEL, FF_PAD))],
        out_specs=[row, _full((1, D_MODEL))],
        out_shape=[S((L, D_MODEL), f32), S((1, D_MODEL), f32)],
        compiler_params=_cp("arbitrary"),
    )(dxo, x, g, dgate, dup, wg, wu)


def _mm_tn(a, b, out_dtype, name, tm=512, tn=1024):
    L, M = a.shape
    N = b.shape[1]
    tm, tn = min(tm, M), min(tn, N)
    while N % tn:
        tn //= 2

    def body(a_ref, b_ref, o_ref):
        o_ref[...] = _dot(a_ref[...].astype(bf16), b_ref[...].astype(bf16), TN).astype(out_dtype)

    return pl.pallas_call(
        body, name=name, grid=(M // tm, N // tn),
        in_specs=[pl.BlockSpec((L, tm), lambda i, j: (0, i)), pl.BlockSpec((L, tn), lambda i, j: (0, j))],
        out_specs=pl.BlockSpec((tm, tn), lambda i, j: (i, j)),
        out_shape=S((M, N), out_dtype),
        compiler_params=_cp("parallel", "parallel"),
    )(a, b)


def _mix_in(x, g, w_in, tm):
    L = x.shape[0]

    def body(x_ref, g_ref, w_ref, h_ref, us_ref, v_ref):
        h = _rms(x_ref[...], g_ref[...]).astype(bf16)
        h_ref[...] = h
        u = _dot(h, w_ref[...])
        us_ref[...] = u[:, :S5_WIDTH]
        v_ref[...] = u[:, S5_WIDTH:]

    row = lambda c: pl.BlockSpec((tm, c), lambda i: (i, 0))
    return pl.pallas_call(
        body, name="mix_in", grid=(L // tm,),
        in_specs=[row(D_MODEL), _full((1, D_MODEL)), _full((D_MODEL, IN_COLS))],
        out_specs=[row(D_MODEL), row(S5_WIDTH), row(2 * CONV_WIDTH)],
        out_shape=[S((L, D_MODEL), bf16), S((L, S5_WIDTH), f32), S((L, 2 * CONV_WIDTH), f32)],
        compiler_params=_cp("parallel"),
    )(x, g, w_in)


def _mix_in_bwd(dxo, x, g, du_s5, dv, w_in, tm):
    L = x.shape[0]

    def body(dxo_ref, x_ref, g_ref, dus_ref, dv_ref, w_ref, dx_ref, dg_ref, dub_ref):
        @pl.when(pl.program_id(0) == 0)
        def _():
            dg_ref[...] = jnp.zeros_like(dg_ref)

        dus = dus_ref[...].astype(bf16)
        dvb = dv_ref[...].astype(bf16)
        dub_ref[:, :S5_WIDTH] = dus
        dub_ref[:, S5_WIDTH:] = dvb
        dh = _dot(dus, w_ref[:, :S5_WIDTH], NT) + _dot(dvb, w_ref[:, S5_WIDTH:], NT)
        dx, dg = _rms_bwd(x_ref[...], g_ref[...], dh)
        dx_ref[...] = dxo_ref[...] + dx
        dg_ref[...] += dg

    row = lambda c: pl.BlockSpec((tm, c), lambda i: (i, 0))
    return pl.pallas_call(
        body, name="mix_in_bwd", grid=(L // tm,),
        in_specs=[row(D_MODEL), row(D_MODEL), _full((1, D_MODEL)), row(S5_WIDTH), row(2 * CONV_WIDTH),
                  _full((D_MODEL, IN_COLS))],
        out_specs=[row(D_MODEL), _full((1, D_MODEL)), row(IN_COLS)],
        out_shape=[S((L, D_MODEL), f32), S((1, D_MODEL), f32), S((L, IN_COLS), bf16)],
        compiler_params=_cp("arbitrary"),
    )(dxo, x, g, du_s5, dv, w_in)


def _mix_out(x, y_s5, y_conv, w_out, tm):
    L = x.shape[0]

    def body(x_ref, ys_ref, yc_ref, w_ref, o_ref):
        o_ref[...] = x_ref[...] + _dot(ys_ref[...], w_ref[:S5_WIDTH, :]) + _dot(yc_ref[...], w_ref[S5_WIDTH:, :])

    row = lambda c: pl.BlockSpec((tm, c), lambda i: (i, 0))
    return pl.pallas_call(
        body, name="mix_out", grid=(L // tm,),
        in_specs=[row(D_MODEL), row(S5_WIDTH), row(CONV_WIDTH), _full((D_MODEL, D_MODEL))],
        out_specs=row(D_MODEL), out_shape=S((L, D_MODEL), f32),
        compiler_params=_cp("parallel"),
    )(x, y_s5, y_conv, w_out)


def _mix_out_bwd(dx, w_out, tm):
    L = dx.shape[0]

    def body(dx_ref, w_ref, dys_ref, dyc_ref, dxb_ref):
        dxb = dx_ref[...].astype(bf16)
        dxb_ref[...] = dxb
        dys_ref[...] = _dot(dxb, w_ref[:S5_WIDTH, :], NT)
        dyc_ref[...] = _dot(dxb, w_ref[S5_WIDTH:, :], NT)

    row = lambda c: pl.BlockSpec((tm, c), lambda i: (i, 0))
    return pl.pallas_call(
        body, name="mix_out_bwd", grid=(L // tm,),
        in_specs=[row(D_MODEL), _full((D_MODEL, D_MODEL))],
        out_specs=[row(S5_WIDTH), row(CONV_WIDTH), row(D_MODEL)],
        out_shape=[S((L, S5_WIDTH), f32), S((L, CONV_WIDTH), f32), S((L, D_MODEL), bf16)],
        compiler_params=_cp("parallel"),
    )(dx, w_out)


def _s5_discretise(lam_re, lam_im, log_dt, b_re, b_im):
    dt = jnp.exp(log_dt)
    mag = jnp.exp(lam_re * dt)
    abar_re = mag * jnp.cos(lam_im * dt)
    abar_im = mag * jnp.sin(lam_im * dt)
    den = lam_re * lam_re + lam_im * lam_im
    num_re = abar_re - 1.0
    f_re = ((num_re * lam_re + abar_im * lam_im) / den)[:, None, :]
    f_im = ((abar_im * lam_re - num_re * lam_im) / den)[:, None, :]
    return abar_re, abar_im, f_re * b_re - f_im * b_im, f_re * b_im + f_im * b_re


def _s5_params(lam_re, lam_im, log_dt, b_re, b_im):
    def body(lr, li, ld, br, bi, ar_ref, ai_ref, bbr_ref, bbi_ref):
        ar, ai, bbr, bbi = _s5_discretise(lr[...], li[...], ld[...], br[...], bi[...])
        ar_ref[...], ai_ref[...], bbr_ref[...], bbi_ref[...] = ar, ai, bbr, bbi

    gp = S((S5_GROUPS, S5_STATE), f32)
    gcp = S((S5_GROUPS, S5_GROUP_CH, S5_STATE), f32)
    return pl.pallas_call(body, name="s5_params", out_shape=[gp, gp, gcp, gcp])(lam_re, lam_im, log_dt, b_re, b_im)


def _s5_params_bwd(lam_re, lam_im, log_dt, b_re, b_im, d_ar, d_ai, d_bbr, d_bbi):
    def body(lr, li, ld, br, bi, car, cai, cbr, cbi, o_lr, o_li, o_ld, o_br, o_bi):
        _, vjp = jax.vjp(_s5_discretise, lr[...], li[...], ld[...], br[...], bi[...])
        o_lr[...], o_li[...], o_ld[...], o_br[...], o_bi[...] = vjp((car[...], cai[...], cbr[...], cbi[...]))

    gp = S((S5_GROUPS, S5_STATE), f32)
    gcp = S((S5_GROUPS, S5_GROUP_CH, S5_STATE), f32)
    return pl.pallas_call(body, name="s5_params_bwd", out_shape=[gp, gp, S((S5_GROUPS, 1), f32), gcp, gcp])(
        lam_re, lam_im, log_dt, b_re, b_im, d_ar, d_ai, d_bbr, d_bbi)


def _cmul(ar, ai, br, bi):
    return ar * br - ai * bi, ar * bi + ai * br


def _scan_specs(ni, bi, reverse):
    nb = ni // bi
    blk = pl.BlockSpec((bi, SEGMENTS, SCAN_LANES), (lambda c, j: (nb - 1 - j, 0, c)) if reverse else (lambda c, j: (j, 0, c)))
    vec = pl.BlockSpec((1, SCAN_LANES), lambda c, j: (0, c))
    tile = pl.BlockSpec((SEGMENTS, SCAN_LANES), lambda c, j: (0, c))
    return nb, blk, vec, tile


def _scan_ends(bu_re, bu_im, a_re, a_im, reverse, bi):
    ni = bu_re.shape[0]
    nb, blk, vec, tile = _scan_specs(ni, bi, reverse)

    def body(br_ref, bi_ref, ar_ref, ai_ref, er_ref, ei_ref):
        @pl.when(pl.program_id(1) == 0)
        def _():
            er_ref[...] = jnp.zeros_like(er_ref)
            ei_ref[...] = jnp.zeros_like(ei_ref)

        ar = jnp.broadcast_to(ar_ref[...], (SEGMENTS, SCAN_LANES))
        ai = jnp.broadcast_to(ai_ref[...], (SEGMENTS, SCAN_LANES))

        def step(n, c):
            i = (bi - 1 - n) if reverse else n
            pr, pi = _cmul(ar, ai, c[0], c[1])
            return pr + br_ref[i], pi + bi_ref[i]

        er_ref[...], ei_ref[...] = lax.fori_loop(0, bi, step, (er_ref[...], ei_ref[...]), unroll=4)

    out = S((SEGMENTS, S5_LANES), f32)
    return pl.pallas_call(
        body, name="s5_rscan_ends" if reverse else "s5_scan_ends", grid=(S5_LANES // SCAN_LANES, nb),
        in_specs=[blk, blk, vec, vec], out_specs=[tile, tile], out_shape=[out, out],
        compiler_params=_cp("parallel", "arbitrary"),
    )(bu_re, bu_im, a_re, a_im)


def _segment_starts(er, ei, ar, ai, steps, reverse):
    pr, pi = ar, ai
    n = 1
    while n < steps:
        pr, pi = _cmul(pr, pi, pr, pi)
        n *= 2
    assert n == steps
    row = lax.broadcasted_iota(jnp.int32, (SEGMENTS, SCAN_LANES), 0)
    hr = jnp.zeros((1, SCAN_LANES), f32)
    hi = jnp.zeros((1, SCAN_LANES), f32)
    out_r = jnp.zeros((SEGMENTS, SCAN_LANES), f32)
    out_i = jnp.zeros((SEGMENTS, SCAN_LANES), f32)
    order = range(SEGMENTS - 1, 0, -1) if reverse else range(0, SEGMENTS - 1)
    for r in order:
        qr, qi = _cmul(pr, pi, hr, hi)
        hr, hi = qr + er[r:r + 1, :], qi + ei[r:r + 1, :]
        nxt = r - 1 if reverse else r + 1
        out_r = jnp.where(row == nxt, hr, out_r)
        out_i = jnp.where(row == nxt, hi, out_i)
    return out_r, out_i


def _scan_states(bu_re, bu_im, a_re, a_im, e_re, e_im, bi):
    ni = bu_re.shape[0]
    nb, blk, vec, tile = _scan_specs(ni, bi, False)

    def body(br_ref, bi_ref, ar_ref, ai_ref, er_ref, ei_ref, sr_ref, si_ref, cr_ref, ci_ref):
        @pl.when(pl.program_id(1) == 0)
        def _():
            cr_ref[...], ci_ref[...] = _segment_starts(er_ref[...], ei_ref[...], ar_ref[...], ai_ref[...], ni, False)

        ar = jnp.broadcast_to(ar_ref[...], (SEGMENTS, SCAN_LANES))
        ai = jnp.broadcast_to(ai_ref[...], (SEGMENTS, SCAN_LANES))

        def step(i, c):
            pr, pi = _cmul(ar, ai, c[0], c[1])
            nr, nim = pr + br_ref[i], pi + bi_ref[i]
            sr_ref[i] = nr
            si_ref[i] = nim
            return nr, nim

        cr_ref[...], ci_ref[...] = lax.fori_loop(0, bi, step, (cr_ref[...], ci_ref[...]), unroll=4)

    return pl.pallas_call(
        body, name="s5_scan_states", grid=(S5_LANES // SCAN_LANES, nb),
        in_specs=[blk, blk, vec, vec, tile, tile],
        out_specs=[blk, blk], out_shape=[S(bu_re.shape, f32)] * 2,
        scratch_shapes=[pltpu.VMEM((SEGMENTS, SCAN_LANES), f32)] * 2,
        compiler_params=_cp("parallel", "arbitrary"),
    )(bu_re, bu_im, a_re, a_im, e_re, e_im)


def _rscan_states(ds_re, ds_im, a_re, a_im, e_re, e_im, s_re, s_im, bi):
    ni = ds_re.shape[0]
    nb, blk, vec, tile = _scan_specs(ni, bi, True)

    def body(dr_ref, di_ref, ar_ref, ai_ref, er_ref, ei_ref, sr_ref, si_ref, pr_ref, pi_ref, lr_ref, li_ref,
             gr_ref, gi_ref, dar_ref, dai_ref, cr_ref, ci_ref):
        j = pl.program_id(1)

        @pl.when(j == 0)
        def _():
            cr_ref[...], ci_ref[...] = _segment_starts(er_ref[...], ei_ref[...], ar_ref[...], ai_ref[...], ni, True)
            dar_ref[...] = jnp.zeros_like(dar_ref)
            dai_ref[...] = jnp.zeros_like(dai_ref)

        ar = jnp.broadcast_to(ar_ref[...], (SEGMENTS, SCAN_LANES))
        ai = jnp.broadcast_to(ai_ref[...], (SEGMENTS, SCAN_LANES))

        def step(n, c):
            i = bi - 1 - n
            gr, gi, accr, acci = c
            qr, qi = _cmul(ar, ai, gr, gi)
            gr, gi = qr + dr_ref[i], qi + di_ref[i]
            gr_ref[i] = gr
            gi_ref[i] = gi
            sr, si = sr_ref[i - 1], si_ref[i - 1]
            return gr, gi, accr + (gr * sr + gi * si), acci + (gi * sr - gr * si)

        gr, gi, accr, acci = lax.fori_loop(0, bi - 1, step, (cr_ref[...], ci_ref[...], dar_ref[...], dai_ref[...]), unroll=3)
        qr, qi = _cmul(ar, ai, gr, gi)
        gr, gi = qr + dr_ref[0], qi + di_ref[0]
        gr_ref[0] = gr
        gi_ref[0] = gi
        cr_ref[...], ci_ref[...] = gr, gi
        row = lax.broadcasted_iota(jnp.int32, (SEGMENTS, SCAN_LANES), 0)
        first = j == nb - 1
        wrap_r = jnp.where(row == 0, 0.0, pltpu.roll(lr_ref[0], 1, 0))
        wrap_i = jnp.where(row == 0, 0.0, pltpu.roll(li_ref[0], 1, 0))
        sr = jnp.where(first, wrap_r, pr_ref[0])
        si = jnp.where(first, wrap_i, pi_ref[0])
        dar_ref[...] = accr + gr * sr + gi * si
        dai_ref[...] = acci + gi * sr - gr * si

    prev = pl.BlockSpec((1, SEGMENTS, SCAN_LANES), lambda c, j: (jnp.maximum((nb - 1 - j) * bi - 1, 0), 0, c))
    last = pl.BlockSpec((1, SEGMENTS, SCAN_LANES), lambda c, j: (ni - 1, 0, c))
    return pl.pallas_call(
        body, name="s5_rscan_states", grid=(S5_LANES // SCAN_LANES, nb),
        in_specs=[blk, blk, vec, vec, tile, tile, blk, blk, prev, prev, last, last],
        out_specs=[blk, blk, tile, tile],
        out_shape=[S(ds_re.shape, f32)] * 2 + [S((SEGMENTS, S5_LANES), f32)] * 2,
        scratch_shapes=[pltpu.VMEM((SEGMENTS, SCAN_LANES), f32)] * 2,
        compiler_params=_cp("parallel", "arbitrary"),
    )(ds_re, ds_im, a_re, a_im, e_re, e_im, s_re, s_im, s_re, s_im, s_re, s_im)


def _s5_drive(u, bb_re, bb_im, tm):
    L = u.shape[0]

    def body(u_ref, br_ref, bi_ref, or_ref, oi_ref):
        ub = u_ref[...].astype(bf16)
        or_ref[...] = _dot(ub, br_ref[...])
        oi_ref[...] = _dot(ub, bi_ref[...])

    big = pl.BlockSpec((tm, S5_LANES), lambda i: (i, 0))
    return pl.pallas_call(
        body, name="s5_drive", grid=(L // tm,),
        in_specs=[pl.BlockSpec((tm, S5_WIDTH), lambda i: (i, 0)), _full((S5_WIDTH, S5_LANES)), _full((S5_WIDTH, S5_LANES))],
        out_specs=[big, big], out_shape=[S((L, S5_LANES), f32)] * 2,
        compiler_params=_cp("parallel"),
    )(u, bb_re, bb_im)


def _s5_read(s_re, s_im, u, c_re, c_im, d_skip, w_glu, b_glu, tm):
    L = u.shape[0]

    def body(sr_ref, si_ref, u_ref, cr_ref, ci_ref, d_ref, w_ref, b_ref, yl_ref, o_ref):
        yl = _dot(sr_ref[...].astype(bf16), cr_ref[...], NT) - _dot(si_ref[...].astype(bf16), ci_ref[...], NT)
        yl_ref[...] = yl
        y = _gelu(yl + d_ref[...] * u_ref[...])
        z = _dot(y.astype(bf16), w_ref[...]) + b_ref[...]
        o_ref[...] = (y * _sigmoid(z)).astype(bf16)

    big = pl.BlockSpec((tm, S5_LANES), lambda i: (i, 0))
    row = pl.BlockSpec((tm, S5_WIDTH), lambda i: (i, 0))
    vec = _full((1, S5_WIDTH))
    return pl.pallas_call(
        body, name="s5_read", grid=(L // tm,),
        in_specs=[big, big, row, _full((S5_WIDTH, S5_LANES)), _full((S5_WIDTH, S5_LANES)), vec,
                  _full((S5_WIDTH, S5_WIDTH)), vec],
        out_specs=[row, row], out_shape=[S((L, S5_WIDTH), f32), S((L, S5_WIDTH), bf16)],
        compiler_params=_cp("parallel"),
    )(s_re, s_im, u, c_re, c_im, d_skip, w_glu, b_glu)


def _s5_read_bwd(dout, y_lin, u, d_skip, w_glu, b_glu, tm):
    L = u.shape[0]

    def body(do_ref, yl_ref, u_ref, d_ref, w_ref, b_ref, dyl_ref, du_ref, dd_ref, dw_ref, db_ref):
        @pl.when(pl.program_id(0) == 0)
        def _():
            dd_ref[...] = jnp.zeros_like(dd_ref)
            dw_ref[...] = jnp.zeros_like(dw_ref)
            db_ref[...] = jnp.zeros_like(db_ref)

        u, d, dout = u_ref[...], d_ref[...], do_ref[...]
        y, gelu_vjp = jax.vjp(_gelu, yl_ref[...] + d * u)
        yb = y.astype(bf16)
        sig = _sigmoid(_dot(yb, w_ref[...]) + b_ref[...])
        dz = dout * y * sig * (1.0 - sig)
        dzb = dz.astype(bf16)
        dy = dout * sig + _dot(dzb, w_ref[...], NT)
        (dyp,) = gelu_vjp(dy)
        dyl_ref[...] = dyp.astype(bf16)
        du_ref[...] = d * dyp
        dd_ref[...] += _rows8(dyp * u)
        db_ref[...] += _rows8(dz)
        dw_ref[...] += _dot(yb, dzb, TN)

    row = pl.BlockSpec((tm, S5_WIDTH), lambda i: (i, 0))
    vec = _full((1, S5_WIDTH))
    part = _full((8, S5_WIDTH))
    return pl.pallas_call(
        body, name="s5_read_bwd", grid=(L // tm,),
        in_specs=[row, row, row, vec, _full((S5_WIDTH, S5_WIDTH)), vec],
        out_specs=[row, row, part, _full((S5_WIDTH, S5_WIDTH)), part],
        out_shape=[S((L, S5_WIDTH), bf16), S((L, S5_WIDTH), f32), S((8, S5_WIDTH), f32),
                   S((S5_WIDTH, S5_WIDTH), f32), S((8, S5_WIDTH), f32)],
        compiler_params=_cp("arbitrary"),
    )(dout, y_lin, u, d_skip, w_glu, b_glu)


def _s5_state_grad(dy_lin, s_re, s_im, c_re, c_im, tm):
    L = dy_lin.shape[0]

    def body(dy_ref, sr_ref, si_ref, cr_ref, ci_ref, dsr_ref, dsi_ref, dcr_ref, dci_ref):
        @pl.when(pl.program_id(0) == 0)
        def _():
            dcr_ref[...] = jnp.zeros_like(dcr_ref)
            dci_ref[...] = jnp.zeros_like(dci_ref)

        dy = dy_ref[...]
        dsr_ref[...] = _dot(dy, cr_ref[...])
        dsi_ref[...] = -_dot(dy, ci_ref[...])
        dcr_ref[...] += _dot(dy, sr_ref[...].astype(bf16), TN)
        dci_ref[...] -= _dot(dy, si_ref[...].astype(bf16), TN)

    big = pl.BlockSpec((tm, S5_LANES), lambda i: (i, 0))
    mat = _full((S5_WIDTH, S5_LANES))
    return pl.pallas_call(
        body, name="s5_state_grad", grid=(L // tm,),
        in_specs=[pl.BlockSpec((tm, S5_WIDTH), lambda i: (i, 0)), big, big, mat, mat],
        out_specs=[big, big, mat, mat],
        out_shape=[S((L, S5_LANES), f32)] * 2 + [S((S5_WIDTH, S5_LANES), f32)] * 2,
        compiler_params=_cp("arbitrary"),
    )(dy_lin, s_re, s_im, c_re, c_im)


def _s5_drive_bwd(g_re, g_im, u, du_skip, bb_re, bb_im, tm):
    L = u.shape[0]

    def body(gr_ref, gi_ref, u_ref, dus_ref, br_ref, bi_ref, du_ref, dbr_ref, dbi_ref):
        @pl.when(pl.program_id(0) == 0)
        def _():
            dbr_ref[...] = jnp.zeros_like(dbr_ref)
            dbi_ref[...] = jnp.zeros_like(dbi_ref)

        gr = gr_ref[...].astype(bf16)
        gi = gi_ref[...].astype(bf16)
        ub = u_ref[...].astype(bf16)
        du_ref[...] = dus_ref[...] + _dot(gr, br_ref[...], NT) + _dot(gi, bi_ref[...], NT)
        dbr_ref[...] += _dot(ub, gr, TN)
        dbi_ref[...] += _dot(ub, gi, TN)

    big = pl.BlockSpec((tm, S5_LANES), lambda i: (i, 0))
    row = pl.BlockSpec((tm, S5_WIDTH), lambda i: (i, 0))
    mat = _full((S5_WIDTH, S5_LANES))
    return pl.pallas_call(
        body, name="s5_drive_bwd", grid=(L // tm,),
        in_specs=[big, big, row, row, mat, mat],
        out_specs=[row, mat, mat],
        out_shape=[S((L, S5_WIDTH), f32)] + [S((S5_WIDTH, S5_LANES), f32)] * 2,
        compiler_params=_cp("arbitrary"),
    )(g_re, g_im, u, du_skip, bb_re, bb_im)


def _group_mean(x, avg):
    hi = x.astype(bf16)
    lo = (x - hi.astype(f32)).astype(bf16)
    return _dot(hi, avg) + _dot(lo, avg)


def _conv_act(zn, ln_g, ln_b):
    t = zn * ln_g + ln_b
    return t * _sigmoid(t)


def _glu_padded(v_ref, halo_ref, zpad_ref, tm):
    v = v_ref[...]
    vh = halo_ref[...]
    zh = vh[:, :CONV_WIDTH] * _sigmoid(vh[:, CONV_WIDTH:])
    zpad_ref[:CONV_HALO, :] = jnp.where(pl.program_id(0) > 0, zh, 0.0)
    zpad_ref[CONV_HALO:CONV_HALO + tm, :] = v[:, :CONV_WIDTH] * _sigmoid(v[:, CONV_WIDTH:])
    zpad_ref[CONV_HALO + tm:, :] = jnp.zeros((8, CONV_WIDTH), f32)


def _shifted(pad_ref, sh_ref, tm):
    for b in range(8):
        sh_ref[b] = pad_ref[pl.ds(b, tm + CONV_HALO), :]


def _window(sh_ref, r0, off, rows):
    return sh_ref[off % 8, pl.ds(pl.multiple_of(r0 + 8 * (off // 8), 8), rows), :]


def _tap_sum(w_ref, sh_ref, taps, out_ref, tm, bias):
    def chunk(c, carry):
        r0 = pl.multiple_of(c * CONV_ROWS, CONV_ROWS)
        acc = jnp.zeros((CONV_ROWS, CONV_WIDTH), f32) + bias
        for k, off in taps:
            acc = acc + w_ref[k:k + 1, :] * _window(sh_ref, r0, off, CONV_ROWS)
        out_ref[pl.ds(r0, CONV_ROWS), :] = acc
        return carry

    lax.fori_loop(0, tm // CONV_ROWS, chunk, 0)


FWD_TAPS = [(k, CONV_HALO - (CONV_K - 1) + k) for k in range(CONV_K)]
BWD_TAPS = [(k, CONV_K - 1 - k) for k in range(CONV_K)]


def _conv_specs(tm):
    per = tm // CONV_HALO
    vrow = pl.BlockSpec((tm, 2 * CONV_WIDTH), lambda i: (i, 0))
    vhalo = pl.BlockSpec((CONV_HALO, 2 * CONV_WIDTH), lambda i: (jnp.maximum(i * per - 1, 0), 0))
    return vrow, vhalo


def _conv_scratch(tm):
    return [pltpu.VMEM((tm + CONV_HALO + 8, CONV_WIDTH), f32), pltpu.VMEM((8, tm + CONV_HALO, CONV_WIDTH), f32)]


def _conv_fwd(v, w_dw, b_dw, ln_g, ln_b, avg, tm):
    L = v.shape[0]

    def body(v_ref, halo_ref, w_ref, b_ref, g_ref, bb_ref, avg_ref, o_ref, zc_ref, zpad_ref, zs_ref):
        _glu_padded(v_ref, halo_ref, zpad_ref, tm)
        _shifted(zpad_ref, zs_ref, tm)
        _tap_sum(w_ref, zs_ref, FWD_TAPS, zc_ref, tm, b_ref[...])
        zc = zc_ref[...]
        xc = zc - _group_mean(zc, avg_ref[...])
        zn = xc * lax.rsqrt(_group_mean(xc * xc, avg_ref[...]) + EPS)
        o_ref[...] = _conv_act(zn, g_ref[...], bb_ref[...]).astype(bf16)

    vrow, vhalo = _conv_specs(tm)
    vec = _full((1, CONV_WIDTH))
    row = pl.BlockSpec((tm, CONV_WIDTH), lambda i: (i, 0))
    return pl.pallas_call(
        body, name="conv_fwd", grid=(L // tm,),
        in_specs=[vrow, vhalo, _full((CONV_HALO, CONV_WIDTH)), vec, vec, vec, _full((CONV_WIDTH, CONV_WIDTH))],
        out_specs=[row, row], out_shape=[S((L, CONV_WIDTH), bf16), S((L, CONV_WIDTH), f32)],
        scratch_shapes=_conv_scratch(tm),
        compiler_params=_cp("arbitrary"),
    )(v, v, w_dw, b_dw, ln_g, ln_b, avg)


def _conv_bwd_norm(dout, zc, ln_g, ln_b, avg, tm):
    L = zc.shape[0]

    def body(do_ref, zc_ref, g_ref, bb_ref, avg_ref, dzc_ref, dg_ref, db_ref, dbd_ref):
        @pl.when(pl.program_id(0) == 0)
        def _():
            dg_ref[...] = jnp.zeros_like(dg_ref)
            db_ref[...] = jnp.zeros_like(db_ref)
            dbd_ref[...] = jnp.zeros_like(dbd_ref)

        avg = avg_ref[...]
        zc = zc_ref[...]
        xc = zc - _group_mean(zc, avg)
        rstd = lax.rsqrt(_group_mean(xc * xc, avg) + EPS)
        xhat = xc * rstd
        _, act_vjp = jax.vjp(_conv_act, xhat, g_ref[...], bb_ref[...])
        dxhat, dg, db = act_vjp(do_ref[...])
        dzc = rstd * (dxhat - _group_mean(dxhat, avg) - xhat * _group_mean(dxhat * xhat, avg))
        dzc_ref[...] = dzc
        dg_ref[0:1, :] += dg
        db_ref[0:1, :] += db
        dbd_ref[...] += _rows8(dzc)

    vec = _full((1, CONV_WIDTH))
    row = pl.BlockSpec((tm, CONV_WIDTH), lambda i: (i, 0))
    part = _full((8, CONV_WIDTH))
    return pl.pallas_call(
        body, name="conv_bwd_norm", grid=(L // tm,),
        in_specs=[row, row, vec, vec, _full((CONV_WIDTH, CONV_WIDTH))],
        out_specs=[row, part, part, part],
        out_shape=[S((L, CONV_WIDTH), f32)] + [S((8, CONV_WIDTH), f32)] * 3,
        compiler_params=_cp("arbitrary"),
    )(dout, zc, ln_g, ln_b, avg)


def _conv_bwd_taps(dzc, v, w_dw, tm):
    L = v.shape[0]
    nt = L // tm
    per = tm // CONV_HALO

    def body(d_ref, dn_ref, v_ref, halo_ref, w_ref, dv_ref, dw_ref, zpad_ref, zs_ref, dpad_ref, ds_ref, dz_ref):
        i = pl.program_id(0)

        @pl.when(i == 0)
        def _():
            dw_ref[...] = jnp.zeros_like(dw_ref)

        _glu_padded(v_ref, halo_ref, zpad_ref, tm)
        _shifted(zpad_ref, zs_ref, tm)
        dpad_ref[:tm, :] = d_ref[...]
        dpad_ref[tm:tm + CONV_HALO, :] = jnp.where(i < nt - 1, dn_ref[...], 0.0)
        dpad_ref[tm + CONV_HALO:, :] = jnp.zeros((8, CONV_WIDTH), f32)
        _shifted(dpad_ref, ds_ref, tm)
        _tap_sum(w_ref, ds_ref, BWD_TAPS, dz_ref, tm, 0.0)

        for first in range(0, CONV_K, 8):
            taps = FWD_TAPS[first:first + 8]

            def chunk(c, accs, taps=taps):
                r0 = pl.multiple_of(c * 8, 8)
                d = d_ref[pl.ds(r0, 8), :]
                return tuple(acc + d * _window(zs_ref, r0, off, 8) for acc, (_, off) in zip(accs, taps))

            accs = lax.fori_loop(0, tm // 8, chunk, tuple(jnp.zeros((8, CONV_WIDTH), f32) for _ in taps), unroll=2)
            for acc, (k, _) in zip(accs, taps):
                dw_ref[k] += acc

        dz = dz_ref[...]
        v = v_ref[...]
        sig = _sigmoid(v[:, CONV_WIDTH:])
        dv_ref[:, :CONV_WIDTH] = dz * sig
        dv_ref[:, CONV_WIDTH:] = dz * v[:, :CONV_WIDTH] * sig * (1.0 - sig)

    vrow, vhalo = _conv_specs(tm)
    row = pl.BlockSpec((tm, CONV_WIDTH), lambda i: (i, 0))
    nxt = pl.BlockSpec((CONV_HALO, CONV_WIDTH), lambda i: (jnp.minimum((i + 1) * per, nt * per - 1), 0))
    return pl.pallas_call(
        body, name="conv_bwd_taps", grid=(nt,),
        in_specs=[row, nxt, vrow, vhalo, _full((CONV_HALO, CONV_WIDTH))],
        out_specs=[vrow, _full((CONV_HALO, 8, CONV_WIDTH))],
        out_shape=[S((L, 2 * CONV_WIDTH), f32), S((CONV_HALO, 8, CONV_WIDTH), f32)],
        scratch_shapes=_conv_scratch(tm) * 2 + [pltpu.VMEM((tm, CONV_WIDTH), f32)],
        compiler_params=_cp("arbitrary"),
    )(dzc, dzc, v, v, w_dw)


def _loss_head(x, target, g, tm):
    L = x.shape[0]

    def body(x_ref, t_ref, g_ref, dx_ref, dg_ref, l_ref):
        @pl.when(pl.program_id(0) == 0)
        def _():
            dg_ref[...] = jnp.zeros_like(dg_ref)
            l_ref[...] = jnp.zeros_like(l_ref)

        x, g = x_ref[...], g_ref[...]
        e = _rms(x, g) - t_ref[...]
        l_ref[...] += _rows8(e * e) * (0.5 / D_MODEL)
        dx, dg = _rms_bwd(x, g, e * (1.0 / D_MODEL))
        dx_ref[...] = dx
        dg_ref[...] += dg

    row = pl.BlockSpec((tm, D_MODEL), lambda i: (i, 0))
    return pl.pallas_call(
        body, name="loss_head", grid=(L // tm,),
        in_specs=[row, row, _full((1, D_MODEL))],
        out_specs=[row, _full((1, D_MODEL)), _full((8, D_MODEL))],
        out_shape=[S((L, D_MODEL), f32), S((1, D_MODEL), f32), S((8, D_MODEL), f32)],
        compiler_params=_cp("arbitrary"),
    )(x, target, g)


def _to_segments(a):
    L, c = a.shape
    return a.reshape(SEGMENTS, L // SEGMENTS, c).transpose(1, 0, 2).reshape(L, c)


def _from_segments(a):
    L, c = a.shape
    return a.reshape(L // SEGMENTS, SEGMENTS, c).transpose(1, 0, 2).reshape(L, c)


def _diag_mask_and_tiling():
    shape = (S5_WIDTH, S5_LANES)
    mask = (lax.broadcasted_iota(jnp.int32, shape, 0) >> 4) == (lax.broadcasted_iota(jnp.int32, shape, 1) >> 6)
    tshape = (S5_STATE, S5_LANES)
    tiling = lax.broadcasted_iota(jnp.int32, tshape, 0) == (lax.broadcasted_iota(jnp.int32, tshape, 1) & (S5_STATE - 1))
    return mask, tiling.astype(f32)


def _block_diag(ms):
    n = len(ms)

    def body(*refs):
        mask, tiling = _diag_mask_and_tiling()
        for a in range(n):
            spread = lax.dot_general(refs[a][...], tiling, NN, precision=lax.Precision.HIGHEST, preferred_element_type=f32)
            refs[n + a][...] = jnp.where(mask, spread, 0.0).astype(bf16)

    return pl.pallas_call(body, name="s5_block_diag", out_shape=[S((S5_WIDTH, S5_LANES), bf16)] * n,
                          compiler_params=pltpu.CompilerParams(vmem_limit_bytes=VMEM_LIMIT))(
        *[m.reshape(S5_WIDTH, S5_STATE) for m in ms])


def _diag_blocks(ms):
    n = len(ms)

    def body(*refs):
        mask, tiling = _diag_mask_and_tiling()
        for a in range(n):
            kept = jnp.where(mask, refs[a][...], 0.0)
            refs[n + a][...] = lax.dot_general(kept, tiling, NT, precision=lax.Precision.HIGHEST, preferred_element_type=f32)

    out = pl.pallas_call(body, name="s5_diag_blocks", out_shape=[S((S5_WIDTH, S5_STATE), f32)] * n,
                         compiler_params=pltpu.CompilerParams(vmem_limit_bytes=VMEM_LIMIT))(*ms)
    return [o.reshape(S5_GROUPS, S5_GROUP_CH, S5_STATE) for o in out]


def _ffn_block(x, p, tag, tm):
    h, gate, up, a = _ffn_up(x, p[tag + "_norm"], p[tag + "_w_gate"], p[tag + "_w_up"], tm, 768, tag)
    return _ffn_down(x, a, p[tag + "_w_down"], tm, tag), (h, gate, up, a)


def _ffn_block_bwd(dxo, x, p, tag, saved, tm, grads):
    h, gate, up, a = saved
    dgate, dup, dxh = _ffn_bwd_act(dxo, p[tag + "_w_down"], gate, up, tm, 768, tag)
    dx, grads[tag + "_norm"] = _ffn_bwd_in(dxo, x, p[tag + "_norm"], dgate, dup, p[tag + "_w_gate"], p[tag + "_w_up"], tm, tag)
    grads[tag + "_w_gate"] = _mm_tn(h, dgate, bf16, tag + "_dw_gate")
    grads[tag + "_w_up"] = _mm_tn(h, dup, bf16, tag + "_dw_up")
    grads[tag + "_w_down"] = _mm_tn(a, dxh, bf16, tag + "_dw_down")
    return dx


def _local_step(x, target, p):
    L = x.shape[0]
    tm = min(512, L)
    ts = min(256, L)
    ni = L // SEGMENTS
    bi = min(64, ni)
    grads = {}

    x1, saved1 = _ffn_block(x, p, "ffn1", tm)

    h2, u_s5, v = _mix_in(x1, p["mix_norm"], p["w_in"], tm)
    b_t = lambda b: b.transpose(0, 2, 1)
    s5_in = (p["s5_lam_re"], p["s5_lam_im"], p["s5_log_dt"].reshape(S5_GROUPS, 1), b_t(p["s5_b_re"]), b_t(p["s5_b_im"]))
    abar_re, abar_im, bbar_re, bbar_im = _s5_params(*s5_in)
    a_re, a_im = abar_re.reshape(1, S5_LANES), abar_im.reshape(1, S5_LANES)
    bb_re, bb_im, cc_re, cc_im = _block_diag([bbar_re, bbar_im, p["s5_c_re"], p["s5_c_im"]])
    u_seg = _to_segments(u_s5)
    bu_re, bu_im = _s5_drive(u_seg, bb_re, bb_im, tm)
    seg3 = lambda a: a.reshape(ni, SEGMENTS, S5_LANES)
    flat = lambda a: a.reshape(L, S5_LANES)
    e_re, e_im = _scan_ends(seg3(bu_re), seg3(bu_im), a_re, a_im, False, bi)
    s_re, s_im = _scan_states(seg3(bu_re), seg3(bu_im), a_re, a_im, e_re, e_im, bi)
    y_lin, y_s5_seg = _s5_read(flat(s_re), flat(s_im), u_seg, cc_re, cc_im, p["s5_d"], p["s5_w_glu"], p["s5_b_glu"], ts)
    y_s5 = _from_segments(y_s5_seg)
    w_dw = jnp.pad(p["conv_w_dw"], ((0, CONV_HALO - CONV_K), (0, 0)))
    heads = jnp.arange(CONV_WIDTH) // CONV_HEAD
    avg = ((heads[:, None] == heads[None, :]).astype(f32) / CONV_HEAD).astype(bf16)
    conv_args = (w_dw, p["conv_b_dw"], p["conv_ln_g"], p["conv_ln_b"], avg)
    y_conv, zc = _conv_fwd(v, *conv_args, tm)
    x2 = _mix_out(x1, y_s5, y_conv, p["w_out"], tm)

    x3, saved2 = _ffn_block(x2, p, "ffn2", tm)
    dx3, grads["final_norm"], loss_terms = _loss_head(x3, target, p["final_norm"].reshape(1, D_MODEL), tm)

    dx2 = _ffn_block_bwd(dx3, x2, p, "ffn2", saved2, tm, grads)

    dy_s5, dy_conv, dx2b = _mix_out_bwd(dx2, p["w_out"], tm)
    grads["w_out"] = jnp.concatenate([_mm_tn(y_s5, dx2b, bf16, "dw_out_s5"), _mm_tn(y_conv, dx2b, bf16, "dw_out_conv")], axis=0)
    dy_lin, du_skip, dd8, grads["s5_w_glu"], dbg8 = _s5_read_bwd(
        _to_segments(dy_s5), y_lin, u_seg, p["s5_d"], p["s5_w_glu"], p["s5_b_glu"], tm)
    grads["s5_d"] = dd8.sum(axis=0, keepdims=True)
    grads["s5_b_glu"] = dbg8.sum(axis=0, keepdims=True)
    ds_re, ds_im, dcc_re, dcc_im = _s5_state_grad(dy_lin, flat(s_re), flat(s_im), cc_re, cc_im, ts)
    g_e_re, g_e_im = _scan_ends(seg3(ds_re), seg3(ds_im), a_re, -a_im, True, bi)
    g_re, g_im, da_re8, da_im8 = _rscan_states(seg3(ds_re), seg3(ds_im), a_re, -a_im, g_e_re, g_e_im, s_re, s_im, bi)
    du_seg, dbb_re, dbb_im = _s5_drive_bwd(flat(g_re), flat(g_im), u_seg, du_skip, bb_re, bb_im, ts)
    d_abar = lambda a8: a8.sum(axis=0).reshape(S5_GROUPS, S5_STATE)
    grads["s5_c_re"], grads["s5_c_im"], d_bbr, d_bbi = _diag_blocks([dcc_re, dcc_im, dbb_re, dbb_im])
    d_lr, d_li, d_ld, d_br, d_bi = _s5_params_bwd(*s5_in, d_abar(da_re8), d_abar(da_im8), d_bbr, d_bbi)
    grads["s5_lam_re"], grads["s5_lam_im"], grads["s5_log_dt"] = d_lr, d_li, d_ld.reshape(1, S5_GROUPS)
    grads["s5_b_re"], grads["s5_b_im"] = b_t(d_br), b_t(d_bi)
    dzc, dlg8, dlb8, dbd8 = _conv_bwd_norm(dy_conv, zc, p["conv_ln_g"], p["conv_ln_b"], avg, tm)
    grads["conv_ln_g"] = dlg8.sum(axis=0, keepdims=True)
    grads["conv_ln_b"] = dlb8.sum(axis=0, keepdims=True)
    grads["conv_b_dw"] = dbd8.sum(axis=0, keepdims=True)
    dv, dw8 = _conv_bwd_taps(dzc, v, w_dw, tm)
    grads["conv_w_dw"] = dw8.sum(axis=1)[:CONV_K]
    dx1, grads["mix_norm"], dub = _mix_in_bwd(dx2, x1, p["mix_norm"], _from_segments(du_seg), dv, p["w_in"], tm)
    grads["w_in"] = _mm_tn(h2, dub, bf16, "dw_in")

    dx0 = _ffn_block_bwd(dx1, x, p, "ffn1", saved1, tm, grads)
    return loss_terms, dx0, grads


MESH = pl.DeviceIdType.MESH
ANY = pl.BlockSpec(memory_space=pl.ANY)


def _place():
    return lax.axis_index("x"), lax.axis_index("y"), lax.axis_index("c")


def _all_gather(arrs, name):
    n = len(arrs)

    def body(*refs):
        ins, outs = refs[:n], refs[n:2 * n]
        send_sems, recv_sems, local_sems = refs[2 * n:]
        x, y, c = _place()
        me, sibling = (x, y, c), (x, y, 1 - c)
        chips = [(1 - x, y), (x, 1 - y), (1 - x, 1 - y)]

        def copy(a, k, block, to, src=None):
            px, py, pc = block
            dst = outs[a].at[4 * px + 2 * py + pc]
            return pltpu.make_async_remote_copy(
                src_ref=dst if src is None else src, dst_ref=dst, send_sem=send_sems.at[7 * a + k],
                recv_sem=recv_sems.at[7 * a + k], device_id=to, device_id_type=MESH)

        mine = [pltpu.make_async_copy(ins[a], outs[a].at[4 * x + 2 * y + c], local_sems.at[a]) for a in range(n)]
        for cp in mine:
            cp.start()
        first = []
        for a in range(n):
            first.append(copy(a, 0, me, sibling, src=ins[a]))
            first += [copy(a, 1 + j, me, (*chip, c), src=ins[a]) for j, chip in enumerate(chips)]
        for cp in first:
            cp.start()
        passed = [[copy(a, 4 + j, (*chip, c), sibling) for j, chip in enumerate(chips)] for a in range(n)]
        for j, chip in enumerate(chips):
            for a in range(n):
                copy(a, 1 + j, (*chip, c), me).wait_recv()
                passed[a][j].start()
        for a in range(n):
            copy(a, 0, sibling, me).wait_recv()
            for j, chip in enumerate(chips):
                copy(a, 4 + j, (*chip, 1 - c), me).wait_recv()
        for cp in first + [cp for row in passed for cp in row]:
            cp.wait_send()
        for cp in mine:
            cp.wait()

    return pl.pallas_call(
        body, name=name, in_specs=[ANY] * n, out_specs=[ANY] * n,
        out_shape=[S((N_DEV, *a.shape), a.dtype) for a in arrs],
        scratch_shapes=[pltpu.SemaphoreType.DMA((7 * n,)), pltpu.SemaphoreType.DMA((7 * n,)), pltpu.SemaphoreType.DMA((n,))],
    )(*arrs)


def _swap_with_sibling(gs, name):
    n = len(gs)

    def body(*refs):
        ins, outs = refs[:n], refs[n:2 * n]
        send_sems, recv_sems = refs[2 * n:]
        x, y, c = _place()
        copies = [pltpu.make_async_remote_copy(
            src_ref=ins[a].at[:, 1 - c], dst_ref=outs[a], send_sem=send_sems.at[a], recv_sem=recv_sems.at[a],
            device_id=(x, y, 1 - c), device_id_type=MESH) for a in range(n)]
        for cp in copies:
            cp.start()
        for cp in copies:
            cp.wait()

    return pl.pallas_call(
        body, name=name, in_specs=[ANY] * n, out_specs=[ANY] * n,
        out_shape=[S((N_CHIP, *g.shape[2:]), g.dtype) for g in gs],
        scratch_shapes=[pltpu.SemaphoreType.DMA((n,)), pltpu.SemaphoreType.DMA((n,))],
    )(*gs)


def _swap_with_chips(ps, name):
    n = len(ps)

    def body(*refs):
        ins, outs = refs[:n], refs[n:2 * n]
        send_sems, recv_sems = refs[2 * n:]
        x, y, c = _place()
        q = 2 * x + y
        peers = [(x, 1 - y), (1 - x, y), (1 - x, 1 - y)]
        copies = []
        for a in range(n):
            for j, (px, py) in enumerate(peers):
                copies.append(pltpu.make_async_remote_copy(
                    src_ref=ins[a].at[2 * px + py], dst_ref=outs[a].at[q], send_sem=send_sems.at[3 * a + j],
                    recv_sem=recv_sems.at[3 * a + j], device_id=(px, py, c), device_id_type=MESH))
        for cp in copies:
            cp.start()
        for a in range(n):
            for j, (px, py) in enumerate(peers):
                pltpu.make_async_remote_copy(
                    src_ref=ins[a].at[q], dst_ref=outs[a].at[2 * px + py], send_sem=send_sems.at[3 * a + j],
                    recv_sem=recv_sems.at[3 * a + j], device_id=(px, py, c), device_id_type=MESH).wait_recv()
        for cp in copies:
            cp.wait_send()

    return pl.pallas_call(
        body, name=name, in_specs=[ANY] * n, out_specs=[ANY] * n,
        out_shape=[S(p.shape, p.dtype) for p in ps],
        scratch_shapes=[pltpu.SemaphoreType.DMA((3 * n,)), pltpu.SemaphoreType.DMA((3 * n,))],
    )(*ps)


def _row_tile(rows, cols, itemsize):
    t = rows
    while t * cols * itemsize > (1 << 20) and t % 32 == 0:
        t //= 2
    return t


def _add_sibling(g4, st, core, name):
    _, _, R, C = g4.shape
    tr = _row_tile(R, C, 4)

    def body(c_ref, g_ref, s_ref, o_ref):
        o_ref[...] = (g_ref[...].astype(f32) + s_ref[...].astype(f32)).astype(bf16)

    return pl.pallas_call(
        body, name=name,
        grid_spec=pltpu.PrefetchScalarGridSpec(
            num_scalar_prefetch=1, grid=(N_CHIP, R // tr),
            in_specs=[pl.BlockSpec((None, None, tr, C), lambda q, i, c: (q, c[0], i, 0)),
                      pl.BlockSpec((None, tr, C), lambda q, i, c: (q, i, 0))],
            out_specs=pl.BlockSpec((None, tr, C), lambda q, i, c: (q, i, 0))),
        out_shape=S((N_CHIP, R, C), bf16),
        compiler_params=_cp("parallel", "parallel"),
    )(core, g4, st)


def _adamw(w, g, m, v):
    m = B1 * m + (1.0 - B1) * g
    v = B2 * v + (1.0 - B2) * (g * g)
    m_hat = m / (1.0 - B1 ** STEP)
    v_hat = v / (1.0 - B2 ** STEP)
    return -LR * (m_hat / (jnp.sqrt(v_hat) + ADAM_EPS) + WD * w), m, v


def _adam_sharded(w, m, v, part, got, slots, name):
    R, C = w.shape
    _, Rp, Cp = part.shape
    tr = _row_tile(R, Cp, 4) if Rp == R else R

    def body(s_ref, w_ref, m_ref, v_ref, p_ref, a_ref, b_ref, c_ref, g_out, d_out, m_out, v_out):
        g = p_ref[...].astype(f32) + a_ref[...].astype(f32) + b_ref[...].astype(f32) + c_ref[...].astype(f32)
        g = g[:, :C]
        g_out[...] = g
        d_out[...], m_out[...], v_out[...] = _adamw(w_ref[...], g, m_ref[...], v_ref[...])

    shard = pl.BlockSpec((tr, C), lambda i, s: (i, 0))
    slot = lambda k: pl.BlockSpec((None, tr, Cp), lambda i, s: (s[k], i, 0))
    return pl.pallas_call(
        body, name=name,
        grid_spec=pltpu.PrefetchScalarGridSpec(
            num_scalar_prefetch=1, grid=(R // tr,),
            in_specs=[shard, shard, shard, slot(0), slot(1), slot(2), slot(3)],
            out_specs=[shard] * 4),
        out_shape=[S((R, C), f32)] * 4,
        compiler_params=_cp("parallel"),
    )(slots, w, m, v, part, got, got, got)


def _adam_replicated(w, m, v, gathered, name):
    R = w.shape[0]

    def body(w_ref, m_ref, v_ref, g_ref, g_out, d_out, m_out, v_out):
        g = g_ref[0]
        for d in range(1, N_DEV):
            g = g + g_ref[d]
        g_out[...] = g
        d_out[...], m_out[...], v_out[...] = _adamw(w_ref[...], g, m_ref[...], v_ref[...])

    return pl.pallas_call(body, name=name, out_shape=[S((R, 128), f32)] * 4,
                          compiler_params=pltpu.CompilerParams(vmem_limit_bytes=VMEM_LIMIT))(w, m, v, gathered)


WEIGHTS = ["ffn1_norm", "ffn1_w_gate", "ffn1_w_up", "ffn1_w_down", "mix_norm", "w_in", "s5_lam_re", "s5_lam_im", "s5_log_dt",
           "s5_b_re", "s5_b_im", "s5_c_re", "s5_c_im", "s5_d", "s5_w_glu", "s5_b_glu", "conv_w_dw", "conv_b_dw", "conv_ln_g",
           "conv_ln_b", "w_out", "ffn2_norm", "ffn2_w_gate", "ffn2_w_up", "ffn2_w_down", "final_norm"]
SHARDED = ["ffn1_w_gate", "ffn1_w_up", "ffn1_w_down", "w_in", "s5_w_glu", "conv_w_dw", "w_out", "ffn2_w_gate", "ffn2_w_up",
           "ffn2_w_down"]
REPLICATED = [n for n in WEIGHTS if n not in SHARDED]
PACK = 8 * 128


def _is_up(n):
    return n.endswith("w_gate") or n.endswith("w_up")


def _shard_to_wire(n, w):
    if _is_up(n):
        w = jnp.pad(w, ((0, 0), (0, FF_SHARD_PAD - FF_SHARD)))
    elif n.endswith("w_down"):
        w = jnp.pad(w, ((0, FF_SHARD_PAD - FF_SHARD), (0, 0)))
    elif n == "conv_w_dw":
        return jnp.pad(w, ((0, CONV_HALO - CONV_K), (0, 0)))
    return w.astype(bf16)


def _gathered_to_full(n, g):
    if n == "conv_w_dw":
        return g.transpose(1, 0, 2).reshape(CONV_HALO, CONV_WIDTH)[:CONV_K]
    if _is_up(n) or n == "w_in":
        return g.transpose(1, 0, 2).reshape(g.shape[1], N_DEV * g.shape[2])
    return g.reshape(N_DEV * g.shape[1], g.shape[2])


def _grad_to_blocks(n, g):
    if n == "conv_w_dw":
        g = jnp.pad(g, ((0, CONV_HALO - CONV_K), (0, 0)))
    g = g.astype(bf16)
    if _is_up(n) or n in ("w_in", "conv_w_dw"):
        g = g.reshape(g.shape[0], N_DEV, g.shape[1] // N_DEV).transpose(1, 0, 2)
    else:
        g = g.reshape(N_DEV, g.shape[0] // N_DEV, g.shape[1])
    return g.reshape(N_CHIP, 2, *g.shape[1:])


def _pack(parts):
    out = []
    for a in parts:
        a = a.reshape(-1)
        out.append(jnp.pad(a, (0, -a.size % PACK)))
    return jnp.concatenate(out).reshape(-1, 128)


def _unpack(packed, like):
    out, at = [], 0
    flat = packed.reshape(-1)
    for a in like:
        out.append(flat[at:at + a.size].reshape(a.shape))
        at += a.size + (-a.size % PACK)
    return out


def kernel(x, ffn1_norm, ffn1_w_gate, ffn1_w_up, ffn1_w_down, mix_norm, w_in, s5_lam_re, s5_lam_im, s5_log_dt, s5_b_re, s5_b_im, s5_c_re, s5_c_im, s5_d, s5_w_glu, s5_b_glu, conv_w_dw, conv_b_dw, conv_ln_g, conv_ln_b, w_out, ffn2_norm, ffn2_w_gate, ffn2_w_up, ffn2_w_down, final_norm, loss_target, m_ffn1_norm, m_ffn1_w_gate, m_ffn1_w_up, m_ffn1_w_down, m_mix_norm, m_w_in, m_s5_lam_re, m_s5_lam_im, m_s5_log_dt, m_s5_b_re, m_s5_b_im, m_s5_c_re, m_s5_c_im, m_s5_d, m_s5_w_glu, m_s5_b_glu, m_conv_w_dw, m_conv_b_dw, m_conv_ln_g, m_conv_ln_b, m_w_out, m_ffn2_norm, m_ffn2_w_gate, m_ffn2_w_up, m_ffn2_w_down, m_final_norm, v_ffn1_norm, v_ffn1_w_gate, v_ffn1_w_up, v_ffn1_w_down, v_mix_norm, v_w_in, v_s5_lam_re, v_s5_lam_im, v_s5_log_dt, v_s5_b_re, v_s5_b_im, v_s5_c_re, v_s5_c_im, v_s5_d, v_s5_w_glu, v_s5_b_glu, v_conv_w_dw, v_conv_b_dw, v_conv_ln_g, v_conv_ln_b, v_w_out, v_ffn2_norm, v_ffn2_w_gate, v_ffn2_w_up, v_ffn2_w_down, v_final_norm):
    args = locals()
    w = {n: args[n] for n in WEIGHTS}
    m = {n: args["m_" + n] for n in WEIGHTS}
    v = {n: args["v_" + n] for n in WEIGHTS}
    shard2d = lambda a: a.reshape(a.shape[-2:])
    xq, yq, cq = _place()

    wire = [_shard_to_wire(n, shard2d(w[n])) for n in SHARDED]
    gathered = _all_gather(wire, "gather_weights")
    p = {n: _gathered_to_full(n, g) for n, g in zip(SHARDED, gathered)}
    for n in REPLICATED:
        p[n] = w[n].reshape(w[n].shape[1:]) if w[n].ndim >= 3 else w[n]

    loss_terms, dx, grads = _local_step(x[0], loss_target[0], p)

    blocks = [_grad_to_blocks(n, grads[n]) for n in SHARDED]
    from_sibling = _swap_with_sibling(blocks, "reduce_sibling")
    core = jnp.reshape(cq, (1,)).astype(jnp.int32)
    partial = [_add_sibling(b, s, core, "reduce_add_" + n) for n, b, s in zip(SHARDED, blocks, from_sibling)]
    from_chips = _swap_with_chips(partial, "reduce_chips")
    q = 2 * xq + yq
    slots = jnp.stack([q, q ^ 1, q ^ 2, q ^ 3]).astype(jnp.int32)
    out = {}
    for n, part, got in zip(SHARDED, partial, from_chips):
        rows = part.shape[1] if n == "conv_w_dw" else w[n].shape[-2]
        fit = lambda a: jnp.pad(shard2d(a), ((0, rows - a.shape[-2]), (0, 0)))
        res = _adam_sharded(fit(w[n]), fit(m[n]), fit(v[n]), part, got, slots, "adam_" + n)
        out[n] = [r[:w[n].shape[-2]].reshape(w[n].shape) for r in res]

    mine = _pack([loss_terms.sum().reshape(1)] + [grads[n] for n in REPLICATED])
    everyone = _all_gather([mine], "gather_small")[0]
    zero = jnp.zeros((1,), f32)
    res = _adam_replicated(_pack([zero] + [w[n] for n in REPLICATED]), _pack([zero] + [m[n] for n in REPLICATED]),
                           _pack([zero] + [v[n] for n in REPLICATED]), everyone, "adam_replicated")
    like = [zero] + [w[n] for n in REPLICATED]
    unpacked = [_unpack(r, like) for r in res]
    loss = unpacked[0][0].reshape(())
    for i, n in enumerate(REPLICATED):
        out[n] = [u[1 + i] for u in unpacked]

    return (loss, dx.reshape(x.shape), *[out[n][0] for n in WEIGHTS], *[out[n][1] for n in WEIGHTS],
            *[out[n][2] for n in WEIGHTS], *[out[n][3] for n in WEIGHTS])
```

```python
import functools

import jax
import jax.numpy as jnp
from jax import lax
from jax.experimental import pallas as pl
from jax.experimental.pallas import tpu as pltpu

f32 = jnp.float32
bf16 = jnp.bfloat16
S = jax.ShapeDtypeStruct

N_DEV = 8
N_CHIP = 4
D_MODEL = 1024
D_FF = 2816
FF_SHARD = D_FF // N_DEV
FF_SHARD_PAD = 384
FF_PAD = FF_SHARD_PAD * N_DEV
S5_WIDTH = 512
S5_GROUPS = 32
S5_GROUP_CH = 16
S5_STATE = 64
S5_LANES = S5_GROUPS * S5_STATE
CONV_WIDTH = 512
CONV_K = 31
CONV_HALO = 32
CONV_HEAD = 64
CONV_ROWS = 32
IN_COLS = S5_WIDTH + 2 * CONV_WIDTH
SEGMENTS = 8
SCAN_LANES = 512
EPS = 1e-6
LR, B1, B2, ADAM_EPS, WD, STEP = 0.001, 0.9, 0.999, 1e-08, 0.01, 10
VMEM_LIMIT = 56 * 1024 * 1024

NN = (((1,), (0,)), ((), ()))
NT = (((1,), (1,)), ((), ()))
TN = (((0,), (0,)), ((), ()))


def _dot(a, b, dims=NN):
    return lax.dot_general(a, b, dims, preferred_element_type=f32)


def _cp(*sem):
    return pltpu.CompilerParams(dimension_semantics=sem, vmem_limit_bytes=VMEM_LIMIT)


def _rms(x, g):
    return x * lax.rsqrt(jnp.mean(x * x, axis=-1, keepdims=True) + EPS) * g


def _rms_bwd(x, g, dh):
    _, vjp = jax.vjp(_rms, x, g)
    return vjp(dh)


def _sigmoid(x):
    return 1.0 / (1.0 + jnp.exp(-x))


def _gelu(x):
    return 0.5 * x * (1.0 + jnp.tanh(0.7978845608028654 * (x + 0.044715 * x * x * x)))


def _rows8(x):
    t, c = x.shape
    return x.reshape(t // 8, 8, c).sum(axis=0)


def _full(shape):
    return pl.BlockSpec(shape, lambda *_: (0,) * len(shape))


def _resident(shape):
    return pl.BlockSpec(shape, lambda *_: (0,) * len(shape), pipeline_mode=pl.Buffered(1))


def _ffn_up(x, g, wg, wu, tm, tn, tag, ride=()):
    L = x.shape[0]

    def body(x_ref, g_ref, wg_ref, wu_ref, h_ref, dadg_ref, dadu_ref, a_ref):
        h = _rms(x_ref[...], g_ref[...]).astype(bf16)
        h_ref[...] = h
        for j in range(FF_PAD // tn):
            cols = slice(j * tn, (j + 1) * tn)
            gate = _dot(h, wg_ref[:, cols])
            up = _dot(h, wu_ref[:, cols])
            sig = _sigmoid(gate)
            silu = gate * sig
            dadg_ref[:, cols] = (up * (sig + silu * (1.0 - sig))).astype(bf16)
            dadu_ref[:, cols] = silu.astype(bf16)
            a_ref[:, cols] = (silu * up).astype(bf16)

    row = pl.BlockSpec((tm, D_MODEL), lambda i: (i, 0))
    wide = pl.BlockSpec((tm, FF_PAD), lambda i: (i, 0))
    return _pallas(
        body, ride=ride, name=tag + "_up", grid=(L // tm,),
        in_specs=[row, _full((1, D_MODEL)), _resident((D_MODEL, FF_PAD)), _resident((D_MODEL, FF_PAD))],
        out_specs=[row, wide, wide, wide],
        out_shape=[S((L, D_MODEL), bf16)] + [S((L, FF_PAD), bf16)] * 3,
        compiler_params=_cp("parallel"),
    )(x, g, wg, wu)


def _ffn_down(x, a, wd, tm, tag):
    L = x.shape[0]

    def body(x_ref, a_ref, wd_ref, o_ref):
        o_ref[...] = x_ref[...] + 0.5 * _dot(a_ref[...], wd_ref[...])

    return pl.pallas_call(
        body, name=tag + "_down", grid=(L // tm,),
        in_specs=[pl.BlockSpec((tm, D_MODEL), lambda i: (i, 0)), pl.BlockSpec((tm, FF_PAD), lambda i: (i, 0)),
                  _resident((FF_PAD, D_MODEL))],
        out_specs=pl.BlockSpec((tm, D_MODEL), lambda i: (i, 0)),
        out_shape=S((L, D_MODEL), f32),
        compiler_params=_cp("parallel"),
    )(x, a, wd)


def _ffn_bwd_act(dxo, wd, dadg, dadu, tm, tn, tag, ride=()):
    L = dxo.shape[0]

    def body(dx_ref, wd_ref, dadg_ref, dadu_ref, dgate_ref, dup_ref, dxh_ref):
        dxh = (0.5 * dx_ref[...]).astype(bf16)
        dxh_ref[...] = dxh
        for j in range(FF_PAD // tn):
            cols = slice(j * tn, (j + 1) * tn)
            da = _dot(dxh, wd_ref[cols, :], NT)
            dgate_ref[:, cols] = (da * dadg_ref[:, cols].astype(f32)).astype(bf16)
            dup_ref[:, cols] = (da * dadu_ref[:, cols].astype(f32)).astype(bf16)

    row = pl.BlockSpec((tm, D_MODEL), lambda i: (i, 0))
    wide = pl.BlockSpec((tm, FF_PAD), lambda i: (i, 0))
    return _pallas(
        body, ride=ride, name=tag + "_bwd_act", grid=(L // tm,),
        in_specs=[row, _resident((FF_PAD, D_MODEL)), wide, wide],
        out_specs=[wide, wide, row],
        out_shape=[S((L, FF_PAD), bf16), S((L, FF_PAD), bf16), S((L, D_MODEL), bf16)],
        compiler_params=_cp("parallel"),
    )(dxo, wd, dadg, dadu)


def _ffn_bwd_in(dxo, x, g, dgate, dup, wg, wu, tm, tag, ride=()):
    L = x.shape[0]

    def body(dxo_ref, x_ref, g_ref, dgate_ref, dup_ref, wg_ref, wu_ref, dx_ref, dg_ref):
        @pl.when(pl.program_id(0) == 0)
        def _():
            dg_ref[...] = jnp.zeros_like(dg_ref)

        dh = _dot(dgate_ref[...], wg_ref[...], NT) + _dot(dup_ref[...], wu_ref[...], NT)
        dx, dg = _rms_bwd(x_ref[...], g_ref[...], dh)
        dx_ref[...] = dxo_ref[...] + dx
        dg_ref[...] += dg

    row = pl.BlockSpec((tm, D_MODEL), lambda i: (i, 0))
    wide = pl.BlockSpec((tm, FF_PAD), lambda i: (i, 0))
    return _pallas(
        body, ride=ride, name=tag + "_bwd_in", grid=(L // tm,),
        in_specs=[row, row, _full((1, D_MODEL)), wide, wide, _resident((D_MODEL, FF_PAD)), _resident((D_MODEL, FF_PAD))],
        out_specs=[row, _full((1, D_MODEL))],
        out_shape=[S((L, D_MODEL), f32), S((1, D_MODEL), f32)],
        compiler_params=_cp("arbitrary"),
    )(dxo, x, g, dgate, dup, wg, wu)


def _mm_tn(a, b, out_dtype, name, tm=512, tn=1024, ride=()):
    L, M = a.shape
    N = b.shape[1]
    tm, tn = min(tm, M), min(tn, N)
    while N % tn:
        tn //= 2

    def body(a_ref, b_ref, o_ref):
        o_ref[...] = _dot(a_ref[...].astype(bf16), b_ref[...].astype(bf16), TN).astype(out_dtype)

    return _pallas(
        body, ride=ride, name=name, grid=(M // tm, N // tn),
        in_specs=[pl.BlockSpec((L, tm), lambda i, j: (0, i)), pl.BlockSpec((L, tn), lambda i, j: (0, j))],
        out_specs=pl.BlockSpec((tm, tn), lambda i, j: (i, j)),
        out_shape=S((M, N), out_dtype),
        compiler_params=_cp("parallel", "parallel"),
    )(a, b)


def _mix_in(x, g, w_in, tm):
    L = x.shape[0]

    def body(x_ref, g_ref, w_ref, h_ref, us_ref, v_ref):
        h = _rms(x_ref[...], g_ref[...]).astype(bf16)
        h_ref[...] = h
        u = _dot(h, w_ref[...])
        us_ref[...] = u[:, :S5_WIDTH]
        v_ref[...] = u[:, S5_WIDTH:]

    row = lambda c: pl.BlockSpec((tm, c), lambda i: (i, 0))
    return pl.pallas_call(
        body, name="mix_in", grid=(L // tm,),
        in_specs=[row(D_MODEL), _full((1, D_MODEL)), _full((D_MODEL, IN_COLS))],
        out_specs=[row(D_MODEL), row(S5_WIDTH), row(2 * CONV_WIDTH)],
        out_shape=[S((L, D_MODEL), bf16), S((L, S5_WIDTH), f32), S((L, 2 * CONV_WIDTH), f32)],
        compiler_params=_cp("parallel"),
    )(x, g, w_in)


def _mix_in_bwd(dxo, x, g, du_s5, dv, w_in, tm):
    L = x.shape[0]

    def body(dxo_ref, x_ref, g_ref, dus_ref, dv_ref, w_ref, dx_ref, dg_ref, dub_ref):
        @pl.when(pl.program_id(0) == 0)
        def _():
            dg_ref[...] = jnp.zeros_like(dg_ref)

        dus = dus_ref[...].astype(bf16)
        dvb = dv_ref[...].astype(bf16)
        dub_ref[:, :S5_WIDTH] = dus
        dub_ref[:, S5_WIDTH:] = dvb
        dh = _dot(dus, w_ref[:, :S5_WIDTH], NT) + _dot(dvb, w_ref[:, S5_WIDTH:], NT)
        dx, dg = _rms_bwd(x_ref[...], g_ref[...], dh)
        dx_ref[...] = dxo_ref[...] + dx
        dg_ref[...] += dg

    row = lambda c: pl.BlockSpec((tm, c), lambda i: (i, 0))
    return pl.pallas_call(
        body, name="mix_in_bwd", grid=(L // tm,),
        in_specs=[row(D_MODEL), row(D_MODEL), _full((1, D_MODEL)), row(S5_WIDTH), row(2 * CONV_WIDTH),
                  _full((D_MODEL, IN_COLS))],
        out_specs=[row(D_MODEL), _full((1, D_MODEL)), row(IN_COLS)],
        out_shape=[S((L, D_MODEL), f32), S((1, D_MODEL), f32), S((L, IN_COLS), bf16)],
        compiler_params=_cp("arbitrary"),
    )(dxo, x, g, du_s5, dv, w_in)


def _mix_out(x, y_s5, y_conv, w_out, tm):
    L = x.shape[0]

    def body(x_ref, ys_ref, yc_ref, w_ref, o_ref):
        o_ref[...] = x_ref[...] + _dot(ys_ref[...], w_ref[:S5_WIDTH, :]) + _dot(yc_ref[...], w_ref[S5_WIDTH:, :])

    row = lambda c: pl.BlockSpec((tm, c), lambda i: (i, 0))
    return pl.pallas_call(
        body, name="mix_out", grid=(L // tm,),
        in_specs=[row(D_MODEL), row(S5_WIDTH), row(CONV_WIDTH), _full((D_MODEL, D_MODEL))],
        out_specs=row(D_MODEL), out_shape=S((L, D_MODEL), f32),
        compiler_params=_cp("parallel"),
    )(x, y_s5, y_conv, w_out)


def _mix_out_bwd(dx, w_out, tm, ride=()):
    L = dx.shape[0]

    def body(dx_ref, w_ref, dys_ref, dyc_ref, dxb_ref):
        dxb = dx_ref[...].astype(bf16)
        dxb_ref[...] = dxb
        dys_ref[...] = _dot(dxb, w_ref[:S5_WIDTH, :], NT)
        dyc_ref[...] = _dot(dxb, w_ref[S5_WIDTH:, :], NT)

    row = lambda c: pl.BlockSpec((tm, c), lambda i: (i, 0))
    return _pallas(
        body, ride=ride, name="mix_out_bwd", grid=(L // tm,),
        in_specs=[row(D_MODEL), _full((D_MODEL, D_MODEL))],
        out_specs=[row(S5_WIDTH), row(CONV_WIDTH), row(D_MODEL)],
        out_shape=[S((L, S5_WIDTH), f32), S((L, CONV_WIDTH), f32), S((L, D_MODEL), bf16)],
        compiler_params=_cp("parallel"),
    )(dx, w_out)


def _s5_discretise(lam_re, lam_im, log_dt, b_re, b_im):
    dt = jnp.exp(log_dt)
    mag = jnp.exp(lam_re * dt)
    abar_re = mag * jnp.cos(lam_im * dt)
    abar_im = mag * jnp.sin(lam_im * dt)
    den = lam_re * lam_re + lam_im * lam_im
    num_re = abar_re - 1.0
    f_re = ((num_re * lam_re + abar_im * lam_im) / den)[:, None, :]
    f_im = ((abar_im * lam_re - num_re * lam_im) / den)[:, None, :]
    return abar_re, abar_im, f_re * b_re - f_im * b_im, f_re * b_im + f_im * b_re


def _s5_params(lam_re, lam_im, log_dt, b_re, b_im):
    def body(lr, li, ld, br, bi, ar_ref, ai_ref, bbr_ref, bbi_ref):
        ar, ai, bbr, bbi = _s5_discretise(lr[...], li[...], ld[...], br[...], bi[...])
        ar_ref[...], ai_ref[...], bbr_ref[...], bbi_ref[...] = ar, ai, bbr, bbi

    gp = S((S5_GROUPS, S5_STATE), f32)
    gcp = S((S5_GROUPS, S5_GROUP_CH, S5_STATE), f32)
    return pl.pallas_call(body, name="s5_params", out_shape=[gp, gp, gcp, gcp])(lam_re, lam_im, log_dt, b_re, b_im)


def _s5_params_bwd(lam_re, lam_im, log_dt, b_re, b_im, d_ar, d_ai, d_bbr, d_bbi):
    def body(lr, li, ld, br, bi, car, cai, cbr, cbi, o_lr, o_li, o_ld, o_br, o_bi):
        _, vjp = jax.vjp(_s5_discretise, lr[...], li[...], ld[...], br[...], bi[...])
        o_lr[...], o_li[...], o_ld[...], o_br[...], o_bi[...] = vjp((car[...], cai[...], cbr[...], cbi[...]))

    gp = S((S5_GROUPS, S5_STATE), f32)
    gcp = S((S5_GROUPS, S5_GROUP_CH, S5_STATE), f32)
    return pl.pallas_call(body, name="s5_params_bwd", out_shape=[gp, gp, S((S5_GROUPS, 1), f32), gcp, gcp])(
        lam_re, lam_im, log_dt, b_re, b_im, d_ar, d_ai, d_bbr, d_bbi)


def _cmul(ar, ai, br, bi):
    return ar * br - ai * bi, ar * bi + ai * br


def _scan_specs(ni, bi, reverse):
    nb = ni // bi
    blk = pl.BlockSpec((bi, SEGMENTS, SCAN_LANES), (lambda c, j: (nb - 1 - j, 0, c)) if reverse else (lambda c, j: (j, 0, c)))
    vec = pl.BlockSpec((1, SCAN_LANES), lambda c, j: (0, c))
    tile = pl.BlockSpec((SEGMENTS, SCAN_LANES), lambda c, j: (0, c))
    return nb, blk, vec, tile


def _scan_ends(bu_re, bu_im, a_re, a_im, reverse, bi):
    ni = bu_re.shape[0]
    nb, blk, vec, tile = _scan_specs(ni, bi, reverse)

    def body(br_ref, bi_ref, ar_ref, ai_ref, er_ref, ei_ref):
        @pl.when(pl.program_id(1) == 0)
        def _():
            er_ref[...] = jnp.zeros_like(er_ref)
            ei_ref[...] = jnp.zeros_like(ei_ref)

        ar = jnp.broadcast_to(ar_ref[...], (SEGMENTS, SCAN_LANES))
        ai = jnp.broadcast_to(ai_ref[...], (SEGMENTS, SCAN_LANES))

        def step(n, c):
            i = (bi - 1 - n) if reverse else n
            pr, pi = _cmul(ar, ai, c[0], c[1])
            return pr + br_ref[i], pi + bi_ref[i]

        er_ref[...], ei_ref[...] = lax.fori_loop(0, bi, step, (er_ref[...], ei_ref[...]), unroll=4)

    out = S((SEGMENTS, S5_LANES), f32)
    return pl.pallas_call(
        body, name="s5_rscan_ends" if reverse else "s5_scan_ends", grid=(S5_LANES // SCAN_LANES, nb),
        in_specs=[blk, blk, vec, vec], out_specs=[tile, tile], out_shape=[out, out],
        compiler_params=_cp("parallel", "arbitrary"),
    )(bu_re, bu_im, a_re, a_im)


def _segment_starts(er, ei, ar, ai, steps, reverse):
    pr, pi = ar, ai
    n = 1
    while n < steps:
        pr, pi = _cmul(pr, pi, pr, pi)
        n *= 2
    assert n == steps
    row = lax.broadcasted_iota(jnp.int32, (SEGMENTS, SCAN_LANES), 0)
    hr = jnp.zeros((1, SCAN_LANES), f32)
    hi = jnp.zeros((1, SCAN_LANES), f32)
    out_r = jnp.zeros((SEGMENTS, SCAN_LANES), f32)
    out_i = jnp.zeros((SEGMENTS, SCAN_LANES), f32)
    order = range(SEGMENTS - 1, 0, -1) if reverse else range(0, SEGMENTS - 1)
    for r in order:
        qr, qi = _cmul(pr, pi, hr, hi)
        hr, hi = qr + er[r:r + 1, :], qi + ei[r:r + 1, :]
        nxt = r - 1 if reverse else r + 1
        out_r = jnp.where(row == nxt, hr, out_r)
        out_i = jnp.where(row == nxt, hi, out_i)
    return out_r, out_i


def _scan_states(bu_re, bu_im, a_re, a_im, e_re, e_im, bi, ride=()):
    ni = bu_re.shape[0]
    nb, blk, vec, tile = _scan_specs(ni, bi, False)

    def body(br_ref, bi_ref, ar_ref, ai_ref, er_ref, ei_ref, sr_ref, si_ref, cr_ref, ci_ref):
        @pl.when(pl.program_id(1) == 0)
        def _():
            cr_ref[...], ci_ref[...] = _segment_starts(er_ref[...], ei_ref[...], ar_ref[...], ai_ref[...], ni, False)

        ar = jnp.broadcast_to(ar_ref[...], (SEGMENTS, SCAN_LANES))
        ai = jnp.broadcast_to(ai_ref[...], (SEGMENTS, SCAN_LANES))

        def step(i, c):
            pr, pi = _cmul(ar, ai, c[0], c[1])
            nr, nim = pr + br_ref[i], pi + bi_ref[i]
            sr_ref[i] = nr
            si_ref[i] = nim
            return nr, nim

        cr_ref[...], ci_ref[...] = lax.fori_loop(0, bi, step, (cr_ref[...], ci_ref[...]), unroll=4)

    return _pallas(
        body, ride=ride, name="s5_scan_states", grid=(S5_LANES // SCAN_LANES, nb),
        in_specs=[blk, blk, vec, vec, tile, tile],
        out_specs=[blk, blk], out_shape=[S(bu_re.shape, f32)] * 2,
        scratch_shapes=[pltpu.VMEM((SEGMENTS, SCAN_LANES), f32)] * 2,
        compiler_params=_cp("parallel", "arbitrary"),
    )(bu_re, bu_im, a_re, a_im, e_re, e_im)


def _rscan_states(ds_re, ds_im, a_re, a_im, e_re, e_im, s_re, s_im, bi, ride=()):
    ni = ds_re.shape[0]
    nb, blk, vec, tile = _scan_specs(ni, bi, True)

    def body(dr_ref, di_ref, ar_ref, ai_ref, er_ref, ei_ref, sr_ref, si_ref, pr_ref, pi_ref, lr_ref, li_ref,
             gr_ref, gi_ref, dar_ref, dai_ref, cr_ref, ci_ref):
        j = pl.program_id(1)

        @pl.when(j == 0)
        def _():
            cr_ref[...], ci_ref[...] = _segment_starts(er_ref[...], ei_ref[...], ar_ref[...], ai_ref[...], ni, True)
            dar_ref[...] = jnp.zeros_like(dar_ref)
            dai_ref[...] = jnp.zeros_like(dai_ref)

        ar = jnp.broadcast_to(ar_ref[...], (SEGMENTS, SCAN_LANES))
        ai = jnp.broadcast_to(ai_ref[...], (SEGMENTS, SCAN_LANES))

        def step(n, c):
            i = bi - 1 - n
            gr, gi, accr, acci = c
            qr, qi = _cmul(ar, ai, gr, gi)
            gr, gi = qr + dr_ref[i], qi + di_ref[i]
            gr_ref[i] = gr
            gi_ref[i] = gi
            sr, si = sr_ref[i - 1], si_ref[i - 1]
            return gr, gi, accr + (gr * sr + gi * si), acci + (gi * sr - gr * si)

        gr, gi, accr, acci = lax.fori_loop(0, bi - 1, step, (cr_ref[...], ci_ref[...], dar_ref[...], dai_ref[...]), unroll=3)
        qr, qi = _cmul(ar, ai, gr, gi)
        gr, gi = qr + dr_ref[0], qi + di_ref[0]
        gr_ref[0] = gr
        gi_ref[0] = gi
        cr_ref[...], ci_ref[...] = gr, gi
        row = lax.broadcasted_iota(jnp.int32, (SEGMENTS, SCAN_LANES), 0)
        first = j == nb - 1
        wrap_r = jnp.where(row == 0, 0.0, pltpu.roll(lr_ref[0], 1, 0))
        wrap_i = jnp.where(row == 0, 0.0, pltpu.roll(li_ref[0], 1, 0))
        sr = jnp.where(first, wrap_r, pr_ref[0])
        si = jnp.where(first, wrap_i, pi_ref[0])
        dar_ref[...] = accr + gr * sr + gi * si
        dai_ref[...] = acci + gi * sr - gr * si

    prev = pl.BlockSpec((1, SEGMENTS, SCAN_LANES), lambda c, j: (jnp.maximum((nb - 1 - j) * bi - 1, 0), 0, c))
    last = pl.BlockSpec((1, SEGMENTS, SCAN_LANES), lambda c, j: (ni - 1, 0, c))
    return _pallas(
        body, ride=ride, name="s5_rscan_states", grid=(S5_LANES // SCAN_LANES, nb),
        in_specs=[blk, blk, vec, vec, tile, tile, blk, blk, prev, prev, last, last],
        out_specs=[blk, blk, tile, tile],
        out_shape=[S(ds_re.shape, f32)] * 2 + [S((SEGMENTS, S5_LANES), f32)] * 2,
        scratch_shapes=[pltpu.VMEM((SEGMENTS, SCAN_LANES), f32)] * 2,
        compiler_params=_cp("parallel", "arbitrary"),
    )(ds_re, ds_im, a_re, a_im, e_re, e_im, s_re, s_im, s_re, s_im, s_re, s_im)


def _s5_drive(u, bb_re, bb_im, tm):
    L = u.shape[0]

    def body(u_ref, br_ref, bi_ref, or_ref, oi_ref):
        ub = u_ref[...].astype(bf16)
        or_ref[...] = _dot(ub, br_ref[...])
        oi_ref[...] = _dot(ub, bi_ref[...])

    big = pl.BlockSpec((tm, S5_LANES), lambda i: (i, 0))
    return pl.pallas_call(
        body, name="s5_drive", grid=(L // tm,),
        in_specs=[pl.BlockSpec((tm, S5_WIDTH), lambda i: (i, 0)), _full((S5_WIDTH, S5_LANES)), _full((S5_WIDTH, S5_LANES))],
        out_specs=[big, big], out_shape=[S((L, S5_LANES), f32)] * 2,
        compiler_params=_cp("parallel"),
    )(u, bb_re, bb_im)


def _s5_read(s_re, s_im, u, c_re, c_im, d_skip, w_glu, b_glu, tm, ride=()):
    L = u.shape[0]

    def body(sr_ref, si_ref, u_ref, cr_ref, ci_ref, d_ref, w_ref, b_ref, yl_ref, o_ref):
        yl = _dot(sr_ref[...].astype(bf16), cr_ref[...], NT) - _dot(si_ref[...].astype(bf16), ci_ref[...], NT)
        yl_ref[...] = yl
        y = _gelu(yl + d_ref[...] * u_ref[...])
        z = _dot(y.astype(bf16), w_ref[...]) + b_ref[...]
        o_ref[...] = (y * _sigmoid(z)).astype(bf16)

    big = pl.BlockSpec((tm, S5_LANES), lambda i: (i, 0))
    row = pl.BlockSpec((tm, S5_WIDTH), lambda i: (i, 0))
    vec = _full((1, S5_WIDTH))
    return _pallas(
        body, ride=ride, name="s5_read", grid=(L // tm,),
        in_specs=[big, big, row, _full((S5_WIDTH, S5_LANES)), _full((S5_WIDTH, S5_LANES)), vec,
                  _full((S5_WIDTH, S5_WIDTH)), vec],
        out_specs=[row, row], out_shape=[S((L, S5_WIDTH), f32), S((L, S5_WIDTH), bf16)],
        compiler_params=_cp("parallel"),
    )(s_re, s_im, u, c_re, c_im, d_skip, w_glu, b_glu)


def _s5_read_bwd(dout, y_lin, u, d_skip, w_glu, b_glu, tm):
    L = u.shape[0]

    def body(do_ref, yl_ref, u_ref, d_ref, w_ref, b_ref, dyl_ref, du_ref, dd_ref, dw_ref, db_ref):
        @pl.when(pl.program_id(0) == 0)
        def _():
            dd_ref[...] = jnp.zeros_like(dd_ref)
            dw_ref[...] = jnp.zeros_like(dw_ref)
            db_ref[...] = jnp.zeros_like(db_ref)

        u, d, dout = u_ref[...], d_ref[...], do_ref[...]
        y, gelu_vjp = jax.vjp(_gelu, yl_ref[...] + d * u)
        yb = y.astype(bf16)
        sig = _sigmoid(_dot(yb, w_ref[...]) + b_ref[...])
        dz = dout * y * sig * (1.0 - sig)
        dzb = dz.astype(bf16)
        dy = dout * sig + _dot(dzb, w_ref[...], NT)
        (dyp,) = gelu_vjp(dy)
        dyl_ref[...] = dyp.astype(bf16)
        du_ref[...] = d * dyp
        dd_ref[...] += _rows8(dyp * u)
        db_ref[...] += _rows8(dz)
        dw_ref[...] += _dot(yb, dzb, TN)

    row = pl.BlockSpec((tm, S5_WIDTH), lambda i: (i, 0))
    vec = _full((1, S5_WIDTH))
    part = _full((8, S5_WIDTH))
    return pl.pallas_call(
        body, name="s5_read_bwd", grid=(L // tm,),
        in_specs=[row, row, row, vec, _full((S5_WIDTH, S5_WIDTH)), vec],
        out_specs=[row, row, part, _full((S5_WIDTH, S5_WIDTH)), part],
        out_shape=[S((L, S5_WIDTH), bf16), S((L, S5_WIDTH), f32), S((8, S5_WIDTH), f32),
                   S((S5_WIDTH, S5_WIDTH), f32), S((8, S5_WIDTH), f32)],
        compiler_params=_cp("arbitrary"),
    )(dout, y_lin, u, d_skip, w_glu, b_glu)


def _s5_state_grad(dy_lin, s_re, s_im, c_re, c_im, tm, ride=()):
    L = dy_lin.shape[0]

    def body(dy_ref, sr_ref, si_ref, cr_ref, ci_ref, dsr_ref, dsi_ref, dcr_ref, dci_ref):
        @pl.when(pl.program_id(0) == 0)
        def _():
            dcr_ref[...] = jnp.zeros_like(dcr_ref)
            dci_ref[...] = jnp.zeros_like(dci_ref)

        dy = dy_ref[...]
        dsr_ref[...] = _dot(dy, cr_ref[...])
        dsi_ref[...] = -_dot(dy, ci_ref[...])
        dcr_ref[...] += _dot(dy, sr_ref[...].astype(bf16), TN)
        dci_ref[...] -= _dot(dy, si_ref[...].astype(bf16), TN)

    big = pl.BlockSpec((tm, S5_LANES), lambda i: (i, 0))
    mat = _full((S5_WIDTH, S5_LANES))
    return _pallas(
        body, ride=ride, name="s5_state_grad", grid=(L // tm,),
        in_specs=[pl.BlockSpec((tm, S5_WIDTH), lambda i: (i, 0)), big, big, mat, mat],
        out_specs=[big, big, mat, mat],
        out_shape=[S((L, S5_LANES), f32)] * 2 + [S((S5_WIDTH, S5_LANES), f32)] * 2,
        compiler_params=_cp("arbitrary"),
    )(dy_lin, s_re, s_im, c_re, c_im)


def _s5_drive_bwd(g_re, g_im, u, du_skip, bb_re, bb_im, tm, ride=()):
    L = u.shape[0]

    def body(gr_ref, gi_ref, u_ref, dus_ref, br_ref, bi_ref, du_ref, dbr_ref, dbi_ref):
        @pl.when(pl.program_id(0) == 0)
        def _():
            dbr_ref[...] = jnp.zeros_like(dbr_ref)
            dbi_ref[...] = jnp.zeros_like(dbi_ref)

        gr = gr_ref[...].astype(bf16)
        gi = gi_ref[...].astype(bf16)
        ub = u_ref[...].astype(bf16)
        du_ref[...] = dus_ref[...] + _dot(gr, br_ref[...], NT) + _dot(gi, bi_ref[...], NT)
        dbr_ref[...] += _dot(ub, gr, TN)
        dbi_ref[...] += _dot(ub, gi, TN)

    big = pl.BlockSpec((tm, S5_LANES), lambda i: (i, 0))
    row = pl.BlockSpec((tm, S5_WIDTH), lambda i: (i, 0))
    mat = _full((S5_WIDTH, S5_LANES))
    return _pallas(
        body, ride=ride, name="s5_drive_bwd", grid=(L // tm,),
        in_specs=[big, big, row, row, mat, mat],
        out_specs=[row, mat, mat],
        out_shape=[S((L, S5_WIDTH), f32)] + [S((S5_WIDTH, S5_LANES), f32)] * 2,
        compiler_params=_cp("arbitrary"),
    )(g_re, g_im, u, du_skip, bb_re, bb_im)


def _group_mean(x, avg):
    hi = x.astype(bf16)
    lo = (x - hi.astype(f32)).astype(bf16)
    return _dot(hi, avg) + _dot(lo, avg)


def _conv_act(zn, ln_g, ln_b):
    t = zn * ln_g + ln_b
    return t * _sigmoid(t)


def _glu_padded(v_ref, halo_ref, zpad_ref, tm):
    v = v_ref[...]
    vh = halo_ref[...]
    zh = vh[:, :CONV_WIDTH] * _sigmoid(vh[:, CONV_WIDTH:])
    zpad_ref[:CONV_HALO, :] = jnp.where(pl.program_id(0) > 0, zh, 0.0)
    zpad_ref[CONV_HALO:CONV_HALO + tm, :] = v[:, :CONV_WIDTH] * _sigmoid(v[:, CONV_WIDTH:])
    zpad_ref[CONV_HALO + tm:, :] = jnp.zeros((8, CONV_WIDTH), f32)


def _shifted(pad_ref, sh_ref, tm):
    for b in range(8):
        sh_ref[b] = pad_ref[pl.ds(b, tm + CONV_HALO), :]


def _window(sh_ref, r0, off, rows):
    return sh_ref[off % 8, pl.ds(pl.multiple_of(r0 + 8 * (off // 8), 8), rows), :]


def _tap_sum(w_ref, sh_ref, taps, out_ref, tm, bias):
    def chunk(c, carry):
        r0 = pl.multiple_of(c * CONV_ROWS, CONV_ROWS)
        acc = jnp.zeros((CONV_ROWS, CONV_WIDTH), f32) + bias
        for k, off in taps:
            acc = acc + w_ref[k:k + 1, :] * _window(sh_ref, r0, off, CONV_ROWS)
        out_ref[pl.ds(r0, CONV_ROWS), :] = acc
        return carry

    lax.fori_loop(0, tm // CONV_ROWS, chunk, 0)


FWD_TAPS = [(k, CONV_HALO - (CONV_K - 1) + k) for k in range(CONV_K)]
BWD_TAPS = [(k, CONV_K - 1 - k) for k in range(CONV_K)]


def _conv_specs(tm):
    per = tm // CONV_HALO
    vrow = pl.BlockSpec((tm, 2 * CONV_WIDTH), lambda i: (i, 0))
    vhalo = pl.BlockSpec((CONV_HALO, 2 * CONV_WIDTH), lambda i: (jnp.maximum(i * per - 1, 0), 0))
    return vrow, vhalo


def _conv_scratch(tm):
    return [pltpu.VMEM((tm + CONV_HALO + 8, CONV_WIDTH), f32), pltpu.VMEM((8, tm + CONV_HALO, CONV_WIDTH), f32)]


def _conv_fwd(v, w_dw, b_dw, ln_g, ln_b, avg, tm, ride=()):
    L = v.shape[0]

    def body(v_ref, halo_ref, w_ref, b_ref, g_ref, bb_ref, avg_ref, o_ref, zc_ref, zpad_ref, zs_ref):
        _glu_padded(v_ref, halo_ref, zpad_ref, tm)
        _shifted(zpad_ref, zs_ref, tm)
        _tap_sum(w_ref, zs_ref, FWD_TAPS, zc_ref, tm, b_ref[...])
        zc = zc_ref[...]
        xc = zc - _group_mean(zc, avg_ref[...])
        zn = xc * lax.rsqrt(_group_mean(xc * xc, avg_ref[...]) + EPS)
        o_ref[...] = _conv_act(zn, g_ref[...], bb_ref[...]).astype(bf16)

    vrow, vhalo = _conv_specs(tm)
    vec = _full((1, CONV_WIDTH))
    row = pl.BlockSpec((tm, CONV_WIDTH), lambda i: (i, 0))
    return _pallas(
        body, ride=ride, name="conv_fwd", grid=(L // tm,),
        in_specs=[vrow, vhalo, _full((CONV_HALO, CONV_WIDTH)), vec, vec, vec, _full((CONV_WIDTH, CONV_WIDTH))],
        out_specs=[row, row], out_shape=[S((L, CONV_WIDTH), bf16), S((L, CONV_WIDTH), f32)],
        scratch_shapes=_conv_scratch(tm),
        compiler_params=_cp("arbitrary"),
    )(v, v, w_dw, b_dw, ln_g, ln_b, avg)


def _conv_bwd_norm(dout, zc, ln_g, ln_b, avg, tm):
    L = zc.shape[0]

    def body(do_ref, zc_ref, g_ref, bb_ref, avg_ref, dzc_ref, dg_ref, db_ref, dbd_ref):
        @pl.when(pl.program_id(0) == 0)
        def _():
            dg_ref[...] = jnp.zeros_like(dg_ref)
            db_ref[...] = jnp.zeros_like(db_ref)
            dbd_ref[...] = jnp.zeros_like(dbd_ref)

        avg = avg_ref[...]
        zc = zc_ref[...]
        xc = zc - _group_mean(zc, avg)
        rstd = lax.rsqrt(_group_mean(xc * xc, avg) + EPS)
        xhat = xc * rstd
        _, act_vjp = jax.vjp(_conv_act, xhat, g_ref[...], bb_ref[...])
        dxhat, dg, db = act_vjp(do_ref[...])
        dzc = rstd * (dxhat - _group_mean(dxhat, avg) - xhat * _group_mean(dxhat * xhat, avg))
        dzc_ref[...] = dzc
        dg_ref[0:1, :] += dg
        db_ref[0:1, :] += db
        dbd_ref[...] += _rows8(dzc)

    vec = _full((1, CONV_WIDTH))
    row = pl.BlockSpec((tm, CONV_WIDTH), lambda i: (i, 0))
    part = _full((8, CONV_WIDTH))
    return pl.pallas_call(
        body, name="conv_bwd_norm", grid=(L // tm,),
        in_specs=[row, row, vec, vec, _full((CONV_WIDTH, CONV_WIDTH))],
        out_specs=[row, part, part, part],
        out_shape=[S((L, CONV_WIDTH), f32)] + [S((8, CONV_WIDTH), f32)] * 3,
        compiler_params=_cp("arbitrary"),
    )(dout, zc, ln_g, ln_b, avg)


def _conv_bwd_taps(dzc, v, w_dw, tm):
    L = v.shape[0]
    nt = L // tm
    per = tm // CONV_HALO

    def body(d_ref, dn_ref, v_ref, halo_ref, w_ref, dv_ref, dw_ref, zpad_ref, zs_ref, dpad_ref, ds_ref, dz_ref):
        i = pl.program_id(0)

        @pl.when(i == 0)
        def _():
            dw_ref[...] = jnp.zeros_like(dw_ref)

        _glu_padded(v_ref, halo_ref, zpad_ref, tm)
        _shifted(zpad_ref, zs_ref, tm)
        dpad_ref[:tm, :] = d_ref[...]
        dpad_ref[tm:tm + CONV_HALO, :] = jnp.where(i < nt - 1, dn_ref[...], 0.0)
        dpad_ref[tm + CONV_HALO:, :] = jnp.zeros((8, CONV_WIDTH), f32)
        _shifted(dpad_ref, ds_ref, tm)
        _tap_sum(w_ref, ds_ref, BWD_TAPS, dz_ref, tm, 0.0)

        for first in range(0, CONV_K, 8):
            taps = FWD_TAPS[first:first + 8]

            def chunk(c, accs, taps=taps):
                r0 = pl.multiple_of(c * 8, 8)
                d = d_ref[pl.ds(r0, 8), :]
                return tuple(acc + d * _window(zs_ref, r0, off, 8) for acc, (_, off) in zip(accs, taps))

            accs = lax.fori_loop(0, tm // 8, chunk, tuple(jnp.zeros((8, CONV_WIDTH), f32) for _ in taps), unroll=2)
            for acc, (k, _) in zip(accs, taps):
                dw_ref[k] += acc

        dz = dz_ref[...]
        v = v_ref[...]
        sig = _sigmoid(v[:, CONV_WIDTH:])
        dv_ref[:, :CONV_WIDTH] = dz * sig
        dv_ref[:, CONV_WIDTH:] = dz * v[:, :CONV_WIDTH] * sig * (1.0 - sig)

    vrow, vhalo = _conv_specs(tm)
    row = pl.BlockSpec((tm, CONV_WIDTH), lambda i: (i, 0))
    nxt = pl.BlockSpec((CONV_HALO, CONV_WIDTH), lambda i: (jnp.minimum((i + 1) * per, nt * per - 1), 0))
    return pl.pallas_call(
        body, name="conv_bwd_taps", grid=(nt,),
        in_specs=[row, nxt, vrow, vhalo, _full((CONV_HALO, CONV_WIDTH))],
        out_specs=[vrow, _full((CONV_HALO, 8, CONV_WIDTH))],
        out_shape=[S((L, 2 * CONV_WIDTH), f32), S((CONV_HALO, 8, CONV_WIDTH), f32)],
        scratch_shapes=_conv_scratch(tm) * 2 + [pltpu.VMEM((tm, CONV_WIDTH), f32)],
        compiler_params=_cp("arbitrary"),
    )(dzc, dzc, v, v, w_dw)


def _loss_head(x, target, g, tm):
    L = x.shape[0]

    def body(x_ref, t_ref, g_ref, dx_ref, dg_ref, l_ref):
        @pl.when(pl.program_id(0) == 0)
        def _():
            dg_ref[...] = jnp.zeros_like(dg_ref)
            l_ref[...] = jnp.zeros_like(l_ref)

        x, g = x_ref[...], g_ref[...]
        e = _rms(x, g) - t_ref[...]
        l_ref[...] += _rows8(e * e) * (0.5 / D_MODEL)
        dx, dg = _rms_bwd(x, g, e * (1.0 / D_MODEL))
        dx_ref[...] = dx
        dg_ref[...] += dg

    row = pl.BlockSpec((tm, D_MODEL), lambda i: (i, 0))
    return pl.pallas_call(
        body, name="loss_head", grid=(L // tm,),
        in_specs=[row, row, _full((1, D_MODEL))],
        out_specs=[row, _full((1, D_MODEL)), _full((8, D_MODEL))],
        out_shape=[S((L, D_MODEL), f32), S((1, D_MODEL), f32), S((8, D_MODEL), f32)],
        compiler_params=_cp("arbitrary"),
    )(x, target, g)


def _to_segments(a):
    L, c = a.shape
    return a.reshape(SEGMENTS, L // SEGMENTS, c).transpose(1, 0, 2).reshape(L, c)


def _from_segments(a):
    L, c = a.shape
    return a.reshape(L // SEGMENTS, SEGMENTS, c).transpose(1, 0, 2).reshape(L, c)


def _block_diag(ms):
    n = len(ms)

    def body(*refs):
        for a in range(n):
            out = refs[n + a]
            out[...] = jnp.zeros_like(out)
            for g in range(S5_GROUPS):
                rows = slice(g * S5_GROUP_CH, (g + 1) * S5_GROUP_CH)
                out[rows, g * S5_STATE:(g + 1) * S5_STATE] = refs[a][rows, :].astype(bf16)

    return pl.pallas_call(body, name="s5_block_diag", out_shape=[S((S5_WIDTH, S5_LANES), bf16)] * n,
                          compiler_params=pltpu.CompilerParams(vmem_limit_bytes=VMEM_LIMIT))(
        *[m.reshape(S5_WIDTH, S5_STATE) for m in ms])


def _diag_blocks(ms):
    n = len(ms)

    def body(*refs):
        for a in range(n):
            for g in range(S5_GROUPS):
                rows = slice(g * S5_GROUP_CH, (g + 1) * S5_GROUP_CH)
                refs[n + a][rows, :] = refs[a][rows, g * S5_STATE:(g + 1) * S5_STATE]

    out = pl.pallas_call(body, name="s5_diag_blocks", out_shape=[S((S5_WIDTH, S5_STATE), f32)] * n,
                         compiler_params=pltpu.CompilerParams(vmem_limit_bytes=VMEM_LIMIT))(*ms)
    return [o.reshape(S5_GROUPS, S5_GROUP_CH, S5_STATE) for o in out]


class _NoExchanges:
    def before(self, point):
        return ()

    def after(self, point):
        pass

    def alone(self, point):
        pass


def _ffn_block(x, p, tag, tm, sched):
    point = tag + "_up"
    h, dadg, dadu, a = _ffn_up(x, p[tag + "_norm"], p[tag + "_w_gate"], p[tag + "_w_up"], tm, 768, tag, ride=sched.before(point))
    sched.after(point)
    return _ffn_down(x, a, p[tag + "_w_down"], tm, tag), (h, dadg, dadu, a)


def _ffn_block_bwd(dxo, x, p, tag, saved, tm, grads, sched):
    h, dadg, dadu, a = saved
    dgate, dup, dxh = _ffn_bwd_act(dxo, p[tag + "_w_down"], dadg, dadu, tm, 768, tag, ride=sched.before(tag + "_bwd_act"))
    sched.after(tag + "_bwd_act")
    dx, grads[tag + "_norm"] = _ffn_bwd_in(dxo, x, p[tag + "_norm"], dgate, dup, p[tag + "_w_gate"], p[tag + "_w_up"], tm, tag,
                                           ride=sched.before(tag + "_bwd_in"))
    sched.after(tag + "_bwd_in")
    for which, lhs, rhs in (("gate", h, dgate), ("up", h, dup), ("down", a, dxh)):
        point = tag + "_dw_" + which
        grads[tag + "_w_" + which] = _mm_tn(lhs, rhs, bf16, point, ride=sched.before(point))
        sched.after(point)
    return dx


def _local_step(x, target, p, grads, sched):
    L = x.shape[0]
    tm = min(512, L)
    ts = min(256, L)
    ni = L // SEGMENTS
    bi = min(64, ni)

    def carried(point, fn, *args):
        out = fn(*args, ride=sched.before(point))
        sched.after(point)
        return out

    sched.alone("start")
    x1, saved1 = _ffn_block(x, p, "ffn1", tm, sched)

    h2, u_s5, v = _mix_in(x1, p["mix_norm"], p["w_in"], tm)
    b_t = lambda b: b.transpose(0, 2, 1)
    s5_in = (p["s5_lam_re"], p["s5_lam_im"], p["s5_log_dt"].reshape(S5_GROUPS, 1), b_t(p["s5_b_re"]), b_t(p["s5_b_im"]))
    abar_re, abar_im, bbar_re, bbar_im = _s5_params(*s5_in)
    a_re, a_im = abar_re.reshape(1, S5_LANES), abar_im.reshape(1, S5_LANES)
    bb_re, bb_im, cc_re, cc_im = _block_diag([bbar_re, bbar_im, p["s5_c_re"], p["s5_c_im"]])
    u_seg = _to_segments(u_s5)
    bu_re, bu_im = _s5_drive(u_seg, bb_re, bb_im, tm)
    seg3 = lambda a: a.reshape(ni, SEGMENTS, S5_LANES)
    flat = lambda a: a.reshape(L, S5_LANES)
    e_re, e_im = _scan_ends(seg3(bu_re), seg3(bu_im), a_re, a_im, False, bi)
    s_re, s_im = carried("s5_scan_states", _scan_states, seg3(bu_re), seg3(bu_im), a_re, a_im, e_re, e_im, bi)
    y_lin, y_s5_seg = carried("s5_read", _s5_read, flat(s_re), flat(s_im), u_seg, cc_re, cc_im, p["s5_d"], p["s5_w_glu"],
                              p["s5_b_glu"], ts)
    y_s5 = _from_segments(y_s5_seg)
    w_dw = jnp.pad(p["conv_w_dw"], ((0, CONV_HALO - CONV_K), (0, 0)))
    heads = jnp.arange(CONV_WIDTH) // CONV_HEAD
    avg = ((heads[:, None] == heads[None, :]).astype(f32) / CONV_HEAD).astype(bf16)
    y_conv, zc = carried("conv_fwd", _conv_fwd, v, w_dw, p["conv_b_dw"], p["conv_ln_g"], p["conv_ln_b"], avg, tm)
    x2 = _mix_out(x1, y_s5, y_conv, p["w_out"], tm)

    x3, saved2 = _ffn_block(x2, p, "ffn2", tm, sched)
    dx3, grads["final_norm"], loss_terms = _loss_head(x3, target, p["final_norm"].reshape(1, D_MODEL), tm)

    dx2 = _ffn_block_bwd(dx3, x2, p, "ffn2", saved2, tm, grads, sched)

    dy_s5, dy_conv, dx2b = carried("mix_out_bwd", _mix_out_bwd, dx2, p["w_out"], tm)
    grads["w_out"] = jnp.concatenate([_mm_tn(y_s5, dx2b, bf16, "dw_out_s5"), _mm_tn(y_conv, dx2b, bf16, "dw_out_conv")], axis=0)
    dy_lin, du_skip, dd8, grads["s5_w_glu"], dbg8 = _s5_read_bwd(
        _to_segments(dy_s5), y_lin, u_seg, p["s5_d"], p["s5_w_glu"], p["s5_b_glu"], tm)
    grads["s5_d"] = dd8.sum(axis=0, keepdims=True)
    grads["s5_b_glu"] = dbg8.sum(axis=0, keepdims=True)
    ds_re, ds_im, dcc_re, dcc_im = carried("s5_state_grad", _s5_state_grad, dy_lin, flat(s_re), flat(s_im), cc_re, cc_im, ts)
    g_e_re, g_e_im = _scan_ends(seg3(ds_re), seg3(ds_im), a_re, -a_im, True, bi)
    g_re, g_im, da_re8, da_im8 = carried("s5_rscan_states", _rscan_states, seg3(ds_re), seg3(ds_im), a_re, -a_im, g_e_re, g_e_im,
                                         s_re, s_im, bi)
    du_seg, dbb_re, dbb_im = carried("s5_drive_bwd", _s5_drive_bwd, flat(g_re), flat(g_im), u_seg, du_skip, bb_re, bb_im, ts)
    d_abar = lambda a8: a8.sum(axis=0).reshape(S5_GROUPS, S5_STATE)
    grads["s5_c_re"], grads["s5_c_im"], d_bbr, d_bbi = _diag_blocks([dcc_re, dcc_im, dbb_re, dbb_im])
    d_lr, d_li, d_ld, d_br, d_bi = _s5_params_bwd(*s5_in, d_abar(da_re8), d_abar(da_im8), d_bbr, d_bbi)
    grads["s5_lam_re"], grads["s5_lam_im"], grads["s5_log_dt"] = d_lr, d_li, d_ld.reshape(1, S5_GROUPS)
    grads["s5_b_re"], grads["s5_b_im"] = b_t(d_br), b_t(d_bi)
    dzc, dlg8, dlb8, dbd8 = _conv_bwd_norm(dy_conv, zc, p["conv_ln_g"], p["conv_ln_b"], avg, tm)
    grads["conv_ln_g"] = dlg8.sum(axis=0, keepdims=True)
    grads["conv_ln_b"] = dlb8.sum(axis=0, keepdims=True)
    grads["conv_b_dw"] = dbd8.sum(axis=0, keepdims=True)
    dv, dw8 = _conv_bwd_taps(dzc, v, w_dw, tm)
    grads["conv_w_dw"] = dw8.sum(axis=1)[:CONV_K]
    dx1, grads["mix_norm"], dub = _mix_in_bwd(dx2, x1, p["mix_norm"], _from_segments(du_seg), dv, p["w_in"], tm)
    grads["w_in"] = _mm_tn(h2, dub, bf16, "dw_in")

    grads["loss_terms"] = loss_terms
    dx0 = _ffn_block_bwd(dx1, x, p, "ffn1", saved1, tm, grads, sched)
    sched.alone("tail_sibling")
    sched.alone("tail_chips")
    return loss_terms, dx0


MESH = pl.DeviceIdType.MESH
ANY = pl.BlockSpec(memory_space=pl.ANY)


def _place():
    return lax.axis_index("x"), lax.axis_index("y"), lax.axis_index("c")


class _Exchange:
    def __init__(self, ins, out_shape, sems, start, finish):
        self.ins, self.out_shape, self.sems, self.start, self.finish = list(ins), list(out_shape), list(sems), start, finish
        self.out = None


def _pallas(body, *, ride=(), **kw):
    if not ride:
        return pl.pallas_call(body, **kw)

    def run(*args):
        out_shape = kw.get("out_shape", [])
        single = not isinstance(out_shape, (list, tuple))
        shapes = [out_shape] if single else list(out_shape)
        out_specs = [kw["out_specs"]] if single else list(kw.get("out_specs", []))
        grid = tuple(kw.get("grid", ()))
        scratch = list(kw.get("scratch_shapes", ()))
        n_in, n_out, n_scr = len(args), len(shapes), len(scratch)
        r_in = [len(e.ins) for e in ride]
        r_out = [len(e.out_shape) for e in ride]
        r_sem = [len(e.sems) for e in ride]

        def wrapped(*refs):
            own_in, refs = refs[:n_in], refs[n_in:]
            ex_in, refs = refs[:sum(r_in)], refs[sum(r_in):]
            own_out, refs = refs[:n_out], refs[n_out:]
            ex_out, refs = refs[:sum(r_out)], refs[sum(r_out):]
            own_scr, ex_sem = refs[:n_scr], refs[n_scr:]
            parts = []
            for e, ni, no, ns in zip(ride, r_in, r_out, r_sem):
                parts.append((e, ex_in[:ni], ex_out[:no], ex_sem[:ns]))
                ex_in, ex_out, ex_sem = ex_in[ni:], ex_out[no:], ex_sem[ns:]

            def at(step):
                def go():
                    for e, i, o, s in parts:
                        getattr(e, step)(i, o, s)
                if grid:
                    ids = [pl.program_id(d) for d in range(len(grid))]
                    when = [i == (0 if step == "start" else g - 1) for i, g in zip(ids, grid)]
                    pl.when(functools.reduce(lambda a, b: a & b, when))(go)
                else:
                    go()

            at("start")
            if body is not None:
                body(*own_in, *own_out, *own_scr)
            at("finish")

        outs = pl.pallas_call(
            wrapped, name=kw["name"], grid=grid,
            in_specs=list(kw.get("in_specs", [])) + [ANY] * sum(r_in),
            out_specs=out_specs + [ANY] * sum(r_out),
            out_shape=shapes + [s for e in ride for s in e.out_shape],
            scratch_shapes=scratch + [s for e in ride for s in e.sems],
            compiler_params=_cp(*["arbitrary"] * len(grid)),
        )(*args, *[a for e in ride for a in e.ins])
        own, rest = outs[:n_out], outs[n_out:]
        for e, no in zip(ride, r_out):
            e.out, rest = list(rest[:no]), rest[no:]
        return own[0] if single else own

    return run


def _exchange(ride, name):
    _pallas(None, ride=ride, name=name)()


def _gather(arrs):
    n = len(arrs)

    def copies(ins, outs, sems):
        send_sems, recv_sems, local_sems = sems
        x, y, c = _place()
        me, sibling = (x, y, c), (x, y, 1 - c)
        chips = [(1 - x, y), (x, 1 - y), (1 - x, 1 - y)]

        def copy(a, k, block, to, src=None):
            px, py, pc = block
            dst = outs[a].at[4 * px + 2 * py + pc]
            return pltpu.make_async_remote_copy(
                src_ref=dst if src is None else src, dst_ref=dst, send_sem=send_sems.at[7 * a + k],
                recv_sem=recv_sems.at[7 * a + k], device_id=to, device_id_type=MESH)

        def own():
            local = [pltpu.make_async_copy(ins[a], outs[a].at[4 * x + 2 * y + c], local_sems.at[a]) for a in range(n)]
            remote = []
            for a in range(n):
                remote.append(copy(a, 0, me, sibling, src=ins[a]))
                remote += [copy(a, 1 + j, me, (*chip, c), src=ins[a]) for j, chip in enumerate(chips)]
            return local, remote

        return c, me, sibling, chips, copy, own

    def start(ins, outs, sems):
        local, remote = copies(ins, outs, sems)[-1]()
        for cp in local + remote:
            cp.start()

    def finish(ins, outs, sems):
        c, me, sibling, chips, copy, own = copies(ins, outs, sems)
        passed = []
        for j, chip in enumerate(chips):
            for a in range(n):
                copy(a, 1 + j, (*chip, c), me).wait_recv()
                passed.append(copy(a, 4 + j, (*chip, c), sibling))
                passed[-1].start()
        for a in range(n):
            copy(a, 0, sibling, me).wait_recv()
            for j, chip in enumerate(chips):
                copy(a, 4 + j, (*chip, 1 - c), me).wait_recv()
        local, remote = own()
        for cp in remote + passed:
            cp.wait_send()
        for cp in local:
            cp.wait()

    dma = pltpu.SemaphoreType.DMA
    return _Exchange(arrs, [S((N_DEV, *a.shape), a.dtype) for a in arrs], [dma((7 * n,)), dma((7 * n,)), dma((n,))], start, finish)


def _swap_with_sibling(gs):
    n = len(gs)

    def copies(ins, outs, sems):
        x, y, c = _place()
        return [pltpu.make_async_remote_copy(
            src_ref=ins[a].at[:, 1 - c], dst_ref=outs[a], send_sem=sems[0].at[a], recv_sem=sems[1].at[a],
            device_id=(x, y, 1 - c), device_id_type=MESH) for a in range(n)]

    def start(ins, outs, sems):
        for cp in copies(ins, outs, sems):
            cp.start()

    def finish(ins, outs, sems):
        for cp in copies(ins, outs, sems):
            cp.wait()

    dma = pltpu.SemaphoreType.DMA
    return _Exchange(gs, [S((N_CHIP, *g.shape[2:]), g.dtype) for g in gs], [dma((n,)), dma((n,))], start, finish)


def _swap_with_chips(ps):
    n = len(ps)

    def copies(ins, outs, sems):
        x, y, c = _place()
        q = 2 * x + y
        peers = [(x, 1 - y), (1 - x, y), (1 - x, 1 - y)]

        def copy(a, j, slot_from, slot_to):
            px, py = peers[j]
            return pltpu.make_async_remote_copy(
                src_ref=ins[a].at[slot_from], dst_ref=outs[a].at[slot_to], send_sem=sems[0].at[3 * a + j],
                recv_sem=sems[1].at[3 * a + j], device_id=(px, py, c), device_id_type=MESH)

        sends = lambda: [copy(a, j, 2 * peers[j][0] + peers[j][1], q) for a in range(n) for j in range(3)]
        lands = lambda: [copy(a, j, q, 2 * peers[j][0] + peers[j][1]) for a in range(n) for j in range(3)]
        return sends, lands

    def start(ins, outs, sems):
        for cp in copies(ins, outs, sems)[0]():
            cp.start()

    def finish(ins, outs, sems):
        sends, lands = copies(ins, outs, sems)
        for cp in lands():
            cp.wait_recv()
        for cp in sends():
            cp.wait_send()

    dma = pltpu.SemaphoreType.DMA
    return _Exchange(ps, [S(p.shape, p.dtype) for p in ps], [dma((3 * n,)), dma((3 * n,))], start, finish)


def _row_tile(rows, cols, itemsize):
    t = rows
    while t * cols * itemsize > (1 << 20) and t % 32 == 0:
        t //= 2
    return t


def _add_sibling(g4, st, core, name):
    _, _, R, C = g4.shape
    tr = _row_tile(R, C, 4)

    def body(c_ref, g_ref, s_ref, o_ref):
        o_ref[...] = (g_ref[...].astype(f32) + s_ref[...].astype(f32)).astype(bf16)

    return pl.pallas_call(
        body, name=name,
        grid_spec=pltpu.PrefetchScalarGridSpec(
            num_scalar_prefetch=1, grid=(N_CHIP, R // tr),
            in_specs=[pl.BlockSpec((None, None, tr, C), lambda q, i, c: (q, c[0], i, 0)),
                      pl.BlockSpec((None, tr, C), lambda q, i, c: (q, i, 0))],
            out_specs=pl.BlockSpec((None, tr, C), lambda q, i, c: (q, i, 0))),
        out_shape=S((N_CHIP, R, C), bf16),
        compiler_params=_cp("parallel", "parallel"),
    )(core, g4, st)


def _adamw(w, g, m, v):
    m = B1 * m + (1.0 - B1) * g
    v = B2 * v + (1.0 - B2) * (g * g)
    m_hat = m / (1.0 - B1 ** STEP)
    v_hat = v / (1.0 - B2 ** STEP)
    return -LR * (m_hat / (jnp.sqrt(v_hat) + ADAM_EPS) + WD * w), m, v


def _adam_sharded(w, m, v, part, got, slots, name):
    R, C = w.shape
    _, Rp, Cp = part.shape
    tr = _row_tile(R, Cp, 4) if Rp == R else R

    def body(s_ref, w_ref, m_ref, v_ref, p_ref, a_ref, b_ref, c_ref, g_out, d_out, m_out, v_out):
        g = p_ref[...].astype(f32) + a_ref[...].astype(f32) + b_ref[...].astype(f32) + c_ref[...].astype(f32)
        g = g[:, :C]
        g_out[...] = g
        d_out[...], m_out[...], v_out[...] = _adamw(w_ref[...], g, m_ref[...], v_ref[...])

    shard = pl.BlockSpec((tr, C), lambda i, s: (i, 0))
    slot = lambda k: pl.BlockSpec((None, tr, Cp), lambda i, s: (s[k], i, 0))
    return pl.pallas_call(
        body, name=name,
        grid_spec=pltpu.PrefetchScalarGridSpec(
            num_scalar_prefetch=1, grid=(R // tr,),
            in_specs=[shard, shard, shard, slot(0), slot(1), slot(2), slot(3)],
            out_specs=[shard] * 4),
        out_shape=[S((R, C), f32)] * 4,
        compiler_params=_cp("parallel"),
    )(slots, w, m, v, part, got, got, got)


def _adam_replicated(w, m, v, gathered, name):
    R = w.shape[0]

    def body(w_ref, m_ref, v_ref, g_ref, g_out, d_out, m_out, v_out):
        g = g_ref[0]
        for d in range(1, N_DEV):
            g = g + g_ref[d]
        g_out[...] = g
        d_out[...], m_out[...], v_out[...] = _adamw(w_ref[...], g, m_ref[...], v_ref[...])

    return pl.pallas_call(body, name=name, out_shape=[S((R, 128), f32)] * 4,
                          compiler_params=pltpu.CompilerParams(vmem_limit_bytes=VMEM_LIMIT))(w, m, v, gathered)


WEIGHTS = ["ffn1_norm", "ffn1_w_gate", "ffn1_w_up", "ffn1_w_down", "mix_norm", "w_in", "s5_lam_re", "s5_lam_im", "s5_log_dt",
           "s5_b_re", "s5_b_im", "s5_c_re", "s5_c_im", "s5_d", "s5_w_glu", "s5_b_glu", "conv_w_dw", "conv_b_dw", "conv_ln_g",
           "conv_ln_b", "w_out", "ffn2_norm", "ffn2_w_gate", "ffn2_w_up", "ffn2_w_down", "final_norm"]
SHARDED = ["ffn1_w_gate", "ffn1_w_up", "ffn1_w_down", "w_in", "s5_w_glu", "conv_w_dw", "w_out", "ffn2_w_gate", "ffn2_w_up",
           "ffn2_w_down"]
REPLICATED = [n for n in WEIGHTS if n not in SHARDED]
PACK = 8 * 128


def _is_up(n):
    return n.endswith("w_gate") or n.endswith("w_up")


def _shard_to_wire(n, w):
    if _is_up(n):
        w = jnp.pad(w, ((0, 0), (0, FF_SHARD_PAD - FF_SHARD)))
    elif n.endswith("w_down"):
        w = jnp.pad(w, ((0, FF_SHARD_PAD - FF_SHARD), (0, 0)))
    elif n == "conv_w_dw":
        return jnp.pad(w, ((0, CONV_HALO - CONV_K), (0, 0)))
    return w.astype(bf16)


def _gathered_to_full(n, g):
    if n == "conv_w_dw":
        return g.transpose(1, 0, 2).reshape(CONV_HALO, CONV_WIDTH)[:CONV_K]
    if _is_up(n) or n == "w_in":
        return g.transpose(1, 0, 2).reshape(g.shape[1], N_DEV * g.shape[2])
    return g.reshape(N_DEV * g.shape[1], g.shape[2])


def _grad_to_blocks(n, g):
    if n == "conv_w_dw":
        g = jnp.pad(g, ((0, CONV_HALO - CONV_K), (0, 0)))
    g = g.astype(bf16)
    if _is_up(n) or n in ("w_in", "conv_w_dw"):
        g = g.reshape(g.shape[0], N_DEV, g.shape[1] // N_DEV).transpose(1, 0, 2)
    else:
        g = g.reshape(N_DEV, g.shape[0] // N_DEV, g.shape[1])
    return g.reshape(N_CHIP, 2, *g.shape[1:])


def _pack(parts):
    out = []
    for a in parts:
        a = a.reshape(-1)
        out.append(jnp.pad(a, (0, -a.size % PACK)))
    return jnp.concatenate(out).reshape(-1, 128)


def _unpack(packed, like):
    out, at = [], 0
    flat = packed.reshape(-1)
    for a in like:
        out.append(flat[at:at + a.size].reshape(a.shape))
        at += a.size + (-a.size % PACK)
    return out


PLAN = {
    "start": [("gather", ["ffn1_w_gate", "ffn1_w_up"])],
    "ffn1_up": [("gather", ["ffn1_w_down", "w_in", "w_out", "s5_w_glu", "conv_w_dw"])],
    "s5_scan_states": [("gather", ["ffn2_w_gate"])],
    "s5_read": [("gather", ["ffn2_w_up"])],
    "conv_fwd": [("gather", ["ffn2_w_down"])],
    "ffn2_dw_up": [("sibling", ["ffn2_w_gate"])],
    "ffn2_dw_down": [("sibling", ["ffn2_w_up"])],
    "mix_out_bwd": [("sibling", ["ffn2_w_down"])],
    "s5_state_grad": [("chips", ["ffn2_w_gate"])],
    "s5_rscan_states": [("chips", ["ffn2_w_up"])],
    "s5_drive_bwd": [("chips", ["ffn2_w_down"])],
    "ffn1_bwd_act": [("sibling", ["w_in", "s5_w_glu", "conv_w_dw", "w_out"])],
    "ffn1_bwd_in": [("chips", ["w_in", "s5_w_glu", "conv_w_dw", "w_out"])],
    "ffn1_dw_gate": [("replicated", [])],
    "ffn1_dw_up": [("sibling", ["ffn1_w_gate"])],
    "ffn1_dw_down": [("sibling", ["ffn1_w_up"]), ("chips", ["ffn1_w_gate"])],
    "tail_sibling": [("sibling", ["ffn1_w_down"])],
    "tail_chips": [("chips", ["ffn1_w_up", "ffn1_w_down"])],
}


class _Schedule:
    def __init__(self, wire, p, grads, core, pack_mine):
        self.wire, self.p, self.grads, self.core, self.pack_mine = wire, p, grads, core, pack_mine
        self.partial, self.reduced, self.everyone, self.pending = {}, {}, None, []

    def before(self, point):
        assert not self.pending
        for kind, names in PLAN.get(point, ()):
            if kind == "gather":
                given = [self.wire[n] for n in names]
                ex = _gather(given)
            elif kind == "sibling":
                given = [_grad_to_blocks(n, self.grads[n]) for n in names]
                ex = _swap_with_sibling(given)
            elif kind == "chips":
                given = [self.partial.pop(n) for n in names]
                ex = _swap_with_chips(given)
            else:
                given = [self.pack_mine()]
                ex = _gather(given)
            self.pending.append((kind, names, given, ex))
        return [ex for _, _, _, ex in self.pending]

    def after(self, point):
        for kind, names, given, ex in self.pending:
            if kind == "gather":
                for n, g in zip(names, ex.out):
                    self.p[n] = _gathered_to_full(n, g)
            elif kind == "sibling":
                for n, blocks, got in zip(names, given, ex.out):
                    self.partial[n] = _add_sibling(blocks, got, self.core, "reduce_add_" + n)
            elif kind == "chips":
                for n, part, got in zip(names, given, ex.out):
                    self.reduced[n] = (part, got)
            else:
                self.everyone = ex.out[0]
        self.pending = []

    def alone(self, point):
        _exchange(self.before(point), point)
        self.after(point)


def kernel(x, ffn1_norm, ffn1_w_gate, ffn1_w_up, ffn1_w_down, mix_norm, w_in, s5_lam_re, s5_lam_im, s5_log_dt, s5_b_re, s5_b_im, s5_c_re, s5_c_im, s5_d, s5_w_glu, s5_b_glu, conv_w_dw, conv_b_dw, conv_ln_g, conv_ln_b, w_out, ffn2_norm, ffn2_w_gate, ffn2_w_up, ffn2_w_down, final_norm, loss_target, m_ffn1_norm, m_ffn1_w_gate, m_ffn1_w_up, m_ffn1_w_down, m_mix_norm, m_w_in, m_s5_lam_re, m_s5_lam_im, m_s5_log_dt, m_s5_b_re, m_s5_b_im, m_s5_c_re, m_s5_c_im, m_s5_d, m_s5_w_glu, m_s5_b_glu, m_conv_w_dw, m_conv_b_dw, m_conv_ln_g, m_conv_ln_b, m_w_out, m_ffn2_norm, m_ffn2_w_gate, m_ffn2_w_up, m_ffn2_w_down, m_final_norm, v_ffn1_norm, v_ffn1_w_gate, v_ffn1_w_up, v_ffn1_w_down, v_mix_norm, v_w_in, v_s5_lam_re, v_s5_lam_im, v_s5_log_dt, v_s5_b_re, v_s5_b_im, v_s5_c_re, v_s5_c_im, v_s5_d, v_s5_w_glu, v_s5_b_glu, v_conv_w_dw, v_conv_b_dw, v_conv_ln_g, v_conv_ln_b, v_w_out, v_ffn2_norm, v_ffn2_w_gate, v_ffn2_w_up, v_ffn2_w_down, v_final_norm):
    args = locals()
    w = {n: args[n] for n in WEIGHTS}
    m = {n: args["m_" + n] for n in WEIGHTS}
    v = {n: args["v_" + n] for n in WEIGHTS}
    shard2d = lambda a: a.reshape(a.shape[-2:])
    xq, yq, cq = _place()
    q = 2 * xq + yq
    slots = jnp.stack([q, q ^ 1, q ^ 2, q ^ 3]).astype(jnp.int32)

    p = {}
    for n in REPLICATED:
        p[n] = w[n].reshape(w[n].shape[1:]) if w[n].ndim >= 3 else w[n]
    grads = {}
    zero = jnp.zeros((1,), f32)
    pack_mine = lambda: _pack([grads["loss_terms"].sum().reshape(1)] + [grads[n] for n in REPLICATED])
    sched = _Schedule({n: _shard_to_wire(n, shard2d(w[n])) for n in SHARDED}, p, grads,
                      jnp.reshape(cq, (1,)).astype(jnp.int32), pack_mine)
    _, dx = _local_step(x[0], loss_target[0], p, grads, sched)

    out = {}
    for n in SHARDED:
        part, got = sched.reduced[n]
        rows = part.shape[1] if n == "conv_w_dw" else w[n].shape[-2]
        fit = lambda a: jnp.pad(shard2d(a), ((0, rows - a.shape[-2]), (0, 0)))
        res = _adam_sharded(fit(w[n]), fit(m[n]), fit(v[n]), part, got, slots, "adam_" + n)
        out[n] = [r[:w[n].shape[-2]].reshape(w[n].shape) for r in res]

    everyone = sched.everyone
    res = _adam_replicated(_pack([zero] + [w[n] for n in REPLICATED]), _pack([zero] + [m[n] for n in REPLICATED]),
                           _pack([zero] + [v[n] for n in REPLICATED]), everyone, "adam_replicated")
    like = [zero] + [w[n] for n in REPLICATED]
    unpacked = [_unpack(r, like) for r in res]
    loss = unpacked[0][0].reshape(())
    for i, n in enumerate(REPLICATED):
        out[n] = [u[1 + i] for u in unpacked]

    return (loss, dx.reshape(x.shape), *[out[n][0] for n in WEIGHTS], *[out[n][1] for n in WEIGHTS],
            *[out[n][2] for n in WEIGHTS], *[out[n][3] for n in WEIGHTS])
```

```python
import functools

import jax
import jax.numpy as jnp
from jax import lax
from jax.experimental import pallas as pl
from jax.experimental.pallas import tpu as pltpu

f32 = jnp.float32
bf16 = jnp.bfloat16
S = jax.ShapeDtypeStruct

N_DEV = 8
N_CHIP = 4
D_MODEL = 1024
D_FF = 2816
FF_SHARD = D_FF // N_DEV
FF_SHARD_PAD = 384
FF_PAD = FF_SHARD_PAD * N_DEV
S5_WIDTH = 512
S5_GROUPS = 32
S5_GROUP_CH = 16
S5_STATE = 64
S5_LANES = S5_GROUPS * S5_STATE
CONV_WIDTH = 512
CONV_K = 31
CONV_HALO = 32
CONV_HEAD = 64
CONV_ROWS = 32
IN_COLS = S5_WIDTH + 2 * CONV_WIDTH
SEGMENTS = 8
SCAN_LANES = 512
EPS = 1e-6
LR, B1, B2, ADAM_EPS, WD, STEP = 0.001, 0.9, 0.999, 1e-08, 0.01, 10
VMEM_LIMIT = 56 * 1024 * 1024

NN = (((1,), (0,)), ((), ()))
NT = (((1,), (1,)), ((), ()))
TN = (((0,), (0,)), ((), ()))


def _dot(a, b, dims=NN):
    return lax.dot_general(a, b, dims, preferred_element_type=f32)


def _cp(*sem):
    return pltpu.CompilerParams(dimension_semantics=sem, vmem_limit_bytes=VMEM_LIMIT)


def _rms(x, g):
    return x * lax.rsqrt(jnp.mean(x * x, axis=-1, keepdims=True) + EPS) * g


def _rms_bwd(x, g, dh):
    _, vjp = jax.vjp(_rms, x, g)
    return vjp(dh)


def _sigmoid(x):
    return 1.0 / (1.0 + jnp.exp(-x))


def _gelu(x):
    return 0.5 * x * (1.0 + jnp.tanh(0.7978845608028654 * (x + 0.044715 * x * x * x)))


def _rows8(x):
    t, c = x.shape
    return x.reshape(t // 8, 8, c).sum(axis=0)


def _full(shape):
    return pl.BlockSpec(shape, lambda *_: (0,) * len(shape))


def _resident(shape):
    return pl.BlockSpec(shape, lambda *_: (0,) * len(shape), pipeline_mode=pl.Buffered(1))


def _ffn_up(x, g, wg, wu, tm, tn, tag, ride=()):
    L = x.shape[0]

    def body(x_ref, g_ref, wg_ref, wu_ref, h_ref, dadg_ref, dadu_ref, a_ref):
        h = _rms(x_ref[...], g_ref[...]).astype(bf16)
        h_ref[...] = h
        for j in range(FF_PAD // tn):
            cols = slice(j * tn, (j + 1) * tn)
            gate = _dot(h, wg_ref[:, cols])
            up = _dot(h, wu_ref[:, cols])
            sig = _sigmoid(gate)
            silu = gate * sig
            dadg_ref[:, cols] = (up * (sig + silu * (1.0 - sig))).astype(bf16)
            dadu_ref[:, cols] = silu.astype(bf16)
            a_ref[:, cols] = (silu * up).astype(bf16)

    row = pl.BlockSpec((tm, D_MODEL), lambda i: (i, 0))
    wide = pl.BlockSpec((tm, FF_PAD), lambda i: (i, 0))
    return _pallas(
        body, ride=ride, name=tag + "_up", grid=(L // tm,),
        in_specs=[row, _full((1, D_MODEL)), _resident((D_MODEL, FF_PAD)), _resident((D_MODEL, FF_PAD))],
        out_specs=[row, wide, wide, wide],
        out_shape=[S((L, D_MODEL), bf16)] + [S((L, FF_PAD), bf16)] * 3,
        compiler_params=_cp("parallel"),
    )(x, g, wg, wu)


def _ffn_down(x, a, wd, tm, tag):
    L = x.shape[0]

    def body(x_ref, a_ref, wd_ref, o_ref):
        o_ref[...] = x_ref[...] + 0.5 * _dot(a_ref[...], wd_ref[...])

    return pl.pallas_call(
        body, name=tag + "_down", grid=(L // tm,),
        in_specs=[pl.BlockSpec((tm, D_MODEL), lambda i: (i, 0)), pl.BlockSpec((tm, FF_PAD), lambda i: (i, 0)),
                  _resident((FF_PAD, D_MODEL))],
        out_specs=pl.BlockSpec((tm, D_MODEL), lambda i: (i, 0)),
        out_shape=S((L, D_MODEL), f32),
        compiler_params=_cp("parallel"),
    )(x, a, wd)


def _ffn_bwd_act(dxo, wd, dadg, dadu, tm, tn, tag, ride=()):
    L = dxo.shape[0]

    def body(dx_ref, wd_ref, dadg_ref, dadu_ref, dgate_ref, dup_ref, dxh_ref):
        dxh = (0.5 * dx_ref[...]).astype(bf16)
        dxh_ref[...] = dxh
        for j in range(FF_PAD // tn):
            cols = slice(j * tn, (j + 1) * tn)
            da = _dot(dxh, wd_ref[cols, :], NT)
            dgate_ref[:, cols] = (da * dadg_ref[:, cols].astype(f32)).astype(bf16)
            dup_ref[:, cols] = (da * dadu_ref[:, cols].astype(f32)).astype(bf16)

    row = pl.BlockSpec((tm, D_MODEL), lambda i: (i, 0))
    wide = pl.BlockSpec((tm, FF_PAD), lambda i: (i, 0))
    return _pallas(
        body, ride=ride, name=tag + "_bwd_act", grid=(L // tm,),
        in_specs=[row, _resident((FF_PAD, D_MODEL)), wide, wide],
        out_specs=[wide, wide, row],
        out_shape=[S((L, FF_PAD), bf16), S((L, FF_PAD), bf16), S((L, D_MODEL), bf16)],
        compiler_params=_cp("parallel"),
    )(dxo, wd, dadg, dadu)


def _ffn_bwd_in(dxo, x, g, dgate, dup, wg, wu, tm, tag, ride=()):
    L = x.shape[0]

    def body(dxo_ref, x_ref, g_ref, dgate_ref, dup_ref, wg_ref, wu_ref, dx_ref, dg_ref):
        @pl.when(pl.program_id(0) == 0)
        def _():
            dg_ref[...] = jnp.zeros_like(dg_ref)

        dh = _dot(dgate_ref[...], wg_ref[...], NT) + _dot(dup_ref[...], wu_ref[...], NT)
        dx, dg = _rms_bwd(x_ref[...], g_ref[...], dh)
        dx_ref[...] = dxo_ref[...] + dx
        dg_ref[...] += dg

    row = pl.BlockSpec((tm, D_MODEL), lambda i: (i, 0))
    wide = pl.BlockSpec((tm, FF_PAD), lambda i: (i, 0))
    return _pallas(
        body, ride=ride, name=tag + "_bwd_in", grid=(L // tm,),
        in_specs=[row, row, _full((1, D_MODEL)), wide, wide, _resident((D_MODEL, FF_PAD)), _resident((D_MODEL, FF_PAD))],
        out_specs=[row, _full((1, D_MODEL))],
        out_shape=[S((L, D_MODEL), f32), S((1, D_MODEL), f32)],
        compiler_params=_cp("arbitrary"),
    )(dxo, x, g, dgate, dup, wg, wu)


def _mm_tn(a, b, out_dtype, name, tm=512, tn=1024, ride=()):
    L, M = a.shape
    N = b.shape[1]
    tm, tn = min(tm, M), min(tn, N)
    while N % tn:
        tn //= 2

    def body(a_ref, b_ref, o_ref):
        o_ref[...] = _dot(a_ref[...].astype(bf16), b_ref[...].astype(bf16), TN).astype(out_dtype)

    return _pallas(
        body, ride=ride, name=name, grid=(M // tm, N // tn),
        in_specs=[pl.BlockSpec((L, tm), lambda i, j: (0, i)), pl.BlockSpec((L, tn), lambda i, j: (0, j))],
        out_specs=pl.BlockSpec((tm, tn), lambda i, j: (i, j)),
        out_shape=S((M, N), out_dtype),
        compiler_params=_cp("parallel", "parallel"),
    )(a, b)


def _mix_in(x, g, w_in, tm):
    L = x.shape[0]

    def body(x_ref, g_ref, w_ref, h_ref, us_ref, v_ref):
        h = _rms(x_ref[...], g_ref[...]).astype(bf16)
        h_ref[...] = h
        u = _dot(h, w_ref[...])
        us_ref[...] = u[:, :S5_WIDTH]
        v_ref[...] = u[:, S5_WIDTH:]

    row = lambda c: pl.BlockSpec((tm, c), lambda i: (i, 0))
    return pl.pallas_call(
        body, name="mix_in", grid=(L // tm,),
        in_specs=[row(D_MODEL), _full((1, D_MODEL)), _full((D_MODEL, IN_COLS))],
        out_specs=[row(D_MODEL), row(S5_WIDTH), row(2 * CONV_WIDTH)],
        out_shape=[S((L, D_MODEL), bf16), S((L, S5_WIDTH), f32), S((L, 2 * CONV_WIDTH), f32)],
        compiler_params=_cp("parallel"),
    )(x, g, w_in)


def _mix_in_bwd(dxo, x, g, du_s5, dv, w_in, tm):
    L = x.shape[0]

    def body(dxo_ref, x_ref, g_ref, dus_ref, dv_ref, w_ref, dx_ref, dg_ref, dub_ref):
        @pl.when(pl.program_id(0) == 0)
        def _():
            dg_ref[...] = jnp.zeros_like(dg_ref)

        dus = dus_ref[...].astype(bf16)
        dvb = dv_ref[...].astype(bf16)
        dub_ref[:, :S5_WIDTH] = dus
        dub_ref[:, S5_WIDTH:] = dvb
        dh = _dot(dus, w_ref[:, :S5_WIDTH], NT) + _dot(dvb, w_ref[:, S5_WIDTH:], NT)
        dx, dg = _rms_bwd(x_ref[...], g_ref[...], dh)
        dx_ref[...] = dxo_ref[...] + dx
        dg_ref[...] += dg

    row = lambda c: pl.BlockSpec((tm, c), lambda i: (i, 0))
    return pl.pallas_call(
        body, name="mix_in_bwd", grid=(L // tm,),
        in_specs=[row(D_MODEL), row(D_MODEL), _full((1, D_MODEL)), row(S5_WIDTH), row(2 * CONV_WIDTH),
                  _full((D_MODEL, IN_COLS))],
        out_specs=[row(D_MODEL), _full((1, D_MODEL)), row(IN_COLS)],
        out_shape=[S((L, D_MODEL), f32), S((1, D_MODEL), f32), S((L, IN_COLS), bf16)],
        compiler_params=_cp("arbitrary"),
    )(dxo, x, g, du_s5, dv, w_in)


def _mix_out(x, y_s5, y_conv, w_out, tm):
    L = x.shape[0]

    def body(x_ref, ys_ref, yc_ref, w_ref, o_ref):
        o_ref[...] = x_ref[...] + _dot(ys_ref[...], w_ref[:S5_WIDTH, :]) + _dot(yc_ref[...], w_ref[S5_WIDTH:, :])

    row = lambda c: pl.BlockSpec((tm, c), lambda i: (i, 0))
    return pl.pallas_call(
        body, name="mix_out", grid=(L // tm,),
        in_specs=[row(D_MODEL), row(S5_WIDTH), row(CONV_WIDTH), _full((D_MODEL, D_MODEL))],
        out_specs=row(D_MODEL), out_shape=S((L, D_MODEL), f32),
        compiler_params=_cp("parallel"),
    )(x, y_s5, y_conv, w_out)


def _mix_out_bwd(dx, w_out, tm, ride=()):
    L = dx.shape[0]

    def body(dx_ref, w_ref, dys_ref, dyc_ref, dxb_ref):
        dxb = dx_ref[...].astype(bf16)
        dxb_ref[...] = dxb
        dys_ref[...] = _dot(dxb, w_ref[:S5_WIDTH, :], NT)
        dyc_ref[...] = _dot(dxb, w_ref[S5_WIDTH:, :], NT)

    row = lambda c: pl.BlockSpec((tm, c), lambda i: (i, 0))
    return _pallas(
        body, ride=ride, name="mix_out_bwd", grid=(L // tm,),
        in_specs=[row(D_MODEL), _full((D_MODEL, D_MODEL))],
        out_specs=[row(S5_WIDTH), row(CONV_WIDTH), row(D_MODEL)],
        out_shape=[S((L, S5_WIDTH), f32), S((L, CONV_WIDTH), f32), S((L, D_MODEL), bf16)],
        compiler_params=_cp("parallel"),
    )(dx, w_out)


def _s5_discretise(lam_re, lam_im, log_dt, b_re, b_im):
    dt = jnp.exp(log_dt)
    mag = jnp.exp(lam_re * dt)
    abar_re = mag * jnp.cos(lam_im * dt)
    abar_im = mag * jnp.sin(lam_im * dt)
    den = lam_re * lam_re + lam_im * lam_im
    num_re = abar_re - 1.0
    f_re = ((num_re * lam_re + abar_im * lam_im) / den)[:, None, :]
    f_im = ((abar_im * lam_re - num_re * lam_im) / den)[:, None, :]
    return abar_re, abar_im, f_re * b_re - f_im * b_im, f_re * b_im + f_im * b_re


def _s5_params(lam_re, lam_im, log_dt, b_re, b_im):
    def body(lr, li, ld, br, bi, ar_ref, ai_ref, bbr_ref, bbi_ref):
        ar, ai, bbr, bbi = _s5_discretise(lr[...], li[...], ld[...], br[...], bi[...])
        ar_ref[...], ai_ref[...], bbr_ref[...], bbi_ref[...] = ar, ai, bbr, bbi

    gp = S((S5_GROUPS, S5_STATE), f32)
    gcp = S((S5_GROUPS, S5_GROUP_CH, S5_STATE), f32)
    return pl.pallas_call(body, name="s5_params", out_shape=[gp, gp, gcp, gcp])(lam_re, lam_im, log_dt, b_re, b_im)


def _s5_params_bwd(lam_re, lam_im, log_dt, b_re, b_im, d_ar, d_ai, d_bbr, d_bbi):
    def body(lr, li, ld, br, bi, car, cai, cbr, cbi, o_lr, o_li, o_ld, o_br, o_bi):
        _, vjp = jax.vjp(_s5_discretise, lr[...], li[...], ld[...], br[...], bi[...])
        o_lr[...], o_li[...], o_ld[...], o_br[...], o_bi[...] = vjp((car[...], cai[...], cbr[...], cbi[...]))

    gp = S((S5_GROUPS, S5_STATE), f32)
    gcp = S((S5_GROUPS, S5_GROUP_CH, S5_STATE), f32)
    return pl.pallas_call(body, name="s5_params_bwd", out_shape=[gp, gp, S((S5_GROUPS, 1), f32), gcp, gcp])(
        lam_re, lam_im, log_dt, b_re, b_im, d_ar, d_ai, d_bbr, d_bbi)


def _cmul(ar, ai, br, bi):
    return ar * br - ai * bi, ar * bi + ai * br


def _segment_starts(er, ei, ar, ai, steps, reverse):
    pr, pi = ar, ai
    n = 1
    while n < steps:
        pr, pi = _cmul(pr, pi, pr, pi)
        n *= 2
    assert n == steps
    row = lax.broadcasted_iota(jnp.int32, (SEGMENTS, SCAN_LANES), 0)
    hr = jnp.zeros((1, SCAN_LANES), f32)
    hi = jnp.zeros((1, SCAN_LANES), f32)
    out_r = jnp.zeros((SEGMENTS, SCAN_LANES), f32)
    out_i = jnp.zeros((SEGMENTS, SCAN_LANES), f32)
    order = range(SEGMENTS - 1, 0, -1) if reverse else range(0, SEGMENTS - 1)
    for r in order:
        qr, qi = _cmul(pr, pi, hr, hi)
        hr, hi = qr + er[r:r + 1, :], qi + ei[r:r + 1, :]
        nxt = r - 1 if reverse else r + 1
        out_r = jnp.where(row == nxt, hr, out_r)
        out_i = jnp.where(row == nxt, hi, out_i)
    return out_r, out_i


def _s5_read_bwd(dout, y_lin, u, d_skip, w_glu, b_glu, tm):
    L = u.shape[0]

    def body(do_ref, yl_ref, u_ref, d_ref, w_ref, b_ref, dyl_ref, du_ref, dd_ref, dw_ref, db_ref):
        @pl.when(pl.program_id(0) == 0)
        def _():
            dd_ref[...] = jnp.zeros_like(dd_ref)
            dw_ref[...] = jnp.zeros_like(dw_ref)
            db_ref[...] = jnp.zeros_like(db_ref)

        u, d, dout = u_ref[...], d_ref[...], do_ref[...]
        y, gelu_vjp = jax.vjp(_gelu, yl_ref[...] + d * u)
        yb = y.astype(bf16)
        sig = _sigmoid(_dot(yb, w_ref[...]) + b_ref[...])
        dz = dout * y * sig * (1.0 - sig)
        dzb = dz.astype(bf16)
        dy = dout * sig + _dot(dzb, w_ref[...], NT)
        (dyp,) = gelu_vjp(dy)
        dyl_ref[...] = dyp.astype(bf16)
        du_ref[...] = d * dyp
        dd_ref[...] += _rows8(dyp * u)
        db_ref[...] += _rows8(dz)
        dw_ref[...] += _dot(yb, dzb, TN)

    row = pl.BlockSpec((tm, S5_WIDTH), lambda i: (i, 0))
    vec = _full((1, S5_WIDTH))
    part = _full((8, S5_WIDTH))
    return pl.pallas_call(
        body, name="s5_read_bwd", grid=(L // tm,),
        in_specs=[row, row, row, vec, _full((S5_WIDTH, S5_WIDTH)), vec],
        out_specs=[row, row, part, _full((S5_WIDTH, S5_WIDTH)), part],
        out_shape=[S((L, S5_WIDTH), bf16), S((L, S5_WIDTH), f32), S((8, S5_WIDTH), f32),
                   S((S5_WIDTH, S5_WIDTH), f32), S((8, S5_WIDTH), f32)],
        compiler_params=_cp("arbitrary"),
    )(dout, y_lin, u, d_skip, w_glu, b_glu)


S5_CHUNK_CH = SCAN_LANES // S5_STATE * S5_GROUP_CH


def _s5_specs(L, bi, reverse):
    nb = L // (bi * SEGMENTS)
    blk = (lambda c, j: (nb - 1 - j, c)) if reverse else (lambda c, j: (j, c))
    chan = pl.BlockSpec((bi * SEGMENTS, S5_CHUNK_CH), blk)
    state = pl.BlockSpec((bi * SEGMENTS, SCAN_LANES), blk)
    mat = pl.BlockSpec((S5_CHUNK_CH, SCAN_LANES), lambda c, j: (c, c))
    vec = pl.BlockSpec((1, SCAN_LANES), lambda c, j: (0, c))
    tile = pl.BlockSpec((SEGMENTS, SCAN_LANES), lambda c, j: (0, c))
    return nb, chan, state, mat, vec, tile


def _drive_into(src, m_re, m_im, negate_im, dr_ref, di_ref, bi):
    d_im = _dot(src, m_im)
    dr_ref[...] = _dot(src, m_re).reshape(bi, SEGMENTS, SCAN_LANES)
    di_ref[...] = (-d_im if negate_im else d_im).reshape(bi, SEGMENTS, SCAN_LANES)


def _s5_ends(src, a_re, a_im, m_re, m_im, reverse, bi, name, ride=()):
    L = src.shape[0]
    nb, chan, _, mat, vec, tile = _s5_specs(L, bi, reverse)

    def body(src_ref, ar_ref, ai_ref, mr_ref, mi_ref, er_ref, ei_ref, dr_ref, di_ref):
        @pl.when(pl.program_id(1) == 0)
        def _():
            er_ref[...] = jnp.zeros_like(er_ref)
            ei_ref[...] = jnp.zeros_like(ei_ref)

        _drive_into(src_ref[...].astype(bf16), mr_ref[...], mi_ref[...], reverse, dr_ref, di_ref, bi)
        ar = jnp.broadcast_to(ar_ref[...], (SEGMENTS, SCAN_LANES))
        ai = jnp.broadcast_to(ai_ref[...], (SEGMENTS, SCAN_LANES))

        def step(n, c):
            i = (bi - 1 - n) if reverse else n
            pr, pi = _cmul(ar, ai, c[0], c[1])
            return pr + dr_ref[i], pi + di_ref[i]

        er_ref[...], ei_ref[...] = lax.fori_loop(0, bi, step, (er_ref[...], ei_ref[...]), unroll=4)

    out = S((SEGMENTS, S5_LANES), f32)
    return _pallas(
        body, ride=ride, name=name, grid=(S5_LANES // SCAN_LANES, nb),
        in_specs=[chan, vec, vec, mat, mat], out_specs=[tile, tile], out_shape=[out, out],
        scratch_shapes=[pltpu.VMEM((bi, SEGMENTS, SCAN_LANES), f32)] * 2,
        compiler_params=_cp("parallel", "arbitrary"),
    )(src, a_re, a_im, m_re, m_im)


def _s5_states(u, a_re, a_im, bb_re, bb_im, cc_re, cc_im, e_re, e_im, bi, ride=()):
    L = u.shape[0]
    nb, chan, state, mat, vec, tile = _s5_specs(L, bi, False)
    rows = bi * SEGMENTS

    def body(u_ref, ar_ref, ai_ref, br_ref, bi_ref, cr_ref, ci_ref, er_ref, ei_ref, sr_ref, si_ref, yl_ref,
             hr_ref, hi_ref, dr_ref, di_ref):
        @pl.when(pl.program_id(1) == 0)
        def _():
            hr_ref[...], hi_ref[...] = _segment_starts(er_ref[...], ei_ref[...], ar_ref[...], ai_ref[...], L // SEGMENTS, False)

        _drive_into(u_ref[...].astype(bf16), br_ref[...], bi_ref[...], False, dr_ref, di_ref, bi)
        ar = jnp.broadcast_to(ar_ref[...], (SEGMENTS, SCAN_LANES))
        ai = jnp.broadcast_to(ai_ref[...], (SEGMENTS, SCAN_LANES))

        def step(i, c):
            pr, pi = _cmul(ar, ai, c[0], c[1])
            nr, nim = pr + dr_ref[i], pi + di_ref[i]
            dr_ref[i] = nr
            di_ref[i] = nim
            return nr, nim

        hr_ref[...], hi_ref[...] = lax.fori_loop(0, bi, step, (hr_ref[...], hi_ref[...]), unroll=4)
        sr = dr_ref[...].reshape(rows, SCAN_LANES).astype(bf16)
        si = di_ref[...].reshape(rows, SCAN_LANES).astype(bf16)
        sr_ref[...] = sr
        si_ref[...] = si
        yl_ref[...] = _dot(sr, cr_ref[...], NT) - _dot(si, ci_ref[...], NT)

    return _pallas(
        body, ride=ride, name="s5_states", grid=(S5_LANES // SCAN_LANES, nb),
        in_specs=[chan, vec, vec, mat, mat, mat, mat, tile, tile],
        out_specs=[state, state, chan],
        out_shape=[S((L, S5_LANES), bf16)] * 2 + [S((L, S5_WIDTH), f32)],
        scratch_shapes=[pltpu.VMEM((SEGMENTS, SCAN_LANES), f32)] * 2 + [pltpu.VMEM((bi, SEGMENTS, SCAN_LANES), f32)] * 2,
        compiler_params=_cp("parallel", "arbitrary"),
    )(u, a_re, a_im, bb_re, bb_im, cc_re, cc_im, e_re, e_im)


def _s5_states_bwd(dy, u, du_skip, s_re, s_im, a_re, a_im, bb_re, bb_im, cc_re, cc_im, e_re, e_im, bi, ride=()):
    L = u.shape[0]
    nb, chan, state, mat, vec, tile = _s5_specs(L, bi, True)
    rows = bi * SEGMENTS
    per = rows // 16

    def body(dy_ref, u_ref, dus_ref, sr_ref, si_ref, pr_ref, pi_ref, lr_ref, li_ref, ar_ref, ai_ref, br_ref, bi_ref, cr_ref,
             ci_ref, er_ref, ei_ref, du_ref, dar_ref, dai_ref, dbr_ref, dbi_ref, dcr_ref, dci_ref,
             hr_ref, hi_ref, gr_ref, gi_ref, fr_ref, fi_ref):
        j = pl.program_id(1)

        @pl.when(j == 0)
        def _():
            hr_ref[...], hi_ref[...] = _segment_starts(er_ref[...], ei_ref[...], ar_ref[...], ai_ref[...], L // SEGMENTS, True)
            for ref in (dar_ref, dai_ref, dbr_ref, dbi_ref, dcr_ref, dci_ref):
                ref[...] = jnp.zeros_like(ref)

        dy = dy_ref[...]
        sr, si = sr_ref[...], si_ref[...]
        _drive_into(dy, cr_ref[...], ci_ref[...], True, gr_ref, gi_ref, bi)
        fr_ref[...] = sr.astype(f32).reshape(bi, SEGMENTS, SCAN_LANES)
        fi_ref[...] = si.astype(f32).reshape(bi, SEGMENTS, SCAN_LANES)
        ar = jnp.broadcast_to(ar_ref[...], (SEGMENTS, SCAN_LANES))
        ai = jnp.broadcast_to(ai_ref[...], (SEGMENTS, SCAN_LANES))

        def step(n, c):
            i = bi - 1 - n
            gr, gi, accr, acci = c
            qr, qi = _cmul(ar, ai, gr, gi)
            gr, gi = qr + gr_ref[i], qi + gi_ref[i]
            gr_ref[i] = gr
            gi_ref[i] = gi
            pr, pi = fr_ref[i - 1], fi_ref[i - 1]
            return gr, gi, accr + (gr * pr + gi * pi), acci + (gi * pr - gr * pi)

        gr, gi, accr, acci = lax.fori_loop(0, bi - 1, step, (hr_ref[...], hi_ref[...], dar_ref[...], dai_ref[...]), unroll=3)
        qr, qi = _cmul(ar, ai, gr, gi)
        gr, gi = qr + gr_ref[0], qi + gi_ref[0]
        gr_ref[0] = gr
        gi_ref[0] = gi
        hr_ref[...], hi_ref[...] = gr, gi
        row = lax.broadcasted_iota(jnp.int32, (SEGMENTS, SCAN_LANES), 0)
        first = j == nb - 1
        older = lambda ref: ref[...].astype(f32)[SEGMENTS:, :]
        wrap_r = jnp.where(row == 0, 0.0, pltpu.roll(older(lr_ref), 1, 0))
        wrap_i = jnp.where(row == 0, 0.0, pltpu.roll(older(li_ref), 1, 0))
        pr = jnp.where(first, wrap_r, older(pr_ref))
        pi = jnp.where(first, wrap_i, older(pi_ref))
        dar_ref[...] = accr + gr * pr + gi * pi
        dai_ref[...] = acci + gi * pr - gr * pi

        g_re = gr_ref[...].reshape(rows, SCAN_LANES).astype(bf16)
        g_im = gi_ref[...].reshape(rows, SCAN_LANES).astype(bf16)
        ub = u_ref[...].astype(bf16)
        du_ref[...] = dus_ref[...] + _dot(g_re, br_ref[...], NT) + _dot(g_im, bi_ref[...], NT)
        dbr_ref[...] += _dot(ub, g_re, TN)
        dbi_ref[...] += _dot(ub, g_im, TN)
        dcr_ref[...] += _dot(dy, sr, TN)
        dci_ref[...] -= _dot(dy, si, TN)

    prev = pl.BlockSpec((16, SCAN_LANES), lambda c, j: (jnp.maximum((nb - 1 - j) * per - 1, 0), c))
    last = pl.BlockSpec((16, SCAN_LANES), lambda c, j: (L // 16 - 1, c))
    grad = pl.BlockSpec((S5_CHUNK_CH, SCAN_LANES), lambda c, j: (c, 0))
    big = pltpu.VMEM((bi, SEGMENTS, SCAN_LANES), f32)
    return _pallas(
        body, ride=ride, name="s5_states_bwd", grid=(S5_LANES // SCAN_LANES, nb),
        in_specs=[chan, chan, chan, state, state, prev, prev, last, last, vec, vec, mat, mat, mat, mat, tile, tile],
        out_specs=[chan, tile, tile, grad, grad, grad, grad],
        out_shape=[S((L, S5_WIDTH), f32)] + [S((SEGMENTS, S5_LANES), f32)] * 2 + [S((S5_WIDTH, SCAN_LANES), f32)] * 4,
        scratch_shapes=[pltpu.VMEM((SEGMENTS, SCAN_LANES), f32)] * 2 + [big] * 4,
        compiler_params=_cp("parallel", "arbitrary"),
    )(dy, u, du_skip, s_re, s_im, s_re, s_im, s_re, s_im, a_re, a_im, bb_re, bb_im, cc_re, cc_im, e_re, e_im)


def _s5_gate(y_lin, u, d_skip, w_glu, b_glu, tm, ride=()):
    L = u.shape[0]

    def body(yl_ref, u_ref, d_ref, w_ref, b_ref, o_ref):
        y = _gelu(yl_ref[...] + d_ref[...] * u_ref[...])
        z = _dot(y.astype(bf16), w_ref[...]) + b_ref[...]
        o_ref[...] = (y * _sigmoid(z)).astype(bf16)

    row = pl.BlockSpec((tm, S5_WIDTH), lambda i: (i, 0))
    vec = _full((1, S5_WIDTH))
    return _pallas(
        body, ride=ride, name="s5_gate", grid=(L // tm,),
        in_specs=[row, row, vec, _full((S5_WIDTH, S5_WIDTH)), vec],
        out_specs=row, out_shape=S((L, S5_WIDTH), bf16),
        compiler_params=_cp("parallel"),
    )(y_lin, u, d_skip, w_glu, b_glu)


def _group_mean(x, avg):
    hi = x.astype(bf16)
    lo = (x - hi.astype(f32)).astype(bf16)
    return _dot(hi, avg) + _dot(lo, avg)


def _conv_act(zn, ln_g, ln_b):
    t = zn * ln_g + ln_b
    return t * _sigmoid(t)


def _glu_padded(v_ref, halo_ref, zpad_ref, tm):
    v = v_ref[...]
    vh = halo_ref[...]
    zh = vh[:, :CONV_WIDTH] * _sigmoid(vh[:, CONV_WIDTH:])
    zpad_ref[:CONV_HALO, :] = jnp.where(pl.program_id(0) > 0, zh, 0.0)
    zpad_ref[CONV_HALO:CONV_HALO + tm, :] = v[:, :CONV_WIDTH] * _sigmoid(v[:, CONV_WIDTH:])
    zpad_ref[CONV_HALO + tm:, :] = jnp.zeros((8, CONV_WIDTH), f32)


def _shifted(pad_ref, sh_ref, tm):
    for b in range(8):
        sh_ref[b] = pad_ref[pl.ds(b, tm + CONV_HALO), :]


def _window(sh_ref, r0, off, rows):
    return sh_ref[off % 8, pl.ds(pl.multiple_of(r0 + 8 * (off // 8), 8), rows), :]


def _tap_sum(w_ref, sh_ref, taps, out_ref, tm, bias):
    def chunk(c, carry):
        r0 = pl.multiple_of(c * CONV_ROWS, CONV_ROWS)
        acc = jnp.zeros((CONV_ROWS, CONV_WIDTH), f32) + bias
        for k, off in taps:
            acc = acc + w_ref[k:k + 1, :] * _window(sh_ref, r0, off, CONV_ROWS)
        out_ref[pl.ds(r0, CONV_ROWS), :] = acc
        return carry

    lax.fori_loop(0, tm // CONV_ROWS, chunk, 0)


FWD_TAPS = [(k, CONV_HALO - (CONV_K - 1) + k) for k in range(CONV_K)]
BWD_TAPS = [(k, CONV_K - 1 - k) for k in range(CONV_K)]


def _conv_specs(tm):
    per = tm // CONV_HALO
    vrow = pl.BlockSpec((tm, 2 * CONV_WIDTH), lambda i: (i, 0))
    vhalo = pl.BlockSpec((CONV_HALO, 2 * CONV_WIDTH), lambda i: (jnp.maximum(i * per - 1, 0), 0))
    return vrow, vhalo


def _conv_scratch(tm):
    return [pltpu.VMEM((tm + CONV_HALO + 8, CONV_WIDTH), f32), pltpu.VMEM((8, tm + CONV_HALO, CONV_WIDTH), f32)]


def _conv_fwd(v, w_dw, b_dw, ln_g, ln_b, avg, tm, ride=()):
    L = v.shape[0]

    def body(v_ref, halo_ref, w_ref, b_ref, g_ref, bb_ref, avg_ref, o_ref, zc_ref, zpad_ref, zs_ref):
        _glu_padded(v_ref, halo_ref, zpad_ref, tm)
        _shifted(zpad_ref, zs_ref, tm)
        _tap_sum(w_ref, zs_ref, FWD_TAPS, zc_ref, tm, b_ref[...])
        zc = zc_ref[...]
        xc = zc - _group_mean(zc, avg_ref[...])
        zn = xc * lax.rsqrt(_group_mean(xc * xc, avg_ref[...]) + EPS)
        o_ref[...] = _conv_act(zn, g_ref[...], bb_ref[...]).astype(bf16)

    vrow, vhalo = _conv_specs(tm)
    vec = _full((1, CONV_WIDTH))
    row = pl.BlockSpec((tm, CONV_WIDTH), lambda i: (i, 0))
    return _pallas(
        body, ride=ride, name="conv_fwd", grid=(L // tm,),
        in_specs=[vrow, vhalo, _full((CONV_HALO, CONV_WIDTH)), vec, vec, vec, _full((CONV_WIDTH, CONV_WIDTH))],
        out_specs=[row, row], out_shape=[S((L, CONV_WIDTH), bf16), S((L, CONV_WIDTH), f32)],
        scratch_shapes=_conv_scratch(tm),
        compiler_params=_cp("arbitrary"),
    )(v, v, w_dw, b_dw, ln_g, ln_b, avg)


def _conv_bwd_norm(dout, zc, ln_g, ln_b, avg, tm):
    L = zc.shape[0]

    def body(do_ref, zc_ref, g_ref, bb_ref, avg_ref, dzc_ref, dg_ref, db_ref, dbd_ref):
        @pl.when(pl.program_id(0) == 0)
        def _():
            dg_ref[...] = jnp.zeros_like(dg_ref)
            db_ref[...] = jnp.zeros_like(db_ref)
            dbd_ref[...] = jnp.zeros_like(dbd_ref)

        avg = avg_ref[...]
        zc = zc_ref[...]
        xc = zc - _group_mean(zc, avg)
        rstd = lax.rsqrt(_group_mean(xc * xc, avg) + EPS)
        xhat = xc * rstd
        _, act_vjp = jax.vjp(_conv_act, xhat, g_ref[...], bb_ref[...])
        dxhat, dg, db = act_vjp(do_ref[...])
        dzc = rstd * (dxhat - _group_mean(dxhat, avg) - xhat * _group_mean(dxhat * xhat, avg))
        dzc_ref[...] = dzc
        dg_ref[0:1, :] += dg
        db_ref[0:1, :] += db
        dbd_ref[...] += _rows8(dzc)

    vec = _full((1, CONV_WIDTH))
    row = pl.BlockSpec((tm, CONV_WIDTH), lambda i: (i, 0))
    part = _full((8, CONV_WIDTH))
    return pl.pallas_call(
        body, name="conv_bwd_norm", grid=(L // tm,),
        in_specs=[row, row, vec, vec, _full((CONV_WIDTH, CONV_WIDTH))],
        out_specs=[row, part, part, part],
        out_shape=[S((L, CONV_WIDTH), f32)] + [S((8, CONV_WIDTH), f32)] * 3,
        compiler_params=_cp("arbitrary"),
    )(dout, zc, ln_g, ln_b, avg)


def _conv_bwd_taps(dzc, v, w_dw, tm, ride=()):
    L = v.shape[0]
    nt = L // tm
    per = tm // CONV_HALO

    def body(d_ref, dn_ref, v_ref, halo_ref, w_ref, dv_ref, dw_ref, zpad_ref, zs_ref, dpad_ref, ds_ref, dz_ref):
        i = pl.program_id(0)

        @pl.when(i == 0)
        def _():
            dw_ref[...] = jnp.zeros_like(dw_ref)

        _glu_padded(v_ref, halo_ref, zpad_ref, tm)
        _shifted(zpad_ref, zs_ref, tm)
        dpad_ref[:tm, :] = d_ref[...]
        dpad_ref[tm:tm + CONV_HALO, :] = jnp.where(i < nt - 1, dn_ref[...], 0.0)
        dpad_ref[tm + CONV_HALO:, :] = jnp.zeros((8, CONV_WIDTH), f32)
        _shifted(dpad_ref, ds_ref, tm)
        _tap_sum(w_ref, ds_ref, BWD_TAPS, dz_ref, tm, 0.0)

        for first in range(0, CONV_K, 8):
            taps = FWD_TAPS[first:first + 8]

            def chunk(c, accs, taps=taps):
                r0 = pl.multiple_of(c * 8, 8)
                d = d_ref[pl.ds(r0, 8), :]
                return tuple(acc + d * _window(zs_ref, r0, off, 8) for acc, (_, off) in zip(accs, taps))

            accs = lax.fori_loop(0, tm // 8, chunk, tuple(jnp.zeros((8, CONV_WIDTH), f32) for _ in taps), unroll=2)
            for acc, (k, _) in zip(accs, taps):
                dw_ref[k] += acc

        dz = dz_ref[...]
        v = v_ref[...]
        sig = _sigmoid(v[:, CONV_WIDTH:])
        dv_ref[:, :CONV_WIDTH] = dz * sig
        dv_ref[:, CONV_WIDTH:] = dz * v[:, :CONV_WIDTH] * sig * (1.0 - sig)

    vrow, vhalo = _conv_specs(tm)
    row = pl.BlockSpec((tm, CONV_WIDTH), lambda i: (i, 0))
    nxt = pl.BlockSpec((CONV_HALO, CONV_WIDTH), lambda i: (jnp.minimum((i + 1) * per, nt * per - 1), 0))
    return _pallas(
        body, ride=ride, name="conv_bwd_taps", grid=(nt,),
        in_specs=[row, nxt, vrow, vhalo, _full((CONV_HALO, CONV_WIDTH))],
        out_specs=[vrow, _full((CONV_HALO, 8, CONV_WIDTH))],
        out_shape=[S((L, 2 * CONV_WIDTH), f32), S((CONV_HALO, 8, CONV_WIDTH), f32)],
        scratch_shapes=_conv_scratch(tm) * 2 + [pltpu.VMEM((tm, CONV_WIDTH), f32)],
        compiler_params=_cp("arbitrary"),
    )(dzc, dzc, v, v, w_dw)


def _loss_head(x, target, g, tm):
    L = x.shape[0]

    def body(x_ref, t_ref, g_ref, dx_ref, dg_ref, l_ref):
        @pl.when(pl.program_id(0) == 0)
        def _():
            dg_ref[...] = jnp.zeros_like(dg_ref)
            l_ref[...] = jnp.zeros_like(l_ref)

        x, g = x_ref[...], g_ref[...]
        e = _rms(x, g) - t_ref[...]
        l_ref[...] += _rows8(e * e) * (0.5 / D_MODEL)
        dx, dg = _rms_bwd(x, g, e * (1.0 / D_MODEL))
        dx_ref[...] = dx
        dg_ref[...] += dg

    row = pl.BlockSpec((tm, D_MODEL), lambda i: (i, 0))
    return pl.pallas_call(
        body, name="loss_head", grid=(L // tm,),
        in_specs=[row, row, _full((1, D_MODEL))],
        out_specs=[row, _full((1, D_MODEL)), _full((8, D_MODEL))],
        out_shape=[S((L, D_MODEL), f32), S((1, D_MODEL), f32), S((8, D_MODEL), f32)],
        compiler_params=_cp("arbitrary"),
    )(x, target, g)


def _to_segments(a):
    L, c = a.shape
    return a.reshape(SEGMENTS, L // SEGMENTS, c).transpose(1, 0, 2).reshape(L, c)


def _from_segments(a):
    L, c = a.shape
    return a.reshape(L // SEGMENTS, SEGMENTS, c).transpose(1, 0, 2).reshape(L, c)


def _block_diag(ms):
    n = len(ms)

    def body(*refs):
        for a in range(n):
            out = refs[n + a]
            out[...] = jnp.zeros_like(out)
            for g in range(S5_GROUPS):
                rows = slice(g * S5_GROUP_CH, (g + 1) * S5_GROUP_CH)
                out[rows, g * S5_STATE:(g + 1) * S5_STATE] = refs[a][rows, :].astype(bf16)

    return pl.pallas_call(body, name="s5_block_diag", out_shape=[S((S5_WIDTH, S5_LANES), bf16)] * n,
                          compiler_params=pltpu.CompilerParams(vmem_limit_bytes=VMEM_LIMIT))(
        *[m.reshape(S5_WIDTH, S5_STATE) for m in ms])


def _diag_blocks(ms):
    n = len(ms)
    per_chunk = SCAN_LANES // S5_STATE

    def body(*refs):
        for a in range(n):
            for g in range(S5_GROUPS):
                rows = slice(g * S5_GROUP_CH, (g + 1) * S5_GROUP_CH)
                at = g % per_chunk * S5_STATE
                refs[n + a][rows, :] = refs[a][rows, at:at + S5_STATE]

    out = pl.pallas_call(body, name="s5_diag_blocks", out_shape=[S((S5_WIDTH, S5_STATE), f32)] * n,
                         compiler_params=pltpu.CompilerParams(vmem_limit_bytes=VMEM_LIMIT))(*ms)
    return [o.reshape(S5_GROUPS, S5_GROUP_CH, S5_STATE) for o in out]


class _NoExchanges:
    def before(self, point):
        return ()

    def after(self, point):
        pass

    def alone(self, point):
        pass


def _ffn_block(x, p, tag, tm, sched):
    point = tag + "_up"
    h, dadg, dadu, a = _ffn_up(x, p[tag + "_norm"], p[tag + "_w_gate"], p[tag + "_w_up"], tm, 768, tag, ride=sched.before(point))
    sched.after(point)
    return _ffn_down(x, a, p[tag + "_w_down"], tm, tag), (h, dadg, dadu, a)


def _ffn_block_bwd(dxo, x, p, tag, saved, tm, grads, sched):
    h, dadg, dadu, a = saved
    dgate, dup, dxh = _ffn_bwd_act(dxo, p[tag + "_w_down"], dadg, dadu, tm, 768, tag, ride=sched.before(tag + "_bwd_act"))
    sched.after(tag + "_bwd_act")
    for which, lhs, rhs in (("gate", h, dgate), ("up", h, dup), ("down", a, dxh)):
        point = tag + "_dw_" + which
        grads[tag + "_w_" + which] = _mm_tn(lhs, rhs, bf16, point, ride=sched.before(point))
        sched.after(point)
    dx, grads[tag + "_norm"] = _ffn_bwd_in(dxo, x, p[tag + "_norm"], dgate, dup, p[tag + "_w_gate"], p[tag + "_w_up"], tm, tag,
                                           ride=sched.before(tag + "_bwd_in"))
    sched.after(tag + "_bwd_in")
    return dx


def _local_step(x, target, p, grads, sched):
    L = x.shape[0]
    tm = min(512, L)
    ts = min(256, L)
    ni = L // SEGMENTS
    bi = min(64, ni)

    def carried(point, fn, *args):
        out = fn(*args, ride=sched.before(point))
        sched.after(point)
        return out

    sched.alone("start")
    x1, saved1 = _ffn_block(x, p, "ffn1", tm, sched)

    h2, u_s5, v = _mix_in(x1, p["mix_norm"], p["w_in"], tm)
    b_t = lambda b: b.transpose(0, 2, 1)
    s5_in = (p["s5_lam_re"], p["s5_lam_im"], p["s5_log_dt"].reshape(S5_GROUPS, 1), b_t(p["s5_b_re"]), b_t(p["s5_b_im"]))
    abar_re, abar_im, bbar_re, bbar_im = _s5_params(*s5_in)
    a_re, a_im = abar_re.reshape(1, S5_LANES), abar_im.reshape(1, S5_LANES)
    bb_re, bb_im, cc_re, cc_im = _block_diag([bbar_re, bbar_im, p["s5_c_re"], p["s5_c_im"]])
    u_seg = _to_segments(u_s5)
    e_re, e_im = carried("s5_ends", _s5_ends, u_seg, a_re, a_im, bb_re, bb_im, False, bi, "s5_ends")
    s_re, s_im, y_lin = carried("s5_states", _s5_states, u_seg, a_re, a_im, bb_re, bb_im, cc_re, cc_im, e_re, e_im, bi)
    y_s5 = _from_segments(_s5_gate(y_lin, u_seg, p["s5_d"], p["s5_w_glu"], p["s5_b_glu"], tm))
    w_dw = jnp.pad(p["conv_w_dw"], ((0, CONV_HALO - CONV_K), (0, 0)))
    heads = jnp.arange(CONV_WIDTH) // CONV_HEAD
    avg = ((heads[:, None] == heads[None, :]).astype(f32) / CONV_HEAD).astype(bf16)
    y_conv, zc = carried("conv_fwd", _conv_fwd, v, w_dw, p["conv_b_dw"], p["conv_ln_g"], p["conv_ln_b"], avg, tm)
    x2 = _mix_out(x1, y_s5, y_conv, p["w_out"], tm)

    x3, saved2 = _ffn_block(x2, p, "ffn2", tm, sched)
    dx3, grads["final_norm"], loss_terms = _loss_head(x3, target, p["final_norm"].reshape(1, D_MODEL), tm)

    dx2 = _ffn_block_bwd(dx3, x2, p, "ffn2", saved2, tm, grads, sched)

    dy_s5, dy_conv, dx2b = carried("mix_out_bwd", _mix_out_bwd, dx2, p["w_out"], tm)
    grads["w_out"] = jnp.concatenate([_mm_tn(y_s5, dx2b, bf16, "dw_out_s5"), _mm_tn(y_conv, dx2b, bf16, "dw_out_conv")], axis=0)
    dy_lin, du_skip, dd8, grads["s5_w_glu"], dbg8 = _s5_read_bwd(
        _to_segments(dy_s5), y_lin, u_seg, p["s5_d"], p["s5_w_glu"], p["s5_b_glu"], tm)
    grads["s5_d"] = dd8.sum(axis=0, keepdims=True)
    grads["s5_b_glu"] = dbg8.sum(axis=0, keepdims=True)
    g_e_re, g_e_im = carried("s5_ends_bwd", _s5_ends, dy_lin, a_re, -a_im, cc_re, cc_im, True, bi, "s5_ends_bwd")
    du_seg, da_re8, da_im8, dbb_re, dbb_im, dcc_re, dcc_im = carried(
        "s5_states_bwd", _s5_states_bwd, dy_lin, u_seg, du_skip, s_re, s_im, a_re, -a_im, bb_re, bb_im, cc_re, cc_im,
        g_e_re, g_e_im, bi)
    d_abar = lambda a8: a8.sum(axis=0).reshape(S5_GROUPS, S5_STATE)
    grads["s5_c_re"], grads["s5_c_im"], d_bbr, d_bbi = _diag_blocks([dcc_re, dcc_im, dbb_re, dbb_im])
    d_lr, d_li, d_ld, d_br, d_bi = _s5_params_bwd(*s5_in, d_abar(da_re8), d_abar(da_im8), d_bbr, d_bbi)
    grads["s5_lam_re"], grads["s5_lam_im"], grads["s5_log_dt"] = d_lr, d_li, d_ld.reshape(1, S5_GROUPS)
    grads["s5_b_re"], grads["s5_b_im"] = b_t(d_br), b_t(d_bi)
    dzc, dlg8, dlb8, dbd8 = _conv_bwd_norm(dy_conv, zc, p["conv_ln_g"], p["conv_ln_b"], avg, tm)
    grads["conv_ln_g"] = dlg8.sum(axis=0, keepdims=True)
    grads["conv_ln_b"] = dlb8.sum(axis=0, keepdims=True)
    grads["conv_b_dw"] = dbd8.sum(axis=0, keepdims=True)
    dv, dw8 = carried("conv_bwd_taps", _conv_bwd_taps, dzc, v, w_dw, tm)
    grads["conv_w_dw"] = dw8.sum(axis=1)[:CONV_K]
    dx1, grads["mix_norm"], dub = _mix_in_bwd(dx2, x1, p["mix_norm"], _from_segments(du_seg), dv, p["w_in"], tm)
    grads["w_in"] = _mm_tn(h2, dub, bf16, "dw_in")

    grads["loss_terms"] = loss_terms
    dx0 = _ffn_block_bwd(dx1, x, p, "ffn1", saved1, tm, grads, sched)
    sched.alone("tail")
    return loss_terms, dx0


MESH = pl.DeviceIdType.MESH
ANY = pl.BlockSpec(memory_space=pl.ANY)


def _place():
    return lax.axis_index("x"), lax.axis_index("y"), lax.axis_index("c")


class _Exchange:
    def __init__(self, ins, out_shape, sems, start, finish):
        self.ins, self.out_shape, self.sems, self.start, self.finish = list(ins), list(out_shape), list(sems), start, finish
        self.out = None


def _pallas(body, *, ride=(), **kw):
    if not ride:
        return pl.pallas_call(body, **kw)

    def run(*args):
        out_shape = kw.get("out_shape", [])
        single = not isinstance(out_shape, (list, tuple))
        shapes = [out_shape] if single else list(out_shape)
        out_specs = [kw["out_specs"]] if single else list(kw.get("out_specs", []))
        grid = tuple(kw.get("grid", ()))
        scratch = list(kw.get("scratch_shapes", ()))
        n_in, n_out, n_scr = len(args), len(shapes), len(scratch)
        r_in = [len(e.ins) for e in ride]
        r_out = [len(e.out_shape) for e in ride]
        r_sem = [len(e.sems) for e in ride]

        def wrapped(*refs):
            own_in, refs = refs[:n_in], refs[n_in:]
            ex_in, refs = refs[:sum(r_in)], refs[sum(r_in):]
            own_out, refs = refs[:n_out], refs[n_out:]
            ex_out, refs = refs[:sum(r_out)], refs[sum(r_out):]
            own_scr, ex_sem = refs[:n_scr], refs[n_scr:]
            parts = []
            for e, ni, no, ns in zip(ride, r_in, r_out, r_sem):
                parts.append((e, ex_in[:ni], ex_out[:no], ex_sem[:ns]))
                ex_in, ex_out, ex_sem = ex_in[ni:], ex_out[no:], ex_sem[ns:]

            def at(step):
                def go():
                    for e, i, o, s in parts:
                        getattr(e, step)(i, o, s)
                if grid:
                    ids = [pl.program_id(d) for d in range(len(grid))]
                    when = [i == (0 if step == "start" else g - 1) for i, g in zip(ids, grid)]
                    pl.when(functools.reduce(lambda a, b: a & b, when))(go)
                else:
                    go()

            at("start")
            if body is not None:
                body(*own_in, *own_out, *own_scr)
            at("finish")

        outs = pl.pallas_call(
            wrapped, name=kw["name"], grid=grid,
            in_specs=list(kw.get("in_specs", [])) + [ANY] * sum(r_in),
            out_specs=out_specs + [ANY] * sum(r_out),
            out_shape=shapes + [s for e in ride for s in e.out_shape],
            scratch_shapes=scratch + [s for e in ride for s in e.sems],
            compiler_params=_cp(*["arbitrary"] * len(grid)),
        )(*args, *[a for e in ride for a in e.ins])
        own, rest = outs[:n_out], outs[n_out:]
        for e, no in zip(ride, r_out):
            e.out, rest = list(rest[:no]), rest[no:]
        return own[0] if single else own

    return run


def _exchange(ride, name):
    _pallas(None, ride=ride, name=name)()


def _gather(arrs):
    n = len(arrs)

    def copies(ins, outs, sems):
        send_sems, recv_sems, local_sems = sems
        x, y, c = _place()
        me, sibling = (x, y, c), (x, y, 1 - c)
        chips = [(1 - x, y), (x, 1 - y), (1 - x, 1 - y)]

        def copy(a, k, block, to, src=None):
            px, py, pc = block
            dst = outs[a].at[4 * px + 2 * py + pc]
            return pltpu.make_async_remote_copy(
                src_ref=dst if src is None else src, dst_ref=dst, send_sem=send_sems.at[7 * a + k],
                recv_sem=recv_sems.at[7 * a + k], device_id=to, device_id_type=MESH)

        def own():
            local = [pltpu.make_async_copy(ins[a], outs[a].at[4 * x + 2 * y + c], local_sems.at[a]) for a in range(n)]
            remote = []
            for a in range(n):
                remote.append(copy(a, 0, me, sibling, src=ins[a]))
                remote += [copy(a, 1 + j, me, (*chip, c), src=ins[a]) for j, chip in enumerate(chips)]
            return local, remote

        return c, me, sibling, chips, copy, own

    def start(ins, outs, sems):
        local, remote = copies(ins, outs, sems)[-1]()
        for cp in local + remote:
            cp.start()

    def finish(ins, outs, sems):
        c, me, sibling, chips, copy, own = copies(ins, outs, sems)
        passed = []
        for j, chip in enumerate(chips):
            for a in range(n):
                copy(a, 1 + j, (*chip, c), me).wait_recv()
                passed.append(copy(a, 4 + j, (*chip, c), sibling))
                passed[-1].start()
        for a in range(n):
            copy(a, 0, sibling, me).wait_recv()
            for j, chip in enumerate(chips):
                copy(a, 4 + j, (*chip, 1 - c), me).wait_recv()
        local, remote = own()
        for cp in remote + passed:
            cp.wait_send()
        for cp in local:
            cp.wait()

    dma = pltpu.SemaphoreType.DMA
    return _Exchange(arrs, [S((N_DEV, *a.shape), a.dtype) for a in arrs], [dma((7 * n,)), dma((7 * n,)), dma((n,))], start, finish)


def _swap_with_sibling(gs):
    n = len(gs)

    def copies(ins, outs, sems):
        x, y, c = _place()
        return [pltpu.make_async_remote_copy(
            src_ref=ins[a].at[:, 1 - c], dst_ref=outs[a], send_sem=sems[0].at[a], recv_sem=sems[1].at[a],
            device_id=(x, y, 1 - c), device_id_type=MESH) for a in range(n)]

    def start(ins, outs, sems):
        for cp in copies(ins, outs, sems):
            cp.start()

    def finish(ins, outs, sems):
        for cp in copies(ins, outs, sems):
            cp.wait()

    dma = pltpu.SemaphoreType.DMA
    return _Exchange(gs, [S((N_CHIP, *g.shape[2:]), g.dtype) for g in gs], [dma((n,)), dma((n,))], start, finish)


def _swap_with_chips(ps):
    n = len(ps)

    def copies(ins, outs, sems):
        x, y, c = _place()
        q = 2 * x + y
        peers = [(x, 1 - y), (1 - x, y), (1 - x, 1 - y)]

        def copy(a, j, slot_from, slot_to):
            px, py = peers[j]
            return pltpu.make_async_remote_copy(
                src_ref=ins[a].at[slot_from], dst_ref=outs[a].at[slot_to], send_sem=sems[0].at[3 * a + j],
                recv_sem=sems[1].at[3 * a + j], device_id=(px, py, c), device_id_type=MESH)

        sends = lambda: [copy(a, j, 2 * peers[j][0] + peers[j][1], q) for a in range(n) for j in range(3)]
        lands = lambda: [copy(a, j, q, 2 * peers[j][0] + peers[j][1]) for a in range(n) for j in range(3)]
        return sends, lands

    def start(ins, outs, sems):
        for cp in copies(ins, outs, sems)[0]():
            cp.start()

    def finish(ins, outs, sems):
        sends, lands = copies(ins, outs, sems)
        for cp in lands():
            cp.wait_recv()
        for cp in sends():
            cp.wait_send()

    dma = pltpu.SemaphoreType.DMA
    return _Exchange(ps, [S(p.shape, p.dtype) for p in ps], [dma((3 * n,)), dma((3 * n,))], start, finish)


def _row_tile(rows, cols, itemsize):
    t = rows
    while t * cols * itemsize > (1 << 20) and t % 32 == 0:
        t //= 2
    return t


def _add_sibling(g4, st, core, name):
    _, _, R, C = g4.shape
    tr = _row_tile(R, C, 4)

    def body(c_ref, g_ref, s_ref, o_ref):
        o_ref[...] = (g_ref[...].astype(f32) + s_ref[...].astype(f32)).astype(bf16)

    return pl.pallas_call(
        body, name=name,
        grid_spec=pltpu.PrefetchScalarGridSpec(
            num_scalar_prefetch=1, grid=(N_CHIP, R // tr),
            in_specs=[pl.BlockSpec((None, None, tr, C), lambda q, i, c: (q, c[0], i, 0)),
                      pl.BlockSpec((None, tr, C), lambda q, i, c: (q, i, 0))],
            out_specs=pl.BlockSpec((None, tr, C), lambda q, i, c: (q, i, 0))),
        out_shape=S((N_CHIP, R, C), bf16),
        compiler_params=_cp("parallel", "parallel"),
    )(core, g4, st)


def _adamw(w, g, m, v):
    m = B1 * m + (1.0 - B1) * g
    v = B2 * v + (1.0 - B2) * (g * g)
    m_hat = m / (1.0 - B1 ** STEP)
    v_hat = v / (1.0 - B2 ** STEP)
    return -LR * (m_hat / (jnp.sqrt(v_hat) + ADAM_EPS) + WD * w), m, v


def _adam_sharded(w, m, v, part, got, slots, name):
    R, C = w.shape
    _, Rp, Cp = part.shape
    tr = _row_tile(R, Cp, 4) if Rp == R else R

    def body(s_ref, w_ref, m_ref, v_ref, p_ref, a_ref, b_ref, c_ref, g_out, d_out, m_out, v_out):
        g = p_ref[...].astype(f32) + a_ref[...].astype(f32) + b_ref[...].astype(f32) + c_ref[...].astype(f32)
        g = g[:, :C]
        g_out[...] = g
        d_out[...], m_out[...], v_out[...] = _adamw(w_ref[...], g, m_ref[...], v_ref[...])

    shard = pl.BlockSpec((tr, C), lambda i, s: (i, 0))
    slot = lambda k: pl.BlockSpec((None, tr, Cp), lambda i, s: (s[k], i, 0))
    return pl.pallas_call(
        body, name=name,
        grid_spec=pltpu.PrefetchScalarGridSpec(
            num_scalar_prefetch=1, grid=(R // tr,),
            in_specs=[shard, shard, shard, slot(0), slot(1), slot(2), slot(3)],
            out_specs=[shard] * 4),
        out_shape=[S((R, C), f32)] * 4,
        compiler_params=_cp("parallel"),
    )(slots, w, m, v, part, got, got, got)


def _adam_replicated(w, m, v, gathered, name):
    R = w.shape[0]

    def body(w_ref, m_ref, v_ref, g_ref, g_out, d_out, m_out, v_out):
        g = g_ref[0]
        for d in range(1, N_DEV):
            g = g + g_ref[d]
        g_out[...] = g
        d_out[...], m_out[...], v_out[...] = _adamw(w_ref[...], g, m_ref[...], v_ref[...])

    return pl.pallas_call(body, name=name, out_shape=[S((R, 128), f32)] * 4,
                          compiler_params=pltpu.CompilerParams(vmem_limit_bytes=VMEM_LIMIT))(w, m, v, gathered)


WEIGHTS = ["ffn1_norm", "ffn1_w_gate", "ffn1_w_up", "ffn1_w_down", "mix_norm", "w_in", "s5_lam_re", "s5_lam_im", "s5_log_dt",
           "s5_b_re", "s5_b_im", "s5_c_re", "s5_c_im", "s5_d", "s5_w_glu", "s5_b_glu", "conv_w_dw", "conv_b_dw", "conv_ln_g",
           "conv_ln_b", "w_out", "ffn2_norm", "ffn2_w_gate", "ffn2_w_up", "ffn2_w_down", "final_norm"]
SHARDED = ["ffn1_w_gate", "ffn1_w_up", "ffn1_w_down", "w_in", "s5_w_glu", "conv_w_dw", "w_out", "ffn2_w_gate", "ffn2_w_up",
           "ffn2_w_down"]
REPLICATED = [n for n in WEIGHTS if n not in SHARDED]
PACK = 8 * 128


def _is_up(n):
    return n.endswith("w_gate") or n.endswith("w_up")


def _shard_to_wire(n, w):
    if _is_up(n):
        w = jnp.pad(w, ((0, 0), (0, FF_SHARD_PAD - FF_SHARD)))
    elif n.endswith("w_down"):
        w = jnp.pad(w, ((0, FF_SHARD_PAD - FF_SHARD), (0, 0)))
    elif n == "conv_w_dw":
        return jnp.pad(w, ((0, CONV_HALO - CONV_K), (0, 0)))
    return w.astype(bf16)


def _gathered_to_full(n, g):
    if n == "conv_w_dw":
        return g.transpose(1, 0, 2).reshape(CONV_HALO, CONV_WIDTH)[:CONV_K]
    if _is_up(n) or n == "w_in":
        return g.transpose(1, 0, 2).reshape(g.shape[1], N_DEV * g.shape[2])
    return g.reshape(N_DEV * g.shape[1], g.shape[2])


def _grad_to_blocks(n, g):
    if n == "conv_w_dw":
        g = jnp.pad(g, ((0, CONV_HALO - CONV_K), (0, 0)))
    g = g.astype(bf16)
    if _is_up(n) or n in ("w_in", "conv_w_dw"):
        g = g.reshape(g.shape[0], N_DEV, g.shape[1] // N_DEV).transpose(1, 0, 2)
    else:
        g = g.reshape(N_DEV, g.shape[0] // N_DEV, g.shape[1])
    return g.reshape(N_CHIP, 2, *g.shape[1:])


def _pack(parts):
    out = []
    for a in parts:
        a = a.reshape(-1)
        out.append(jnp.pad(a, (0, -a.size % PACK)))
    return jnp.concatenate(out).reshape(-1, 128)


def _unpack(packed, like):
    out, at = [], 0
    flat = packed.reshape(-1)
    for a in like:
        out.append(flat[at:at + a.size].reshape(a.shape))
        at += a.size + (-a.size % PACK)
    return out


REPLICATED_LATE = ["ffn1_norm"]
REPLICATED_EARLY = [n for n in REPLICATED if n not in REPLICATED_LATE]

PLAN = {
    "start": [("gather", ["ffn1_w_gate", "ffn1_w_up"])],
    "ffn1_up": [("gather", ["ffn1_w_down", "w_in", "w_out", "s5_w_glu", "conv_w_dw"])],
    "s5_ends": [("gather", ["ffn2_w_gate"])],
    "s5_states": [("gather", ["ffn2_w_up"])],
    "conv_fwd": [("gather", ["ffn2_w_down"])],
    "ffn2_dw_up": [("sibling", ["ffn2_w_gate"])],
    "ffn2_dw_down": [("sibling", ["ffn2_w_up"])],
    "mix_out_bwd": [("sibling", ["ffn2_w_down"])],
    "s5_ends_bwd": [("chips", ["ffn2_w_gate"])],
    "s5_states_bwd": [("chips", ["ffn2_w_up"])],
    "conv_bwd_taps": [("chips", ["ffn2_w_down"])],
    "ffn1_bwd_act": [("sibling", ["w_in", "s5_w_glu", "conv_w_dw", "w_out"]), ("replicated", REPLICATED_EARLY)],
    "ffn1_dw_gate": [("chips", ["w_in", "s5_w_glu", "conv_w_dw", "w_out"])],
    "ffn1_dw_up": [("sibling", ["ffn1_w_gate"])],
    "ffn1_dw_down": [("sibling", ["ffn1_w_up"]), ("chips", ["ffn1_w_gate"])],
    "ffn1_bwd_in": [("sibling", ["ffn1_w_down"]), ("chips", ["ffn1_w_up"])],
    "tail": [("chips", ["ffn1_w_down"]), ("replicated", REPLICATED_LATE)],
}


class _Schedule:
    def __init__(self, wire, p, grads, core, pack_mine):
        self.wire, self.p, self.grads, self.core, self.pack_mine = wire, p, grads, core, pack_mine
        self.partial, self.reduced, self.everyone, self.pending = {}, {}, {}, []

    def before(self, point):
        assert not self.pending
        for kind, names in PLAN.get(point, ()):
            if kind == "gather":
                given = [self.wire[n] for n in names]
                ex = _gather(given)
            elif kind == "sibling":
                given = [_grad_to_blocks(n, self.grads[n]) for n in names]
                ex = _swap_with_sibling(given)
            elif kind == "chips":
                given = [self.partial.pop(n) for n in names]
                ex = _swap_with_chips(given)
            else:
                given = [self.pack_mine(names)]
                ex = _gather(given)
            self.pending.append((kind, names, given, ex))
        return [ex for _, _, _, ex in self.pending]

    def after(self, point):
        for kind, names, given, ex in self.pending:
            if kind == "gather":
                for n, g in zip(names, ex.out):
                    self.p[n] = _gathered_to_full(n, g)
            elif kind == "sibling":
                for n, blocks, got in zip(names, given, ex.out):
                    self.partial[n] = _add_sibling(blocks, got, self.core, "reduce_add_" + n)
            elif kind == "chips":
                for n, part, got in zip(names, given, ex.out):
                    self.reduced[n] = (part, got)
            else:
                self.everyone[names[0]] = ex.out[0]
        self.pending = []

    def alone(self, point):
        _exchange(self.before(point), point)
        self.after(point)


def kernel(x, ffn1_norm, ffn1_w_gate, ffn1_w_up, ffn1_w_down, mix_norm, w_in, s5_lam_re, s5_lam_im, s5_log_dt, s5_b_re, s5_b_im, s5_c_re, s5_c_im, s5_d, s5_w_glu, s5_b_glu, conv_w_dw, conv_b_dw, conv_ln_g, conv_ln_b, w_out, ffn2_norm, ffn2_w_gate, ffn2_w_up, ffn2_w_down, final_norm, loss_target, m_ffn1_norm, m_ffn1_w_gate, m_ffn1_w_up, m_ffn1_w_down, m_mix_norm, m_w_in, m_s5_lam_re, m_s5_lam_im, m_s5_log_dt, m_s5_b_re, m_s5_b_im, m_s5_c_re, m_s5_c_im, m_s5_d, m_s5_w_glu, m_s5_b_glu, m_conv_w_dw, m_conv_b_dw, m_conv_ln_g, m_conv_ln_b, m_w_out, m_ffn2_norm, m_ffn2_w_gate, m_ffn2_w_up, m_ffn2_w_down, m_final_norm, v_ffn1_norm, v_ffn1_w_gate, v_ffn1_w_up, v_ffn1_w_down, v_mix_norm, v_w_in, v_s5_lam_re, v_s5_lam_im, v_s5_log_dt, v_s5_b_re, v_s5_b_im, v_s5_c_re, v_s5_c_im, v_s5_d, v_s5_w_glu, v_s5_b_glu, v_conv_w_dw, v_conv_b_dw, v_conv_ln_g, v_conv_ln_b, v_w_out, v_ffn2_norm, v_ffn2_w_gate, v_ffn2_w_up, v_ffn2_w_down, v_final_norm):
    args = locals()
    w = {n: args[n] for n in WEIGHTS}
    m = {n: args["m_" + n] for n in WEIGHTS}
    v = {n: args["v_" + n] for n in WEIGHTS}
    shard2d = lambda a: a.reshape(a.shape[-2:])
    xq, yq, cq = _place()
    q = 2 * xq + yq
    slots = jnp.stack([q, q ^ 1, q ^ 2, q ^ 3]).astype(jnp.int32)

    p = {}
    for n in REPLICATED:
        p[n] = w[n].reshape(w[n].shape[1:]) if w[n].ndim >= 3 else w[n]
    grads = {}
    zero = jnp.zeros((1,), f32)
    lead = lambda names, first: [first] if names is REPLICATED_EARLY else []
    pack_mine = lambda names: _pack(lead(names, grads["loss_terms"].sum().reshape(1)) + [grads[n] for n in names])
    sched = _Schedule({n: _shard_to_wire(n, shard2d(w[n])) for n in SHARDED}, p, grads,
                      jnp.reshape(cq, (1,)).astype(jnp.int32), pack_mine)
    _, dx = _local_step(x[0], loss_target[0], p, grads, sched)

    out = {}
    for n in SHARDED:
        part, got = sched.reduced[n]
        rows = part.shape[1] if n == "conv_w_dw" else w[n].shape[-2]
        fit = lambda a: jnp.pad(shard2d(a), ((0, rows - a.shape[-2]), (0, 0)))
        res = _adam_sharded(fit(w[n]), fit(m[n]), fit(v[n]), part, got, slots, "adam_" + n)
        out[n] = [r[:w[n].shape[-2]].reshape(w[n].shape) for r in res]

    for names in (REPLICATED_EARLY, REPLICATED_LATE):
        head = lead(names, zero)
        like = head + [w[n] for n in names]
        packed = lambda d: _pack(head + [d[n] for n in names])
        res = _adam_replicated(packed(w), packed(m), packed(v), sched.everyone[names[0]], "adam_" + names[0])
        unpacked = [_unpack(r, like) for r in res]
        for i, n in enumerate(names):
            out[n] = [u[len(head) + i] for u in unpacked]
        if head:
            loss = unpacked[0][0].reshape(())

    return (loss, dx.reshape(x.shape), *[out[n][0] for n in WEIGHTS], *[out[n][1] for n in WEIGHTS],
            *[out[n][2] for n in WEIGHTS], *[out[n][3] for n in WEIGHTS])
```

```python
import functools

import jax
import jax.numpy as jnp
from jax import lax
from jax.experimental import pallas as pl
from jax.experimental.pallas import tpu as pltpu

f32 = jnp.float32
bf16 = jnp.bfloat16
S = jax.ShapeDtypeStruct

N_DEV = 8
N_CHIP = 4
D_MODEL = 1024
D_FF = 2816
FF_SHARD = D_FF // N_DEV
FF_SHARD_PAD = 384
FF_PAD = FF_SHARD_PAD * N_DEV
S5_WIDTH = 512
S5_GROUPS = 32
S5_GROUP_CH = 16
S5_STATE = 64
S5_LANES = S5_GROUPS * S5_STATE
CONV_WIDTH = 512
CONV_K = 31
CONV_HALO = 32
CONV_HEAD = 64
CONV_ROWS = 32
IN_COLS = S5_WIDTH + 2 * CONV_WIDTH
SEGMENTS = 8
SCAN_LANES = 512
EPS = 1e-6
LR, B1, B2, ADAM_EPS, WD, STEP = 0.001, 0.9, 0.999, 1e-08, 0.01, 10
VMEM_LIMIT = 56 * 1024 * 1024

NN = (((1,), (0,)), ((), ()))
NT = (((1,), (1,)), ((), ()))
TN = (((0,), (0,)), ((), ()))


def _dot(a, b, dims=NN):
    return lax.dot_general(a, b, dims, preferred_element_type=f32)


def _cp(*sem):
    return pltpu.CompilerParams(dimension_semantics=sem, vmem_limit_bytes=VMEM_LIMIT)


def _rms(x, g):
    return x * lax.rsqrt(jnp.mean(x * x, axis=-1, keepdims=True) + EPS) * g


def _rms_bwd(x, g, dh):
    _, vjp = jax.vjp(_rms, x, g)
    return vjp(dh)


def _sigmoid(x):
    return 1.0 / (1.0 + jnp.exp(-x))


def _gelu(x):
    return 0.5 * x * (1.0 + jnp.tanh(0.7978845608028654 * (x + 0.044715 * x * x * x)))


def _rows8(x):
    t, c = x.shape
    return x.reshape(t // 8, 8, c).sum(axis=0)


def _full(shape):
    return pl.BlockSpec(shape, lambda *_: (0,) * len(shape))


def _resident(shape):
    return pl.BlockSpec(shape, lambda *_: (0,) * len(shape), pipeline_mode=pl.Buffered(1))


def _ffn_up(x, g, wg, wu, tm, tn, tag, ride=()):
    L = x.shape[0]

    def body(x_ref, g_ref, wg_ref, wu_ref, h_ref, dadg_ref, dadu_ref, a_ref):
        h = _rms(x_ref[...], g_ref[...]).astype(bf16)
        h_ref[...] = h
        for j in range(FF_PAD // tn):
            cols = slice(j * tn, (j + 1) * tn)
            gate = _dot(h, wg_ref[:, cols])
            up = _dot(h, wu_ref[:, cols])
            sig = _sigmoid(gate)
            silu = gate * sig
            dadg_ref[:, cols] = (up * (sig + silu * (1.0 - sig))).astype(bf16)
            dadu_ref[:, cols] = silu.astype(bf16)
            a_ref[:, cols] = (silu * up).astype(bf16)

    row = pl.BlockSpec((tm, D_MODEL), lambda i: (i, 0))
    wide = pl.BlockSpec((tm, FF_PAD), lambda i: (i, 0))
    return _pallas(
        body, ride=ride, name=tag + "_up", grid=(L // tm,),
        in_specs=[row, _full((1, D_MODEL)), _resident((D_MODEL, FF_PAD)), _resident((D_MODEL, FF_PAD))],
        out_specs=[row, wide, wide, wide],
        out_shape=[S((L, D_MODEL), bf16)] + [S((L, FF_PAD), bf16)] * 3,
        compiler_params=_cp("parallel"),
    )(x, g, wg, wu)


def _ffn_down(x, a, wd, tm, tag):
    L = x.shape[0]

    def body(x_ref, a_ref, wd_ref, o_ref):
        o_ref[...] = x_ref[...] + 0.5 * _dot(a_ref[...], wd_ref[...])

    return pl.pallas_call(
        body, name=tag + "_down", grid=(L // tm,),
        in_specs=[pl.BlockSpec((tm, D_MODEL), lambda i: (i, 0)), pl.BlockSpec((tm, FF_PAD), lambda i: (i, 0)),
                  _resident((FF_PAD, D_MODEL))],
        out_specs=pl.BlockSpec((tm, D_MODEL), lambda i: (i, 0)),
        out_shape=S((L, D_MODEL), f32),
        compiler_params=_cp("parallel"),
    )(x, a, wd)


def _ffn_down_loss(x, a, wd, target, g, tm, tag):
    L = x.shape[0]

    def body(x_ref, a_ref, wd_ref, t_ref, g_ref, dx_ref, dg_ref, l_ref):
        @pl.when(pl.program_id(0) == 0)
        def _():
            dg_ref[...] = jnp.zeros_like(dg_ref)
            l_ref[...] = jnp.zeros_like(l_ref)

        xo = x_ref[...] + 0.5 * _dot(a_ref[...], wd_ref[...])
        g = g_ref[...]
        e = _rms(xo, g) - t_ref[...]
        l_ref[...] += _rows8(e * e) * (0.5 / D_MODEL)
        dx, dg = _rms_bwd(xo, g, e * (1.0 / D_MODEL))
        dx_ref[...] = dx
        dg_ref[...] += dg

    row = pl.BlockSpec((tm, D_MODEL), lambda i: (i, 0))
    return pl.pallas_call(
        body, name=tag + "_down_loss", grid=(L // tm,),
        in_specs=[row, pl.BlockSpec((tm, FF_PAD), lambda i: (i, 0)), _resident((FF_PAD, D_MODEL)), row, _full((1, D_MODEL))],
        out_specs=[row, _full((1, D_MODEL)), _full((8, D_MODEL))],
        out_shape=[S((L, D_MODEL), f32), S((1, D_MODEL), f32), S((8, D_MODEL), f32)],
        compiler_params=_cp("arbitrary"),
    )(x, a, wd, target, g)


def _ffn_bwd_act(dxo, wd, dadg, dadu, tm, tn, tag, ride=()):
    L = dxo.shape[0]

    def body(dx_ref, wd_ref, dadg_ref, dadu_ref, dgate_ref, dup_ref, dxh_ref):
        dxh = (0.5 * dx_ref[...]).astype(bf16)
        dxh_ref[...] = dxh
        for j in range(FF_PAD // tn):
            cols = slice(j * tn, (j + 1) * tn)
            da = _dot(dxh, wd_ref[cols, :], NT)
            dgate_ref[:, cols] = (da * dadg_ref[:, cols].astype(f32)).astype(bf16)
            dup_ref[:, cols] = (da * dadu_ref[:, cols].astype(f32)).astype(bf16)

    row = pl.BlockSpec((tm, D_MODEL), lambda i: (i, 0))
    wide = pl.BlockSpec((tm, FF_PAD), lambda i: (i, 0))
    return _pallas(
        body, ride=ride, name=tag + "_bwd_act", grid=(L // tm,),
        in_specs=[row, _resident((FF_PAD, D_MODEL)), wide, wide],
        out_specs=[wide, wide, row],
        out_shape=[S((L, FF_PAD), bf16), S((L, FF_PAD), bf16), S((L, D_MODEL), bf16)],
        compiler_params=_cp("parallel"),
    )(dxo, wd, dadg, dadu)


def _ffn_bwd_in(dxo, x, g, dgate, dup, wg, wu, tm, tag, ride=()):
    L = x.shape[0]

    def body(dxo_ref, x_ref, g_ref, dgate_ref, dup_ref, wg_ref, wu_ref, dx_ref, dg_ref):
        @pl.when(pl.program_id(0) == 0)
        def _():
            dg_ref[...] = jnp.zeros_like(dg_ref)

        dh = _dot(dgate_ref[...], wg_ref[...], NT) + _dot(dup_ref[...], wu_ref[...], NT)
        dx, dg = _rms_bwd(x_ref[...], g_ref[...], dh)
        dx_ref[...] = dxo_ref[...] + dx
        dg_ref[...] += dg

    row = pl.BlockSpec((tm, D_MODEL), lambda i: (i, 0))
    wide = pl.BlockSpec((tm, FF_PAD), lambda i: (i, 0))
    return _pallas(
        body, ride=ride, name=tag + "_bwd_in", grid=(L // tm,),
        in_specs=[row, row, _full((1, D_MODEL)), wide, wide, _resident((D_MODEL, FF_PAD)), _resident((D_MODEL, FF_PAD))],
        out_specs=[row, _full((1, D_MODEL))],
        out_shape=[S((L, D_MODEL), f32), S((1, D_MODEL), f32)],
        compiler_params=_cp("arbitrary"),
    )(dxo, x, g, dgate, dup, wg, wu)


def _mm_tn(a, b, out_dtype, name, tm=512, tn=1024, ride=()):
    L, M = a.shape
    N = b.shape[1]
    tm, tn = min(tm, M), min(tn, N)
    while N % tn:
        tn //= 2

    def body(a_ref, b_ref, o_ref):
        o_ref[...] = _dot(a_ref[...].astype(bf16), b_ref[...].astype(bf16), TN).astype(out_dtype)

    return _pallas(
        body, ride=ride, name=name, grid=(M // tm, N // tn),
        in_specs=[pl.BlockSpec((L, tm), lambda i, j: (0, i)), pl.BlockSpec((L, tn), lambda i, j: (0, j))],
        out_specs=pl.BlockSpec((tm, tn), lambda i, j: (i, j)),
        out_shape=S((M, N), out_dtype),
        compiler_params=_cp("parallel", "parallel"),
    )(a, b)


def _mix_in(x, g, w_in, tm):
    L = x.shape[0]

    def body(x_ref, g_ref, w_ref, h_ref, us_ref, v_ref):
        h = _rms(x_ref[...], g_ref[...]).astype(bf16)
        h_ref[...] = h
        u = _dot(h, w_ref[...])
        us_ref[...] = u[:, :S5_WIDTH]
        v_ref[...] = u[:, S5_WIDTH:]

    row = lambda c: pl.BlockSpec((tm, c), lambda i: (i, 0))
    return pl.pallas_call(
        body, name="mix_in", grid=(L // tm,),
        in_specs=[row(D_MODEL), _full((1, D_MODEL)), _full((D_MODEL, IN_COLS))],
        out_specs=[row(D_MODEL), row(S5_WIDTH), row(2 * CONV_WIDTH)],
        out_shape=[S((L, D_MODEL), bf16), S((L, S5_WIDTH), f32), S((L, 2 * CONV_WIDTH), f32)],
        compiler_params=_cp("parallel"),
    )(x, g, w_in)


def _mix_in_bwd(dxo, x, g, du_s5, dv, w_in, tm):
    L = x.shape[0]

    def body(dxo_ref, x_ref, g_ref, dus_ref, dv_ref, w_ref, dx_ref, dg_ref, dub_ref):
        @pl.when(pl.program_id(0) == 0)
        def _():
            dg_ref[...] = jnp.zeros_like(dg_ref)

        dus = dus_ref[...].astype(bf16)
        dvb = dv_ref[...].astype(bf16)
        dub_ref[:, :S5_WIDTH] = dus
        dub_ref[:, S5_WIDTH:] = dvb
        dh = _dot(dus, w_ref[:, :S5_WIDTH], NT) + _dot(dvb, w_ref[:, S5_WIDTH:], NT)
        dx, dg = _rms_bwd(x_ref[...], g_ref[...], dh)
        dx_ref[...] = dxo_ref[...] + dx
        dg_ref[...] += dg

    row = lambda c: pl.BlockSpec((tm, c), lambda i: (i, 0))
    return pl.pallas_call(
        body, name="mix_in_bwd", grid=(L // tm,),
        in_specs=[row(D_MODEL), row(D_MODEL), _full((1, D_MODEL)), row(S5_WIDTH), row(2 * CONV_WIDTH),
                  _full((D_MODEL, IN_COLS))],
        out_specs=[row(D_MODEL), _full((1, D_MODEL)), row(IN_COLS)],
        out_shape=[S((L, D_MODEL), f32), S((1, D_MODEL), f32), S((L, IN_COLS), bf16)],
        compiler_params=_cp("arbitrary"),
    )(dxo, x, g, du_s5, dv, w_in)


def _mix_out(x, y_s5, y_conv, w_out, tm):
    L = x.shape[0]

    def body(x_ref, ys_ref, yc_ref, w_ref, o_ref):
        o_ref[...] = x_ref[...] + _dot(ys_ref[...], w_ref[:S5_WIDTH, :]) + _dot(yc_ref[...], w_ref[S5_WIDTH:, :])

    row = lambda c: pl.BlockSpec((tm, c), lambda i: (i, 0))
    return pl.pallas_call(
        body, name="mix_out", grid=(L // tm,),
        in_specs=[row(D_MODEL), row(S5_WIDTH), row(CONV_WIDTH), _full((D_MODEL, D_MODEL))],
        out_specs=row(D_MODEL), out_shape=S((L, D_MODEL), f32),
        compiler_params=_cp("parallel"),
    )(x, y_s5, y_conv, w_out)


def _mix_out_bwd(dx, w_out, tm, ride=()):
    L = dx.shape[0]

    def body(dx_ref, w_ref, dys_ref, dyc_ref, dxb_ref):
        dxb = dx_ref[...].astype(bf16)
        dxb_ref[...] = dxb
        dys_ref[...] = _dot(dxb, w_ref[:S5_WIDTH, :], NT)
        dyc_ref[...] = _dot(dxb, w_ref[S5_WIDTH:, :], NT)

    row = lambda c: pl.BlockSpec((tm, c), lambda i: (i, 0))
    return _pallas(
        body, ride=ride, name="mix_out_bwd", grid=(L // tm,),
        in_specs=[row(D_MODEL), _full((D_MODEL, D_MODEL))],
        out_specs=[row(S5_WIDTH), row(CONV_WIDTH), row(D_MODEL)],
        out_shape=[S((L, S5_WIDTH), f32), S((L, CONV_WIDTH), f32), S((L, D_MODEL), bf16)],
        compiler_params=_cp("parallel"),
    )(dx, w_out)


def _s5_discretise(lam_re, lam_im, log_dt, b_re, b_im):
    dt = jnp.exp(log_dt)
    mag = jnp.exp(lam_re * dt)
    abar_re = mag * jnp.cos(lam_im * dt)
    abar_im = mag * jnp.sin(lam_im * dt)
    den = lam_re * lam_re + lam_im * lam_im
    num_re = abar_re - 1.0
    f_re = ((num_re * lam_re + abar_im * lam_im) / den)[:, None, :]
    f_im = ((abar_im * lam_re - num_re * lam_im) / den)[:, None, :]
    return abar_re, abar_im, f_re * b_re - f_im * b_im, f_re * b_im + f_im * b_re


def _s5_params(lam_re, lam_im, log_dt, b_re, b_im):
    def body(lr, li, ld, br, bi, ar_ref, ai_ref, bbr_ref, bbi_ref):
        ar, ai, bbr, bbi = _s5_discretise(lr[...], li[...], ld[...], br[...], bi[...])
        ar_ref[...], ai_ref[...], bbr_ref[...], bbi_ref[...] = ar, ai, bbr, bbi

    gp = S((S5_GROUPS, S5_STATE), f32)
    gcp = S((S5_GROUPS, S5_GROUP_CH, S5_STATE), f32)
    return pl.pallas_call(body, name="s5_params", out_shape=[gp, gp, gcp, gcp])(lam_re, lam_im, log_dt, b_re, b_im)


def _s5_params_bwd(lam_re, lam_im, log_dt, b_re, b_im, d_ar, d_ai, d_bbr, d_bbi):
    def body(lr, li, ld, br, bi, car, cai, cbr, cbi, o_lr, o_li, o_ld, o_br, o_bi):
        _, vjp = jax.vjp(_s5_discretise, lr[...], li[...], ld[...], br[...], bi[...])
        o_lr[...], o_li[...], o_ld[...], o_br[...], o_bi[...] = vjp((car[...], cai[...], cbr[...], cbi[...]))

    gp = S((S5_GROUPS, S5_STATE), f32)
    gcp = S((S5_GROUPS, S5_GROUP_CH, S5_STATE), f32)
    return pl.pallas_call(body, name="s5_params_bwd", out_shape=[gp, gp, S((S5_GROUPS, 1), f32), gcp, gcp])(
        lam_re, lam_im, log_dt, b_re, b_im, d_ar, d_ai, d_bbr, d_bbi)


def _cmul(ar, ai, br, bi):
    return ar * br - ai * bi, ar * bi + ai * br


def _segment_starts(er, ei, ar, ai, steps, reverse):
    pr, pi = ar, ai
    n = 1
    while n < steps:
        pr, pi = _cmul(pr, pi, pr, pi)
        n *= 2
    assert n == steps
    row = lax.broadcasted_iota(jnp.int32, (SEGMENTS, SCAN_LANES), 0)
    hr = jnp.zeros((1, SCAN_LANES), f32)
    hi = jnp.zeros((1, SCAN_LANES), f32)
    out_r = jnp.zeros((SEGMENTS, SCAN_LANES), f32)
    out_i = jnp.zeros((SEGMENTS, SCAN_LANES), f32)
    order = range(SEGMENTS - 1, 0, -1) if reverse else range(0, SEGMENTS - 1)
    for r in order:
        qr, qi = _cmul(pr, pi, hr, hi)
        hr, hi = qr + er[r:r + 1, :], qi + ei[r:r + 1, :]
        nxt = r - 1 if reverse else r + 1
        out_r = jnp.where(row == nxt, hr, out_r)
        out_i = jnp.where(row == nxt, hi, out_i)
    return out_r, out_i


def _s5_read_bwd(dout, y_lin, u, d_skip, w_glu, b_glu, tm):
    L = u.shape[0]

    def body(do_ref, yl_ref, u_ref, d_ref, w_ref, b_ref, dyl_ref, du_ref, dd_ref, dw_ref, db_ref):
        @pl.when(pl.program_id(0) == 0)
        def _():
            dd_ref[...] = jnp.zeros_like(dd_ref)
            dw_ref[...] = jnp.zeros_like(dw_ref)
            db_ref[...] = jnp.zeros_like(db_ref)

        u, d, dout = u_ref[...], d_ref[...], do_ref[...]
        y, gelu_vjp = jax.vjp(_gelu, yl_ref[...] + d * u)
        yb = y.astype(bf16)
        sig = _sigmoid(_dot(yb, w_ref[...]) + b_ref[...])
        dz = dout * y * sig * (1.0 - sig)
        dzb = dz.astype(bf16)
        dy = dout * sig + _dot(dzb, w_ref[...], NT)
        (dyp,) = gelu_vjp(dy)
        dyl_ref[...] = dyp.astype(bf16)
        du_ref[...] = d * dyp
        dd_ref[...] += _rows8(dyp * u)
        db_ref[...] += _rows8(dz)
        dw_ref[...] += _dot(yb, dzb, TN)

    row = pl.BlockSpec((tm, S5_WIDTH), lambda i: (i, 0))
    vec = _full((1, S5_WIDTH))
    part = _full((8, S5_WIDTH))
    return pl.pallas_call(
        body, name="s5_read_bwd", grid=(L // tm,),
        in_specs=[row, row, row, vec, _full((S5_WIDTH, S5_WIDTH)), vec],
        out_specs=[row, row, part, _full((S5_WIDTH, S5_WIDTH)), part],
        out_shape=[S((L, S5_WIDTH), bf16), S((L, S5_WIDTH), f32), S((8, S5_WIDTH), f32),
                   S((S5_WIDTH, S5_WIDTH), f32), S((8, S5_WIDTH), f32)],
        compiler_params=_cp("arbitrary"),
    )(dout, y_lin, u, d_skip, w_glu, b_glu)


S5_CHUNK_CH = SCAN_LANES // S5_STATE * S5_GROUP_CH


def _s5_specs(L, bi, reverse):
    nb = L // (bi * SEGMENTS)
    blk = (lambda c, j: (nb - 1 - j, c)) if reverse else (lambda c, j: (j, c))
    chan = pl.BlockSpec((bi * SEGMENTS, S5_CHUNK_CH), blk)
    state = pl.BlockSpec((bi * SEGMENTS, SCAN_LANES), blk)
    mat = pl.BlockSpec((S5_CHUNK_CH, SCAN_LANES), lambda c, j: (c, c))
    vec = pl.BlockSpec((1, SCAN_LANES), lambda c, j: (0, c))
    tile = pl.BlockSpec((SEGMENTS, SCAN_LANES), lambda c, j: (0, c))
    return nb, chan, state, mat, vec, tile


def _drive_into(src, m_re, m_im, negate_im, dr_ref, di_ref, bi):
    d_im = _dot(src, m_im)
    dr_ref[...] = _dot(src, m_re).reshape(bi, SEGMENTS, SCAN_LANES)
    di_ref[...] = (-d_im if negate_im else d_im).reshape(bi, SEGMENTS, SCAN_LANES)


def _s5_ends(src, a_re, a_im, m_re, m_im, reverse, bi, name, ride=()):
    L = src.shape[0]
    nb, chan, _, mat, vec, tile = _s5_specs(L, bi, reverse)

    def body(src_ref, ar_ref, ai_ref, mr_ref, mi_ref, er_ref, ei_ref, dr_ref, di_ref):
        @pl.when(pl.program_id(1) == 0)
        def _():
            er_ref[...] = jnp.zeros_like(er_ref)
            ei_ref[...] = jnp.zeros_like(ei_ref)

        _drive_into(src_ref[...].astype(bf16), mr_ref[...], mi_ref[...], reverse, dr_ref, di_ref, bi)
        ar = jnp.broadcast_to(ar_ref[...], (SEGMENTS, SCAN_LANES))
        ai = jnp.broadcast_to(ai_ref[...], (SEGMENTS, SCAN_LANES))

        def step(n, c):
            i = (bi - 1 - n) if reverse else n
            pr, pi = _cmul(ar, ai, c[0], c[1])
            return pr + dr_ref[i], pi + di_ref[i]

        er_ref[...], ei_ref[...] = lax.fori_loop(0, bi, step, (er_ref[...], ei_ref[...]), unroll=4)

    out = S((SEGMENTS, S5_LANES), f32)
    return _pallas(
        body, ride=ride, name=name, grid=(S5_LANES // SCAN_LANES, nb),
        in_specs=[chan, vec, vec, mat, mat], out_specs=[tile, tile], out_shape=[out, out],
        scratch_shapes=[pltpu.VMEM((bi, SEGMENTS, SCAN_LANES), f32)] * 2,
        compiler_params=_cp("parallel", "arbitrary"),
    )(src, a_re, a_im, m_re, m_im)


def _s5_states(u, a_re, a_im, bb_re, bb_im, cc_re, cc_im, e_re, e_im, bi, ride=()):
    L = u.shape[0]
    nb, chan, state, mat, vec, tile = _s5_specs(L, bi, False)
    rows = bi * SEGMENTS

    def body(u_ref, ar_ref, ai_ref, br_ref, bi_ref, cr_ref, ci_ref, er_ref, ei_ref, sr_ref, si_ref, yl_ref,
             hr_ref, hi_ref, dr_ref, di_ref):
        @pl.when(pl.program_id(1) == 0)
        def _():
            hr_ref[...], hi_ref[...] = _segment_starts(er_ref[...], ei_ref[...], ar_ref[...], ai_ref[...], L // SEGMENTS, False)

        _drive_into(u_ref[...].astype(bf16), br_ref[...], bi_ref[...], False, dr_ref, di_ref, bi)
        ar = jnp.broadcast_to(ar_ref[...], (SEGMENTS, SCAN_LANES))
        ai = jnp.broadcast_to(ai_ref[...], (SEGMENTS, SCAN_LANES))

        def step(i, c):
            pr, pi = _cmul(ar, ai, c[0], c[1])
            nr, nim = pr + dr_ref[i], pi + di_ref[i]
            dr_ref[i] = nr
            di_ref[i] = nim
            return nr, nim

        hr_ref[...], hi_ref[...] = lax.fori_loop(0, bi, step, (hr_ref[...], hi_ref[...]), unroll=4)
        sr = dr_ref[...].reshape(rows, SCAN_LANES).astype(bf16)
        si = di_ref[...].reshape(rows, SCAN_LANES).astype(bf16)
        sr_ref[...] = sr
        si_ref[...] = si
        yl_ref[...] = _dot(sr, cr_ref[...], NT) - _dot(si, ci_ref[...], NT)

    return _pallas(
        body, ride=ride, name="s5_states", grid=(S5_LANES // SCAN_LANES, nb),
        in_specs=[chan, vec, vec, mat, mat, mat, mat, tile, tile],
        out_specs=[state, state, chan],
        out_shape=[S((L, S5_LANES), bf16)] * 2 + [S((L, S5_WIDTH), f32)],
        scratch_shapes=[pltpu.VMEM((SEGMENTS, SCAN_LANES), f32)] * 2 + [pltpu.VMEM((bi, SEGMENTS, SCAN_LANES), f32)] * 2,
        compiler_params=_cp("parallel", "arbitrary"),
    )(u, a_re, a_im, bb_re, bb_im, cc_re, cc_im, e_re, e_im)


def _s5_states_bwd(dy, u, du_skip, s_re, s_im, a_re, a_im, bb_re, bb_im, cc_re, cc_im, e_re, e_im, bi, ride=()):
    L = u.shape[0]
    nb, chan, state, mat, vec, tile = _s5_specs(L, bi, True)
    rows = bi * SEGMENTS
    per = rows // 16

    def body(dy_ref, u_ref, dus_ref, sr_ref, si_ref, pr_ref, pi_ref, lr_ref, li_ref, ar_ref, ai_ref, br_ref, bi_ref, cr_ref,
             ci_ref, er_ref, ei_ref, du_ref, dar_ref, dai_ref, dbr_ref, dbi_ref, dcr_ref, dci_ref,
             hr_ref, hi_ref, gr_ref, gi_ref, fr_ref, fi_ref):
        j = pl.program_id(1)

        @pl.when(j == 0)
        def _():
            hr_ref[...], hi_ref[...] = _segment_starts(er_ref[...], ei_ref[...], ar_ref[...], ai_ref[...], L // SEGMENTS, True)
            for ref in (dar_ref, dai_ref, dbr_ref, dbi_ref, dcr_ref, dci_ref):
                ref[...] = jnp.zeros_like(ref)

        dy = dy_ref[...]
        sr, si = sr_ref[...], si_ref[...]
        _drive_into(dy, cr_ref[...], ci_ref[...], True, gr_ref, gi_ref, bi)
        fr_ref[...] = sr.astype(f32).reshape(bi, SEGMENTS, SCAN_LANES)
        fi_ref[...] = si.astype(f32).reshape(bi, SEGMENTS, SCAN_LANES)
        ar = jnp.broadcast_to(ar_ref[...], (SEGMENTS, SCAN_LANES))
        ai = jnp.broadcast_to(ai_ref[...], (SEGMENTS, SCAN_LANES))

        def step(n, c):
            i = bi - 1 - n
            gr, gi, accr, acci = c
            qr, qi = _cmul(ar, ai, gr, gi)
            gr, gi = qr + gr_ref[i], qi + gi_ref[i]
            gr_ref[i] = gr
            gi_ref[i] = gi
            pr, pi = fr_ref[i - 1], fi_ref[i - 1]
            return gr, gi, accr + (gr * pr + gi * pi), acci + (gi * pr - gr * pi)

        gr, gi, accr, acci = lax.fori_loop(0, bi - 1, step, (hr_ref[...], hi_ref[...], dar_ref[...], dai_ref[...]), unroll=3)
        qr, qi = _cmul(ar, ai, gr, gi)
        gr, gi = qr + gr_ref[0], qi + gi_ref[0]
        gr_ref[0] = gr
        gi_ref[0] = gi
        hr_ref[...], hi_ref[...] = gr, gi
        row = lax.broadcasted_iota(jnp.int32, (SEGMENTS, SCAN_LANES), 0)
        first = j == nb - 1
        older = lambda ref: ref[...].astype(f32)[SEGMENTS:, :]
        wrap_r = jnp.where(row == 0, 0.0, pltpu.roll(older(lr_ref), 1, 0))
        wrap_i = jnp.where(row == 0, 0.0, pltpu.roll(older(li_ref), 1, 0))
        pr = jnp.where(first, wrap_r, older(pr_ref))
        pi = jnp.where(first, wrap_i, older(pi_ref))
        dar_ref[...] = accr + gr * pr + gi * pi
        dai_ref[...] = acci + gi * pr - gr * pi

        g_re = gr_ref[...].reshape(rows, SCAN_LANES).astype(bf16)
        g_im = gi_ref[...].reshape(rows, SCAN_LANES).astype(bf16)
        ub = u_ref[...].astype(bf16)
        du_ref[...] = dus_ref[...] + _dot(g_re, br_ref[...], NT) + _dot(g_im, bi_ref[...], NT)
        dbr_ref[...] += _dot(ub, g_re, TN)
        dbi_ref[...] += _dot(ub, g_im, TN)
        dcr_ref[...] += _dot(dy, sr, TN)
        dci_ref[...] -= _dot(dy, si, TN)

    prev = pl.BlockSpec((16, SCAN_LANES), lambda c, j: (jnp.maximum((nb - 1 - j) * per - 1, 0), c))
    last = pl.BlockSpec((16, SCAN_LANES), lambda c, j: (L // 16 - 1, c))
    grad = pl.BlockSpec((S5_CHUNK_CH, SCAN_LANES), lambda c, j: (c, 0))
    big = pltpu.VMEM((bi, SEGMENTS, SCAN_LANES), f32)
    return _pallas(
        body, ride=ride, name="s5_states_bwd", grid=(S5_LANES // SCAN_LANES, nb),
        in_specs=[chan, chan, chan, state, state, prev, prev, last, last, vec, vec, mat, mat, mat, mat, tile, tile],
        out_specs=[chan, tile, tile, grad, grad, grad, grad],
        out_shape=[S((L, S5_WIDTH), f32)] + [S((SEGMENTS, S5_LANES), f32)] * 2 + [S((S5_WIDTH, SCAN_LANES), f32)] * 4,
        scratch_shapes=[pltpu.VMEM((SEGMENTS, SCAN_LANES), f32)] * 2 + [big] * 4,
        compiler_params=_cp("parallel", "arbitrary"),
    )(dy, u, du_skip, s_re, s_im, s_re, s_im, s_re, s_im, a_re, a_im, bb_re, bb_im, cc_re, cc_im, e_re, e_im)


def _s5_gate(y_lin, u, d_skip, w_glu, b_glu, tm, ride=()):
    L = u.shape[0]

    def body(yl_ref, u_ref, d_ref, w_ref, b_ref, o_ref):
        y = _gelu(yl_ref[...] + d_ref[...] * u_ref[...])
        z = _dot(y.astype(bf16), w_ref[...]) + b_ref[...]
        o_ref[...] = (y * _sigmoid(z)).astype(bf16)

    row = pl.BlockSpec((tm, S5_WIDTH), lambda i: (i, 0))
    vec = _full((1, S5_WIDTH))
    return _pallas(
        body, ride=ride, name="s5_gate", grid=(L // tm,),
        in_specs=[row, row, vec, _full((S5_WIDTH, S5_WIDTH)), vec],
        out_specs=row, out_shape=S((L, S5_WIDTH), bf16),
        compiler_params=_cp("parallel"),
    )(y_lin, u, d_skip, w_glu, b_glu)


def _group_mean(x, avg):
    hi = x.astype(bf16)
    lo = (x - hi.astype(f32)).astype(bf16)
    return _dot(hi, avg) + _dot(lo, avg)


def _conv_act(zn, ln_g, ln_b):
    t = zn * ln_g + ln_b
    return t * _sigmoid(t)


def _glu_padded(v_ref, halo_ref, zpad_ref, tm):
    v = v_ref[...]
    vh = halo_ref[...]
    zh = vh[:, :CONV_WIDTH] * _sigmoid(vh[:, CONV_WIDTH:])
    zpad_ref[:CONV_HALO, :] = jnp.where(pl.program_id(0) > 0, zh, 0.0)
    zpad_ref[CONV_HALO:CONV_HALO + tm, :] = v[:, :CONV_WIDTH] * _sigmoid(v[:, CONV_WIDTH:])
    zpad_ref[CONV_HALO + tm:, :] = jnp.zeros((8, CONV_WIDTH), f32)


def _shifted(pad_ref, sh_ref, tm):
    for b in range(8):
        sh_ref[b] = pad_ref[pl.ds(b, tm + CONV_HALO), :]


def _window(sh_ref, r0, off, rows):
    return sh_ref[off % 8, pl.ds(pl.multiple_of(r0 + 8 * (off // 8), 8), rows), :]


def _tap_sum(w_ref, sh_ref, taps, out_ref, tm, bias):
    def chunk(c, carry):
        r0 = pl.multiple_of(c * CONV_ROWS, CONV_ROWS)
        acc = jnp.zeros((CONV_ROWS, CONV_WIDTH), f32) + bias
        for k, off in taps:
            acc = acc + w_ref[k:k + 1, :] * _window(sh_ref, r0, off, CONV_ROWS)
        out_ref[pl.ds(r0, CONV_ROWS), :] = acc
        return carry

    lax.fori_loop(0, tm // CONV_ROWS, chunk, 0)


FWD_TAPS = [(k, CONV_HALO - (CONV_K - 1) + k) for k in range(CONV_K)]
BWD_TAPS = [(k, CONV_K - 1 - k) for k in range(CONV_K)]


def _conv_specs(tm):
    per = tm // CONV_HALO
    vrow = pl.BlockSpec((tm, 2 * CONV_WIDTH), lambda i: (i, 0))
    vhalo = pl.BlockSpec((CONV_HALO, 2 * CONV_WIDTH), lambda i: (jnp.maximum(i * per - 1, 0), 0))
    return vrow, vhalo


def _conv_scratch(tm):
    return [pltpu.VMEM((tm + CONV_HALO + 8, CONV_WIDTH), f32), pltpu.VMEM((8, tm + CONV_HALO, CONV_WIDTH), f32)]


def _conv_fwd(v, w_dw, b_dw, ln_g, ln_b, avg, tm, ride=()):
    L = v.shape[0]

    def body(v_ref, halo_ref, w_ref, b_ref, g_ref, bb_ref, avg_ref, o_ref, zc_ref, zpad_ref, zs_ref):
        _glu_padded(v_ref, halo_ref, zpad_ref, tm)
        _shifted(zpad_ref, zs_ref, tm)
        _tap_sum(w_ref, zs_ref, FWD_TAPS, zc_ref, tm, b_ref[...])
        zc = zc_ref[...]
        xc = zc - _group_mean(zc, avg_ref[...])
        zn = xc * lax.rsqrt(_group_mean(xc * xc, avg_ref[...]) + EPS)
        o_ref[...] = _conv_act(zn, g_ref[...], bb_ref[...]).astype(bf16)

    vrow, vhalo = _conv_specs(tm)
    vec = _full((1, CONV_WIDTH))
    row = pl.BlockSpec((tm, CONV_WIDTH), lambda i: (i, 0))
    return _pallas(
        body, ride=ride, name="conv_fwd", grid=(L // tm,),
        in_specs=[vrow, vhalo, _full((CONV_HALO, CONV_WIDTH)), vec, vec, vec, _full((CONV_WIDTH, CONV_WIDTH))],
        out_specs=[row, row], out_shape=[S((L, CONV_WIDTH), bf16), S((L, CONV_WIDTH), f32)],
        scratch_shapes=_conv_scratch(tm),
        compiler_params=_cp("arbitrary"),
    )(v, v, w_dw, b_dw, ln_g, ln_b, avg)


def _conv_bwd_norm(dout, zc, ln_g, ln_b, avg, tm):
    L = zc.shape[0]

    def body(do_ref, zc_ref, g_ref, bb_ref, avg_ref, dzc_ref, dg_ref, db_ref, dbd_ref):
        @pl.when(pl.program_id(0) == 0)
        def _():
            dg_ref[...] = jnp.zeros_like(dg_ref)
            db_ref[...] = jnp.zeros_like(db_ref)
            dbd_ref[...] = jnp.zeros_like(dbd_ref)

        avg = avg_ref[...]
        zc = zc_ref[...]
        xc = zc - _group_mean(zc, avg)
        rstd = lax.rsqrt(_group_mean(xc * xc, avg) + EPS)
        xhat = xc * rstd
        _, act_vjp = jax.vjp(_conv_act, xhat, g_ref[...], bb_ref[...])
        dxhat, dg, db = act_vjp(do_ref[...])
        dzc = rstd * (dxhat - _group_mean(dxhat, avg) - xhat * _group_mean(dxhat * xhat, avg))
        dzc_ref[...] = dzc
        dg_ref[0:1, :] += dg
        db_ref[0:1, :] += db
        dbd_ref[...] += _rows8(dzc)

    vec = _full((1, CONV_WIDTH))
    row = pl.BlockSpec((tm, CONV_WIDTH), lambda i: (i, 0))
    part = _full((8, CONV_WIDTH))
    return pl.pallas_call(
        body, name="conv_bwd_norm", grid=(L // tm,),
        in_specs=[row, row, vec, vec, _full((CONV_WIDTH, CONV_WIDTH))],
        out_specs=[row, part, part, part],
        out_shape=[S((L, CONV_WIDTH), f32)] + [S((8, CONV_WIDTH), f32)] * 3,
        compiler_params=_cp("arbitrary"),
    )(dout, zc, ln_g, ln_b, avg)


def _conv_bwd_taps(dzc, v, w_dw, tm, ride=()):
    L = v.shape[0]
    nt = L // tm
    per = tm // CONV_HALO

    def body(d_ref, dn_ref, v_ref, halo_ref, w_ref, dv_ref, dw_ref, zpad_ref, zs_ref, dpad_ref, ds_ref, dz_ref):
        i = pl.program_id(0)

        @pl.when(i == 0)
        def _():
            dw_ref[...] = jnp.zeros_like(dw_ref)

        _glu_padded(v_ref, halo_ref, zpad_ref, tm)
        _shifted(zpad_ref, zs_ref, tm)
        dpad_ref[:tm, :] = d_ref[...]
        dpad_ref[tm:tm + CONV_HALO, :] = jnp.where(i < nt - 1, dn_ref[...], 0.0)
        dpad_ref[tm + CONV_HALO:, :] = jnp.zeros((8, CONV_WIDTH), f32)
        _shifted(dpad_ref, ds_ref, tm)
        _tap_sum(w_ref, ds_ref, BWD_TAPS, dz_ref, tm, 0.0)

        for first in range(0, CONV_K, 8):
            taps = FWD_TAPS[first:first + 8]

            def chunk(c, accs, taps=taps):
                r0 = pl.multiple_of(c * 8, 8)
                d = d_ref[pl.ds(r0, 8), :]
                return tuple(acc + d * _window(zs_ref, r0, off, 8) for acc, (_, off) in zip(accs, taps))

            accs = lax.fori_loop(0, tm // 8, chunk, tuple(jnp.zeros((8, CONV_WIDTH), f32) for _ in taps), unroll=2)
            for acc, (k, _) in zip(accs, taps):
                dw_ref[k] += acc

        dz = dz_ref[...]
        v = v_ref[...]
        sig = _sigmoid(v[:, CONV_WIDTH:])
        dv_ref[:, :CONV_WIDTH] = dz * sig
        dv_ref[:, CONV_WIDTH:] = dz * v[:, :CONV_WIDTH] * sig * (1.0 - sig)

    vrow, vhalo = _conv_specs(tm)
    row = pl.BlockSpec((tm, CONV_WIDTH), lambda i: (i, 0))
    nxt = pl.BlockSpec((CONV_HALO, CONV_WIDTH), lambda i: (jnp.minimum((i + 1) * per, nt * per - 1), 0))
    return _pallas(
        body, ride=ride, name="conv_bwd_taps", grid=(nt,),
        in_specs=[row, nxt, vrow, vhalo, _full((CONV_HALO, CONV_WIDTH))],
        out_specs=[vrow, _full((CONV_HALO, 8, CONV_WIDTH))],
        out_shape=[S((L, 2 * CONV_WIDTH), f32), S((CONV_HALO, 8, CONV_WIDTH), f32)],
        scratch_shapes=_conv_scratch(tm) * 2 + [pltpu.VMEM((tm, CONV_WIDTH), f32)],
        compiler_params=_cp("arbitrary"),
    )(dzc, dzc, v, v, w_dw)


def _to_segments(a):
    L, c = a.shape
    return a.reshape(SEGMENTS, L // SEGMENTS, c).transpose(1, 0, 2).reshape(L, c)


def _from_segments(a):
    L, c = a.shape
    return a.reshape(L // SEGMENTS, SEGMENTS, c).transpose(1, 0, 2).reshape(L, c)


def _block_diag(ms):
    n = len(ms)

    def body(*refs):
        for a in range(n):
            out = refs[n + a]
            out[...] = jnp.zeros_like(out)
            for g in range(S5_GROUPS):
                rows = slice(g * S5_GROUP_CH, (g + 1) * S5_GROUP_CH)
                out[rows, g * S5_STATE:(g + 1) * S5_STATE] = refs[a][rows, :].astype(bf16)

    return pl.pallas_call(body, name="s5_block_diag", out_shape=[S((S5_WIDTH, S5_LANES), bf16)] * n,
                          compiler_params=pltpu.CompilerParams(vmem_limit_bytes=VMEM_LIMIT))(
        *[m.reshape(S5_WIDTH, S5_STATE) for m in ms])


def _diag_blocks(ms):
    n = len(ms)
    per_chunk = SCAN_LANES // S5_STATE

    def body(*refs):
        for a in range(n):
            for g in range(S5_GROUPS):
                rows = slice(g * S5_GROUP_CH, (g + 1) * S5_GROUP_CH)
                at = g % per_chunk * S5_STATE
                refs[n + a][rows, :] = refs[a][rows, at:at + S5_STATE]

    out = pl.pallas_call(body, name="s5_diag_blocks", out_shape=[S((S5_WIDTH, S5_STATE), f32)] * n,
                         compiler_params=pltpu.CompilerParams(vmem_limit_bytes=VMEM_LIMIT))(*ms)
    return [o.reshape(S5_GROUPS, S5_GROUP_CH, S5_STATE) for o in out]


class _NoExchanges:
    def before(self, point):
        return ()

    def after(self, point):
        pass

    def alone(self, point):
        pass


def _ffn_block(x, p, tag, tm, sched, head=None):
    point = tag + "_up"
    h, dadg, dadu, a = _ffn_up(x, p[tag + "_norm"], p[tag + "_w_gate"], p[tag + "_w_up"], tm, 768, tag, ride=sched.before(point))
    sched.after(point)
    if head is None:
        out = _ffn_down(x, a, p[tag + "_w_down"], tm, tag)
    else:
        out = _ffn_down_loss(x, a, p[tag + "_w_down"], *head, tm, tag)
    return out, (h, dadg, dadu, a)


def _ffn_block_bwd(dxo, x, p, tag, saved, tm, grads, sched):
    h, dadg, dadu, a = saved
    dgate, dup, dxh = _ffn_bwd_act(dxo, p[tag + "_w_down"], dadg, dadu, tm, 768, tag, ride=sched.before(tag + "_bwd_act"))
    sched.after(tag + "_bwd_act")
    for which, lhs, rhs in (("gate", h, dgate), ("up", h, dup), ("down", a, dxh)):
        point = tag + "_dw_" + which
        grads[tag + "_w_" + which] = _mm_tn(lhs, rhs, bf16, point, ride=sched.before(point))
        sched.after(point)
    dx, grads[tag + "_norm"] = _ffn_bwd_in(dxo, x, p[tag + "_norm"], dgate, dup, p[tag + "_w_gate"], p[tag + "_w_up"], tm, tag,
                                           ride=sched.before(tag + "_bwd_in"))
    sched.after(tag + "_bwd_in")
    return dx


def _local_step(x, target, p, grads, sched):
    L = x.shape[0]
    tm = min(512, L)
    ts = min(256, L)
    ni = L // SEGMENTS
    bi = min(64, ni)

    def carried(point, fn, *args):
        out = fn(*args, ride=sched.before(point))
        sched.after(point)
        return out

    x1, saved1 = _ffn_block(x, p, "ffn1", tm, sched)

    h2, u_s5, v = _mix_in(x1, p["mix_norm"], p["w_in"], tm)
    b_t = lambda b: b.transpose(0, 2, 1)
    s5_in = (p["s5_lam_re"], p["s5_lam_im"], p["s5_log_dt"].reshape(S5_GROUPS, 1), b_t(p["s5_b_re"]), b_t(p["s5_b_im"]))
    abar_re, abar_im, bbar_re, bbar_im = _s5_params(*s5_in)
    a_re, a_im = abar_re.reshape(1, S5_LANES), abar_im.reshape(1, S5_LANES)
    bb_re, bb_im, cc_re, cc_im = _block_diag([bbar_re, bbar_im, p["s5_c_re"], p["s5_c_im"]])
    u_seg = _to_segments(u_s5)
    e_re, e_im = carried("s5_ends", _s5_ends, u_seg, a_re, a_im, bb_re, bb_im, False, bi, "s5_ends")
    s_re, s_im, y_lin = carried("s5_states", _s5_states, u_seg, a_re, a_im, bb_re, bb_im, cc_re, cc_im, e_re, e_im, bi)
    y_s5 = _from_segments(_s5_gate(y_lin, u_seg, p["s5_d"], p["s5_w_glu"], p["s5_b_glu"], tm))
    w_dw = jnp.pad(p["conv_w_dw"], ((0, CONV_HALO - CONV_K), (0, 0)))
    heads = jnp.arange(CONV_WIDTH) // CONV_HEAD
    avg = ((heads[:, None] == heads[None, :]).astype(f32) / CONV_HEAD).astype(bf16)
    y_conv, zc = carried("conv_fwd", _conv_fwd, v, w_dw, p["conv_b_dw"], p["conv_ln_g"], p["conv_ln_b"], avg, tm)
    x2 = _mix_out(x1, y_s5, y_conv, p["w_out"], tm)

    (dx3, grads["final_norm"], loss_terms), saved2 = _ffn_block(
        x2, p, "ffn2", tm, sched, head=(target, p["final_norm"].reshape(1, D_MODEL)))

    dx2 = _ffn_block_bwd(dx3, x2, p, "ffn2", saved2, tm, grads, sched)

    dy_s5, dy_conv, dx2b = carried("mix_out_bwd", _mix_out_bwd, dx2, p["w_out"], tm)
    grads["w_out"] = jnp.concatenate([_mm_tn(y_s5, dx2b, bf16, "dw_out_s5"), _mm_tn(y_conv, dx2b, bf16, "dw_out_conv")], axis=0)
    dy_lin, du_skip, dd8, grads["s5_w_glu"], dbg8 = _s5_read_bwd(
        _to_segments(dy_s5), y_lin, u_seg, p["s5_d"], p["s5_w_glu"], p["s5_b_glu"], tm)
    grads["s5_d"] = dd8.sum(axis=0, keepdims=True)
    grads["s5_b_glu"] = dbg8.sum(axis=0, keepdims=True)
    g_e_re, g_e_im = carried("s5_ends_bwd", _s5_ends, dy_lin, a_re, -a_im, cc_re, cc_im, True, bi, "s5_ends_bwd")
    du_seg, da_re8, da_im8, dbb_re, dbb_im, dcc_re, dcc_im = carried(
        "s5_states_bwd", _s5_states_bwd, dy_lin, u_seg, du_skip, s_re, s_im, a_re, -a_im, bb_re, bb_im, cc_re, cc_im,
        g_e_re, g_e_im, bi)
    d_abar = lambda a8: a8.sum(axis=0).reshape(S5_GROUPS, S5_STATE)
    grads["s5_c_re"], grads["s5_c_im"], d_bbr, d_bbi = _diag_blocks([dcc_re, dcc_im, dbb_re, dbb_im])
    d_lr, d_li, d_ld, d_br, d_bi = _s5_params_bwd(*s5_in, d_abar(da_re8), d_abar(da_im8), d_bbr, d_bbi)
    grads["s5_lam_re"], grads["s5_lam_im"], grads["s5_log_dt"] = d_lr, d_li, d_ld.reshape(1, S5_GROUPS)
    grads["s5_b_re"], grads["s5_b_im"] = b_t(d_br), b_t(d_bi)
    dzc, dlg8, dlb8, dbd8 = _conv_bwd_norm(dy_conv, zc, p["conv_ln_g"], p["conv_ln_b"], avg, tm)
    grads["conv_ln_g"] = dlg8.sum(axis=0, keepdims=True)
    grads["conv_ln_b"] = dlb8.sum(axis=0, keepdims=True)
    grads["conv_b_dw"] = dbd8.sum(axis=0, keepdims=True)
    dv, dw8 = carried("conv_bwd_taps", _conv_bwd_taps, dzc, v, w_dw, tm)
    grads["conv_w_dw"] = dw8.sum(axis=1)[:CONV_K]
    dx1, grads["mix_norm"], dub = _mix_in_bwd(dx2, x1, p["mix_norm"], _from_segments(du_seg), dv, p["w_in"], tm)
    grads["w_in"] = _mm_tn(h2, dub, bf16, "dw_in")

    grads["loss_terms"] = loss_terms
    dx0 = _ffn_block_bwd(dx1, x, p, "ffn1", saved1, tm, grads, sched)
    sched.alone("tail")
    return loss_terms, dx0


MESH = pl.DeviceIdType.MESH
ANY = pl.BlockSpec(memory_space=pl.ANY)


def _place():
    return lax.axis_index("x"), lax.axis_index("y"), lax.axis_index("c")


class _Exchange:
    def __init__(self, ins, out_shape, sems, start, finish):
        self.ins, self.out_shape, self.sems, self.start, self.finish = list(ins), list(out_shape), list(sems), start, finish
        self.out = None


def _pallas(body, *, ride=(), **kw):
    if not ride:
        return pl.pallas_call(body, **kw)

    def run(*args):
        out_shape = kw.get("out_shape", [])
        single = not isinstance(out_shape, (list, tuple))
        shapes = [out_shape] if single else list(out_shape)
        out_specs = [kw["out_specs"]] if single else list(kw.get("out_specs", []))
        grid = tuple(kw.get("grid", ()))
        scratch = list(kw.get("scratch_shapes", ()))
        n_in, n_out, n_scr = len(args), len(shapes), len(scratch)
        r_in = [len(e.ins) for e in ride]
        r_out = [len(e.out_shape) for e in ride]
        r_sem = [len(e.sems) for e in ride]

        def wrapped(*refs):
            own_in, refs = refs[:n_in], refs[n_in:]
            ex_in, refs = refs[:sum(r_in)], refs[sum(r_in):]
            own_out, refs = refs[:n_out], refs[n_out:]
            ex_out, refs = refs[:sum(r_out)], refs[sum(r_out):]
            own_scr, ex_sem = refs[:n_scr], refs[n_scr:]
            parts = []
            for e, ni, no, ns in zip(ride, r_in, r_out, r_sem):
                parts.append((e, ex_in[:ni], ex_out[:no], ex_sem[:ns]))
                ex_in, ex_out, ex_sem = ex_in[ni:], ex_out[no:], ex_sem[ns:]

            def at(step):
                def go():
                    for e, i, o, s in parts:
                        getattr(e, step)(i, o, s)
                if grid:
                    ids = [pl.program_id(d) for d in range(len(grid))]
                    when = [i == (0 if step == "start" else g - 1) for i, g in zip(ids, grid)]
                    pl.when(functools.reduce(lambda a, b: a & b, when))(go)
                else:
                    go()

            at("start")
            if body is not None:
                body(*own_in, *own_out, *own_scr)
            at("finish")

        outs = pl.pallas_call(
            wrapped, name=kw["name"], grid=grid,
            in_specs=list(kw.get("in_specs", [])) + [ANY] * sum(r_in),
            out_specs=out_specs + [ANY] * sum(r_out),
            out_shape=shapes + [s for e in ride for s in e.out_shape],
            scratch_shapes=scratch + [s for e in ride for s in e.sems],
            compiler_params=_cp(*["arbitrary"] * len(grid)),
        )(*args, *[a for e in ride for a in e.ins])
        own, rest = outs[:n_out], outs[n_out:]
        for e, no in zip(ride, r_out):
            e.out, rest = list(rest[:no]), rest[no:]
        return own[0] if single else own

    return run


def _exchange(ride, name):
    _pallas(None, ride=ride, name=name)()


def _columns(ref, block, width):
    return ref.at[:, pl.ds(pl.multiple_of(block * width, 128), width)]


def _gather(arrs, wide=()):
    n = len(arrs)

    def copies(ins, outs, sems):
        send_sems, recv_sems, local_sems = sems
        x, y, c = _place()
        me, sibling = (x, y, c), (x, y, 1 - c)
        chips = [(1 - x, y), (x, 1 - y), (1 - x, 1 - y)]

        def place(a, block):
            return _columns(outs[a], block, arrs[a].shape[1]) if a in wide else outs[a].at[block]

        def copy(a, k, block, to, src=None):
            px, py, pc = block
            dst = place(a, 4 * px + 2 * py + pc)
            return pltpu.make_async_remote_copy(
                src_ref=dst if src is None else src, dst_ref=dst, send_sem=send_sems.at[7 * a + k],
                recv_sem=recv_sems.at[7 * a + k], device_id=to, device_id_type=MESH)

        def own():
            local = [pltpu.make_async_copy(ins[a], place(a, 4 * x + 2 * y + c), local_sems.at[a]) for a in range(n)]
            remote = []
            for a in range(n):
                remote.append(copy(a, 0, me, sibling, src=ins[a]))
                remote += [copy(a, 1 + j, me, (*chip, c), src=ins[a]) for j, chip in enumerate(chips)]
            return local, remote

        return c, me, sibling, chips, copy, own

    def start(ins, outs, sems):
        local, remote = copies(ins, outs, sems)[-1]()
        for cp in local + remote:
            cp.start()

    def finish(ins, outs, sems):
        c, me, sibling, chips, copy, own = copies(ins, outs, sems)
        passed = []
        for j, chip in enumerate(chips):
            for a in range(n):
                copy(a, 1 + j, (*chip, c), me).wait_recv()
                passed.append(copy(a, 4 + j, (*chip, c), sibling))
                passed[-1].start()
        for a in range(n):
            copy(a, 0, sibling, me).wait_recv()
            for j, chip in enumerate(chips):
                copy(a, 4 + j, (*chip, 1 - c), me).wait_recv()
        local, remote = own()
        for cp in remote + passed:
            cp.wait_send()
        for cp in local:
            cp.wait()

    dma = pltpu.SemaphoreType.DMA
    shapes = [S((a.shape[0], N_DEV * a.shape[1]) if i in wide else (N_DEV, *a.shape), a.dtype) for i, a in enumerate(arrs)]
    return _Exchange(arrs, shapes, [dma((7 * n,)), dma((7 * n,)), dma((n,))], start, finish)


def _swap_with_sibling(gs, wide=()):
    n = len(gs)
    shape = lambda a: (gs[a].shape[0], gs[a].shape[1] // N_DEV) if a in wide else gs[a].shape[2:]

    def copies(ins, outs, sems):
        x, y, c = _place()

        def copy(a, k, src, dst):
            return pltpu.make_async_remote_copy(
                src_ref=src, dst_ref=dst, send_sem=sems[0].at[N_CHIP * a + k], recv_sem=sems[1].at[N_CHIP * a + k],
                device_id=(x, y, 1 - c), device_id_type=MESH)

        out = []
        for a in range(n):
            if a in wide:
                out += [copy(a, q, _columns(ins[a], 2 * q + 1 - c, shape(a)[1]), outs[a].at[q]) for q in range(N_CHIP)]
            else:
                out.append(copy(a, 0, ins[a].at[:, 1 - c], outs[a]))
        return out

    def start(ins, outs, sems):
        for cp in copies(ins, outs, sems):
            cp.start()

    def finish(ins, outs, sems):
        for cp in copies(ins, outs, sems):
            cp.wait()

    dma = pltpu.SemaphoreType.DMA
    return _Exchange(gs, [S((N_CHIP, *shape(a)), gs[a].dtype) for a in range(n)], [dma((N_CHIP * n,)), dma((N_CHIP * n,))],
                     start, finish)


def _swap_with_chips(ps):
    n = len(ps)

    def copies(ins, outs, sems):
        x, y, c = _place()
        q = 2 * x + y
        peers = [(x, 1 - y), (1 - x, y), (1 - x, 1 - y)]

        def copy(a, j, slot_from, slot_to):
            px, py = peers[j]
            return pltpu.make_async_remote_copy(
                src_ref=ins[a].at[slot_from], dst_ref=outs[a].at[slot_to], send_sem=sems[0].at[3 * a + j],
                recv_sem=sems[1].at[3 * a + j], device_id=(px, py, c), device_id_type=MESH)

        sends = lambda: [copy(a, j, 2 * peers[j][0] + peers[j][1], q) for a in range(n) for j in range(3)]
        lands = lambda: [copy(a, j, q, 2 * peers[j][0] + peers[j][1]) for a in range(n) for j in range(3)]
        return sends, lands

    def start(ins, outs, sems):
        for cp in copies(ins, outs, sems)[0]():
            cp.start()

    def finish(ins, outs, sems):
        sends, lands = copies(ins, outs, sems)
        for cp in lands():
            cp.wait_recv()
        for cp in sends():
            cp.wait_send()

    dma = pltpu.SemaphoreType.DMA
    return _Exchange(ps, [S(p.shape, p.dtype) for p in ps], [dma((3 * n,)), dma((3 * n,))], start, finish)


def _row_tile(rows, cols, itemsize):
    t = rows
    while t * cols * itemsize > (1 << 20) and t % 32 == 0:
        t //= 2
    return t


def _add_sibling(g4, st, core, name):
    _, R, C = st.shape
    tr = _row_tile(R, C, 4)

    def body(c_ref, g_ref, s_ref, o_ref):
        o_ref[...] = (g_ref[...].astype(f32) + s_ref[...].astype(f32)).astype(bf16)

    if g4.ndim == 2:
        mine = pl.BlockSpec((tr, C), lambda q, i, c: (i, 2 * q + c[0]))
    else:
        mine = pl.BlockSpec((None, None, tr, C), lambda q, i, c: (q, c[0], i, 0))
    return pl.pallas_call(
        body, name=name,
        grid_spec=pltpu.PrefetchScalarGridSpec(
            num_scalar_prefetch=1, grid=(N_CHIP, R // tr),
            in_specs=[mine,
                      pl.BlockSpec((None, tr, C), lambda q, i, c: (q, i, 0))],
            out_specs=pl.BlockSpec((None, tr, C), lambda q, i, c: (q, i, 0))),
        out_shape=S((N_CHIP, R, C), bf16),
        compiler_params=_cp("parallel", "parallel"),
    )(core, g4, st)


def _adamw(w, g, m, v):
    m = B1 * m + (1.0 - B1) * g
    v = B2 * v + (1.0 - B2) * (g * g)
    m_hat = m / (1.0 - B1 ** STEP)
    v_hat = v / (1.0 - B2 ** STEP)
    return -LR * (m_hat / (jnp.sqrt(v_hat) + ADAM_EPS) + WD * w), m, v


def _adam_sharded(w, m, v, part, got, slots, name):
    R, C = w.shape
    _, Rp, Cp = part.shape
    tr = _row_tile(R, Cp, 4) if Rp == R else R

    def body(s_ref, w_ref, m_ref, v_ref, p_ref, a_ref, b_ref, c_ref, g_out, d_out, m_out, v_out):
        g = p_ref[...].astype(f32) + a_ref[...].astype(f32) + b_ref[...].astype(f32) + c_ref[...].astype(f32)
        g = g[:, :C]
        g_out[...] = g
        d_out[...], m_out[...], v_out[...] = _adamw(w_ref[...], g, m_ref[...], v_ref[...])

    shard = pl.BlockSpec((tr, C), lambda i, s: (i, 0))
    slot = lambda k: pl.BlockSpec((None, tr, Cp), lambda i, s: (s[k], i, 0))
    return pl.pallas_call(
        body, name=name,
        grid_spec=pltpu.PrefetchScalarGridSpec(
            num_scalar_prefetch=1, grid=(R // tr,),
            in_specs=[shard, shard, shard, slot(0), slot(1), slot(2), slot(3)],
            out_specs=[shard] * 4),
        out_shape=[S((R, C), f32)] * 4,
        compiler_params=_cp("parallel"),
    )(slots, w, m, v, part, got, got, got)


def _adam_replicated(w, m, v, gathered, name):
    R = w.shape[0]

    def body(w_ref, m_ref, v_ref, g_ref, g_out, d_out, m_out, v_out):
        g = g_ref[0]
        for d in range(1, N_DEV):
            g = g + g_ref[d]
        g_out[...] = g
        d_out[...], m_out[...], v_out[...] = _adamw(w_ref[...], g, m_ref[...], v_ref[...])

    return pl.pallas_call(body, name=name, out_shape=[S((R, 128), f32)] * 4,
                          compiler_params=pltpu.CompilerParams(vmem_limit_bytes=VMEM_LIMIT))(w, m, v, gathered)


WEIGHTS = ["ffn1_norm", "ffn1_w_gate", "ffn1_w_up", "ffn1_w_down", "mix_norm", "w_in", "s5_lam_re", "s5_lam_im", "s5_log_dt",
           "s5_b_re", "s5_b_im", "s5_c_re", "s5_c_im", "s5_d", "s5_w_glu", "s5_b_glu", "conv_w_dw", "conv_b_dw", "conv_ln_g",
           "conv_ln_b", "w_out", "ffn2_norm", "ffn2_w_gate", "ffn2_w_up", "ffn2_w_down", "final_norm"]
SHARDED = ["ffn1_w_gate", "ffn1_w_up", "ffn1_w_down", "w_in", "s5_w_glu", "conv_w_dw", "w_out", "ffn2_w_gate", "ffn2_w_up",
           "ffn2_w_down"]
REPLICATED = [n for n in WEIGHTS if n not in SHARDED]
PACK = 8 * 128


def _is_up(n):
    return n.endswith("w_gate") or n.endswith("w_up")


def _shard_to_wire(n, w):
    if _is_up(n):
        w = jnp.pad(w, ((0, 0), (0, FF_SHARD_PAD - FF_SHARD)))
    elif n.endswith("w_down"):
        w = jnp.pad(w, ((0, FF_SHARD_PAD - FF_SHARD), (0, 0)))
    elif n == "conv_w_dw":
        return jnp.pad(w, ((0, CONV_HALO - CONV_K), (0, 0)))
    return w.astype(bf16)


def _to_wire(shards, ride):
    names = list(shards)
    shapes = [jax.eval_shape(functools.partial(_shard_to_wire, n), shards[n]) for n in names]

    def body(*refs):
        for src, dst in zip(refs[:len(names)], refs[len(names):]):
            (r, c), (rp, cp) = src.shape, dst.shape
            dst[:r, :c] = src[...].astype(dst.dtype)
            if cp > c:
                dst[:, c:] = jnp.zeros((rp, cp - c), dst.dtype)
            if rp > r:
                dst[r:, :] = jnp.zeros((rp - r, cp), dst.dtype)

    out = _pallas(body, ride=ride, name="to_wire", out_shape=shapes, in_specs=[pl.BlockSpec(memory_space=pltpu.VMEM)] * len(names),
                  out_specs=[pl.BlockSpec(memory_space=pltpu.VMEM)] * len(names))(*[shards[n] for n in names])
    return dict(zip(names, out))


def _gathered_to_full(n, g):
    if _is_up(n):
        return g
    if n == "conv_w_dw":
        return g.transpose(1, 0, 2).reshape(CONV_HALO, CONV_WIDTH)[:CONV_K]
    if n == "w_in":
        return g.transpose(1, 0, 2).reshape(g.shape[1], N_DEV * g.shape[2])
    return g.reshape(N_DEV * g.shape[1], g.shape[2])


def _grad_to_blocks(n, g):
    if n == "conv_w_dw":
        g = jnp.pad(g, ((0, CONV_HALO - CONV_K), (0, 0)))
    g = g.astype(bf16)
    if n in ("w_in", "conv_w_dw"):
        g = g.reshape(g.shape[0], N_DEV, g.shape[1] // N_DEV).transpose(1, 0, 2)
    else:
        g = g.reshape(N_DEV, g.shape[0] // N_DEV, g.shape[1])
    return g.reshape(N_CHIP, 2, *g.shape[1:])


def _pack(parts):
    flat = [a.reshape(-1) for a in parts]
    size = sum(a.size for a in flat)
    return jnp.concatenate(flat + [jnp.zeros((-size % PACK,), f32)]).reshape(-1, 128)


def _unpack(packed, like):
    out, at = [], 0
    flat = packed.reshape(-1)
    for a in like:
        out.append(flat[at:at + a.size].reshape(a.shape))
        at += a.size
    return out


REPLICATED_LATE = ["ffn1_norm"]
REPLICATED_EARLY = [n for n in REPLICATED if n not in REPLICATED_LATE]

PLAN = {
    "start": [("gather", ["ffn1_w_gate", "ffn1_w_up"])],
    "ffn1_up": [("gather", ["ffn1_w_down", "w_in", "w_out", "s5_w_glu", "conv_w_dw"])],
    "s5_ends": [("gather", ["ffn2_w_gate"])],
    "s5_states": [("gather", ["ffn2_w_up"])],
    "conv_fwd": [("gather", ["ffn2_w_down"])],
    "ffn2_dw_up": [("sibling", ["ffn2_w_gate"])],
    "ffn2_dw_down": [("sibling", ["ffn2_w_up"])],
    "mix_out_bwd": [("sibling", ["ffn2_w_down"])],
    "s5_ends_bwd": [("chips", ["ffn2_w_gate"])],
    "s5_states_bwd": [("chips", ["ffn2_w_up"])],
    "conv_bwd_taps": [("chips", ["ffn2_w_down"])],
    "ffn1_bwd_act": [("sibling", ["w_in", "s5_w_glu", "conv_w_dw", "w_out"]), ("replicated", REPLICATED_EARLY)],
    "ffn1_dw_gate": [("chips", ["w_in", "s5_w_glu", "conv_w_dw", "w_out"])],
    "ffn1_dw_up": [("sibling", ["ffn1_w_gate"])],
    "ffn1_dw_down": [("sibling", ["ffn1_w_up"]), ("chips", ["ffn1_w_gate"])],
    "ffn1_bwd_in": [("sibling", ["ffn1_w_down"]), ("chips", ["ffn1_w_up"])],
    "tail": [("chips", ["ffn1_w_down"]), ("replicated", REPLICATED_LATE)],
}


class _Schedule:
    def __init__(self, wire, p, grads, core, pack_mine):
        self.wire, self.p, self.grads, self.core, self.pack_mine = wire, p, grads, core, pack_mine
        self.partial, self.reduced, self.everyone, self.pending = {}, {}, {}, []

    def before(self, point):
        assert not self.pending
        for kind, names in PLAN.get(point, ()):
            wide = [i for i, n in enumerate(names) if _is_up(n)]
            if kind == "gather":
                given = [self.wire[n] for n in names]
                ex = _gather(given, wide)
            elif kind == "sibling":
                given = [self.grads[n] if _is_up(n) else _grad_to_blocks(n, self.grads[n]) for n in names]
                ex = _swap_with_sibling(given, wide)
            elif kind == "chips":
                given = [self.partial.pop(n) for n in names]
                ex = _swap_with_chips(given)
            else:
                given = [self.pack_mine(names)]
                ex = _gather(given)
            self.pending.append((kind, names, given, ex))
        return [ex for _, _, _, ex in self.pending]

    def after(self, point):
        for kind, names, given, ex in self.pending:
            if kind == "gather":
                for n, g in zip(names, ex.out):
                    self.p[n] = _gathered_to_full(n, g)
            elif kind == "sibling":
                for n, blocks, got in zip(names, given, ex.out):
                    self.partial[n] = _add_sibling(blocks, got, self.core, "reduce_add_" + n)
            elif kind == "chips":
                for n, part, got in zip(names, given, ex.out):
                    self.reduced[n] = (part, got)
            else:
                self.everyone[names[0]] = ex.out[0]
        self.pending = []

    def alone(self, point):
        _exchange(self.before(point), point)
        self.after(point)


def kernel(x, ffn1_norm, ffn1_w_gate, ffn1_w_up, ffn1_w_down, mix_norm, w_in, s5_lam_re, s5_lam_im, s5_log_dt, s5_b_re, s5_b_im, s5_c_re, s5_c_im, s5_d, s5_w_glu, s5_b_glu, conv_w_dw, conv_b_dw, conv_ln_g, conv_ln_b, w_out, ffn2_norm, ffn2_w_gate, ffn2_w_up, ffn2_w_down, final_norm, loss_target, m_ffn1_norm, m_ffn1_w_gate, m_ffn1_w_up, m_ffn1_w_down, m_mix_norm, m_w_in, m_s5_lam_re, m_s5_lam_im, m_s5_log_dt, m_s5_b_re, m_s5_b_im, m_s5_c_re, m_s5_c_im, m_s5_d, m_s5_w_glu, m_s5_b_glu, m_conv_w_dw, m_conv_b_dw, m_conv_ln_g, m_conv_ln_b, m_w_out, m_ffn2_norm, m_ffn2_w_gate, m_ffn2_w_up, m_ffn2_w_down, m_final_norm, v_ffn1_norm, v_ffn1_w_gate, v_ffn1_w_up, v_ffn1_w_down, v_mix_norm, v_w_in, v_s5_lam_re, v_s5_lam_im, v_s5_log_dt, v_s5_b_re, v_s5_b_im, v_s5_c_re, v_s5_c_im, v_s5_d, v_s5_w_glu, v_s5_b_glu, v_conv_w_dw, v_conv_b_dw, v_conv_ln_g, v_conv_ln_b, v_w_out, v_ffn2_norm, v_ffn2_w_gate, v_ffn2_w_up, v_ffn2_w_down, v_final_norm):
    args = locals()
    w = {n: args[n] for n in WEIGHTS}
    m = {n: args["m_" + n] for n in WEIGHTS}
    v = {n: args["v_" + n] for n in WEIGHTS}
    shard2d = lambda a: a.reshape(a.shape[-2:])
    xq, yq, cq = _place()
    q = 2 * xq + yq
    slots = jnp.stack([q, q ^ 1, q ^ 2, q ^ 3]).astype(jnp.int32)

    p = {}
    for n in REPLICATED:
        p[n] = w[n].reshape(w[n].shape[1:]) if w[n].ndim >= 3 else w[n]
    grads = {}
    zero = jnp.zeros((1,), f32)
    lead = lambda names, first: [first] if names is REPLICATED_EARLY else []
    pack_mine = lambda names: _pack(lead(names, grads["loss_terms"].sum().reshape(1)) + [grads[n] for n in names])
    first = PLAN["start"][0][1]
    wire = {n: _shard_to_wire(n, shard2d(w[n])) for n in first}
    sched = _Schedule(wire, p, grads, jnp.reshape(cq, (1,)).astype(jnp.int32), pack_mine)
    wire.update(_to_wire({n: shard2d(w[n]) for n in SHARDED if n not in first}, sched.before("start")))
    sched.after("start")
    _, dx = _local_step(x[0], loss_target[0], p, grads, sched)

    out = {}
    for n in SHARDED:
        part, got = sched.reduced[n]
        rows = part.shape[1] if n == "conv_w_dw" else w[n].shape[-2]
        fit = lambda a: jnp.pad(shard2d(a), ((0, rows - a.shape[-2]), (0, 0)))
        res = _adam_sharded(fit(w[n]), fit(m[n]), fit(v[n]), part, got, slots, "adam_" + n)
        out[n] = [r[:w[n].shape[-2]].reshape(w[n].shape) for r in res]

    for names in (REPLICATED_EARLY, REPLICATED_LATE):
        head = lead(names, zero)
        like = head + [w[n] for n in names]
        packed = lambda d: _pack(head + [d[n] for n in names])
        res = _adam_replicated(packed(w), packed(m), packed(v), sched.everyone[names[0]], "adam_" + names[0])
        unpacked = [_unpack(r, like) for r in res]
        for i, n in enumerate(names):
            out[n] = [u[len(head) + i] for u in unpacked]
        if head:
            loss = unpacked[0][0].reshape(())

    return (loss, dx.reshape(x.shape), *[out[n][0] for n in WEIGHTS], *[out[n][1] for n in WEIGHTS],
            *[out[n][2] for n in WEIGHTS], *[out[n][3] for n in WEIGHTS])
```

```python
import functools

import jax
import jax.numpy as jnp
from jax import lax
from jax.experimental import pallas as pl
from jax.experimental.pallas import tpu as pltpu

f32 = jnp.float32
bf16 = jnp.bfloat16
S = jax.ShapeDtypeStruct

N_DEV = 8
N_CHIP = 4
D_MODEL = 1024
D_FF = 2816
FF_SHARD = D_FF // N_DEV
FF_SHARD_PAD = 384
FF_PAD = FF_SHARD_PAD * N_DEV
S5_WIDTH = 512
S5_GROUPS = 32
S5_GROUP_CH = 16
S5_STATE = 64
S5_LANES = S5_GROUPS * S5_STATE
CONV_WIDTH = 512
CONV_K = 31
CONV_HALO = 32
CONV_HEAD = 64
CONV_ROWS = 32
IN_COLS = S5_WIDTH + 2 * CONV_WIDTH
SEGMENTS = 8
SCAN_LANES = 512
EPS = 1e-6
LR, B1, B2, ADAM_EPS, WD, STEP = 0.001, 0.9, 0.999, 1e-08, 0.01, 10
VMEM_LIMIT = 56 * 1024 * 1024

NN = (((1,), (0,)), ((), ()))
NT = (((1,), (1,)), ((), ()))
TN = (((0,), (0,)), ((), ()))


def _dot(a, b, dims=NN):
    return lax.dot_general(a, b, dims, preferred_element_type=f32)


def _cp(*sem):
    return pltpu.CompilerParams(dimension_semantics=sem, vmem_limit_bytes=VMEM_LIMIT)


def _rms(x, g):
    return x * lax.rsqrt(jnp.mean(x * x, axis=-1, keepdims=True) + EPS) * g


def _rms_bwd(x, g, dh):
    _, vjp = jax.vjp(_rms, x, g)
    return vjp(dh)


def _sigmoid(x):
    return 1.0 / (1.0 + jnp.exp(-x))


def _gelu(x):
    return 0.5 * x * (1.0 + jnp.tanh(0.7978845608028654 * (x + 0.044715 * x * x * x)))


def _rows8(x):
    t, c = x.shape
    return x.reshape(t // 8, 8, c).sum(axis=0)


def _full(shape):
    return pl.BlockSpec(shape, lambda *_: (0,) * len(shape))


def _resident(shape):
    return pl.BlockSpec(shape, lambda *_: (0,) * len(shape), pipeline_mode=pl.Buffered(1))


def _ffn_up(x, g, wg, wu, tm, tn, tag, ride=()):
    L = x.shape[0]

    def body(x_ref, g_ref, wg_ref, wu_ref, h_ref, dadg_ref, dadu_ref, a_ref):
        h = _rms(x_ref[...], g_ref[...]).astype(bf16)
        h_ref[...] = h
        for j in range(FF_PAD // tn):
            cols = slice(j * tn, (j + 1) * tn)
            gate = _dot(h, wg_ref[cols, :], NT)
            up = _dot(h, wu_ref[cols, :], NT)
            sig = _sigmoid(gate)
            silu = gate * sig
            dadg_ref[:, cols] = (up * (sig + silu * (1.0 - sig))).astype(bf16)
            dadu_ref[:, cols] = silu.astype(bf16)
            a_ref[:, cols] = (silu * up).astype(bf16)

    row = pl.BlockSpec((tm, D_MODEL), lambda i: (i, 0))
    wide = pl.BlockSpec((tm, FF_PAD), lambda i: (i, 0))
    return _pallas(
        body, ride=ride, name=tag + "_up", grid=(L // tm,),
        in_specs=[row, _full((1, D_MODEL)), _resident((FF_PAD, D_MODEL)), _resident((FF_PAD, D_MODEL))],
        out_specs=[row, wide, wide, wide],
        out_shape=[S((L, D_MODEL), bf16)] + [S((L, FF_PAD), bf16)] * 3,
        compiler_params=_cp("parallel"),
    )(x, g, wg, wu)


def _ffn_down(x, a, wd, tm, tag):
    L = x.shape[0]

    def body(x_ref, a_ref, wd_ref, o_ref):
        o_ref[...] = x_ref[...] + 0.5 * _dot(a_ref[...], wd_ref[...])

    return pl.pallas_call(
        body, name=tag + "_down", grid=(L // tm,),
        in_specs=[pl.BlockSpec((tm, D_MODEL), lambda i: (i, 0)), pl.BlockSpec((tm, FF_PAD), lambda i: (i, 0)),
                  _resident((FF_PAD, D_MODEL))],
        out_specs=pl.BlockSpec((tm, D_MODEL), lambda i: (i, 0)),
        out_shape=S((L, D_MODEL), f32),
        compiler_params=_cp("parallel"),
    )(x, a, wd)


def _ffn_down_loss(x, a, wd, target, g, tm, tag):
    L = x.shape[0]

    def body(x_ref, a_ref, wd_ref, t_ref, g_ref, dx_ref, dg_ref, l_ref):
        @pl.when(pl.program_id(0) == 0)
        def _():
            dg_ref[...] = jnp.zeros_like(dg_ref)
            l_ref[...] = jnp.zeros_like(l_ref)

        xo = x_ref[...] + 0.5 * _dot(a_ref[...], wd_ref[...])
        g = g_ref[...]
        e = _rms(xo, g) - t_ref[...]
        l_ref[...] += _rows8(e * e) * (0.5 / D_MODEL)
        dx, dg = _rms_bwd(xo, g, e * (1.0 / D_MODEL))
        dx_ref[...] = dx
        dg_ref[...] += dg

    row = pl.BlockSpec((tm, D_MODEL), lambda i: (i, 0))
    return pl.pallas_call(
        body, name=tag + "_down_loss", grid=(L // tm,),
        in_specs=[row, pl.BlockSpec((tm, FF_PAD), lambda i: (i, 0)), _resident((FF_PAD, D_MODEL)), row, _full((1, D_MODEL))],
        out_specs=[row, _full((1, D_MODEL)), _full((8, D_MODEL))],
        out_shape=[S((L, D_MODEL), f32), S((1, D_MODEL), f32), S((8, D_MODEL), f32)],
        compiler_params=_cp("arbitrary"),
    )(x, a, wd, target, g)


def _ffn_bwd_act(dxo, wd, dadg, dadu, tm, tn, tag, ride=()):
    L = dxo.shape[0]

    def body(dx_ref, wd_ref, dadg_ref, dadu_ref, dgate_ref, dup_ref, dxh_ref):
        dxh = (0.5 * dx_ref[...]).astype(bf16)
        dxh_ref[...] = dxh
        for j in range(FF_PAD // tn):
            cols = slice(j * tn, (j + 1) * tn)
            da = _dot(dxh, wd_ref[cols, :], NT)
            dgate_ref[:, cols] = (da * dadg_ref[:, cols].astype(f32)).astype(bf16)
            dup_ref[:, cols] = (da * dadu_ref[:, cols].astype(f32)).astype(bf16)

    row = pl.BlockSpec((tm, D_MODEL), lambda i: (i, 0))
    wide = pl.BlockSpec((tm, FF_PAD), lambda i: (i, 0))
    return _pallas(
        body, ride=ride, name=tag + "_bwd_act", grid=(L // tm,),
        in_specs=[row, _resident((FF_PAD, D_MODEL)), wide, wide],
        out_specs=[wide, wide, row],
        out_shape=[S((L, FF_PAD), bf16), S((L, FF_PAD), bf16), S((L, D_MODEL), bf16)],
        compiler_params=_cp("parallel"),
    )(dxo, wd, dadg, dadu)


def _ffn_bwd_in(dxo, x, g, dgate, dup, wg, wu, tm, tag, ride=()):
    L = x.shape[0]

    def body(dxo_ref, x_ref, g_ref, dgate_ref, dup_ref, wg_ref, wu_ref, dx_ref, dg_ref):
        @pl.when(pl.program_id(0) == 0)
        def _():
            dg_ref[...] = jnp.zeros_like(dg_ref)

        dh = _dot(dgate_ref[...], wg_ref[...]) + _dot(dup_ref[...], wu_ref[...])
        dx, dg = _rms_bwd(x_ref[...], g_ref[...], dh)
        dx_ref[...] = dxo_ref[...] + dx
        dg_ref[...] += dg

    row = pl.BlockSpec((tm, D_MODEL), lambda i: (i, 0))
    wide = pl.BlockSpec((tm, FF_PAD), lambda i: (i, 0))
    return _pallas(
        body, ride=ride, name=tag + "_bwd_in", grid=(L // tm,),
        in_specs=[row, row, _full((1, D_MODEL)), wide, wide, _resident((FF_PAD, D_MODEL)), _resident((FF_PAD, D_MODEL))],
        out_specs=[row, _full((1, D_MODEL))],
        out_shape=[S((L, D_MODEL), f32), S((1, D_MODEL), f32)],
        compiler_params=_cp("arbitrary"),
    )(dxo, x, g, dgate, dup, wg, wu)


def _mm_tn(a, b, out_dtype, name, tm=512, tn=1024, ride=()):
    L, M = a.shape
    N = b.shape[1]
    tm, tn = min(tm, M), min(tn, N)
    while N % tn:
        tn //= 2

    def body(a_ref, b_ref, o_ref):
        o_ref[...] = _dot(a_ref[...].astype(bf16), b_ref[...].astype(bf16), TN).astype(out_dtype)

    return _pallas(
        body, ride=ride, name=name, grid=(M // tm, N // tn),
        in_specs=[pl.BlockSpec((L, tm), lambda i, j: (0, i)), pl.BlockSpec((L, tn), lambda i, j: (0, j))],
        out_specs=pl.BlockSpec((tm, tn), lambda i, j: (i, j)),
        out_shape=S((M, N), out_dtype),
        compiler_params=_cp("parallel", "parallel"),
    )(a, b)


def _mix_in(x, g, w_in, tm):
    L = x.shape[0]

    def body(x_ref, g_ref, w_ref, h_ref, us_ref, v_ref):
        h = _rms(x_ref[...], g_ref[...]).astype(bf16)
        h_ref[...] = h
        u = _dot(h, w_ref[...], NT)
        us_ref[...] = u[:, :S5_WIDTH]
        v_ref[...] = u[:, S5_WIDTH:]

    row = lambda c: pl.BlockSpec((tm, c), lambda i: (i, 0))
    return pl.pallas_call(
        body, name="mix_in", grid=(L // tm,),
        in_specs=[row(D_MODEL), _full((1, D_MODEL)), _full((IN_COLS, D_MODEL))],
        out_specs=[row(D_MODEL), row(S5_WIDTH), row(2 * CONV_WIDTH)],
        out_shape=[S((L, D_MODEL), bf16), S((L, S5_WIDTH), f32), S((L, 2 * CONV_WIDTH), f32)],
        compiler_params=_cp("parallel"),
    )(x, g, w_in)


def _mix_in_bwd(dxo, x, g, du_s5, dv, w_in, tm):
    L = x.shape[0]

    def body(dxo_ref, x_ref, g_ref, dus_ref, dv_ref, w_ref, dx_ref, dg_ref, dub_ref):
        @pl.when(pl.program_id(0) == 0)
        def _():
            dg_ref[...] = jnp.zeros_like(dg_ref)

        dus = dus_ref[...].astype(bf16)
        dvb = dv_ref[...].astype(bf16)
        dub_ref[:, :S5_WIDTH] = dus
        dub_ref[:, S5_WIDTH:] = dvb
        dh = _dot(dus, w_ref[:S5_WIDTH, :]) + _dot(dvb, w_ref[S5_WIDTH:, :])
        dx, dg = _rms_bwd(x_ref[...], g_ref[...], dh)
        dx_ref[...] = dxo_ref[...] + dx
        dg_ref[...] += dg

    row = lambda c: pl.BlockSpec((tm, c), lambda i: (i, 0))
    return pl.pallas_call(
        body, name="mix_in_bwd", grid=(L // tm,),
        in_specs=[row(D_MODEL), row(D_MODEL), _full((1, D_MODEL)), row(S5_WIDTH), row(2 * CONV_WIDTH),
                  _full((IN_COLS, D_MODEL))],
        out_specs=[row(D_MODEL), _full((1, D_MODEL)), row(IN_COLS)],
        out_shape=[S((L, D_MODEL), f32), S((1, D_MODEL), f32), S((L, IN_COLS), bf16)],
        compiler_params=_cp("arbitrary"),
    )(dxo, x, g, du_s5, dv, w_in)


def _mix_out(x, y_s5, y_conv, w_out, tm):
    L = x.shape[0]

    def body(x_ref, ys_ref, yc_ref, w_ref, o_ref):
        o_ref[...] = x_ref[...] + _dot(ys_ref[...], w_ref[:S5_WIDTH, :]) + _dot(yc_ref[...], w_ref[S5_WIDTH:, :])

    row = lambda c: pl.BlockSpec((tm, c), lambda i: (i, 0))
    return pl.pallas_call(
        body, name="mix_out", grid=(L // tm,),
        in_specs=[row(D_MODEL), row(S5_WIDTH), row(CONV_WIDTH), _full((D_MODEL, D_MODEL))],
        out_specs=row(D_MODEL), out_shape=S((L, D_MODEL), f32),
        compiler_params=_cp("parallel"),
    )(x, y_s5, y_conv, w_out)


def _mix_out_bwd(dx, w_out, tm, ride=()):
    L = dx.shape[0]

    def body(dx_ref, w_ref, dys_ref, dyc_ref, dxb_ref):
        dxb = dx_ref[...].astype(bf16)
        dxb_ref[...] = dxb
        dys_ref[...] = _dot(dxb, w_ref[:S5_WIDTH, :], NT)
        dyc_ref[...] = _dot(dxb, w_ref[S5_WIDTH:, :], NT)

    row = lambda c: pl.BlockSpec((tm, c), lambda i: (i, 0))
    return _pallas(
        body, ride=ride, name="mix_out_bwd", grid=(L // tm,),
        in_specs=[row(D_MODEL), _full((D_MODEL, D_MODEL))],
        out_specs=[row(S5_WIDTH), row(CONV_WIDTH), row(D_MODEL)],
        out_shape=[S((L, S5_WIDTH), f32), S((L, CONV_WIDTH), f32), S((L, D_MODEL), bf16)],
        compiler_params=_cp("parallel"),
    )(dx, w_out)


def _s5_discretise(lam_re, lam_im, log_dt, b_re, b_im):
    dt = jnp.exp(log_dt)
    mag = jnp.exp(lam_re * dt)
    abar_re = mag * jnp.cos(lam_im * dt)
    abar_im = mag * jnp.sin(lam_im * dt)
    den = lam_re * lam_re + lam_im * lam_im
    num_re = abar_re - 1.0
    f_re = ((num_re * lam_re + abar_im * lam_im) / den)[:, None, :]
    f_im = ((abar_im * lam_re - num_re * lam_im) / den)[:, None, :]
    return abar_re, abar_im, f_re * b_re - f_im * b_im, f_re * b_im + f_im * b_re


def _s5_params(lam_re, lam_im, log_dt, b_re, b_im):
    def body(lr, li, ld, br, bi, ar_ref, ai_ref, bbr_ref, bbi_ref):
        ar, ai, bbr, bbi = _s5_discretise(lr[...], li[...], ld[...], br[...], bi[...])
        ar_ref[...], ai_ref[...], bbr_ref[...], bbi_ref[...] = ar, ai, bbr, bbi

    gp = S((S5_GROUPS, S5_STATE), f32)
    gcp = S((S5_GROUPS, S5_GROUP_CH, S5_STATE), f32)
    return pl.pallas_call(body, name="s5_params", out_shape=[gp, gp, gcp, gcp])(lam_re, lam_im, log_dt, b_re, b_im)


def _s5_params_bwd(lam_re, lam_im, log_dt, b_re, b_im, d_ar, d_ai, d_bbr, d_bbi):
    def body(lr, li, ld, br, bi, car, cai, cbr, cbi, o_lr, o_li, o_ld, o_br, o_bi):
        _, vjp = jax.vjp(_s5_discretise, lr[...], li[...], ld[...], br[...], bi[...])
        o_lr[...], o_li[...], o_ld[...], o_br[...], o_bi[...] = vjp((car[...], cai[...], cbr[...], cbi[...]))

    gp = S((S5_GROUPS, S5_STATE), f32)
    gcp = S((S5_GROUPS, S5_GROUP_CH, S5_STATE), f32)
    return pl.pallas_call(body, name="s5_params_bwd", out_shape=[gp, gp, S((S5_GROUPS, 1), f32), gcp, gcp])(
        lam_re, lam_im, log_dt, b_re, b_im, d_ar, d_ai, d_bbr, d_bbi)


def _cmul(ar, ai, br, bi):
    return ar * br - ai * bi, ar * bi + ai * br


def _segment_starts(er, ei, ar, ai, steps, reverse):
    pr, pi = ar, ai
    n = 1
    while n < steps:
        pr, pi = _cmul(pr, pi, pr, pi)
        n *= 2
    assert n == steps
    row = lax.broadcasted_iota(jnp.int32, (SEGMENTS, SCAN_LANES), 0)
    hr = jnp.zeros((1, SCAN_LANES), f32)
    hi = jnp.zeros((1, SCAN_LANES), f32)
    out_r = jnp.zeros((SEGMENTS, SCAN_LANES), f32)
    out_i = jnp.zeros((SEGMENTS, SCAN_LANES), f32)
    order = range(SEGMENTS - 1, 0, -1) if reverse else range(0, SEGMENTS - 1)
    for r in order:
        qr, qi = _cmul(pr, pi, hr, hi)
        hr, hi = qr + er[r:r + 1, :], qi + ei[r:r + 1, :]
        nxt = r - 1 if reverse else r + 1
        out_r = jnp.where(row == nxt, hr, out_r)
        out_i = jnp.where(row == nxt, hi, out_i)
    return out_r, out_i


def _s5_read_bwd(dout, y_lin, u, d_skip, w_glu, b_glu, tm):
    L = u.shape[0]

    def body(do_ref, yl_ref, u_ref, d_ref, w_ref, b_ref, dyl_ref, du_ref, dd_ref, dw_ref, db_ref):
        @pl.when(pl.program_id(0) == 0)
        def _():
            dd_ref[...] = jnp.zeros_like(dd_ref)
            dw_ref[...] = jnp.zeros_like(dw_ref)
            db_ref[...] = jnp.zeros_like(db_ref)

        u, d, dout = u_ref[...], d_ref[...], do_ref[...]
        y, gelu_vjp = jax.vjp(_gelu, yl_ref[...] + d * u)
        yb = y.astype(bf16)
        sig = _sigmoid(_dot(yb, w_ref[...]) + b_ref[...])
        dz = dout * y * sig * (1.0 - sig)
        dzb = dz.astype(bf16)
        dy = dout * sig + _dot(dzb, w_ref[...], NT)
        (dyp,) = gelu_vjp(dy)
        dyl_ref[...] = dyp.astype(bf16)
        du_ref[...] = d * dyp
        dd_ref[...] += _rows8(dyp * u)
        db_ref[...] += _rows8(dz)
        dw_ref[...] += _dot(yb, dzb, TN)

    row = pl.BlockSpec((tm, S5_WIDTH), lambda i: (i, 0))
    vec = _full((1, S5_WIDTH))
    part = _full((8, S5_WIDTH))
    return pl.pallas_call(
        body, name="s5_read_bwd", grid=(L // tm,),
        in_specs=[row, row, row, vec, _full((S5_WIDTH, S5_WIDTH)), vec],
        out_specs=[row, row, part, _full((S5_WIDTH, S5_WIDTH)), part],
        out_shape=[S((L, S5_WIDTH), bf16), S((L, S5_WIDTH), f32), S((8, S5_WIDTH), f32),
                   S((S5_WIDTH, S5_WIDTH), f32), S((8, S5_WIDTH), f32)],
        compiler_params=_cp("arbitrary"),
    )(dout, y_lin, u, d_skip, w_glu, b_glu)


S5_CHUNK_CH = SCAN_LANES // S5_STATE * S5_GROUP_CH


def _s5_specs(L, bi, reverse):
    nb = L // (bi * SEGMENTS)
    blk = (lambda c, j: (nb - 1 - j, c)) if reverse else (lambda c, j: (j, c))
    chan = pl.BlockSpec((bi * SEGMENTS, S5_CHUNK_CH), blk)
    state = pl.BlockSpec((bi * SEGMENTS, SCAN_LANES), blk)
    mat = pl.BlockSpec((S5_CHUNK_CH, SCAN_LANES), lambda c, j: (c, c))
    vec = pl.BlockSpec((1, SCAN_LANES), lambda c, j: (0, c))
    tile = pl.BlockSpec((SEGMENTS, SCAN_LANES), lambda c, j: (0, c))
    return nb, chan, state, mat, vec, tile


def _drive_into(src, m_re, m_im, negate_im, dr_ref, di_ref, bi):
    d_im = _dot(src, m_im)
    dr_ref[...] = _dot(src, m_re).reshape(bi, SEGMENTS, SCAN_LANES)
    di_ref[...] = (-d_im if negate_im else d_im).reshape(bi, SEGMENTS, SCAN_LANES)


def _s5_ends(src, a_re, a_im, m_re, m_im, reverse, bi, name, ride=()):
    L = src.shape[0]
    nb, chan, _, mat, vec, tile = _s5_specs(L, bi, reverse)

    def body(src_ref, ar_ref, ai_ref, mr_ref, mi_ref, er_ref, ei_ref, dr_ref, di_ref):
        @pl.when(pl.program_id(1) == 0)
        def _():
            er_ref[...] = jnp.zeros_like(er_ref)
            ei_ref[...] = jnp.zeros_like(ei_ref)

        _drive_into(src_ref[...].astype(bf16), mr_ref[...], mi_ref[...], reverse, dr_ref, di_ref, bi)
        ar = jnp.broadcast_to(ar_ref[...], (SEGMENTS, SCAN_LANES))
        ai = jnp.broadcast_to(ai_ref[...], (SEGMENTS, SCAN_LANES))

        def step(n, c):
            i = (bi - 1 - n) if reverse else n
            pr, pi = _cmul(ar, ai, c[0], c[1])
            return pr + dr_ref[i], pi + di_ref[i]

        er_ref[...], ei_ref[...] = lax.fori_loop(0, bi, step, (er_ref[...], ei_ref[...]), unroll=4)

    out = S((SEGMENTS, S5_LANES), f32)
    return _pallas(
        body, ride=ride, name=name, grid=(S5_LANES // SCAN_LANES, nb),
        in_specs=[chan, vec, vec, mat, mat], out_specs=[tile, tile], out_shape=[out, out],
        scratch_shapes=[pltpu.VMEM((bi, SEGMENTS, SCAN_LANES), f32)] * 2,
        compiler_params=_cp("parallel", "arbitrary"),
    )(src, a_re, a_im, m_re, m_im)


def _s5_states(u, a_re, a_im, bb_re, bb_im, cc_re, cc_im, e_re, e_im, bi, ride=()):
    L = u.shape[0]
    nb, chan, state, mat, vec, tile = _s5_specs(L, bi, False)
    rows = bi * SEGMENTS

    def body(u_ref, ar_ref, ai_ref, br_ref, bi_ref, cr_ref, ci_ref, er_ref, ei_ref, sr_ref, si_ref, yl_ref,
             hr_ref, hi_ref, dr_ref, di_ref):
        @pl.when(pl.program_id(1) == 0)
        def _():
            hr_ref[...], hi_ref[...] = _segment_starts(er_ref[...], ei_ref[...], ar_ref[...], ai_ref[...], L // SEGMENTS, False)

        _drive_into(u_ref[...].astype(bf16), br_ref[...], bi_ref[...], False, dr_ref, di_ref, bi)
        ar = jnp.broadcast_to(ar_ref[...], (SEGMENTS, SCAN_LANES))
        ai = jnp.broadcast_to(ai_ref[...], (SEGMENTS, SCAN_LANES))

        def step(i, c):
            pr, pi = _cmul(ar, ai, c[0], c[1])
            nr, nim = pr + dr_ref[i], pi + di_ref[i]
            dr_ref[i] = nr
            di_ref[i] = nim
            return nr, nim

        hr_ref[...], hi_ref[...] = lax.fori_loop(0, bi, step, (hr_ref[...], hi_ref[...]), unroll=4)
        sr = dr_ref[...].reshape(rows, SCAN_LANES).astype(bf16)
        si = di_ref[...].reshape(rows, SCAN_LANES).astype(bf16)
        sr_ref[...] = sr
        si_ref[...] = si
        yl_ref[...] = _dot(sr, cr_ref[...], NT) - _dot(si, ci_ref[...], NT)

    return _pallas(
        body, ride=ride, name="s5_states", grid=(S5_LANES // SCAN_LANES, nb),
        in_specs=[chan, vec, vec, mat, mat, mat, mat, tile, tile],
        out_specs=[state, state, chan],
        out_shape=[S((L, S5_LANES), bf16)] * 2 + [S((L, S5_WIDTH), f32)],
        scratch_shapes=[pltpu.VMEM((SEGMENTS, SCAN_LANES), f32)] * 2 + [pltpu.VMEM((bi, SEGMENTS, SCAN_LANES), f32)] * 2,
        compiler_params=_cp("parallel", "arbitrary"),
    )(u, a_re, a_im, bb_re, bb_im, cc_re, cc_im, e_re, e_im)


def _s5_states_bwd(dy, u, du_skip, s_re, s_im, a_re, a_im, bb_re, bb_im, cc_re, cc_im, e_re, e_im, bi, ride=()):
    L = u.shape[0]
    nb, chan, state, mat, vec, tile = _s5_specs(L, bi, True)
    rows = bi * SEGMENTS
    per = rows // 16

    def body(dy_ref, u_ref, dus_ref, sr_ref, si_ref, pr_ref, pi_ref, lr_ref, li_ref, ar_ref, ai_ref, br_ref, bi_ref, cr_ref,
             ci_ref, er_ref, ei_ref, du_ref, dar_ref, dai_ref, dbr_ref, dbi_ref, dcr_ref, dci_ref,
             hr_ref, hi_ref, gr_ref, gi_ref, fr_ref, fi_ref):
        j = pl.program_id(1)

        @pl.when(j == 0)
        def _():
            hr_ref[...], hi_ref[...] = _segment_starts(er_ref[...], ei_ref[...], ar_ref[...], ai_ref[...], L // SEGMENTS, True)
            for ref in (dar_ref, dai_ref, dbr_ref, dbi_ref, dcr_ref, dci_ref):
                ref[...] = jnp.zeros_like(ref)

        dy = dy_ref[...]
        sr, si = sr_ref[...], si_ref[...]
        _drive_into(dy, cr_ref[...], ci_ref[...], True, gr_ref, gi_ref, bi)
        fr_ref[...] = sr.astype(f32).reshape(bi, SEGMENTS, SCAN_LANES)
        fi_ref[...] = si.astype(f32).reshape(bi, SEGMENTS, SCAN_LANES)
        ar = jnp.broadcast_to(ar_ref[...], (SEGMENTS, SCAN_LANES))
        ai = jnp.broadcast_to(ai_ref[...], (SEGMENTS, SCAN_LANES))

        def step(n, c):
            i = bi - 1 - n
            gr, gi, accr, acci = c
            qr, qi = _cmul(ar, ai, gr, gi)
            gr, gi = qr + gr_ref[i], qi + gi_ref[i]
            gr_ref[i] = gr
            gi_ref[i] = gi
            pr, pi = fr_ref[i - 1], fi_ref[i - 1]
            return gr, gi, accr + (gr * pr + gi * pi), acci + (gi * pr - gr * pi)

        gr, gi, accr, acci = lax.fori_loop(0, bi - 1, step, (hr_ref[...], hi_ref[...], dar_ref[...], dai_ref[...]), unroll=3)
        qr, qi = _cmul(ar, ai, gr, gi)
        gr, gi = qr + gr_ref[0], qi + gi_ref[0]
        gr_ref[0] = gr
        gi_ref[0] = gi
        hr_ref[...], hi_ref[...] = gr, gi
        row = lax.broadcasted_iota(jnp.int32, (SEGMENTS, SCAN_LANES), 0)
        first = j == nb - 1
        older = lambda ref: ref[...].astype(f32)[SEGMENTS:, :]
        wrap_r = jnp.where(row == 0, 0.0, pltpu.roll(older(lr_ref), 1, 0))
        wrap_i = jnp.where(row == 0, 0.0, pltpu.roll(older(li_ref), 1, 0))
        pr = jnp.where(first, wrap_r, older(pr_ref))
        pi = jnp.where(first, wrap_i, older(pi_ref))
        dar_ref[...] = accr + gr * pr + gi * pi
        dai_ref[...] = acci + gi * pr - gr * pi

        g_re = gr_ref[...].reshape(rows, SCAN_LANES).astype(bf16)
        g_im = gi_ref[...].reshape(rows, SCAN_LANES).astype(bf16)
        ub = u_ref[...].astype(bf16)
        du_ref[...] = dus_ref[...] + _dot(g_re, br_ref[...], NT) + _dot(g_im, bi_ref[...], NT)
        dbr_ref[...] += _dot(ub, g_re, TN)
        dbi_ref[...] += _dot(ub, g_im, TN)
        dcr_ref[...] += _dot(dy, sr, TN)
        dci_ref[...] -= _dot(dy, si, TN)

    prev = pl.BlockSpec((16, SCAN_LANES), lambda c, j: (jnp.maximum((nb - 1 - j) * per - 1, 0), c))
    last = pl.BlockSpec((16, SCAN_LANES), lambda c, j: (L // 16 - 1, c))
    grad = pl.BlockSpec((S5_CHUNK_CH, SCAN_LANES), lambda c, j: (c, 0))
    big = pltpu.VMEM((bi, SEGMENTS, SCAN_LANES), f32)
    return _pallas(
        body, ride=ride, name="s5_states_bwd", grid=(S5_LANES // SCAN_LANES, nb),
        in_specs=[chan, chan, chan, state, state, prev, prev, last, last, vec, vec, mat, mat, mat, mat, tile, tile],
        out_specs=[chan, tile, tile, grad, grad, grad, grad],
        out_shape=[S((L, S5_WIDTH), f32)] + [S((SEGMENTS, S5_LANES), f32)] * 2 + [S((S5_WIDTH, SCAN_LANES), f32)] * 4,
        scratch_shapes=[pltpu.VMEM((SEGMENTS, SCAN_LANES), f32)] * 2 + [big] * 4,
        compiler_params=_cp("parallel", "arbitrary"),
    )(dy, u, du_skip, s_re, s_im, s_re, s_im, s_re, s_im, a_re, a_im, bb_re, bb_im, cc_re, cc_im, e_re, e_im)


def _s5_gate(y_lin, u, d_skip, w_glu, b_glu, tm, ride=()):
    L = u.shape[0]

    def body(yl_ref, u_ref, d_ref, w_ref, b_ref, o_ref):
        y = _gelu(yl_ref[...] + d_ref[...] * u_ref[...])
        z = _dot(y.astype(bf16), w_ref[...]) + b_ref[...]
        o_ref[...] = (y * _sigmoid(z)).astype(bf16)

    row = pl.BlockSpec((tm, S5_WIDTH), lambda i: (i, 0))
    vec = _full((1, S5_WIDTH))
    return _pallas(
        body, ride=ride, name="s5_gate", grid=(L // tm,),
        in_specs=[row, row, vec, _full((S5_WIDTH, S5_WIDTH)), vec],
        out_specs=row, out_shape=S((L, S5_WIDTH), bf16),
        compiler_params=_cp("parallel"),
    )(y_lin, u, d_skip, w_glu, b_glu)


def _group_mean(x, avg):
    hi = x.astype(bf16)
    lo = (x - hi.astype(f32)).astype(bf16)
    return _dot(hi, avg) + _dot(lo, avg)


def _conv_act(zn, ln_g, ln_b):
    t = zn * ln_g + ln_b
    return t * _sigmoid(t)


def _glu_padded(v_ref, halo_ref, zpad_ref, tm):
    v = v_ref[...]
    vh = halo_ref[...]
    zh = vh[:, :CONV_WIDTH] * _sigmoid(vh[:, CONV_WIDTH:])
    zpad_ref[:CONV_HALO, :] = jnp.where(pl.program_id(0) > 0, zh, 0.0)
    zpad_ref[CONV_HALO:CONV_HALO + tm, :] = v[:, :CONV_WIDTH] * _sigmoid(v[:, CONV_WIDTH:])
    zpad_ref[CONV_HALO + tm:, :] = jnp.zeros((8, CONV_WIDTH), f32)


def _shifted(pad_ref, sh_ref, tm):
    for b in range(8):
        sh_ref[b] = pad_ref[pl.ds(b, tm + CONV_HALO), :]


def _window(sh_ref, r0, off, rows):
    return sh_ref[off % 8, pl.ds(pl.multiple_of(r0 + 8 * (off // 8), 8), rows), :]


def _tap_sum(w_ref, sh_ref, taps, out_ref, tm, bias):
    def chunk(c, carry):
        r0 = pl.multiple_of(c * CONV_ROWS, CONV_ROWS)
        acc = jnp.zeros((CONV_ROWS, CONV_WIDTH), f32) + bias
        for k, off in taps:
            acc = acc + w_ref[k:k + 1, :] * _window(sh_ref, r0, off, CONV_ROWS)
        out_ref[pl.ds(r0, CONV_ROWS), :] = acc
        return carry

    lax.fori_loop(0, tm // CONV_ROWS, chunk, 0)


FWD_TAPS = [(k, CONV_HALO - (CONV_K - 1) + k) for k in range(CONV_K)]
BWD_TAPS = [(k, CONV_K - 1 - k) for k in range(CONV_K)]


def _conv_specs(tm):
    per = tm // CONV_HALO
    vrow = pl.BlockSpec((tm, 2 * CONV_WIDTH), lambda i: (i, 0))
    vhalo = pl.BlockSpec((CONV_HALO, 2 * CONV_WIDTH), lambda i: (jnp.maximum(i * per - 1, 0), 0))
    return vrow, vhalo


def _conv_scratch(tm):
    return [pltpu.VMEM((tm + CONV_HALO + 8, CONV_WIDTH), f32), pltpu.VMEM((8, tm + CONV_HALO, CONV_WIDTH), f32)]


def _conv_fwd(v, w_dw, b_dw, ln_g, ln_b, avg, tm, ride=()):
    L = v.shape[0]

    def body(v_ref, halo_ref, w_ref, b_ref, g_ref, bb_ref, avg_ref, o_ref, zc_ref, zpad_ref, zs_ref):
        _glu_padded(v_ref, halo_ref, zpad_ref, tm)
        _shifted(zpad_ref, zs_ref, tm)
        _tap_sum(w_ref, zs_ref, FWD_TAPS, zc_ref, tm, b_ref[...])
        zc = zc_ref[...]
        xc = zc - _group_mean(zc, avg_ref[...])
        zn = xc * lax.rsqrt(_group_mean(xc * xc, avg_ref[...]) + EPS)
        o_ref[...] = _conv_act(zn, g_ref[...], bb_ref[...]).astype(bf16)

    vrow, vhalo = _conv_specs(tm)
    vec = _full((1, CONV_WIDTH))
    row = pl.BlockSpec((tm, CONV_WIDTH), lambda i: (i, 0))
    return _pallas(
        body, ride=ride, name="conv_fwd", grid=(L // tm,),
        in_specs=[vrow, vhalo, _full((CONV_HALO, CONV_WIDTH)), vec, vec, vec, _full((CONV_WIDTH, CONV_WIDTH))],
        out_specs=[row, row], out_shape=[S((L, CONV_WIDTH), bf16), S((L, CONV_WIDTH), f32)],
        scratch_shapes=_conv_scratch(tm),
        compiler_params=_cp("arbitrary"),
    )(v, v, w_dw, b_dw, ln_g, ln_b, avg)


def _conv_bwd_norm(dout, zc, ln_g, ln_b, avg, tm):
    L = zc.shape[0]

    def body(do_ref, zc_ref, g_ref, bb_ref, avg_ref, dzc_ref, dg_ref, db_ref, dbd_ref):
        @pl.when(pl.program_id(0) == 0)
        def _():
            dg_ref[...] = jnp.zeros_like(dg_ref)
            db_ref[...] = jnp.zeros_like(db_ref)
            dbd_ref[...] = jnp.zeros_like(dbd_ref)

        avg = avg_ref[...]
        zc = zc_ref[...]
        xc = zc - _group_mean(zc, avg)
        rstd = lax.rsqrt(_group_mean(xc * xc, avg) + EPS)
        xhat = xc * rstd
        _, act_vjp = jax.vjp(_conv_act, xhat, g_ref[...], bb_ref[...])
        dxhat, dg, db = act_vjp(do_ref[...])
        dzc = rstd * (dxhat - _group_mean(dxhat, avg) - xhat * _group_mean(dxhat * xhat, avg))
        dzc_ref[...] = dzc
        dg_ref[0:1, :] += dg
        db_ref[0:1, :] += db
        dbd_ref[...] += _rows8(dzc)

    vec = _full((1, CONV_WIDTH))
    row = pl.BlockSpec((tm, CONV_WIDTH), lambda i: (i, 0))
    part = _full((8, CONV_WIDTH))
    return pl.pallas_call(
        body, name="conv_bwd_norm", grid=(L // tm,),
        in_specs=[row, row, vec, vec, _full((CONV_WIDTH, CONV_WIDTH))],
        out_specs=[row, part, part, part],
        out_shape=[S((L, CONV_WIDTH), f32)] + [S((8, CONV_WIDTH), f32)] * 3,
        compiler_params=_cp("arbitrary"),
    )(dout, zc, ln_g, ln_b, avg)


def _conv_bwd_taps(dzc, v, w_dw, tm, ride=()):
    L = v.shape[0]
    nt = L // tm
    per = tm // CONV_HALO

    def body(d_ref, dn_ref, v_ref, halo_ref, w_ref, dv_ref, dw_ref, zpad_ref, zs_ref, dpad_ref, ds_ref, dz_ref):
        i = pl.program_id(0)

        @pl.when(i == 0)
        def _():
            dw_ref[...] = jnp.zeros_like(dw_ref)

        _glu_padded(v_ref, halo_ref, zpad_ref, tm)
        _shifted(zpad_ref, zs_ref, tm)
        dpad_ref[:tm, :] = d_ref[...]
        dpad_ref[tm:tm + CONV_HALO, :] = jnp.where(i < nt - 1, dn_ref[...], 0.0)
        dpad_ref[tm + CONV_HALO:, :] = jnp.zeros((8, CONV_WIDTH), f32)
        _shifted(dpad_ref, ds_ref, tm)
        _tap_sum(w_ref, ds_ref, BWD_TAPS, dz_ref, tm, 0.0)

        for first in range(0, CONV_K, 8):
            taps = FWD_TAPS[first:first + 8]

            def chunk(c, accs, taps=taps):
                r0 = pl.multiple_of(c * 8, 8)
                d = d_ref[pl.ds(r0, 8), :]
                return tuple(acc + d * _window(zs_ref, r0, off, 8) for acc, (_, off) in zip(accs, taps))

            accs = lax.fori_loop(0, tm // 8, chunk, tuple(jnp.zeros((8, CONV_WIDTH), f32) for _ in taps), unroll=2)
            for acc, (k, _) in zip(accs, taps):
                dw_ref[k] += acc

        dz = dz_ref[...]
        v = v_ref[...]
        sig = _sigmoid(v[:, CONV_WIDTH:])
        dv_ref[:, :CONV_WIDTH] = dz * sig
        dv_ref[:, CONV_WIDTH:] = dz * v[:, :CONV_WIDTH] * sig * (1.0 - sig)

    vrow, vhalo = _conv_specs(tm)
    row = pl.BlockSpec((tm, CONV_WIDTH), lambda i: (i, 0))
    nxt = pl.BlockSpec((CONV_HALO, CONV_WIDTH), lambda i: (jnp.minimum((i + 1) * per, nt * per - 1), 0))
    return _pallas(
        body, ride=ride, name="conv_bwd_taps", grid=(nt,),
        in_specs=[row, nxt, vrow, vhalo, _full((CONV_HALO, CONV_WIDTH))],
        out_specs=[vrow, _full((CONV_HALO, 8, CONV_WIDTH))],
        out_shape=[S((L, 2 * CONV_WIDTH), f32), S((CONV_HALO, 8, CONV_WIDTH), f32)],
        scratch_shapes=_conv_scratch(tm) * 2 + [pltpu.VMEM((tm, CONV_WIDTH), f32)],
        compiler_params=_cp("arbitrary"),
    )(dzc, dzc, v, v, w_dw)


def _to_segments(a):
    L, c = a.shape
    return a.reshape(SEGMENTS, L // SEGMENTS, c).transpose(1, 0, 2).reshape(L, c)


def _from_segments(a):
    L, c = a.shape
    return a.reshape(L // SEGMENTS, SEGMENTS, c).transpose(1, 0, 2).reshape(L, c)


def _block_diag(ms):
    n = len(ms)

    def body(*refs):
        for a in range(n):
            out = refs[n + a]
            out[...] = jnp.zeros_like(out)
            for g in range(S5_GROUPS):
                rows = slice(g * S5_GROUP_CH, (g + 1) * S5_GROUP_CH)
                out[rows, g * S5_STATE:(g + 1) * S5_STATE] = refs[a][rows, :].astype(bf16)

    return pl.pallas_call(body, name="s5_block_diag", out_shape=[S((S5_WIDTH, S5_LANES), bf16)] * n,
                          compiler_params=pltpu.CompilerParams(vmem_limit_bytes=VMEM_LIMIT))(
        *[m.reshape(S5_WIDTH, S5_STATE) for m in ms])


def _diag_blocks(ms):
    n = len(ms)
    per_chunk = SCAN_LANES // S5_STATE

    def body(*refs):
        for a in range(n):
            for g in range(S5_GROUPS):
                rows = slice(g * S5_GROUP_CH, (g + 1) * S5_GROUP_CH)
                at = g % per_chunk * S5_STATE
                refs[n + a][rows, :] = refs[a][rows, at:at + S5_STATE]

    out = pl.pallas_call(body, name="s5_diag_blocks", out_shape=[S((S5_WIDTH, S5_STATE), f32)] * n,
                         compiler_params=pltpu.CompilerParams(vmem_limit_bytes=VMEM_LIMIT))(*ms)
    return [o.reshape(S5_GROUPS, S5_GROUP_CH, S5_STATE) for o in out]


class _NoExchanges:
    def before(self, point):
        return ()

    def after(self, point):
        pass

    def alone(self, point):
        pass


def _ffn_block(x, p, tag, tm, sched, head=None):
    point = tag + "_up"
    h, dadg, dadu, a = _ffn_up(x, p[tag + "_norm"], p[tag + "_w_gate"], p[tag + "_w_up"], tm, 768, tag, ride=sched.before(point))
    sched.after(point)
    if head is None:
        out = _ffn_down(x, a, p[tag + "_w_down"], tm, tag)
    else:
        out = _ffn_down_loss(x, a, p[tag + "_w_down"], *head, tm, tag)
    return out, (h, dadg, dadu, a)


def _ffn_block_bwd(dxo, x, p, tag, saved, tm, grads, sched):
    h, dadg, dadu, a = saved
    dgate, dup, dxh = _ffn_bwd_act(dxo, p[tag + "_w_down"], dadg, dadu, tm, 768, tag, ride=sched.before(tag + "_bwd_act"))
    sched.after(tag + "_bwd_act")
    for which, lhs, rhs in (("gate", dgate, h), ("up", dup, h), ("down", a, dxh)):
        point = tag + "_dw_" + which
        grads[tag + "_w_" + which] = _mm_tn(lhs, rhs, bf16, point, ride=sched.before(point))
        sched.after(point)
    dx, grads[tag + "_norm"] = _ffn_bwd_in(dxo, x, p[tag + "_norm"], dgate, dup, p[tag + "_w_gate"], p[tag + "_w_up"], tm, tag,
                                           ride=sched.before(tag + "_bwd_in"))
    sched.after(tag + "_bwd_in")
    return dx


def _local_step(x, target, p, grads, sched):
    L = x.shape[0]
    tm = min(512, L)
    ts = min(256, L)
    ni = L // SEGMENTS
    bi = min(64, ni)

    def carried(point, fn, *args):
        out = fn(*args, ride=sched.before(point))
        sched.after(point)
        return out

    x1, saved1 = _ffn_block(x, p, "ffn1", tm, sched)

    h2, u_s5, v = _mix_in(x1, p["mix_norm"], p["w_in"], tm)
    s5_in = (p["s5_lam_re"], p["s5_lam_im"], p["s5_log_dt"].reshape(S5_GROUPS, 1), p["s5_b_re"], p["s5_b_im"])
    abar_re, abar_im, bbar_re, bbar_im = _s5_params(*s5_in)
    a_re, a_im = abar_re.reshape(1, S5_LANES), abar_im.reshape(1, S5_LANES)
    bb_re, bb_im, cc_re, cc_im = _block_diag([bbar_re, bbar_im, p["s5_c_re"], p["s5_c_im"]])
    u_seg = _to_segments(u_s5)
    e_re, e_im = carried("s5_ends", _s5_ends, u_seg, a_re, a_im, bb_re, bb_im, False, bi, "s5_ends")
    s_re, s_im, y_lin = carried("s5_states", _s5_states, u_seg, a_re, a_im, bb_re, bb_im, cc_re, cc_im, e_re, e_im, bi)
    y_s5 = _from_segments(_s5_gate(y_lin, u_seg, p["s5_d"], p["s5_w_glu"], p["s5_b_glu"], tm))
    w_dw = jnp.pad(p["conv_w_dw"], ((0, CONV_HALO - CONV_K), (0, 0)))
    heads = jnp.arange(CONV_WIDTH) // CONV_HEAD
    avg = ((heads[:, None] == heads[None, :]).astype(f32) / CONV_HEAD).astype(bf16)
    y_conv, zc = carried("conv_fwd", _conv_fwd, v, w_dw, p["conv_b_dw"], p["conv_ln_g"], p["conv_ln_b"], avg, tm)
    x2 = _mix_out(x1, y_s5, y_conv, p["w_out"], tm)

    (dx3, grads["final_norm"], loss_terms), saved2 = _ffn_block(
        x2, p, "ffn2", tm, sched, head=(target, p["final_norm"].reshape(1, D_MODEL)))

    dx2 = _ffn_block_bwd(dx3, x2, p, "ffn2", saved2, tm, grads, sched)

    dy_s5, dy_conv, dx2b = carried("mix_out_bwd", _mix_out_bwd, dx2, p["w_out"], tm)
    grads["w_out"] = jnp.concatenate([_mm_tn(y_s5, dx2b, bf16, "dw_out_s5"), _mm_tn(y_conv, dx2b, bf16, "dw_out_conv")], axis=0)
    dy_lin, du_skip, dd8, grads["s5_w_glu"], dbg8 = _s5_read_bwd(
        _to_segments(dy_s5), y_lin, u_seg, p["s5_d"], p["s5_w_glu"], p["s5_b_glu"], tm)
    grads["s5_d"] = dd8.sum(axis=0, keepdims=True)
    grads["s5_b_glu"] = dbg8.sum(axis=0, keepdims=True)
    g_e_re, g_e_im = carried("s5_ends_bwd", _s5_ends, dy_lin, a_re, -a_im, cc_re, cc_im, True, bi, "s5_ends_bwd")
    du_seg, da_re8, da_im8, dbb_re, dbb_im, dcc_re, dcc_im = carried(
        "s5_states_bwd", _s5_states_bwd, dy_lin, u_seg, du_skip, s_re, s_im, a_re, -a_im, bb_re, bb_im, cc_re, cc_im,
        g_e_re, g_e_im, bi)
    d_abar = lambda a8: a8.sum(axis=0).reshape(S5_GROUPS, S5_STATE)
    grads["s5_c_re"], grads["s5_c_im"], d_bbr, d_bbi = _diag_blocks([dcc_re, dcc_im, dbb_re, dbb_im])
    d_lr, d_li, d_ld, d_br, d_bi = _s5_params_bwd(*s5_in, d_abar(da_re8), d_abar(da_im8), d_bbr, d_bbi)
    grads["s5_lam_re"], grads["s5_lam_im"], grads["s5_log_dt"] = d_lr, d_li, d_ld.reshape(1, S5_GROUPS)
    grads["s5_b_re"], grads["s5_b_im"] = d_br, d_bi
    dzc, dlg8, dlb8, dbd8 = _conv_bwd_norm(dy_conv, zc, p["conv_ln_g"], p["conv_ln_b"], avg, tm)
    grads["conv_ln_g"] = dlg8.sum(axis=0, keepdims=True)
    grads["conv_ln_b"] = dlb8.sum(axis=0, keepdims=True)
    grads["conv_b_dw"] = dbd8.sum(axis=0, keepdims=True)
    dv, dw8 = carried("conv_bwd_taps", _conv_bwd_taps, dzc, v, w_dw, tm)
    grads["conv_w_dw"] = dw8.sum(axis=1)[:CONV_K]
    dx1, grads["mix_norm"], dub = _mix_in_bwd(dx2, x1, p["mix_norm"], _from_segments(du_seg), dv, p["w_in"], tm)
    grads["w_in"] = _mm_tn(dub, h2, bf16, "dw_in")

    grads["loss_terms"] = loss_terms
    dx0 = _ffn_block_bwd(dx1, x, p, "ffn1", saved1, tm, grads, sched)
    sched.alone("tail")
    return loss_terms, dx0


MESH = pl.DeviceIdType.MESH
ANY = pl.BlockSpec(memory_space=pl.ANY)


def _place():
    return lax.axis_index("x"), lax.axis_index("y"), lax.axis_index("c")


class _Exchange:
    def __init__(self, ins, out_shape, sems, start, finish):
        self.ins, self.out_shape, self.sems, self.start, self.finish = list(ins), list(out_shape), list(sems), start, finish
        self.out = None


def _pallas(body, *, ride=(), **kw):
    if not ride:
        return pl.pallas_call(body, **kw)

    def run(*args):
        out_shape = kw.get("out_shape", [])
        single = not isinstance(out_shape, (list, tuple))
        shapes = [out_shape] if single else list(out_shape)
        out_specs = [kw["out_specs"]] if single else list(kw.get("out_specs", []))
        grid = tuple(kw.get("grid", ()))
        scratch = list(kw.get("scratch_shapes", ()))
        n_in, n_out, n_scr = len(args), len(shapes), len(scratch)
        r_in = [len(e.ins) for e in ride]
        r_out = [len(e.out_shape) for e in ride]
        r_sem = [len(e.sems) for e in ride]

        def wrapped(*refs):
            own_in, refs = refs[:n_in], refs[n_in:]
            ex_in, refs = refs[:sum(r_in)], refs[sum(r_in):]
            own_out, refs = refs[:n_out], refs[n_out:]
            ex_out, refs = refs[:sum(r_out)], refs[sum(r_out):]
            own_scr, ex_sem = refs[:n_scr], refs[n_scr:]
            parts = []
            for e, ni, no, ns in zip(ride, r_in, r_out, r_sem):
                parts.append((e, ex_in[:ni], ex_out[:no], ex_sem[:ns]))
                ex_in, ex_out, ex_sem = ex_in[ni:], ex_out[no:], ex_sem[ns:]

            def at(step):
                def go():
                    for e, i, o, s in parts:
                        getattr(e, step)(i, o, s)
                if grid:
                    ids = [pl.program_id(d) for d in range(len(grid))]
                    when = [i == (0 if step == "start" else g - 1) for i, g in zip(ids, grid)]
                    pl.when(functools.reduce(lambda a, b: a & b, when))(go)
                else:
                    go()

            at("start")
            if body is not None:
                body(*own_in, *own_out, *own_scr)
            at("finish")

        outs = pl.pallas_call(
            wrapped, name=kw["name"], grid=grid,
            in_specs=list(kw.get("in_specs", [])) + [ANY] * sum(r_in),
            out_specs=out_specs + [ANY] * sum(r_out),
            out_shape=shapes + [s for e in ride for s in e.out_shape],
            scratch_shapes=scratch + [s for e in ride for s in e.sems],
            compiler_params=_cp(*["arbitrary"] * len(grid)),
        )(*args, *[a for e in ride for a in e.ins])
        own, rest = outs[:n_out], outs[n_out:]
        for e, no in zip(ride, r_out):
            e.out, rest = list(rest[:no]), rest[no:]
        return own[0] if single else own

    return run


def _exchange(ride, name):
    _pallas(None, ride=ride, name=name)()


def _gather(arrs):
    n = len(arrs)

    def copies(ins, outs, sems):
        send_sems, recv_sems, local_sems = sems
        x, y, c = _place()
        me, sibling = (x, y, c), (x, y, 1 - c)
        chips = [(1 - x, y), (x, 1 - y), (1 - x, 1 - y)]

        def place(a, block):
            return outs[a].at[block]

        def copy(a, k, block, to, src=None):
            px, py, pc = block
            dst = place(a, 4 * px + 2 * py + pc)
            return pltpu.make_async_remote_copy(
                src_ref=dst if src is None else src, dst_ref=dst, send_sem=send_sems.at[7 * a + k],
                recv_sem=recv_sems.at[7 * a + k], device_id=to, device_id_type=MESH)

        def own():
            local = [pltpu.make_async_copy(ins[a], place(a, 4 * x + 2 * y + c), local_sems.at[a]) for a in range(n)]
            remote = []
            for a in range(n):
                remote.append(copy(a, 0, me, sibling, src=ins[a]))
                remote += [copy(a, 1 + j, me, (*chip, c), src=ins[a]) for j, chip in enumerate(chips)]
            return local, remote

        return c, me, sibling, chips, copy, own

    def start(ins, outs, sems):
        local, remote = copies(ins, outs, sems)[-1]()
        for cp in local + remote:
            cp.start()

    def finish(ins, outs, sems):
        c, me, sibling, chips, copy, own = copies(ins, outs, sems)
        passed = []
        for j, chip in enumerate(chips):
            for a in range(n):
                copy(a, 1 + j, (*chip, c), me).wait_recv()
                passed.append(copy(a, 4 + j, (*chip, c), sibling))
                passed[-1].start()
        for a in range(n):
            copy(a, 0, sibling, me).wait_recv()
            for j, chip in enumerate(chips):
                copy(a, 4 + j, (*chip, 1 - c), me).wait_recv()
        local, remote = own()
        for cp in remote + passed:
            cp.wait_send()
        for cp in local:
            cp.wait()

    dma = pltpu.SemaphoreType.DMA
    shapes = [S((N_DEV, *a.shape), a.dtype) for a in arrs]
    return _Exchange(arrs, shapes, [dma((7 * n,)), dma((7 * n,)), dma((n,))], start, finish)


def _swap_with_sibling(gs):
    n = len(gs)

    def copies(ins, outs, sems):
        x, y, c = _place()
        return [pltpu.make_async_remote_copy(
            src_ref=ins[a].at[:, 1 - c], dst_ref=outs[a], send_sem=sems[0].at[a], recv_sem=sems[1].at[a],
            device_id=(x, y, 1 - c), device_id_type=MESH) for a in range(n)]

    def start(ins, outs, sems):
        for cp in copies(ins, outs, sems):
            cp.start()

    def finish(ins, outs, sems):
        for cp in copies(ins, outs, sems):
            cp.wait()

    dma = pltpu.SemaphoreType.DMA
    return _Exchange(gs, [S((N_CHIP, *g.shape[2:]), g.dtype) for g in gs], [dma((n,)), dma((n,))], start, finish)


def _swap_with_chips(ps):
    n = len(ps)

    def copies(ins, outs, sems):
        x, y, c = _place()
        q = 2 * x + y
        peers = [(x, 1 - y), (1 - x, y), (1 - x, 1 - y)]

        def copy(a, j, slot_from, slot_to):
            px, py = peers[j]
            return pltpu.make_async_remote_copy(
                src_ref=ins[a].at[slot_from], dst_ref=outs[a].at[slot_to], send_sem=sems[0].at[3 * a + j],
                recv_sem=sems[1].at[3 * a + j], device_id=(px, py, c), device_id_type=MESH)

        sends = lambda: [copy(a, j, 2 * peers[j][0] + peers[j][1], q) for a in range(n) for j in range(3)]
        lands = lambda: [copy(a, j, q, 2 * peers[j][0] + peers[j][1]) for a in range(n) for j in range(3)]
        return sends, lands

    def start(ins, outs, sems):
        for cp in copies(ins, outs, sems)[0]():
            cp.start()

    def finish(ins, outs, sems):
        sends, lands = copies(ins, outs, sems)
        for cp in lands():
            cp.wait_recv()
        for cp in sends():
            cp.wait_send()

    dma = pltpu.SemaphoreType.DMA
    return _Exchange(ps, [S(p.shape, p.dtype) for p in ps], [dma((3 * n,)), dma((3 * n,))], start, finish)


def _row_tile(rows, cols, itemsize):
    t = rows
    while t * cols * itemsize > (1 << 20) and t % 32 == 0:
        t //= 2
    return t


def _add_sibling(g4, st, core, name):
    _, R, C = st.shape
    tr = _row_tile(R, C, 4)

    def body(c_ref, g_ref, s_ref, o_ref):
        o_ref[...] = (g_ref[...].astype(f32) + s_ref[...].astype(f32)).astype(bf16)

    mine = pl.BlockSpec((None, None, tr, C), lambda q, i, c: (q, c[0], i, 0))
    return pl.pallas_call(
        body, name=name,
        grid_spec=pltpu.PrefetchScalarGridSpec(
            num_scalar_prefetch=1, grid=(N_CHIP, R // tr),
            in_specs=[mine,
                      pl.BlockSpec((None, tr, C), lambda q, i, c: (q, i, 0))],
            out_specs=pl.BlockSpec((None, tr, C), lambda q, i, c: (q, i, 0))),
        out_shape=S((N_CHIP, R, C), bf16),
        compiler_params=_cp("parallel", "parallel"),
    )(core, g4, st)


def _adamw(w, g, m, v):
    m = B1 * m + (1.0 - B1) * g
    v = B2 * v + (1.0 - B2) * (g * g)
    m_hat = m / (1.0 - B1 ** STEP)
    v_hat = v / (1.0 - B2 ** STEP)
    return -LR * (m_hat / (jnp.sqrt(v_hat) + ADAM_EPS) + WD * w), m, v


def _adam_sharded(w, m, v, part, got, slots, name):
    R, C = w.shape
    _, Rp, Cp = part.shape
    tr = _row_tile(R, Cp, 4) if Rp == R else R

    def body(s_ref, w_ref, m_ref, v_ref, p_ref, a_ref, b_ref, c_ref, g_out, d_out, m_out, v_out):
        g = p_ref[...].astype(f32) + a_ref[...].astype(f32) + b_ref[...].astype(f32) + c_ref[...].astype(f32)
        g = g[:, :C]
        g_out[...] = g
        d_out[...], m_out[...], v_out[...] = _adamw(w_ref[...], g, m_ref[...], v_ref[...])

    shard = pl.BlockSpec((tr, C), lambda i, s: (i, 0))
    slot = lambda k: pl.BlockSpec((None, tr, Cp), lambda i, s: (s[k], i, 0))
    return pl.pallas_call(
        body, name=name,
        grid_spec=pltpu.PrefetchScalarGridSpec(
            num_scalar_prefetch=1, grid=(R // tr,),
            in_specs=[shard, shard, shard, slot(0), slot(1), slot(2), slot(3)],
            out_specs=[shard] * 4),
        out_shape=[S((R, C), f32)] * 4,
        compiler_params=_cp("parallel"),
    )(slots, w, m, v, part, got, got, got)


def _adam_replicated(items, loss_terms, name):
    n = len(items)
    has_loss = loss_terms is not None

    def total(ref):
        g = ref[0]
        for d in range(1, N_DEV):
            g = g + ref[d]
        return g

    def body(*refs):
        ins, outs = refs[:4 * n + has_loss], refs[4 * n + has_loss:]
        for i in range(n):
            w_ref, m_ref, v_ref, g_ref = ins[4 * i:4 * i + 4]
            g = total(g_ref)
            outs[4 * i][...] = g
            outs[4 * i + 1][...], outs[4 * i + 2][...], outs[4 * i + 3][...] = _adamw(w_ref[...], g, m_ref[...], v_ref[...])
        if has_loss:
            outs[-1][...] = jnp.sum(total(ins[-1]), keepdims=True)

    flat = [a for item in items for a in item] + ([loss_terms] if has_loss else [])
    shapes = [S(item[0].shape, f32) for item in items for _ in range(4)] + ([S((1, 1), f32)] if has_loss else [])
    out = pl.pallas_call(body, name=name, out_shape=shapes,
                         compiler_params=pltpu.CompilerParams(vmem_limit_bytes=VMEM_LIMIT))(*flat)
    return [out[4 * i:4 * i + 4] for i in range(n)], (out[-1] if has_loss else None)


WEIGHTS = ["ffn1_norm", "ffn1_w_gate", "ffn1_w_up", "ffn1_w_down", "mix_norm", "w_in", "s5_lam_re", "s5_lam_im", "s5_log_dt",
           "s5_b_re", "s5_b_im", "s5_c_re", "s5_c_im", "s5_d", "s5_w_glu", "s5_b_glu", "conv_w_dw", "conv_b_dw", "conv_ln_g",
           "conv_ln_b", "w_out", "ffn2_norm", "ffn2_w_gate", "ffn2_w_up", "ffn2_w_down", "final_norm"]
SHARDED = ["ffn1_w_gate", "ffn1_w_up", "ffn1_w_down", "w_in", "s5_w_glu", "conv_w_dw", "w_out", "ffn2_w_gate", "ffn2_w_up",
           "ffn2_w_down"]
REPLICATED = [n for n in WEIGHTS if n not in SHARDED]
TRANSPOSED = ["ffn1_w_gate", "ffn1_w_up", "ffn2_w_gate", "ffn2_w_up", "w_in"]


def _shard_to_wire(n, w):
    if n.startswith("ffn"):
        w = jnp.pad(w, ((0, FF_SHARD_PAD - FF_SHARD), (0, 0)))
    elif n == "conv_w_dw":
        return jnp.pad(w, ((0, CONV_HALO - CONV_K), (0, 0)))
    return w.astype(bf16)


def _to_wire(shards, ride):
    names = list(shards)
    shapes = [jax.eval_shape(functools.partial(_shard_to_wire, n), shards[n]) for n in names]

    def body(*refs):
        for src, dst in zip(refs[:len(names)], refs[len(names):]):
            (r, c), (rp, cp) = src.shape, dst.shape
            dst[:r, :c] = src[...].astype(dst.dtype)
            if cp > c:
                dst[:, c:] = jnp.zeros((rp, cp - c), dst.dtype)
            if rp > r:
                dst[r:, :] = jnp.zeros((rp - r, cp), dst.dtype)

    out = _pallas(body, ride=ride, name="to_wire", out_shape=shapes, in_specs=[pl.BlockSpec(memory_space=pltpu.VMEM)] * len(names),
                  out_specs=[pl.BlockSpec(memory_space=pltpu.VMEM)] * len(names))(*[shards[n] for n in names])
    return dict(zip(names, out))


def _gathered_to_full(n, g):
    if n == "conv_w_dw":
        return g.transpose(1, 0, 2).reshape(CONV_HALO, CONV_WIDTH)[:CONV_K]
    return g.reshape(N_DEV * g.shape[1], g.shape[2])


def _grad_to_blocks(n, g):
    if n == "conv_w_dw":
        g = jnp.pad(g, ((0, CONV_HALO - CONV_K), (0, 0)))
        g = g.reshape(g.shape[0], N_DEV, g.shape[1] // N_DEV).transpose(1, 0, 2)
    else:
        g = g.reshape(N_DEV, g.shape[0] // N_DEV, g.shape[1])
    return g.astype(bf16).reshape(N_CHIP, 2, *g.shape[1:])


REPLICATED_LATE = ["ffn1_norm"]
REPLICATED_EARLY = [n for n in REPLICATED if n not in REPLICATED_LATE]

PLAN = {
    "start": [("gather", ["ffn1_w_gate", "ffn1_w_up"])],
    "ffn1_up": [("gather", ["ffn1_w_down", "w_in", "w_out", "s5_w_glu", "conv_w_dw"])],
    "s5_ends": [("gather", ["ffn2_w_gate"])],
    "s5_states": [("gather", ["ffn2_w_up"])],
    "conv_fwd": [("gather", ["ffn2_w_down"])],
    "ffn2_dw_up": [("sibling", ["ffn2_w_gate"])],
    "ffn2_dw_down": [("sibling", ["ffn2_w_up"])],
    "mix_out_bwd": [("sibling", ["ffn2_w_down"])],
    "s5_ends_bwd": [("chips", ["ffn2_w_gate"])],
    "s5_states_bwd": [("chips", ["ffn2_w_up"])],
    "conv_bwd_taps": [("chips", ["ffn2_w_down"])],
    "ffn1_bwd_act": [("sibling", ["w_in", "s5_w_glu", "conv_w_dw", "w_out"]), ("replicated", REPLICATED_EARLY)],
    "ffn1_dw_gate": [("chips", ["w_in", "s5_w_glu", "conv_w_dw", "w_out"])],
    "ffn1_dw_up": [("sibling", ["ffn1_w_gate"])],
    "ffn1_dw_down": [("sibling", ["ffn1_w_up"]), ("chips", ["ffn1_w_gate"])],
    "ffn1_bwd_in": [("sibling", ["ffn1_w_down"]), ("chips", ["ffn1_w_up"])],
    "tail": [("chips", ["ffn1_w_down"]), ("replicated", REPLICATED_LATE)],
}


class _Schedule:
    def __init__(self, wire, p, grads, core):
        self.wire, self.p, self.grads, self.core = wire, p, grads, core
        self.partial, self.reduced, self.everyone, self.pending = {}, {}, {}, []

    def before(self, point):
        assert not self.pending
        for kind, names in PLAN.get(point, ()):
            if kind == "gather":
                given = [self.wire[n] for n in names]
                ex = _gather(given)
            elif kind == "sibling":
                given = [_grad_to_blocks(n, self.grads[n]) for n in names]
                ex = _swap_with_sibling(given)
            elif kind == "chips":
                given = [self.partial.pop(n) for n in names]
                ex = _swap_with_chips(given)
            else:
                names = names + ["loss_terms"] * (names is REPLICATED_EARLY)
                given = [self.grads[n].reshape(self.p[n].shape) if n in self.p else self.grads[n] for n in names]
                ex = _gather(given)
            self.pending.append((kind, names, given, ex))
        return [ex for _, _, _, ex in self.pending]

    def after(self, point):
        for kind, names, given, ex in self.pending:
            if kind == "gather":
                for n, g in zip(names, ex.out):
                    self.p[n] = _gathered_to_full(n, g)
            elif kind == "sibling":
                for n, blocks, got in zip(names, given, ex.out):
                    self.partial[n] = _add_sibling(blocks, got, self.core, "reduce_add_" + n)
            elif kind == "chips":
                for n, part, got in zip(names, given, ex.out):
                    self.reduced[n] = (part, got)
            else:
                self.everyone.update(zip(names, ex.out))
        self.pending = []

    def alone(self, point):
        _exchange(self.before(point), point)
        self.after(point)


def kernel(x, ffn1_norm, ffn1_w_gate, ffn1_w_up, ffn1_w_down, mix_norm, w_in, s5_lam_re, s5_lam_im, s5_log_dt, s5_b_re, s5_b_im, s5_c_re, s5_c_im, s5_d, s5_w_glu, s5_b_glu, conv_w_dw, conv_b_dw, conv_ln_g, conv_ln_b, w_out, ffn2_norm, ffn2_w_gate, ffn2_w_up, ffn2_w_down, final_norm, loss_target, m_ffn1_norm, m_ffn1_w_gate, m_ffn1_w_up, m_ffn1_w_down, m_mix_norm, m_w_in, m_s5_lam_re, m_s5_lam_im, m_s5_log_dt, m_s5_b_re, m_s5_b_im, m_s5_c_re, m_s5_c_im, m_s5_d, m_s5_w_glu, m_s5_b_glu, m_conv_w_dw, m_conv_b_dw, m_conv_ln_g, m_conv_ln_b, m_w_out, m_ffn2_norm, m_ffn2_w_gate, m_ffn2_w_up, m_ffn2_w_down, m_final_norm, v_ffn1_norm, v_ffn1_w_gate, v_ffn1_w_up, v_ffn1_w_down, v_mix_norm, v_w_in, v_s5_lam_re, v_s5_lam_im, v_s5_log_dt, v_s5_b_re, v_s5_b_im, v_s5_c_re, v_s5_c_im, v_s5_d, v_s5_w_glu, v_s5_b_glu, v_conv_w_dw, v_conv_b_dw, v_conv_ln_g, v_conv_ln_b, v_w_out, v_ffn2_norm, v_ffn2_w_gate, v_ffn2_w_up, v_ffn2_w_down, v_final_norm):
    args = locals()
    w = {n: args[n] for n in WEIGHTS}
    m = {n: args["m_" + n] for n in WEIGHTS}
    v = {n: args["v_" + n] for n in WEIGHTS}
    xq, yq, cq = _place()
    q = 2 * xq + yq
    slots = jnp.stack([q, q ^ 1, q ^ 2, q ^ 3]).astype(jnp.int32)

    def shard2d(n, a):
        a = a.reshape(a.shape[-2:])
        return a.T if n in TRANSPOSED else a

    def view(n, a):
        if n.startswith("s5_b_") and a.ndim == 4:
            return a[0].transpose(0, 2, 1)
        return a[0] if a.ndim >= 3 else a.reshape(1, -1)

    def unview(n, a):
        return (a.transpose(0, 2, 1) if n.startswith("s5_b_") and a.ndim == 3 else a).reshape(w[n].shape)

    p = {n: view(n, w[n]) for n in REPLICATED}
    grads = {}
    first = PLAN["start"][0][1]
    wire = {n: _shard_to_wire(n, shard2d(n, w[n])) for n in first}
    sched = _Schedule(wire, p, grads, jnp.reshape(cq, (1,)).astype(jnp.int32))
    wire.update(_to_wire({n: shard2d(n, w[n]) for n in SHARDED if n not in first}, sched.before("start")))
    sched.after("start")
    _, dx = _local_step(x[0], loss_target[0], p, grads, sched)

    out = {}
    for n in SHARDED:
        part, got = sched.reduced[n]
        rows = part.shape[1] if n == "conv_w_dw" else shard2d(n, w[n]).shape[0]
        fit = lambda a: jnp.pad(shard2d(n, a), ((0, rows - shard2d(n, a).shape[0]), (0, 0)))
        res = _adam_sharded(fit(w[n]), fit(m[n]), fit(v[n]), part, got, slots, "adam_" + n)
        back = lambda r: r[:shard2d(n, w[n]).shape[0]]
        out[n] = [(back(r).T if n in TRANSPOSED else back(r)).reshape(w[n].shape) for r in res]

    for names in (REPLICATED_EARLY, REPLICATED_LATE):
        items = [(view(n, w[n]), view(n, m[n]), view(n, v[n]), sched.everyone[n]) for n in names]
        res, total = _adam_replicated(items, sched.everyone.get("loss_terms") if names is REPLICATED_EARLY else None,
                                      "adam_" + names[0])
        for n, r in zip(names, res):
            out[n] = [unview(n, a) for a in r]
        if total is not None:
            loss = total.reshape(())

    return (loss, dx.reshape(x.shape), *[out[n][0] for n in WEIGHTS], *[out[n][1] for n in WEIGHTS],
            *[out[n][2] for n in WEIGHTS], *[out[n][3] for n in WEIGHTS])
```

```python
import functools

import jax
import jax.numpy as jnp
from jax import lax
from jax.experimental import pallas as pl
from jax.experimental.pallas import tpu as pltpu

f32 = jnp.float32
bf16 = jnp.bfloat16
S = jax.ShapeDtypeStruct

N_DEV = 8
N_CHIP = 4
D_MODEL = 1024
D_FF = 2816
FF_SHARD = D_FF // N_DEV
FF_SHARD_PAD = 384
FF_PAD = FF_SHARD_PAD * N_DEV
S5_WIDTH = 512
S5_GROUPS = 32
S5_GROUP_CH = 16
S5_STATE = 64
S5_LANES = S5_GROUPS * S5_STATE
CONV_WIDTH = 512
CONV_K = 31
CONV_HALO = 32
CONV_HEAD = 64
CONV_ROWS = 32
IN_COLS = S5_WIDTH + 2 * CONV_WIDTH
SEGMENTS = 8
SCAN_LANES = 512
EPS = 1e-6
LR, B1, B2, ADAM_EPS, WD, STEP = 0.001, 0.9, 0.999, 1e-08, 0.01, 10
VMEM_LIMIT = 56 * 1024 * 1024

NN = (((1,), (0,)), ((), ()))
NT = (((1,), (1,)), ((), ()))
TN = (((0,), (0,)), ((), ()))


def _dot(a, b, dims=NN):
    return lax.dot_general(a, b, dims, preferred_element_type=f32)


def _cp(*sem):
    return pltpu.CompilerParams(dimension_semantics=sem, vmem_limit_bytes=VMEM_LIMIT)


def _rms(x, g):
    return x * lax.rsqrt(jnp.mean(x * x, axis=-1, keepdims=True) + EPS) * g


def _rms_bwd(x, g, dh):
    _, vjp = jax.vjp(_rms, x, g)
    return vjp(dh)


def _sigmoid(x):
    return 1.0 / (1.0 + jnp.exp(-x))


def _gelu(x):
    return 0.5 * x * (1.0 + jnp.tanh(0.7978845608028654 * (x + 0.044715 * x * x * x)))


def _rows8(x):
    t, c = x.shape
    return x.reshape(t // 8, 8, c).sum(axis=0)


def _full(shape):
    return pl.BlockSpec(shape, lambda *_: (0,) * len(shape))


def _resident(shape):
    return pl.BlockSpec(shape, lambda *_: (0,) * len(shape), pipeline_mode=pl.Buffered(1))


def _ffn_up(x, g, wg, wu, tm, tn, tag, ride=()):
    L = x.shape[0]

    def body(x_ref, g_ref, wg_ref, wu_ref, h_ref, dadg_ref, dadu_ref, a_ref):
        h = _rms(x_ref[...], g_ref[...]).astype(bf16)
        h_ref[...] = h
        for j in range(FF_PAD // tn):
            cols = slice(j * tn, (j + 1) * tn)
            gate = _dot(h, wg_ref[cols, :], NT)
            up = _dot(h, wu_ref[cols, :], NT)
            sig = _sigmoid(gate)
            silu = gate * sig
            dadg_ref[:, cols] = (up * (sig + silu * (1.0 - sig))).astype(bf16)
            dadu_ref[:, cols] = silu.astype(bf16)
            a_ref[:, cols] = (silu * up).astype(bf16)

    row = pl.BlockSpec((tm, D_MODEL), lambda i: (i, 0))
    wide = pl.BlockSpec((tm, FF_PAD), lambda i: (i, 0))
    return _pallas(
        body, ride=ride, name=tag + "_up", grid=(L // tm,),
        in_specs=[row, _full((1, D_MODEL)), _resident((FF_PAD, D_MODEL)), _resident((FF_PAD, D_MODEL))],
        out_specs=[row, wide, wide, wide],
        out_shape=[S((L, D_MODEL), bf16)] + [S((L, FF_PAD), bf16)] * 3,
        compiler_params=_cp("parallel"),
    )(x, g, wg, wu)


def _ffn_down(x, a, wd, tm, tag):
    L = x.shape[0]

    def body(x_ref, a_ref, wd_ref, o_ref):
        o_ref[...] = x_ref[...] + 0.5 * _dot(a_ref[...], wd_ref[...])

    return pl.pallas_call(
        body, name=tag + "_down", grid=(L // tm,),
        in_specs=[pl.BlockSpec((tm, D_MODEL), lambda i: (i, 0)), pl.BlockSpec((tm, FF_PAD), lambda i: (i, 0)),
                  _resident((FF_PAD, D_MODEL))],
        out_specs=pl.BlockSpec((tm, D_MODEL), lambda i: (i, 0)),
        out_shape=S((L, D_MODEL), f32),
        compiler_params=_cp("parallel"),
    )(x, a, wd)


def _ffn_down_loss(x, a, wd, target, g, tm, tag):
    L = x.shape[0]

    def body(x_ref, a_ref, wd_ref, t_ref, g_ref, dx_ref, dg_ref, l_ref):
        @pl.when(pl.program_id(0) == 0)
        def _():
            dg_ref[...] = jnp.zeros_like(dg_ref)
            l_ref[...] = jnp.zeros_like(l_ref)

        xo = x_ref[...] + 0.5 * _dot(a_ref[...], wd_ref[...])
        g = g_ref[...]
        e = _rms(xo, g) - t_ref[...]
        l_ref[...] += _rows8(e * e) * (0.5 / D_MODEL)
        dx, dg = _rms_bwd(xo, g, e * (1.0 / D_MODEL))
        dx_ref[...] = dx
        dg_ref[...] += dg

    row = pl.BlockSpec((tm, D_MODEL), lambda i: (i, 0))
    return pl.pallas_call(
        body, name=tag + "_down_loss", grid=(L // tm,),
        in_specs=[row, pl.BlockSpec((tm, FF_PAD), lambda i: (i, 0)), _resident((FF_PAD, D_MODEL)), row, _full((1, D_MODEL))],
        out_specs=[row, _full((1, D_MODEL)), _full((8, D_MODEL))],
        out_shape=[S((L, D_MODEL), f32), S((1, D_MODEL), f32), S((8, D_MODEL), f32)],
        compiler_params=_cp("arbitrary"),
    )(x, a, wd, target, g)


def _ffn_bwd_act(dxo, wd, dadg, dadu, tm, tn, tag, ride=()):
    L = dxo.shape[0]

    def body(dx_ref, wd_ref, dadg_ref, dadu_ref, dgate_ref, dup_ref, dxh_ref):
        dxh = (0.5 * dx_ref[...]).astype(bf16)
        dxh_ref[...] = dxh
        for j in range(FF_PAD // tn):
            cols = slice(j * tn, (j + 1) * tn)
            da = _dot(dxh, wd_ref[cols, :], NT)
            dgate_ref[:, cols] = (da * dadg_ref[:, cols].astype(f32)).astype(bf16)
            dup_ref[:, cols] = (da * dadu_ref[:, cols].astype(f32)).astype(bf16)

    row = pl.BlockSpec((tm, D_MODEL), lambda i: (i, 0))
    wide = pl.BlockSpec((tm, FF_PAD), lambda i: (i, 0))
    return _pallas(
        body, ride=ride, name=tag + "_bwd_act", grid=(L // tm,),
        in_specs=[row, _resident((FF_PAD, D_MODEL)), wide, wide],
        out_specs=[wide, wide, row],
        out_shape=[S((L, FF_PAD), bf16), S((L, FF_PAD), bf16), S((L, D_MODEL), bf16)],
        compiler_params=_cp("parallel"),
    )(dxo, wd, dadg, dadu)


def _ffn_bwd_in(dxo, x, g, dgate, dup, wg, wu, tm, tag, ride=()):
    L = x.shape[0]

    def body(dxo_ref, x_ref, g_ref, dgate_ref, dup_ref, wg_ref, wu_ref, dx_ref, dg_ref):
        @pl.when(pl.program_id(0) == 0)
        def _():
            dg_ref[...] = jnp.zeros_like(dg_ref)

        dh = _dot(dgate_ref[...], wg_ref[...]) + _dot(dup_ref[...], wu_ref[...])
        dx, dg = _rms_bwd(x_ref[...], g_ref[...], dh)
        dx_ref[...] = dxo_ref[...] + dx
        dg_ref[...] += dg

    row = pl.BlockSpec((tm, D_MODEL), lambda i: (i, 0))
    wide = pl.BlockSpec((tm, FF_PAD), lambda i: (i, 0))
    return _pallas(
        body, ride=ride, name=tag + "_bwd_in", grid=(L // tm,),
        in_specs=[row, row, _full((1, D_MODEL)), wide, wide, _resident((FF_PAD, D_MODEL)), _resident((FF_PAD, D_MODEL))],
        out_specs=[row, _full((1, D_MODEL))],
        out_shape=[S((L, D_MODEL), f32), S((1, D_MODEL), f32)],
        compiler_params=_cp("arbitrary"),
    )(dxo, x, g, dgate, dup, wg, wu)


def _mm_tn(a, b, out_dtype, name, tm=512, tn=1024, ride=()):
    L, M = a.shape
    N = b.shape[1]
    tm, tn = min(tm, M), min(tn, N)
    while N % tn:
        tn //= 2

    def body(a_ref, b_ref, o_ref):
        o_ref[...] = _dot(a_ref[...].astype(bf16), b_ref[...].astype(bf16), TN).astype(out_dtype)

    return _pallas(
        body, ride=ride, name=name, grid=(M // tm, N // tn),
        in_specs=[pl.BlockSpec((L, tm), lambda i, j: (0, i)), pl.BlockSpec((L, tn), lambda i, j: (0, j))],
        out_specs=pl.BlockSpec((tm, tn), lambda i, j: (i, j)),
        out_shape=S((M, N), out_dtype),
        compiler_params=_cp("parallel", "parallel"),
    )(a, b)


def _mix_in(x, g, w_in, tm):
    L = x.shape[0]

    def body(x_ref, g_ref, w_ref, h_ref, us_ref, v_ref):
        h = _rms(x_ref[...], g_ref[...]).astype(bf16)
        h_ref[...] = h
        u = _dot(h, w_ref[...], NT)
        us_ref[...] = u[:, :S5_WIDTH]
        v_ref[...] = u[:, S5_WIDTH:]

    row = lambda c: pl.BlockSpec((tm, c), lambda i: (i, 0))
    return pl.pallas_call(
        body, name="mix_in", grid=(L // tm,),
        in_specs=[row(D_MODEL), _full((1, D_MODEL)), _full((IN_COLS, D_MODEL))],
        out_specs=[row(D_MODEL), row(S5_WIDTH), row(2 * CONV_WIDTH)],
        out_shape=[S((L, D_MODEL), bf16), S((L, S5_WIDTH), f32), S((L, 2 * CONV_WIDTH), f32)],
        compiler_params=_cp("parallel"),
    )(x, g, w_in)


def _mix_in_bwd(dxo, x, g, du_s5, dv, w_in, tm):
    L = x.shape[0]

    def body(dxo_ref, x_ref, g_ref, dus_ref, dv_ref, w_ref, dx_ref, dg_ref, dub_ref):
        @pl.when(pl.program_id(0) == 0)
        def _():
            dg_ref[...] = jnp.zeros_like(dg_ref)

        dus = dus_ref[...].astype(bf16)
        dvb = dv_ref[...].astype(bf16)
        dub_ref[:, :S5_WIDTH] = dus
        dub_ref[:, S5_WIDTH:] = dvb
        dh = _dot(dus, w_ref[:S5_WIDTH, :]) + _dot(dvb, w_ref[S5_WIDTH:, :])
        dx, dg = _rms_bwd(x_ref[...], g_ref[...], dh)
        dx_ref[...] = dxo_ref[...] + dx
        dg_ref[...] += dg

    row = lambda c: pl.BlockSpec((tm, c), lambda i: (i, 0))
    return pl.pallas_call(
        body, name="mix_in_bwd", grid=(L // tm,),
        in_specs=[row(D_MODEL), row(D_MODEL), _full((1, D_MODEL)), row(S5_WIDTH), row(2 * CONV_WIDTH),
                  _full((IN_COLS, D_MODEL))],
        out_specs=[row(D_MODEL), _full((1, D_MODEL)), row(IN_COLS)],
        out_shape=[S((L, D_MODEL), f32), S((1, D_MODEL), f32), S((L, IN_COLS), bf16)],
        compiler_params=_cp("arbitrary"),
    )(dxo, x, g, du_s5, dv, w_in)


def _mix_out(x, y_s5, y_conv, w_out, tm):
    L = x.shape[0]

    def body(x_ref, ys_ref, yc_ref, w_ref, o_ref):
        o_ref[...] = x_ref[...] + _dot(ys_ref[...], w_ref[:S5_WIDTH, :]) + _dot(yc_ref[...], w_ref[S5_WIDTH:, :])

    row = lambda c: pl.BlockSpec((tm, c), lambda i: (i, 0))
    return pl.pallas_call(
        body, name="mix_out", grid=(L // tm,),
        in_specs=[row(D_MODEL), row(S5_WIDTH), row(CONV_WIDTH), _full((D_MODEL, D_MODEL))],
        out_specs=row(D_MODEL), out_shape=S((L, D_MODEL), f32),
        compiler_params=_cp("parallel"),
    )(x, y_s5, y_conv, w_out)


def _mix_out_bwd(dx, w_out, tm, ride=()):
    L = dx.shape[0]

    def body(dx_ref, w_ref, dys_ref, dyc_ref, dxb_ref):
        dxb = dx_ref[...].astype(bf16)
        dxb_ref[...] = dxb
        dys_ref[...] = _dot(dxb, w_ref[:S5_WIDTH, :], NT)
        dyc_ref[...] = _dot(dxb, w_ref[S5_WIDTH:, :], NT)

    row = lambda c: pl.BlockSpec((tm, c), lambda i: (i, 0))
    return _pallas(
        body, ride=ride, name="mix_out_bwd", grid=(L // tm,),
        in_specs=[row(D_MODEL), _full((D_MODEL, D_MODEL))],
        out_specs=[row(S5_WIDTH), row(CONV_WIDTH), row(D_MODEL)],
        out_shape=[S((L, S5_WIDTH), f32), S((L, CONV_WIDTH), f32), S((L, D_MODEL), bf16)],
        compiler_params=_cp("parallel"),
    )(dx, w_out)


def _s5_discretise(lam_re, lam_im, log_dt, b_re, b_im):
    dt = jnp.exp(log_dt)
    mag = jnp.exp(lam_re * dt)
    abar_re = mag * jnp.cos(lam_im * dt)
    abar_im = mag * jnp.sin(lam_im * dt)
    den = lam_re * lam_re + lam_im * lam_im
    num_re = abar_re - 1.0
    f_re = ((num_re * lam_re + abar_im * lam_im) / den)[:, None, :]
    f_im = ((abar_im * lam_re - num_re * lam_im) / den)[:, None, :]
    return abar_re, abar_im, f_re * b_re - f_im * b_im, f_re * b_im + f_im * b_re


def _s5_params(lam_re, lam_im, log_dt, b_re, b_im):
    def body(lr, li, ld, br, bi, ar_ref, ai_ref, bbr_ref, bbi_ref):
        ar, ai, bbr, bbi = _s5_discretise(lr[...], li[...], ld[...], br[...], bi[...])
        ar_ref[...], ai_ref[...], bbr_ref[...], bbi_ref[...] = ar, ai, bbr, bbi

    gp = S((S5_GROUPS, S5_STATE), f32)
    gcp = S((S5_GROUPS, S5_GROUP_CH, S5_STATE), f32)
    return pl.pallas_call(body, name="s5_params", out_shape=[gp, gp, gcp, gcp])(lam_re, lam_im, log_dt, b_re, b_im)


def _s5_params_bwd(lam_re, lam_im, log_dt, b_re, b_im, d_ar, d_ai, d_bbr, d_bbi):
    def body(lr, li, ld, br, bi, car, cai, cbr, cbi, o_lr, o_li, o_ld, o_br, o_bi):
        _, vjp = jax.vjp(_s5_discretise, lr[...], li[...], ld[...], br[...], bi[...])
        o_lr[...], o_li[...], o_ld[...], o_br[...], o_bi[...] = vjp((car[...], cai[...], cbr[...], cbi[...]))

    gp = S((S5_GROUPS, S5_STATE), f32)
    gcp = S((S5_GROUPS, S5_GROUP_CH, S5_STATE), f32)
    return pl.pallas_call(body, name="s5_params_bwd", out_shape=[gp, gp, S((S5_GROUPS, 1), f32), gcp, gcp])(
        lam_re, lam_im, log_dt, b_re, b_im, d_ar, d_ai, d_bbr, d_bbi)


def _cmul(ar, ai, br, bi):
    return ar * br - ai * bi, ar * bi + ai * br


def _segment_starts(er, ei, ar, ai, steps, reverse):
    pr, pi = ar, ai
    n = 1
    while n < steps:
        pr, pi = _cmul(pr, pi, pr, pi)
        n *= 2
    assert n == steps
    row = lax.broadcasted_iota(jnp.int32, (SEGMENTS, SCAN_LANES), 0)
    hr = jnp.zeros((1, SCAN_LANES), f32)
    hi = jnp.zeros((1, SCAN_LANES), f32)
    out_r = jnp.zeros((SEGMENTS, SCAN_LANES), f32)
    out_i = jnp.zeros((SEGMENTS, SCAN_LANES), f32)
    order = range(SEGMENTS - 1, 0, -1) if reverse else range(0, SEGMENTS - 1)
    for r in order:
        qr, qi = _cmul(pr, pi, hr, hi)
        hr, hi = qr + er[r:r + 1, :], qi + ei[r:r + 1, :]
        nxt = r - 1 if reverse else r + 1
        out_r = jnp.where(row == nxt, hr, out_r)
        out_i = jnp.where(row == nxt, hi, out_i)
    return out_r, out_i


def _s5_read_bwd(dout, y_lin, u, d_skip, w_glu, b_glu, tm):
    L = u.shape[0]

    def body(do_ref, yl_ref, u_ref, d_ref, w_ref, b_ref, dyl_ref, du_ref, dd_ref, dw_ref, db_ref):
        @pl.when(pl.program_id(0) == 0)
        def _():
            dd_ref[...] = jnp.zeros_like(dd_ref)
            dw_ref[...] = jnp.zeros_like(dw_ref)
            db_ref[...] = jnp.zeros_like(db_ref)

        u, d, dout = u_ref[...], d_ref[...], do_ref[...]
        y, gelu_vjp = jax.vjp(_gelu, yl_ref[...] + d * u)
        yb = y.astype(bf16)
        sig = _sigmoid(_dot(yb, w_ref[...]) + b_ref[...])
        dz = dout * y * sig * (1.0 - sig)
        dzb = dz.astype(bf16)
        dy = dout * sig + _dot(dzb, w_ref[...], NT)
        (dyp,) = gelu_vjp(dy)
        dyl_ref[...] = dyp.astype(bf16)
        du_ref[...] = d * dyp
        dd_ref[...] += _rows8(dyp * u)
        db_ref[...] += _rows8(dz)
        dw_ref[...] += _dot(yb, dzb, TN)

    row = pl.BlockSpec((tm, S5_WIDTH), lambda i: (i, 0))
    vec = _full((1, S5_WIDTH))
    part = _full((8, S5_WIDTH))
    return pl.pallas_call(
        body, name="s5_read_bwd", grid=(L // tm,),
        in_specs=[row, row, row, vec, _full((S5_WIDTH, S5_WIDTH)), vec],
        out_specs=[row, row, part, _full((S5_WIDTH, S5_WIDTH)), part],
        out_shape=[S((L, S5_WIDTH), bf16), S((L, S5_WIDTH), f32), S((8, S5_WIDTH), f32),
                   S((S5_WIDTH, S5_WIDTH), f32), S((8, S5_WIDTH), f32)],
        compiler_params=_cp("arbitrary"),
    )(dout, y_lin, u, d_skip, w_glu, b_glu)


S5_CHUNK_CH = SCAN_LANES // S5_STATE * S5_GROUP_CH


def _s5_specs(L, bi, reverse):
    nb = L // (bi * SEGMENTS)
    blk = (lambda c, j: (nb - 1 - j, c)) if reverse else (lambda c, j: (j, c))
    chan = pl.BlockSpec((bi * SEGMENTS, S5_CHUNK_CH), blk)
    state = pl.BlockSpec((bi * SEGMENTS, SCAN_LANES), blk)
    mat = pl.BlockSpec((S5_CHUNK_CH, SCAN_LANES), lambda c, j: (c, c))
    vec = pl.BlockSpec((1, SCAN_LANES), lambda c, j: (0, c))
    tile = pl.BlockSpec((SEGMENTS, SCAN_LANES), lambda c, j: (0, c))
    return nb, chan, state, mat, vec, tile


def _drive_into(src, m_re, m_im, negate_im, dr_ref, di_ref, bi):
    d_im = _dot(src, m_im)
    dr_ref[...] = _dot(src, m_re).reshape(bi, SEGMENTS, SCAN_LANES)
    di_ref[...] = (-d_im if negate_im else d_im).reshape(bi, SEGMENTS, SCAN_LANES)


def _s5_ends(src, a_re, a_im, m_re, m_im, reverse, bi, name, ride=()):
    L = src.shape[0]
    nb, chan, _, mat, vec, tile = _s5_specs(L, bi, reverse)

    def body(src_ref, ar_ref, ai_ref, mr_ref, mi_ref, er_ref, ei_ref, dr_ref, di_ref):
        @pl.when(pl.program_id(1) == 0)
        def _():
            er_ref[...] = jnp.zeros_like(er_ref)
            ei_ref[...] = jnp.zeros_like(ei_ref)

        _drive_into(src_ref[...].astype(bf16), mr_ref[...], mi_ref[...], reverse, dr_ref, di_ref, bi)
        ar = jnp.broadcast_to(ar_ref[...], (SEGMENTS, SCAN_LANES))
        ai = jnp.broadcast_to(ai_ref[...], (SEGMENTS, SCAN_LANES))

        def step(n, c):
            i = (bi - 1 - n) if reverse else n
            pr, pi = _cmul(ar, ai, c[0], c[1])
            return pr + dr_ref[i], pi + di_ref[i]

        er_ref[...], ei_ref[...] = lax.fori_loop(0, bi, step, (er_ref[...], ei_ref[...]), unroll=4)

    out = S((SEGMENTS, S5_LANES), f32)
    return _pallas(
        body, ride=ride, name=name, grid=(S5_LANES // SCAN_LANES, nb),
        in_specs=[chan, vec, vec, mat, mat], out_specs=[tile, tile], out_shape=[out, out],
        scratch_shapes=[pltpu.VMEM((bi, SEGMENTS, SCAN_LANES), f32)] * 2,
        compiler_params=_cp("parallel", "arbitrary"),
    )(src, a_re, a_im, m_re, m_im)


def _s5_states(u, a_re, a_im, bb_re, bb_im, cc_re, cc_im, e_re, e_im, bi, ride=()):
    L = u.shape[0]
    nb, chan, state, mat, vec, tile = _s5_specs(L, bi, False)
    rows = bi * SEGMENTS

    def body(u_ref, ar_ref, ai_ref, br_ref, bi_ref, cr_ref, ci_ref, er_ref, ei_ref, sr_ref, si_ref, yl_ref,
             hr_ref, hi_ref, dr_ref, di_ref):
        @pl.when(pl.program_id(1) == 0)
        def _():
            hr_ref[...], hi_ref[...] = _segment_starts(er_ref[...], ei_ref[...], ar_ref[...], ai_ref[...], L // SEGMENTS, False)

        _drive_into(u_ref[...].astype(bf16), br_ref[...], bi_ref[...], False, dr_ref, di_ref, bi)
        ar = jnp.broadcast_to(ar_ref[...], (SEGMENTS, SCAN_LANES))
        ai = jnp.broadcast_to(ai_ref[...], (SEGMENTS, SCAN_LANES))

        def step(i, c):
            pr, pi = _cmul(ar, ai, c[0], c[1])
            nr, nim = pr + dr_ref[i], pi + di_ref[i]
            dr_ref[i] = nr
            di_ref[i] = nim
            return nr, nim

        hr_ref[...], hi_ref[...] = lax.fori_loop(0, bi, step, (hr_ref[...], hi_ref[...]), unroll=4)
        sr = dr_ref[...].reshape(rows, SCAN_LANES).astype(bf16)
        si = di_ref[...].reshape(rows, SCAN_LANES).astype(bf16)
        sr_ref[...] = sr
        si_ref[...] = si
        yl_ref[...] = _dot(sr, cr_ref[...], NT) - _dot(si, ci_ref[...], NT)

    return _pallas(
        body, ride=ride, name="s5_states", grid=(S5_LANES // SCAN_LANES, nb),
        in_specs=[chan, vec, vec, mat, mat, mat, mat, tile, tile],
        out_specs=[state, state, chan],
        out_shape=[S((L, S5_LANES), bf16)] * 2 + [S((L, S5_WIDTH), f32)],
        scratch_shapes=[pltpu.VMEM((SEGMENTS, SCAN_LANES), f32)] * 2 + [pltpu.VMEM((bi, SEGMENTS, SCAN_LANES), f32)] * 2,
        compiler_params=_cp("parallel", "arbitrary"),
    )(u, a_re, a_im, bb_re, bb_im, cc_re, cc_im, e_re, e_im)


def _s5_states_bwd(dy, u, du_skip, s_re, s_im, a_re, a_im, bb_re, bb_im, cc_re, cc_im, e_re, e_im, bi, ride=()):
    L = u.shape[0]
    nb, chan, state, mat, vec, tile = _s5_specs(L, bi, True)
    rows = bi * SEGMENTS
    per = rows // 16

    def body(dy_ref, u_ref, dus_ref, sr_ref, si_ref, pr_ref, pi_ref, lr_ref, li_ref, ar_ref, ai_ref, br_ref, bi_ref, cr_ref,
             ci_ref, er_ref, ei_ref, du_ref, dar_ref, dai_ref, dbr_ref, dbi_ref, dcr_ref, dci_ref,
             hr_ref, hi_ref, gr_ref, gi_ref, fr_ref, fi_ref):
        j = pl.program_id(1)

        @pl.when(j == 0)
        def _():
            hr_ref[...], hi_ref[...] = _segment_starts(er_ref[...], ei_ref[...], ar_ref[...], ai_ref[...], L // SEGMENTS, True)
            for ref in (dar_ref, dai_ref, dbr_ref, dbi_ref, dcr_ref, dci_ref):
                ref[...] = jnp.zeros_like(ref)

        dy = dy_ref[...]
        sr, si = sr_ref[...], si_ref[...]
        _drive_into(dy, cr_ref[...], ci_ref[...], True, gr_ref, gi_ref, bi)
        fr_ref[...] = sr.astype(f32).reshape(bi, SEGMENTS, SCAN_LANES)
        fi_ref[...] = si.astype(f32).reshape(bi, SEGMENTS, SCAN_LANES)
        ar = jnp.broadcast_to(ar_ref[...], (SEGMENTS, SCAN_LANES))
        ai = jnp.broadcast_to(ai_ref[...], (SEGMENTS, SCAN_LANES))

        def step(n, c):
            i = bi - 1 - n
            gr, gi, accr, acci = c
            qr, qi = _cmul(ar, ai, gr, gi)
            gr, gi = qr + gr_ref[i], qi + gi_ref[i]
            gr_ref[i] = gr
            gi_ref[i] = gi
            pr, pi = fr_ref[i - 1], fi_ref[i - 1]
            return gr, gi, accr + (gr * pr + gi * pi), acci + (gi * pr - gr * pi)

        gr, gi, accr, acci = lax.fori_loop(0, bi - 1, step, (hr_ref[...], hi_ref[...], dar_ref[...], dai_ref[...]), unroll=3)
        qr, qi = _cmul(ar, ai, gr, gi)
        gr, gi = qr + gr_ref[0], qi + gi_ref[0]
        gr_ref[0] = gr
        gi_ref[0] = gi
        hr_ref[...], hi_ref[...] = gr, gi
        row = lax.broadcasted_iota(jnp.int32, (SEGMENTS, SCAN_LANES), 0)
        first = j == nb - 1
        older = lambda ref: ref[...].astype(f32)[SEGMENTS:, :]
        wrap_r = jnp.where(row == 0, 0.0, pltpu.roll(older(lr_ref), 1, 0))
        wrap_i = jnp.where(row == 0, 0.0, pltpu.roll(older(li_ref), 1, 0))
        pr = jnp.where(first, wrap_r, older(pr_ref))
        pi = jnp.where(first, wrap_i, older(pi_ref))
        dar_ref[...] = accr + gr * pr + gi * pi
        dai_ref[...] = acci + gi * pr - gr * pi

        g_re = gr_ref[...].reshape(rows, SCAN_LANES).astype(bf16)
        g_im = gi_ref[...].reshape(rows, SCAN_LANES).astype(bf16)
        ub = u_ref[...].astype(bf16)
        du_ref[...] = dus_ref[...] + _dot(g_re, br_ref[...], NT) + _dot(g_im, bi_ref[...], NT)
        dbr_ref[...] += _dot(ub, g_re, TN)
        dbi_ref[...] += _dot(ub, g_im, TN)
        dcr_ref[...] += _dot(dy, sr, TN)
        dci_ref[...] -= _dot(dy, si, TN)

    prev = pl.BlockSpec((16, SCAN_LANES), lambda c, j: (jnp.maximum((nb - 1 - j) * per - 1, 0), c))
    last = pl.BlockSpec((16, SCAN_LANES), lambda c, j: (L // 16 - 1, c))
    grad = pl.BlockSpec((S5_CHUNK_CH, SCAN_LANES), lambda c, j: (c, 0))
    big = pltpu.VMEM((bi, SEGMENTS, SCAN_LANES), f32)
    return _pallas(
        body, ride=ride, name="s5_states_bwd", grid=(S5_LANES // SCAN_LANES, nb),
        in_specs=[chan, chan, chan, state, state, prev, prev, last, last, vec, vec, mat, mat, mat, mat, tile, tile],
        out_specs=[chan, tile, tile, grad, grad, grad, grad],
        out_shape=[S((L, S5_WIDTH), f32)] + [S((SEGMENTS, S5_LANES), f32)] * 2 + [S((S5_WIDTH, SCAN_LANES), f32)] * 4,
        scratch_shapes=[pltpu.VMEM((SEGMENTS, SCAN_LANES), f32)] * 2 + [big] * 4,
        compiler_params=_cp("parallel", "arbitrary"),
    )(dy, u, du_skip, s_re, s_im, s_re, s_im, s_re, s_im, a_re, a_im, bb_re, bb_im, cc_re, cc_im, e_re, e_im)


def _s5_two_phase(L, bi):
    rows = bi * SEGMENTS
    nb = L // rows
    whole = pltpu.VMEM((L // SEGMENTS, SEGMENTS, SCAN_LANES), f32)
    mat = pl.BlockSpec((S5_CHUNK_CH, SCAN_LANES), lambda c, j: (c, c))
    vec = pl.BlockSpec((1, SCAN_LANES), lambda c, j: (0, c))
    tile = pl.BlockSpec((SEGMENTS, SCAN_LANES), lambda c, j: (0, c))
    return rows, nb, whole, mat, vec, tile


def _s5_forward(u, a_re, a_im, bb_re, bb_im, cc_re, cc_im, bi, ride=()):
    L = u.shape[0]
    rows, nb, whole, mat, vec, _ = _s5_two_phase(L, bi)

    def body(u_ref, ar_ref, ai_ref, br_ref, bi_ref, cr_ref, ci_ref, sr_ref, si_ref, yl_ref, hr_ref, hi_ref, dr_ref, di_ref):
        j = pl.program_id(1)
        ar = jnp.broadcast_to(ar_ref[...], (SEGMENTS, SCAN_LANES))
        ai = jnp.broadcast_to(ai_ref[...], (SEGMENTS, SCAN_LANES))

        @pl.when(j == 0)
        def _():
            hr_ref[...] = jnp.zeros_like(hr_ref)
            hi_ref[...] = jnp.zeros_like(hi_ref)

        @pl.when(j < nb)
        def _():
            base = j * bi
            ub = u_ref[...].astype(bf16)
            dr_ref[pl.ds(base, bi)] = _dot(ub, br_ref[...]).reshape(bi, SEGMENTS, SCAN_LANES)
            di_ref[pl.ds(base, bi)] = _dot(ub, bi_ref[...]).reshape(bi, SEGMENTS, SCAN_LANES)

            def step(i, c):
                pr, pi = _cmul(ar, ai, c[0], c[1])
                return pr + dr_ref[base + i], pi + di_ref[base + i]

            hr_ref[...], hi_ref[...] = lax.fori_loop(0, bi, step, (hr_ref[...], hi_ref[...]), unroll=4)

        @pl.when(j == nb - 1)
        def _():
            hr_ref[...], hi_ref[...] = _segment_starts(hr_ref[...], hi_ref[...], ar_ref[...], ai_ref[...], L // SEGMENTS, False)

        @pl.when(j >= nb)
        def _():
            base = (j - nb) * bi

            def step(i, c):
                pr, pi = _cmul(ar, ai, c[0], c[1])
                nr, nim = pr + dr_ref[base + i], pi + di_ref[base + i]
                dr_ref[base + i] = nr
                di_ref[base + i] = nim
                return nr, nim

            hr_ref[...], hi_ref[...] = lax.fori_loop(0, bi, step, (hr_ref[...], hi_ref[...]), unroll=4)
            sr = dr_ref[pl.ds(base, bi)].reshape(rows, SCAN_LANES).astype(bf16)
            si = di_ref[pl.ds(base, bi)].reshape(rows, SCAN_LANES).astype(bf16)
            sr_ref[...] = sr
            si_ref[...] = si
            yl_ref[...] = _dot(sr, cr_ref[...], NT) - _dot(si, ci_ref[...], NT)

    u_spec = pl.BlockSpec((rows, S5_CHUNK_CH), lambda c, j: (jnp.minimum(j, nb - 1), c))
    late = lambda width: pl.BlockSpec((rows, width), lambda c, j: (jnp.maximum(j - nb, 0), c))
    return _pallas(
        body, ride=ride, name="s5_forward", grid=(S5_LANES // SCAN_LANES, 2 * nb),
        in_specs=[u_spec, vec, vec, mat, mat, mat, mat],
        out_specs=[late(SCAN_LANES), late(SCAN_LANES), late(S5_CHUNK_CH)],
        out_shape=[S((L, S5_LANES), bf16)] * 2 + [S((L, S5_WIDTH), f32)],
        scratch_shapes=[pltpu.VMEM((SEGMENTS, SCAN_LANES), f32)] * 2 + [whole] * 2,
        compiler_params=_cp("parallel", "arbitrary"),
    )(u, a_re, a_im, bb_re, bb_im, cc_re, cc_im)


def _s5_backward(dy, u, du_skip, s_re, s_im, a_re, a_im, bb_re, bb_im, cc_re, cc_im, bi, ride=()):
    L = u.shape[0]
    rows, nb, whole, mat, vec, tile = _s5_two_phase(L, bi)
    per = rows // 16

    def body(dy_ref, u_ref, dus_ref, sr_ref, si_ref, pr_ref, pi_ref, lr_ref, li_ref, ar_ref, ai_ref, br_ref, bi_ref, cr_ref,
             ci_ref, du_ref, dar_ref, dai_ref, dbr_ref, dbi_ref, dcr_ref, dci_ref, hr_ref, hi_ref, gr_ref, gi_ref, fr_ref, fi_ref):
        j = pl.program_id(1)
        ar = jnp.broadcast_to(ar_ref[...], (SEGMENTS, SCAN_LANES))
        ai = jnp.broadcast_to(ai_ref[...], (SEGMENTS, SCAN_LANES))

        @pl.when(j == 0)
        def _():
            for ref in (hr_ref, hi_ref, dar_ref, dai_ref, dbr_ref, dbi_ref, dcr_ref, dci_ref):
                ref[...] = jnp.zeros_like(ref)

        @pl.when(j < nb)
        def _():
            base = (nb - 1 - j) * bi
            dy = dy_ref[...]
            gr_ref[pl.ds(base, bi)] = _dot(dy, cr_ref[...]).reshape(bi, SEGMENTS, SCAN_LANES)
            gi_ref[pl.ds(base, bi)] = (-_dot(dy, ci_ref[...])).reshape(bi, SEGMENTS, SCAN_LANES)

            def step(n, c):
                i = base + bi - 1 - n
                qr, qi = _cmul(ar, ai, c[0], c[1])
                return qr + gr_ref[i], qi + gi_ref[i]

            hr_ref[...], hi_ref[...] = lax.fori_loop(0, bi, step, (hr_ref[...], hi_ref[...]), unroll=4)

        @pl.when(j == nb - 1)
        def _():
            hr_ref[...], hi_ref[...] = _segment_starts(hr_ref[...], hi_ref[...], ar_ref[...], ai_ref[...], L // SEGMENTS, True)

        @pl.when(j >= nb)
        def _():
            blk = 2 * nb - 1 - j
            base = blk * bi
            sr, si = sr_ref[...], si_ref[...]
            fr_ref[...] = sr.astype(f32).reshape(bi, SEGMENTS, SCAN_LANES)
            fi_ref[...] = si.astype(f32).reshape(bi, SEGMENTS, SCAN_LANES)

            def step(n, c):
                i = bi - 1 - n
                gr, gi, accr, acci = c
                qr, qi = _cmul(ar, ai, gr, gi)
                gr, gi = qr + gr_ref[base + i], qi + gi_ref[base + i]
                gr_ref[base + i] = gr
                gi_ref[base + i] = gi
                pr, pi = fr_ref[i - 1], fi_ref[i - 1]
                return gr, gi, accr + (gr * pr + gi * pi), acci + (gi * pr - gr * pi)

            gr, gi, accr, acci = lax.fori_loop(0, bi - 1, step, (hr_ref[...], hi_ref[...], dar_ref[...], dai_ref[...]), unroll=3)
            qr, qi = _cmul(ar, ai, gr, gi)
            gr, gi = qr + gr_ref[base], qi + gi_ref[base]
            gr_ref[base] = gr
            gi_ref[base] = gi
            hr_ref[...], hi_ref[...] = gr, gi
            row = lax.broadcasted_iota(jnp.int32, (SEGMENTS, SCAN_LANES), 0)
            older = lambda ref: ref[...].astype(f32)[SEGMENTS:, :]
            wrap_r = jnp.where(row == 0, 0.0, pltpu.roll(older(lr_ref), 1, 0))
            wrap_i = jnp.where(row == 0, 0.0, pltpu.roll(older(li_ref), 1, 0))
            pr = jnp.where(blk == 0, wrap_r, older(pr_ref))
            pi = jnp.where(blk == 0, wrap_i, older(pi_ref))
            dar_ref[...] = accr + gr * pr + gi * pi
            dai_ref[...] = acci + gi * pr - gr * pi

            g_re = gr_ref[pl.ds(base, bi)].reshape(rows, SCAN_LANES).astype(bf16)
            g_im = gi_ref[pl.ds(base, bi)].reshape(rows, SCAN_LANES).astype(bf16)
            ub = u_ref[...].astype(bf16)
            dy = dy_ref[...]
            du_ref[...] = dus_ref[...] + _dot(g_re, br_ref[...], NT) + _dot(g_im, bi_ref[...], NT)
            dbr_ref[...] += _dot(ub, g_re, TN)
            dbi_ref[...] += _dot(ub, g_im, TN)
            dcr_ref[...] += _dot(dy, sr, TN)
            dci_ref[...] -= _dot(dy, si, TN)

    block = lambda c, j: jnp.where(j < nb, nb - 1 - j, 2 * nb - 1 - j)
    late_block = lambda c, j: jnp.minimum(2 * nb - 1 - j, nb - 1)
    both = pl.BlockSpec((rows, S5_CHUNK_CH), lambda c, j: (block(c, j), c))
    chan = pl.BlockSpec((rows, S5_CHUNK_CH), lambda c, j: (late_block(c, j), c))
    state = pl.BlockSpec((rows, SCAN_LANES), lambda c, j: (late_block(c, j), c))
    prev = pl.BlockSpec((16, SCAN_LANES), lambda c, j: (jnp.maximum(late_block(c, j) * per - 1, 0), c))
    last = pl.BlockSpec((16, SCAN_LANES), lambda c, j: (L // 16 - 1, c))
    grad = pl.BlockSpec((S5_CHUNK_CH, SCAN_LANES), lambda c, j: (c, 0))
    return _pallas(
        body, ride=ride, name="s5_backward", grid=(S5_LANES // SCAN_LANES, 2 * nb),
        in_specs=[both, chan, chan, state, state, prev, prev, last, last, vec, vec, mat, mat, mat, mat],
        out_specs=[chan, tile, tile, grad, grad, grad, grad],
        out_shape=[S((L, S5_WIDTH), f32)] + [S((SEGMENTS, S5_LANES), f32)] * 2 + [S((S5_WIDTH, SCAN_LANES), f32)] * 4,
        scratch_shapes=[pltpu.VMEM((SEGMENTS, SCAN_LANES), f32)] * 2 + [whole] * 2 + [pltpu.VMEM((bi, SEGMENTS, SCAN_LANES), f32)] * 2,
        compiler_params=_cp("parallel", "arbitrary"),
    )(dy, u, du_skip, s_re, s_im, s_re, s_im, s_re, s_im, a_re, a_im, bb_re, bb_im, cc_re, cc_im)


def _s5_gate(y_lin, u, d_skip, w_glu, b_glu, tm, ride=()):
    L = u.shape[0]

    def body(yl_ref, u_ref, d_ref, w_ref, b_ref, o_ref):
        y = _gelu(yl_ref[...] + d_ref[...] * u_ref[...])
        z = _dot(y.astype(bf16), w_ref[...]) + b_ref[...]
        o_ref[...] = (y * _sigmoid(z)).astype(bf16)

    row = pl.BlockSpec((tm, S5_WIDTH), lambda i: (i, 0))
    vec = _full((1, S5_WIDTH))
    return _pallas(
        body, ride=ride, name="s5_gate", grid=(L // tm,),
        in_specs=[row, row, vec, _full((S5_WIDTH, S5_WIDTH)), vec],
        out_specs=row, out_shape=S((L, S5_WIDTH), bf16),
        compiler_params=_cp("parallel"),
    )(y_lin, u, d_skip, w_glu, b_glu)


def _group_mean(x, avg):
    hi = x.astype(bf16)
    lo = (x - hi.astype(f32)).astype(bf16)
    return _dot(hi, avg) + _dot(lo, avg)


def _conv_act(zn, ln_g, ln_b):
    t = zn * ln_g + ln_b
    return t * _sigmoid(t)


def _glu_padded(v_ref, halo_ref, zpad_ref, tm):
    v = v_ref[...]
    vh = halo_ref[...]
    zh = vh[:, :CONV_WIDTH] * _sigmoid(vh[:, CONV_WIDTH:])
    zpad_ref[:CONV_HALO, :] = jnp.where(pl.program_id(0) > 0, zh, 0.0)
    zpad_ref[CONV_HALO:CONV_HALO + tm, :] = v[:, :CONV_WIDTH] * _sigmoid(v[:, CONV_WIDTH:])
    zpad_ref[CONV_HALO + tm:, :] = jnp.zeros((8, CONV_WIDTH), f32)


def _shifted(pad_ref, sh_ref, tm):
    for b in range(8):
        sh_ref[b] = pad_ref[pl.ds(b, tm + CONV_HALO), :]


def _window(sh_ref, r0, off, rows):
    return sh_ref[off % 8, pl.ds(pl.multiple_of(r0 + 8 * (off // 8), 8), rows), :]


def _tap_sum(w_ref, sh_ref, taps, out_ref, tm, bias):
    def chunk(c, carry):
        r0 = pl.multiple_of(c * CONV_ROWS, CONV_ROWS)
        acc = jnp.zeros((CONV_ROWS, CONV_WIDTH), f32) + bias
        for k, off in taps:
            acc = acc + w_ref[k:k + 1, :] * _window(sh_ref, r0, off, CONV_ROWS)
        out_ref[pl.ds(r0, CONV_ROWS), :] = acc
        return carry

    lax.fori_loop(0, tm // CONV_ROWS, chunk, 0)


FWD_TAPS = [(k, CONV_HALO - (CONV_K - 1) + k) for k in range(CONV_K)]
BWD_TAPS = [(k, CONV_K - 1 - k) for k in range(CONV_K)]


def _conv_specs(tm):
    per = tm // CONV_HALO
    vrow = pl.BlockSpec((tm, 2 * CONV_WIDTH), lambda i: (i, 0))
    vhalo = pl.BlockSpec((CONV_HALO, 2 * CONV_WIDTH), lambda i: (jnp.maximum(i * per - 1, 0), 0))
    return vrow, vhalo


def _conv_scratch(tm):
    return [pltpu.VMEM((tm + CONV_HALO + 8, CONV_WIDTH), f32), pltpu.VMEM((8, tm + CONV_HALO, CONV_WIDTH), f32)]


def _conv_fwd(v, w_dw, b_dw, ln_g, ln_b, avg, tm, ride=()):
    L = v.shape[0]

    def body(v_ref, halo_ref, w_ref, b_ref, g_ref, bb_ref, avg_ref, o_ref, zc_ref, zpad_ref, zs_ref):
        _glu_padded(v_ref, halo_ref, zpad_ref, tm)
        _shifted(zpad_ref, zs_ref, tm)
        _tap_sum(w_ref, zs_ref, FWD_TAPS, zc_ref, tm, b_ref[...])
        zc = zc_ref[...]
        xc = zc - _group_mean(zc, avg_ref[...])
        zn = xc * lax.rsqrt(_group_mean(xc * xc, avg_ref[...]) + EPS)
        o_ref[...] = _conv_act(zn, g_ref[...], bb_ref[...]).astype(bf16)

    vrow, vhalo = _conv_specs(tm)
    vec = _full((1, CONV_WIDTH))
    row = pl.BlockSpec((tm, CONV_WIDTH), lambda i: (i, 0))
    return _pallas(
        body, ride=ride, name="conv_fwd", grid=(L // tm,),
        in_specs=[vrow, vhalo, _full((CONV_HALO, CONV_WIDTH)), vec, vec, vec, _full((CONV_WIDTH, CONV_WIDTH))],
        out_specs=[row, row], out_shape=[S((L, CONV_WIDTH), bf16), S((L, CONV_WIDTH), f32)],
        scratch_shapes=_conv_scratch(tm),
        compiler_params=_cp("arbitrary"),
    )(v, v, w_dw, b_dw, ln_g, ln_b, avg)


def _conv_bwd_norm(dout, zc, ln_g, ln_b, avg, tm):
    L = zc.shape[0]

    def body(do_ref, zc_ref, g_ref, bb_ref, avg_ref, dzc_ref, dg_ref, db_ref, dbd_ref):
        @pl.when(pl.program_id(0) == 0)
        def _():
            dg_ref[...] = jnp.zeros_like(dg_ref)
            db_ref[...] = jnp.zeros_like(db_ref)
            dbd_ref[...] = jnp.zeros_like(dbd_ref)

        avg = avg_ref[...]
        zc = zc_ref[...]
        xc = zc - _group_mean(zc, avg)
        rstd = lax.rsqrt(_group_mean(xc * xc, avg) + EPS)
        xhat = xc * rstd
        _, act_vjp = jax.vjp(_conv_act, xhat, g_ref[...], bb_ref[...])
        dxhat, dg, db = act_vjp(do_ref[...])
        dzc = rstd * (dxhat - _group_mean(dxhat, avg) - xhat * _group_mean(dxhat * xhat, avg))
        dzc_ref[...] = dzc
        dg_ref[0:1, :] += dg
        db_ref[0:1, :] += db
        dbd_ref[...] += _rows8(dzc)

    vec = _full((1, CONV_WIDTH))
    row = pl.BlockSpec((tm, CONV_WIDTH), lambda i: (i, 0))
    part = _full((8, CONV_WIDTH))
    return pl.pallas_call(
        body, name="conv_bwd_norm", grid=(L // tm,),
        in_specs=[row, row, vec, vec, _full((CONV_WIDTH, CONV_WIDTH))],
        out_specs=[row, part, part, part],
        out_shape=[S((L, CONV_WIDTH), f32)] + [S((8, CONV_WIDTH), f32)] * 3,
        compiler_params=_cp("arbitrary"),
    )(dout, zc, ln_g, ln_b, avg)


def _conv_bwd_taps(dzc, v, w_dw, tm, ride=()):
    L = v.shape[0]
    nt = L // tm
    per = tm // CONV_HALO

    def body(d_ref, dn_ref, v_ref, halo_ref, w_ref, dv_ref, dw_ref, zpad_ref, zs_ref, dpad_ref, ds_ref, dz_ref):
        i = pl.program_id(0)

        @pl.when(i == 0)
        def _():
            dw_ref[...] = jnp.zeros_like(dw_ref)

        _glu_padded(v_ref, halo_ref, zpad_ref, tm)
        _shifted(zpad_ref, zs_ref, tm)
        dpad_ref[:tm, :] = d_ref[...]
        dpad_ref[tm:tm + CONV_HALO, :] = jnp.where(i < nt - 1, dn_ref[...], 0.0)
        dpad_ref[tm + CONV_HALO:, :] = jnp.zeros((8, CONV_WIDTH), f32)
        _shifted(dpad_ref, ds_ref, tm)
        _tap_sum(w_ref, ds_ref, BWD_TAPS, dz_ref, tm, 0.0)

        for first in range(0, CONV_K, 8):
            taps = FWD_TAPS[first:first + 8]

            def chunk(c, accs, taps=taps):
                r0 = pl.multiple_of(c * 8, 8)
                d = d_ref[pl.ds(r0, 8), :]
                return tuple(acc + d * _window(zs_ref, r0, off, 8) for acc, (_, off) in zip(accs, taps))

            accs = lax.fori_loop(0, tm // 8, chunk, tuple(jnp.zeros((8, CONV_WIDTH), f32) for _ in taps), unroll=2)
            for acc, (k, _) in zip(accs, taps):
                dw_ref[k] += acc

        dz = dz_ref[...]
        v = v_ref[...]
        sig = _sigmoid(v[:, CONV_WIDTH:])
        dv_ref[:, :CONV_WIDTH] = dz * sig
        dv_ref[:, CONV_WIDTH:] = dz * v[:, :CONV_WIDTH] * sig * (1.0 - sig)

    vrow, vhalo = _conv_specs(tm)
    row = pl.BlockSpec((tm, CONV_WIDTH), lambda i: (i, 0))
    nxt = pl.BlockSpec((CONV_HALO, CONV_WIDTH), lambda i: (jnp.minimum((i + 1) * per, nt * per - 1), 0))
    return _pallas(
        body, ride=ride, name="conv_bwd_taps", grid=(nt,),
        in_specs=[row, nxt, vrow, vhalo, _full((CONV_HALO, CONV_WIDTH))],
        out_specs=[vrow, _full((CONV_HALO, 8, CONV_WIDTH))],
        out_shape=[S((L, 2 * CONV_WIDTH), f32), S((CONV_HALO, 8, CONV_WIDTH), f32)],
        scratch_shapes=_conv_scratch(tm) * 2 + [pltpu.VMEM((tm, CONV_WIDTH), f32)],
        compiler_params=_cp("arbitrary"),
    )(dzc, dzc, v, v, w_dw)


def _to_segments(a):
    L, c = a.shape
    return a.reshape(SEGMENTS, L // SEGMENTS, c).transpose(1, 0, 2).reshape(L, c)


def _from_segments(a):
    L, c = a.shape
    return a.reshape(L // SEGMENTS, SEGMENTS, c).transpose(1, 0, 2).reshape(L, c)


def _block_diag(ms):
    n = len(ms)

    def body(*refs):
        for a in range(n):
            out = refs[n + a]
            out[...] = jnp.zeros_like(out)
            for g in range(S5_GROUPS):
                rows = slice(g * S5_GROUP_CH, (g + 1) * S5_GROUP_CH)
                out[rows, g * S5_STATE:(g + 1) * S5_STATE] = refs[a][rows, :].astype(bf16)

    return pl.pallas_call(body, name="s5_block_diag", out_shape=[S((S5_WIDTH, S5_LANES), bf16)] * n,
                          compiler_params=pltpu.CompilerParams(vmem_limit_bytes=VMEM_LIMIT))(
        *[m.reshape(S5_WIDTH, S5_STATE) for m in ms])


def _diag_blocks(ms):
    n = len(ms)
    per_chunk = SCAN_LANES // S5_STATE

    def body(*refs):
        for a in range(n):
            for g in range(S5_GROUPS):
                rows = slice(g * S5_GROUP_CH, (g + 1) * S5_GROUP_CH)
                at = g % per_chunk * S5_STATE
                refs[n + a][rows, :] = refs[a][rows, at:at + S5_STATE]

    out = pl.pallas_call(body, name="s5_diag_blocks", out_shape=[S((S5_WIDTH, S5_STATE), f32)] * n,
                         compiler_params=pltpu.CompilerParams(vmem_limit_bytes=VMEM_LIMIT))(*ms)
    return [o.reshape(S5_GROUPS, S5_GROUP_CH, S5_STATE) for o in out]


class _NoExchanges:
    def before(self, point):
        return ()

    def after(self, point):
        pass

    def alone(self, point):
        pass


def _ffn_block(x, p, tag, tm, sched, head=None):
    point = tag + "_up"
    h, dadg, dadu, a = _ffn_up(x, p[tag + "_norm"], p[tag + "_w_gate"], p[tag + "_w_up"], tm, 768, tag, ride=sched.before(point))
    sched.after(point)
    if head is None:
        out = _ffn_down(x, a, p[tag + "_w_down"], tm, tag)
    else:
        out = _ffn_down_loss(x, a, p[tag + "_w_down"], *head, tm, tag)
    return out, (h, dadg, dadu, a)


def _ffn_block_bwd(dxo, x, p, tag, saved, tm, grads, sched):
    h, dadg, dadu, a = saved
    dgate, dup, dxh = _ffn_bwd_act(dxo, p[tag + "_w_down"], dadg, dadu, tm, 768, tag, ride=sched.before(tag + "_bwd_act"))
    sched.after(tag + "_bwd_act")
    for which, lhs, rhs in (("gate", dgate, h), ("up", dup, h), ("down", a, dxh)):
        point = tag + "_dw_" + which
        grads[tag + "_w_" + which] = _mm_tn(lhs, rhs, bf16, point, ride=sched.before(point))
        sched.after(point)
    dx, grads[tag + "_norm"] = _ffn_bwd_in(dxo, x, p[tag + "_norm"], dgate, dup, p[tag + "_w_gate"], p[tag + "_w_up"], tm, tag,
                                           ride=sched.before(tag + "_bwd_in"))
    sched.after(tag + "_bwd_in")
    return dx


def _local_step(x, target, p, grads, sched):
    L = x.shape[0]
    tm = min(512, L)
    ts = min(256, L)
    ni = L // SEGMENTS
    bi = min(64, ni)

    def carried(point, fn, *args):
        out = fn(*args, ride=sched.before(point))
        sched.after(point)
        return out

    x1, saved1 = _ffn_block(x, p, "ffn1", tm, sched)

    h2, u_s5, v = _mix_in(x1, p["mix_norm"], p["w_in"], tm)
    s5_in = (p["s5_lam_re"], p["s5_lam_im"], p["s5_log_dt"].reshape(S5_GROUPS, 1), p["s5_b_re"], p["s5_b_im"])
    abar_re, abar_im, bbar_re, bbar_im = _s5_params(*s5_in)
    a_re, a_im = abar_re.reshape(1, S5_LANES), abar_im.reshape(1, S5_LANES)
    bb_re, bb_im, cc_re, cc_im = _block_diag([bbar_re, bbar_im, p["s5_c_re"], p["s5_c_im"]])
    u_seg = _to_segments(u_s5)
    s_re, s_im, y_lin = carried("s5_forward", _s5_forward, u_seg, a_re, a_im, bb_re, bb_im, cc_re, cc_im, bi)
    y_s5 = _from_segments(_s5_gate(y_lin, u_seg, p["s5_d"], p["s5_w_glu"], p["s5_b_glu"], tm))
    w_dw = jnp.pad(p["conv_w_dw"], ((0, CONV_HALO - CONV_K), (0, 0)))
    heads = jnp.arange(CONV_WIDTH) // CONV_HEAD
    avg = ((heads[:, None] == heads[None, :]).astype(f32) / CONV_HEAD).astype(bf16)
    y_conv, zc = carried("conv_fwd", _conv_fwd, v, w_dw, p["conv_b_dw"], p["conv_ln_g"], p["conv_ln_b"], avg, tm)
    x2 = _mix_out(x1, y_s5, y_conv, p["w_out"], tm)

    (dx3, grads["final_norm"], loss_terms), saved2 = _ffn_block(
        x2, p, "ffn2", tm, sched, head=(target, p["final_norm"].reshape(1, D_MODEL)))

    dx2 = _ffn_block_bwd(dx3, x2, p, "ffn2", saved2, tm, grads, sched)

    dy_s5, dy_conv, dx2b = carried("mix_out_bwd", _mix_out_bwd, dx2, p["w_out"], tm)
    grads["w_out"] = jnp.concatenate([_mm_tn(y_s5, dx2b, bf16, "dw_out_s5"), _mm_tn(y_conv, dx2b, bf16, "dw_out_conv")], axis=0)
    dy_lin, du_skip, dd8, grads["s5_w_glu"], dbg8 = _s5_read_bwd(
        _to_segments(dy_s5), y_lin, u_seg, p["s5_d"], p["s5_w_glu"], p["s5_b_glu"], tm)
    grads["s5_d"] = dd8.sum(axis=0, keepdims=True)
    grads["s5_b_glu"] = dbg8.sum(axis=0, keepdims=True)
    du_seg, da_re8, da_im8, dbb_re, dbb_im, dcc_re, dcc_im = carried(
        "s5_backward", _s5_backward, dy_lin, u_seg, du_skip, s_re, s_im, a_re, -a_im, bb_re, bb_im, cc_re, cc_im, bi)
    d_abar = lambda a8: a8.sum(axis=0).reshape(S5_GROUPS, S5_STATE)
    grads["s5_c_re"], grads["s5_c_im"], d_bbr, d_bbi = _diag_blocks([dcc_re, dcc_im, dbb_re, dbb_im])
    d_lr, d_li, d_ld, d_br, d_bi = _s5_params_bwd(*s5_in, d_abar(da_re8), d_abar(da_im8), d_bbr, d_bbi)
    grads["s5_lam_re"], grads["s5_lam_im"], grads["s5_log_dt"] = d_lr, d_li, d_ld.reshape(1, S5_GROUPS)
    grads["s5_b_re"], grads["s5_b_im"] = d_br, d_bi
    dzc, dlg8, dlb8, dbd8 = _conv_bwd_norm(dy_conv, zc, p["conv_ln_g"], p["conv_ln_b"], avg, tm)
    grads["conv_ln_g"] = dlg8.sum(axis=0, keepdims=True)
    grads["conv_ln_b"] = dlb8.sum(axis=0, keepdims=True)
    grads["conv_b_dw"] = dbd8.sum(axis=0, keepdims=True)
    dv, dw8 = carried("conv_bwd_taps", _conv_bwd_taps, dzc, v, w_dw, tm)
    grads["conv_w_dw"] = dw8.sum(axis=1)[:CONV_K]
    dx1, grads["mix_norm"], dub = _mix_in_bwd(dx2, x1, p["mix_norm"], _from_segments(du_seg), dv, p["w_in"], tm)
    grads["w_in"] = _mm_tn(dub, h2, bf16, "dw_in")

    grads["loss_terms"] = loss_terms
    dx0 = _ffn_block_bwd(dx1, x, p, "ffn1", saved1, tm, grads, sched)
    sched.alone("tail")
    return loss_terms, dx0


MESH = pl.DeviceIdType.MESH
ANY = pl.BlockSpec(memory_space=pl.ANY)


def _place():
    return lax.axis_index("x"), lax.axis_index("y"), lax.axis_index("c")


class _Exchange:
    def __init__(self, ins, out_shape, sems, start, finish):
        self.ins, self.out_shape, self.sems, self.start, self.finish = list(ins), list(out_shape), list(sems), start, finish
        self.out = None


def _pallas(body, *, ride=(), **kw):
    if not ride:
        return pl.pallas_call(body, **kw)

    def run(*args):
        out_shape = kw.get("out_shape", [])
        single = not isinstance(out_shape, (list, tuple))
        shapes = [out_shape] if single else list(out_shape)
        out_specs = [kw["out_specs"]] if single else list(kw.get("out_specs", []))
        grid = tuple(kw.get("grid", ()))
        scratch = list(kw.get("scratch_shapes", ()))
        n_in, n_out, n_scr = len(args), len(shapes), len(scratch)
        r_in = [len(e.ins) for e in ride]
        r_out = [len(e.out_shape) for e in ride]
        r_sem = [len(e.sems) for e in ride]

        def wrapped(*refs):
            own_in, refs = refs[:n_in], refs[n_in:]
            ex_in, refs = refs[:sum(r_in)], refs[sum(r_in):]
            own_out, refs = refs[:n_out], refs[n_out:]
            ex_out, refs = refs[:sum(r_out)], refs[sum(r_out):]
            own_scr, ex_sem = refs[:n_scr], refs[n_scr:]
            parts = []
            for e, ni, no, ns in zip(ride, r_in, r_out, r_sem):
                parts.append((e, ex_in[:ni], ex_out[:no], ex_sem[:ns]))
                ex_in, ex_out, ex_sem = ex_in[ni:], ex_out[no:], ex_sem[ns:]

            def at(step):
                def go():
                    for e, i, o, s in parts:
                        getattr(e, step)(i, o, s)
                if grid:
                    ids = [pl.program_id(d) for d in range(len(grid))]
                    when = [i == (0 if step == "start" else g - 1) for i, g in zip(ids, grid)]
                    pl.when(functools.reduce(lambda a, b: a & b, when))(go)
                else:
                    go()

            at("start")
            if body is not None:
                body(*own_in, *own_out, *own_scr)
            at("finish")

        outs = pl.pallas_call(
            wrapped, name=kw["name"], grid=grid,
            in_specs=list(kw.get("in_specs", [])) + [ANY] * sum(r_in),
            out_specs=out_specs + [ANY] * sum(r_out),
            out_shape=shapes + [s for e in ride for s in e.out_shape],
            scratch_shapes=scratch + [s for e in ride for s in e.sems],
            compiler_params=_cp(*["arbitrary"] * len(grid)),
        )(*args, *[a for e in ride for a in e.ins])
        own, rest = outs[:n_out], outs[n_out:]
        for e, no in zip(ride, r_out):
            e.out, rest = list(rest[:no]), rest[no:]
        return own[0] if single else own

    return run


def _exchange(ride, name):
    _pallas(None, ride=ride, name=name)()


def _gather(arrs):
    n = len(arrs)

    def copies(ins, outs, sems):
        send_sems, recv_sems, local_sems = sems
        x, y, c = _place()
        me, sibling = (x, y, c), (x, y, 1 - c)
        chips = [(1 - x, y), (x, 1 - y), (1 - x, 1 - y)]

        def place(a, block):
            return outs[a].at[block]

        def copy(a, k, block, to, src=None):
            px, py, pc = block
            dst = place(a, 4 * px + 2 * py + pc)
            return pltpu.make_async_remote_copy(
                src_ref=dst if src is None else src, dst_ref=dst, send_sem=send_sems.at[7 * a + k],
                recv_sem=recv_sems.at[7 * a + k], device_id=to, device_id_type=MESH)

        def own():
            local = [pltpu.make_async_copy(ins[a], place(a, 4 * x + 2 * y + c), local_sems.at[a]) for a in range(n)]
            remote = []
            for a in range(n):
                remote.append(copy(a, 0, me, sibling, src=ins[a]))
                remote += [copy(a, 1 + j, me, (*chip, c), src=ins[a]) for j, chip in enumerate(chips)]
            return local, remote

        return c, me, sibling, chips, copy, own

    def start(ins, outs, sems):
        local, remote = copies(ins, outs, sems)[-1]()
        for cp in local + remote:
            cp.start()

    def finish(ins, outs, sems):
        c, me, sibling, chips, copy, own = copies(ins, outs, sems)
        passed = []
        for j, chip in enumerate(chips):
            for a in range(n):
                copy(a, 1 + j, (*chip, c), me).wait_recv()
                passed.append(copy(a, 4 + j, (*chip, c), sibling))
                passed[-1].start()
        for a in range(n):
            copy(a, 0, sibling, me).wait_recv()
            for j, chip in enumerate(chips):
                copy(a, 4 + j, (*chip, 1 - c), me).wait_recv()
        local, remote = own()
        for cp in remote + passed:
            cp.wait_send()
        for cp in local:
            cp.wait()

    dma = pltpu.SemaphoreType.DMA
    shapes = [S((N_DEV, *a.shape), a.dtype) for a in arrs]
    return _Exchange(arrs, shapes, [dma((7 * n,)), dma((7 * n,)), dma((n,))], start, finish)


def _swap_with_sibling(gs):
    n = len(gs)

    def copies(ins, outs, sems):
        x, y, c = _place()
        return [pltpu.make_async_remote_copy(
            src_ref=ins[a].at[:, 1 - c], dst_ref=outs[a], send_sem=sems[0].at[a], recv_sem=sems[1].at[a],
            device_id=(x, y, 1 - c), device_id_type=MESH) for a in range(n)]

    def start(ins, outs, sems):
        for cp in copies(ins, outs, sems):
            cp.start()

    def finish(ins, outs, sems):
        for cp in copies(ins, outs, sems):
            cp.wait()

    dma = pltpu.SemaphoreType.DMA
    return _Exchange(gs, [S((N_CHIP, *g.shape[2:]), g.dtype) for g in gs], [dma((n,)), dma((n,))], start, finish)


def _send_to_owners(gs):
    n = len(gs)

    def copies(ins, outs, sems):
        x, y, c = _place()
        me = 4 * x + 2 * y + c
        flip = lambda v, bit: 1 - v if bit else v

        def copy(a, k, slot_from, slot_to):
            peer = (flip(x, k & 4), flip(y, k & 2), flip(c, k & 1))
            return pltpu.make_async_remote_copy(
                src_ref=ins[a].at[slot_from], dst_ref=outs[a].at[slot_to], send_sem=sems[0].at[7 * a + k - 1],
                recv_sem=sems[1].at[7 * a + k - 1], device_id=peer, device_id_type=MESH)

        his = lambda k: 4 * flip(x, k & 4) + 2 * flip(y, k & 2) + flip(c, k & 1)
        sends = lambda: [copy(a, k, his(k), me) for a in range(n) for k in range(1, N_DEV)]
        lands = lambda: [copy(a, k, me, his(k)) for a in range(n) for k in range(1, N_DEV)]
        local = lambda: [pltpu.make_async_copy(ins[a].at[me], outs[a].at[me], sems[2].at[a]) for a in range(n)]
        return sends, lands, local

    def start(ins, outs, sems):
        sends, _, local = copies(ins, outs, sems)
        for cp in local() + sends():
            cp.start()

    def finish(ins, outs, sems):
        sends, lands, local = copies(ins, outs, sems)
        for cp in lands():
            cp.wait_recv()
        for cp in sends():
            cp.wait_send()
        for cp in local():
            cp.wait()

    dma = pltpu.SemaphoreType.DMA
    return _Exchange(gs, [S(g.shape, g.dtype) for g in gs], [dma((7 * n,)), dma((7 * n,)), dma((n,))], start, finish)


def _swap_with_chips(ps):
    n = len(ps)

    def copies(ins, outs, sems):
        x, y, c = _place()
        q = 2 * x + y
        peers = [(x, 1 - y), (1 - x, y), (1 - x, 1 - y)]

        def copy(a, j, slot_from, slot_to):
            px, py = peers[j]
            return pltpu.make_async_remote_copy(
                src_ref=ins[a].at[slot_from], dst_ref=outs[a].at[slot_to], send_sem=sems[0].at[3 * a + j],
                recv_sem=sems[1].at[3 * a + j], device_id=(px, py, c), device_id_type=MESH)

        sends = lambda: [copy(a, j, 2 * peers[j][0] + peers[j][1], q) for a in range(n) for j in range(3)]
        lands = lambda: [copy(a, j, q, 2 * peers[j][0] + peers[j][1]) for a in range(n) for j in range(3)]
        return sends, lands

    def start(ins, outs, sems):
        for cp in copies(ins, outs, sems)[0]():
            cp.start()

    def finish(ins, outs, sems):
        sends, lands = copies(ins, outs, sems)
        for cp in lands():
            cp.wait_recv()
        for cp in sends():
            cp.wait_send()

    dma = pltpu.SemaphoreType.DMA
    return _Exchange(ps, [S(p.shape, p.dtype) for p in ps], [dma((3 * n,)), dma((3 * n,))], start, finish)


def _row_tile(rows, cols, itemsize):
    t = rows
    while t * cols * itemsize > (1 << 20) and t % 32 == 0:
        t //= 2
    return t


def _add_sibling(g4, st, core, name):
    _, R, C = st.shape
    tr = _row_tile(R, C, 4)

    def body(c_ref, g_ref, s_ref, o_ref):
        o_ref[...] = (g_ref[...].astype(f32) + s_ref[...].astype(f32)).astype(bf16)

    mine = pl.BlockSpec((None, None, tr, C), lambda q, i, c: (q, c[0], i, 0))
    return pl.pallas_call(
        body, name=name,
        grid_spec=pltpu.PrefetchScalarGridSpec(
            num_scalar_prefetch=1, grid=(N_CHIP, R // tr),
            in_specs=[mine,
                      pl.BlockSpec((None, tr, C), lambda q, i, c: (q, i, 0))],
            out_specs=pl.BlockSpec((None, tr, C), lambda q, i, c: (q, i, 0))),
        out_shape=S((N_CHIP, R, C), bf16),
        compiler_params=_cp("parallel", "parallel"),
    )(core, g4, st)


def _adamw(w, g, m, v):
    m = B1 * m + (1.0 - B1) * g
    v = B2 * v + (1.0 - B2) * (g * g)
    m_hat = m / (1.0 - B1 ** STEP)
    v_hat = v / (1.0 - B2 ** STEP)
    return -LR * (m_hat / (jnp.sqrt(v_hat) + ADAM_EPS) + WD * w), m, v


def _adam_sharded(w, m, v, part, got, slots, name):
    R, C = w.shape
    _, Rp, Cp = part.shape
    tr = _row_tile(R, Cp, 4) if Rp == R else R

    def body(s_ref, w_ref, m_ref, v_ref, p_ref, a_ref, b_ref, c_ref, g_out, d_out, m_out, v_out):
        g = p_ref[...].astype(f32) + a_ref[...].astype(f32) + b_ref[...].astype(f32) + c_ref[...].astype(f32)
        g = g[:, :C]
        g_out[...] = g
        d_out[...], m_out[...], v_out[...] = _adamw(w_ref[...], g, m_ref[...], v_ref[...])

    shard = pl.BlockSpec((tr, C), lambda i, s: (i, 0))
    slot = lambda k: pl.BlockSpec((None, tr, Cp), lambda i, s: (s[k], i, 0))
    return pl.pallas_call(
        body, name=name,
        grid_spec=pltpu.PrefetchScalarGridSpec(
            num_scalar_prefetch=1, grid=(R // tr,),
            in_specs=[shard, shard, shard, slot(0), slot(1), slot(2), slot(3)],
            out_specs=[shard] * 4),
        out_shape=[S((R, C), f32)] * 4,
        compiler_params=_cp("parallel"),
    )(slots, w, m, v, part, got, got, got)


def _adam_from_all(w, m, v, got, name):
    R, C = w.shape
    _, Rp, Cp = got.shape
    tr = _row_tile(R, N_DEV * Cp, 2) if Rp == R else R

    def body(w_ref, m_ref, v_ref, p_ref, g_out, d_out, m_out, v_out):
        g = p_ref[0].astype(f32)
        for d in range(1, N_DEV):
            g = g + p_ref[d].astype(f32)
        g = g[:, :C]
        g_out[...] = g
        d_out[...], m_out[...], v_out[...] = _adamw(w_ref[...], g, m_ref[...], v_ref[...])

    shard = pl.BlockSpec((tr, C), lambda i: (i, 0))
    return pl.pallas_call(
        body, name=name, grid=(R // tr,),
        in_specs=[shard, shard, shard, pl.BlockSpec((N_DEV, tr, Cp), lambda i: (0, i, 0))],
        out_specs=[shard] * 4, out_shape=[S((R, C), f32)] * 4,
        compiler_params=_cp("parallel"),
    )(w, m, v, got)


def _adam_replicated(items, loss_terms, name):
    n = len(items)
    has_loss = loss_terms is not None

    def total(ref):
        g = ref[0]
        for d in range(1, N_DEV):
            g = g + ref[d]
        return g

    def body(*refs):
        ins, outs = refs[:4 * n + has_loss], refs[4 * n + has_loss:]
        for i in range(n):
            w_ref, m_ref, v_ref, g_ref = ins[4 * i:4 * i + 4]
            g = total(g_ref)
            outs[4 * i][...] = g
            outs[4 * i + 1][...], outs[4 * i + 2][...], outs[4 * i + 3][...] = _adamw(w_ref[...], g, m_ref[...], v_ref[...])
        if has_loss:
            outs[-1][...] = jnp.sum(total(ins[-1]), keepdims=True)

    flat = [a for item in items for a in item] + ([loss_terms] if has_loss else [])
    shapes = [S(item[0].shape, f32) for item in items for _ in range(4)] + ([S((1, 1), f32)] if has_loss else [])
    out = pl.pallas_call(body, name=name, out_shape=shapes,
                         compiler_params=pltpu.CompilerParams(vmem_limit_bytes=VMEM_LIMIT))(*flat)
    return [out[4 * i:4 * i + 4] for i in range(n)], (out[-1] if has_loss else None)


WEIGHTS = ["ffn1_norm", "ffn1_w_gate", "ffn1_w_up", "ffn1_w_down", "mix_norm", "w_in", "s5_lam_re", "s5_lam_im", "s5_log_dt",
           "s5_b_re", "s5_b_im", "s5_c_re", "s5_c_im", "s5_d", "s5_w_glu", "s5_b_glu", "conv_w_dw", "conv_b_dw", "conv_ln_g",
           "conv_ln_b", "w_out", "ffn2_norm", "ffn2_w_gate", "ffn2_w_up", "ffn2_w_down", "final_norm"]
SHARDED = ["ffn1_w_gate", "ffn1_w_up", "ffn1_w_down", "w_in", "s5_w_glu", "conv_w_dw", "w_out", "ffn2_w_gate", "ffn2_w_up",
           "ffn2_w_down"]
REPLICATED = [n for n in WEIGHTS if n not in SHARDED]
TRANSPOSED = ["ffn1_w_gate", "ffn1_w_up", "ffn2_w_gate", "ffn2_w_up", "w_in"]


def _shard_to_wire(n, w):
    if n.startswith("ffn"):
        w = jnp.pad(w, ((0, FF_SHARD_PAD - FF_SHARD), (0, 0)))
    elif n == "conv_w_dw":
        return jnp.pad(w, ((0, CONV_HALO - CONV_K), (0, 0)))
    return w.astype(bf16)


def _to_wire(shards, ride):
    names = list(shards)
    shapes = [jax.eval_shape(functools.partial(_shard_to_wire, n), shards[n]) for n in names]

    def body(*refs):
        for src, dst in zip(refs[:len(names)], refs[len(names):]):
            (r, c), (rp, cp) = src.shape, dst.shape
            dst[:r, :c] = src[...].astype(dst.dtype)
            if cp > c:
                dst[:, c:] = jnp.zeros((rp, cp - c), dst.dtype)
            if rp > r:
                dst[r:, :] = jnp.zeros((rp - r, cp), dst.dtype)

    out = _pallas(body, ride=ride, name="to_wire", out_shape=shapes, in_specs=[pl.BlockSpec(memory_space=pltpu.VMEM)] * len(names),
                  out_specs=[pl.BlockSpec(memory_space=pltpu.VMEM)] * len(names))(*[shards[n] for n in names])
    return dict(zip(names, out))


def _gathered_to_full(n, g):
    if n == "conv_w_dw":
        return g.transpose(1, 0, 2).reshape(CONV_HALO, CONV_WIDTH)[:CONV_K]
    return g.reshape(N_DEV * g.shape[1], g.shape[2])


def _grad_to_blocks(n, g):
    if n == "conv_w_dw":
        g = jnp.pad(g, ((0, CONV_HALO - CONV_K), (0, 0)))
        g = g.reshape(g.shape[0], N_DEV, g.shape[1] // N_DEV).transpose(1, 0, 2)
    else:
        g = g.reshape(N_DEV, g.shape[0] // N_DEV, g.shape[1])
    return g.astype(bf16).reshape(N_CHIP, 2, *g.shape[1:])


REPLICATED_LATE = ["ffn1_norm"]
REPLICATED_EARLY = [n for n in REPLICATED if n not in REPLICATED_LATE]

PLAN = {
    "start": [("gather", ["ffn1_w_gate", "ffn1_w_up"])],
    "ffn1_up": [("gather", ["ffn1_w_down", "w_in", "w_out", "s5_w_glu", "conv_w_dw"])],
    "s5_forward": [("gather", ["ffn2_w_gate", "ffn2_w_up"])],
    "conv_fwd": [("gather", ["ffn2_w_down"])],
    "ffn2_dw_up": [("sibling", ["ffn2_w_gate"])],
    "ffn2_dw_down": [("sibling", ["ffn2_w_up"])],
    "mix_out_bwd": [("sibling", ["ffn2_w_down"])],
    "s5_backward": [("chips", ["ffn2_w_gate", "ffn2_w_up"])],
    "conv_bwd_taps": [("chips", ["ffn2_w_down"])],
    "ffn1_bwd_act": [("sibling", ["w_in", "s5_w_glu", "conv_w_dw", "w_out"]), ("replicated", REPLICATED_EARLY)],
    "ffn1_dw_gate": [("chips", ["w_in", "s5_w_glu", "conv_w_dw", "w_out"])],
    "ffn1_dw_up": [("sibling", ["ffn1_w_gate"])],
    "ffn1_dw_down": [("sibling", ["ffn1_w_up"]), ("chips", ["ffn1_w_gate"])],
    "ffn1_bwd_in": [("owners", ["ffn1_w_down"]), ("chips", ["ffn1_w_up"])],
    "tail": [("replicated", REPLICATED_LATE)],
}


class _Schedule:
    def __init__(self, wire, p, grads, core):
        self.wire, self.p, self.grads, self.core = wire, p, grads, core
        self.partial, self.reduced, self.everyone, self.pending = {}, {}, {}, []

    def before(self, point):
        assert not self.pending
        for kind, names in PLAN.get(point, ()):
            if kind == "gather":
                given = [self.wire[n] for n in names]
                ex = _gather(given)
            elif kind == "sibling":
                given = [_grad_to_blocks(n, self.grads[n]) for n in names]
                ex = _swap_with_sibling(given)
            elif kind == "chips":
                given = [self.partial.pop(n) for n in names]
                ex = _swap_with_chips(given)
            elif kind == "owners":
                given = [_grad_to_blocks(n, self.grads[n]).reshape(N_DEV, *self.wire[n].shape) for n in names]
                ex = _send_to_owners(given)
            else:
                names = names + ["loss_terms"] * (names is REPLICATED_EARLY)
                given = [self.grads[n].reshape(self.p[n].shape) if n in self.p else self.grads[n] for n in names]
                ex = _gather(given)
            self.pending.append((kind, names, given, ex))
        return [ex for _, _, _, ex in self.pending]

    def after(self, point):
        for kind, names, given, ex in self.pending:
            if kind == "gather":
                for n, g in zip(names, ex.out):
                    self.p[n] = _gathered_to_full(n, g)
            elif kind == "sibling":
                for n, blocks, got in zip(names, given, ex.out):
                    self.partial[n] = _add_sibling(blocks, got, self.core, "reduce_add_" + n)
            elif kind == "chips":
                for n, part, got in zip(names, given, ex.out):
                    self.reduced[n] = (part, got)
            elif kind == "owners":
                for n, got in zip(names, ex.out):
                    self.reduced[n] = (None, got)
            else:
                self.everyone.update(zip(names, ex.out))
        self.pending = []

    def alone(self, point):
        _exchange(self.before(point), point)
        self.after(point)


def kernel(x, ffn1_norm, ffn1_w_gate, ffn1_w_up, ffn1_w_down, mix_norm, w_in, s5_lam_re, s5_lam_im, s5_log_dt, s5_b_re, s5_b_im, s5_c_re, s5_c_im, s5_d, s5_w_glu, s5_b_glu, conv_w_dw, conv_b_dw, conv_ln_g, conv_ln_b, w_out, ffn2_norm, ffn2_w_gate, ffn2_w_up, ffn2_w_down, final_norm, loss_target, m_ffn1_norm, m_ffn1_w_gate, m_ffn1_w_up, m_ffn1_w_down, m_mix_norm, m_w_in, m_s5_lam_re, m_s5_lam_im, m_s5_log_dt, m_s5_b_re, m_s5_b_im, m_s5_c_re, m_s5_c_im, m_s5_d, m_s5_w_glu, m_s5_b_glu, m_conv_w_dw, m_conv_b_dw, m_conv_ln_g, m_conv_ln_b, m_w_out, m_ffn2_norm, m_ffn2_w_gate, m_ffn2_w_up, m_ffn2_w_down, m_final_norm, v_ffn1_norm, v_ffn1_w_gate, v_ffn1_w_up, v_ffn1_w_down, v_mix_norm, v_w_in, v_s5_lam_re, v_s5_lam_im, v_s5_log_dt, v_s5_b_re, v_s5_b_im, v_s5_c_re, v_s5_c_im, v_s5_d, v_s5_w_glu, v_s5_b_glu, v_conv_w_dw, v_conv_b_dw, v_conv_ln_g, v_conv_ln_b, v_w_out, v_ffn2_norm, v_ffn2_w_gate, v_ffn2_w_up, v_ffn2_w_down, v_final_norm):
    args = locals()
    w = {n: args[n] for n in WEIGHTS}
    m = {n: args["m_" + n] for n in WEIGHTS}
    v = {n: args["v_" + n] for n in WEIGHTS}
    xq, yq, cq = _place()
    q = 2 * xq + yq
    slots = jnp.stack([q, q ^ 1, q ^ 2, q ^ 3]).astype(jnp.int32)

    def shard2d(n, a):
        a = a.reshape(a.shape[-2:])
        return a.T if n in TRANSPOSED else a

    def view(n, a):
        if n.startswith("s5_b_") and a.ndim == 4:
            return a[0].transpose(0, 2, 1)
        return a[0] if a.ndim >= 3 else a.reshape(1, -1)

    def unview(n, a):
        return (a.transpose(0, 2, 1) if n.startswith("s5_b_") and a.ndim == 3 else a).reshape(w[n].shape)

    p = {n: view(n, w[n]) for n in REPLICATED}
    grads = {}
    first = PLAN["start"][0][1]
    wire = {n: _shard_to_wire(n, shard2d(n, w[n])) for n in first}
    sched = _Schedule(wire, p, grads, jnp.reshape(cq, (1,)).astype(jnp.int32))
    wire.update(_to_wire({n: shard2d(n, w[n]) for n in SHARDED if n not in first}, sched.before("start")))
    sched.after("start")
    _, dx = _local_step(x[0], loss_target[0], p, grads, sched)

    out = {}
    for n in SHARDED:
        part, got = sched.reduced[n]
        rows = got.shape[1] if n == "conv_w_dw" else shard2d(n, w[n]).shape[0]
        fit = lambda a: jnp.pad(shard2d(n, a), ((0, rows - shard2d(n, a).shape[0]), (0, 0)))
        if part is None:
            res = _adam_from_all(fit(w[n]), fit(m[n]), fit(v[n]), got, "adam_" + n)
        else:
            res = _adam_sharded(fit(w[n]), fit(m[n]), fit(v[n]), part, got, slots, "adam_" + n)
        back = lambda r: r[:shard2d(n, w[n]).shape[0]]
        out[n] = [(back(r).T if n in TRANSPOSED else back(r)).reshape(w[n].shape) for r in res]

    for names in (REPLICATED_EARLY, REPLICATED_LATE):
        items = [(view(n, w[n]), view(n, m[n]), view(n, v[n]), sched.everyone[n]) for n in names]
        res, total = _adam_replicated(items, sched.everyone.get("loss_terms") if names is REPLICATED_EARLY else None,
                                      "adam_" + names[0])
        for n, r in zip(names, res):
            out[n] = [unview(n, a) for a in r]
        if total is not None:
            loss = total.reshape(())

    return (loss, dx.reshape(x.shape), *[out[n][0] for n in WEIGHTS], *[out[n][1] for n in WEIGHTS],
            *[out[n][2] for n in WEIGHTS], *[out[n][3] for n in WEIGHTS])
```

```python
import functools

import jax
import jax.numpy as jnp
from jax import lax
from jax.experimental import pallas as pl
from jax.experimental.pallas import tpu as pltpu

f32 = jnp.float32
bf16 = jnp.bfloat16
S = jax.ShapeDtypeStruct

N_DEV = 8
N_CHIP = 4
D_MODEL = 1024
D_FF = 2816
FF_SHARD = D_FF // N_DEV
FF_SHARD_PAD = 384
FF_PAD = FF_SHARD_PAD * N_DEV
S5_WIDTH = 512
S5_GROUPS = 32
S5_GROUP_CH = 16
S5_STATE = 64
S5_LANES = S5_GROUPS * S5_STATE
CONV_WIDTH = 512
CONV_K = 31
CONV_HALO = 32
CONV_HEAD = 64
CONV_ROWS = 32
IN_COLS = S5_WIDTH + 2 * CONV_WIDTH
SEGMENTS = 8
SCAN_LANES = 512
EPS = 1e-6
LR, B1, B2, ADAM_EPS, WD, STEP = 0.001, 0.9, 0.999, 1e-08, 0.01, 10
VMEM_LIMIT = 56 * 1024 * 1024

NN = (((1,), (0,)), ((), ()))
NT = (((1,), (1,)), ((), ()))
TN = (((0,), (0,)), ((), ()))


def _dot(a, b, dims=NN):
    return lax.dot_general(a, b, dims, preferred_element_type=f32)


def _cp(*sem):
    return pltpu.CompilerParams(dimension_semantics=sem, vmem_limit_bytes=VMEM_LIMIT)


def _rms(x, g):
    return x * lax.rsqrt(jnp.mean(x * x, axis=-1, keepdims=True) + EPS) * g


def _rms_bwd(x, g, dh):
    _, vjp = jax.vjp(_rms, x, g)
    return vjp(dh)


def _sigmoid(x):
    return 1.0 / (1.0 + jnp.exp(-x))


def _gelu(x):
    return 0.5 * x * (1.0 + jnp.tanh(0.7978845608028654 * (x + 0.044715 * x * x * x)))


def _rows8(x):
    t, c = x.shape
    return x.reshape(t // 8, 8, c).sum(axis=0)


def _full(shape):
    return pl.BlockSpec(shape, lambda *_: (0,) * len(shape))


def _resident(shape):
    return pl.BlockSpec(shape, lambda *_: (0,) * len(shape), pipeline_mode=pl.Buffered(1))


def _ffn_up(x, g, wg, wu, tm, tn, tag, ride=()):
    L = x.shape[0]

    def body(x_ref, g_ref, wg_ref, wu_ref, h_ref, dadg_ref, dadu_ref, a_ref):
        h = _rms(x_ref[...], g_ref[...]).astype(bf16)
        h_ref[...] = h
        for j in range(FF_PAD // tn):
            cols = slice(j * tn, (j + 1) * tn)
            gate = _dot(h, wg_ref[cols, :], NT)
            up = _dot(h, wu_ref[cols, :], NT)
            sig = _sigmoid(gate)
            silu = gate * sig
            dadg_ref[:, cols] = (up * (sig + silu * (1.0 - sig))).astype(bf16)
            dadu_ref[:, cols] = silu.astype(bf16)
            a_ref[:, cols] = (silu * up).astype(bf16)

    row = pl.BlockSpec((tm, D_MODEL), lambda i: (i, 0))
    wide = pl.BlockSpec((tm, FF_PAD), lambda i: (i, 0))
    return _pallas(
        body, ride=ride, name=tag + "_up", grid=(L // tm,),
        in_specs=[row, _full((1, D_MODEL)), _resident((FF_PAD, D_MODEL)), _resident((FF_PAD, D_MODEL))],
        out_specs=[row, wide, wide, wide],
        out_shape=[S((L, D_MODEL), bf16)] + [S((L, FF_PAD), bf16)] * 3,
        compiler_params=_cp("parallel"),
    )(x, g, wg, wu)


def _ffn_down(x, a, wd, tm, tag, ride=()):
    L = x.shape[0]

    def body(x_ref, a_ref, wd_ref, o_ref):
        o_ref[...] = x_ref[...] + 0.5 * _dot(a_ref[...], wd_ref[...])

    return _pallas(
        body, ride=ride, name=tag + "_down", grid=(L // tm,),
        in_specs=[pl.BlockSpec((tm, D_MODEL), lambda i: (i, 0)), pl.BlockSpec((tm, FF_PAD), lambda i: (i, 0)),
                  _resident((FF_PAD, D_MODEL))],
        out_specs=pl.BlockSpec((tm, D_MODEL), lambda i: (i, 0)),
        out_shape=S((L, D_MODEL), f32),
        compiler_params=_cp("parallel"),
    )(x, a, wd)


def _ffn_down_loss(x, a, wd, target, g, tm, tag):
    L = x.shape[0]

    def body(x_ref, a_ref, wd_ref, t_ref, g_ref, dx_ref, dg_ref, l_ref):
        @pl.when(pl.program_id(0) == 0)
        def _():
            dg_ref[...] = jnp.zeros_like(dg_ref)
            l_ref[...] = jnp.zeros_like(l_ref)

        xo = x_ref[...] + 0.5 * _dot(a_ref[...], wd_ref[...])
        g = g_ref[...]
        e = _rms(xo, g) - t_ref[...]
        l_ref[...] += _rows8(e * e) * (0.5 / D_MODEL)
        dx, dg = _rms_bwd(xo, g, e * (1.0 / D_MODEL))
        dx_ref[...] = dx
        dg_ref[...] += dg

    row = pl.BlockSpec((tm, D_MODEL), lambda i: (i, 0))
    return pl.pallas_call(
        body, name=tag + "_down_loss", grid=(L // tm,),
        in_specs=[row, pl.BlockSpec((tm, FF_PAD), lambda i: (i, 0)), _resident((FF_PAD, D_MODEL)), row, _full((1, D_MODEL))],
        out_specs=[row, _full((1, D_MODEL)), _full((8, D_MODEL))],
        out_shape=[S((L, D_MODEL), f32), S((1, D_MODEL), f32), S((8, D_MODEL), f32)],
        compiler_params=_cp("arbitrary"),
    )(x, a, wd, target, g)


def _ffn_bwd_act(dxo, wd, dadg, dadu, tm, tn, tag, ride=()):
    L = dxo.shape[0]

    def body(dx_ref, wd_ref, dadg_ref, dadu_ref, dgate_ref, dup_ref, dxh_ref):
        dxh = (0.5 * dx_ref[...]).astype(bf16)
        dxh_ref[...] = dxh
        for j in range(FF_PAD // tn):
            cols = slice(j * tn, (j + 1) * tn)
            da = _dot(dxh, wd_ref[cols, :], NT)
            dgate_ref[:, cols] = (da * dadg_ref[:, cols].astype(f32)).astype(bf16)
            dup_ref[:, cols] = (da * dadu_ref[:, cols].astype(f32)).astype(bf16)

    row = pl.BlockSpec((tm, D_MODEL), lambda i: (i, 0))
    wide = pl.BlockSpec((tm, FF_PAD), lambda i: (i, 0))
    return _pallas(
        body, ride=ride, name=tag + "_bwd_act", grid=(L // tm,),
        in_specs=[row, _resident((FF_PAD, D_MODEL)), wide, wide],
        out_specs=[wide, wide, row],
        out_shape=[S((L, FF_PAD), bf16), S((L, FF_PAD), bf16), S((L, D_MODEL), bf16)],
        compiler_params=_cp("parallel"),
    )(dxo, wd, dadg, dadu)


def _ffn_bwd_in(dxo, x, g, dgate, dup, wg, wu, tm, name, tiles=None, into=None, ride=()):
    L = x.shape[0]
    first, count = tiles or (0, L // tm)

    def body(dxo_ref, x_ref, g_ref, dgate_ref, dup_ref, wg_ref, wu_ref, *rest):
        dx_ref, dg_ref = rest[-2:]

        @pl.when(pl.program_id(0) == 0)
        def _():
            dg_ref[...] = jnp.zeros_like(dg_ref)

        dh = _dot(dgate_ref[...], wg_ref[...]) + _dot(dup_ref[...], wu_ref[...])
        dx, dg = _rms_bwd(x_ref[...], g_ref[...], dh)
        dx_ref[...] = dxo_ref[...] + dx
        dg_ref[...] += dg

    row = pl.BlockSpec((tm, D_MODEL), lambda i: (first + i, 0))
    wide = pl.BlockSpec((tm, FF_PAD), lambda i: (first + i, 0))
    return _pallas(
        body, ride=ride, name=name, grid=(count,),
        in_specs=[row, row, _full((1, D_MODEL)), wide, wide, _resident((FF_PAD, D_MODEL)), _resident((FF_PAD, D_MODEL))]
        + [ANY] * (into is not None),
        out_specs=[row, _full((1, D_MODEL))],
        out_shape=[S((L, D_MODEL), f32), S((1, D_MODEL), f32)],
        input_output_aliases={7: 0} if into is not None else {},
        compiler_params=_cp("arbitrary"),
    )(dxo, x, g, dgate, dup, wg, wu, *([into] if into is not None else []))


def _mm_tn(a, b, out_dtype, name, tm=512, tn=1024, ride=()):
    L, M = a.shape
    N = b.shape[1]
    tm, tn = min(tm, M), min(tn, N)
    while N % tn:
        tn //= 2

    def body(a_ref, b_ref, o_ref):
        o_ref[...] = _dot(a_ref[...].astype(bf16), b_ref[...].astype(bf16), TN).astype(out_dtype)

    return _pallas(
        body, ride=ride, name=name, grid=(M // tm, N // tn),
        in_specs=[pl.BlockSpec((L, tm), lambda i, j: (0, i)), pl.BlockSpec((L, tn), lambda i, j: (0, j))],
        out_specs=pl.BlockSpec((tm, tn), lambda i, j: (i, j)),
        out_shape=S((M, N), out_dtype),
        compiler_params=_cp("parallel", "parallel"),
    )(a, b)


def _mix_in(x, g, w_in, tm):
    L = x.shape[0]

    def body(x_ref, g_ref, w_ref, h_ref, us_ref, v_ref):
        h = _rms(x_ref[...], g_ref[...]).astype(bf16)
        h_ref[...] = h
        u = _dot(h, w_ref[...], NT)
        us_ref[...] = u[:, :S5_WIDTH]
        v_ref[...] = u[:, S5_WIDTH:]

    row = lambda c: pl.BlockSpec((tm, c), lambda i: (i, 0))
    return pl.pallas_call(
        body, name="mix_in", grid=(L // tm,),
        in_specs=[row(D_MODEL), _full((1, D_MODEL)), _full((IN_COLS, D_MODEL))],
        out_specs=[row(D_MODEL), row(S5_WIDTH), row(2 * CONV_WIDTH)],
        out_shape=[S((L, D_MODEL), bf16), S((L, S5_WIDTH), f32), S((L, 2 * CONV_WIDTH), f32)],
        compiler_params=_cp("parallel"),
    )(x, g, w_in)


def _mix_in_bwd(dxo, x, g, du_s5, dv, w_in, tm):
    L = x.shape[0]

    def body(dxo_ref, x_ref, g_ref, dus_ref, dv_ref, w_ref, dx_ref, dg_ref, dub_ref):
        @pl.when(pl.program_id(0) == 0)
        def _():
            dg_ref[...] = jnp.zeros_like(dg_ref)

        dus = dus_ref[...].astype(bf16)
        dvb = dv_ref[...].astype(bf16)
        dub_ref[:, :S5_WIDTH] = dus
        dub_ref[:, S5_WIDTH:] = dvb
        dh = _dot(dus, w_ref[:S5_WIDTH, :]) + _dot(dvb, w_ref[S5_WIDTH:, :])
        dx, dg = _rms_bwd(x_ref[...], g_ref[...], dh)
        dx_ref[...] = dxo_ref[...] + dx
        dg_ref[...] += dg

    row = lambda c: pl.BlockSpec((tm, c), lambda i: (i, 0))
    return pl.pallas_call(
        body, name="mix_in_bwd", grid=(L // tm,),
        in_specs=[row(D_MODEL), row(D_MODEL), _full((1, D_MODEL)), row(S5_WIDTH), row(2 * CONV_WIDTH),
                  _full((IN_COLS, D_MODEL))],
        out_specs=[row(D_MODEL), _full((1, D_MODEL)), row(IN_COLS)],
        out_shape=[S((L, D_MODEL), f32), S((1, D_MODEL), f32), S((L, IN_COLS), bf16)],
        compiler_params=_cp("arbitrary"),
    )(dxo, x, g, du_s5, dv, w_in)


def _mix_out(x, y_s5, y_conv, w_out, tm):
    L = x.shape[0]

    def body(x_ref, ys_ref, yc_ref, w_ref, o_ref):
        o_ref[...] = x_ref[...] + _dot(ys_ref[...], w_ref[:S5_WIDTH, :]) + _dot(yc_ref[...], w_ref[S5_WIDTH:, :])

    row = lambda c: pl.BlockSpec((tm, c), lambda i: (i, 0))
    return pl.pallas_call(
        body, name="mix_out", grid=(L // tm,),
        in_specs=[row(D_MODEL), row(S5_WIDTH), row(CONV_WIDTH), _full((D_MODEL, D_MODEL))],
        out_specs=row(D_MODEL), out_shape=S((L, D_MODEL), f32),
        compiler_params=_cp("parallel"),
    )(x, y_s5, y_conv, w_out)


def _mix_out_bwd(dx, w_out, tm, ride=()):
    L = dx.shape[0]

    def body(dx_ref, w_ref, dys_ref, dyc_ref, dxb_ref):
        dxb = dx_ref[...].astype(bf16)
        dxb_ref[...] = dxb
        dys_ref[...] = _dot(dxb, w_ref[:S5_WIDTH, :], NT)
        dyc_ref[...] = _dot(dxb, w_ref[S5_WIDTH:, :], NT)

    row = lambda c: pl.BlockSpec((tm, c), lambda i: (i, 0))
    return _pallas(
        body, ride=ride, name="mix_out_bwd", grid=(L // tm,),
        in_specs=[row(D_MODEL), _full((D_MODEL, D_MODEL))],
        out_specs=[row(S5_WIDTH), row(CONV_WIDTH), row(D_MODEL)],
        out_shape=[S((L, S5_WIDTH), f32), S((L, CONV_WIDTH), f32), S((L, D_MODEL), bf16)],
        compiler_params=_cp("parallel"),
    )(dx, w_out)


def _s5_discretise(lam_re, lam_im, log_dt, b_re, b_im):
    dt = jnp.exp(log_dt)
    mag = jnp.exp(lam_re * dt)
    abar_re = mag * jnp.cos(lam_im * dt)
    abar_im = mag * jnp.sin(lam_im * dt)
    den = lam_re * lam_re + lam_im * lam_im
    num_re = abar_re - 1.0
    f_re = ((num_re * lam_re + abar_im * lam_im) / den)[:, None, :]
    f_im = ((abar_im * lam_re - num_re * lam_im) / den)[:, None, :]
    return abar_re, abar_im, f_re * b_re - f_im * b_im, f_re * b_im + f_im * b_re


def _s5_params(lam_re, lam_im, log_dt, b_re, b_im):
    def body(lr, li, ld, br, bi, ar_ref, ai_ref, bbr_ref, bbi_ref):
        ar, ai, bbr, bbi = _s5_discretise(lr[...], li[...], ld[...], br[...], bi[...])
        ar_ref[...], ai_ref[...], bbr_ref[...], bbi_ref[...] = ar, ai, bbr, bbi

    gp = S((S5_GROUPS, S5_STATE), f32)
    gcp = S((S5_GROUPS, S5_GROUP_CH, S5_STATE), f32)
    return pl.pallas_call(body, name="s5_params", out_shape=[gp, gp, gcp, gcp])(lam_re, lam_im, log_dt, b_re, b_im)


def _s5_params_bwd(lam_re, lam_im, log_dt, b_re, b_im, d_ar, d_ai, d_bbr, d_bbi):
    def body(lr, li, ld, br, bi, car, cai, cbr, cbi, o_lr, o_li, o_ld, o_br, o_bi):
        _, vjp = jax.vjp(_s5_discretise, lr[...], li[...], ld[...], br[...], bi[...])
        o_lr[...], o_li[...], o_ld[...], o_br[...], o_bi[...] = vjp((car[...], cai[...], cbr[...], cbi[...]))

    gp = S((S5_GROUPS, S5_STATE), f32)
    gcp = S((S5_GROUPS, S5_GROUP_CH, S5_STATE), f32)
    return pl.pallas_call(body, name="s5_params_bwd", out_shape=[gp, gp, S((S5_GROUPS, 1), f32), gcp, gcp])(
        lam_re, lam_im, log_dt, b_re, b_im, d_ar, d_ai, d_bbr, d_bbi)


def _cmul(ar, ai, br, bi):
    return ar * br - ai * bi, ar * bi + ai * br


def _segment_starts(er, ei, ar, ai, steps, reverse):
    pr, pi = ar, ai
    n = 1
    while n < steps:
        pr, pi = _cmul(pr, pi, pr, pi)
        n *= 2
    assert n == steps
    row = lax.broadcasted_iota(jnp.int32, (SEGMENTS, SCAN_LANES), 0)
    hr = jnp.zeros((1, SCAN_LANES), f32)
    hi = jnp.zeros((1, SCAN_LANES), f32)
    out_r = jnp.zeros((SEGMENTS, SCAN_LANES), f32)
    out_i = jnp.zeros((SEGMENTS, SCAN_LANES), f32)
    order = range(SEGMENTS - 1, 0, -1) if reverse else range(0, SEGMENTS - 1)
    for r in order:
        qr, qi = _cmul(pr, pi, hr, hi)
        hr, hi = qr + er[r:r + 1, :], qi + ei[r:r + 1, :]
        nxt = r - 1 if reverse else r + 1
        out_r = jnp.where(row == nxt, hr, out_r)
        out_i = jnp.where(row == nxt, hi, out_i)
    return out_r, out_i


def _s5_read_bwd(dout, y_lin, u, d_skip, w_glu, b_glu, tm):
    L = u.shape[0]

    def body(do_ref, yl_ref, u_ref, d_ref, w_ref, b_ref, dyl_ref, du_ref, dd_ref, dw_ref, db_ref):
        @pl.when(pl.program_id(0) == 0)
        def _():
            dd_ref[...] = jnp.zeros_like(dd_ref)
            dw_ref[...] = jnp.zeros_like(dw_ref)
            db_ref[...] = jnp.zeros_like(db_ref)

        u, d, dout = u_ref[...], d_ref[...], do_ref[...]
        y, gelu_vjp = jax.vjp(_gelu, yl_ref[...] + d * u)
        yb = y.astype(bf16)
        sig = _sigmoid(_dot(yb, w_ref[...]) + b_ref[...])
        dz = dout * y * sig * (1.0 - sig)
        dzb = dz.astype(bf16)
        dy = dout * sig + _dot(dzb, w_ref[...], NT)
        (dyp,) = gelu_vjp(dy)
        dyl_ref[...] = dyp.astype(bf16)
        du_ref[...] = d * dyp
        dd_ref[...] += _rows8(dyp * u)
        db_ref[...] += _rows8(dz)
        dw_ref[...] += _dot(yb, dzb, TN)

    row = pl.BlockSpec((tm, S5_WIDTH), lambda i: (i, 0))
    vec = _full((1, S5_WIDTH))
    part = _full((8, S5_WIDTH))
    return pl.pallas_call(
        body, name="s5_read_bwd", grid=(L // tm,),
        in_specs=[row, row, row, vec, _full((S5_WIDTH, S5_WIDTH)), vec],
        out_specs=[row, row, part, _full((S5_WIDTH, S5_WIDTH)), part],
        out_shape=[S((L, S5_WIDTH), bf16), S((L, S5_WIDTH), f32), S((8, S5_WIDTH), f32),
                   S((S5_WIDTH, S5_WIDTH), f32), S((8, S5_WIDTH), f32)],
        compiler_params=_cp("arbitrary"),
    )(dout, y_lin, u, d_skip, w_glu, b_glu)


S5_CHUNK_CH = SCAN_LANES // S5_STATE * S5_GROUP_CH


def _s5_two_phase(L, bi):
    rows = bi * SEGMENTS
    nb = L // rows
    whole = pltpu.VMEM((L // SEGMENTS, SEGMENTS, SCAN_LANES), f32)
    mat = pl.BlockSpec((S5_CHUNK_CH, SCAN_LANES), lambda c, j: (c, c))
    vec = pl.BlockSpec((1, SCAN_LANES), lambda c, j: (0, c))
    tile = pl.BlockSpec((SEGMENTS, SCAN_LANES), lambda c, j: (0, c))
    return rows, nb, whole, mat, vec, tile


def _s5_forward(u, a_re, a_im, bb_re, bb_im, cc_re, cc_im, bi, ride=()):
    L = u.shape[0]
    rows, nb, whole, mat, vec, _ = _s5_two_phase(L, bi)

    def body(u_ref, ar_ref, ai_ref, br_ref, bi_ref, cr_ref, ci_ref, sr_ref, si_ref, yl_ref, hr_ref, hi_ref, dr_ref, di_ref):
        j = pl.program_id(1)
        ar = jnp.broadcast_to(ar_ref[...], (SEGMENTS, SCAN_LANES))
        ai = jnp.broadcast_to(ai_ref[...], (SEGMENTS, SCAN_LANES))

        @pl.when(j == 0)
        def _():
            hr_ref[...] = jnp.zeros_like(hr_ref)
            hi_ref[...] = jnp.zeros_like(hi_ref)

        @pl.when(j < nb)
        def _():
            base = j * bi
            ub = u_ref[...].astype(bf16)
            dr_ref[pl.ds(base, bi)] = _dot(ub, br_ref[...]).reshape(bi, SEGMENTS, SCAN_LANES)
            di_ref[pl.ds(base, bi)] = _dot(ub, bi_ref[...]).reshape(bi, SEGMENTS, SCAN_LANES)

            def step(i, c):
                pr, pi = _cmul(ar, ai, c[0], c[1])
                return pr + dr_ref[base + i], pi + di_ref[base + i]

            hr_ref[...], hi_ref[...] = lax.fori_loop(0, bi, step, (hr_ref[...], hi_ref[...]), unroll=4)

        @pl.when(j == nb - 1)
        def _():
            hr_ref[...], hi_ref[...] = _segment_starts(hr_ref[...], hi_ref[...], ar_ref[...], ai_ref[...], L // SEGMENTS, False)

        @pl.when(j >= nb)
        def _():
            base = (j - nb) * bi

            def step(i, c):
                pr, pi = _cmul(ar, ai, c[0], c[1])
                nr, nim = pr + dr_ref[base + i], pi + di_ref[base + i]
                dr_ref[base + i] = nr
                di_ref[base + i] = nim
                return nr, nim

            hr_ref[...], hi_ref[...] = lax.fori_loop(0, bi, step, (hr_ref[...], hi_ref[...]), unroll=4)
            sr = dr_ref[pl.ds(base, bi)].reshape(rows, SCAN_LANES).astype(bf16)
            si = di_ref[pl.ds(base, bi)].reshape(rows, SCAN_LANES).astype(bf16)
            sr_ref[...] = sr
            si_ref[...] = si
            yl_ref[...] = _dot(sr, cr_ref[...], NT) - _dot(si, ci_ref[...], NT)

    u_spec = pl.BlockSpec((rows, S5_CHUNK_CH), lambda c, j: (jnp.minimum(j, nb - 1), c))
    late = lambda width: pl.BlockSpec((rows, width), lambda c, j: (jnp.maximum(j - nb, 0), c))
    return _pallas(
        body, ride=ride, name="s5_forward", grid=(S5_LANES // SCAN_LANES, 2 * nb),
        in_specs=[u_spec, vec, vec, mat, mat, mat, mat],
        out_specs=[late(SCAN_LANES), late(SCAN_LANES), late(S5_CHUNK_CH)],
        out_shape=[S((L, S5_LANES), bf16)] * 2 + [S((L, S5_WIDTH), f32)],
        scratch_shapes=[pltpu.VMEM((SEGMENTS, SCAN_LANES), f32)] * 2 + [whole] * 2,
        compiler_params=_cp("parallel", "arbitrary"),
    )(u, a_re, a_im, bb_re, bb_im, cc_re, cc_im)


def _s5_backward(dy, u, du_skip, s_re, s_im, a_re, a_im, bb_re, bb_im, cc_re, cc_im, bi, ride=()):
    L = u.shape[0]
    rows, nb, whole, mat, vec, tile = _s5_two_phase(L, bi)
    per = rows // 16

    def body(dy_ref, u_ref, dus_ref, sr_ref, si_ref, pr_ref, pi_ref, lr_ref, li_ref, ar_ref, ai_ref, br_ref, bi_ref, cr_ref,
             ci_ref, du_ref, dar_ref, dai_ref, dbr_ref, dbi_ref, dcr_ref, dci_ref, hr_ref, hi_ref, gr_ref, gi_ref, fr_ref, fi_ref):
        j = pl.program_id(1)
        ar = jnp.broadcast_to(ar_ref[...], (SEGMENTS, SCAN_LANES))
        ai = jnp.broadcast_to(ai_ref[...], (SEGMENTS, SCAN_LANES))

        @pl.when(j == 0)
        def _():
            for ref in (hr_ref, hi_ref, dar_ref, dai_ref, dbr_ref, dbi_ref, dcr_ref, dci_ref):
                ref[...] = jnp.zeros_like(ref)

        @pl.when(j < nb)
        def _():
            base = (nb - 1 - j) * bi
            dy = dy_ref[...]
            gr_ref[pl.ds(base, bi)] = _dot(dy, cr_ref[...]).reshape(bi, SEGMENTS, SCAN_LANES)
            gi_ref[pl.ds(base, bi)] = (-_dot(dy, ci_ref[...])).reshape(bi, SEGMENTS, SCAN_LANES)

            def step(n, c):
                i = base + bi - 1 - n
                qr, qi = _cmul(ar, ai, c[0], c[1])
                return qr + gr_ref[i], qi + gi_ref[i]

            hr_ref[...], hi_ref[...] = lax.fori_loop(0, bi, step, (hr_ref[...], hi_ref[...]), unroll=4)

        @pl.when(j == nb - 1)
        def _():
            hr_ref[...], hi_ref[...] = _segment_starts(hr_ref[...], hi_ref[...], ar_ref[...], ai_ref[...], L // SEGMENTS, True)

        @pl.when(j >= nb)
        def _():
            blk = 2 * nb - 1 - j
            base = blk * bi
            sr, si = sr_ref[...], si_ref[...]
            fr_ref[...] = sr.astype(f32).reshape(bi, SEGMENTS, SCAN_LANES)
            fi_ref[...] = si.astype(f32).reshape(bi, SEGMENTS, SCAN_LANES)

            def step(n, c):
                i = bi - 1 - n
                gr, gi, accr, acci = c
                qr, qi = _cmul(ar, ai, gr, gi)
                gr, gi = qr + gr_ref[base + i], qi + gi_ref[base + i]
                gr_ref[base + i] = gr
                gi_ref[base + i] = gi
                pr, pi = fr_ref[i - 1], fi_ref[i - 1]
                return gr, gi, accr + (gr * pr + gi * pi), acci + (gi * pr - gr * pi)

            gr, gi, accr, acci = lax.fori_loop(0, bi - 1, step, (hr_ref[...], hi_ref[...], dar_ref[...], dai_ref[...]), unroll=3)
            qr, qi = _cmul(ar, ai, gr, gi)
            gr, gi = qr + gr_ref[base], qi + gi_ref[base]
            gr_ref[base] = gr
            gi_ref[base] = gi
            hr_ref[...], hi_ref[...] = gr, gi
            row = lax.broadcasted_iota(jnp.int32, (SEGMENTS, SCAN_LANES), 0)
            older = lambda ref: ref[...].astype(f32)[SEGMENTS:, :]
            wrap_r = jnp.where(row == 0, 0.0, pltpu.roll(older(lr_ref), 1, 0))
            wrap_i = jnp.where(row == 0, 0.0, pltpu.roll(older(li_ref), 1, 0))
            pr = jnp.where(blk == 0, wrap_r, older(pr_ref))
            pi = jnp.where(blk == 0, wrap_i, older(pi_ref))
            dar_ref[...] = accr + gr * pr + gi * pi
            dai_ref[...] = acci + gi * pr - gr * pi

            g_re = gr_ref[pl.ds(base, bi)].reshape(rows, SCAN_LANES).astype(bf16)
            g_im = gi_ref[pl.ds(base, bi)].reshape(rows, SCAN_LANES).astype(bf16)
            ub = u_ref[...].astype(bf16)
            dy = dy_ref[...]
            du_ref[...] = dus_ref[...] + _dot(g_re, br_ref[...], NT) + _dot(g_im, bi_ref[...], NT)
            dbr_ref[...] += _dot(ub, g_re, TN)
            dbi_ref[...] += _dot(ub, g_im, TN)
            dcr_ref[...] += _dot(dy, sr, TN)
            dci_ref[...] -= _dot(dy, si, TN)

    block = lambda c, j: jnp.where(j < nb, nb - 1 - j, 2 * nb - 1 - j)
    late_block = lambda c, j: jnp.minimum(2 * nb - 1 - j, nb - 1)
    both = pl.BlockSpec((rows, S5_CHUNK_CH), lambda c, j: (block(c, j), c))
    chan = pl.BlockSpec((rows, S5_CHUNK_CH), lambda c, j: (late_block(c, j), c))
    state = pl.BlockSpec((rows, SCAN_LANES), lambda c, j: (late_block(c, j), c))
    prev = pl.BlockSpec((16, SCAN_LANES), lambda c, j: (jnp.maximum(late_block(c, j) * per - 1, 0), c))
    last = pl.BlockSpec((16, SCAN_LANES), lambda c, j: (L // 16 - 1, c))
    grad = pl.BlockSpec((S5_CHUNK_CH, SCAN_LANES), lambda c, j: (c, 0))
    return _pallas(
        body, ride=ride, name="s5_backward", grid=(S5_LANES // SCAN_LANES, 2 * nb),
        in_specs=[both, chan, chan, state, state, prev, prev, last, last, vec, vec, mat, mat, mat, mat],
        out_specs=[chan, tile, tile, grad, grad, grad, grad],
        out_shape=[S((L, S5_WIDTH), f32)] + [S((SEGMENTS, S5_LANES), f32)] * 2 + [S((S5_WIDTH, SCAN_LANES), f32)] * 4,
        scratch_shapes=[pltpu.VMEM((SEGMENTS, SCAN_LANES), f32)] * 2 + [whole] * 2 + [pltpu.VMEM((bi, SEGMENTS, SCAN_LANES), f32)] * 2,
        compiler_params=_cp("parallel", "arbitrary"),
    )(dy, u, du_skip, s_re, s_im, s_re, s_im, s_re, s_im, a_re, a_im, bb_re, bb_im, cc_re, cc_im)


def _s5_gate(y_lin, u, d_skip, w_glu, b_glu, tm, ride=()):
    L = u.shape[0]

    def body(yl_ref, u_ref, d_ref, w_ref, b_ref, o_ref):
        y = _gelu(yl_ref[...] + d_ref[...] * u_ref[...])
        z = _dot(y.astype(bf16), w_ref[...]) + b_ref[...]
        o_ref[...] = (y * _sigmoid(z)).astype(bf16)

    row = pl.BlockSpec((tm, S5_WIDTH), lambda i: (i, 0))
    vec = _full((1, S5_WIDTH))
    return _pallas(
        body, ride=ride, name="s5_gate", grid=(L // tm,),
        in_specs=[row, row, vec, _full((S5_WIDTH, S5_WIDTH)), vec],
        out_specs=row, out_shape=S((L, S5_WIDTH), bf16),
        compiler_params=_cp("parallel"),
    )(y_lin, u, d_skip, w_glu, b_glu)


def _group_mean(x, avg):
    hi = x.astype(bf16)
    lo = (x - hi.astype(f32)).astype(bf16)
    return _dot(hi, avg) + _dot(lo, avg)


def _conv_act(zn, ln_g, ln_b):
    t = zn * ln_g + ln_b
    return t * _sigmoid(t)


def _glu_padded(v_ref, halo_ref, zpad_ref, tm):
    v = v_ref[...]
    vh = halo_ref[...]
    zh = vh[:, :CONV_WIDTH] * _sigmoid(vh[:, CONV_WIDTH:])
    zpad_ref[:CONV_HALO, :] = jnp.where(pl.program_id(0) > 0, zh, 0.0)
    zpad_ref[CONV_HALO:CONV_HALO + tm, :] = v[:, :CONV_WIDTH] * _sigmoid(v[:, CONV_WIDTH:])
    zpad_ref[CONV_HALO + tm:, :] = jnp.zeros((8, CONV_WIDTH), f32)


def _shifted(pad_ref, sh_ref, tm):
    for b in range(8):
        sh_ref[b] = pad_ref[pl.ds(b, tm + CONV_HALO), :]


def _window(sh_ref, r0, off, rows):
    return sh_ref[off % 8, pl.ds(pl.multiple_of(r0 + 8 * (off // 8), 8), rows), :]


def _tap_sum(w_ref, sh_ref, taps, out_ref, tm, bias):
    def chunk(c, carry):
        r0 = pl.multiple_of(c * CONV_ROWS, CONV_ROWS)
        acc = jnp.zeros((CONV_ROWS, CONV_WIDTH), f32) + bias
        for k, off in taps:
            acc = acc + w_ref[k:k + 1, :] * _window(sh_ref, r0, off, CONV_ROWS)
        out_ref[pl.ds(r0, CONV_ROWS), :] = acc
        return carry

    lax.fori_loop(0, tm // CONV_ROWS, chunk, 0)


FWD_TAPS = [(k, CONV_HALO - (CONV_K - 1) + k) for k in range(CONV_K)]
BWD_TAPS = [(k, CONV_K - 1 - k) for k in range(CONV_K)]


def _conv_specs(tm):
    per = tm // CONV_HALO
    vrow = pl.BlockSpec((tm, 2 * CONV_WIDTH), lambda i: (i, 0))
    vhalo = pl.BlockSpec((CONV_HALO, 2 * CONV_WIDTH), lambda i: (jnp.maximum(i * per - 1, 0), 0))
    return vrow, vhalo


def _conv_scratch(tm):
    return [pltpu.VMEM((tm + CONV_HALO + 8, CONV_WIDTH), f32), pltpu.VMEM((8, tm + CONV_HALO, CONV_WIDTH), f32)]


def _conv_fwd(v, w_dw, b_dw, ln_g, ln_b, avg, tm, ride=()):
    L = v.shape[0]

    def body(v_ref, halo_ref, w_ref, b_ref, g_ref, bb_ref, avg_ref, o_ref, zc_ref, zpad_ref, zs_ref):
        _glu_padded(v_ref, halo_ref, zpad_ref, tm)
        _shifted(zpad_ref, zs_ref, tm)
        _tap_sum(w_ref, zs_ref, FWD_TAPS, zc_ref, tm, b_ref[...])
        zc = zc_ref[...]
        xc = zc - _group_mean(zc, avg_ref[...])
        zn = xc * lax.rsqrt(_group_mean(xc * xc, avg_ref[...]) + EPS)
        o_ref[...] = _conv_act(zn, g_ref[...], bb_ref[...]).astype(bf16)

    vrow, vhalo = _conv_specs(tm)
    vec = _full((1, CONV_WIDTH))
    row = pl.BlockSpec((tm, CONV_WIDTH), lambda i: (i, 0))
    return _pallas(
        body, ride=ride, name="conv_fwd", grid=(L // tm,),
        in_specs=[vrow, vhalo, _full((CONV_HALO, CONV_WIDTH)), vec, vec, vec, _full((CONV_WIDTH, CONV_WIDTH))],
        out_specs=[row, row], out_shape=[S((L, CONV_WIDTH), bf16), S((L, CONV_WIDTH), f32)],
        scratch_shapes=_conv_scratch(tm),
        compiler_params=_cp("arbitrary"),
    )(v, v, w_dw, b_dw, ln_g, ln_b, avg)


def _conv_bwd_norm(dout, zc, ln_g, ln_b, avg, tm):
    L = zc.shape[0]

    def body(do_ref, zc_ref, g_ref, bb_ref, avg_ref, dzc_ref, dg_ref, db_ref, dbd_ref):
        @pl.when(pl.program_id(0) == 0)
        def _():
            dg_ref[...] = jnp.zeros_like(dg_ref)
            db_ref[...] = jnp.zeros_like(db_ref)
            dbd_ref[...] = jnp.zeros_like(dbd_ref)

        avg = avg_ref[...]
        zc = zc_ref[...]
        xc = zc - _group_mean(zc, avg)
        rstd = lax.rsqrt(_group_mean(xc * xc, avg) + EPS)
        xhat = xc * rstd
        _, act_vjp = jax.vjp(_conv_act, xhat, g_ref[...], bb_ref[...])
        dxhat, dg, db = act_vjp(do_ref[...])
        dzc = rstd * (dxhat - _group_mean(dxhat, avg) - xhat * _group_mean(dxhat * xhat, avg))
        dzc_ref[...] = dzc
        dg_ref[0:1, :] += dg
        db_ref[0:1, :] += db
        dbd_ref[...] += _rows8(dzc)

    vec = _full((1, CONV_WIDTH))
    row = pl.BlockSpec((tm, CONV_WIDTH), lambda i: (i, 0))
    part = _full((8, CONV_WIDTH))
    return pl.pallas_call(
        body, name="conv_bwd_norm", grid=(L // tm,),
        in_specs=[row, row, vec, vec, _full((CONV_WIDTH, CONV_WIDTH))],
        out_specs=[row, part, part, part],
        out_shape=[S((L, CONV_WIDTH), f32)] + [S((8, CONV_WIDTH), f32)] * 3,
        compiler_params=_cp("arbitrary"),
    )(dout, zc, ln_g, ln_b, avg)


def _conv_bwd_taps(dzc, v, w_dw, tm, ride=()):
    L = v.shape[0]
    nt = L // tm
    per = tm // CONV_HALO

    def body(d_ref, dn_ref, v_ref, halo_ref, w_ref, dv_ref, dw_ref, zpad_ref, zs_ref, dpad_ref, ds_ref, dz_ref):
        i = pl.program_id(0)

        @pl.when(i == 0)
        def _():
            dw_ref[...] = jnp.zeros_like(dw_ref)

        _glu_padded(v_ref, halo_ref, zpad_ref, tm)
        _shifted(zpad_ref, zs_ref, tm)
        dpad_ref[:tm, :] = d_ref[...]
        dpad_ref[tm:tm + CONV_HALO, :] = jnp.where(i < nt - 1, dn_ref[...], 0.0)
        dpad_ref[tm + CONV_HALO:, :] = jnp.zeros((8, CONV_WIDTH), f32)
        _shifted(dpad_ref, ds_ref, tm)
        _tap_sum(w_ref, ds_ref, BWD_TAPS, dz_ref, tm, 0.0)

        for first in range(0, CONV_K, 8):
            taps = FWD_TAPS[first:first + 8]

            def chunk(c, accs, taps=taps):
                r0 = pl.multiple_of(c * 8, 8)
                d = d_ref[pl.ds(r0, 8), :]
                return tuple(acc + d * _window(zs_ref, r0, off, 8) for acc, (_, off) in zip(accs, taps))

            accs = lax.fori_loop(0, tm // 8, chunk, tuple(jnp.zeros((8, CONV_WIDTH), f32) for _ in taps), unroll=2)
            for acc, (k, _) in zip(accs, taps):
                dw_ref[k] += acc

        dz = dz_ref[...]
        v = v_ref[...]
        sig = _sigmoid(v[:, CONV_WIDTH:])
        dv_ref[:, :CONV_WIDTH] = dz * sig
        dv_ref[:, CONV_WIDTH:] = dz * v[:, :CONV_WIDTH] * sig * (1.0 - sig)

    vrow, vhalo = _conv_specs(tm)
    row = pl.BlockSpec((tm, CONV_WIDTH), lambda i: (i, 0))
    nxt = pl.BlockSpec((CONV_HALO, CONV_WIDTH), lambda i: (jnp.minimum((i + 1) * per, nt * per - 1), 0))
    return _pallas(
        body, ride=ride, name="conv_bwd_taps", grid=(nt,),
        in_specs=[row, nxt, vrow, vhalo, _full((CONV_HALO, CONV_WIDTH))],
        out_specs=[vrow, _full((CONV_HALO, 8, CONV_WIDTH))],
        out_shape=[S((L, 2 * CONV_WIDTH), f32), S((CONV_HALO, 8, CONV_WIDTH), f32)],
        scratch_shapes=_conv_scratch(tm) * 2 + [pltpu.VMEM((tm, CONV_WIDTH), f32)],
        compiler_params=_cp("arbitrary"),
    )(dzc, dzc, v, v, w_dw)


def _to_segments(a):
    L, c = a.shape
    return a.reshape(SEGMENTS, L // SEGMENTS, c).transpose(1, 0, 2).reshape(L, c)


def _from_segments(a):
    L, c = a.shape
    return a.reshape(L // SEGMENTS, SEGMENTS, c).transpose(1, 0, 2).reshape(L, c)


def _block_diag(ms):
    n = len(ms)

    def body(*refs):
        for a in range(n):
            out = refs[n + a]
            out[...] = jnp.zeros_like(out)
            for g in range(S5_GROUPS):
                rows = slice(g * S5_GROUP_CH, (g + 1) * S5_GROUP_CH)
                out[rows, g * S5_STATE:(g + 1) * S5_STATE] = refs[a][rows, :].astype(bf16)

    return pl.pallas_call(body, name="s5_block_diag", out_shape=[S((S5_WIDTH, S5_LANES), bf16)] * n,
                          compiler_params=pltpu.CompilerParams(vmem_limit_bytes=VMEM_LIMIT))(
        *[m.reshape(S5_WIDTH, S5_STATE) for m in ms])


def _diag_blocks(ms):
    n = len(ms)
    per_chunk = SCAN_LANES // S5_STATE

    def body(*refs):
        for a in range(n):
            for g in range(S5_GROUPS):
                rows = slice(g * S5_GROUP_CH, (g + 1) * S5_GROUP_CH)
                at = g % per_chunk * S5_STATE
                refs[n + a][rows, :] = refs[a][rows, at:at + S5_STATE]

    out = pl.pallas_call(body, name="s5_diag_blocks", out_shape=[S((S5_WIDTH, S5_STATE), f32)] * n,
                         compiler_params=pltpu.CompilerParams(vmem_limit_bytes=VMEM_LIMIT))(*ms)
    return [o.reshape(S5_GROUPS, S5_GROUP_CH, S5_STATE) for o in out]


class _NoExchanges:
    def before(self, point):
        return ()

    def after(self, point):
        pass

    def alone(self, point):
        pass


def _ffn_block(x, p, tag, tm, sched, head=None):
    point = tag + "_up"
    h, dadg, dadu, a = _ffn_up(x, p[tag + "_norm"], p[tag + "_w_gate"], p[tag + "_w_up"], tm, 768, tag, ride=sched.before(point))
    sched.after(point)
    if head is None:
        out = _ffn_down(x, a, p[tag + "_w_down"], tm, tag, ride=sched.before(tag + "_down"))
        sched.after(tag + "_down")
    else:
        out = _ffn_down_loss(x, a, p[tag + "_w_down"], *head, tm, tag)
    return out, (h, dadg, dadu, a)


def _ffn_block_bwd(dxo, x, p, tag, saved, tm, grads, sched, parts=1):
    h, dadg, dadu, a = saved
    dgate, dup, dxh = _ffn_bwd_act(dxo, p[tag + "_w_down"], dadg, dadu, tm, 768, tag, ride=sched.before(tag + "_bwd_act"))
    sched.after(tag + "_bwd_act")
    for which, lhs, rhs in (("gate", dgate, h), ("up", dup, h), ("down", a, dxh)):
        point = tag + "_dw_" + which
        grads[tag + "_w_" + which] = _mm_tn(lhs, rhs, bf16, point, ride=sched.before(point))
        sched.after(point)
    tiles = x.shape[0] // tm
    dx, dgs = None, []
    for k in range(parts):
        point = tag + "_bwd_in" + ("_%d" % k) * (parts > 1)
        dx, dg = _ffn_bwd_in(dxo, x, p[tag + "_norm"], dgate, dup, p[tag + "_w_gate"], p[tag + "_w_up"], tm, point,
                             tiles=(k * tiles // parts, tiles // parts), into=dx, ride=sched.before(point))
        sched.after(point)
        dgs.append(dg)
    grads[tag + "_norm"] = functools.reduce(jnp.add, dgs)
    return dx


def _local_step(x, target, p, grads, sched):
    L = x.shape[0]
    tm = min(512, L // 2)
    ni = L // SEGMENTS
    bi = min(64, ni)

    def carried(point, fn, *args):
        out = fn(*args, ride=sched.before(point))
        sched.after(point)
        return out

    x1, saved1 = _ffn_block(x, p, "ffn1", tm, sched)

    h2, u_s5, v = _mix_in(x1, p["mix_norm"], p["w_in"], tm)
    s5_in = (p["s5_lam_re"], p["s5_lam_im"], p["s5_log_dt"].reshape(S5_GROUPS, 1), p["s5_b_re"], p["s5_b_im"])
    abar_re, abar_im, bbar_re, bbar_im = _s5_params(*s5_in)
    a_re, a_im = abar_re.reshape(1, S5_LANES), abar_im.reshape(1, S5_LANES)
    bb_re, bb_im, cc_re, cc_im = _block_diag([bbar_re, bbar_im, p["s5_c_re"], p["s5_c_im"]])
    u_seg = _to_segments(u_s5)
    s_re, s_im, y_lin = carried("s5_forward", _s5_forward, u_seg, a_re, a_im, bb_re, bb_im, cc_re, cc_im, bi)
    y_s5 = _from_segments(_s5_gate(y_lin, u_seg, p["s5_d"], p["s5_w_glu"], p["s5_b_glu"], tm))
    w_dw = jnp.pad(p["conv_w_dw"], ((0, CONV_HALO - CONV_K), (0, 0)))
    heads = jnp.arange(CONV_WIDTH) // CONV_HEAD
    avg = ((heads[:, None] == heads[None, :]).astype(f32) / CONV_HEAD).astype(bf16)
    y_conv, zc = carried("conv_fwd", _conv_fwd, v, w_dw, p["conv_b_dw"], p["conv_ln_g"], p["conv_ln_b"], avg, tm)
    x2 = _mix_out(x1, y_s5, y_conv, p["w_out"], tm)

    (dx3, grads["final_norm"], loss_terms), saved2 = _ffn_block(
        x2, p, "ffn2", tm, sched, head=(target, p["final_norm"].reshape(1, D_MODEL)))

    dx2 = _ffn_block_bwd(dx3, x2, p, "ffn2", saved2, tm, grads, sched)

    dy_s5, dy_conv, dx2b = carried("mix_out_bwd", _mix_out_bwd, dx2, p["w_out"], tm)
    grads["w_out"] = jnp.concatenate([_mm_tn(y_s5, dx2b, bf16, "dw_out_s5"), _mm_tn(y_conv, dx2b, bf16, "dw_out_conv")], axis=0)
    dy_lin, du_skip, dd8, grads["s5_w_glu"], dbg8 = _s5_read_bwd(
        _to_segments(dy_s5), y_lin, u_seg, p["s5_d"], p["s5_w_glu"], p["s5_b_glu"], tm)
    grads["s5_d"] = dd8.sum(axis=0, keepdims=True)
    grads["s5_b_glu"] = dbg8.sum(axis=0, keepdims=True)
    du_seg, da_re8, da_im8, dbb_re, dbb_im, dcc_re, dcc_im = carried(
        "s5_backward", _s5_backward, dy_lin, u_seg, du_skip, s_re, s_im, a_re, -a_im, bb_re, bb_im, cc_re, cc_im, bi)
    d_abar = lambda a8: a8.sum(axis=0).reshape(S5_GROUPS, S5_STATE)
    grads["s5_c_re"], grads["s5_c_im"], d_bbr, d_bbi = _diag_blocks([dcc_re, dcc_im, dbb_re, dbb_im])
    d_lr, d_li, d_ld, d_br, d_bi = _s5_params_bwd(*s5_in, d_abar(da_re8), d_abar(da_im8), d_bbr, d_bbi)
    grads["s5_lam_re"], grads["s5_lam_im"], grads["s5_log_dt"] = d_lr, d_li, d_ld.reshape(1, S5_GROUPS)
    grads["s5_b_re"], grads["s5_b_im"] = d_br, d_bi
    dzc, dlg8, dlb8, dbd8 = _conv_bwd_norm(dy_conv, zc, p["conv_ln_g"], p["conv_ln_b"], avg, tm)
    grads["conv_ln_g"] = dlg8.sum(axis=0, keepdims=True)
    grads["conv_ln_b"] = dlb8.sum(axis=0, keepdims=True)
    grads["conv_b_dw"] = dbd8.sum(axis=0, keepdims=True)
    dv, dw8 = carried("conv_bwd_taps", _conv_bwd_taps, dzc, v, w_dw, tm)
    grads["conv_w_dw"] = dw8.sum(axis=1)[:CONV_K]
    dx1, grads["mix_norm"], dub = _mix_in_bwd(dx2, x1, p["mix_norm"], _from_segments(du_seg), dv, p["w_in"], tm)
    grads["w_in"] = _mm_tn(dub, h2, bf16, "dw_in")

    grads["loss_terms"] = loss_terms
    dx0 = _ffn_block_bwd(dx1, x, p, "ffn1", saved1, tm, grads, sched, parts=min(2, L // tm))
    sched.alone("tail")
    return loss_terms, dx0


MESH = pl.DeviceIdType.MESH
ANY = pl.BlockSpec(memory_space=pl.ANY)


def _place():
    return lax.axis_index("x"), lax.axis_index("y"), lax.axis_index("c")


class _Exchange:
    def __init__(self, ins, out_shape, sems, start, finish):
        self.ins, self.out_shape, self.sems, self.start, self.finish = list(ins), list(out_shape), list(sems), start, finish
        self.out = None


def _pallas(body, *, ride=(), **kw):
    if not ride:
        return pl.pallas_call(body, **kw)

    def run(*args):
        out_shape = kw.get("out_shape", [])
        single = not isinstance(out_shape, (list, tuple))
        shapes = [out_shape] if single else list(out_shape)
        out_specs = [kw["out_specs"]] if single else list(kw.get("out_specs", []))
        grid = tuple(kw.get("grid", ()))
        scratch = list(kw.get("scratch_shapes", ()))
        n_in, n_out, n_scr = len(args), len(shapes), len(scratch)
        r_in = [len(e.ins) for e in ride]
        r_out = [len(e.out_shape) for e in ride]
        r_sem = [len(e.sems) for e in ride]

        def wrapped(*refs):
            own_in, refs = refs[:n_in], refs[n_in:]
            ex_in, refs = refs[:sum(r_in)], refs[sum(r_in):]
            own_out, refs = refs[:n_out], refs[n_out:]
            ex_out, refs = refs[:sum(r_out)], refs[sum(r_out):]
            own_scr, ex_sem = refs[:n_scr], refs[n_scr:]
            parts = []
            for e, ni, no, ns in zip(ride, r_in, r_out, r_sem):
                parts.append((e, ex_in[:ni], ex_out[:no], ex_sem[:ns]))
                ex_in, ex_out, ex_sem = ex_in[ni:], ex_out[no:], ex_sem[ns:]

            def at(step):
                def go():
                    for e, i, o, s in parts:
                        getattr(e, step)(i, o, s)
                if grid:
                    ids = [pl.program_id(d) for d in range(len(grid))]
                    when = [i == (0 if step == "start" else g - 1) for i, g in zip(ids, grid)]
                    pl.when(functools.reduce(lambda a, b: a & b, when))(go)
                else:
                    go()

            at("start")
            if body is not None:
                body(*own_in, *own_out, *own_scr)
            at("finish")

        outs = pl.pallas_call(
            wrapped, name=kw["name"], grid=grid,
            in_specs=list(kw.get("in_specs", [])) + [ANY] * sum(r_in),
            out_specs=out_specs + [ANY] * sum(r_out),
            out_shape=shapes + [s for e in ride for s in e.out_shape],
            scratch_shapes=scratch + [s for e in ride for s in e.sems],
            input_output_aliases=kw.get("input_output_aliases", {}),
            compiler_params=_cp(*["arbitrary"] * len(grid)),
        )(*args, *[a for e in ride for a in e.ins])
        own, rest = outs[:n_out], outs[n_out:]
        for e, no in zip(ride, r_out):
            e.out, rest = list(rest[:no]), rest[no:]
        return own[0] if single else own

    return run


def _exchange(ride, name):
    _pallas(None, ride=ride, name=name)()


def _gather(arrs):
    n = len(arrs)

    def copies(ins, outs, sems):
        send_sems, recv_sems, local_sems = sems
        x, y, c = _place()
        me, sibling = (x, y, c), (x, y, 1 - c)
        chips = [(1 - x, y), (x, 1 - y), (1 - x, 1 - y)]

        def place(a, block):
            return outs[a].at[block]

        def copy(a, k, block, to, src=None):
            px, py, pc = block
            dst = place(a, 4 * px + 2 * py + pc)
            return pltpu.make_async_remote_copy(
                src_ref=dst if src is None else src, dst_ref=dst, send_sem=send_sems.at[7 * a + k],
                recv_sem=recv_sems.at[7 * a + k], device_id=to, device_id_type=MESH)

        def own():
            local = [pltpu.make_async_copy(ins[a], place(a, 4 * x + 2 * y + c), local_sems.at[a]) for a in range(n)]
            remote = []
            for a in range(n):
                remote.append(copy(a, 0, me, sibling, src=ins[a]))
                remote += [copy(a, 1 + j, me, (*chip, c), src=ins[a]) for j, chip in enumerate(chips)]
            return local, remote

        return c, me, sibling, chips, copy, own

    def start(ins, outs, sems):
        local, remote = copies(ins, outs, sems)[-1]()
        for cp in local + remote:
            cp.start()

    def finish(ins, outs, sems):
        c, me, sibling, chips, copy, own = copies(ins, outs, sems)
        passed = []
        for j, chip in enumerate(chips):
            for a in range(n):
                copy(a, 1 + j, (*chip, c), me).wait_recv()
                passed.append(copy(a, 4 + j, (*chip, c), sibling))
                passed[-1].start()
        for a in range(n):
            copy(a, 0, sibling, me).wait_recv()
            for j, chip in enumerate(chips):
                copy(a, 4 + j, (*chip, 1 - c), me).wait_recv()
        local, remote = own()
        for cp in remote + passed:
            cp.wait_send()
        for cp in local:
            cp.wait()

    dma = pltpu.SemaphoreType.DMA
    shapes = [S((N_DEV, *a.shape), a.dtype) for a in arrs]
    return _Exchange(arrs, shapes, [dma((7 * n,)), dma((7 * n,)), dma((n,))], start, finish)


def _swap_with_sibling(gs):
    n = len(gs)

    def copies(ins, outs, sems):
        x, y, c = _place()
        return [pltpu.make_async_remote_copy(
            src_ref=ins[a].at[:, 1 - c], dst_ref=outs[a], send_sem=sems[0].at[a], recv_sem=sems[1].at[a],
            device_id=(x, y, 1 - c), device_id_type=MESH) for a in range(n)]

    def start(ins, outs, sems):
        for cp in copies(ins, outs, sems):
            cp.start()

    def finish(ins, outs, sems):
        for cp in copies(ins, outs, sems):
            cp.wait()

    dma = pltpu.SemaphoreType.DMA
    return _Exchange(gs, [S((N_CHIP, *g.shape[2:]), g.dtype) for g in gs], [dma((n,)), dma((n,))], start, finish)


def _swap_with_chips(ps):
    n = len(ps)

    def copies(ins, outs, sems):
        x, y, c = _place()
        q = 2 * x + y
        peers = [(x, 1 - y), (1 - x, y), (1 - x, 1 - y)]

        def copy(a, j, slot_from, slot_to):
            px, py = peers[j]
            return pltpu.make_async_remote_copy(
                src_ref=ins[a].at[slot_from], dst_ref=outs[a].at[slot_to], send_sem=sems[0].at[3 * a + j],
                recv_sem=sems[1].at[3 * a + j], device_id=(px, py, c), device_id_type=MESH)

        sends = lambda: [copy(a, j, 2 * peers[j][0] + peers[j][1], q) for a in range(n) for j in range(3)]
        lands = lambda: [copy(a, j, q, 2 * peers[j][0] + peers[j][1]) for a in range(n) for j in range(3)]
        return sends, lands

    def start(ins, outs, sems):
        for cp in copies(ins, outs, sems)[0]():
            cp.start()

    def finish(ins, outs, sems):
        sends, lands = copies(ins, outs, sems)
        for cp in lands():
            cp.wait_recv()
        for cp in sends():
            cp.wait_send()

    dma = pltpu.SemaphoreType.DMA
    return _Exchange(ps, [S(p.shape, p.dtype) for p in ps], [dma((3 * n,)), dma((3 * n,))], start, finish)


def _row_tile(rows, cols, itemsize):
    t = rows
    while t * cols * itemsize > (1 << 20) and t % 32 == 0:
        t //= 2
    return t


def _add_sibling(g4, st, core, name):
    _, R, C = st.shape
    tr = _row_tile(R, C, 4)

    def body(c_ref, g_ref, s_ref, o_ref):
        o_ref[...] = (g_ref[...].astype(f32) + s_ref[...].astype(f32)).astype(bf16)

    mine = pl.BlockSpec((None, None, tr, C), lambda q, i, c: (q, c[0], i, 0))
    return pl.pallas_call(
        body, name=name,
        grid_spec=pltpu.PrefetchScalarGridSpec(
            num_scalar_prefetch=1, grid=(N_CHIP, R // tr),
            in_specs=[mine,
                      pl.BlockSpec((None, tr, C), lambda q, i, c: (q, i, 0))],
            out_specs=pl.BlockSpec((None, tr, C), lambda q, i, c: (q, i, 0))),
        out_shape=S((N_CHIP, R, C), bf16),
        compiler_params=_cp("parallel", "parallel"),
    )(core, g4, st)


def _adamw(w, g, m, v):
    m = B1 * m + (1.0 - B1) * g
    v = B2 * v + (1.0 - B2) * (g * g)
    m_hat = m / (1.0 - B1 ** STEP)
    v_hat = v / (1.0 - B2 ** STEP)
    return -LR * (m_hat / (jnp.sqrt(v_hat) + ADAM_EPS) + WD * w), m, v


def _adam_sharded(w, m, v, part, got, slots, name):
    R, C = w.shape
    _, Rp, Cp = part.shape
    if Rp == R:
        tr = _row_tile(R, Cp, 4)
    else:
        tr = R // 2 if R % 32 == 0 else R

    def body(s_ref, w_ref, m_ref, v_ref, p_ref, a_ref, b_ref, c_ref, g_out, d_out, m_out, v_out):
        g = p_ref[...].astype(f32) + a_ref[...].astype(f32) + b_ref[...].astype(f32) + c_ref[...].astype(f32)
        g = g[:, :C]
        g_out[...] = g
        d_out[...], m_out[...], v_out[...] = _adamw(w_ref[...], g, m_ref[...], v_ref[...])

    shard = pl.BlockSpec((tr, C), lambda i, s: (i, 0))
    slot = lambda k: pl.BlockSpec((None, tr, Cp), lambda i, s: (s[k], i, 0))
    return pl.pallas_call(
        body, name=name,
        grid_spec=pltpu.PrefetchScalarGridSpec(
            num_scalar_prefetch=1, grid=(R // tr,),
            in_specs=[shard, shard, shard, slot(0), slot(1), slot(2), slot(3)],
            out_specs=[shard] * 4),
        out_shape=[S((R, C), f32)] * 4,
        compiler_params=_cp("parallel"),
    )(slots, w, m, v, part, got, got, got)


def _adam_replicated(items, loss_terms, name):
    n = len(items)
    has_loss = loss_terms is not None

    def total(ref):
        g = ref[0]
        for d in range(1, N_DEV):
            g = g + ref[d]
        return g

    def body(*refs):
        ins, outs = refs[:4 * n + has_loss], refs[4 * n + has_loss:]
        for i in range(n):
            w_ref, m_ref, v_ref, g_ref = ins[4 * i:4 * i + 4]
            g = total(g_ref)
            outs[4 * i][...] = g
            outs[4 * i + 1][...], outs[4 * i + 2][...], outs[4 * i + 3][...] = _adamw(w_ref[...], g, m_ref[...], v_ref[...])
        if has_loss:
            outs[-1][...] = jnp.sum(total(ins[-1]), keepdims=True)

    flat = [a for item in items for a in item] + ([loss_terms] if has_loss else [])
    shapes = [S(item[0].shape, f32) for item in items for _ in range(4)] + ([S((1, 1), f32)] if has_loss else [])
    out = pl.pallas_call(body, name=name, out_shape=shapes,
                         compiler_params=pltpu.CompilerParams(vmem_limit_bytes=VMEM_LIMIT))(*flat)
    return [out[4 * i:4 * i + 4] for i in range(n)], (out[-1] if has_loss else None)


WEIGHTS = ["ffn1_norm", "ffn1_w_gate", "ffn1_w_up", "ffn1_w_down", "mix_norm", "w_in", "s5_lam_re", "s5_lam_im", "s5_log_dt",
           "s5_b_re", "s5_b_im", "s5_c_re", "s5_c_im", "s5_d", "s5_w_glu", "s5_b_glu", "conv_w_dw", "conv_b_dw", "conv_ln_g",
           "conv_ln_b", "w_out", "ffn2_norm", "ffn2_w_gate", "ffn2_w_up", "ffn2_w_down", "final_norm"]
SHARDED = ["ffn1_w_gate", "ffn1_w_up", "ffn1_w_down", "w_in", "s5_w_glu", "conv_w_dw", "w_out", "ffn2_w_gate", "ffn2_w_up",
           "ffn2_w_down"]
REPLICATED = [n for n in WEIGHTS if n not in SHARDED]
TRANSPOSED = ["ffn1_w_gate", "ffn1_w_up", "ffn2_w_gate", "ffn2_w_up", "w_in"]


def _shard_to_wire(n, w):
    if n.startswith("ffn"):
        w = jnp.pad(w, ((0, FF_SHARD_PAD - FF_SHARD), (0, 0)))
    elif n == "conv_w_dw":
        return jnp.pad(w, ((0, CONV_HALO - CONV_K), (0, 0)))
    return w.astype(bf16)


def _to_wire(shards, ride):
    names = list(shards)
    shapes = [jax.eval_shape(functools.partial(_shard_to_wire, n), shards[n]) for n in names]

    def body(*refs):
        for src, dst in zip(refs[:len(names)], refs[len(names):]):
            (r, c), (rp, cp) = src.shape, dst.shape
            dst[:r, :c] = src[...].astype(dst.dtype)
            if cp > c:
                dst[:, c:] = jnp.zeros((rp, cp - c), dst.dtype)
            if rp > r:
                dst[r:, :] = jnp.zeros((rp - r, cp), dst.dtype)

    out = _pallas(body, ride=ride, name="to_wire", out_shape=shapes, in_specs=[pl.BlockSpec(memory_space=pltpu.VMEM)] * len(names),
                  out_specs=[pl.BlockSpec(memory_space=pltpu.VMEM)] * len(names))(*[shards[n] for n in names])
    return dict(zip(names, out))


def _gathered_to_full(n, g):
    if n == "conv_w_dw":
        return g.transpose(1, 0, 2).reshape(CONV_HALO, CONV_WIDTH)[:CONV_K]
    return g.reshape(N_DEV * g.shape[1], g.shape[2])


def _grad_to_blocks(n, g):
    if n == "conv_w_dw":
        g = jnp.pad(g, ((0, CONV_HALO - CONV_K), (0, 0)))
        g = g.reshape(g.shape[0], N_DEV, g.shape[1] // N_DEV).transpose(1, 0, 2)
    else:
        g = g.reshape(N_DEV, g.shape[0] // N_DEV, g.shape[1])
    return g.astype(bf16).reshape(N_CHIP, 2, *g.shape[1:])


REPLICATED_LATE = ["ffn1_norm"]
REPLICATED_EARLY = [n for n in REPLICATED if n not in REPLICATED_LATE]

PLAN = {
    "start": [("gather", ["ffn1_w_gate", "ffn1_w_up"])],
    "ffn1_up": [("gather", ["ffn1_w_down", "w_in", "w_out", "s5_w_glu", "conv_w_dw"])],
    "ffn1_down": [("gather", ["ffn2_w_gate"])],
    "s5_forward": [("gather", ["ffn2_w_up"])],
    "conv_fwd": [("gather", ["ffn2_w_down"])],
    "ffn2_dw_up": [("sibling", ["ffn2_w_gate"])],
    "ffn2_dw_down": [("sibling", ["ffn2_w_up"])],
    "mix_out_bwd": [("sibling", ["ffn2_w_down"])],
    "s5_backward": [("chips", ["ffn2_w_gate", "ffn2_w_up"])],
    "conv_bwd_taps": [("chips", ["ffn2_w_down"])],
    "ffn1_bwd_act": [("sibling", ["w_in", "s5_w_glu", "conv_w_dw", "w_out"]), ("replicated", REPLICATED_EARLY)],
    "ffn1_dw_gate": [("chips", ["w_in", "s5_w_glu", "conv_w_dw", "w_out"])],
    "ffn1_dw_up": [("sibling", ["ffn1_w_gate"])],
    "ffn1_dw_down": [("sibling", ["ffn1_w_up"]), ("chips", ["ffn1_w_gate"])],
    "ffn1_bwd_in_0": [("sibling", ["ffn1_w_down"]), ("chips", ["ffn1_w_up"])],
    "ffn1_bwd_in_1": [("chips", ["ffn1_w_down"])],
    "tail": [("replicated", REPLICATED_LATE)],
}


class _Schedule:
    def __init__(self, wire, p, grads, core):
        self.wire, self.p, self.grads, self.core = wire, p, grads, core
        self.partial, self.reduced, self.everyone, self.pending = {}, {}, {}, []

    def before(self, point):
        assert not self.pending
        for kind, names in PLAN.get(point, ()):
            if kind == "gather":
                given = [self.wire[n] for n in names]
                ex = _gather(given)
            elif kind == "sibling":
                given = [_grad_to_blocks(n, self.grads[n]) for n in names]
                ex = _swap_with_sibling(given)
            elif kind == "chips":
                given = [self.partial.pop(n) for n in names]
                ex = _swap_with_chips(given)
            else:
                names = names + ["loss_terms"] * (names is REPLICATED_EARLY)
                given = [self.grads[n].reshape(self.p[n].shape) if n in self.p else self.grads[n] for n in names]
                ex = _gather(given)
            self.pending.append((kind, names, given, ex))
        return [ex for _, _, _, ex in self.pending]

    def after(self, point):
        for kind, names, given, ex in self.pending:
            if kind == "gather":
                for n, g in zip(names, ex.out):
                    self.p[n] = _gathered_to_full(n, g)
            elif kind == "sibling":
                for n, blocks, got in zip(names, given, ex.out):
                    self.partial[n] = _add_sibling(blocks, got, self.core, "reduce_add_" + n)
            elif kind == "chips":
                for n, part, got in zip(names, given, ex.out):
                    self.reduced[n] = (part, got)
            else:
                self.everyone.update(zip(names, ex.out))
        self.pending = []

    def alone(self, point):
        _exchange(self.before(point), point)
        self.after(point)


def kernel(x, ffn1_norm, ffn1_w_gate, ffn1_w_up, ffn1_w_down, mix_norm, w_in, s5_lam_re, s5_lam_im, s5_log_dt, s5_b_re, s5_b_im, s5_c_re, s5_c_im, s5_d, s5_w_glu, s5_b_glu, conv_w_dw, conv_b_dw, conv_ln_g, conv_ln_b, w_out, ffn2_norm, ffn2_w_gate, ffn2_w_up, ffn2_w_down, final_norm, loss_target, m_ffn1_norm, m_ffn1_w_gate, m_ffn1_w_up, m_ffn1_w_down, m_mix_norm, m_w_in, m_s5_lam_re, m_s5_lam_im, m_s5_log_dt, m_s5_b_re, m_s5_b_im, m_s5_c_re, m_s5_c_im, m_s5_d, m_s5_w_glu, m_s5_b_glu, m_conv_w_dw, m_conv_b_dw, m_conv_ln_g, m_conv_ln_b, m_w_out, m_ffn2_norm, m_ffn2_w_gate, m_ffn2_w_up, m_ffn2_w_down, m_final_norm, v_ffn1_norm, v_ffn1_w_gate, v_ffn1_w_up, v_ffn1_w_down, v_mix_norm, v_w_in, v_s5_lam_re, v_s5_lam_im, v_s5_log_dt, v_s5_b_re, v_s5_b_im, v_s5_c_re, v_s5_c_im, v_s5_d, v_s5_w_glu, v_s5_b_glu, v_conv_w_dw, v_conv_b_dw, v_conv_ln_g, v_conv_ln_b, v_w_out, v_ffn2_norm, v_ffn2_w_gate, v_ffn2_w_up, v_ffn2_w_down, v_final_norm):
    args = locals()
    w = {n: args[n] for n in WEIGHTS}
    m = {n: args["m_" + n] for n in WEIGHTS}
    v = {n: args["v_" + n] for n in WEIGHTS}
    xq, yq, cq = _place()
    q = 2 * xq + yq
    slots = jnp.stack([q, q ^ 1, q ^ 2, q ^ 3]).astype(jnp.int32)

    def shard2d(n, a):
        a = a.reshape(a.shape[-2:])
        return a.T if n in TRANSPOSED else a

    def view(n, a):
        if n.startswith("s5_b_") and a.ndim == 4:
            return a[0].transpose(0, 2, 1)
        return a[0] if a.ndim >= 3 else a.reshape(1, -1)

    def unview(n, a):
        return (a.transpose(0, 2, 1) if n.startswith("s5_b_") and a.ndim == 3 else a).reshape(w[n].shape)

    p = {n: view(n, w[n]) for n in REPLICATED}
    grads = {}
    first = PLAN["start"][0][1]
    wire = {n: _shard_to_wire(n, shard2d(n, w[n])) for n in first}
    sched = _Schedule(wire, p, grads, jnp.reshape(cq, (1,)).astype(jnp.int32))
    wire.update(_to_wire({n: shard2d(n, w[n]) for n in SHARDED if n not in first}, sched.before("start")))
    sched.after("start")
    _, dx = _local_step(x[0], loss_target[0], p, grads, sched)

    out = {}
    for n in SHARDED:
        part, got = sched.reduced[n]
        rows = got.shape[1] if n == "conv_w_dw" else shard2d(n, w[n]).shape[0]
        fit = lambda a: jnp.pad(shard2d(n, a), ((0, rows - shard2d(n, a).shape[0]), (0, 0)))
        res = _adam_sharded(fit(w[n]), fit(m[n]), fit(v[n]), part, got, slots, "adam_" + n)
        back = lambda r: r[:shard2d(n, w[n]).shape[0]]
        out[n] = [(back(r).T if n in TRANSPOSED else back(r)).reshape(w[n].shape) for r in res]

    for names in (REPLICATED_EARLY, REPLICATED_LATE):
        items = [(view(n, w[n]), view(n, m[n]), view(n, v[n]), sched.everyone[n]) for n in names]
        res, total = _adam_replicated(items, sched.everyone.get("loss_terms") if names is REPLICATED_EARLY else None,
                                      "adam_" + names[0])
        for n, r in zip(names, res):
            out[n] = [unview(n, a) for a in r]
        if total is not None:
            loss = total.reshape(())

    return (loss, dx.reshape(x.shape), *[out[n][0] for n in WEIGHTS], *[out[n][1] for n in WEIGHTS],
            *[out[n][2] for n in WEIGHTS], *[out[n][3] for n in WEIGHTS])
```

```python
import functools

import jax
import jax.numpy as jnp
from jax import lax
from jax.experimental import pallas as pl
from jax.experimental.pallas import tpu as pltpu

f32 = jnp.float32
bf16 = jnp.bfloat16
S = jax.ShapeDtypeStruct

N_DEV = 8
N_CHIP = 4
D_MODEL = 1024
D_FF = 2816
FF_CHUNKS = [(0, 768), (768, 1536), (1536, 2304), (2304, D_FF)]
S5_WIDTH = 512
S5_GROUPS = 32
S5_GROUP_CH = 16
S5_STATE = 64
S5_LANES = S5_GROUPS * S5_STATE
CONV_WIDTH = 512
CONV_K = 31
CONV_HALO = 32
CONV_HEAD = 64
CONV_ROWS = 32
IN_COLS = S5_WIDTH + 2 * CONV_WIDTH
SEGMENTS = 8
SCAN_LANES = 512
EPS = 1e-6
LR, B1, B2, ADAM_EPS, WD, STEP = 0.001, 0.9, 0.999, 1e-08, 0.01, 10
VMEM_LIMIT = 56 * 1024 * 1024

NN = (((1,), (0,)), ((), ()))
NT = (((1,), (1,)), ((), ()))
TN = (((0,), (0,)), ((), ()))


def _dot(a, b, dims=NN):
    return lax.dot_general(a, b, dims, preferred_element_type=f32)


def _cp(*sem):
    return pltpu.CompilerParams(dimension_semantics=sem, vmem_limit_bytes=VMEM_LIMIT)


def _rms(x, g):
    return x * lax.rsqrt(jnp.mean(x * x, axis=-1, keepdims=True) + EPS) * g


def _rms_bwd(x, g, dh):
    _, vjp = jax.vjp(_rms, x, g)
    return vjp(dh)


def _sigmoid(x):
    return 1.0 / (1.0 + jnp.exp(-x))


def _gelu(x):
    return 0.5 * x * (1.0 + jnp.tanh(0.7978845608028654 * (x + 0.044715 * x * x * x)))


def _rows8(x):
    t, c = x.shape
    return x.reshape(t // 8, 8, c).sum(axis=0)


def _full(shape):
    return pl.BlockSpec(shape, lambda *_: (0,) * len(shape))


def _resident(shape):
    return pl.BlockSpec(shape, lambda *_: (0,) * len(shape), pipeline_mode=pl.Buffered(1))


def _ffn_up(x, g, wg, wu, tm, tag, ride=()):
    L = x.shape[0]

    def body(x_ref, g_ref, wg_ref, wu_ref, h_ref, dadg_ref, dadu_ref, a_ref):
        h = _rms(x_ref[...], g_ref[...]).astype(bf16)
        h_ref[...] = h
        for lo, hi in FF_CHUNKS:
            cols = slice(lo, hi)
            gate =_dot(h, wg_ref[cols, :], NT)
            up = _dot(h, wu_ref[cols, :], NT)
            sig = _sigmoid(gate)
            silu = gate * sig
            dadg_ref[:, cols] = (up * (sig + silu * (1.0 - sig))).astype(bf16)
            dadu_ref[:, cols] = silu.astype(bf16)
            a_ref[:, cols] = (silu * up).astype(bf16)

    row = pl.BlockSpec((tm, D_MODEL), lambda i: (i, 0))
    wide = pl.BlockSpec((tm, D_FF), lambda i: (i, 0))
    return _pallas(
        body, ride=ride, name=tag + "_up", grid=(L // tm,),
        in_specs=[row, _full((1, D_MODEL)), _resident((D_FF, D_MODEL)), _resident((D_FF, D_MODEL))],
        out_specs=[row, wide, wide, wide],
        out_shape=[S((L, D_MODEL), bf16)] + [S((L, D_FF), bf16)] * 3,
        compiler_params=_cp("parallel"),
    )(x, g, wg, wu)


def _ffn_down(x, a, wd, tm, tag, ride=()):
    L = x.shape[0]

    def body(x_ref, a_ref, wd_ref, o_ref):
        o_ref[...] = x_ref[...] + 0.5 * _dot(a_ref[...], wd_ref[...])

    return _pallas(
        body, ride=ride, name=tag + "_down", grid=(L // tm,),
        in_specs=[pl.BlockSpec((tm, D_MODEL), lambda i: (i, 0)), pl.BlockSpec((tm, D_FF), lambda i: (i, 0)),
                  _resident((D_FF, D_MODEL))],
        out_specs=pl.BlockSpec((tm, D_MODEL), lambda i: (i, 0)),
        out_shape=S((L, D_MODEL), f32),
        compiler_params=_cp("parallel"),
    )(x, a, wd)


def _ffn_down_loss(x, a, wd, target, g, tm, tag):
    L = x.shape[0]

    def body(x_ref, a_ref, wd_ref, t_ref, g_ref, dx_ref, dg_ref, l_ref):
        @pl.when(pl.program_id(0) == 0)
        def _():
            dg_ref[...] = jnp.zeros_like(dg_ref)
            l_ref[...] = jnp.zeros_like(l_ref)

        xo = x_ref[...] + 0.5 * _dot(a_ref[...], wd_ref[...])
        g = g_ref[...]
        e = _rms(xo, g) - t_ref[...]
        l_ref[...] += _rows8(e * e) * (0.5 / D_MODEL)
        dx, dg = _rms_bwd(xo, g, e * (1.0 / D_MODEL))
        dx_ref[...] = dx
        dg_ref[...] += dg

    row = pl.BlockSpec((tm, D_MODEL), lambda i: (i, 0))
    return pl.pallas_call(
        body, name=tag + "_down_loss", grid=(L // tm,),
        in_specs=[row, pl.BlockSpec((tm, D_FF), lambda i: (i, 0)), _resident((D_FF, D_MODEL)), row, _full((1, D_MODEL))],
        out_specs=[row, _full((1, D_MODEL)), _full((8, D_MODEL))],
        out_shape=[S((L, D_MODEL), f32), S((1, D_MODEL), f32), S((8, D_MODEL), f32)],
        compiler_params=_cp("arbitrary"),
    )(x, a, wd, target, g)


def _ffn_bwd_act(dxo, wd, dadg, dadu, tm, tag, ride=()):
    L = dxo.shape[0]

    def body(dx_ref, wd_ref, dadg_ref, dadu_ref, dgate_ref, dup_ref, dxh_ref):
        dxh = (0.5 * dx_ref[...]).astype(bf16)
        dxh_ref[...] = dxh
        for lo, hi in FF_CHUNKS:
            cols = slice(lo, hi)
            da =_dot(dxh, wd_ref[cols, :], NT)
            dgate_ref[:, cols] = (da * dadg_ref[:, cols].astype(f32)).astype(bf16)
            dup_ref[:, cols] = (da * dadu_ref[:, cols].astype(f32)).astype(bf16)

    row = pl.BlockSpec((tm, D_MODEL), lambda i: (i, 0))
    wide = pl.BlockSpec((tm, D_FF), lambda i: (i, 0))
    return _pallas(
        body, ride=ride, name=tag + "_bwd_act", grid=(L // tm,),
        in_specs=[row, _resident((D_FF, D_MODEL)), wide, wide],
        out_specs=[wide, wide, row],
        out_shape=[S((L, D_FF), bf16), S((L, D_FF), bf16), S((L, D_MODEL), bf16)],
        compiler_params=_cp("parallel"),
    )(dxo, wd, dadg, dadu)


def _ffn_bwd_in(dxo, x, g, dgate, dup, wg, wu, tm, name, tiles=None, into=None, ride=()):
    L = x.shape[0]
    first, count = tiles or (0, L // tm)

    def body(dxo_ref, x_ref, g_ref, dgate_ref, dup_ref, wg_ref, wu_ref, *rest):
        dx_ref, dg_ref = rest[-2:]

        @pl.when(pl.program_id(0) == 0)
        def _():
            dg_ref[...] = jnp.zeros_like(dg_ref)

        dh = _dot(dgate_ref[...], wg_ref[...]) + _dot(dup_ref[...], wu_ref[...])
        dx, dg = _rms_bwd(x_ref[...], g_ref[...], dh)
        dx_ref[...] = dxo_ref[...] + dx
        dg_ref[...] += dg

    row = pl.BlockSpec((tm, D_MODEL), lambda i: (first + i, 0))
    wide = pl.BlockSpec((tm, D_FF), lambda i: (first + i, 0))
    return _pallas(
        body, ride=ride, name=name, grid=(count,),
        in_specs=[row, row, _full((1, D_MODEL)), wide, wide, _resident((D_FF, D_MODEL)), _resident((D_FF, D_MODEL))]
        + [ANY] * (into is not None),
        out_specs=[row, _full((1, D_MODEL))],
        out_shape=[S((L, D_MODEL), f32), S((1, D_MODEL), f32)],
        input_output_aliases={7: 0} if into is not None else {},
        compiler_params=_cp("arbitrary"),
    )(dxo, x, g, dgate, dup, wg, wu, *([into] if into is not None else []))


def _mm_tn(a, b, out_dtype, name, tm=512, tn=1024, ride=()):
    L, M = a.shape
    N = b.shape[1]
    tm, tn = min(tm, M), min(tn, N)
    while M % tm:
        tm //= 2
    while N % tn:
        tn //= 2

    def body(a_ref, b_ref, o_ref):
        o_ref[...] = _dot(a_ref[...].astype(bf16), b_ref[...].astype(bf16), TN).astype(out_dtype)

    return _pallas(
        body, ride=ride, name=name, grid=(M // tm, N // tn),
        in_specs=[pl.BlockSpec((L, tm), lambda i, j: (0, i)), pl.BlockSpec((L, tn), lambda i, j: (0, j))],
        out_specs=pl.BlockSpec((tm, tn), lambda i, j: (i, j)),
        out_shape=S((M, N), out_dtype),
        compiler_params=_cp("parallel", "parallel"),
    )(a, b)


def _mix_in(x, g, w_in, tm):
    L = x.shape[0]

    def body(x_ref, g_ref, w_ref, h_ref, us_ref, v_ref):
        h = _rms(x_ref[...], g_ref[...]).astype(bf16)
        h_ref[...] = h
        u = _dot(h, w_ref[...], NT)
        us_ref[...] = u[:, :S5_WIDTH]
        v_ref[...] = u[:, S5_WIDTH:]

    row = lambda c: pl.BlockSpec((tm, c), lambda i: (i, 0))
    return pl.pallas_call(
        body, name="mix_in", grid=(L // tm,),
        in_specs=[row(D_MODEL), _full((1, D_MODEL)), _full((IN_COLS, D_MODEL))],
        out_specs=[row(D_MODEL), row(S5_WIDTH), row(2 * CONV_WIDTH)],
        out_shape=[S((L, D_MODEL), bf16), S((L, S5_WIDTH), f32), S((L, 2 * CONV_WIDTH), f32)],
        compiler_params=_cp("parallel"),
    )(x, g, w_in)


def _mix_in_bwd(dxo, x, g, du_s5, dv, w_in, tm):
    L = x.shape[0]

    def body(dxo_ref, x_ref, g_ref, dus_ref, dv_ref, w_ref, dx_ref, dg_ref, dub_ref):
        @pl.when(pl.program_id(0) == 0)
        def _():
            dg_ref[...] = jnp.zeros_like(dg_ref)

        dus = dus_ref[...].astype(bf16)
        dvb = dv_ref[...].astype(bf16)
        dub_ref[:, :S5_WIDTH] = dus
        dub_ref[:, S5_WIDTH:] = dvb
        dh = _dot(dus, w_ref[:S5_WIDTH, :]) + _dot(dvb, w_ref[S5_WIDTH:, :])
        dx, dg = _rms_bwd(x_ref[...], g_ref[...], dh)
        dx_ref[...] = dxo_ref[...] + dx
        dg_ref[...] += dg

    row = lambda c: pl.BlockSpec((tm, c), lambda i: (i, 0))
    return pl.pallas_call(
        body, name="mix_in_bwd", grid=(L // tm,),
        in_specs=[row(D_MODEL), row(D_MODEL), _full((1, D_MODEL)), row(S5_WIDTH), row(2 * CONV_WIDTH),
                  _full((IN_COLS, D_MODEL))],
        out_specs=[row(D_MODEL), _full((1, D_MODEL)), row(IN_COLS)],
        out_shape=[S((L, D_MODEL), f32), S((1, D_MODEL), f32), S((L, IN_COLS), bf16)],
        compiler_params=_cp("arbitrary"),
    )(dxo, x, g, du_s5, dv, w_in)


def _mix_out(x, y_s5, y_conv, w_out, tm):
    L = x.shape[0]

    def body(x_ref, ys_ref, yc_ref, w_ref, o_ref):
        o_ref[...] = x_ref[...] + _dot(ys_ref[...], w_ref[:S5_WIDTH, :]) + _dot(yc_ref[...], w_ref[S5_WIDTH:, :])

    row = lambda c: pl.BlockSpec((tm, c), lambda i: (i, 0))
    return pl.pallas_call(
        body, name="mix_out", grid=(L // tm,),
        in_specs=[row(D_MODEL), row(S5_WIDTH), row(CONV_WIDTH), _full((D_MODEL, D_MODEL))],
        out_specs=row(D_MODEL), out_shape=S((L, D_MODEL), f32),
        compiler_params=_cp("parallel"),
    )(x, y_s5, y_conv, w_out)


def _mix_out_bwd(dx, w_out, tm, ride=()):
    L = dx.shape[0]

    def body(dx_ref, w_ref, dys_ref, dyc_ref, dxb_ref):
        dxb = dx_ref[...].astype(bf16)
        dxb_ref[...] = dxb
        dys_ref[...] = _dot(dxb, w_ref[:S5_WIDTH, :], NT)
        dyc_ref[...] = _dot(dxb, w_ref[S5_WIDTH:, :], NT)

    row = lambda c: pl.BlockSpec((tm, c), lambda i: (i, 0))
    return _pallas(
        body, ride=ride, name="mix_out_bwd", grid=(L // tm,),
        in_specs=[row(D_MODEL), _full((D_MODEL, D_MODEL))],
        out_specs=[row(S5_WIDTH), row(CONV_WIDTH), row(D_MODEL)],
        out_shape=[S((L, S5_WIDTH), f32), S((L, CONV_WIDTH), f32), S((L, D_MODEL), bf16)],
        compiler_params=_cp("parallel"),
    )(dx, w_out)


def _s5_discretise(lam_re, lam_im, log_dt, b_re, b_im):
    dt = jnp.exp(log_dt)
    mag = jnp.exp(lam_re * dt)
    abar_re = mag * jnp.cos(lam_im * dt)
    abar_im = mag * jnp.sin(lam_im * dt)
    den = lam_re * lam_re + lam_im * lam_im
    num_re = abar_re - 1.0
    f_re = ((num_re * lam_re + abar_im * lam_im) / den)[:, None, :]
    f_im = ((abar_im * lam_re - num_re * lam_im) / den)[:, None, :]
    return abar_re, abar_im, f_re * b_re - f_im * b_im, f_re * b_im + f_im * b_re


def _s5_params(lam_re, lam_im, log_dt, b_re, b_im):
    def body(lr, li, ld, br, bi, ar_ref, ai_ref, bbr_ref, bbi_ref):
        ar, ai, bbr, bbi = _s5_discretise(lr[...], li[...], ld[...], br[...], bi[...])
        ar_ref[...], ai_ref[...], bbr_ref[...], bbi_ref[...] = ar, ai, bbr, bbi

    gp = S((S5_GROUPS, S5_STATE), f32)
    gcp = S((S5_GROUPS, S5_GROUP_CH, S5_STATE), f32)
    return pl.pallas_call(body, name="s5_params", out_shape=[gp, gp, gcp, gcp])(lam_re, lam_im, log_dt, b_re, b_im)


def _s5_params_bwd(lam_re, lam_im, log_dt, b_re, b_im, d_ar, d_ai, d_bbr, d_bbi):
    def body(lr, li, ld, br, bi, car, cai, cbr, cbi, o_lr, o_li, o_ld, o_br, o_bi):
        _, vjp = jax.vjp(_s5_discretise, lr[...], li[...], ld[...], br[...], bi[...])
        o_lr[...], o_li[...], o_ld[...], o_br[...], o_bi[...] = vjp((car[...], cai[...], cbr[...], cbi[...]))

    gp = S((S5_GROUPS, S5_STATE), f32)
    gcp = S((S5_GROUPS, S5_GROUP_CH, S5_STATE), f32)
    return pl.pallas_call(body, name="s5_params_bwd", out_shape=[gp, gp, S((S5_GROUPS, 1), f32), gcp, gcp])(
        lam_re, lam_im, log_dt, b_re, b_im, d_ar, d_ai, d_bbr, d_bbi)


def _cmul(ar, ai, br, bi):
    return ar * br - ai * bi, ar * bi + ai * br


def _segment_starts(er, ei, ar, ai, steps, reverse):
    pr, pi = ar, ai
    n = 1
    while n < steps:
        pr, pi = _cmul(pr, pi, pr, pi)
        n *= 2
    assert n == steps
    row = lax.broadcasted_iota(jnp.int32, (SEGMENTS, SCAN_LANES), 0)
    hr = jnp.zeros((1, SCAN_LANES), f32)
    hi = jnp.zeros((1, SCAN_LANES), f32)
    out_r = jnp.zeros((SEGMENTS, SCAN_LANES), f32)
    out_i = jnp.zeros((SEGMENTS, SCAN_LANES), f32)
    order = range(SEGMENTS - 1, 0, -1) if reverse else range(0, SEGMENTS - 1)
    for r in order:
        qr, qi = _cmul(pr, pi, hr, hi)
        hr, hi = qr + er[r:r + 1, :], qi + ei[r:r + 1, :]
        nxt = r - 1 if reverse else r + 1
        out_r = jnp.where(row == nxt, hr, out_r)
        out_i = jnp.where(row == nxt, hi, out_i)
    return out_r, out_i


def _s5_read_bwd(dout, y_lin, u, d_skip, w_glu, b_glu, tm):
    L = u.shape[0]

    def body(do_ref, yl_ref, u_ref, d_ref, w_ref, b_ref, dyl_ref, du_ref, dd_ref, dw_ref, db_ref):
        @pl.when(pl.program_id(0) == 0)
        def _():
            dd_ref[...] = jnp.zeros_like(dd_ref)
            dw_ref[...] = jnp.zeros_like(dw_ref)
            db_ref[...] = jnp.zeros_like(db_ref)

        u, d, dout = u_ref[...], d_ref[...], do_ref[...]
        y, gelu_vjp = jax.vjp(_gelu, yl_ref[...] + d * u)
        yb = y.astype(bf16)
        sig = _sigmoid(_dot(yb, w_ref[...]) + b_ref[...])
        dz = dout * y * sig * (1.0 - sig)
        dzb = dz.astype(bf16)
        dy = dout * sig + _dot(dzb, w_ref[...], NT)
        (dyp,) = gelu_vjp(dy)
        dyl_ref[...] = dyp.astype(bf16)
        du_ref[...] = d * dyp
        dd_ref[...] += _rows8(dyp * u)
        db_ref[...] += _rows8(dz)
        dw_ref[...] += _dot(yb, dzb, TN)

    row = pl.BlockSpec((tm, S5_WIDTH), lambda i: (i, 0))
    vec = _full((1, S5_WIDTH))
    part = _full((8, S5_WIDTH))
    return pl.pallas_call(
        body, name="s5_read_bwd", grid=(L // tm,),
        in_specs=[row, row, row, vec, _full((S5_WIDTH, S5_WIDTH)), vec],
        out_specs=[row, row, part, _full((S5_WIDTH, S5_WIDTH)), part],
        out_shape=[S((L, S5_WIDTH), bf16), S((L, S5_WIDTH), f32), S((8, S5_WIDTH), f32),
                   S((S5_WIDTH, S5_WIDTH), f32), S((8, S5_WIDTH), f32)],
        compiler_params=_cp("arbitrary"),
    )(dout, y_lin, u, d_skip, w_glu, b_glu)


S5_CHUNK_CH = SCAN_LANES // S5_STATE * S5_GROUP_CH


def _s5_two_phase(L, bi):
    rows = bi * SEGMENTS
    nb = L // rows
    whole = pltpu.VMEM((L // SEGMENTS, SEGMENTS, SCAN_LANES), f32)
    mat = pl.BlockSpec((S5_CHUNK_CH, SCAN_LANES), lambda c, j: (c, c))
    vec = pl.BlockSpec((1, SCAN_LANES), lambda c, j: (0, c))
    tile = pl.BlockSpec((SEGMENTS, SCAN_LANES), lambda c, j: (0, c))
    return rows, nb, whole, mat, vec, tile


def _s5_forward(u, a_re, a_im, bb_re, bb_im, cc_re, cc_im, bi, ride=()):
    L = u.shape[0]
    rows, nb, whole, mat, vec, _ = _s5_two_phase(L, bi)

    def body(u_ref, ar_ref, ai_ref, br_ref, bi_ref, cr_ref, ci_ref, sr_ref, si_ref, yl_ref, hr_ref, hi_ref, dr_ref, di_ref):
        j = pl.program_id(1)
        ar = jnp.broadcast_to(ar_ref[...], (SEGMENTS, SCAN_LANES))
        ai = jnp.broadcast_to(ai_ref[...], (SEGMENTS, SCAN_LANES))

        @pl.when(j == 0)
        def _():
            hr_ref[...] = jnp.zeros_like(hr_ref)
            hi_ref[...] = jnp.zeros_like(hi_ref)

        @pl.when(j < nb)
        def _():
            base = j * bi
            ub = u_ref[...].astype(bf16)
            dr_ref[pl.ds(base, bi)] = _dot(ub, br_ref[...]).reshape(bi, SEGMENTS, SCAN_LANES)
            di_ref[pl.ds(base, bi)] = _dot(ub, bi_ref[...]).reshape(bi, SEGMENTS, SCAN_LANES)

            def step(i, c):
                pr, pi = _cmul(ar, ai, c[0], c[1])
                return pr + dr_ref[base + i], pi + di_ref[base + i]

            hr_ref[...], hi_ref[...] = lax.fori_loop(0, bi, step, (hr_ref[...], hi_ref[...]), unroll=4)

        @pl.when(j == nb - 1)
        def _():
            hr_ref[...], hi_ref[...] = _segment_starts(hr_ref[...], hi_ref[...], ar_ref[...], ai_ref[...], L // SEGMENTS, False)

        @pl.when(j >= nb)
        def _():
            base = (j - nb) * bi

            def step(i, c):
                pr, pi = _cmul(ar, ai, c[0], c[1])
                nr, nim = pr + dr_ref[base + i], pi + di_ref[base + i]
                dr_ref[base + i] = nr
                di_ref[base + i] = nim
                return nr, nim

            hr_ref[...], hi_ref[...] = lax.fori_loop(0, bi, step, (hr_ref[...], hi_ref[...]), unroll=4)
            sr = dr_ref[pl.ds(base, bi)].reshape(rows, SCAN_LANES).astype(bf16)
            si = di_ref[pl.ds(base, bi)].reshape(rows, SCAN_LANES).astype(bf16)
            sr_ref[...] = sr
            si_ref[...] = si
            yl_ref[...] = _dot(sr, cr_ref[...], NT) - _dot(si, ci_ref[...], NT)

    u_spec = pl.BlockSpec((rows, S5_CHUNK_CH), lambda c, j: (jnp.minimum(j, nb - 1), c))
    late = lambda width: pl.BlockSpec((rows, width), lambda c, j: (jnp.maximum(j - nb, 0), c))
    return _pallas(
        body, ride=ride, name="s5_forward", grid=(S5_LANES // SCAN_LANES, 2 * nb),
        in_specs=[u_spec, vec, vec, mat, mat, mat, mat],
        out_specs=[late(SCAN_LANES), late(SCAN_LANES), late(S5_CHUNK_CH)],
        out_shape=[S((L, S5_LANES), bf16)] * 2 + [S((L, S5_WIDTH), f32)],
        scratch_shapes=[pltpu.VMEM((SEGMENTS, SCAN_LANES), f32)] * 2 + [whole] * 2,
        compiler_params=_cp("parallel", "arbitrary"),
    )(u, a_re, a_im, bb_re, bb_im, cc_re, cc_im)


def _s5_backward(dy, u, du_skip, s_re, s_im, a_re, a_im, bb_re, bb_im, cc_re, cc_im, bi, ride=()):
    L = u.shape[0]
    rows, nb, whole, mat, vec, tile = _s5_two_phase(L, bi)
    per = rows // 16

    def body(dy_ref, u_ref, dus_ref, sr_ref, si_ref, pr_ref, pi_ref, lr_ref, li_ref, ar_ref, ai_ref, br_ref, bi_ref, cr_ref,
             ci_ref, du_ref, dar_ref, dai_ref, dbr_ref, dbi_ref, dcr_ref, dci_ref, hr_ref, hi_ref, gr_ref, gi_ref, fr_ref, fi_ref):
        j = pl.program_id(1)
        ar = jnp.broadcast_to(ar_ref[...], (SEGMENTS, SCAN_LANES))
        ai = jnp.broadcast_to(ai_ref[...], (SEGMENTS, SCAN_LANES))

        @pl.when(j == 0)
        def _():
            for ref in (hr_ref, hi_ref, dar_ref, dai_ref, dbr_ref, dbi_ref, dcr_ref, dci_ref):
                ref[...] = jnp.zeros_like(ref)

        @pl.when(j < nb)
        def _():
            base = (nb - 1 - j) * bi
            dy = dy_ref[...]
            gr_ref[pl.ds(base, bi)] = _dot(dy, cr_ref[...]).reshape(bi, SEGMENTS, SCAN_LANES)
            gi_ref[pl.ds(base, bi)] = (-_dot(dy, ci_ref[...])).reshape(bi, SEGMENTS, SCAN_LANES)

            def step(n, c):
                i = base + bi - 1 - n
                qr, qi = _cmul(ar, ai, c[0], c[1])
                return qr + gr_ref[i], qi + gi_ref[i]

            hr_ref[...], hi_ref[...] = lax.fori_loop(0, bi, step, (hr_ref[...], hi_ref[...]), unroll=4)

        @pl.when(j == nb - 1)
        def _():
            hr_ref[...], hi_ref[...] = _segment_starts(hr_ref[...], hi_ref[...], ar_ref[...], ai_ref[...], L // SEGMENTS, True)

        @pl.when(j >= nb)
        def _():
            blk = 2 * nb - 1 - j
            base = blk * bi
            sr, si = sr_ref[...], si_ref[...]
            fr_ref[...] = sr.astype(f32).reshape(bi, SEGMENTS, SCAN_LANES)
            fi_ref[...] = si.astype(f32).reshape(bi, SEGMENTS, SCAN_LANES)

            def step(n, c):
                i = bi - 1 - n
                gr, gi, accr, acci = c
                qr, qi = _cmul(ar, ai, gr, gi)
                gr, gi = qr + gr_ref[base + i], qi + gi_ref[base + i]
                gr_ref[base + i] = gr
                gi_ref[base + i] = gi
                pr, pi = fr_ref[i - 1], fi_ref[i - 1]
                return gr, gi, accr + (gr * pr + gi * pi), acci + (gi * pr - gr * pi)

            gr, gi, accr, acci = lax.fori_loop(0, bi - 1, step, (hr_ref[...], hi_ref[...], dar_ref[...], dai_ref[...]), unroll=3)
            qr, qi = _cmul(ar, ai, gr, gi)
            gr, gi = qr + gr_ref[base], qi + gi_ref[base]
            gr_ref[base] = gr
            gi_ref[base] = gi
            hr_ref[...], hi_ref[...] = gr, gi
            row = lax.broadcasted_iota(jnp.int32, (SEGMENTS, SCAN_LANES), 0)
            older = lambda ref: ref[...].astype(f32)[SEGMENTS:, :]
            wrap_r = jnp.where(row == 0, 0.0, pltpu.roll(older(lr_ref), 1, 0))
            wrap_i = jnp.where(row == 0, 0.0, pltpu.roll(older(li_ref), 1, 0))
            pr = jnp.where(blk == 0, wrap_r, older(pr_ref))
            pi = jnp.where(blk == 0, wrap_i, older(pi_ref))
            dar_ref[...] = accr + gr * pr + gi * pi
            dai_ref[...] = acci + gi * pr - gr * pi

            g_re = gr_ref[pl.ds(base, bi)].reshape(rows, SCAN_LANES).astype(bf16)
            g_im = gi_ref[pl.ds(base, bi)].reshape(rows, SCAN_LANES).astype(bf16)
            ub = u_ref[...].astype(bf16)
            dy = dy_ref[...]
            du_ref[...] = dus_ref[...] + _dot(g_re, br_ref[...], NT) + _dot(g_im, bi_ref[...], NT)
            dbr_ref[...] += _dot(ub, g_re, TN)
            dbi_ref[...] += _dot(ub, g_im, TN)
            dcr_ref[...] += _dot(dy, sr, TN)
            dci_ref[...] -= _dot(dy, si, TN)

    block = lambda c, j: jnp.where(j < nb, nb - 1 - j, 2 * nb - 1 - j)
    late_block = lambda c, j: jnp.minimum(2 * nb - 1 - j, nb - 1)
    both = pl.BlockSpec((rows, S5_CHUNK_CH), lambda c, j: (block(c, j), c))
    chan = pl.BlockSpec((rows, S5_CHUNK_CH), lambda c, j: (late_block(c, j), c))
    state = pl.BlockSpec((rows, SCAN_LANES), lambda c, j: (late_block(c, j), c))
    prev = pl.BlockSpec((16, SCAN_LANES), lambda c, j: (jnp.maximum(late_block(c, j) * per - 1, 0), c))
    last = pl.BlockSpec((16, SCAN_LANES), lambda c, j: (L // 16 - 1, c))
    grad = pl.BlockSpec((S5_CHUNK_CH, SCAN_LANES), lambda c, j: (c, 0))
    return _pallas(
        body, ride=ride, name="s5_backward", grid=(S5_LANES // SCAN_LANES, 2 * nb),
        in_specs=[both, chan, chan, state, state, prev, prev, last, last, vec, vec, mat, mat, mat, mat],
        out_specs=[chan, tile, tile, grad, grad, grad, grad],
        out_shape=[S((L, S5_WIDTH), f32)] + [S((SEGMENTS, S5_LANES), f32)] * 2 + [S((S5_WIDTH, SCAN_LANES), f32)] * 4,
        scratch_shapes=[pltpu.VMEM((SEGMENTS, SCAN_LANES), f32)] * 2 + [whole] * 2 + [pltpu.VMEM((bi, SEGMENTS, SCAN_LANES), f32)] * 2,
        compiler_params=_cp("parallel", "arbitrary"),
    )(dy, u, du_skip, s_re, s_im, s_re, s_im, s_re, s_im, a_re, a_im, bb_re, bb_im, cc_re, cc_im)


def _s5_gate(y_lin, u, d_skip, w_glu, b_glu, tm, ride=()):
    L = u.shape[0]

    def body(yl_ref, u_ref, d_ref, w_ref, b_ref, o_ref):
        y = _gelu(yl_ref[...] + d_ref[...] * u_ref[...])
        z = _dot(y.astype(bf16), w_ref[...]) + b_ref[...]
        o_ref[...] = (y * _sigmoid(z)).astype(bf16)

    row = pl.BlockSpec((tm, S5_WIDTH), lambda i: (i, 0))
    vec = _full((1, S5_WIDTH))
    return _pallas(
        body, ride=ride, name="s5_gate", grid=(L // tm,),
        in_specs=[row, row, vec, _full((S5_WIDTH, S5_WIDTH)), vec],
        out_specs=row, out_shape=S((L, S5_WIDTH), bf16),
        compiler_params=_cp("parallel"),
    )(y_lin, u, d_skip, w_glu, b_glu)


def _group_mean(x, avg):
    hi = x.astype(bf16)
    lo = (x - hi.astype(f32)).astype(bf16)
    return _dot(hi, avg) + _dot(lo, avg)


def _conv_act(zn, ln_g, ln_b):
    t = zn * ln_g + ln_b
    return t * _sigmoid(t)


def _glu_padded(v_ref, halo_ref, zpad_ref, tm):
    v = v_ref[...]
    vh = halo_ref[...]
    zh = vh[:, :CONV_WIDTH] * _sigmoid(vh[:, CONV_WIDTH:])
    zpad_ref[:CONV_HALO, :] = jnp.where(pl.program_id(0) > 0, zh, 0.0)
    zpad_ref[CONV_HALO:CONV_HALO + tm, :] = v[:, :CONV_WIDTH] * _sigmoid(v[:, CONV_WIDTH:])
    zpad_ref[CONV_HALO + tm:, :] = jnp.zeros((8, CONV_WIDTH), f32)


def _shifted(pad_ref, sh_ref, tm):
    for b in range(8):
        sh_ref[b] = pad_ref[pl.ds(b, tm + CONV_HALO), :]


def _window(sh_ref, r0, off, rows):
    return sh_ref[off % 8, pl.ds(pl.multiple_of(r0 + 8 * (off // 8), 8), rows), :]


def _tap_sum(w_ref, sh_ref, taps, out_ref, tm, bias):
    def chunk(c, carry):
        r0 = pl.multiple_of(c * CONV_ROWS, CONV_ROWS)
        acc = jnp.zeros((CONV_ROWS, CONV_WIDTH), f32) + bias
        for k, off in taps:
            acc = acc + w_ref[k:k + 1, :] * _window(sh_ref, r0, off, CONV_ROWS)
        out_ref[pl.ds(r0, CONV_ROWS), :] = acc
        return carry

    lax.fori_loop(0, tm // CONV_ROWS, chunk, 0)


FWD_TAPS = [(k, CONV_HALO - (CONV_K - 1) + k) for k in range(CONV_K)]
BWD_TAPS = [(k, CONV_K - 1 - k) for k in range(CONV_K)]


def _conv_specs(tm):
    per = tm // CONV_HALO
    vrow = pl.BlockSpec((tm, 2 * CONV_WIDTH), lambda i: (i, 0))
    vhalo = pl.BlockSpec((CONV_HALO, 2 * CONV_WIDTH), lambda i: (jnp.maximum(i * per - 1, 0), 0))
    return vrow, vhalo


def _conv_scratch(tm):
    return [pltpu.VMEM((tm + CONV_HALO + 8, CONV_WIDTH), f32), pltpu.VMEM((8, tm + CONV_HALO, CONV_WIDTH), f32)]


def _conv_fwd(v, w_dw, b_dw, ln_g, ln_b, avg, tm, ride=()):
    L = v.shape[0]

    def body(v_ref, halo_ref, w_ref, b_ref, g_ref, bb_ref, avg_ref, o_ref, zc_ref, zpad_ref, zs_ref):
        _glu_padded(v_ref, halo_ref, zpad_ref, tm)
        _shifted(zpad_ref, zs_ref, tm)
        _tap_sum(w_ref, zs_ref, FWD_TAPS, zc_ref, tm, b_ref[...])
        zc = zc_ref[...]
        xc = zc - _group_mean(zc, avg_ref[...])
        zn = xc * lax.rsqrt(_group_mean(xc * xc, avg_ref[...]) + EPS)
        o_ref[...] = _conv_act(zn, g_ref[...], bb_ref[...]).astype(bf16)

    vrow, vhalo = _conv_specs(tm)
    vec = _full((1, CONV_WIDTH))
    row = pl.BlockSpec((tm, CONV_WIDTH), lambda i: (i, 0))
    return _pallas(
        body, ride=ride, name="conv_fwd", grid=(L // tm,),
        in_specs=[vrow, vhalo, _full((CONV_HALO, CONV_WIDTH)), vec, vec, vec, _full((CONV_WIDTH, CONV_WIDTH))],
        out_specs=[row, row], out_shape=[S((L, CONV_WIDTH), bf16), S((L, CONV_WIDTH), f32)],
        scratch_shapes=_conv_scratch(tm),
        compiler_params=_cp("arbitrary"),
    )(v, v, w_dw, b_dw, ln_g, ln_b, avg)


def _conv_bwd_norm(dout, zc, ln_g, ln_b, avg, tm):
    L = zc.shape[0]

    def body(do_ref, zc_ref, g_ref, bb_ref, avg_ref, dzc_ref, dg_ref, db_ref, dbd_ref):
        @pl.when(pl.program_id(0) == 0)
        def _():
            dg_ref[...] = jnp.zeros_like(dg_ref)
            db_ref[...] = jnp.zeros_like(db_ref)
            dbd_ref[...] = jnp.zeros_like(dbd_ref)

        avg = avg_ref[...]
        zc = zc_ref[...]
        xc = zc - _group_mean(zc, avg)
        rstd = lax.rsqrt(_group_mean(xc * xc, avg) + EPS)
        xhat = xc * rstd
        _, act_vjp = jax.vjp(_conv_act, xhat, g_ref[...], bb_ref[...])
        dxhat, dg, db = act_vjp(do_ref[...])
        dzc = rstd * (dxhat - _group_mean(dxhat, avg) - xhat * _group_mean(dxhat * xhat, avg))
        dzc_ref[...] = dzc
        dg_ref[0:1, :] += dg
        db_ref[0:1, :] += db
        dbd_ref[...] += _rows8(dzc)

    vec = _full((1, CONV_WIDTH))
    row = pl.BlockSpec((tm, CONV_WIDTH), lambda i: (i, 0))
    part = _full((8, CONV_WIDTH))
    return pl.pallas_call(
        body, name="conv_bwd_norm", grid=(L // tm,),
        in_specs=[row, row, vec, vec, _full((CONV_WIDTH, CONV_WIDTH))],
        out_specs=[row, part, part, part],
        out_shape=[S((L, CONV_WIDTH), f32)] + [S((8, CONV_WIDTH), f32)] * 3,
        compiler_params=_cp("arbitrary"),
    )(dout, zc, ln_g, ln_b, avg)


def _conv_bwd_taps(dzc, v, w_dw, tm, ride=()):
    L = v.shape[0]
    nt = L // tm
    per = tm // CONV_HALO

    def body(d_ref, dn_ref, v_ref, halo_ref, w_ref, dv_ref, dw_ref, zpad_ref, zs_ref, dpad_ref, ds_ref, dz_ref):
        i = pl.program_id(0)

        @pl.when(i == 0)
        def _():
            dw_ref[...] = jnp.zeros_like(dw_ref)

        _glu_padded(v_ref, halo_ref, zpad_ref, tm)
        _shifted(zpad_ref, zs_ref, tm)
        dpad_ref[:tm, :] = d_ref[...]
        dpad_ref[tm:tm + CONV_HALO, :] = jnp.where(i < nt - 1, dn_ref[...], 0.0)
        dpad_ref[tm + CONV_HALO:, :] = jnp.zeros((8, CONV_WIDTH), f32)
        _shifted(dpad_ref, ds_ref, tm)
        _tap_sum(w_ref, ds_ref, BWD_TAPS, dz_ref, tm, 0.0)

        for first in range(0, CONV_K, 8):
            taps = FWD_TAPS[first:first + 8]

            def chunk(c, accs, taps=taps):
                r0 = pl.multiple_of(c * 8, 8)
                d = d_ref[pl.ds(r0, 8), :]
                return tuple(acc + d * _window(zs_ref, r0, off, 8) for acc, (_, off) in zip(accs, taps))

            accs = lax.fori_loop(0, tm // 8, chunk, tuple(jnp.zeros((8, CONV_WIDTH), f32) for _ in taps), unroll=2)
            for acc, (k, _) in zip(accs, taps):
                dw_ref[k] += acc

        dz = dz_ref[...]
        v = v_ref[...]
        sig = _sigmoid(v[:, CONV_WIDTH:])
        dv_ref[:, :CONV_WIDTH] = dz * sig
        dv_ref[:, CONV_WIDTH:] = dz * v[:, :CONV_WIDTH] * sig * (1.0 - sig)

    vrow, vhalo = _conv_specs(tm)
    row = pl.BlockSpec((tm, CONV_WIDTH), lambda i: (i, 0))
    nxt = pl.BlockSpec((CONV_HALO, CONV_WIDTH), lambda i: (jnp.minimum((i + 1) * per, nt * per - 1), 0))
    return _pallas(
        body, ride=ride, name="conv_bwd_taps", grid=(nt,),
        in_specs=[row, nxt, vrow, vhalo, _full((CONV_HALO, CONV_WIDTH))],
        out_specs=[vrow, _full((CONV_HALO, 8, CONV_WIDTH))],
        out_shape=[S((L, 2 * CONV_WIDTH), f32), S((CONV_HALO, 8, CONV_WIDTH), f32)],
        scratch_shapes=_conv_scratch(tm) * 2 + [pltpu.VMEM((tm, CONV_WIDTH), f32)],
        compiler_params=_cp("arbitrary"),
    )(dzc, dzc, v, v, w_dw)


def _to_segments(a):
    L, c = a.shape
    return a.reshape(SEGMENTS, L // SEGMENTS, c).transpose(1, 0, 2).reshape(L, c)


def _from_segments(a):
    L, c = a.shape
    return a.reshape(L // SEGMENTS, SEGMENTS, c).transpose(1, 0, 2).reshape(L, c)


def _block_diag(ms):
    n = len(ms)

    def body(*refs):
        for a in range(n):
            out = refs[n + a]
            out[...] = jnp.zeros_like(out)
            for g in range(S5_GROUPS):
                rows = slice(g * S5_GROUP_CH, (g + 1) * S5_GROUP_CH)
                out[rows, g * S5_STATE:(g + 1) * S5_STATE] = refs[a][rows, :].astype(bf16)

    return pl.pallas_call(body, name="s5_block_diag", out_shape=[S((S5_WIDTH, S5_LANES), bf16)] * n,
                          compiler_params=pltpu.CompilerParams(vmem_limit_bytes=VMEM_LIMIT))(
        *[m.reshape(S5_WIDTH, S5_STATE) for m in ms])


def _diag_blocks(ms):
    n = len(ms)
    per_chunk = SCAN_LANES // S5_STATE

    def body(*refs):
        for a in range(n):
            for g in range(S5_GROUPS):
                rows = slice(g * S5_GROUP_CH, (g + 1) * S5_GROUP_CH)
                at = g % per_chunk * S5_STATE
                refs[n + a][rows, :] = refs[a][rows, at:at + S5_STATE]

    out = pl.pallas_call(body, name="s5_diag_blocks", out_shape=[S((S5_WIDTH, S5_STATE), f32)] * n,
                         compiler_params=pltpu.CompilerParams(vmem_limit_bytes=VMEM_LIMIT))(*ms)
    return [o.reshape(S5_GROUPS, S5_GROUP_CH, S5_STATE) for o in out]


class _NoExchanges:
    def before(self, point):
        return ()

    def after(self, point):
        pass

    def alone(self, point):
        pass


def _ffn_block(x, p, tag, tm, sched, head=None):
    point = tag + "_up"
    h, dadg, dadu, a = _ffn_up(x, p[tag + "_norm"], p[tag + "_w_gate"], p[tag + "_w_up"], tm, tag, ride=sched.before(point))
    sched.after(point)
    if head is None:
        out = _ffn_down(x, a, p[tag + "_w_down"], tm, tag, ride=sched.before(tag + "_down"))
        sched.after(tag + "_down")
    else:
        out = _ffn_down_loss(x, a, p[tag + "_w_down"], *head, tm, tag)
    return out, (h, dadg, dadu, a)


def _ffn_block_bwd(dxo, x, p, tag, saved, tm, grads, sched, parts=1):
    h, dadg, dadu, a = saved
    dgate, dup, dxh = _ffn_bwd_act(dxo, p[tag + "_w_down"], dadg, dadu, tm, tag, ride=sched.before(tag + "_bwd_act"))
    sched.after(tag + "_bwd_act")
    for which, lhs, rhs in (("gate", dgate, h), ("up", dup, h), ("down", a, dxh)):
        point = tag + "_dw_" + which
        grads[tag + "_w_" + which] = _mm_tn(lhs, rhs, bf16, point, ride=sched.before(point))
        sched.after(point)
    tiles = x.shape[0] // tm
    dx, dgs = None, []
    for k in range(parts):
        point = tag + "_bwd_in" + ("_%d" % k) * (parts > 1)
        dx, dg = _ffn_bwd_in(dxo, x, p[tag + "_norm"], dgate, dup, p[tag + "_w_gate"], p[tag + "_w_up"], tm, point,
                             tiles=(k * tiles // parts, tiles // parts), into=dx, ride=sched.before(point))
        sched.after(point)
        dgs.append(dg)
    grads[tag + "_norm"] = functools.reduce(jnp.add, dgs)
    return dx


def _local_step(x, target, p, grads, sched):
    L = x.shape[0]
    tm = min(512, L // 2)
    ni = L // SEGMENTS
    bi = min(64, ni)

    def carried(point, fn, *args):
        out = fn(*args, ride=sched.before(point))
        sched.after(point)
        return out

    x1, saved1 = _ffn_block(x, p, "ffn1", tm, sched)

    h2, u_s5, v = _mix_in(x1, p["mix_norm"], p["w_in"], tm)
    s5_in = (p["s5_lam_re"], p["s5_lam_im"], p["s5_log_dt"].reshape(S5_GROUPS, 1), p["s5_b_re"], p["s5_b_im"])
    abar_re, abar_im, bbar_re, bbar_im = _s5_params(*s5_in)
    a_re, a_im = abar_re.reshape(1, S5_LANES), abar_im.reshape(1, S5_LANES)
    bb_re, bb_im, cc_re, cc_im = _block_diag([bbar_re, bbar_im, p["s5_c_re"], p["s5_c_im"]])
    u_seg = _to_segments(u_s5)
    s_re, s_im, y_lin = carried("s5_forward", _s5_forward, u_seg, a_re, a_im, bb_re, bb_im, cc_re, cc_im, bi)
    y_s5 = _from_segments(_s5_gate(y_lin, u_seg, p["s5_d"], p["s5_w_glu"], p["s5_b_glu"], tm))
    w_dw = jnp.pad(p["conv_w_dw"], ((0, CONV_HALO - CONV_K), (0, 0)))
    heads = jnp.arange(CONV_WIDTH) // CONV_HEAD
    avg = ((heads[:, None] == heads[None, :]).astype(f32) / CONV_HEAD).astype(bf16)
    y_conv, zc = carried("conv_fwd", _conv_fwd, v, w_dw, p["conv_b_dw"], p["conv_ln_g"], p["conv_ln_b"], avg, tm)
    x2 = _mix_out(x1, y_s5, y_conv, p["w_out"], tm)

    (dx3, grads["final_norm"], loss_terms), saved2 = _ffn_block(
        x2, p, "ffn2", tm, sched, head=(target, p["final_norm"].reshape(1, D_MODEL)))

    dx2 = _ffn_block_bwd(dx3, x2, p, "ffn2", saved2, tm, grads, sched)

    dy_s5, dy_conv, dx2b = carried("mix_out_bwd", _mix_out_bwd, dx2, p["w_out"], tm)
    grads["w_out"] = jnp.concatenate([_mm_tn(y_s5, dx2b, bf16, "dw_out_s5"), _mm_tn(y_conv, dx2b, bf16, "dw_out_conv")], axis=0)
    dy_lin, du_skip, dd8, grads["s5_w_glu"], dbg8 = _s5_read_bwd(
        _to_segments(dy_s5), y_lin, u_seg, p["s5_d"], p["s5_w_glu"], p["s5_b_glu"], tm)
    grads["s5_d"] = dd8.sum(axis=0, keepdims=True)
    grads["s5_b_glu"] = dbg8.sum(axis=0, keepdims=True)
    du_seg, da_re8, da_im8, dbb_re, dbb_im, dcc_re, dcc_im = carried(
        "s5_backward", _s5_backward, dy_lin, u_seg, du_skip, s_re, s_im, a_re, -a_im, bb_re, bb_im, cc_re, cc_im, bi)
    d_abar = lambda a8: a8.sum(axis=0).reshape(S5_GROUPS, S5_STATE)
    grads["s5_c_re"], grads["s5_c_im"], d_bbr, d_bbi = _diag_blocks([dcc_re, dcc_im, dbb_re, dbb_im])
    d_lr, d_li, d_ld, d_br, d_bi = _s5_params_bwd(*s5_in, d_abar(da_re8), d_abar(da_im8), d_bbr, d_bbi)
    grads["s5_lam_re"], grads["s5_lam_im"], grads["s5_log_dt"] = d_lr, d_li, d_ld.reshape(1, S5_GROUPS)
    grads["s5_b_re"], grads["s5_b_im"] = d_br, d_bi
    dzc, dlg8, dlb8, dbd8 = _conv_bwd_norm(dy_conv, zc, p["conv_ln_g"], p["conv_ln_b"], avg, tm)
    grads["conv_ln_g"] = dlg8.sum(axis=0, keepdims=True)
    grads["conv_ln_b"] = dlb8.sum(axis=0, keepdims=True)
    grads["conv_b_dw"] = dbd8.sum(axis=0, keepdims=True)
    dv, dw8 = carried("conv_bwd_taps", _conv_bwd_taps, dzc, v, w_dw, tm)
    grads["conv_w_dw"] = dw8.sum(axis=1)[:CONV_K]
    dx1, grads["mix_norm"], dub = _mix_in_bwd(dx2, x1, p["mix_norm"], _from_segments(du_seg), dv, p["w_in"], tm)
    grads["w_in"] = _mm_tn(dub, h2, bf16, "dw_in")

    grads["loss_terms"] = loss_terms
    dx0 = _ffn_block_bwd(dx1, x, p, "ffn1", saved1, tm, grads, sched, parts=min(2, L // tm))
    sched.alone("tail")
    return loss_terms, dx0


MESH = pl.DeviceIdType.MESH
ANY = pl.BlockSpec(memory_space=pl.ANY)


def _place():
    return lax.axis_index("x"), lax.axis_index("y"), lax.axis_index("c")


class _Exchange:
    def __init__(self, ins, out_shape, sems, start, finish):
        self.ins, self.out_shape, self.sems, self.start, self.finish = list(ins), list(out_shape), list(sems), start, finish
        self.out = None


def _pallas(body, *, ride=(), **kw):
    if not ride:
        return pl.pallas_call(body, **kw)

    def run(*args):
        out_shape = kw.get("out_shape", [])
        single = not isinstance(out_shape, (list, tuple))
        shapes = [out_shape] if single else list(out_shape)
        out_specs = [kw["out_specs"]] if single else list(kw.get("out_specs", []))
        grid = tuple(kw.get("grid", ()))
        scratch = list(kw.get("scratch_shapes", ()))
        n_in, n_out, n_scr = len(args), len(shapes), len(scratch)
        r_in = [len(e.ins) for e in ride]
        r_out = [len(e.out_shape) for e in ride]
        r_sem = [len(e.sems) for e in ride]

        def wrapped(*refs):
            own_in, refs = refs[:n_in], refs[n_in:]
            ex_in, refs = refs[:sum(r_in)], refs[sum(r_in):]
            own_out, refs = refs[:n_out], refs[n_out:]
            ex_out, refs = refs[:sum(r_out)], refs[sum(r_out):]
            own_scr, ex_sem = refs[:n_scr], refs[n_scr:]
            parts = []
            for e, ni, no, ns in zip(ride, r_in, r_out, r_sem):
                parts.append((e, ex_in[:ni], ex_out[:no], ex_sem[:ns]))
                ex_in, ex_out, ex_sem = ex_in[ni:], ex_out[no:], ex_sem[ns:]

            def at(step):
                def go():
                    for e, i, o, s in parts:
                        getattr(e, step)(i, o, s)
                if grid:
                    ids = [pl.program_id(d) for d in range(len(grid))]
                    when = [i == (0 if step == "start" else g - 1) for i, g in zip(ids, grid)]
                    pl.when(functools.reduce(lambda a, b: a & b, when))(go)
                else:
                    go()

            at("start")
            if body is not None:
                body(*own_in, *own_out, *own_scr)
            at("finish")

        outs = pl.pallas_call(
            wrapped, name=kw["name"], grid=grid,
            in_specs=list(kw.get("in_specs", [])) + [ANY] * sum(r_in),
            out_specs=out_specs + [ANY] * sum(r_out),
            out_shape=shapes + [s for e in ride for s in e.out_shape],
            scratch_shapes=scratch + [s for e in ride for s in e.sems],
            input_output_aliases=kw.get("input_output_aliases", {}),
            compiler_params=_cp(*["arbitrary"] * len(grid)),
        )(*args, *[a for e in ride for a in e.ins])
        own, rest = outs[:n_out], outs[n_out:]
        for e, no in zip(ride, r_out):
            e.out, rest = list(rest[:no]), rest[no:]
        return own[0] if single else own

    return run


def _exchange(ride, name):
    _pallas(None, ride=ride, name=name)()


def _gather(arrs):
    n = len(arrs)

    def copies(ins, outs, sems):
        send_sems, recv_sems, local_sems = sems
        x, y, c = _place()
        me, sibling = (x, y, c), (x, y, 1 - c)
        chips = [(1 - x, y), (x, 1 - y), (1 - x, 1 - y)]

        def place(a, block):
            return outs[a].at[block]

        def copy(a, k, block, to, src=None):
            px, py, pc = block
            dst = place(a, 4 * px + 2 * py + pc)
            return pltpu.make_async_remote_copy(
                src_ref=dst if src is None else src, dst_ref=dst, send_sem=send_sems.at[7 * a + k],
                recv_sem=recv_sems.at[7 * a + k], device_id=to, device_id_type=MESH)

        def own():
            local = [pltpu.make_async_copy(ins[a], place(a, 4 * x + 2 * y + c), local_sems.at[a]) for a in range(n)]
            remote = []
            for a in range(n):
                remote.append(copy(a, 0, me, sibling, src=ins[a]))
                remote += [copy(a, 1 + j, me, (*chip, c), src=ins[a]) for j, chip in enumerate(chips)]
            return local, remote

        return c, me, sibling, chips, copy, own

    def start(ins, outs, sems):
        local, remote = copies(ins, outs, sems)[-1]()
        for cp in local + remote:
            cp.start()

    def finish(ins, outs, sems):
        c, me, sibling, chips, copy, own = copies(ins, outs, sems)
        passed = []
        for j, chip in enumerate(chips):
            for a in range(n):
                copy(a, 1 + j, (*chip, c), me).wait_recv()
                passed.append(copy(a, 4 + j, (*chip, c), sibling))
                passed[-1].start()
        for a in range(n):
            copy(a, 0, sibling, me).wait_recv()
            for j, chip in enumerate(chips):
                copy(a, 4 + j, (*chip, 1 - c), me).wait_recv()
        local, remote = own()
        for cp in remote + passed:
            cp.wait_send()
        for cp in local:
            cp.wait()

    dma = pltpu.SemaphoreType.DMA
    shapes = [S((N_DEV, *a.shape), a.dtype) for a in arrs]
    return _Exchange(arrs, shapes, [dma((7 * n,)), dma((7 * n,)), dma((n,))], start, finish)


def _swap_with_sibling(gs):
    n = len(gs)

    def copies(ins, outs, sems):
        x, y, c = _place()
        return [pltpu.make_async_remote_copy(
            src_ref=ins[a].at[:, 1 - c], dst_ref=outs[a], send_sem=sems[0].at[a], recv_sem=sems[1].at[a],
            device_id=(x, y, 1 - c), device_id_type=MESH) for a in range(n)]

    def start(ins, outs, sems):
        for cp in copies(ins, outs, sems):
            cp.start()

    def finish(ins, outs, sems):
        for cp in copies(ins, outs, sems):
            cp.wait()

    dma = pltpu.SemaphoreType.DMA
    return _Exchange(gs, [S((N_CHIP, *g.shape[2:]), g.dtype) for g in gs], [dma((n,)), dma((n,))], start, finish)


def _swap_with_chips(ps):
    n = len(ps)

    def copies(ins, outs, sems):
        x, y, c = _place()
        q = 2 * x + y
        peers = [(x, 1 - y), (1 - x, y), (1 - x, 1 - y)]

        def copy(a, j, slot_from, slot_to):
            px, py = peers[j]
            return pltpu.make_async_remote_copy(
                src_ref=ins[a].at[slot_from], dst_ref=outs[a].at[slot_to], send_sem=sems[0].at[3 * a + j],
                recv_sem=sems[1].at[3 * a + j], device_id=(px, py, c), device_id_type=MESH)

        sends = lambda: [copy(a, j, 2 * peers[j][0] + peers[j][1], q) for a in range(n) for j in range(3)]
        lands = lambda: [copy(a, j, q, 2 * peers[j][0] + peers[j][1]) for a in range(n) for j in range(3)]
        return sends, lands

    def start(ins, outs, sems):
        for cp in copies(ins, outs, sems)[0]():
            cp.start()

    def finish(ins, outs, sems):
        sends, lands = copies(ins, outs, sems)
        for cp in lands():
            cp.wait_recv()
        for cp in sends():
            cp.wait_send()

    dma = pltpu.SemaphoreType.DMA
    return _Exchange(ps, [S(p.shape, p.dtype) for p in ps], [dma((3 * n,)), dma((3 * n,))], start, finish)


def _row_tile(rows, cols, itemsize):
    t = rows
    while t * cols * itemsize > (1 << 20) and t % 32 == 0:
        t //= 2
    return t


def _add_sibling(g4, st, core, name):
    _, R, C = st.shape
    tr = _row_tile(R, C, 1)

    def body(c_ref, g_ref, s_ref, o_ref):
        o_ref[...] = (g_ref[...].astype(f32) + s_ref[...].astype(f32)).astype(bf16)

    mine = pl.BlockSpec((None, None, tr, C), lambda q, i, c: (q, c[0], i, 0))
    return pl.pallas_call(
        body, name=name,
        grid_spec=pltpu.PrefetchScalarGridSpec(
            num_scalar_prefetch=1, grid=(N_CHIP, R // tr),
            in_specs=[mine,
                      pl.BlockSpec((None, tr, C), lambda q, i, c: (q, i, 0))],
            out_specs=pl.BlockSpec((None, tr, C), lambda q, i, c: (q, i, 0))),
        out_shape=S((N_CHIP, R, C), bf16),
        compiler_params=_cp("parallel", "parallel"),
    )(core, g4, st)


def _adamw(w, g, m, v):
    m = B1 * m + (1.0 - B1) * g
    v = B2 * v + (1.0 - B2) * (g * g)
    m_hat = m / (1.0 - B1 ** STEP)
    v_hat = v / (1.0 - B2 ** STEP)
    return -LR * (m_hat / (jnp.sqrt(v_hat) + ADAM_EPS) + WD * w), m, v


def _adam_sharded(w, m, v, part, got, slots, name):
    R, C = w.shape
    _, Rp, Cp = part.shape
    if Rp == R:
        tr = _row_tile(R, Cp, 4)
    else:
        tr = R // 2 if R % 32 == 0 else R

    def body(s_ref, w_ref, m_ref, v_ref, p_ref, a_ref, b_ref, c_ref, g_out, d_out, m_out, v_out):
        g = p_ref[...].astype(f32) + a_ref[...].astype(f32) + b_ref[...].astype(f32) + c_ref[...].astype(f32)
        g = g[:, :C]
        g_out[...] = g
        d_out[...], m_out[...], v_out[...] = _adamw(w_ref[...], g, m_ref[...], v_ref[...])

    shard = pl.BlockSpec((tr, C), lambda i, s: (i, 0))
    slot = lambda k: pl.BlockSpec((None, tr, Cp), lambda i, s: (s[k], i, 0))
    return pl.pallas_call(
        body, name=name,
        grid_spec=pltpu.PrefetchScalarGridSpec(
            num_scalar_prefetch=1, grid=(R // tr,),
            in_specs=[shard, shard, shard, slot(0), slot(1), slot(2), slot(3)],
            out_specs=[shard] * 4),
        out_shape=[S((R, C), f32)] * 4,
        compiler_params=_cp("parallel"),
    )(slots, w, m, v, part, got, got, got)


def _adam_replicated(items, loss_terms, name):
    n = len(items)
    has_loss = loss_terms is not None

    def total(ref):
        g = ref[0]
        for d in range(1, N_DEV):
            g = g + ref[d]
        return g

    def body(*refs):
        ins, outs = refs[:4 * n + has_loss], refs[4 * n + has_loss:]
        for i in range(n):
            w_ref, m_ref, v_ref, g_ref = ins[4 * i:4 * i + 4]
            g = total(g_ref)
            outs[4 * i][...] = g
            outs[4 * i + 1][...], outs[4 * i + 2][...], outs[4 * i + 3][...] = _adamw(w_ref[...], g, m_ref[...], v_ref[...])
        if has_loss:
            outs[-1][...] = jnp.sum(total(ins[-1]), keepdims=True)

    flat = [a for item in items for a in item] + ([loss_terms] if has_loss else [])
    shapes = [S(item[0].shape, f32) for item in items for _ in range(4)] + ([S((1, 1), f32)] if has_loss else [])
    out = pl.pallas_call(body, name=name, out_shape=shapes,
                         compiler_params=pltpu.CompilerParams(vmem_limit_bytes=VMEM_LIMIT))(*flat)
    return [out[4 * i:4 * i + 4] for i in range(n)], (out[-1] if has_loss else None)


WEIGHTS = ["ffn1_norm", "ffn1_w_gate", "ffn1_w_up", "ffn1_w_down", "mix_norm", "w_in", "s5_lam_re", "s5_lam_im", "s5_log_dt",
           "s5_b_re", "s5_b_im", "s5_c_re", "s5_c_im", "s5_d", "s5_w_glu", "s5_b_glu", "conv_w_dw", "conv_b_dw", "conv_ln_g",
           "conv_ln_b", "w_out", "ffn2_norm", "ffn2_w_gate", "ffn2_w_up", "ffn2_w_down", "final_norm"]
SHARDED = ["ffn1_w_gate", "ffn1_w_up", "ffn1_w_down", "w_in", "s5_w_glu", "conv_w_dw", "w_out", "ffn2_w_gate", "ffn2_w_up",
           "ffn2_w_down"]
REPLICATED = [n for n in WEIGHTS if n not in SHARDED]
TRANSPOSED = ["ffn1_w_gate", "ffn1_w_up", "ffn2_w_gate", "ffn2_w_up", "w_in"]


def _shard_to_wire(n, w):
    if n == "conv_w_dw":
        return jnp.pad(w, ((0, CONV_HALO - CONV_K), (0, 0)))
    return w.astype(bf16)


def _to_wire(shards, ride):
    names = list(shards)
    shapes = [jax.eval_shape(functools.partial(_shard_to_wire, n), shards[n]) for n in names]

    def body(*refs):
        for src, dst in zip(refs[:len(names)], refs[len(names):]):
            (r, c), (rp, cp) = src.shape, dst.shape
            dst[:r, :c] = src[...].astype(dst.dtype)
            if cp > c:
                dst[:, c:] = jnp.zeros((rp, cp - c), dst.dtype)
            if rp > r:
                dst[r:, :] = jnp.zeros((rp - r, cp), dst.dtype)

    out = _pallas(body, ride=ride, name="to_wire", out_shape=shapes, in_specs=[pl.BlockSpec(memory_space=pltpu.VMEM)] * len(names),
                  out_specs=[pl.BlockSpec(memory_space=pltpu.VMEM)] * len(names))(*[shards[n] for n in names])
    return dict(zip(names, out))


def _gathered_to_full(n, g):
    if n == "conv_w_dw":
        return g.transpose(1, 0, 2).reshape(CONV_HALO, CONV_WIDTH)[:CONV_K]
    return g.reshape(N_DEV * g.shape[1], g.shape[2])


def _grad_to_blocks(n, g):
    if n == "conv_w_dw":
        g = jnp.pad(g, ((0, CONV_HALO - CONV_K), (0, 0)))
        g = g.reshape(g.shape[0], N_DEV, g.shape[1] // N_DEV).transpose(1, 0, 2)
    else:
        g = g.reshape(N_DEV, g.shape[0] // N_DEV, g.shape[1])
    return g.astype(bf16).reshape(N_CHIP, 2, *g.shape[1:])


REPLICATED_LATE = ["ffn1_norm"]
REPLICATED_EARLY = [n for n in REPLICATED if n not in REPLICATED_LATE]

PLAN = {
    "start": [("gather", ["ffn1_w_gate", "ffn1_w_up"])],
    "ffn1_up": [("gather", ["ffn1_w_down", "w_in", "w_out", "s5_w_glu", "conv_w_dw"])],
    "s5_forward": [("gather", ["ffn2_w_gate", "ffn2_w_up"])],
    "conv_fwd": [("gather", ["ffn2_w_down"])],
    "ffn2_dw_up": [("sibling", ["ffn2_w_gate"])],
    "ffn2_dw_down": [("sibling", ["ffn2_w_up"])],
    "mix_out_bwd": [("sibling", ["ffn2_w_down"])],
    "s5_backward": [("chips", ["ffn2_w_gate", "ffn2_w_up"])],
    "conv_bwd_taps": [("chips", ["ffn2_w_down"])],
    "ffn1_bwd_act": [("sibling", ["w_in", "s5_w_glu", "conv_w_dw", "w_out"]), ("replicated", REPLICATED_EARLY)],
    "ffn1_dw_gate": [("chips", ["w_in", "s5_w_glu", "conv_w_dw", "w_out"])],
    "ffn1_dw_up": [("sibling", ["ffn1_w_gate"])],
    "ffn1_dw_down": [("sibling", ["ffn1_w_up"]), ("chips", ["ffn1_w_gate"])],
    "ffn1_bwd_in_0": [("sibling", ["ffn1_w_down"]), ("chips", ["ffn1_w_up"])],
    "ffn1_bwd_in_1": [("chips", ["ffn1_w_down"])],
    "tail": [("replicated", REPLICATED_LATE)],
}


class _Schedule:
    def __init__(self, wire, p, grads, core):
        self.wire, self.p, self.grads, self.core = wire, p, grads, core
        self.partial, self.reduced, self.everyone, self.pending = {}, {}, {}, []

    def before(self, point):
        assert not self.pending
        for kind, names in PLAN.get(point, ()):
            if kind == "gather":
                given = [self.wire[n] for n in names]
                ex = _gather(given)
            elif kind == "sibling":
                given = [_grad_to_blocks(n, self.grads[n]) for n in names]
                ex = _swap_with_sibling(given)
            elif kind == "chips":
                given = [self.partial.pop(n) for n in names]
                ex = _swap_with_chips(given)
            else:
                names = names + ["loss_terms"] * (names is REPLICATED_EARLY)
                given = [self.grads[n].reshape(self.p[n].shape) if n in self.p else self.grads[n] for n in names]
                ex = _gather(given)
            self.pending.append((kind, names, given, ex))
        return [ex for _, _, _, ex in self.pending]

    def after(self, point):
        for kind, names, given, ex in self.pending:
            if kind == "gather":
                for n, g in zip(names, ex.out):
                    self.p[n] = _gathered_to_full(n, g)
            elif kind == "sibling":
                for n, blocks, got in zip(names, given, ex.out):
                    self.partial[n] = _add_sibling(blocks, got, self.core, "reduce_add_" + n)
            elif kind == "chips":
                for n, part, got in zip(names, given, ex.out):
                    self.reduced[n] = (part, got)
            else:
                self.everyone.update(zip(names, ex.out))
        self.pending = []

    def alone(self, point):
        _exchange(self.before(point), point)
        self.after(point)


def kernel(x, ffn1_norm, ffn1_w_gate, ffn1_w_up, ffn1_w_down, mix_norm, w_in, s5_lam_re, s5_lam_im, s5_log_dt, s5_b_re, s5_b_im, s5_c_re, s5_c_im, s5_d, s5_w_glu, s5_b_glu, conv_w_dw, conv_b_dw, conv_ln_g, conv_ln_b, w_out, ffn2_norm, ffn2_w_gate, ffn2_w_up, ffn2_w_down, final_norm, loss_target, m_ffn1_norm, m_ffn1_w_gate, m_ffn1_w_up, m_ffn1_w_down, m_mix_norm, m_w_in, m_s5_lam_re, m_s5_lam_im, m_s5_log_dt, m_s5_b_re, m_s5_b_im, m_s5_c_re, m_s5_c_im, m_s5_d, m_s5_w_glu, m_s5_b_glu, m_conv_w_dw, m_conv_b_dw, m_conv_ln_g, m_conv_ln_b, m_w_out, m_ffn2_norm, m_ffn2_w_gate, m_ffn2_w_up, m_ffn2_w_down, m_final_norm, v_ffn1_norm, v_ffn1_w_gate, v_ffn1_w_up, v_ffn1_w_down, v_mix_norm, v_w_in, v_s5_lam_re, v_s5_lam_im, v_s5_log_dt, v_s5_b_re, v_s5_b_im, v_s5_c_re, v_s5_c_im, v_s5_d, v_s5_w_glu, v_s5_b_glu, v_conv_w_dw, v_conv_b_dw, v_conv_ln_g, v_conv_ln_b, v_w_out, v_ffn2_norm, v_ffn2_w_gate, v_ffn2_w_up, v_ffn2_w_down, v_final_norm):
    args = locals()
    w = {n: args[n] for n in WEIGHTS}
    m = {n: args["m_" + n] for n in WEIGHTS}
    v = {n: args["v_" + n] for n in WEIGHTS}
    xq, yq, cq = _place()
    q = 2 * xq + yq
    slots = jnp.stack([q, q ^ 1, q ^ 2, q ^ 3]).astype(jnp.int32)

    def shard2d(n, a):
        a = a.reshape(a.shape[-2:])
        return a.T if n in TRANSPOSED else a

    def view(n, a):
        if n.startswith("s5_b_") and a.ndim == 4:
            return a[0].transpose(0, 2, 1)
        return a[0] if a.ndim >= 3 else a.reshape(1, -1)

    def unview(n, a):
        return (a.transpose(0, 2, 1) if n.startswith("s5_b_") and a.ndim == 3 else a).reshape(w[n].shape)

    p = {n: view(n, w[n]) for n in REPLICATED}
    grads = {}
    first = PLAN["start"][0][1]
    wire = {n: _shard_to_wire(n, shard2d(n, w[n])) for n in first}
    sched = _Schedule(wire, p, grads, jnp.reshape(cq, (1,)).astype(jnp.int32))
    wire.update(_to_wire({n: shard2d(n, w[n]) for n in SHARDED if n not in first}, sched.before("start")))
    sched.after("start")
    _, dx = _local_step(x[0], loss_target[0], p, grads, sched)

    out = {}
    for n in SHARDED:
        part, got = sched.reduced[n]
        rows = got.shape[1] if n == "conv_w_dw" else shard2d(n, w[n]).shape[0]
        fit = lambda a: jnp.pad(shard2d(n, a), ((0, rows - shard2d(n, a).shape[0]), (0, 0)))
        res = _adam_sharded(fit(w[n]), fit(m[n]), fit(v[n]), part, got, slots, "adam_" + n)
        back = lambda r: r[:shard2d(n, w[n]).shape[0]]
        out[n] = [(back(r).T if n in TRANSPOSED else back(r)).reshape(w[n].shape) for r in res]

    for names in (REPLICATED_EARLY, REPLICATED_LATE):
        items = [(view(n, w[n]), view(n, m[n]), view(n, v[n]), sched.everyone[n]) for n in names]
        res, total = _adam_replicated(items, sched.everyone.get("loss_terms") if names is REPLICATED_EARLY else None,
                                      "adam_" + names[0])
        for n, r in zip(names, res):
            out[n] = [unview(n, a) for a in r]
        if total is not None:
            loss = total.reshape(())

    return (loss, dx.reshape(x.shape), *[out[n][0] for n in WEIGHTS], *[out[n][1] for n in WEIGHTS],
            *[out[n][2] for n in WEIGHTS], *[out[n][3] for n in WEIGHTS])
```

```python
import functools

import jax
import jax.numpy as jnp
from jax import lax
from jax.experimental import pallas as pl
from jax.experimental.pallas import tpu as pltpu

f32 = jnp.float32
bf16 = jnp.bfloat16
S = jax.ShapeDtypeStruct

N_DEV = 8
N_CHIP = 4
D_MODEL = 1024
D_FF = 2816
FF_CHUNKS = [(0, 768), (768, 1536), (1536, 2304), (2304, D_FF)]
S5_WIDTH = 512
S5_GROUPS = 32
S5_GROUP_CH = 16
S5_STATE = 64
S5_LANES = S5_GROUPS * S5_STATE
CONV_WIDTH = 512
CONV_K = 31
CONV_HALO = 32
CONV_HEAD = 64
CONV_ROWS = 32
IN_COLS = S5_WIDTH + 2 * CONV_WIDTH
SEGMENTS = 8
SCAN_LANES = 512
EPS = 1e-6
LR, B1, B2, ADAM_EPS, WD, STEP = 0.001, 0.9, 0.999, 1e-08, 0.01, 10
VMEM_LIMIT = 56 * 1024 * 1024

NN = (((1,), (0,)), ((), ()))
NT = (((1,), (1,)), ((), ()))
TN = (((0,), (0,)), ((), ()))


def _dot(a, b, dims=NN):
    return lax.dot_general(a, b, dims, preferred_element_type=f32)


def _cp(*sem):
    return pltpu.CompilerParams(dimension_semantics=sem, vmem_limit_bytes=VMEM_LIMIT)


def _rms(x, g):
    return x * lax.rsqrt(jnp.mean(x * x, axis=-1, keepdims=True) + EPS) * g


def _rms_bwd(x, g, dh):
    _, vjp = jax.vjp(_rms, x, g)
    return vjp(dh)


def _sigmoid(x):
    return 1.0 / (1.0 + jnp.exp(-x))


def _gelu(x):
    return 0.5 * x * (1.0 + jnp.tanh(0.7978845608028654 * (x + 0.044715 * x * x * x)))


def _rows8(x):
    t, c = x.shape
    return x.reshape(t // 8, 8, c).sum(axis=0)


def _full(shape):
    return pl.BlockSpec(shape, lambda *_: (0,) * len(shape))


def _resident(shape):
    return pl.BlockSpec(shape, lambda *_: (0,) * len(shape), pipeline_mode=pl.Buffered(1))


def _ffn_up(x, g, wg, wu, tm, tag, ride=()):
    L = x.shape[0]

    def body(x_ref, g_ref, wg_ref, wu_ref, h_ref, dadg_ref, dadu_ref, a_ref):
        h = _rms(x_ref[...], g_ref[...]).astype(bf16)
        h_ref[...] = h
        for lo, hi in FF_CHUNKS:
            cols = slice(lo, hi)
            gate =_dot(h, wg_ref[cols, :], NT)
            up = _dot(h, wu_ref[cols, :], NT)
            sig = _sigmoid(gate)
            silu = gate * sig
            dadg_ref[:, cols] = (up * (sig + silu * (1.0 - sig))).astype(bf16)
            dadu_ref[:, cols] = silu.astype(bf16)
            a_ref[:, cols] = (silu * up).astype(bf16)

    row = pl.BlockSpec((tm, D_MODEL), lambda i: (i, 0))
    wide = pl.BlockSpec((tm, D_FF), lambda i: (i, 0))
    return _pallas(
        body, ride=ride, name=tag + "_up", grid=(L // tm,),
        in_specs=[row, _full((1, D_MODEL)), _resident((D_FF, D_MODEL)), _resident((D_FF, D_MODEL))],
        out_specs=[row, wide, wide, wide],
        out_shape=[S((L, D_MODEL), bf16)] + [S((L, D_FF), bf16)] * 3,
        compiler_params=_cp("parallel"),
    )(x, g, wg, wu)


def _ffn_down(x, a, wd, tm, tag, ride=()):
    L = x.shape[0]

    def body(x_ref, a_ref, wd_ref, o_ref):
        o_ref[...] = x_ref[...] + 0.5 * _dot(a_ref[...], wd_ref[...])

    return _pallas(
        body, ride=ride, name=tag + "_down", grid=(L // tm,),
        in_specs=[pl.BlockSpec((tm, D_MODEL), lambda i: (i, 0)), pl.BlockSpec((tm, D_FF), lambda i: (i, 0)),
                  _resident((D_FF, D_MODEL))],
        out_specs=pl.BlockSpec((tm, D_MODEL), lambda i: (i, 0)),
        out_shape=S((L, D_MODEL), f32),
        compiler_params=_cp("parallel"),
    )(x, a, wd)


def _ffn_down_loss(x, a, wd, target, g, tm, tag):
    L = x.shape[0]

    def body(x_ref, a_ref, wd_ref, t_ref, g_ref, dx_ref, dg_ref, l_ref):
        @pl.when(pl.program_id(0) == 0)
        def _():
            dg_ref[...] = jnp.zeros_like(dg_ref)
            l_ref[...] = jnp.zeros_like(l_ref)

        xo = x_ref[...] + 0.5 * _dot(a_ref[...], wd_ref[...])
        g = g_ref[...]
        e = _rms(xo, g) - t_ref[...]
        l_ref[...] += _rows8(e * e) * (0.5 / D_MODEL)
        dx, dg = _rms_bwd(xo, g, e * (1.0 / D_MODEL))
        dx_ref[...] = dx
        dg_ref[...] += dg

    row = pl.BlockSpec((tm, D_MODEL), lambda i: (i, 0))
    return pl.pallas_call(
        body, name=tag + "_down_loss", grid=(L // tm,),
        in_specs=[row, pl.BlockSpec((tm, D_FF), lambda i: (i, 0)), _resident((D_FF, D_MODEL)), row, _full((1, D_MODEL))],
        out_specs=[row, _full((1, D_MODEL)), _full((8, D_MODEL))],
        out_shape=[S((L, D_MODEL), f32), S((1, D_MODEL), f32), S((8, D_MODEL), f32)],
        compiler_params=_cp("arbitrary"),
    )(x, a, wd, target, g)


def _ffn_bwd_act(dxo, wd, dadg, dadu, tm, tag, ride=()):
    L = dxo.shape[0]

    def body(dx_ref, wd_ref, dadg_ref, dadu_ref, dgate_ref, dup_ref, dxh_ref):
        dxh = (0.5 * dx_ref[...]).astype(bf16)
        dxh_ref[...] = dxh
        for lo, hi in FF_CHUNKS:
            cols = slice(lo, hi)
            da =_dot(dxh, wd_ref[cols, :], NT)
            dgate_ref[:, cols] = (da * dadg_ref[:, cols].astype(f32)).astype(bf16)
            dup_ref[:, cols] = (da * dadu_ref[:, cols].astype(f32)).astype(bf16)

    row = pl.BlockSpec((tm, D_MODEL), lambda i: (i, 0))
    wide = pl.BlockSpec((tm, D_FF), lambda i: (i, 0))
    return _pallas(
        body, ride=ride, name=tag + "_bwd_act", grid=(L // tm,),
        in_specs=[row, _resident((D_FF, D_MODEL)), wide, wide],
        out_specs=[wide, wide, row],
        out_shape=[S((L, D_FF), bf16), S((L, D_FF), bf16), S((L, D_MODEL), bf16)],
        compiler_params=_cp("parallel"),
    )(dxo, wd, dadg, dadu)


def _ffn_bwd_in(dxo, x, g, dgate, dup, wg, wu, tm, name, tiles=None, into=None, ride=()):
    L = x.shape[0]
    first, count = tiles or (0, L // tm)

    def body(dxo_ref, x_ref, g_ref, dgate_ref, dup_ref, wg_ref, wu_ref, *rest):
        dx_ref, dg_ref = rest[-2:]

        @pl.when(pl.program_id(0) == 0)
        def _():
            dg_ref[...] = jnp.zeros_like(dg_ref)

        dh = _dot(dgate_ref[...], wg_ref[...]) + _dot(dup_ref[...], wu_ref[...])
        dx, dg = _rms_bwd(x_ref[...], g_ref[...], dh)
        dx_ref[...] = dxo_ref[...] + dx
        dg_ref[...] += dg

    row = pl.BlockSpec((tm, D_MODEL), lambda i: (first + i, 0))
    wide = pl.BlockSpec((tm, D_FF), lambda i: (first + i, 0))
    return _pallas(
        body, ride=ride, name=name, grid=(count,),
        in_specs=[row, row, _full((1, D_MODEL)), wide, wide, _resident((D_FF, D_MODEL)), _resident((D_FF, D_MODEL))]
        + [ANY] * (into is not None),
        out_specs=[row, _full((1, D_MODEL))],
        out_shape=[S((L, D_MODEL), f32), S((1, D_MODEL), f32)],
        input_output_aliases={7: 0} if into is not None else {},
        compiler_params=_cp("arbitrary"),
    )(dxo, x, g, dgate, dup, wg, wu, *([into] if into is not None else []))


def _mm_tn(a, b, out_dtype, name, tm=512, tn=1024, ride=()):
    L, M = a.shape
    N = b.shape[1]
    tm, tn = min(tm, M), min(tn, N)
    while M % tm:
        tm //= 2
    while N % tn:
        tn //= 2

    def body(a_ref, b_ref, o_ref):
        o_ref[...] = _dot(a_ref[...].astype(bf16), b_ref[...].astype(bf16), TN).astype(out_dtype)

    return _pallas(
        body, ride=ride, name=name, grid=(M // tm, N // tn),
        in_specs=[pl.BlockSpec((L, tm), lambda i, j: (0, i)), pl.BlockSpec((L, tn), lambda i, j: (0, j))],
        out_specs=pl.BlockSpec((tm, tn), lambda i, j: (i, j)),
        out_shape=S((M, N), out_dtype),
        compiler_params=_cp("parallel", "parallel"),
    )(a, b)


def _mix_in(x, g, w_in, tm):
    L = x.shape[0]

    def body(x_ref, g_ref, w_ref, h_ref, us_ref, v_ref):
        h = _rms(x_ref[...], g_ref[...]).astype(bf16)
        h_ref[...] = h
        u = _dot(h, w_ref[...], NT)
        us_ref[...] = u[:, :S5_WIDTH]
        v_ref[...] = u[:, S5_WIDTH:]

    row = lambda c: pl.BlockSpec((tm, c), lambda i: (i, 0))
    return pl.pallas_call(
        body, name="mix_in", grid=(L // tm,),
        in_specs=[row(D_MODEL), _full((1, D_MODEL)), _full((IN_COLS, D_MODEL))],
        out_specs=[row(D_MODEL), row(S5_WIDTH), row(2 * CONV_WIDTH)],
        out_shape=[S((L, D_MODEL), bf16), S((L, S5_WIDTH), f32), S((L, 2 * CONV_WIDTH), f32)],
        compiler_params=_cp("parallel"),
    )(x, g, w_in)


def _mix_in_bwd(dxo, x, g, du_s5, dv, w_in, tm):
    L = x.shape[0]

    def body(dxo_ref, x_ref, g_ref, dus_ref, dv_ref, w_ref, dx_ref, dg_ref, dub_ref, dxh_ref):
        @pl.when(pl.program_id(0) == 0)
        def _():
            dg_ref[...] = jnp.zeros_like(dg_ref)

        dus = dus_ref[...].astype(bf16)
        dvb = dv_ref[...].astype(bf16)
        dub_ref[:, :S5_WIDTH] = dus
        dub_ref[:, S5_WIDTH:] = dvb
        dh = _dot(dus, w_ref[:S5_WIDTH, :]) + _dot(dvb, w_ref[S5_WIDTH:, :])
        dx, dg = _rms_bwd(x_ref[...], g_ref[...], dh)
        dx = dxo_ref[...] + dx
        dx_ref[...] = dx
        dxh_ref[...] = (0.5 * dx).astype(bf16)
        dg_ref[...] += dg

    row = lambda c: pl.BlockSpec((tm, c), lambda i: (i, 0))
    return pl.pallas_call(
        body, name="mix_in_bwd", grid=(L // tm,),
        in_specs=[row(D_MODEL), row(D_MODEL), _full((1, D_MODEL)), row(S5_WIDTH), row(2 * CONV_WIDTH),
                  _full((IN_COLS, D_MODEL))],
        out_specs=[row(D_MODEL), _full((1, D_MODEL)), row(IN_COLS), row(D_MODEL)],
        out_shape=[S((L, D_MODEL), f32), S((1, D_MODEL), f32), S((L, IN_COLS), bf16), S((L, D_MODEL), bf16)],
        compiler_params=_cp("arbitrary"),
    )(dxo, x, g, du_s5, dv, w_in)


def _mix_out(x, y_s5, y_conv, w_out, tm):
    L = x.shape[0]

    def body(x_ref, ys_ref, yc_ref, w_ref, o_ref):
        o_ref[...] = x_ref[...] + _dot(ys_ref[...], w_ref[:S5_WIDTH, :]) + _dot(yc_ref[...], w_ref[S5_WIDTH:, :])

    row = lambda c: pl.BlockSpec((tm, c), lambda i: (i, 0))
    return pl.pallas_call(
        body, name="mix_out", grid=(L // tm,),
        in_specs=[row(D_MODEL), row(S5_WIDTH), row(CONV_WIDTH), _full((D_MODEL, D_MODEL))],
        out_specs=row(D_MODEL), out_shape=S((L, D_MODEL), f32),
        compiler_params=_cp("parallel"),
    )(x, y_s5, y_conv, w_out)


def _mix_out_bwd(dx, w_out, tm, ride=()):
    L = dx.shape[0]

    def body(dx_ref, w_ref, dys_ref, dyc_ref, dxb_ref):
        dxb = dx_ref[...].astype(bf16)
        dxb_ref[...] = dxb
        dys_ref[...] = _dot(dxb, w_ref[:S5_WIDTH, :], NT)
        dyc_ref[...] = _dot(dxb, w_ref[S5_WIDTH:, :], NT)

    row = lambda c: pl.BlockSpec((tm, c), lambda i: (i, 0))
    return _pallas(
        body, ride=ride, name="mix_out_bwd", grid=(L // tm,),
        in_specs=[row(D_MODEL), _full((D_MODEL, D_MODEL))],
        out_specs=[row(S5_WIDTH), row(CONV_WIDTH), row(D_MODEL)],
        out_shape=[S((L, S5_WIDTH), f32), S((L, CONV_WIDTH), f32), S((L, D_MODEL), bf16)],
        compiler_params=_cp("parallel"),
    )(dx, w_out)


def _s5_discretise(lam_re, lam_im, log_dt, b_re, b_im):
    dt = jnp.exp(log_dt)
    mag = jnp.exp(lam_re * dt)
    abar_re = mag * jnp.cos(lam_im * dt)
    abar_im = mag * jnp.sin(lam_im * dt)
    den = lam_re * lam_re + lam_im * lam_im
    num_re = abar_re - 1.0
    f_re = ((num_re * lam_re + abar_im * lam_im) / den)[:, None, :]
    f_im = ((abar_im * lam_re - num_re * lam_im) / den)[:, None, :]
    return abar_re, abar_im, f_re * b_re - f_im * b_im, f_re * b_im + f_im * b_re


def _s5_params(lam_re, lam_im, log_dt, b_re, b_im):
    def body(lr, li, ld, br, bi, ar_ref, ai_ref, bbr_ref, bbi_ref):
        ar, ai, bbr, bbi = _s5_discretise(lr[...], li[...], ld[...], br[...], bi[...])
        ar_ref[...], ai_ref[...], bbr_ref[...], bbi_ref[...] = ar, ai, bbr, bbi

    gp = S((S5_GROUPS, S5_STATE), f32)
    gcp = S((S5_GROUPS, S5_GROUP_CH, S5_STATE), f32)
    return pl.pallas_call(body, name="s5_params", out_shape=[gp, gp, gcp, gcp])(lam_re, lam_im, log_dt, b_re, b_im)


def _s5_params_bwd(lam_re, lam_im, log_dt, b_re, b_im, d_ar, d_ai, d_bbr, d_bbi):
    def body(lr, li, ld, br, bi, car, cai, cbr, cbi, o_lr, o_li, o_ld, o_br, o_bi):
        _, vjp = jax.vjp(_s5_discretise, lr[...], li[...], ld[...], br[...], bi[...])
        o_lr[...], o_li[...], o_ld[...], o_br[...], o_bi[...] = vjp((car[...], cai[...], cbr[...], cbi[...]))

    gp = S((S5_GROUPS, S5_STATE), f32)
    gcp = S((S5_GROUPS, S5_GROUP_CH, S5_STATE), f32)
    return pl.pallas_call(body, name="s5_params_bwd", out_shape=[gp, gp, S((S5_GROUPS, 1), f32), gcp, gcp])(
        lam_re, lam_im, log_dt, b_re, b_im, d_ar, d_ai, d_bbr, d_bbi)


def _cmul(ar, ai, br, bi):
    return ar * br - ai * bi, ar * bi + ai * br


def _segment_starts(er, ei, ar, ai, steps, reverse):
    pr, pi = ar, ai
    n = 1
    while n < steps:
        pr, pi = _cmul(pr, pi, pr, pi)
        n *= 2
    assert n == steps
    row = lax.broadcasted_iota(jnp.int32, (SEGMENTS, SCAN_LANES), 0)
    hr = jnp.zeros((1, SCAN_LANES), f32)
    hi = jnp.zeros((1, SCAN_LANES), f32)
    out_r = jnp.zeros((SEGMENTS, SCAN_LANES), f32)
    out_i = jnp.zeros((SEGMENTS, SCAN_LANES), f32)
    order = range(SEGMENTS - 1, 0, -1) if reverse else range(0, SEGMENTS - 1)
    for r in order:
        qr, qi = _cmul(pr, pi, hr, hi)
        hr, hi = qr + er[r:r + 1, :], qi + ei[r:r + 1, :]
        nxt = r - 1 if reverse else r + 1
        out_r = jnp.where(row == nxt, hr, out_r)
        out_i = jnp.where(row == nxt, hi, out_i)
    return out_r, out_i


def _s5_read_bwd(dout, y_lin, u, d_skip, w_glu, b_glu, tm):
    L = u.shape[0]

    def body(do_ref, yl_ref, u_ref, d_ref, w_ref, b_ref, dyl_ref, du_ref, dd_ref, dw_ref, db_ref):
        @pl.when(pl.program_id(0) == 0)
        def _():
            dd_ref[...] = jnp.zeros_like(dd_ref)
            dw_ref[...] = jnp.zeros_like(dw_ref)
            db_ref[...] = jnp.zeros_like(db_ref)

        u, d, dout = u_ref[...], d_ref[...], do_ref[...]
        y, gelu_vjp = jax.vjp(_gelu, yl_ref[...] + d * u)
        yb = y.astype(bf16)
        sig = _sigmoid(_dot(yb, w_ref[...]) + b_ref[...])
        dz = dout * y * sig * (1.0 - sig)
        dzb = dz.astype(bf16)
        dy = dout * sig + _dot(dzb, w_ref[...], NT)
        (dyp,) = gelu_vjp(dy)
        dyl_ref[...] = dyp.astype(bf16)
        du_ref[...] = d * dyp
        dd_ref[...] += _rows8(dyp * u)
        db_ref[...] += _rows8(dz)
        dw_ref[...] += _dot(yb, dzb, TN)

    row = pl.BlockSpec((tm, S5_WIDTH), lambda i: (i, 0))
    vec = _full((1, S5_WIDTH))
    part = _full((8, S5_WIDTH))
    return pl.pallas_call(
        body, name="s5_read_bwd", grid=(L // tm,),
        in_specs=[row, row, row, vec, _full((S5_WIDTH, S5_WIDTH)), vec],
        out_specs=[row, row, part, _full((S5_WIDTH, S5_WIDTH)), part],
        out_shape=[S((L, S5_WIDTH), bf16), S((L, S5_WIDTH), f32), S((8, S5_WIDTH), f32),
                   S((S5_WIDTH, S5_WIDTH), f32), S((8, S5_WIDTH), f32)],
        compiler_params=_cp("arbitrary"),
    )(dout, y_lin, u, d_skip, w_glu, b_glu)


S5_CHUNK_CH = SCAN_LANES // S5_STATE * S5_GROUP_CH


def _s5_two_phase(L, bi):
    rows = bi * SEGMENTS
    nb = L // rows
    whole = pltpu.VMEM((L // SEGMENTS, SEGMENTS, SCAN_LANES), f32)
    mat = pl.BlockSpec((S5_CHUNK_CH, SCAN_LANES), lambda c, j: (c, c))
    vec = pl.BlockSpec((1, SCAN_LANES), lambda c, j: (0, c))
    tile = pl.BlockSpec((SEGMENTS, SCAN_LANES), lambda c, j: (0, c))
    return rows, nb, whole, mat, vec, tile


def _s5_forward(u, a_re, a_im, bb_re, bb_im, cc_re, cc_im, bi, ride=()):
    L = u.shape[0]
    rows, nb, whole, mat, vec, _ = _s5_two_phase(L, bi)

    def body(u_ref, ar_ref, ai_ref, br_ref, bi_ref, cr_ref, ci_ref, sr_ref, si_ref, yl_ref, hr_ref, hi_ref, dr_ref, di_ref):
        j = pl.program_id(1)
        ar = jnp.broadcast_to(ar_ref[...], (SEGMENTS, SCAN_LANES))
        ai = jnp.broadcast_to(ai_ref[...], (SEGMENTS, SCAN_LANES))

        @pl.when(j == 0)
        def _():
            hr_ref[...] = jnp.zeros_like(hr_ref)
            hi_ref[...] = jnp.zeros_like(hi_ref)

        @pl.when(j < nb)
        def _():
            base = j * bi
            ub = u_ref[...].astype(bf16)
            dr_ref[pl.ds(base, bi)] = _dot(ub, br_ref[...]).reshape(bi, SEGMENTS, SCAN_LANES)
            di_ref[pl.ds(base, bi)] = _dot(ub, bi_ref[...]).reshape(bi, SEGMENTS, SCAN_LANES)

            def step(i, c):
                pr, pi = _cmul(ar, ai, c[0], c[1])
                return pr + dr_ref[base + i], pi + di_ref[base + i]

            hr_ref[...], hi_ref[...] = lax.fori_loop(0, bi, step, (hr_ref[...], hi_ref[...]), unroll=4)

        @pl.when(j == nb - 1)
        def _():
            hr_ref[...], hi_ref[...] = _segment_starts(hr_ref[...], hi_ref[...], ar_ref[...], ai_ref[...], L // SEGMENTS, False)

        @pl.when(j >= nb)
        def _():
            base = (j - nb) * bi

            def step(i, c):
                pr, pi = _cmul(ar, ai, c[0], c[1])
                nr, nim = pr + dr_ref[base + i], pi + di_ref[base + i]
                dr_ref[base + i] = nr
                di_ref[base + i] = nim
                return nr, nim

            hr_ref[...], hi_ref[...] = lax.fori_loop(0, bi, step, (hr_ref[...], hi_ref[...]), unroll=4)
            sr = dr_ref[pl.ds(base, bi)].reshape(rows, SCAN_LANES).astype(bf16)
            si = di_ref[pl.ds(base, bi)].reshape(rows, SCAN_LANES).astype(bf16)
            sr_ref[...] = sr
            si_ref[...] = si
            yl_ref[...] = _dot(sr, cr_ref[...], NT) - _dot(si, ci_ref[...], NT)

    u_spec = pl.BlockSpec((rows, S5_CHUNK_CH), lambda c, j: (jnp.minimum(j, nb - 1), c))
    late = lambda width: pl.BlockSpec((rows, width), lambda c, j: (jnp.maximum(j - nb, 0), c))
    return _pallas(
        body, ride=ride, name="s5_forward", grid=(S5_LANES // SCAN_LANES, 2 * nb),
        in_specs=[u_spec, vec, vec, mat, mat, mat, mat],
        out_specs=[late(SCAN_LANES), late(SCAN_LANES), late(S5_CHUNK_CH)],
        out_shape=[S((L, S5_LANES), bf16)] * 2 + [S((L, S5_WIDTH), f32)],
        scratch_shapes=[pltpu.VMEM((SEGMENTS, SCAN_LANES), f32)] * 2 + [whole] * 2,
        compiler_params=_cp("parallel", "arbitrary"),
    )(u, a_re, a_im, bb_re, bb_im, cc_re, cc_im)


def _s5_backward(dy, u, du_skip, s_re, s_im, a_re, a_im, bb_re, bb_im, cc_re, cc_im, bi, ride=()):
    L = u.shape[0]
    rows, nb, whole, mat, vec, tile = _s5_two_phase(L, bi)
    per = rows // 16

    def body(dy_ref, u_ref, dus_ref, sr_ref, si_ref, pr_ref, pi_ref, lr_ref, li_ref, ar_ref, ai_ref, br_ref, bi_ref, cr_ref,
             ci_ref, du_ref, dar_ref, dai_ref, dbr_ref, dbi_ref, dcr_ref, dci_ref, hr_ref, hi_ref, gr_ref, gi_ref, fr_ref, fi_ref):
        j = pl.program_id(1)
        ar = jnp.broadcast_to(ar_ref[...], (SEGMENTS, SCAN_LANES))
        ai = jnp.broadcast_to(ai_ref[...], (SEGMENTS, SCAN_LANES))

        @pl.when(j == 0)
        def _():
            for ref in (hr_ref, hi_ref, dar_ref, dai_ref, dbr_ref, dbi_ref, dcr_ref, dci_ref):
                ref[...] = jnp.zeros_like(ref)

        @pl.when(j < nb)
        def _():
            base = (nb - 1 - j) * bi
            dy = dy_ref[...]
            gr_ref[pl.ds(base, bi)] = _dot(dy, cr_ref[...]).reshape(bi, SEGMENTS, SCAN_LANES)
            gi_ref[pl.ds(base, bi)] = (-_dot(dy, ci_ref[...])).reshape(bi, SEGMENTS, SCAN_LANES)

            def step(n, c):
                i = base + bi - 1 - n
                qr, qi = _cmul(ar, ai, c[0], c[1])
                return qr + gr_ref[i], qi + gi_ref[i]

            hr_ref[...], hi_ref[...] = lax.fori_loop(0, bi, step, (hr_ref[...], hi_ref[...]), unroll=4)

        @pl.when(j == nb - 1)
        def _():
            hr_ref[...], hi_ref[...] = _segment_starts(hr_ref[...], hi_ref[...], ar_ref[...], ai_ref[...], L // SEGMENTS, True)

        @pl.when(j >= nb)
        def _():
            blk = 2 * nb - 1 - j
            base = blk * bi
            sr, si = sr_ref[...], si_ref[...]
            fr_ref[...] = sr.astype(f32).reshape(bi, SEGMENTS, SCAN_LANES)
            fi_ref[...] = si.astype(f32).reshape(bi, SEGMENTS, SCAN_LANES)

            def step(n, c):
                i = bi - 1 - n
                gr, gi, accr, acci = c
                qr, qi = _cmul(ar, ai, gr, gi)
                gr, gi = qr + gr_ref[base + i], qi + gi_ref[base + i]
                gr_ref[base + i] = gr
                gi_ref[base + i] = gi
                pr, pi = fr_ref[i - 1], fi_ref[i - 1]
                return gr, gi, accr + (gr * pr + gi * pi), acci + (gi * pr - gr * pi)

            gr, gi, accr, acci = lax.fori_loop(0, bi - 1, step, (hr_ref[...], hi_ref[...], dar_ref[...], dai_ref[...]), unroll=3)
            qr, qi = _cmul(ar, ai, gr, gi)
            gr, gi = qr + gr_ref[base], qi + gi_ref[base]
            gr_ref[base] = gr
            gi_ref[base] = gi
            hr_ref[...], hi_ref[...] = gr, gi
            row = lax.broadcasted_iota(jnp.int32, (SEGMENTS, SCAN_LANES), 0)
            older = lambda ref: ref[...].astype(f32)[SEGMENTS:, :]
            wrap_r = jnp.where(row == 0, 0.0, pltpu.roll(older(lr_ref), 1, 0))
            wrap_i = jnp.where(row == 0, 0.0, pltpu.roll(older(li_ref), 1, 0))
            pr = jnp.where(blk == 0, wrap_r, older(pr_ref))
            pi = jnp.where(blk == 0, wrap_i, older(pi_ref))
            dar_ref[...] = accr + gr * pr + gi * pi
            dai_ref[...] = acci + gi * pr - gr * pi

            g_re = gr_ref[pl.ds(base, bi)].reshape(rows, SCAN_LANES).astype(bf16)
            g_im = gi_ref[pl.ds(base, bi)].reshape(rows, SCAN_LANES).astype(bf16)
            ub = u_ref[...].astype(bf16)
            dy = dy_ref[...]
            du_ref[...] = dus_ref[...] + _dot(g_re, br_ref[...], NT) + _dot(g_im, bi_ref[...], NT)
            dbr_ref[...] += _dot(ub, g_re, TN)
            dbi_ref[...] += _dot(ub, g_im, TN)
            dcr_ref[...] += _dot(dy, sr, TN)
            dci_ref[...] -= _dot(dy, si, TN)

    block = lambda c, j: jnp.where(j < nb, nb - 1 - j, 2 * nb - 1 - j)
    late_block = lambda c, j: jnp.minimum(2 * nb - 1 - j, nb - 1)
    both = pl.BlockSpec((rows, S5_CHUNK_CH), lambda c, j: (block(c, j), c))
    chan = pl.BlockSpec((rows, S5_CHUNK_CH), lambda c, j: (late_block(c, j), c))
    state = pl.BlockSpec((rows, SCAN_LANES), lambda c, j: (late_block(c, j), c))
    prev = pl.BlockSpec((16, SCAN_LANES), lambda c, j: (jnp.maximum(late_block(c, j) * per - 1, 0), c))
    last = pl.BlockSpec((16, SCAN_LANES), lambda c, j: (L // 16 - 1, c))
    grad = pl.BlockSpec((S5_CHUNK_CH, SCAN_LANES), lambda c, j: (c, 0))
    return _pallas(
        body, ride=ride, name="s5_backward", grid=(S5_LANES // SCAN_LANES, 2 * nb),
        in_specs=[both, chan, chan, state, state, prev, prev, last, last, vec, vec, mat, mat, mat, mat],
        out_specs=[chan, tile, tile, grad, grad, grad, grad],
        out_shape=[S((L, S5_WIDTH), f32)] + [S((SEGMENTS, S5_LANES), f32)] * 2 + [S((S5_WIDTH, SCAN_LANES), f32)] * 4,
        scratch_shapes=[pltpu.VMEM((SEGMENTS, SCAN_LANES), f32)] * 2 + [whole] * 2 + [pltpu.VMEM((bi, SEGMENTS, SCAN_LANES), f32)] * 2,
        compiler_params=_cp("parallel", "arbitrary"),
    )(dy, u, du_skip, s_re, s_im, s_re, s_im, s_re, s_im, a_re, a_im, bb_re, bb_im, cc_re, cc_im)


def _s5_gate(y_lin, u, d_skip, w_glu, b_glu, tm, ride=()):
    L = u.shape[0]

    def body(yl_ref, u_ref, d_ref, w_ref, b_ref, o_ref):
        y = _gelu(yl_ref[...] + d_ref[...] * u_ref[...])
        z = _dot(y.astype(bf16), w_ref[...]) + b_ref[...]
        o_ref[...] = (y * _sigmoid(z)).astype(bf16)

    row = pl.BlockSpec((tm, S5_WIDTH), lambda i: (i, 0))
    vec = _full((1, S5_WIDTH))
    return _pallas(
        body, ride=ride, name="s5_gate", grid=(L // tm,),
        in_specs=[row, row, vec, _full((S5_WIDTH, S5_WIDTH)), vec],
        out_specs=row, out_shape=S((L, S5_WIDTH), bf16),
        compiler_params=_cp("parallel"),
    )(y_lin, u, d_skip, w_glu, b_glu)


def _group_mean(x, avg):
    hi = x.astype(bf16)
    lo = (x - hi.astype(f32)).astype(bf16)
    return _dot(hi, avg) + _dot(lo, avg)


def _conv_act(zn, ln_g, ln_b):
    t = zn * ln_g + ln_b
    return t * _sigmoid(t)


def _glu_padded(v_ref, halo_ref, zpad_ref, tm):
    v = v_ref[...]
    vh = halo_ref[...]
    zh = vh[:, :CONV_WIDTH] * _sigmoid(vh[:, CONV_WIDTH:])
    zpad_ref[:CONV_HALO, :] = jnp.where(pl.program_id(0) > 0, zh, 0.0)
    zpad_ref[CONV_HALO:CONV_HALO + tm, :] = v[:, :CONV_WIDTH] * _sigmoid(v[:, CONV_WIDTH:])
    zpad_ref[CONV_HALO + tm:, :] = jnp.zeros((8, CONV_WIDTH), f32)


def _shifted(pad_ref, sh_ref, tm):
    for b in range(8):
        sh_ref[b] = pad_ref[pl.ds(b, tm + CONV_HALO), :]


def _window(sh_ref, r0, off, rows):
    return sh_ref[off % 8, pl.ds(pl.multiple_of(r0 + 8 * (off // 8), 8), rows), :]


def _tap_sum(w_ref, sh_ref, taps, out_ref, tm, bias):
    def chunk(c, carry):
        r0 = pl.multiple_of(c * CONV_ROWS, CONV_ROWS)
        acc = jnp.zeros((CONV_ROWS, CONV_WIDTH), f32) + bias
        for k, off in taps:
            acc = acc + w_ref[k:k + 1, :] * _window(sh_ref, r0, off, CONV_ROWS)
        out_ref[pl.ds(r0, CONV_ROWS), :] = acc
        return carry

    lax.fori_loop(0, tm // CONV_ROWS, chunk, 0)


FWD_TAPS = [(k, CONV_HALO - (CONV_K - 1) + k) for k in range(CONV_K)]
BWD_TAPS = [(k, CONV_K - 1 - k) for k in range(CONV_K)]


def _conv_specs(tm):
    per = tm // CONV_HALO
    vrow = pl.BlockSpec((tm, 2 * CONV_WIDTH), lambda i: (i, 0))
    vhalo = pl.BlockSpec((CONV_HALO, 2 * CONV_WIDTH), lambda i: (jnp.maximum(i * per - 1, 0), 0))
    return vrow, vhalo


def _conv_scratch(tm):
    return [pltpu.VMEM((tm + CONV_HALO + 8, CONV_WIDTH), f32), pltpu.VMEM((8, tm + CONV_HALO, CONV_WIDTH), f32)]


def _conv_fwd(v, w_dw, b_dw, ln_g, ln_b, avg, tm, ride=()):
    L = v.shape[0]

    def body(v_ref, halo_ref, w_ref, b_ref, g_ref, bb_ref, avg_ref, o_ref, zc_ref, zpad_ref, zs_ref):
        _glu_padded(v_ref, halo_ref, zpad_ref, tm)
        _shifted(zpad_ref, zs_ref, tm)
        _tap_sum(w_ref, zs_ref, FWD_TAPS, zc_ref, tm, b_ref[...])
        zc = zc_ref[...]
        xc = zc - _group_mean(zc, avg_ref[...])
        zn = xc * lax.rsqrt(_group_mean(xc * xc, avg_ref[...]) + EPS)
        o_ref[...] = _conv_act(zn, g_ref[...], bb_ref[...]).astype(bf16)

    vrow, vhalo = _conv_specs(tm)
    vec = _full((1, CONV_WIDTH))
    row = pl.BlockSpec((tm, CONV_WIDTH), lambda i: (i, 0))
    return _pallas(
        body, ride=ride, name="conv_fwd", grid=(L // tm,),
        in_specs=[vrow, vhalo, _full((CONV_HALO, CONV_WIDTH)), vec, vec, vec, _full((CONV_WIDTH, CONV_WIDTH))],
        out_specs=[row, row], out_shape=[S((L, CONV_WIDTH), bf16), S((L, CONV_WIDTH), f32)],
        scratch_shapes=_conv_scratch(tm),
        compiler_params=_cp("arbitrary"),
    )(v, v, w_dw, b_dw, ln_g, ln_b, avg)


def _conv_bwd_norm(dout, zc, ln_g, ln_b, avg, tm):
    L = zc.shape[0]

    def body(do_ref, zc_ref, g_ref, bb_ref, avg_ref, dzc_ref, dg_ref, db_ref, dbd_ref):
        @pl.when(pl.program_id(0) == 0)
        def _():
            dg_ref[...] = jnp.zeros_like(dg_ref)
            db_ref[...] = jnp.zeros_like(db_ref)
            dbd_ref[...] = jnp.zeros_like(dbd_ref)

        avg = avg_ref[...]
        zc = zc_ref[...]
        xc = zc - _group_mean(zc, avg)
        rstd = lax.rsqrt(_group_mean(xc * xc, avg) + EPS)
        xhat = xc * rstd
        _, act_vjp = jax.vjp(_conv_act, xhat, g_ref[...], bb_ref[...])
        dxhat, dg, db = act_vjp(do_ref[...])
        dzc = rstd * (dxhat - _group_mean(dxhat, avg) - xhat * _group_mean(dxhat * xhat, avg))
        dzc_ref[...] = dzc
        dg_ref[0:1, :] += dg
        db_ref[0:1, :] += db
        dbd_ref[...] += _rows8(dzc)

    vec = _full((1, CONV_WIDTH))
    row = pl.BlockSpec((tm, CONV_WIDTH), lambda i: (i, 0))
    part = _full((8, CONV_WIDTH))
    return pl.pallas_call(
        body, name="conv_bwd_norm", grid=(L // tm,),
        in_specs=[row, row, vec, vec, _full((CONV_WIDTH, CONV_WIDTH))],
        out_specs=[row, part, part, part],
        out_shape=[S((L, CONV_WIDTH), f32)] + [S((8, CONV_WIDTH), f32)] * 3,
        compiler_params=_cp("arbitrary"),
    )(dout, zc, ln_g, ln_b, avg)


def _conv_bwd_taps(dzc, v, w_dw, tm, ride=()):
    L = v.shape[0]
    nt = L // tm
    per = tm // CONV_HALO

    def body(d_ref, dn_ref, v_ref, halo_ref, w_ref, dv_ref, dw_ref, zpad_ref, zs_ref, dpad_ref, ds_ref, dz_ref):
        i = pl.program_id(0)

        @pl.when(i == 0)
        def _():
            dw_ref[...] = jnp.zeros_like(dw_ref)

        _glu_padded(v_ref, halo_ref, zpad_ref, tm)
        _shifted(zpad_ref, zs_ref, tm)
        dpad_ref[:tm, :] = d_ref[...]
        dpad_ref[tm:tm + CONV_HALO, :] = jnp.where(i < nt - 1, dn_ref[...], 0.0)
        dpad_ref[tm + CONV_HALO:, :] = jnp.zeros((8, CONV_WIDTH), f32)
        _shifted(dpad_ref, ds_ref, tm)
        _tap_sum(w_ref, ds_ref, BWD_TAPS, dz_ref, tm, 0.0)

        for first in range(0, CONV_K, 8):
            taps = FWD_TAPS[first:first + 8]

            def chunk(c, accs, taps=taps):
                r0 = pl.multiple_of(c * 8, 8)
                d = d_ref[pl.ds(r0, 8), :]
                return tuple(acc + d * _window(zs_ref, r0, off, 8) for acc, (_, off) in zip(accs, taps))

            accs = lax.fori_loop(0, tm // 8, chunk, tuple(jnp.zeros((8, CONV_WIDTH), f32) for _ in taps), unroll=2)
            for acc, (k, _) in zip(accs, taps):
                dw_ref[k] += acc

        dz = dz_ref[...]
        v = v_ref[...]
        sig = _sigmoid(v[:, CONV_WIDTH:])
        dv_ref[:, :CONV_WIDTH] = dz * sig
        dv_ref[:, CONV_WIDTH:] = dz * v[:, :CONV_WIDTH] * sig * (1.0 - sig)

    vrow, vhalo = _conv_specs(tm)
    row = pl.BlockSpec((tm, CONV_WIDTH), lambda i: (i, 0))
    nxt = pl.BlockSpec((CONV_HALO, CONV_WIDTH), lambda i: (jnp.minimum((i + 1) * per, nt * per - 1), 0))
    return _pallas(
        body, ride=ride, name="conv_bwd_taps", grid=(nt,),
        in_specs=[row, nxt, vrow, vhalo, _full((CONV_HALO, CONV_WIDTH))],
        out_specs=[vrow, _full((CONV_HALO, 8, CONV_WIDTH))],
        out_shape=[S((L, 2 * CONV_WIDTH), f32), S((CONV_HALO, 8, CONV_WIDTH), f32)],
        scratch_shapes=_conv_scratch(tm) * 2 + [pltpu.VMEM((tm, CONV_WIDTH), f32)],
        compiler_params=_cp("arbitrary"),
    )(dzc, dzc, v, v, w_dw)


def _to_segments(a):
    L, c = a.shape
    return a.reshape(SEGMENTS, L // SEGMENTS, c).transpose(1, 0, 2).reshape(L, c)


def _from_segments(a):
    L, c = a.shape
    return a.reshape(L // SEGMENTS, SEGMENTS, c).transpose(1, 0, 2).reshape(L, c)


def _block_diag(ms):
    n = len(ms)

    def body(*refs):
        for a in range(n):
            out = refs[n + a]
            out[...] = jnp.zeros_like(out)
            for g in range(S5_GROUPS):
                rows = slice(g * S5_GROUP_CH, (g + 1) * S5_GROUP_CH)
                out[rows, g * S5_STATE:(g + 1) * S5_STATE] = refs[a][rows, :].astype(bf16)

    return pl.pallas_call(body, name="s5_block_diag", out_shape=[S((S5_WIDTH, S5_LANES), bf16)] * n,
                          compiler_params=pltpu.CompilerParams(vmem_limit_bytes=VMEM_LIMIT))(
        *[m.reshape(S5_WIDTH, S5_STATE) for m in ms])


def _diag_blocks(ms):
    n = len(ms)
    per_chunk = SCAN_LANES // S5_STATE

    def body(*refs):
        for a in range(n):
            for g in range(S5_GROUPS):
                rows = slice(g * S5_GROUP_CH, (g + 1) * S5_GROUP_CH)
                at = g % per_chunk * S5_STATE
                refs[n + a][rows, :] = refs[a][rows, at:at + S5_STATE]

    out = pl.pallas_call(body, name="s5_diag_blocks", out_shape=[S((S5_WIDTH, S5_STATE), f32)] * n,
                         compiler_params=pltpu.CompilerParams(vmem_limit_bytes=VMEM_LIMIT))(*ms)
    return [o.reshape(S5_GROUPS, S5_GROUP_CH, S5_STATE) for o in out]


class _NoExchanges:
    def before(self, point):
        return ()

    def after(self, point):
        pass

    def alone(self, point):
        pass


def _ffn_block(x, p, tag, tm, sched, head=None):
    point = tag + "_up"
    h, dadg, dadu, a = _ffn_up(x, p[tag + "_norm"], p[tag + "_w_gate"], p[tag + "_w_up"], tm, tag, ride=sched.before(point))
    sched.after(point)
    if head is None:
        out = _ffn_down(x, a, p[tag + "_w_down"], tm, tag, ride=sched.before(tag + "_down"))
        sched.after(tag + "_down")
    else:
        out = _ffn_down_loss(x, a, p[tag + "_w_down"], *head, tm, tag)
    return out, (h, dadg, dadu, a)


def _ffn_block_bwd(dxo, x, p, tag, saved, tm, grads, sched, parts=1, dxh=None):
    h, dadg, dadu, a = saved

    def weight_grad(which, lhs, rhs):
        point = tag + "_dw_" + which
        grads[tag + "_w_" + which] = _mm_tn(lhs, rhs, bf16, point, ride=sched.before(point))
        sched.after(point)

    if dxh is not None:
        weight_grad("down", a, dxh)
    dgate, dup, own_dxh = _ffn_bwd_act(dxo, p[tag + "_w_down"], dadg, dadu, tm, tag, ride=sched.before(tag + "_bwd_act"))
    sched.after(tag + "_bwd_act")
    weight_grad("gate", dgate, h)
    weight_grad("up", dup, h)
    if dxh is None:
        weight_grad("down", a, own_dxh)
    tiles = x.shape[0] // tm
    dx, dgs = None, []
    for k in range(parts):
        point = tag + "_bwd_in" + ("_%d" % k) * (parts > 1)
        dx, dg = _ffn_bwd_in(dxo, x, p[tag + "_norm"], dgate, dup, p[tag + "_w_gate"], p[tag + "_w_up"], tm, point,
                             tiles=(k * tiles // parts, tiles // parts), into=dx, ride=sched.before(point))
        sched.after(point)
        dgs.append(dg)
    grads[tag + "_norm"] = functools.reduce(jnp.add, dgs)
    return dx


def _local_step(x, target, p, grads, sched):
    L = x.shape[0]
    tm = min(512, L // 2)
    ni = L // SEGMENTS
    bi = min(64, ni)

    def carried(point, fn, *args):
        out = fn(*args, ride=sched.before(point))
        sched.after(point)
        return out

    x1, saved1 = _ffn_block(x, p, "ffn1", tm, sched)

    h2, u_s5, v = _mix_in(x1, p["mix_norm"], p["w_in"], tm)
    s5_in = (p["s5_lam_re"], p["s5_lam_im"], p["s5_log_dt"].reshape(S5_GROUPS, 1), p["s5_b_re"], p["s5_b_im"])
    abar_re, abar_im, bbar_re, bbar_im = _s5_params(*s5_in)
    a_re, a_im = abar_re.reshape(1, S5_LANES), abar_im.reshape(1, S5_LANES)
    bb_re, bb_im, cc_re, cc_im = _block_diag([bbar_re, bbar_im, p["s5_c_re"], p["s5_c_im"]])
    u_seg = _to_segments(u_s5)
    s_re, s_im, y_lin = carried("s5_forward", _s5_forward, u_seg, a_re, a_im, bb_re, bb_im, cc_re, cc_im, bi)
    y_s5 = _from_segments(_s5_gate(y_lin, u_seg, p["s5_d"], p["s5_w_glu"], p["s5_b_glu"], tm))
    w_dw = jnp.pad(p["conv_w_dw"], ((0, CONV_HALO - CONV_K), (0, 0)))
    heads = jnp.arange(CONV_WIDTH) // CONV_HEAD
    avg = ((heads[:, None] == heads[None, :]).astype(f32) / CONV_HEAD).astype(bf16)
    y_conv, zc = carried("conv_fwd", _conv_fwd, v, w_dw, p["conv_b_dw"], p["conv_ln_g"], p["conv_ln_b"], avg, tm)
    x2 = _mix_out(x1, y_s5, y_conv, p["w_out"], tm)

    (dx3, grads["final_norm"], loss_terms), saved2 = _ffn_block(
        x2, p, "ffn2", tm, sched, head=(target, p["final_norm"].reshape(1, D_MODEL)))

    dx2 = _ffn_block_bwd(dx3, x2, p, "ffn2", saved2, tm, grads, sched)

    dy_s5, dy_conv, dx2b = carried("mix_out_bwd", _mix_out_bwd, dx2, p["w_out"], tm)
    grads["w_out"] = jnp.concatenate([_mm_tn(y_s5, dx2b, bf16, "dw_out_s5"), _mm_tn(y_conv, dx2b, bf16, "dw_out_conv")], axis=0)
    dy_lin, du_skip, dd8, grads["s5_w_glu"], dbg8 = _s5_read_bwd(
        _to_segments(dy_s5), y_lin, u_seg, p["s5_d"], p["s5_w_glu"], p["s5_b_glu"], tm)
    grads["s5_d"] = dd8.sum(axis=0, keepdims=True)
    grads["s5_b_glu"] = dbg8.sum(axis=0, keepdims=True)
    du_seg, da_re8, da_im8, dbb_re, dbb_im, dcc_re, dcc_im = carried(
        "s5_backward", _s5_backward, dy_lin, u_seg, du_skip, s_re, s_im, a_re, -a_im, bb_re, bb_im, cc_re, cc_im, bi)
    d_abar = lambda a8: a8.sum(axis=0).reshape(S5_GROUPS, S5_STATE)
    grads["s5_c_re"], grads["s5_c_im"], d_bbr, d_bbi = _diag_blocks([dcc_re, dcc_im, dbb_re, dbb_im])
    d_lr, d_li, d_ld, d_br, d_bi = _s5_params_bwd(*s5_in, d_abar(da_re8), d_abar(da_im8), d_bbr, d_bbi)
    grads["s5_lam_re"], grads["s5_lam_im"], grads["s5_log_dt"] = d_lr, d_li, d_ld.reshape(1, S5_GROUPS)
    grads["s5_b_re"], grads["s5_b_im"] = d_br, d_bi
    dzc, dlg8, dlb8, dbd8 = _conv_bwd_norm(dy_conv, zc, p["conv_ln_g"], p["conv_ln_b"], avg, tm)
    grads["conv_ln_g"] = dlg8.sum(axis=0, keepdims=True)
    grads["conv_ln_b"] = dlb8.sum(axis=0, keepdims=True)
    grads["conv_b_dw"] = dbd8.sum(axis=0, keepdims=True)
    dv, dw8 = carried("conv_bwd_taps", _conv_bwd_taps, dzc, v, w_dw, tm)
    grads["conv_w_dw"] = dw8.sum(axis=1)[:CONV_K]
    dx1, grads["mix_norm"], dub, dx1h = _mix_in_bwd(dx2, x1, p["mix_norm"], _from_segments(du_seg), dv, p["w_in"], tm)
    grads["loss_terms"] = loss_terms
    grads["w_in"] = carried("dw_in", _mm_tn, dub, h2, bf16, "dw_in")

    dx0 = _ffn_block_bwd(dx1, x, p, "ffn1", saved1, tm, grads, sched, parts=min(2, L // tm), dxh=dx1h)
    sched.alone("tail")
    return loss_terms, dx0


MESH = pl.DeviceIdType.MESH
ANY = pl.BlockSpec(memory_space=pl.ANY)


def _place():
    return lax.axis_index("x"), lax.axis_index("y"), lax.axis_index("c")


class _Exchange:
    def __init__(self, ins, out_shape, sems, start, finish):
        self.ins, self.out_shape, self.sems, self.start, self.finish = list(ins), list(out_shape), list(sems), start, finish
        self.out = None


def _pallas(body, *, ride=(), **kw):
    if not ride:
        return pl.pallas_call(body, **kw)

    def run(*args):
        out_shape = kw.get("out_shape", [])
        single = not isinstance(out_shape, (list, tuple))
        shapes = [out_shape] if single else list(out_shape)
        out_specs = [kw["out_specs"]] if single else list(kw.get("out_specs", []))
        grid = tuple(kw.get("grid", ()))
        scratch = list(kw.get("scratch_shapes", ()))
        n_in, n_out, n_scr = len(args), len(shapes), len(scratch)
        r_in = [len(e.ins) for e in ride]
        r_out = [len(e.out_shape) for e in ride]
        r_sem = [len(e.sems) for e in ride]

        def wrapped(*refs):
            own_in, refs = refs[:n_in], refs[n_in:]
            ex_in, refs = refs[:sum(r_in)], refs[sum(r_in):]
            own_out, refs = refs[:n_out], refs[n_out:]
            ex_out, refs = refs[:sum(r_out)], refs[sum(r_out):]
            own_scr, ex_sem = refs[:n_scr], refs[n_scr:]
            parts = []
            for e, ni, no, ns in zip(ride, r_in, r_out, r_sem):
                parts.append((e, ex_in[:ni], ex_out[:no], ex_sem[:ns]))
                ex_in, ex_out, ex_sem = ex_in[ni:], ex_out[no:], ex_sem[ns:]

            def at(step):
                def go():
                    for e, i, o, s in parts:
                        getattr(e, step)(i, o, s)
                if grid:
                    ids = [pl.program_id(d) for d in range(len(grid))]
                    when = [i == (0 if step == "start" else g - 1) for i, g in zip(ids, grid)]
                    pl.when(functools.reduce(lambda a, b: a & b, when))(go)
                else:
                    go()

            at("start")
            if body is not None:
                body(*own_in, *own_out, *own_scr)
            at("finish")

        outs = pl.pallas_call(
            wrapped, name=kw["name"], grid=grid,
            in_specs=list(kw.get("in_specs", [])) + [ANY] * sum(r_in),
            out_specs=out_specs + [ANY] * sum(r_out),
            out_shape=shapes + [s for e in ride for s in e.out_shape],
            scratch_shapes=scratch + [s for e in ride for s in e.sems],
            input_output_aliases=kw.get("input_output_aliases", {}),
            compiler_params=_cp(*["arbitrary"] * len(grid)),
        )(*args, *[a for e in ride for a in e.ins])
        own, rest = outs[:n_out], outs[n_out:]
        for e, no in zip(ride, r_out):
            e.out, rest = list(rest[:no]), rest[no:]
        return own[0] if single else own

    return run


def _exchange(ride, name):
    _pallas(None, ride=ride, name=name)()


def _gather(arrs):
    n = len(arrs)

    def copies(ins, outs, sems):
        send_sems, recv_sems, local_sems = sems
        x, y, c = _place()
        me, sibling = (x, y, c), (x, y, 1 - c)
        chips = [(1 - x, y), (x, 1 - y), (1 - x, 1 - y)]

        def place(a, block):
            return outs[a].at[block]

        def copy(a, k, block, to, src=None):
            px, py, pc = block
            dst = place(a, 4 * px + 2 * py + pc)
            return pltpu.make_async_remote_copy(
                src_ref=dst if src is None else src, dst_ref=dst, send_sem=send_sems.at[7 * a + k],
                recv_sem=recv_sems.at[7 * a + k], device_id=to, device_id_type=MESH)

        def own():
            local = [pltpu.make_async_copy(ins[a], place(a, 4 * x + 2 * y + c), local_sems.at[a]) for a in range(n)]
            remote = []
            for a in range(n):
                remote.append(copy(a, 0, me, sibling, src=ins[a]))
                remote += [copy(a, 1 + j, me, (*chip, c), src=ins[a]) for j, chip in enumerate(chips)]
            return local, remote

        return c, me, sibling, chips, copy, own

    def start(ins, outs, sems):
        local, remote = copies(ins, outs, sems)[-1]()
        for cp in local + remote:
            cp.start()

    def finish(ins, outs, sems):
        c, me, sibling, chips, copy, own = copies(ins, outs, sems)
        passed = []
        for j, chip in enumerate(chips):
            for a in range(n):
                copy(a, 1 + j, (*chip, c), me).wait_recv()
                passed.append(copy(a, 4 + j, (*chip, c), sibling))
                passed[-1].start()
        for a in range(n):
            copy(a, 0, sibling, me).wait_recv()
            for j, chip in enumerate(chips):
                copy(a, 4 + j, (*chip, 1 - c), me).wait_recv()
        local, remote = own()
        for cp in remote + passed:
            cp.wait_send()
        for cp in local:
            cp.wait()

    dma = pltpu.SemaphoreType.DMA
    shapes = [S((N_DEV, *a.shape), a.dtype) for a in arrs]
    return _Exchange(arrs, shapes, [dma((7 * n,)), dma((7 * n,)), dma((n,))], start, finish)


def _swap_with_sibling(gs):
    n = len(gs)

    def copies(ins, outs, sems):
        x, y, c = _place()
        return [pltpu.make_async_remote_copy(
            src_ref=ins[a].at[:, 1 - c], dst_ref=outs[a], send_sem=sems[0].at[a], recv_sem=sems[1].at[a],
            device_id=(x, y, 1 - c), device_id_type=MESH) for a in range(n)]

    def start(ins, outs, sems):
        for cp in copies(ins, outs, sems):
            cp.start()

    def finish(ins, outs, sems):
        for cp in copies(ins, outs, sems):
            cp.wait()

    dma = pltpu.SemaphoreType.DMA
    return _Exchange(gs, [S((N_CHIP, *g.shape[2:]), g.dtype) for g in gs], [dma((n,)), dma((n,))], start, finish)


def _swap_with_chips(ps):
    n = len(ps)

    def copies(ins, outs, sems):
        x, y, c = _place()
        q = 2 * x + y
        peers = [(x, 1 - y), (1 - x, y), (1 - x, 1 - y)]

        def copy(a, j, slot_from, slot_to):
            px, py = peers[j]
            return pltpu.make_async_remote_copy(
                src_ref=ins[a].at[slot_from], dst_ref=outs[a].at[slot_to], send_sem=sems[0].at[3 * a + j],
                recv_sem=sems[1].at[3 * a + j], device_id=(px, py, c), device_id_type=MESH)

        sends = lambda: [copy(a, j, 2 * peers[j][0] + peers[j][1], q) for a in range(n) for j in range(3)]
        lands = lambda: [copy(a, j, q, 2 * peers[j][0] + peers[j][1]) for a in range(n) for j in range(3)]
        return sends, lands

    def start(ins, outs, sems):
        for cp in copies(ins, outs, sems)[0]():
            cp.start()

    def finish(ins, outs, sems):
        sends, lands = copies(ins, outs, sems)
        for cp in lands():
            cp.wait_recv()
        for cp in sends():
            cp.wait_send()

    dma = pltpu.SemaphoreType.DMA
    return _Exchange(ps, [S(p.shape, p.dtype) for p in ps], [dma((3 * n,)), dma((3 * n,))], start, finish)


def _row_tile(rows, cols, itemsize):
    t = rows
    while t * cols * itemsize > (1 << 20) and t % 32 == 0:
        t //= 2
    return t


def _add_sibling(g4, st, core, name):
    _, R, C = st.shape
    tr = _row_tile(R, C, 1)

    def body(c_ref, g_ref, s_ref, o_ref):
        o_ref[...] = (g_ref[...].astype(f32) + s_ref[...].astype(f32)).astype(bf16)

    mine = pl.BlockSpec((None, None, tr, C), lambda q, i, c: (q, c[0], i, 0))
    return pl.pallas_call(
        body, name=name,
        grid_spec=pltpu.PrefetchScalarGridSpec(
            num_scalar_prefetch=1, grid=(N_CHIP, R // tr),
            in_specs=[mine,
                      pl.BlockSpec((None, tr, C), lambda q, i, c: (q, i, 0))],
            out_specs=pl.BlockSpec((None, tr, C), lambda q, i, c: (q, i, 0))),
        out_shape=S((N_CHIP, R, C), bf16),
        compiler_params=_cp("parallel", "parallel"),
    )(core, g4, st)


def _adamw(w, g, m, v):
    m = B1 * m + (1.0 - B1) * g
    v = B2 * v + (1.0 - B2) * (g * g)
    m_hat = m / (1.0 - B1 ** STEP)
    v_hat = v / (1.0 - B2 ** STEP)
    return -LR * (m_hat / (jnp.sqrt(v_hat) + ADAM_EPS) + WD * w), m, v


def _adam_sharded(w, m, v, part, got, slots, name):
    R, C = w.shape
    _, Rp, Cp = part.shape
    if Rp == R:
        tr = _row_tile(R, Cp, 4)
    else:
        tr = R // 2 if R % 32 == 0 else R

    def body(s_ref, w_ref, m_ref, v_ref, p_ref, a_ref, b_ref, c_ref, g_out, d_out, m_out, v_out):
        g = p_ref[...].astype(f32) + a_ref[...].astype(f32) + b_ref[...].astype(f32) + c_ref[...].astype(f32)
        g = g[:, :C]
        g_out[...] = g
        d_out[...], m_out[...], v_out[...] = _adamw(w_ref[...], g, m_ref[...], v_ref[...])

    shard = pl.BlockSpec((tr, C), lambda i, s: (i, 0))
    slot = lambda k: pl.BlockSpec((None, tr, Cp), lambda i, s: (s[k], i, 0))
    return pl.pallas_call(
        body, name=name,
        grid_spec=pltpu.PrefetchScalarGridSpec(
            num_scalar_prefetch=1, grid=(R // tr,),
            in_specs=[shard, shard, shard, slot(0), slot(1), slot(2), slot(3)],
            out_specs=[shard] * 4),
        out_shape=[S((R, C), f32)] * 4,
        compiler_params=_cp("parallel"),
    )(slots, w, m, v, part, got, got, got)


def _adam_replicated(items, loss_terms, name):
    n = len(items)
    has_loss = loss_terms is not None

    def total(ref):
        g = ref[0]
        for d in range(1, N_DEV):
            g = g + ref[d]
        return g

    def body(*refs):
        ins, outs = refs[:4 * n + has_loss], refs[4 * n + has_loss:]
        for i in range(n):
            w_ref, m_ref, v_ref, g_ref = ins[4 * i:4 * i + 4]
            g = total(g_ref)
            outs[4 * i][...] = g
            outs[4 * i + 1][...], outs[4 * i + 2][...], outs[4 * i + 3][...] = _adamw(w_ref[...], g, m_ref[...], v_ref[...])
        if has_loss:
            outs[-1][...] = jnp.sum(total(ins[-1]), keepdims=True)

    flat = [a for item in items for a in item] + ([loss_terms] if has_loss else [])
    shapes = [S(item[0].shape, f32) for item in items for _ in range(4)] + ([S((1, 1), f32)] if has_loss else [])
    out = pl.pallas_call(body, name=name, out_shape=shapes,
                         compiler_params=pltpu.CompilerParams(vmem_limit_bytes=VMEM_LIMIT))(*flat)
    return [out[4 * i:4 * i + 4] for i in range(n)], (out[-1] if has_loss else None)


WEIGHTS = ["ffn1_norm", "ffn1_w_gate", "ffn1_w_up", "ffn1_w_down", "mix_norm", "w_in", "s5_lam_re", "s5_lam_im", "s5_log_dt",
           "s5_b_re", "s5_b_im", "s5_c_re", "s5_c_im", "s5_d", "s5_w_glu", "s5_b_glu", "conv_w_dw", "conv_b_dw", "conv_ln_g",
           "conv_ln_b", "w_out", "ffn2_norm", "ffn2_w_gate", "ffn2_w_up", "ffn2_w_down", "final_norm"]
SHARDED = ["ffn1_w_gate", "ffn1_w_up", "ffn1_w_down", "w_in", "s5_w_glu", "conv_w_dw", "w_out", "ffn2_w_gate", "ffn2_w_up",
           "ffn2_w_down"]
REPLICATED = [n for n in WEIGHTS if n not in SHARDED]
TRANSPOSED = ["ffn1_w_gate", "ffn1_w_up", "ffn2_w_gate", "ffn2_w_up", "w_in"]


def _shard_to_wire(n, w):
    if n == "conv_w_dw":
        return jnp.pad(w, ((0, CONV_HALO - CONV_K), (0, 0)))
    return w.astype(bf16)


def _to_wire(shards, ride):
    names = list(shards)
    shapes = [jax.eval_shape(functools.partial(_shard_to_wire, n), shards[n]) for n in names]

    def body(*refs):
        for src, dst in zip(refs[:len(names)], refs[len(names):]):
            (r, c), (rp, cp) = src.shape, dst.shape
            dst[:r, :c] = src[...].astype(dst.dtype)
            if cp > c:
                dst[:, c:] = jnp.zeros((rp, cp - c), dst.dtype)
            if rp > r:
                dst[r:, :] = jnp.zeros((rp - r, cp), dst.dtype)

    out = _pallas(body, ride=ride, name="to_wire", out_shape=shapes, in_specs=[pl.BlockSpec(memory_space=pltpu.VMEM)] * len(names),
                  out_specs=[pl.BlockSpec(memory_space=pltpu.VMEM)] * len(names))(*[shards[n] for n in names])
    return dict(zip(names, out))


def _gathered_to_full(n, g):
    if n == "conv_w_dw":
        return g.transpose(1, 0, 2).reshape(CONV_HALO, CONV_WIDTH)[:CONV_K]
    return g.reshape(N_DEV * g.shape[1], g.shape[2])


def _grad_to_blocks(n, g):
    if n == "conv_w_dw":
        g = jnp.pad(g, ((0, CONV_HALO - CONV_K), (0, 0)))
        g = g.reshape(g.shape[0], N_DEV, g.shape[1] // N_DEV).transpose(1, 0, 2)
    else:
        g = g.reshape(N_DEV, g.shape[0] // N_DEV, g.shape[1])
    return g.astype(bf16).reshape(N_CHIP, 2, *g.shape[1:])


REPLICATED_LATE = ["ffn1_norm"]
REPLICATED_EARLY = [n for n in REPLICATED if n not in REPLICATED_LATE]

PLAN = {
    "start": [("gather", ["ffn1_w_gate", "ffn1_w_up"])],
    "ffn1_up": [("gather", ["ffn1_w_down", "w_in", "w_out", "s5_w_glu", "conv_w_dw"])],
    "s5_forward": [("gather", ["ffn2_w_gate", "ffn2_w_up"])],
    "conv_fwd": [("gather", ["ffn2_w_down"])],
    "ffn2_dw_up": [("sibling", ["ffn2_w_gate"])],
    "ffn2_dw_down": [("sibling", ["ffn2_w_up"])],
    "mix_out_bwd": [("sibling", ["ffn2_w_down"])],
    "s5_backward": [("chips", ["ffn2_w_gate", "ffn2_w_up"])],
    "conv_bwd_taps": [("chips", ["ffn2_w_down"])],
    "dw_in": [("replicated", REPLICATED_EARLY)],
    "ffn1_dw_down": [("sibling", ["w_in", "s5_w_glu", "conv_w_dw", "w_out"])],
    "ffn1_bwd_act": [("chips", ["w_in", "s5_w_glu", "conv_w_dw", "w_out"]), ("sibling", ["ffn1_w_down"])],
    "ffn1_dw_gate": [("chips", ["ffn1_w_down"])],
    "ffn1_dw_up": [("sibling", ["ffn1_w_gate"])],
    "ffn1_bwd_in_0": [("chips", ["ffn1_w_gate"]), ("sibling", ["ffn1_w_up"])],
    "ffn1_bwd_in_1": [("chips", ["ffn1_w_up"])],
    "tail": [("replicated", REPLICATED_LATE)],
}


class _Schedule:
    def __init__(self, wire, p, grads, core):
        self.wire, self.p, self.grads, self.core = wire, p, grads, core
        self.partial, self.reduced, self.everyone, self.pending = {}, {}, {}, []

    def before(self, point):
        assert not self.pending
        for kind, names in PLAN.get(point, ()):
            if kind == "gather":
                given = [self.wire[n] for n in names]
                ex = _gather(given)
            elif kind == "sibling":
                given = [_grad_to_blocks(n, self.grads[n]) for n in names]
                ex = _swap_with_sibling(given)
            elif kind == "chips":
                given = [self.partial.pop(n) for n in names]
                ex = _swap_with_chips(given)
            else:
                names = names + ["loss_terms"] * (names is REPLICATED_EARLY)
                given = [self.grads[n].reshape(self.p[n].shape) if n in self.p else self.grads[n] for n in names]
                ex = _gather(given)
            self.pending.append((kind, names, given, ex))
        return [ex for _, _, _, ex in self.pending]

    def after(self, point):
        for kind, names, given, ex in self.pending:
            if kind == "gather":
                for n, g in zip(names, ex.out):
                    self.p[n] = _gathered_to_full(n, g)
            elif kind == "sibling":
                for n, blocks, got in zip(names, given, ex.out):
                    self.partial[n] = _add_sibling(blocks, got, self.core, "reduce_add_" + n)
            elif kind == "chips":
                for n, part, got in zip(names, given, ex.out):
                    self.reduced[n] = (part, got)
            else:
                self.everyone.update(zip(names, ex.out))
        self.pending = []

    def alone(self, point):
        _exchange(self.before(point), point)
        self.after(point)


def kernel(x, ffn1_norm, ffn1_w_gate, ffn1_w_up, ffn1_w_down, mix_norm, w_in, s5_lam_re, s5_lam_im, s5_log_dt, s5_b_re, s5_b_im, s5_c_re, s5_c_im, s5_d, s5_w_glu, s5_b_glu, conv_w_dw, conv_b_dw, conv_ln_g, conv_ln_b, w_out, ffn2_norm, ffn2_w_gate, ffn2_w_up, ffn2_w_down, final_norm, loss_target, m_ffn1_norm, m_ffn1_w_gate, m_ffn1_w_up, m_ffn1_w_down, m_mix_norm, m_w_in, m_s5_lam_re, m_s5_lam_im, m_s5_log_dt, m_s5_b_re, m_s5_b_im, m_s5_c_re, m_s5_c_im, m_s5_d, m_s5_w_glu, m_s5_b_glu, m_conv_w_dw, m_conv_b_dw, m_conv_ln_g, m_conv_ln_b, m_w_out, m_ffn2_norm, m_ffn2_w_gate, m_ffn2_w_up, m_ffn2_w_down, m_final_norm, v_ffn1_norm, v_ffn1_w_gate, v_ffn1_w_up, v_ffn1_w_down, v_mix_norm, v_w_in, v_s5_lam_re, v_s5_lam_im, v_s5_log_dt, v_s5_b_re, v_s5_b_im, v_s5_c_re, v_s5_c_im, v_s5_d, v_s5_w_glu, v_s5_b_glu, v_conv_w_dw, v_conv_b_dw, v_conv_ln_g, v_conv_ln_b, v_w_out, v_ffn2_norm, v_ffn2_w_gate, v_ffn2_w_up, v_ffn2_w_down, v_final_norm):
    args = locals()
    w = {n: args[n] for n in WEIGHTS}
    m = {n: args["m_" + n] for n in WEIGHTS}
    v = {n: args["v_" + n] for n in WEIGHTS}
    xq, yq, cq = _place()
    q = 2 * xq + yq
    slots = jnp.stack([q, q ^ 1, q ^ 2, q ^ 3]).astype(jnp.int32)

    def shard2d(n, a):
        a = a.reshape(a.shape[-2:])
        return a.T if n in TRANSPOSED else a

    def view(n, a):
        if n.startswith("s5_b_") and a.ndim == 4:
            return a[0].transpose(0, 2, 1)
        return a[0] if a.ndim >= 3 else a.reshape(1, -1)

    def unview(n, a):
        return (a.transpose(0, 2, 1) if n.startswith("s5_b_") and a.ndim == 3 else a).reshape(w[n].shape)

    p = {n: view(n, w[n]) for n in REPLICATED}
    grads = {}
    first = PLAN["start"][0][1]
    wire = {n: _shard_to_wire(n, shard2d(n, w[n])) for n in first}
    sched = _Schedule(wire, p, grads, jnp.reshape(cq, (1,)).astype(jnp.int32))
    wire.update(_to_wire({n: shard2d(n, w[n]) for n in SHARDED if n not in first}, sched.before("start")))
    sched.after("start")
    _, dx = _local_step(x[0], loss_target[0], p, grads, sched)

    out = {}
    for n in SHARDED:
        part, got = sched.reduced[n]
        rows = got.shape[1] if n == "conv_w_dw" else shard2d(n, w[n]).shape[0]
        fit = lambda a: jnp.pad(shard2d(n, a), ((0, rows - shard2d(n, a).shape[0]), (0, 0)))
        res = _adam_sharded(fit(w[n]), fit(m[n]), fit(v[n]), part, got, slots, "adam_" + n)
        back = lambda r: r[:shard2d(n, w[n]).shape[0]]
        out[n] = [(back(r).T if n in TRANSPOSED else back(r)).reshape(w[n].shape) for r in res]

    for names in (REPLICATED_EARLY, REPLICATED_LATE):
        items = [(view(n, w[n]), view(n, m[n]), view(n, v[n]), sched.everyone[n]) for n in names]
        res, total = _adam_replicated(items, sched.everyone.get("loss_terms") if names is REPLICATED_EARLY else None,
                                      "adam_" + names[0])
        for n, r in zip(names, res):
            out[n] = [unview(n, a) for a in r]
        if total is not None:
            loss = total.reshape(())

    return (loss, dx.reshape(x.shape), *[out[n][0] for n in WEIGHTS], *[out[n][1] for n in WEIGHTS],
            *[out[n][2] for n in WEIGHTS], *[out[n][3] for n in WEIGHTS])
```

```python
import functools

import jax
import jax.numpy as jnp
from jax import lax
from jax.experimental import pallas as pl
from jax.experimental.pallas import tpu as pltpu

f32 = jnp.float32
bf16 = jnp.bfloat16
S = jax.ShapeDtypeStruct

N_DEV = 8
N_CHIP = 4
D_MODEL = 1024
D_FF = 2816
FF_CHUNKS = [(0, 768), (768, 1536), (1536, 2304), (2304, D_FF)]
S5_WIDTH = 512
S5_GROUPS = 32
S5_GROUP_CH = 16
S5_STATE = 64
S5_LANES = S5_GROUPS * S5_STATE
CONV_WIDTH = 512
CONV_K = 31
CONV_HALO = 32
CONV_HEAD = 64
CONV_ROWS = 32
IN_COLS = S5_WIDTH + 2 * CONV_WIDTH
SEGMENTS = 8
SCAN_LANES = 512
EPS = 1e-6
LR, B1, B2, ADAM_EPS, WD, STEP = 0.001, 0.9, 0.999, 1e-08, 0.01, 10
VMEM_LIMIT = 56 * 1024 * 1024

NN = (((1,), (0,)), ((), ()))
NT = (((1,), (1,)), ((), ()))
TN = (((0,), (0,)), ((), ()))


def _dot(a, b, dims=NN):
    return lax.dot_general(a, b, dims, preferred_element_type=f32)


def _cp(*sem):
    return pltpu.CompilerParams(dimension_semantics=sem, vmem_limit_bytes=VMEM_LIMIT)


def _rms(x, g):
    return x * lax.rsqrt(jnp.mean(x * x, axis=-1, keepdims=True) + EPS) * g


def _rms_bwd(x, g, dh):
    _, vjp = jax.vjp(_rms, x, g)
    return vjp(dh)


def _sigmoid(x):
    return 1.0 / (1.0 + jnp.exp(-x))


def _gelu(x):
    return 0.5 * x * (1.0 + jnp.tanh(0.7978845608028654 * (x + 0.044715 * x * x * x)))


def _rows8(x):
    t, c = x.shape
    return x.reshape(t // 8, 8, c).sum(axis=0)


def _full(shape):
    return pl.BlockSpec(shape, lambda *_: (0,) * len(shape))


def _resident(shape):
    return pl.BlockSpec(shape, lambda *_: (0,) * len(shape), pipeline_mode=pl.Buffered(1))


def _ffn_up(x, g, wg, wu, tm, tag, ride=()):
    L = x.shape[0]

    def body(x_ref, g_ref, wg_ref, wu_ref, h_ref, dadg_ref, dadu_ref, a_ref):
        h = _rms(x_ref[...], g_ref[...]).astype(bf16)
        h_ref[...] = h
        for lo, hi in FF_CHUNKS:
            cols = slice(lo, hi)
            gate =_dot(h, wg_ref[cols, :], NT)
            up = _dot(h, wu_ref[cols, :], NT)
            sig = _sigmoid(gate)
            silu = gate * sig
            dadg_ref[:, cols] = (up * (sig + silu * (1.0 - sig))).astype(bf16)
            dadu_ref[:, cols] = silu.astype(bf16)
            a_ref[:, cols] = (silu * up).astype(bf16)

    row = pl.BlockSpec((tm, D_MODEL), lambda i: (i, 0))
    wide = pl.BlockSpec((tm, D_FF), lambda i: (i, 0))
    return _pallas(
        body, ride=ride, name=tag + "_up", grid=(L // tm,),
        in_specs=[row, _full((1, D_MODEL)), _resident((D_FF, D_MODEL)), _resident((D_FF, D_MODEL))],
        out_specs=[row, wide, wide, wide],
        out_shape=[S((L, D_MODEL), bf16)] + [S((L, D_FF), bf16)] * 3,
        compiler_params=_cp("parallel"),
    )(x, g, wg, wu)


def _ffn_down(x, a, wd, tm, tag, ride=()):
    L = x.shape[0]

    def body(x_ref, a_ref, wd_ref, o_ref):
        o_ref[...] = x_ref[...] + 0.5 * _dot(a_ref[...], wd_ref[...])

    return _pallas(
        body, ride=ride, name=tag + "_down", grid=(L // tm,),
        in_specs=[pl.BlockSpec((tm, D_MODEL), lambda i: (i, 0)), pl.BlockSpec((tm, D_FF), lambda i: (i, 0)),
                  _resident((D_FF, D_MODEL))],
        out_specs=pl.BlockSpec((tm, D_MODEL), lambda i: (i, 0)),
        out_shape=S((L, D_MODEL), f32),
        compiler_params=_cp("parallel"),
    )(x, a, wd)


def _ffn_down_loss(x, a, wd, target, g, tm, tag):
    L = x.shape[0]

    def body(x_ref, a_ref, wd_ref, t_ref, g_ref, dx_ref, dg_ref, l_ref):
        @pl.when(pl.program_id(0) == 0)
        def _():
            dg_ref[...] = jnp.zeros_like(dg_ref)
            l_ref[...] = jnp.zeros_like(l_ref)

        xo = x_ref[...] + 0.5 * _dot(a_ref[...], wd_ref[...])
        g = g_ref[...]
        e = _rms(xo, g) - t_ref[...]
        l_ref[...] += _rows8(e * e) * (0.5 / D_MODEL)
        dx, dg = _rms_bwd(xo, g, e * (1.0 / D_MODEL))
        dx_ref[...] = dx
        dg_ref[...] += dg

    row = pl.BlockSpec((tm, D_MODEL), lambda i: (i, 0))
    return pl.pallas_call(
        body, name=tag + "_down_loss", grid=(L // tm,),
        in_specs=[row, pl.BlockSpec((tm, D_FF), lambda i: (i, 0)), _resident((D_FF, D_MODEL)), row, _full((1, D_MODEL))],
        out_specs=[row, _full((1, D_MODEL)), _full((8, D_MODEL))],
        out_shape=[S((L, D_MODEL), f32), S((1, D_MODEL), f32), S((8, D_MODEL), f32)],
        compiler_params=_cp("arbitrary"),
    )(x, a, wd, target, g)


def _ffn_bwd_act(dxo, wd, dadg, dadu, tm, tag, ride=()):
    L = dxo.shape[0]

    def body(dx_ref, wd_ref, dadg_ref, dadu_ref, dgate_ref, dup_ref, dxh_ref):
        dxh = (0.5 * dx_ref[...]).astype(bf16)
        dxh_ref[...] = dxh
        for lo, hi in FF_CHUNKS:
            cols = slice(lo, hi)
            da =_dot(dxh, wd_ref[cols, :], NT)
            dgate_ref[:, cols] = (da * dadg_ref[:, cols].astype(f32)).astype(bf16)
            dup_ref[:, cols] = (da * dadu_ref[:, cols].astype(f32)).astype(bf16)

    row = pl.BlockSpec((tm, D_MODEL), lambda i: (i, 0))
    wide = pl.BlockSpec((tm, D_FF), lambda i: (i, 0))
    return _pallas(
        body, ride=ride, name=tag + "_bwd_act", grid=(L // tm,),
        in_specs=[row, _resident((D_FF, D_MODEL)), wide, wide],
        out_specs=[wide, wide, row],
        out_shape=[S((L, D_FF), bf16), S((L, D_FF), bf16), S((L, D_MODEL), bf16)],
        compiler_params=_cp("parallel"),
    )(dxo, wd, dadg, dadu)


def _ffn_bwd_in(dxo, x, g, dgate, dup, wg, wu, tm, name, tiles=None, into=None, ride=()):
    L = x.shape[0]
    first, count = tiles or (0, L // tm)

    def body(dxo_ref, x_ref, g_ref, dgate_ref, dup_ref, wg_ref, wu_ref, *rest):
        dx_ref, dg_ref = rest[-2:]

        @pl.when(pl.program_id(0) == 0)
        def _():
            dg_ref[...] = jnp.zeros_like(dg_ref)

        dh = _dot(dgate_ref[...], wg_ref[...]) + _dot(dup_ref[...], wu_ref[...])
        dx, dg = _rms_bwd(x_ref[...], g_ref[...], dh)
        dx_ref[...] = dxo_ref[...] + dx
        dg_ref[...] += dg

    row = pl.BlockSpec((tm, D_MODEL), lambda i: (first + i, 0))
    wide = pl.BlockSpec((tm, D_FF), lambda i: (first + i, 0))
    return _pallas(
        body, ride=ride, name=name, grid=(count,),
        in_specs=[row, row, _full((1, D_MODEL)), wide, wide, _resident((D_FF, D_MODEL)), _resident((D_FF, D_MODEL))]
        + [ANY] * (into is not None),
        out_specs=[row, _full((1, D_MODEL))],
        out_shape=[S((L, D_MODEL), f32), S((1, D_MODEL), f32)],
        input_output_aliases={7: 0} if into is not None else {},
        compiler_params=_cp("arbitrary"),
    )(dxo, x, g, dgate, dup, wg, wu, *([into] if into is not None else []))


def _mm_tn(a, b, out_dtype, name, tm=512, tn=1024, ride=()):
    L, M = a.shape
    N = b.shape[1]
    tm, tn = min(tm, M), min(tn, N)
    while M % tm:
        tm //= 2
    while N % tn:
        tn //= 2

    def body(a_ref, b_ref, o_ref):
        o_ref[...] = _dot(a_ref[...].astype(bf16), b_ref[...].astype(bf16), TN).astype(out_dtype)

    return _pallas(
        body, ride=ride, name=name, grid=(M // tm, N // tn),
        in_specs=[pl.BlockSpec((L, tm), lambda i, j: (0, i)), pl.BlockSpec((L, tn), lambda i, j: (0, j))],
        out_specs=pl.BlockSpec((tm, tn), lambda i, j: (i, j)),
        out_shape=S((M, N), out_dtype),
        compiler_params=_cp("parallel", "parallel"),
    )(a, b)


def _mix_in(x, g, w_in, tm):
    L = x.shape[0]

    def body(x_ref, g_ref, w_ref, h_ref, us_ref, v_ref):
        h = _rms(x_ref[...], g_ref[...]).astype(bf16)
        h_ref[...] = h
        u = _dot(h, w_ref[...], NT)
        us_ref[...] = u[:, :S5_WIDTH]
        v_ref[...] = u[:, S5_WIDTH:]

    row = lambda c: pl.BlockSpec((tm, c), lambda i: (i, 0))
    return pl.pallas_call(
        body, name="mix_in", grid=(L // tm,),
        in_specs=[row(D_MODEL), _full((1, D_MODEL)), _full((IN_COLS, D_MODEL))],
        out_specs=[row(D_MODEL), row(S5_WIDTH), row(2 * CONV_WIDTH)],
        out_shape=[S((L, D_MODEL), bf16), S((L, S5_WIDTH), f32), S((L, 2 * CONV_WIDTH), f32)],
        compiler_params=_cp("parallel"),
    )(x, g, w_in)


def _mix_in_bwd(dxo, x, g, du_s5, dv, w_in, tm):
    L = x.shape[0]

    def body(dxo_ref, x_ref, g_ref, dus_ref, dv_ref, w_ref, dx_ref, dg_ref, dub_ref, dxh_ref):
        @pl.when(pl.program_id(0) == 0)
        def _():
            dg_ref[...] = jnp.zeros_like(dg_ref)

        dus = dus_ref[...].astype(bf16)
        dvb = dv_ref[...].astype(bf16)
        dub_ref[:, :S5_WIDTH] = dus
        dub_ref[:, S5_WIDTH:] = dvb
        dh = _dot(dus, w_ref[:S5_WIDTH, :]) + _dot(dvb, w_ref[S5_WIDTH:, :])
        dx, dg = _rms_bwd(x_ref[...], g_ref[...], dh)
        dx = dxo_ref[...] + dx
        dx_ref[...] = dx
        dxh_ref[...] = (0.5 * dx).astype(bf16)
        dg_ref[...] += dg

    row = lambda c: pl.BlockSpec((tm, c), lambda i: (i, 0))
    return pl.pallas_call(
        body, name="mix_in_bwd", grid=(L // tm,),
        in_specs=[row(D_MODEL), row(D_MODEL), _full((1, D_MODEL)), row(S5_WIDTH), row(2 * CONV_WIDTH),
                  _full((IN_COLS, D_MODEL))],
        out_specs=[row(D_MODEL), _full((1, D_MODEL)), row(IN_COLS), row(D_MODEL)],
        out_shape=[S((L, D_MODEL), f32), S((1, D_MODEL), f32), S((L, IN_COLS), bf16), S((L, D_MODEL), bf16)],
        compiler_params=_cp("arbitrary"),
    )(dxo, x, g, du_s5, dv, w_in)


def _mix_out(x, y_s5, y_conv, w_out, tm):
    L = x.shape[0]

    def body(x_ref, ys_ref, yc_ref, w_ref, o_ref):
        o_ref[...] = x_ref[...] + _dot(ys_ref[...], w_ref[:S5_WIDTH, :]) + _dot(yc_ref[...], w_ref[S5_WIDTH:, :])

    row = lambda c: pl.BlockSpec((tm, c), lambda i: (i, 0))
    return pl.pallas_call(
        body, name="mix_out", grid=(L // tm,),
        in_specs=[row(D_MODEL), row(S5_WIDTH), row(CONV_WIDTH), _full((D_MODEL, D_MODEL))],
        out_specs=row(D_MODEL), out_shape=S((L, D_MODEL), f32),
        compiler_params=_cp("parallel"),
    )(x, y_s5, y_conv, w_out)


def _mix_out_bwd(dx, w_out, tm, ride=()):
    L = dx.shape[0]

    def body(dx_ref, w_ref, dys_ref, dyc_ref, dxb_ref):
        dxb = dx_ref[...].astype(bf16)
        dxb_ref[...] = dxb
        dys_ref[...] = _dot(dxb, w_ref[:S5_WIDTH, :], NT)
        dyc_ref[...] = _dot(dxb, w_ref[S5_WIDTH:, :], NT)

    row = lambda c: pl.BlockSpec((tm, c), lambda i: (i, 0))
    return _pallas(
        body, ride=ride, name="mix_out_bwd", grid=(L // tm,),
        in_specs=[row(D_MODEL), _full((D_MODEL, D_MODEL))],
        out_specs=[row(S5_WIDTH), row(CONV_WIDTH), row(D_MODEL)],
        out_shape=[S((L, S5_WIDTH), f32), S((L, CONV_WIDTH), f32), S((L, D_MODEL), bf16)],
        compiler_params=_cp("parallel"),
    )(dx, w_out)


def _s5_discretise(lam_re, lam_im, log_dt, b_re, b_im):
    dt = jnp.exp(log_dt)
    mag = jnp.exp(lam_re * dt)
    abar_re = mag * jnp.cos(lam_im * dt)
    abar_im = mag * jnp.sin(lam_im * dt)
    den = lam_re * lam_re + lam_im * lam_im
    num_re = abar_re - 1.0
    f_re = ((num_re * lam_re + abar_im * lam_im) / den)[:, None, :]
    f_im = ((abar_im * lam_re - num_re * lam_im) / den)[:, None, :]
    return abar_re, abar_im, f_re * b_re - f_im * b_im, f_re * b_im + f_im * b_re


def _s5_params(lam_re, lam_im, log_dt, b_re, b_im):
    def body(lr, li, ld, br, bi, ar_ref, ai_ref, bbr_ref, bbi_ref):
        ar, ai, bbr, bbi = _s5_discretise(lr[...], li[...], ld[...], br[...], bi[...])
        ar_ref[...], ai_ref[...], bbr_ref[...], bbi_ref[...] = ar, ai, bbr, bbi

    gp = S((S5_GROUPS, S5_STATE), f32)
    gcp = S((S5_GROUPS, S5_GROUP_CH, S5_STATE), f32)
    return pl.pallas_call(body, name="s5_params", out_shape=[gp, gp, gcp, gcp])(lam_re, lam_im, log_dt, b_re, b_im)


def _s5_params_bwd(lam_re, lam_im, log_dt, b_re, b_im, d_ar, d_ai, d_bbr, d_bbi):
    def body(lr, li, ld, br, bi, car, cai, cbr, cbi, o_lr, o_li, o_ld, o_br, o_bi):
        _, vjp = jax.vjp(_s5_discretise, lr[...], li[...], ld[...], br[...], bi[...])
        o_lr[...], o_li[...], o_ld[...], o_br[...], o_bi[...] = vjp((car[...], cai[...], cbr[...], cbi[...]))

    gp = S((S5_GROUPS, S5_STATE), f32)
    gcp = S((S5_GROUPS, S5_GROUP_CH, S5_STATE), f32)
    return pl.pallas_call(body, name="s5_params_bwd", out_shape=[gp, gp, S((S5_GROUPS, 1), f32), gcp, gcp])(
        lam_re, lam_im, log_dt, b_re, b_im, d_ar, d_ai, d_bbr, d_bbi)


def _cmul(ar, ai, br, bi):
    return ar * br - ai * bi, ar * bi + ai * br


def _segment_starts(er, ei, ar, ai, steps, reverse):
    pr, pi = ar, ai
    n = 1
    while n < steps:
        pr, pi = _cmul(pr, pi, pr, pi)
        n *= 2
    assert n == steps
    row = lax.broadcasted_iota(jnp.int32, (SEGMENTS, SCAN_LANES), 0)
    hr = jnp.zeros((1, SCAN_LANES), f32)
    hi = jnp.zeros((1, SCAN_LANES), f32)
    out_r = jnp.zeros((SEGMENTS, SCAN_LANES), f32)
    out_i = jnp.zeros((SEGMENTS, SCAN_LANES), f32)
    order = range(SEGMENTS - 1, 0, -1) if reverse else range(0, SEGMENTS - 1)
    for r in order:
        qr, qi = _cmul(pr, pi, hr, hi)
        hr, hi = qr + er[r:r + 1, :], qi + ei[r:r + 1, :]
        nxt = r - 1 if reverse else r + 1
        out_r = jnp.where(row == nxt, hr, out_r)
        out_i = jnp.where(row == nxt, hi, out_i)
    return out_r, out_i


def _s5_read_bwd(dout, y_lin, u, d_skip, w_glu, b_glu, tm):
    L = u.shape[0]

    def body(do_ref, yl_ref, u_ref, d_ref, w_ref, b_ref, dyl_ref, du_ref, dd_ref, dw_ref, db_ref):
        @pl.when(pl.program_id(0) == 0)
        def _():
            dd_ref[...] = jnp.zeros_like(dd_ref)
            dw_ref[...] = jnp.zeros_like(dw_ref)
            db_ref[...] = jnp.zeros_like(db_ref)

        u, d, dout = u_ref[...], d_ref[...], do_ref[...]
        y, gelu_vjp = jax.vjp(_gelu, yl_ref[...] + d * u)
        yb = y.astype(bf16)
        sig = _sigmoid(_dot(yb, w_ref[...]) + b_ref[...])
        dz = dout * y * sig * (1.0 - sig)
        dzb = dz.astype(bf16)
        dy = dout * sig + _dot(dzb, w_ref[...], NT)
        (dyp,) = gelu_vjp(dy)
        dyl_ref[...] = dyp.astype(bf16)
        du_ref[...] = d * dyp
        dd_ref[...] += _rows8(dyp * u)
        db_ref[...] += _rows8(dz)
        dw_ref[...] += _dot(yb, dzb, TN)

    row = pl.BlockSpec((tm, S5_WIDTH), lambda i: (i, 0))
    vec = _full((1, S5_WIDTH))
    part = _full((8, S5_WIDTH))
    return pl.pallas_call(
        body, name="s5_read_bwd", grid=(L // tm,),
        in_specs=[row, row, row, vec, _full((S5_WIDTH, S5_WIDTH)), vec],
        out_specs=[row, row, part, _full((S5_WIDTH, S5_WIDTH)), part],
        out_shape=[S((L, S5_WIDTH), bf16), S((L, S5_WIDTH), f32), S((8, S5_WIDTH), f32),
                   S((S5_WIDTH, S5_WIDTH), f32), S((8, S5_WIDTH), f32)],
        compiler_params=_cp("arbitrary"),
    )(dout, y_lin, u, d_skip, w_glu, b_glu)


S5_CHUNK_CH = SCAN_LANES // S5_STATE * S5_GROUP_CH


def _s5_two_phase(L, bi):
    rows = bi * SEGMENTS
    nb = L // rows
    whole = pltpu.VMEM((L // SEGMENTS, SEGMENTS, SCAN_LANES), f32)
    mat = pl.BlockSpec((S5_CHUNK_CH, SCAN_LANES), lambda c, j: (c, c))
    vec = pl.BlockSpec((1, SCAN_LANES), lambda c, j: (0, c))
    tile = pl.BlockSpec((SEGMENTS, SCAN_LANES), lambda c, j: (0, c))
    return rows, nb, whole, mat, vec, tile


def _s5_forward(u, a_re, a_im, bb_re, bb_im, cc_re, cc_im, bi, ride=()):
    L = u.shape[0]
    rows, nb, whole, mat, vec, _ = _s5_two_phase(L, bi)

    def body(u_ref, ar_ref, ai_ref, br_ref, bi_ref, cr_ref, ci_ref, sr_ref, si_ref, yl_ref, hr_ref, hi_ref, dr_ref, di_ref):
        j = pl.program_id(1)
        ar = jnp.broadcast_to(ar_ref[...], (SEGMENTS, SCAN_LANES))
        ai = jnp.broadcast_to(ai_ref[...], (SEGMENTS, SCAN_LANES))

        @pl.when(j == 0)
        def _():
            hr_ref[...] = jnp.zeros_like(hr_ref)
            hi_ref[...] = jnp.zeros_like(hi_ref)

        @pl.when(j < nb)
        def _():
            base = j * bi
            ub = u_ref[...].astype(bf16)
            dr_ref[pl.ds(base, bi)] = _dot(ub, br_ref[...]).reshape(bi, SEGMENTS, SCAN_LANES)
            di_ref[pl.ds(base, bi)] = _dot(ub, bi_ref[...]).reshape(bi, SEGMENTS, SCAN_LANES)

            def step(i, c):
                pr, pi = _cmul(ar, ai, c[0], c[1])
                return pr + dr_ref[base + i], pi + di_ref[base + i]

            hr_ref[...], hi_ref[...] = lax.fori_loop(0, bi, step, (hr_ref[...], hi_ref[...]), unroll=4)

        @pl.when(j == nb - 1)
        def _():
            hr_ref[...], hi_ref[...] = _segment_starts(hr_ref[...], hi_ref[...], ar_ref[...], ai_ref[...], L // SEGMENTS, False)

        @pl.when(j >= nb)
        def _():
            base = (j - nb) * bi

            def step(i, c):
                pr, pi = _cmul(ar, ai, c[0], c[1])
                nr, nim = pr + dr_ref[base + i], pi + di_ref[base + i]
                dr_ref[base + i] = nr
                di_ref[base + i] = nim
                return nr, nim

            hr_ref[...], hi_ref[...] = lax.fori_loop(0, bi, step, (hr_ref[...], hi_ref[...]), unroll=4)
            sr = dr_ref[pl.ds(base, bi)].reshape(rows, SCAN_LANES).astype(bf16)
            si = di_ref[pl.ds(base, bi)].reshape(rows, SCAN_LANES).astype(bf16)
            sr_ref[...] = sr
            si_ref[...] = si
            yl_ref[...] = _dot(sr, cr_ref[...], NT) - _dot(si, ci_ref[...], NT)

    u_spec = pl.BlockSpec((rows, S5_CHUNK_CH), lambda c, j: (jnp.minimum(j, nb - 1), c))
    late = lambda width: pl.BlockSpec((rows, width), lambda c, j: (jnp.maximum(j - nb, 0), c))
    return _pallas(
        body, ride=ride, name="s5_forward", grid=(S5_LANES // SCAN_LANES, 2 * nb),
        in_specs=[u_spec, vec, vec, mat, mat, mat, mat],
        out_specs=[late(SCAN_LANES), late(SCAN_LANES), late(S5_CHUNK_CH)],
        out_shape=[S((L, S5_LANES), bf16)] * 2 + [S((L, S5_WIDTH), f32)],
        scratch_shapes=[pltpu.VMEM((SEGMENTS, SCAN_LANES), f32)] * 2 + [whole] * 2,
        compiler_params=_cp("parallel", "arbitrary"),
    )(u, a_re, a_im, bb_re, bb_im, cc_re, cc_im)


def _s5_backward(dy, u, du_skip, s_re, s_im, a_re, a_im, bb_re, bb_im, cc_re, cc_im, bi, ride=()):
    L = u.shape[0]
    rows, nb, whole, mat, vec, tile = _s5_two_phase(L, bi)
    per = rows // 16

    def body(dy_ref, u_ref, dus_ref, sr_ref, si_ref, pr_ref, pi_ref, lr_ref, li_ref, ar_ref, ai_ref, br_ref, bi_ref, cr_ref,
             ci_ref, du_ref, dar_ref, dai_ref, dbr_ref, dbi_ref, dcr_ref, dci_ref, hr_ref, hi_ref, gr_ref, gi_ref, fr_ref, fi_ref):
        j = pl.program_id(1)
        ar = jnp.broadcast_to(ar_ref[...], (SEGMENTS, SCAN_LANES))
        ai = jnp.broadcast_to(ai_ref[...], (SEGMENTS, SCAN_LANES))

        @pl.when(j == 0)
        def _():
            for ref in (hr_ref, hi_ref, dar_ref, dai_ref, dbr_ref, dbi_ref, dcr_ref, dci_ref):
                ref[...] = jnp.zeros_like(ref)

        @pl.when(j < nb)
        def _():
            base = (nb - 1 - j) * bi
            dy = dy_ref[...]
            gr_ref[pl.ds(base, bi)] = _dot(dy, cr_ref[...]).reshape(bi, SEGMENTS, SCAN_LANES)
            gi_ref[pl.ds(base, bi)] = (-_dot(dy, ci_ref[...])).reshape(bi, SEGMENTS, SCAN_LANES)

            def step(n, c):
                i = base + bi - 1 - n
                qr, qi = _cmul(ar, ai, c[0], c[1])
                return qr + gr_ref[i], qi + gi_ref[i]

            hr_ref[...], hi_ref[...] = lax.fori_loop(0, bi, step, (hr_ref[...], hi_ref[...]), unroll=4)

        @pl.when(j == nb - 1)
        def _():
            hr_ref[...], hi_ref[...] = _segment_starts(hr_ref[...], hi_ref[...], ar_ref[...], ai_ref[...], L // SEGMENTS, True)

        @pl.when(j >= nb)
        def _():
            blk = 2 * nb - 1 - j
            base = blk * bi
            sr, si = sr_ref[...], si_ref[...]
            fr_ref[...] = sr.astype(f32).reshape(bi, SEGMENTS, SCAN_LANES)
            fi_ref[...] = si.astype(f32).reshape(bi, SEGMENTS, SCAN_LANES)

            def step(n, c):
                i = bi - 1 - n
                gr, gi, accr, acci = c
                qr, qi = _cmul(ar, ai, gr, gi)
                gr, gi = qr + gr_ref[base + i], qi + gi_ref[base + i]
                gr_ref[base + i] = gr
                gi_ref[base + i] = gi
                pr, pi = fr_ref[i - 1], fi_ref[i - 1]
                return gr, gi, accr + (gr * pr + gi * pi), acci + (gi * pr - gr * pi)

            gr, gi, accr, acci = lax.fori_loop(0, bi - 1, step, (hr_ref[...], hi_ref[...], dar_ref[...], dai_ref[...]), unroll=3)
            qr, qi = _cmul(ar, ai, gr, gi)
            gr, gi = qr + gr_ref[base], qi + gi_ref[base]
            gr_ref[base] = gr
            gi_ref[base] = gi
            hr_ref[...], hi_ref[...] = gr, gi
            row = lax.broadcasted_iota(jnp.int32, (SEGMENTS, SCAN_LANES), 0)
            older = lambda ref: ref[...].astype(f32)[SEGMENTS:, :]
            wrap_r = jnp.where(row == 0, 0.0, pltpu.roll(older(lr_ref), 1, 0))
            wrap_i = jnp.where(row == 0, 0.0, pltpu.roll(older(li_ref), 1, 0))
            pr = jnp.where(blk == 0, wrap_r, older(pr_ref))
            pi = jnp.where(blk == 0, wrap_i, older(pi_ref))
            dar_ref[...] = accr + gr * pr + gi * pi
            dai_ref[...] = acci + gi * pr - gr * pi

            g_re = gr_ref[pl.ds(base, bi)].reshape(rows, SCAN_LANES).astype(bf16)
            g_im = gi_ref[pl.ds(base, bi)].reshape(rows, SCAN_LANES).astype(bf16)
            ub = u_ref[...].astype(bf16)
            dy = dy_ref[...]
            du_ref[...] = dus_ref[...] + _dot(g_re, br_ref[...], NT) + _dot(g_im, bi_ref[...], NT)
            dbr_ref[...] += _dot(ub, g_re, TN)
            dbi_ref[...] += _dot(ub, g_im, TN)
            dcr_ref[...] += _dot(dy, sr, TN)
            dci_ref[...] -= _dot(dy, si, TN)

    block = lambda c, j: jnp.where(j < nb, nb - 1 - j, 2 * nb - 1 - j)
    late_block = lambda c, j: jnp.minimum(2 * nb - 1 - j, nb - 1)
    both = pl.BlockSpec((rows, S5_CHUNK_CH), lambda c, j: (block(c, j), c))
    chan = pl.BlockSpec((rows, S5_CHUNK_CH), lambda c, j: (late_block(c, j), c))
    state = pl.BlockSpec((rows, SCAN_LANES), lambda c, j: (late_block(c, j), c))
    prev = pl.BlockSpec((16, SCAN_LANES), lambda c, j: (jnp.maximum(late_block(c, j) * per - 1, 0), c))
    last = pl.BlockSpec((16, SCAN_LANES), lambda c, j: (L // 16 - 1, c))
    grad = pl.BlockSpec((S5_CHUNK_CH, SCAN_LANES), lambda c, j: (c, 0))
    return _pallas(
        body, ride=ride, name="s5_backward", grid=(S5_LANES // SCAN_LANES, 2 * nb),
        in_specs=[both, chan, chan, state, state, prev, prev, last, last, vec, vec, mat, mat, mat, mat],
        out_specs=[chan, tile, tile, grad, grad, grad, grad],
        out_shape=[S((L, S5_WIDTH), f32)] + [S((SEGMENTS, S5_LANES), f32)] * 2 + [S((S5_WIDTH, SCAN_LANES), f32)] * 4,
        scratch_shapes=[pltpu.VMEM((SEGMENTS, SCAN_LANES), f32)] * 2 + [whole] * 2 + [pltpu.VMEM((bi, SEGMENTS, SCAN_LANES), f32)] * 2,
        compiler_params=_cp("parallel", "arbitrary"),
    )(dy, u, du_skip, s_re, s_im, s_re, s_im, s_re, s_im, a_re, a_im, bb_re, bb_im, cc_re, cc_im)


def _s5_gate(y_lin, u, d_skip, w_glu, b_glu, tm, ride=()):
    L = u.shape[0]

    def body(yl_ref, u_ref, d_ref, w_ref, b_ref, o_ref):
        y = _gelu(yl_ref[...] + d_ref[...] * u_ref[...])
        z = _dot(y.astype(bf16), w_ref[...]) + b_ref[...]
        o_ref[...] = (y * _sigmoid(z)).astype(bf16)

    row = pl.BlockSpec((tm, S5_WIDTH), lambda i: (i, 0))
    vec = _full((1, S5_WIDTH))
    return _pallas(
        body, ride=ride, name="s5_gate", grid=(L // tm,),
        in_specs=[row, row, vec, _full((S5_WIDTH, S5_WIDTH)), vec],
        out_specs=row, out_shape=S((L, S5_WIDTH), bf16),
        compiler_params=_cp("parallel"),
    )(y_lin, u, d_skip, w_glu, b_glu)


def _group_mean(x, avg):
    hi = x.astype(bf16)
    lo = (x - hi.astype(f32)).astype(bf16)
    return _dot(hi, avg) + _dot(lo, avg)


def _conv_act(zn, ln_g, ln_b):
    t = zn * ln_g + ln_b
    return t * _sigmoid(t)


def _glu_padded(v_ref, halo_ref, zpad_ref, tm):
    v = v_ref[...]
    vh = halo_ref[...]
    zh = vh[:, :CONV_WIDTH] * _sigmoid(vh[:, CONV_WIDTH:])
    zpad_ref[:CONV_HALO, :] = jnp.where(pl.program_id(0) > 0, zh, 0.0)
    zpad_ref[CONV_HALO:CONV_HALO + tm, :] = v[:, :CONV_WIDTH] * _sigmoid(v[:, CONV_WIDTH:])
    zpad_ref[CONV_HALO + tm:, :] = jnp.zeros((8, CONV_WIDTH), f32)


def _shifted(pad_ref, sh_ref, tm):
    for b in range(8):
        sh_ref[b] = pad_ref[pl.ds(b, tm + CONV_HALO), :]


def _window(sh_ref, r0, off, rows):
    return sh_ref[off % 8, pl.ds(pl.multiple_of(r0 + 8 * (off // 8), 8), rows), :]


def _tap_sum(w_ref, sh_ref, taps, out_ref, tm, bias):
    def chunk(c, carry):
        r0 = pl.multiple_of(c * CONV_ROWS, CONV_ROWS)
        acc = jnp.zeros((CONV_ROWS, CONV_WIDTH), f32) + bias
        for k, off in taps:
            acc = acc + w_ref[k:k + 1, :] * _window(sh_ref, r0, off, CONV_ROWS)
        out_ref[pl.ds(r0, CONV_ROWS), :] = acc
        return carry

    lax.fori_loop(0, tm // CONV_ROWS, chunk, 0)


FWD_TAPS = [(k, CONV_HALO - (CONV_K - 1) + k) for k in range(CONV_K)]
BWD_TAPS = [(k, CONV_K - 1 - k) for k in range(CONV_K)]


def _conv_specs(tm):
    per = tm // CONV_HALO
    vrow = pl.BlockSpec((tm, 2 * CONV_WIDTH), lambda i: (i, 0))
    vhalo = pl.BlockSpec((CONV_HALO, 2 * CONV_WIDTH), lambda i: (jnp.maximum(i * per - 1, 0), 0))
    return vrow, vhalo


def _conv_scratch(tm):
    return [pltpu.VMEM((tm + CONV_HALO + 8, CONV_WIDTH), f32), pltpu.VMEM((8, tm + CONV_HALO, CONV_WIDTH), f32)]


def _conv_fwd(v, w_dw, b_dw, ln_g, ln_b, avg, tm, ride=()):
    L = v.shape[0]

    def body(v_ref, halo_ref, w_ref, b_ref, g_ref, bb_ref, avg_ref, o_ref, zc_ref, zpad_ref, zs_ref):
        _glu_padded(v_ref, halo_ref, zpad_ref, tm)
        _shifted(zpad_ref, zs_ref, tm)
        _tap_sum(w_ref, zs_ref, FWD_TAPS, zc_ref, tm, b_ref[...])
        zc = zc_ref[...]
        xc = zc - _group_mean(zc, avg_ref[...])
        zn = xc * lax.rsqrt(_group_mean(xc * xc, avg_ref[...]) + EPS)
        o_ref[...] = _conv_act(zn, g_ref[...], bb_ref[...]).astype(bf16)

    vrow, vhalo = _conv_specs(tm)
    vec = _full((1, CONV_WIDTH))
    row = pl.BlockSpec((tm, CONV_WIDTH), lambda i: (i, 0))
    return _pallas(
        body, ride=ride, name="conv_fwd", grid=(L // tm,),
        in_specs=[vrow, vhalo, _full((CONV_HALO, CONV_WIDTH)), vec, vec, vec, _full((CONV_WIDTH, CONV_WIDTH))],
        out_specs=[row, row], out_shape=[S((L, CONV_WIDTH), bf16), S((L, CONV_WIDTH), f32)],
        scratch_shapes=_conv_scratch(tm),
        compiler_params=_cp("arbitrary"),
    )(v, v, w_dw, b_dw, ln_g, ln_b, avg)


def _conv_bwd_norm(dout, zc, ln_g, ln_b, avg, tm):
    L = zc.shape[0]

    def body(do_ref, zc_ref, g_ref, bb_ref, avg_ref, dzc_ref, dg_ref, db_ref, dbd_ref):
        @pl.when(pl.program_id(0) == 0)
        def _():
            dg_ref[...] = jnp.zeros_like(dg_ref)
            db_ref[...] = jnp.zeros_like(db_ref)
            dbd_ref[...] = jnp.zeros_like(dbd_ref)

        avg = avg_ref[...]
        zc = zc_ref[...]
        xc = zc - _group_mean(zc, avg)
        rstd = lax.rsqrt(_group_mean(xc * xc, avg) + EPS)
        xhat = xc * rstd
        _, act_vjp = jax.vjp(_conv_act, xhat, g_ref[...], bb_ref[...])
        dxhat, dg, db = act_vjp(do_ref[...])
        dzc = rstd * (dxhat - _group_mean(dxhat, avg) - xhat * _group_mean(dxhat * xhat, avg))
        dzc_ref[...] = dzc
        dg_ref[0:1, :] += dg
        db_ref[0:1, :] += db
        dbd_ref[...] += _rows8(dzc)

    vec = _full((1, CONV_WIDTH))
    row = pl.BlockSpec((tm, CONV_WIDTH), lambda i: (i, 0))
    part = _full((8, CONV_WIDTH))
    return pl.pallas_call(
        body, name="conv_bwd_norm", grid=(L // tm,),
        in_specs=[row, row, vec, vec, _full((CONV_WIDTH, CONV_WIDTH))],
        out_specs=[row, part, part, part],
        out_shape=[S((L, CONV_WIDTH), f32)] + [S((8, CONV_WIDTH), f32)] * 3,
        compiler_params=_cp("arbitrary"),
    )(dout, zc, ln_g, ln_b, avg)


def _conv_bwd_taps(dzc, v, w_dw, tm, ride=()):
    L = v.shape[0]
    nt = L // tm
    per = tm // CONV_HALO

    def body(d_ref, dn_ref, v_ref, halo_ref, w_ref, dv_ref, dw_ref, zpad_ref, zs_ref, dpad_ref, ds_ref, dz_ref):
        i = pl.program_id(0)

        @pl.when(i == 0)
        def _():
            dw_ref[...] = jnp.zeros_like(dw_ref)

        _glu_padded(v_ref, halo_ref, zpad_ref, tm)
        _shifted(zpad_ref, zs_ref, tm)
        dpad_ref[:tm, :] = d_ref[...]
        dpad_ref[tm:tm + CONV_HALO, :] = jnp.where(i < nt - 1, dn_ref[...], 0.0)
        dpad_ref[tm + CONV_HALO:, :] = jnp.zeros((8, CONV_WIDTH), f32)
        _shifted(dpad_ref, ds_ref, tm)
        _tap_sum(w_ref, ds_ref, BWD_TAPS, dz_ref, tm, 0.0)

        for first in range(0, CONV_K, 8):
            taps = FWD_TAPS[first:first + 8]

            def chunk(c, accs, taps=taps):
                r0 = pl.multiple_of(c * 8, 8)
                d = d_ref[pl.ds(r0, 8), :]
                return tuple(acc + d * _window(zs_ref, r0, off, 8) for acc, (_, off) in zip(accs, taps))

            accs = lax.fori_loop(0, tm // 8, chunk, tuple(jnp.zeros((8, CONV_WIDTH), f32) for _ in taps), unroll=2)
            for acc, (k, _) in zip(accs, taps):
                dw_ref[k] += acc

        dz = dz_ref[...]
        v = v_ref[...]
        sig = _sigmoid(v[:, CONV_WIDTH:])
        dv_ref[:, :CONV_WIDTH] = dz * sig
        dv_ref[:, CONV_WIDTH:] = dz * v[:, :CONV_WIDTH] * sig * (1.0 - sig)

    vrow, vhalo = _conv_specs(tm)
    row = pl.BlockSpec((tm, CONV_WIDTH), lambda i: (i, 0))
    nxt = pl.BlockSpec((CONV_HALO, CONV_WIDTH), lambda i: (jnp.minimum((i + 1) * per, nt * per - 1), 0))
    return _pallas(
        body, ride=ride, name="conv_bwd_taps", grid=(nt,),
        in_specs=[row, nxt, vrow, vhalo, _full((CONV_HALO, CONV_WIDTH))],
        out_specs=[vrow, _full((CONV_HALO, 8, CONV_WIDTH))],
        out_shape=[S((L, 2 * CONV_WIDTH), f32), S((CONV_HALO, 8, CONV_WIDTH), f32)],
        scratch_shapes=_conv_scratch(tm) * 2 + [pltpu.VMEM((tm, CONV_WIDTH), f32)],
        compiler_params=_cp("arbitrary"),
    )(dzc, dzc, v, v, w_dw)


def _to_segments(a):
    L, c = a.shape
    return a.reshape(SEGMENTS, L // SEGMENTS, c).transpose(1, 0, 2).reshape(L, c)


def _from_segments(a):
    L, c = a.shape
    return a.reshape(L // SEGMENTS, SEGMENTS, c).transpose(1, 0, 2).reshape(L, c)


def _block_diag(ms):
    n = len(ms)

    def body(*refs):
        for a in range(n):
            out = refs[n + a]
            out[...] = jnp.zeros_like(out)
            for g in range(S5_GROUPS):
                rows = slice(g * S5_GROUP_CH, (g + 1) * S5_GROUP_CH)
                out[rows, g * S5_STATE:(g + 1) * S5_STATE] = refs[a][rows, :].astype(bf16)

    return pl.pallas_call(body, name="s5_block_diag", out_shape=[S((S5_WIDTH, S5_LANES), bf16)] * n,
                          compiler_params=pltpu.CompilerParams(vmem_limit_bytes=VMEM_LIMIT))(
        *[m.reshape(S5_WIDTH, S5_STATE) for m in ms])


def _diag_blocks(ms):
    n = len(ms)
    per_chunk = SCAN_LANES // S5_STATE

    def body(*refs):
        for a in range(n):
            for g in range(S5_GROUPS):
                rows = slice(g * S5_GROUP_CH, (g + 1) * S5_GROUP_CH)
                at = g % per_chunk * S5_STATE
                refs[n + a][rows, :] = refs[a][rows, at:at + S5_STATE]

    out = pl.pallas_call(body, name="s5_diag_blocks", out_shape=[S((S5_WIDTH, S5_STATE), f32)] * n,
                         compiler_params=pltpu.CompilerParams(vmem_limit_bytes=VMEM_LIMIT))(*ms)
    return [o.reshape(S5_GROUPS, S5_GROUP_CH, S5_STATE) for o in out]


class _NoExchanges:
    def before(self, point):
        return ()

    def after(self, point):
        pass

    def alone(self, point):
        pass


def _ffn_block(x, p, tag, tm, sched, head=None):
    point = tag + "_up"
    h, dadg, dadu, a = _ffn_up(x, p[tag + "_norm"], p[tag + "_w_gate"], p[tag + "_w_up"], tm, tag, ride=sched.before(point))
    sched.after(point)
    if head is None:
        out = _ffn_down(x, a, p[tag + "_w_down"], tm, tag, ride=sched.before(tag + "_down"))
        sched.after(tag + "_down")
    else:
        out = _ffn_down_loss(x, a, p[tag + "_w_down"], *head, tm, tag)
    return out, (h, dadg, dadu, a)


def _ffn_block_bwd(dxo, x, p, tag, saved, tm, grads, sched, parts=1, dxh=None):
    h, dadg, dadu, a = saved

    def weight_grad(which, lhs, rhs):
        point = tag + "_dw_" + which
        grads[tag + "_w_" + which] = _mm_tn(lhs, rhs, bf16, point, ride=sched.before(point))
        sched.after(point)

    if dxh is not None:
        weight_grad("down", a, dxh)
    dgate, dup, own_dxh = _ffn_bwd_act(dxo, p[tag + "_w_down"], dadg, dadu, tm, tag, ride=sched.before(tag + "_bwd_act"))
    sched.after(tag + "_bwd_act")
    weight_grad("gate", dgate, h)
    weight_grad("up", dup, h)
    if dxh is None:
        weight_grad("down", a, own_dxh)
    tiles = x.shape[0] // tm
    dx, dgs = None, []
    for k in range(parts):
        point = tag + "_bwd_in" + ("_%d" % k) * (parts > 1)
        dx, dg = _ffn_bwd_in(dxo, x, p[tag + "_norm"], dgate, dup, p[tag + "_w_gate"], p[tag + "_w_up"], tm, point,
                             tiles=(k * tiles // parts, tiles // parts), into=dx, ride=sched.before(point))
        sched.after(point)
        dgs.append(dg)
    grads[tag + "_norm"] = functools.reduce(jnp.add, dgs)
    return dx


def _local_step(x, target, p, grads, sched):
    L = x.shape[0]
    tm = min(512, L // 2)
    ni = L // SEGMENTS
    bi = min(64, ni)

    def carried(point, fn, *args):
        out = fn(*args, ride=sched.before(point))
        sched.after(point)
        return out

    x1, saved1 = _ffn_block(x, p, "ffn1", tm, sched)

    h2, u_s5, v = _mix_in(x1, p["mix_norm"], p["w_in"], tm)
    s5_in = (p["s5_lam_re"], p["s5_lam_im"], p["s5_log_dt"].reshape(S5_GROUPS, 1), p["s5_b_re"], p["s5_b_im"])
    abar_re, abar_im, bbar_re, bbar_im = _s5_params(*s5_in)
    a_re, a_im = abar_re.reshape(1, S5_LANES), abar_im.reshape(1, S5_LANES)
    bb_re, bb_im, cc_re, cc_im = _block_diag([bbar_re, bbar_im, p["s5_c_re"], p["s5_c_im"]])
    u_seg = _to_segments(u_s5)
    s_re, s_im, y_lin = carried("s5_forward", _s5_forward, u_seg, a_re, a_im, bb_re, bb_im, cc_re, cc_im, bi)
    y_s5 = _from_segments(_s5_gate(y_lin, u_seg, p["s5_d"], p["s5_w_glu"], p["s5_b_glu"], tm))
    w_dw = jnp.pad(p["conv_w_dw"], ((0, CONV_HALO - CONV_K), (0, 0)))
    heads = jnp.arange(CONV_WIDTH) // CONV_HEAD
    avg = ((heads[:, None] == heads[None, :]).astype(f32) / CONV_HEAD).astype(bf16)
    y_conv, zc = carried("conv_fwd", _conv_fwd, v, w_dw, p["conv_b_dw"], p["conv_ln_g"], p["conv_ln_b"], avg, tm)
    x2 = _mix_out(x1, y_s5, y_conv, p["w_out"], tm)

    (dx3, grads["final_norm"], loss_terms), saved2 = _ffn_block(
        x2, p, "ffn2", tm, sched, head=(target, p["final_norm"].reshape(1, D_MODEL)))
    grads["loss_terms"] = loss_terms

    dx2 = _ffn_block_bwd(dx3, x2, p, "ffn2", saved2, tm, grads, sched)

    dy_s5, dy_conv, dx2b = carried("mix_out_bwd", _mix_out_bwd, dx2, p["w_out"], tm)
    grads["w_out"] = jnp.concatenate([_mm_tn(y_s5, dx2b, bf16, "dw_out_s5"), _mm_tn(y_conv, dx2b, bf16, "dw_out_conv")], axis=0)
    dy_lin, du_skip, dd8, grads["s5_w_glu"], dbg8 = _s5_read_bwd(
        _to_segments(dy_s5), y_lin, u_seg, p["s5_d"], p["s5_w_glu"], p["s5_b_glu"], tm)
    grads["s5_d"] = dd8.sum(axis=0, keepdims=True)
    grads["s5_b_glu"] = dbg8.sum(axis=0, keepdims=True)
    du_seg, da_re8, da_im8, dbb_re, dbb_im, dcc_re, dcc_im = carried(
        "s5_backward", _s5_backward, dy_lin, u_seg, du_skip, s_re, s_im, a_re, -a_im, bb_re, bb_im, cc_re, cc_im, bi)
    d_abar = lambda a8: a8.sum(axis=0).reshape(S5_GROUPS, S5_STATE)
    grads["s5_c_re"], grads["s5_c_im"], d_bbr, d_bbi = _diag_blocks([dcc_re, dcc_im, dbb_re, dbb_im])
    d_lr, d_li, d_ld, d_br, d_bi = _s5_params_bwd(*s5_in, d_abar(da_re8), d_abar(da_im8), d_bbr, d_bbi)
    grads["s5_lam_re"], grads["s5_lam_im"], grads["s5_log_dt"] = d_lr, d_li, d_ld.reshape(1, S5_GROUPS)
    grads["s5_b_re"], grads["s5_b_im"] = d_br, d_bi
    dzc, dlg8, dlb8, dbd8 = _conv_bwd_norm(dy_conv, zc, p["conv_ln_g"], p["conv_ln_b"], avg, tm)
    grads["conv_ln_g"] = dlg8.sum(axis=0, keepdims=True)
    grads["conv_ln_b"] = dlb8.sum(axis=0, keepdims=True)
    grads["conv_b_dw"] = dbd8.sum(axis=0, keepdims=True)
    dv, dw8 = carried("conv_bwd_taps", _conv_bwd_taps, dzc, v, w_dw, tm)
    grads["conv_w_dw"] = dw8.sum(axis=1)[:CONV_K]
    dx1, grads["mix_norm"], dub, dx1h = _mix_in_bwd(dx2, x1, p["mix_norm"], _from_segments(du_seg), dv, p["w_in"], tm)
    grads["w_in"] = _mm_tn(dub, h2, bf16, "dw_in")

    dx0 = _ffn_block_bwd(dx1, x, p, "ffn1", saved1, tm, grads, sched, parts=min(2, L // tm), dxh=dx1h)
    sched.alone("tail")
    return loss_terms, dx0


MESH = pl.DeviceIdType.MESH
ANY = pl.BlockSpec(memory_space=pl.ANY)


def _place():
    return lax.axis_index("x"), lax.axis_index("y"), lax.axis_index("c")


class _Exchange:
    def __init__(self, ins, out_shape, sems, start, finish):
        self.ins, self.out_shape, self.sems, self.start, self.finish = list(ins), list(out_shape), list(sems), start, finish
        self.out = None


def _pallas(body, *, ride=(), **kw):
    if not ride:
        return pl.pallas_call(body, **kw)

    def run(*args):
        out_shape = kw.get("out_shape", [])
        single = not isinstance(out_shape, (list, tuple))
        shapes = [out_shape] if single else list(out_shape)
        out_specs = [kw["out_specs"]] if single else list(kw.get("out_specs", []))
        grid = tuple(kw.get("grid", ()))
        scratch = list(kw.get("scratch_shapes", ()))
        n_in, n_out, n_scr = len(args), len(shapes), len(scratch)
        r_in = [len(e.ins) for e in ride]
        r_out = [len(e.out_shape) for e in ride]
        r_sem = [len(e.sems) for e in ride]

        def wrapped(*refs):
            own_in, refs = refs[:n_in], refs[n_in:]
            ex_in, refs = refs[:sum(r_in)], refs[sum(r_in):]
            own_out, refs = refs[:n_out], refs[n_out:]
            ex_out, refs = refs[:sum(r_out)], refs[sum(r_out):]
            own_scr, ex_sem = refs[:n_scr], refs[n_scr:]
            parts = []
            for e, ni, no, ns in zip(ride, r_in, r_out, r_sem):
                parts.append((e, ex_in[:ni], ex_out[:no], ex_sem[:ns]))
                ex_in, ex_out, ex_sem = ex_in[ni:], ex_out[no:], ex_sem[ns:]

            def at(step):
                def go():
                    for e, i, o, s in parts:
                        getattr(e, step)(i, o, s)
                if grid:
                    ids = [pl.program_id(d) for d in range(len(grid))]
                    when = [i == (0 if step == "start" else g - 1) for i, g in zip(ids, grid)]
                    pl.when(functools.reduce(lambda a, b: a & b, when))(go)
                else:
                    go()

            at("start")
            if body is not None:
                body(*own_in, *own_out, *own_scr)
            at("finish")

        outs = pl.pallas_call(
            wrapped, name=kw["name"], grid=grid,
            in_specs=list(kw.get("in_specs", [])) + [ANY] * sum(r_in),
            out_specs=out_specs + [ANY] * sum(r_out),
            out_shape=shapes + [s for e in ride for s in e.out_shape],
            scratch_shapes=scratch + [s for e in ride for s in e.sems],
            input_output_aliases=kw.get("input_output_aliases", {}),
            compiler_params=_cp(*["arbitrary"] * len(grid)),
        )(*args, *[a for e in ride for a in e.ins])
        own, rest = outs[:n_out], outs[n_out:]
        for e, no in zip(ride, r_out):
            e.out, rest = list(rest[:no]), rest[no:]
        return own[0] if single else own

    return run


def _exchange(ride, name):
    _pallas(None, ride=ride, name=name)()


def _gather(arrs):
    n = len(arrs)

    def copies(ins, outs, sems):
        send_sems, recv_sems, local_sems = sems
        x, y, c = _place()
        me, sibling = (x, y, c), (x, y, 1 - c)
        chips = [(1 - x, y), (x, 1 - y), (1 - x, 1 - y)]

        def place(a, block):
            return outs[a].at[block]

        def copy(a, k, block, to, src=None):
            px, py, pc = block
            dst = place(a, 4 * px + 2 * py + pc)
            return pltpu.make_async_remote_copy(
                src_ref=dst if src is None else src, dst_ref=dst, send_sem=send_sems.at[7 * a + k],
                recv_sem=recv_sems.at[7 * a + k], device_id=to, device_id_type=MESH)

        def own():
            local = [pltpu.make_async_copy(ins[a], place(a, 4 * x + 2 * y + c), local_sems.at[a]) for a in range(n)]
            remote = []
            for a in range(n):
                remote.append(copy(a, 0, me, sibling, src=ins[a]))
                remote += [copy(a, 1 + j, me, (*chip, c), src=ins[a]) for j, chip in enumerate(chips)]
            return local, remote

        return c, me, sibling, chips, copy, own

    def start(ins, outs, sems):
        local, remote = copies(ins, outs, sems)[-1]()
        for cp in local + remote:
            cp.start()

    def finish(ins, outs, sems):
        c, me, sibling, chips, copy, own = copies(ins, outs, sems)
        passed = []
        for j, chip in enumerate(chips):
            for a in range(n):
                copy(a, 1 + j, (*chip, c), me).wait_recv()
                passed.append(copy(a, 4 + j, (*chip, c), sibling))
                passed[-1].start()
        for a in range(n):
            copy(a, 0, sibling, me).wait_recv()
            for j, chip in enumerate(chips):
                copy(a, 4 + j, (*chip, 1 - c), me).wait_recv()
        local, remote = own()
        for cp in remote + passed:
            cp.wait_send()
        for cp in local:
            cp.wait()

    dma = pltpu.SemaphoreType.DMA
    shapes = [S((N_DEV, *a.shape), a.dtype) for a in arrs]
    return _Exchange(arrs, shapes, [dma((7 * n,)), dma((7 * n,)), dma((n,))], start, finish)


def _swap_with_sibling(gs):
    n = len(gs)

    def copies(ins, outs, sems):
        x, y, c = _place()
        return [pltpu.make_async_remote_copy(
            src_ref=ins[a].at[:, 1 - c], dst_ref=outs[a], send_sem=sems[0].at[a], recv_sem=sems[1].at[a],
            device_id=(x, y, 1 - c), device_id_type=MESH) for a in range(n)]

    def start(ins, outs, sems):
        for cp in copies(ins, outs, sems):
            cp.start()

    def finish(ins, outs, sems):
        for cp in copies(ins, outs, sems):
            cp.wait()

    dma = pltpu.SemaphoreType.DMA
    return _Exchange(gs, [S((N_CHIP, *g.shape[2:]), g.dtype) for g in gs], [dma((n,)), dma((n,))], start, finish)


def _swap_with_chips(ps):
    n = len(ps)

    def copies(ins, outs, sems):
        x, y, c = _place()
        q = 2 * x + y
        peers = [(x, 1 - y), (1 - x, y), (1 - x, 1 - y)]

        def copy(a, j, slot_from, slot_to):
            px, py = peers[j]
            return pltpu.make_async_remote_copy(
                src_ref=ins[a].at[slot_from], dst_ref=outs[a].at[slot_to], send_sem=sems[0].at[3 * a + j],
                recv_sem=sems[1].at[3 * a + j], device_id=(px, py, c), device_id_type=MESH)

        sends = lambda: [copy(a, j, 2 * peers[j][0] + peers[j][1], q) for a in range(n) for j in range(3)]
        lands = lambda: [copy(a, j, q, 2 * peers[j][0] + peers[j][1]) for a in range(n) for j in range(3)]
        return sends, lands

    def start(ins, outs, sems):
        for cp in copies(ins, outs, sems)[0]():
            cp.start()

    def finish(ins, outs, sems):
        sends, lands = copies(ins, outs, sems)
        for cp in lands():
            cp.wait_recv()
        for cp in sends():
            cp.wait_send()

    dma = pltpu.SemaphoreType.DMA
    return _Exchange(ps, [S(p.shape, p.dtype) for p in ps], [dma((3 * n,)), dma((3 * n,))], start, finish)


def _row_tile(rows, cols, itemsize):
    t = rows
    while t * cols * itemsize > (1 << 20) and t % 32 == 0:
        t //= 2
    return t


def _add_sibling(g4, st, core, name):
    _, R, C = st.shape
    tr = _row_tile(R, C, 1)

    def body(c_ref, g_ref, s_ref, o_ref):
        o_ref[...] = (g_ref[...].astype(f32) + s_ref[...].astype(f32)).astype(bf16)

    mine = pl.BlockSpec((None, None, tr, C), lambda q, i, c: (q, c[0], i, 0))
    return pl.pallas_call(
        body, name=name,
        grid_spec=pltpu.PrefetchScalarGridSpec(
            num_scalar_prefetch=1, grid=(N_CHIP, R // tr),
            in_specs=[mine,
                      pl.BlockSpec((None, tr, C), lambda q, i, c: (q, i, 0))],
            out_specs=pl.BlockSpec((None, tr, C), lambda q, i, c: (q, i, 0))),
        out_shape=S((N_CHIP, R, C), bf16),
        compiler_params=_cp("parallel", "parallel"),
    )(core, g4, st)


def _adamw(w, g, m, v):
    m = B1 * m + (1.0 - B1) * g
    v = B2 * v + (1.0 - B2) * (g * g)
    m_hat = m / (1.0 - B1 ** STEP)
    v_hat = v / (1.0 - B2 ** STEP)
    return -LR * (m_hat / (jnp.sqrt(v_hat) + ADAM_EPS) + WD * w), m, v


def _adam_sharded(w, m, v, part, got, slots, name):
    R, C = w.shape
    _, Rp, Cp = part.shape
    if Rp == R:
        tr = _row_tile(R, Cp, 4)
    else:
        tr = R // 2 if R % 32 == 0 else R

    def body(s_ref, w_ref, m_ref, v_ref, p_ref, a_ref, b_ref, c_ref, g_out, d_out, m_out, v_out):
        g = p_ref[...].astype(f32) + a_ref[...].astype(f32) + b_ref[...].astype(f32) + c_ref[...].astype(f32)
        g = g[:, :C]
        g_out[...] = g
        d_out[...], m_out[...], v_out[...] = _adamw(w_ref[...], g, m_ref[...], v_ref[...])

    shard = pl.BlockSpec((tr, C), lambda i, s: (i, 0))
    slot = lambda k: pl.BlockSpec((None, tr, Cp), lambda i, s: (s[k], i, 0))
    return pl.pallas_call(
        body, name=name,
        grid_spec=pltpu.PrefetchScalarGridSpec(
            num_scalar_prefetch=1, grid=(R // tr,),
            in_specs=[shard, shard, shard, slot(0), slot(1), slot(2), slot(3)],
            out_specs=[shard] * 4),
        out_shape=[S((R, C), f32)] * 4,
        compiler_params=_cp("parallel"),
    )(slots, w, m, v, part, got, got, got)


def _adam_replicated(items, loss_terms, name):
    n = len(items)
    has_loss = loss_terms is not None

    def total(ref):
        g = ref[0]
        for d in range(1, N_DEV):
            g = g + ref[d]
        return g

    def body(*refs):
        ins, outs = refs[:4 * n + has_loss], refs[4 * n + has_loss:]
        for i in range(n):
            w_ref, m_ref, v_ref, g_ref = ins[4 * i:4 * i + 4]
            g = total(g_ref)
            outs[4 * i][...] = g
            outs[4 * i + 1][...], outs[4 * i + 2][...], outs[4 * i + 3][...] = _adamw(w_ref[...], g, m_ref[...], v_ref[...])
        if has_loss:
            outs[-1][...] = jnp.sum(total(ins[-1]), keepdims=True)

    flat = [a for item in items for a in item] + ([loss_terms] if has_loss else [])
    shapes = [S(item[0].shape, f32) for item in items for _ in range(4)] + ([S((1, 1), f32)] if has_loss else [])
    out = pl.pallas_call(body, name=name, out_shape=shapes,
                         compiler_params=pltpu.CompilerParams(vmem_limit_bytes=VMEM_LIMIT))(*flat)
    return [out[4 * i:4 * i + 4] for i in range(n)], (out[-1] if has_loss else None)


WEIGHTS = ["ffn1_norm", "ffn1_w_gate", "ffn1_w_up", "ffn1_w_down", "mix_norm", "w_in", "s5_lam_re", "s5_lam_im", "s5_log_dt",
           "s5_b_re", "s5_b_im", "s5_c_re", "s5_c_im", "s5_d", "s5_w_glu", "s5_b_glu", "conv_w_dw", "conv_b_dw", "conv_ln_g",
           "conv_ln_b", "w_out", "ffn2_norm", "ffn2_w_gate", "ffn2_w_up", "ffn2_w_down", "final_norm"]
SHARDED = ["ffn1_w_gate", "ffn1_w_up", "ffn1_w_down", "w_in", "s5_w_glu", "conv_w_dw", "w_out", "ffn2_w_gate", "ffn2_w_up",
           "ffn2_w_down"]
REPLICATED = [n for n in WEIGHTS if n not in SHARDED]
TRANSPOSED = ["ffn1_w_gate", "ffn1_w_up", "ffn2_w_gate", "ffn2_w_up", "w_in"]


def _shard_to_wire(n, w):
    if n == "conv_w_dw":
        return jnp.pad(w, ((0, CONV_HALO - CONV_K), (0, 0)))
    return w.astype(bf16)


def _to_wire(shards, ride):
    names = list(shards)
    shapes = [jax.eval_shape(functools.partial(_shard_to_wire, n), shards[n]) for n in names]

    def body(*refs):
        for src, dst in zip(refs[:len(names)], refs[len(names):]):
            (r, c), (rp, cp) = src.shape, dst.shape
            dst[:r, :c] = src[...].astype(dst.dtype)
            if cp > c:
                dst[:, c:] = jnp.zeros((rp, cp - c), dst.dtype)
            if rp > r:
                dst[r:, :] = jnp.zeros((rp - r, cp), dst.dtype)

    out = _pallas(body, ride=ride, name="to_wire", out_shape=shapes, in_specs=[pl.BlockSpec(memory_space=pltpu.VMEM)] * len(names),
                  out_specs=[pl.BlockSpec(memory_space=pltpu.VMEM)] * len(names))(*[shards[n] for n in names])
    return dict(zip(names, out))


def _gathered_to_full(n, g):
    if n == "conv_w_dw":
        return g.transpose(1, 0, 2).reshape(CONV_HALO, CONV_WIDTH)[:CONV_K]
    return g.reshape(N_DEV * g.shape[1], g.shape[2])


def _grad_to_blocks(n, g):
    if n == "conv_w_dw":
        g = jnp.pad(g, ((0, CONV_HALO - CONV_K), (0, 0)))
        g = g.reshape(g.shape[0], N_DEV, g.shape[1] // N_DEV).transpose(1, 0, 2)
    else:
        g = g.reshape(N_DEV, g.shape[0] // N_DEV, g.shape[1])
    return g.astype(bf16).reshape(N_CHIP, 2, *g.shape[1:])


REPLICATED_LATE = ["ffn1_norm"]
REPLICATED_HEAD = ["ffn2_norm", "final_norm"]
REPLICATED_MIX = ["mix_norm", "conv_b_dw", "conv_ln_g", "conv_ln_b"]
REPLICATED_S5 = [n for n in REPLICATED if n not in REPLICATED_LATE + REPLICATED_HEAD + REPLICATED_MIX]
REPLICATED_EARLY = REPLICATED_HEAD + REPLICATED_S5 + REPLICATED_MIX

PLAN = {
    "start": [("gather", ["ffn1_w_gate", "ffn1_w_up"])],
    "ffn1_up": [("gather", ["ffn1_w_down", "w_in", "w_out", "s5_w_glu", "conv_w_dw"])],
    "s5_forward": [("gather", ["ffn2_w_gate", "ffn2_w_up"])],
    "conv_fwd": [("gather", ["ffn2_w_down"])],
    "ffn2_dw_up": [("sibling", ["ffn2_w_gate"])],
    "ffn2_dw_down": [("sibling", ["ffn2_w_up"])],
    "mix_out_bwd": [("sibling", ["ffn2_w_down"]), ("replicated", REPLICATED_HEAD)],
    "s5_backward": [("chips", ["ffn2_w_gate", "ffn2_w_up"])],
    "conv_bwd_taps": [("chips", ["ffn2_w_down"]), ("replicated", REPLICATED_S5)],
    "ffn1_dw_down": [("sibling", ["w_in", "s5_w_glu", "conv_w_dw", "w_out"]), ("replicated", REPLICATED_MIX)],
    "ffn1_bwd_act": [("chips", ["w_in", "s5_w_glu", "conv_w_dw", "w_out"]), ("sibling", ["ffn1_w_down"])],
    "ffn1_dw_gate": [("chips", ["ffn1_w_down"])],
    "ffn1_dw_up": [("sibling", ["ffn1_w_gate"])],
    "ffn1_bwd_in_0": [("chips", ["ffn1_w_gate"]), ("sibling", ["ffn1_w_up"])],
    "ffn1_bwd_in_1": [("chips", ["ffn1_w_up"])],
    "tail": [("replicated", REPLICATED_LATE)],
}


class _Schedule:
    def __init__(self, wire, p, grads, core):
        self.wire, self.p, self.grads, self.core = wire, p, grads, core
        self.partial, self.reduced, self.everyone, self.pending = {}, {}, {}, []

    def before(self, point):
        assert not self.pending
        for kind, names in PLAN.get(point, ()):
            if kind == "gather":
                given = [self.wire[n] for n in names]
                ex = _gather(given)
            elif kind == "sibling":
                given = [_grad_to_blocks(n, self.grads[n]) for n in names]
                ex = _swap_with_sibling(given)
            elif kind == "chips":
                given = [self.partial.pop(n) for n in names]
                ex = _swap_with_chips(given)
            else:
                names = names + ["loss_terms"] * (names is REPLICATED_HEAD)
                given = [self.grads[n].reshape(self.p[n].shape) if n in self.p else self.grads[n] for n in names]
                ex = _gather(given)
            self.pending.append((kind, names, given, ex))
        return [ex for _, _, _, ex in self.pending]

    def after(self, point):
        for kind, names, given, ex in self.pending:
            if kind == "gather":
                for n, g in zip(names, ex.out):
                    self.p[n] = _gathered_to_full(n, g)
            elif kind == "sibling":
                for n, blocks, got in zip(names, given, ex.out):
                    self.partial[n] = _add_sibling(blocks, got, self.core, "reduce_add_" + n)
            elif kind == "chips":
                for n, part, got in zip(names, given, ex.out):
                    self.reduced[n] = (part, got)
            else:
                self.everyone.update(zip(names, ex.out))
        self.pending = []

    def alone(self, point):
        _exchange(self.before(point), point)
        self.after(point)


def kernel(x, ffn1_norm, ffn1_w_gate, ffn1_w_up, ffn1_w_down, mix_norm, w_in, s5_lam_re, s5_lam_im, s5_log_dt, s5_b_re, s5_b_im, s5_c_re, s5_c_im, s5_d, s5_w_glu, s5_b_glu, conv_w_dw, conv_b_dw, conv_ln_g, conv_ln_b, w_out, ffn2_norm, ffn2_w_gate, ffn2_w_up, ffn2_w_down, final_norm, loss_target, m_ffn1_norm, m_ffn1_w_gate, m_ffn1_w_up, m_ffn1_w_down, m_mix_norm, m_w_in, m_s5_lam_re, m_s5_lam_im, m_s5_log_dt, m_s5_b_re, m_s5_b_im, m_s5_c_re, m_s5_c_im, m_s5_d, m_s5_w_glu, m_s5_b_glu, m_conv_w_dw, m_conv_b_dw, m_conv_ln_g, m_conv_ln_b, m_w_out, m_ffn2_norm, m_ffn2_w_gate, m_ffn2_w_up, m_ffn2_w_down, m_final_norm, v_ffn1_norm, v_ffn1_w_gate, v_ffn1_w_up, v_ffn1_w_down, v_mix_norm, v_w_in, v_s5_lam_re, v_s5_lam_im, v_s5_log_dt, v_s5_b_re, v_s5_b_im, v_s5_c_re, v_s5_c_im, v_s5_d, v_s5_w_glu, v_s5_b_glu, v_conv_w_dw, v_conv_b_dw, v_conv_ln_g, v_conv_ln_b, v_w_out, v_ffn2_norm, v_ffn2_w_gate, v_ffn2_w_up, v_ffn2_w_down, v_final_norm):
    args = locals()
    w = {n: args[n] for n in WEIGHTS}
    m = {n: args["m_" + n] for n in WEIGHTS}
    v = {n: args["v_" + n] for n in WEIGHTS}
    xq, yq, cq = _place()
    q = 2 * xq + yq
    slots = jnp.stack([q, q ^ 1, q ^ 2, q ^ 3]).astype(jnp.int32)

    def shard2d(n, a):
        a = a.reshape(a.shape[-2:])
        return a.T if n in TRANSPOSED else a

    def view(n, a):
        if n.startswith("s5_b_") and a.ndim == 4:
            return a[0].transpose(0, 2, 1)
        return a[0] if a.ndim >= 3 else a.reshape(1, -1)

    def unview(n, a):
        return (a.transpose(0, 2, 1) if n.startswith("s5_b_") and a.ndim == 3 else a).reshape(w[n].shape)

    p = {n: view(n, w[n]) for n in REPLICATED}
    grads = {}
    first = PLAN["start"][0][1]
    wire = {n: _shard_to_wire(n, shard2d(n, w[n])) for n in first}
    sched = _Schedule(wire, p, grads, jnp.reshape(cq, (1,)).astype(jnp.int32))
    wire.update(_to_wire({n: shard2d(n, w[n]) for n in SHARDED if n not in first}, sched.before("start")))
    sched.after("start")
    _, dx = _local_step(x[0], loss_target[0], p, grads, sched)

    out = {}
    for n in SHARDED:
        part, got = sched.reduced[n]
        rows = got.shape[1] if n == "conv_w_dw" else shard2d(n, w[n]).shape[0]
        fit = lambda a: jnp.pad(shard2d(n, a), ((0, rows - shard2d(n, a).shape[0]), (0, 0)))
        res = _adam_sharded(fit(w[n]), fit(m[n]), fit(v[n]), part, got, slots, "adam_" + n)
        back = lambda r: r[:shard2d(n, w[n]).shape[0]]
        out[n] = [(back(r).T if n in TRANSPOSED else back(r)).reshape(w[n].shape) for r in res]

    for names in (REPLICATED_EARLY, REPLICATED_LATE):
        items = [(view(n, w[n]), view(n, m[n]), view(n, v[n]), sched.everyone[n]) for n in names]
        res, total = _adam_replicated(items, sched.everyone.get("loss_terms") if names is REPLICATED_EARLY else None,
                                      "adam_" + names[0])
        for n, r in zip(names, res):
            out[n] = [unview(n, a) for a in r]
        if total is not None:
            loss = total.reshape(())

    return (loss, dx.reshape(x.shape), *[out[n][0] for n in WEIGHTS], *[out[n][1] for n in WEIGHTS],
            *[out[n][2] for n in WEIGHTS], *[out[n][3] for n in WEIGHTS])
```

```python
import functools

import jax
import jax.numpy as jnp
from jax import lax
from jax.experimental import pallas as pl
from jax.experimental.pallas import tpu as pltpu

f32 = jnp.float32
bf16 = jnp.bfloat16
S = jax.ShapeDtypeStruct

N_DEV = 8
N_CHIP = 4
D_MODEL = 1024
D_FF = 2816
FF_CHUNKS = [(0, 768), (768, 1536), (1536, 2304), (2304, D_FF)]
S5_WIDTH = 512
S5_GROUPS = 32
S5_GROUP_CH = 16
S5_STATE = 64
S5_LANES = S5_GROUPS * S5_STATE
CONV_WIDTH = 512
CONV_K = 31
CONV_HALO = 32
CONV_HEAD = 64
CONV_ROWS = 32
IN_COLS = S5_WIDTH + 2 * CONV_WIDTH
SEGMENTS = 8
SCAN_LANES = 512
EPS = 1e-6
LR, B1, B2, ADAM_EPS, WD, STEP = 0.001, 0.9, 0.999, 1e-08, 0.01, 10
VMEM_LIMIT = 56 * 1024 * 1024

NN = (((1,), (0,)), ((), ()))
NT = (((1,), (1,)), ((), ()))
TN = (((0,), (0,)), ((), ()))


def _dot(a, b, dims=NN):
    return lax.dot_general(a, b, dims, preferred_element_type=f32)


def _cp(*sem):
    return pltpu.CompilerParams(dimension_semantics=sem, vmem_limit_bytes=VMEM_LIMIT)


def _rms(x, g):
    return x * lax.rsqrt(jnp.mean(x * x, axis=-1, keepdims=True) + EPS) * g


def _rms_bwd(x, g, dh):
    _, vjp = jax.vjp(_rms, x, g)
    return vjp(dh)


def _sigmoid(x):
    return 1.0 / (1.0 + jnp.exp(-x))


def _gelu(x):
    return 0.5 * x * (1.0 + jnp.tanh(0.7978845608028654 * (x + 0.044715 * x * x * x)))


def _rows8(x):
    t, c = x.shape
    return x.reshape(t // 8, 8, c).sum(axis=0)


def _full(shape):
    return pl.BlockSpec(shape, lambda *_: (0,) * len(shape))


def _resident(shape):
    return pl.BlockSpec(shape, lambda *_: (0,) * len(shape), pipeline_mode=pl.Buffered(1))


def _ffn_up(x, g, wg, wu, tm, tag, mixed=None, ride=()):
    L = x.shape[0]

    def body(x_ref, g_ref, wg_ref, wu_ref, *rest):
        h_ref, dadg_ref, dadu_ref, a_ref = rest[-5:-1] if mixed else rest[-4:]
        x = x_ref[...]
        if mixed:
            ys_ref, yc_ref, wo_ref = rest[:3]
            x = x + _dot(ys_ref[...], wo_ref[:S5_WIDTH, :]) + _dot(yc_ref[...], wo_ref[S5_WIDTH:, :])
            rest[-1][...] = x
        h = _rms(x, g_ref[...]).astype(bf16)
        h_ref[...] = h
        for lo, hi in FF_CHUNKS:
            cols = slice(lo, hi)
            gate =_dot(h, wg_ref[cols, :], NT)
            up = _dot(h, wu_ref[cols, :], NT)
            sig = _sigmoid(gate)
            silu = gate * sig
            dadg_ref[:, cols] = (up * (sig + silu * (1.0 - sig))).astype(bf16)
            dadu_ref[:, cols] = silu.astype(bf16)
            a_ref[:, cols] = (silu * up).astype(bf16)

    row = pl.BlockSpec((tm, D_MODEL), lambda i: (i, 0))
    wide = pl.BlockSpec((tm, D_FF), lambda i: (i, 0))
    half = pl.BlockSpec((tm, S5_WIDTH), lambda i: (i, 0))
    return _pallas(
        body, ride=ride, name=tag + "_up", grid=(L // tm,),
        in_specs=[row, _full((1, D_MODEL)), _resident((D_FF, D_MODEL)), _resident((D_FF, D_MODEL))]
        + ([half, half, _resident((D_MODEL, D_MODEL))] if mixed else []),
        out_specs=[row, wide, wide, wide] + [row] * bool(mixed),
        out_shape=[S((L, D_MODEL), bf16)] + [S((L, D_FF), bf16)] * 3 + [S((L, D_MODEL), f32)] * bool(mixed),
        compiler_params=_cp("parallel"),
    )(x, g, wg, wu, *(mixed or ()))


def _ffn_down(x, a, wd, tm, tag, ride=()):
    L = x.shape[0]

    def body(x_ref, a_ref, wd_ref, o_ref):
        o_ref[...] = x_ref[...] + 0.5 * _dot(a_ref[...], wd_ref[...])

    return _pallas(
        body, ride=ride, name=tag + "_down", grid=(L // tm,),
        in_specs=[pl.BlockSpec((tm, D_MODEL), lambda i: (i, 0)), pl.BlockSpec((tm, D_FF), lambda i: (i, 0)),
                  _resident((D_FF, D_MODEL))],
        out_specs=pl.BlockSpec((tm, D_MODEL), lambda i: (i, 0)),
        out_shape=S((L, D_MODEL), f32),
        compiler_params=_cp("parallel"),
    )(x, a, wd)


def _ffn_down_loss(x, a, wd, target, g, tm, tag):
    L = x.shape[0]

    def body(x_ref, a_ref, wd_ref, t_ref, g_ref, dx_ref, dg_ref, l_ref):
        @pl.when(pl.program_id(0) == 0)
        def _():
            dg_ref[...] = jnp.zeros_like(dg_ref)
            l_ref[...] = jnp.zeros_like(l_ref)

        xo = x_ref[...] + 0.5 * _dot(a_ref[...], wd_ref[...])
        g = g_ref[...]
        e = _rms(xo, g) - t_ref[...]
        l_ref[...] += _rows8(e * e) * (0.5 / D_MODEL)
        dx, dg = _rms_bwd(xo, g, e * (1.0 / D_MODEL))
        dx_ref[...] = dx
        dg_ref[...] += dg

    row = pl.BlockSpec((tm, D_MODEL), lambda i: (i, 0))
    return pl.pallas_call(
        body, name=tag + "_down_loss", grid=(L // tm,),
        in_specs=[row, pl.BlockSpec((tm, D_FF), lambda i: (i, 0)), _resident((D_FF, D_MODEL)), row, _full((1, D_MODEL))],
        out_specs=[row, _full((1, D_MODEL)), _full((8, D_MODEL))],
        out_shape=[S((L, D_MODEL), f32), S((1, D_MODEL), f32), S((8, D_MODEL), f32)],
        compiler_params=_cp("arbitrary"),
    )(x, a, wd, target, g)


def _ffn_bwd_act(dxo, wd, dadg, dadu, tm, tag, ride=()):
    L = dxo.shape[0]

    def body(dx_ref, wd_ref, dadg_ref, dadu_ref, dgate_ref, dup_ref, dxh_ref):
        dxh = (0.5 * dx_ref[...]).astype(bf16)
        dxh_ref[...] = dxh
        for lo, hi in FF_CHUNKS:
            cols = slice(lo, hi)
            da =_dot(dxh, wd_ref[cols, :], NT)
            dgate_ref[:, cols] = (da * dadg_ref[:, cols].astype(f32)).astype(bf16)
            dup_ref[:, cols] = (da * dadu_ref[:, cols].astype(f32)).astype(bf16)

    row = pl.BlockSpec((tm, D_MODEL), lambda i: (i, 0))
    wide = pl.BlockSpec((tm, D_FF), lambda i: (i, 0))
    return _pallas(
        body, ride=ride, name=tag + "_bwd_act", grid=(L // tm,),
        in_specs=[row, _resident((D_FF, D_MODEL)), wide, wide],
        out_specs=[wide, wide, row],
        out_shape=[S((L, D_FF), bf16), S((L, D_FF), bf16), S((L, D_MODEL), bf16)],
        compiler_params=_cp("parallel"),
    )(dxo, wd, dadg, dadu)


def _ffn_bwd_in(dxo, x, g, dgate, dup, wg, wu, tm, name, tiles=None, into=None, ride=()):
    L = x.shape[0]
    first, count = tiles or (0, L // tm)

    def body(dxo_ref, x_ref, g_ref, dgate_ref, dup_ref, wg_ref, wu_ref, *rest):
        dx_ref, dg_ref = rest[-2:]

        @pl.when(pl.program_id(0) == 0)
        def _():
            dg_ref[...] = jnp.zeros_like(dg_ref)

        dh = _dot(dgate_ref[...], wg_ref[...]) + _dot(dup_ref[...], wu_ref[...])
        dx, dg = _rms_bwd(x_ref[...], g_ref[...], dh)
        dx_ref[...] = dxo_ref[...] + dx
        dg_ref[...] += dg

    row = pl.BlockSpec((tm, D_MODEL), lambda i: (first + i, 0))
    wide = pl.BlockSpec((tm, D_FF), lambda i: (first + i, 0))
    return _pallas(
        body, ride=ride, name=name, grid=(count,),
        in_specs=[row, row, _full((1, D_MODEL)), wide, wide, _resident((D_FF, D_MODEL)), _resident((D_FF, D_MODEL))]
        + [ANY] * (into is not None),
        out_specs=[row, _full((1, D_MODEL))],
        out_shape=[S((L, D_MODEL), f32), S((1, D_MODEL), f32)],
        input_output_aliases={7: 0} if into is not None else {},
        compiler_params=_cp("arbitrary"),
    )(dxo, x, g, dgate, dup, wg, wu, *([into] if into is not None else []))


def _mm_tn(a, b, out_dtype, name, tm=512, tn=1024, ride=()):
    L, M = a.shape
    N = b.shape[1]
    tm, tn = min(tm, M), min(tn, N)
    while M % tm:
        tm //= 2
    while N % tn:
        tn //= 2

    def body(a_ref, b_ref, o_ref):
        o_ref[...] = _dot(a_ref[...].astype(bf16), b_ref[...].astype(bf16), TN).astype(out_dtype)

    return _pallas(
        body, ride=ride, name=name, grid=(M // tm, N // tn),
        in_specs=[pl.BlockSpec((L, tm), lambda i, j: (0, i)), pl.BlockSpec((L, tn), lambda i, j: (0, j))],
        out_specs=pl.BlockSpec((tm, tn), lambda i, j: (i, j)),
        out_shape=S((M, N), out_dtype),
        compiler_params=_cp("parallel", "parallel"),
    )(a, b)


def _mix_in(x, g, w_in, tm):
    L = x.shape[0]

    def body(x_ref, g_ref, w_ref, h_ref, us_ref, v_ref):
        h = _rms(x_ref[...], g_ref[...]).astype(bf16)
        h_ref[...] = h
        u = _dot(h, w_ref[...], NT)
        us_ref[...] = u[:, :S5_WIDTH]
        v_ref[...] = u[:, S5_WIDTH:]

    row = lambda c: pl.BlockSpec((tm, c), lambda i: (i, 0))
    return pl.pallas_call(
        body, name="mix_in", grid=(L // tm,),
        in_specs=[row(D_MODEL), _full((1, D_MODEL)), _full((IN_COLS, D_MODEL))],
        out_specs=[row(D_MODEL), row(S5_WIDTH), row(2 * CONV_WIDTH)],
        out_shape=[S((L, D_MODEL), bf16), S((L, S5_WIDTH), f32), S((L, 2 * CONV_WIDTH), f32)],
        compiler_params=_cp("parallel"),
    )(x, g, w_in)


def _mix_in_bwd(dxo, x, g, du_s5, dv, w_in, tm):
    L = x.shape[0]

    def body(dxo_ref, x_ref, g_ref, dus_ref, dv_ref, w_ref, dx_ref, dg_ref, dub_ref, dxh_ref):
        @pl.when(pl.program_id(0) == 0)
        def _():
            dg_ref[...] = jnp.zeros_like(dg_ref)

        dus = dus_ref[...].astype(bf16)
        dvb = dv_ref[...].astype(bf16)
        dub_ref[:, :S5_WIDTH] = dus
        dub_ref[:, S5_WIDTH:] = dvb
        dh = _dot(dus, w_ref[:S5_WIDTH, :]) + _dot(dvb, w_ref[S5_WIDTH:, :])
        dx, dg = _rms_bwd(x_ref[...], g_ref[...], dh)
        dx = dxo_ref[...] + dx
        dx_ref[...] = dx
        dxh_ref[...] = (0.5 * dx).astype(bf16)
        dg_ref[...] += dg

    row = lambda c: pl.BlockSpec((tm, c), lambda i: (i, 0))
    return pl.pallas_call(
        body, name="mix_in_bwd", grid=(L // tm,),
        in_specs=[row(D_MODEL), row(D_MODEL), _full((1, D_MODEL)), row(S5_WIDTH), row(2 * CONV_WIDTH),
                  _full((IN_COLS, D_MODEL))],
        out_specs=[row(D_MODEL), _full((1, D_MODEL)), row(IN_COLS), row(D_MODEL)],
        out_shape=[S((L, D_MODEL), f32), S((1, D_MODEL), f32), S((L, IN_COLS), bf16), S((L, D_MODEL), bf16)],
        compiler_params=_cp("arbitrary"),
    )(dxo, x, g, du_s5, dv, w_in)


def _mix_out_bwd(dx, w_out, tm, ride=()):
    L = dx.shape[0]

    def body(dx_ref, w_ref, dys_ref, dyc_ref, dxb_ref):
        dxb = dx_ref[...].astype(bf16)
        dxb_ref[...] = dxb
        dys_ref[...] = _dot(dxb, w_ref[:S5_WIDTH, :], NT)
        dyc_ref[...] = _dot(dxb, w_ref[S5_WIDTH:, :], NT)

    row = lambda c: pl.BlockSpec((tm, c), lambda i: (i, 0))
    return _pallas(
        body, ride=ride, name="mix_out_bwd", grid=(L // tm,),
        in_specs=[row(D_MODEL), _full((D_MODEL, D_MODEL))],
        out_specs=[row(S5_WIDTH), row(CONV_WIDTH), row(D_MODEL)],
        out_shape=[S((L, S5_WIDTH), f32), S((L, CONV_WIDTH), f32), S((L, D_MODEL), bf16)],
        compiler_params=_cp("parallel"),
    )(dx, w_out)


def _s5_discretise(lam_re, lam_im, log_dt, b_re, b_im):
    dt = jnp.exp(log_dt)
    mag = jnp.exp(lam_re * dt)
    abar_re = mag * jnp.cos(lam_im * dt)
    abar_im = mag * jnp.sin(lam_im * dt)
    den = lam_re * lam_re + lam_im * lam_im
    num_re = abar_re - 1.0
    f_re = ((num_re * lam_re + abar_im * lam_im) / den)[:, None, :]
    f_im = ((abar_im * lam_re - num_re * lam_im) / den)[:, None, :]
    return abar_re, abar_im, f_re * b_re - f_im * b_im, f_re * b_im + f_im * b_re


def _s5_params(lam_re, lam_im, log_dt, b_re, b_im):
    def body(lr, li, ld, br, bi, ar_ref, ai_ref, bbr_ref, bbi_ref):
        ar, ai, bbr, bbi = _s5_discretise(lr[...], li[...], ld[...], br[...], bi[...])
        ar_ref[...], ai_ref[...], bbr_ref[...], bbi_ref[...] = ar, ai, bbr, bbi

    gp = S((S5_GROUPS, S5_STATE), f32)
    gcp = S((S5_GROUPS, S5_GROUP_CH, S5_STATE), f32)
    return pl.pallas_call(body, name="s5_params", out_shape=[gp, gp, gcp, gcp])(lam_re, lam_im, log_dt, b_re, b_im)


def _s5_params_bwd(lam_re, lam_im, log_dt, b_re, b_im, d_ar, d_ai, d_bbr, d_bbi):
    def body(lr, li, ld, br, bi, car, cai, cbr, cbi, o_lr, o_li, o_ld, o_br, o_bi):
        _, vjp = jax.vjp(_s5_discretise, lr[...], li[...], ld[...], br[...], bi[...])
        o_lr[...], o_li[...], o_ld[...], o_br[...], o_bi[...] = vjp((car[...], cai[...], cbr[...], cbi[...]))

    gp = S((S5_GROUPS, S5_STATE), f32)
    gcp = S((S5_GROUPS, S5_GROUP_CH, S5_STATE), f32)
    return pl.pallas_call(body, name="s5_params_bwd", out_shape=[gp, gp, S((S5_GROUPS, 1), f32), gcp, gcp])(
        lam_re, lam_im, log_dt, b_re, b_im, d_ar, d_ai, d_bbr, d_bbi)


def _cmul(ar, ai, br, bi):
    return ar * br - ai * bi, ar * bi + ai * br


def _segment_starts(er, ei, ar, ai, steps, reverse):
    pr, pi = ar, ai
    n = 1
    while n < steps:
        pr, pi = _cmul(pr, pi, pr, pi)
        n *= 2
    assert n == steps
    row = lax.broadcasted_iota(jnp.int32, (SEGMENTS, SCAN_LANES), 0)
    hr = jnp.zeros((1, SCAN_LANES), f32)
    hi = jnp.zeros((1, SCAN_LANES), f32)
    out_r = jnp.zeros((SEGMENTS, SCAN_LANES), f32)
    out_i = jnp.zeros((SEGMENTS, SCAN_LANES), f32)
    order = range(SEGMENTS - 1, 0, -1) if reverse else range(0, SEGMENTS - 1)
    for r in order:
        qr, qi = _cmul(pr, pi, hr, hi)
        hr, hi = qr + er[r:r + 1, :], qi + ei[r:r + 1, :]
        nxt = r - 1 if reverse else r + 1
        out_r = jnp.where(row == nxt, hr, out_r)
        out_i = jnp.where(row == nxt, hi, out_i)
    return out_r, out_i


def _s5_read_bwd(dout, y_lin, u, d_skip, w_glu, b_glu, tm):
    L = u.shape[0]

    def body(do_ref, yl_ref, u_ref, d_ref, w_ref, b_ref, dyl_ref, du_ref, dd_ref, dw_ref, db_ref):
        @pl.when(pl.program_id(0) == 0)
        def _():
            dd_ref[...] = jnp.zeros_like(dd_ref)
            dw_ref[...] = jnp.zeros_like(dw_ref)
            db_ref[...] = jnp.zeros_like(db_ref)

        u, d, dout = u_ref[...], d_ref[...], do_ref[...]
        y, gelu_vjp = jax.vjp(_gelu, yl_ref[...] + d * u)
        yb = y.astype(bf16)
        sig = _sigmoid(_dot(yb, w_ref[...]) + b_ref[...])
        dz = dout * y * sig * (1.0 - sig)
        dzb = dz.astype(bf16)
        dy = dout * sig + _dot(dzb, w_ref[...], NT)
        (dyp,) = gelu_vjp(dy)
        dyl_ref[...] = dyp.astype(bf16)
        du_ref[...] = d * dyp
        dd_ref[...] += _rows8(dyp * u)
        db_ref[...] += _rows8(dz)
        dw_ref[...] += _dot(yb, dzb, TN)

    row = pl.BlockSpec((tm, S5_WIDTH), lambda i: (i, 0))
    vec = _full((1, S5_WIDTH))
    part = _full((8, S5_WIDTH))
    return pl.pallas_call(
        body, name="s5_read_bwd", grid=(L // tm,),
        in_specs=[row, row, row, vec, _full((S5_WIDTH, S5_WIDTH)), vec],
        out_specs=[row, row, part, _full((S5_WIDTH, S5_WIDTH)), part],
        out_shape=[S((L, S5_WIDTH), bf16), S((L, S5_WIDTH), f32), S((8, S5_WIDTH), f32),
                   S((S5_WIDTH, S5_WIDTH), f32), S((8, S5_WIDTH), f32)],
        compiler_params=_cp("arbitrary"),
    )(dout, y_lin, u, d_skip, w_glu, b_glu)


S5_CHUNK_CH = SCAN_LANES // S5_STATE * S5_GROUP_CH


def _s5_two_phase(L, bi):
    rows = bi * SEGMENTS
    nb = L // rows
    whole = pltpu.VMEM((L // SEGMENTS, SEGMENTS, SCAN_LANES), f32)
    mat = pl.BlockSpec((S5_CHUNK_CH, SCAN_LANES), lambda c, j: (c, c))
    vec = pl.BlockSpec((1, SCAN_LANES), lambda c, j: (0, c))
    tile = pl.BlockSpec((SEGMENTS, SCAN_LANES), lambda c, j: (0, c))
    return rows, nb, whole, mat, vec, tile


def _s5_forward(u, a_re, a_im, bb_re, bb_im, cc_re, cc_im, bi, ride=()):
    L = u.shape[0]
    rows, nb, whole, mat, vec, _ = _s5_two_phase(L, bi)

    def body(u_ref, ar_ref, ai_ref, br_ref, bi_ref, cr_ref, ci_ref, sr_ref, si_ref, yl_ref, hr_ref, hi_ref, dr_ref, di_ref):
        j = pl.program_id(1)
        ar = jnp.broadcast_to(ar_ref[...], (SEGMENTS, SCAN_LANES))
        ai = jnp.broadcast_to(ai_ref[...], (SEGMENTS, SCAN_LANES))

        @pl.when(j == 0)
        def _():
            hr_ref[...] = jnp.zeros_like(hr_ref)
            hi_ref[...] = jnp.zeros_like(hi_ref)

        @pl.when(j < nb)
        def _():
            base = j * bi
            ub = u_ref[...].astype(bf16)
            dr_ref[pl.ds(base, bi)] = _dot(ub, br_ref[...]).reshape(bi, SEGMENTS, SCAN_LANES)
            di_ref[pl.ds(base, bi)] = _dot(ub, bi_ref[...]).reshape(bi, SEGMENTS, SCAN_LANES)

            def step(i, c):
                pr, pi = _cmul(ar, ai, c[0], c[1])
                return pr + dr_ref[base + i], pi + di_ref[base + i]

            hr_ref[...], hi_ref[...] = lax.fori_loop(0, bi, step, (hr_ref[...], hi_ref[...]), unroll=4)

        @pl.when(j == nb - 1)
        def _():
            hr_ref[...], hi_ref[...] = _segment_starts(hr_ref[...], hi_ref[...], ar_ref[...], ai_ref[...], L // SEGMENTS, False)

        @pl.when(j >= nb)
        def _():
            base = (j - nb) * bi

            def step(i, c):
                pr, pi = _cmul(ar, ai, c[0], c[1])
                nr, nim = pr + dr_ref[base + i], pi + di_ref[base + i]
                dr_ref[base + i] = nr
                di_ref[base + i] = nim
                return nr, nim

            hr_ref[...], hi_ref[...] = lax.fori_loop(0, bi, step, (hr_ref[...], hi_ref[...]), unroll=4)
            sr = dr_ref[pl.ds(base, bi)].reshape(rows, SCAN_LANES).astype(bf16)
            si = di_ref[pl.ds(base, bi)].reshape(rows, SCAN_LANES).astype(bf16)
            sr_ref[...] = sr
            si_ref[...] = si
            yl_ref[...] = _dot(sr, cr_ref[...], NT) - _dot(si, ci_ref[...], NT)

    u_spec = pl.BlockSpec((rows, S5_CHUNK_CH), lambda c, j: (jnp.minimum(j, nb - 1), c))
    late = lambda width: pl.BlockSpec((rows, width), lambda c, j: (jnp.maximum(j - nb, 0), c))
    return _pallas(
        body, ride=ride, name="s5_forward", grid=(S5_LANES // SCAN_LANES, 2 * nb),
        in_specs=[u_spec, vec, vec, mat, mat, mat, mat],
        out_specs=[late(SCAN_LANES), late(SCAN_LANES), late(S5_CHUNK_CH)],
        out_shape=[S((L, S5_LANES), bf16)] * 2 + [S((L, S5_WIDTH), f32)],
        scratch_shapes=[pltpu.VMEM((SEGMENTS, SCAN_LANES), f32)] * 2 + [whole] * 2,
        compiler_params=_cp("parallel", "arbitrary"),
    )(u, a_re, a_im, bb_re, bb_im, cc_re, cc_im)


def _s5_backward(dy, u, du_skip, s_re, s_im, a_re, a_im, bb_re, bb_im, cc_re, cc_im, bi, ride=()):
    L = u.shape[0]
    rows, nb, whole, mat, vec, tile = _s5_two_phase(L, bi)
    per = rows // 16

    def body(dy_ref, u_ref, dus_ref, sr_ref, si_ref, pr_ref, pi_ref, lr_ref, li_ref, ar_ref, ai_ref, br_ref, bi_ref, cr_ref,
             ci_ref, du_ref, dar_ref, dai_ref, dbr_ref, dbi_ref, dcr_ref, dci_ref, hr_ref, hi_ref, gr_ref, gi_ref, fr_ref, fi_ref):
        j = pl.program_id(1)
        ar = jnp.broadcast_to(ar_ref[...], (SEGMENTS, SCAN_LANES))
        ai = jnp.broadcast_to(ai_ref[...], (SEGMENTS, SCAN_LANES))

        @pl.when(j == 0)
        def _():
            for ref in (hr_ref, hi_ref, dar_ref, dai_ref, dbr_ref, dbi_ref, dcr_ref, dci_ref):
                ref[...] = jnp.zeros_like(ref)

        @pl.when(j < nb)
        def _():
            base = (nb - 1 - j) * bi
            dy = dy_ref[...]
            gr_ref[pl.ds(base, bi)] = _dot(dy, cr_ref[...]).reshape(bi, SEGMENTS, SCAN_LANES)
            gi_ref[pl.ds(base, bi)] = (-_dot(dy, ci_ref[...])).reshape(bi, SEGMENTS, SCAN_LANES)

            def step(n, c):
                i = base + bi - 1 - n
                qr, qi = _cmul(ar, ai, c[0], c[1])
                return qr + gr_ref[i], qi + gi_ref[i]

            hr_ref[...], hi_ref[...] = lax.fori_loop(0, bi, step, (hr_ref[...], hi_ref[...]), unroll=4)

        @pl.when(j == nb - 1)
        def _():
            hr_ref[...], hi_ref[...] = _segment_starts(hr_ref[...], hi_ref[...], ar_ref[...], ai_ref[...], L // SEGMENTS, True)

        @pl.when(j >= nb)
        def _():
            blk = 2 * nb - 1 - j
            base = blk * bi
            sr, si = sr_ref[...], si_ref[...]
            fr_ref[...] = sr.astype(f32).reshape(bi, SEGMENTS, SCAN_LANES)
            fi_ref[...] = si.astype(f32).reshape(bi, SEGMENTS, SCAN_LANES)

            def step(n, c):
                i = bi - 1 - n
                gr, gi, accr, acci = c
                qr, qi = _cmul(ar, ai, gr, gi)
                gr, gi = qr + gr_ref[base + i], qi + gi_ref[base + i]
                gr_ref[base + i] = gr
                gi_ref[base + i] = gi
                pr, pi = fr_ref[i - 1], fi_ref[i - 1]
                return gr, gi, accr + (gr * pr + gi * pi), acci + (gi * pr - gr * pi)

            gr, gi, accr, acci = lax.fori_loop(0, bi - 1, step, (hr_ref[...], hi_ref[...], dar_ref[...], dai_ref[...]), unroll=3)
            qr, qi = _cmul(ar, ai, gr, gi)
            gr, gi = qr + gr_ref[base], qi + gi_ref[base]
            gr_ref[base] = gr
            gi_ref[base] = gi
            hr_ref[...], hi_ref[...] = gr, gi
            row = lax.broadcasted_iota(jnp.int32, (SEGMENTS, SCAN_LANES), 0)
            older = lambda ref: ref[...].astype(f32)[SEGMENTS:, :]
            wrap_r = jnp.where(row == 0, 0.0, pltpu.roll(older(lr_ref), 1, 0))
            wrap_i = jnp.where(row == 0, 0.0, pltpu.roll(older(li_ref), 1, 0))
            pr = jnp.where(blk == 0, wrap_r, older(pr_ref))
            pi = jnp.where(blk == 0, wrap_i, older(pi_ref))
            dar_ref[...] = accr + gr * pr + gi * pi
            dai_ref[...] = acci + gi * pr - gr * pi

            g_re = gr_ref[pl.ds(base, bi)].reshape(rows, SCAN_LANES).astype(bf16)
            g_im = gi_ref[pl.ds(base, bi)].reshape(rows, SCAN_LANES).astype(bf16)
            ub = u_ref[...].astype(bf16)
            dy = dy_ref[...]
            du_ref[...] = dus_ref[...] + _dot(g_re, br_ref[...], NT) + _dot(g_im, bi_ref[...], NT)
            dbr_ref[...] += _dot(ub, g_re, TN)
            dbi_ref[...] += _dot(ub, g_im, TN)
            dcr_ref[...] += _dot(dy, sr, TN)
            dci_ref[...] -= _dot(dy, si, TN)

    block = lambda c, j: jnp.where(j < nb, nb - 1 - j, 2 * nb - 1 - j)
    late_block = lambda c, j: jnp.minimum(2 * nb - 1 - j, nb - 1)
    both = pl.BlockSpec((rows, S5_CHUNK_CH), lambda c, j: (block(c, j), c))
    chan = pl.BlockSpec((rows, S5_CHUNK_CH), lambda c, j: (late_block(c, j), c))
    state = pl.BlockSpec((rows, SCAN_LANES), lambda c, j: (late_block(c, j), c))
    prev = pl.BlockSpec((16, SCAN_LANES), lambda c, j: (jnp.maximum(late_block(c, j) * per - 1, 0), c))
    last = pl.BlockSpec((16, SCAN_LANES), lambda c, j: (L // 16 - 1, c))
    grad = pl.BlockSpec((S5_CHUNK_CH, SCAN_LANES), lambda c, j: (c, 0))
    return _pallas(
        body, ride=ride, name="s5_backward", grid=(S5_LANES // SCAN_LANES, 2 * nb),
        in_specs=[both, chan, chan, state, state, prev, prev, last, last, vec, vec, mat, mat, mat, mat],
        out_specs=[chan, tile, tile, grad, grad, grad, grad],
        out_shape=[S((L, S5_WIDTH), f32)] + [S((SEGMENTS, S5_LANES), f32)] * 2 + [S((S5_WIDTH, SCAN_LANES), f32)] * 4,
        scratch_shapes=[pltpu.VMEM((SEGMENTS, SCAN_LANES), f32)] * 2 + [whole] * 2 + [pltpu.VMEM((bi, SEGMENTS, SCAN_LANES), f32)] * 2,
        compiler_params=_cp("parallel", "arbitrary"),
    )(dy, u, du_skip, s_re, s_im, s_re, s_im, s_re, s_im, a_re, a_im, bb_re, bb_im, cc_re, cc_im)


def _s5_gate(y_lin, u, d_skip, w_glu, b_glu, tm, ride=()):
    L = u.shape[0]

    def body(yl_ref, u_ref, d_ref, w_ref, b_ref, o_ref):
        y = _gelu(yl_ref[...] + d_ref[...] * u_ref[...])
        z = _dot(y.astype(bf16), w_ref[...]) + b_ref[...]
        o_ref[...] = (y * _sigmoid(z)).astype(bf16)

    row = pl.BlockSpec((tm, S5_WIDTH), lambda i: (i, 0))
    vec = _full((1, S5_WIDTH))
    return _pallas(
        body, ride=ride, name="s5_gate", grid=(L // tm,),
        in_specs=[row, row, vec, _full((S5_WIDTH, S5_WIDTH)), vec],
        out_specs=row, out_shape=S((L, S5_WIDTH), bf16),
        compiler_params=_cp("parallel"),
    )(y_lin, u, d_skip, w_glu, b_glu)


def _group_mean(x, avg):
    return _dot(x.astype(bf16), avg)


def _conv_act(zn, ln_g, ln_b):
    t = zn * ln_g + ln_b
    return t * _sigmoid(t)


def _glu_padded(v_ref, halo_ref, zpad_ref, tm):
    v = v_ref[...]
    vh = halo_ref[...]
    zh = vh[:, :CONV_WIDTH] * _sigmoid(vh[:, CONV_WIDTH:])
    zpad_ref[:CONV_HALO, :] = jnp.where(pl.program_id(0) > 0, zh, 0.0)
    zpad_ref[CONV_HALO:CONV_HALO + tm, :] = v[:, :CONV_WIDTH] * _sigmoid(v[:, CONV_WIDTH:])
    zpad_ref[CONV_HALO + tm:, :] = jnp.zeros((8, CONV_WIDTH), f32)


def _shifted(pad_ref, sh_ref, tm):
    for b in range(8):
        sh_ref[b] = pad_ref[pl.ds(b, tm + CONV_HALO), :]


def _window(sh_ref, r0, off, rows):
    return sh_ref[off % 8, pl.ds(pl.multiple_of(r0 + 8 * (off // 8), 8), rows), :]


def _tap_sum(w_ref, sh_ref, taps, out_ref, tm, bias):
    def chunk(c, carry):
        r0 = pl.multiple_of(c * CONV_ROWS, CONV_ROWS)
        acc = jnp.zeros((CONV_ROWS, CONV_WIDTH), f32) + bias
        for k, off in taps:
            acc = acc + w_ref[k:k + 1, :] * _window(sh_ref, r0, off, CONV_ROWS)
        out_ref[pl.ds(r0, CONV_ROWS), :] = acc
        return carry

    lax.fori_loop(0, tm // CONV_ROWS, chunk, 0)


FWD_TAPS = [(k, CONV_HALO - (CONV_K - 1) + k) for k in range(CONV_K)]
BWD_TAPS = [(k, CONV_K - 1 - k) for k in range(CONV_K)]


def _conv_specs(tm):
    per = tm // CONV_HALO
    vrow = pl.BlockSpec((tm, 2 * CONV_WIDTH), lambda i: (i, 0))
    vhalo = pl.BlockSpec((CONV_HALO, 2 * CONV_WIDTH), lambda i: (jnp.maximum(i * per - 1, 0), 0))
    return vrow, vhalo


def _conv_scratch(tm):
    return [pltpu.VMEM((tm + CONV_HALO + 8, CONV_WIDTH), f32), pltpu.VMEM((8, tm + CONV_HALO, CONV_WIDTH), f32)]


def _conv_fwd(v, w_dw, b_dw, ln_g, ln_b, avg, tm, ride=()):
    L = v.shape[0]

    def body(v_ref, halo_ref, w_ref, b_ref, g_ref, bb_ref, avg_ref, o_ref, zc_ref, zpad_ref, zs_ref):
        _glu_padded(v_ref, halo_ref, zpad_ref, tm)
        _shifted(zpad_ref, zs_ref, tm)
        _tap_sum(w_ref, zs_ref, FWD_TAPS, zc_ref, tm, b_ref[...])
        zc = zc_ref[...]
        xc = zc - _group_mean(zc, avg_ref[...])
        zn = xc * lax.rsqrt(_group_mean(xc * xc, avg_ref[...]) + EPS)
        o_ref[...] = _conv_act(zn, g_ref[...], bb_ref[...]).astype(bf16)

    vrow, vhalo = _conv_specs(tm)
    vec = _full((1, CONV_WIDTH))
    row = pl.BlockSpec((tm, CONV_WIDTH), lambda i: (i, 0))
    return _pallas(
        body, ride=ride, name="conv_fwd", grid=(L // tm,),
        in_specs=[vrow, vhalo, _full((CONV_HALO, CONV_WIDTH)), vec, vec, vec, _full((CONV_WIDTH, CONV_WIDTH))],
        out_specs=[row, row], out_shape=[S((L, CONV_WIDTH), bf16), S((L, CONV_WIDTH), f32)],
        scratch_shapes=_conv_scratch(tm),
        compiler_params=_cp("arbitrary"),
    )(v, v, w_dw, b_dw, ln_g, ln_b, avg)


def _conv_bwd_norm(dout, zc, ln_g, ln_b, avg, tm):
    L = zc.shape[0]

    def body(do_ref, zc_ref, g_ref, bb_ref, avg_ref, dzc_ref, dg_ref, db_ref, dbd_ref):
        @pl.when(pl.program_id(0) == 0)
        def _():
            dg_ref[...] = jnp.zeros_like(dg_ref)
            db_ref[...] = jnp.zeros_like(db_ref)
            dbd_ref[...] = jnp.zeros_like(dbd_ref)

        avg = avg_ref[...]
        zc = zc_ref[...]
        xc = zc - _group_mean(zc, avg)
        rstd = lax.rsqrt(_group_mean(xc * xc, avg) + EPS)
        xhat = xc * rstd
        _, act_vjp = jax.vjp(_conv_act, xhat, g_ref[...], bb_ref[...])
        dxhat, dg, db = act_vjp(do_ref[...])
        dzc = rstd * (dxhat - _group_mean(dxhat, avg) - xhat * _group_mean(dxhat * xhat, avg))
        dzc_ref[...] = dzc
        dg_ref[0:1, :] += dg
        db_ref[0:1, :] += db
        dbd_ref[...] += _rows8(dzc)

    vec = _full((1, CONV_WIDTH))
    row = pl.BlockSpec((tm, CONV_WIDTH), lambda i: (i, 0))
    part = _full((8, CONV_WIDTH))
    return pl.pallas_call(
        body, name="conv_bwd_norm", grid=(L // tm,),
        in_specs=[row, row, vec, vec, _full((CONV_WIDTH, CONV_WIDTH))],
        out_specs=[row, part, part, part],
        out_shape=[S((L, CONV_WIDTH), f32)] + [S((8, CONV_WIDTH), f32)] * 3,
        compiler_params=_cp("arbitrary"),
    )(dout, zc, ln_g, ln_b, avg)


def _conv_bwd_taps(dzc, v, w_dw, tm, ride=()):
    L = v.shape[0]
    nt = L // tm
    per = tm // CONV_HALO

    def body(d_ref, dn_ref, v_ref, w_ref, dv_ref, dw_ref, dpad_ref, ds_ref, dz_ref, z_ref):
        i = pl.program_id(0)

        @pl.when(i == 0)
        def _():
            dw_ref[...] = jnp.zeros_like(dw_ref)

        v = v_ref[...]
        sig = _sigmoid(v[:, CONV_WIDTH:])
        z_ref[...] = v[:, :CONV_WIDTH] * sig
        dpad_ref[:tm, :] = d_ref[...]
        dpad_ref[tm:tm + CONV_HALO, :] = jnp.where(i < nt - 1, dn_ref[...], 0.0)
        dpad_ref[tm + CONV_HALO:, :] = jnp.zeros((8, CONV_WIDTH), f32)
        _shifted(dpad_ref, ds_ref, tm)
        _tap_sum(w_ref, ds_ref, BWD_TAPS, dz_ref, tm, 0.0)

        for first in range(0, CONV_K, 8):
            taps = BWD_TAPS[first:first + 8]

            def chunk(c, accs, taps=taps):
                r0 = pl.multiple_of(c * 8, 8)
                z = z_ref[pl.ds(r0, 8), :]
                return tuple(acc + z * _window(ds_ref, r0, off, 8) for acc, (_, off) in zip(accs, taps))

            accs = lax.fori_loop(0, tm // 8, chunk, tuple(jnp.zeros((8, CONV_WIDTH), f32) for _ in taps), unroll=2)
            for acc, (k, _) in zip(accs, taps):
                dw_ref[k] += acc

        dz = dz_ref[...]
        dv_ref[:, :CONV_WIDTH] = dz * sig
        dv_ref[:, CONV_WIDTH:] = dz * v[:, :CONV_WIDTH] * sig * (1.0 - sig)

    vrow, _ = _conv_specs(tm)
    row = pl.BlockSpec((tm, CONV_WIDTH), lambda i: (i, 0))
    nxt = pl.BlockSpec((CONV_HALO, CONV_WIDTH), lambda i: (jnp.minimum((i + 1) * per, nt * per - 1), 0))
    return _pallas(
        body, ride=ride, name="conv_bwd_taps", grid=(nt,),
        in_specs=[row, nxt, vrow, _full((CONV_HALO, CONV_WIDTH))],
        out_specs=[vrow, _full((CONV_HALO, 8, CONV_WIDTH))],
        out_shape=[S((L, 2 * CONV_WIDTH), f32), S((CONV_HALO, 8, CONV_WIDTH), f32)],
        scratch_shapes=_conv_scratch(tm) + [pltpu.VMEM((tm, CONV_WIDTH), f32)] * 2,
        compiler_params=_cp("arbitrary"),
    )(dzc, dzc, v, w_dw)


def _to_segments(a):
    L, c = a.shape
    return a.reshape(SEGMENTS, L // SEGMENTS, c).transpose(1, 0, 2).reshape(L, c)


def _from_segments(a):
    L, c = a.shape
    return a.reshape(L // SEGMENTS, SEGMENTS, c).transpose(1, 0, 2).reshape(L, c)


def _block_diag(ms):
    n = len(ms)

    def body(*refs):
        for a in range(n):
            out = refs[n + a]
            out[...] = jnp.zeros_like(out)
            for g in range(S5_GROUPS):
                rows = slice(g * S5_GROUP_CH, (g + 1) * S5_GROUP_CH)
                out[rows, g * S5_STATE:(g + 1) * S5_STATE] = refs[a][rows, :].astype(bf16)

    return pl.pallas_call(body, name="s5_block_diag", out_shape=[S((S5_WIDTH, S5_LANES), bf16)] * n,
                          compiler_params=pltpu.CompilerParams(vmem_limit_bytes=VMEM_LIMIT))(
        *[m.reshape(S5_WIDTH, S5_STATE) for m in ms])


def _diag_blocks(ms):
    n = len(ms)
    per_chunk = SCAN_LANES // S5_STATE

    def body(*refs):
        for a in range(n):
            for g in range(S5_GROUPS):
                rows = slice(g * S5_GROUP_CH, (g + 1) * S5_GROUP_CH)
                at = g % per_chunk * S5_STATE
                refs[n + a][rows, :] = refs[a][rows, at:at + S5_STATE]

    out = pl.pallas_call(body, name="s5_diag_blocks", out_shape=[S((S5_WIDTH, S5_STATE), f32)] * n,
                         compiler_params=pltpu.CompilerParams(vmem_limit_bytes=VMEM_LIMIT))(*ms)
    return [o.reshape(S5_GROUPS, S5_GROUP_CH, S5_STATE) for o in out]


class _NoExchanges:
    def before(self, point):
        return ()

    def after(self, point):
        pass

    def alone(self, point):
        pass


def _ffn_block(x, p, tag, tm, sched, head=None, mixed=None):
    point = tag + "_up"
    h, dadg, dadu, a, *x_in = _ffn_up(x, p[tag + "_norm"], p[tag + "_w_gate"], p[tag + "_w_up"], tm, tag, mixed,
                                      ride=sched.before(point))
    sched.after(point)
    x, = x_in or [x]
    if head is None:
        out = _ffn_down(x, a, p[tag + "_w_down"], tm, tag, ride=sched.before(tag + "_down"))
        sched.after(tag + "_down")
    else:
        out = _ffn_down_loss(x, a, p[tag + "_w_down"], *head, tm, tag)
    return out, (x, h, dadg, dadu, a)


def _ffn_block_bwd(dxo, x, p, tag, saved, tm, grads, sched, parts=1, dxh=None):
    _, h, dadg, dadu, a = saved

    def weight_grad(which, lhs, rhs):
        point = tag + "_dw_" + which
        grads[tag + "_w_" + which] = _mm_tn(lhs, rhs, bf16, point, ride=sched.before(point))
        sched.after(point)

    if dxh is not None:
        weight_grad("down", a, dxh)
    dgate, dup, own_dxh = _ffn_bwd_act(dxo, p[tag + "_w_down"], dadg, dadu, tm, tag, ride=sched.before(tag + "_bwd_act"))
    sched.after(tag + "_bwd_act")
    weight_grad("gate", dgate, h)
    weight_grad("up", dup, h)
    if dxh is None:
        weight_grad("down", a, own_dxh)
    tiles = x.shape[0] // tm
    dx, dgs = None, []
    for k in range(parts):
        point = tag + "_bwd_in" + ("_%d" % k) * (parts > 1)
        dx, dg = _ffn_bwd_in(dxo, x, p[tag + "_norm"], dgate, dup, p[tag + "_w_gate"], p[tag + "_w_up"], tm, point,
                             tiles=(k * tiles // parts, tiles // parts), into=dx, ride=sched.before(point))
        sched.after(point)
        dgs.append(dg)
    grads[tag + "_norm"] = functools.reduce(jnp.add, dgs)
    return dx


def _local_step(x, target, p, grads, sched):
    L = x.shape[0]
    tm = min(512, L // 2)
    ni = L // SEGMENTS
    bi = min(64, ni)

    def carried(point, fn, *args):
        out = fn(*args, ride=sched.before(point))
        sched.after(point)
        return out

    x1, saved1 = _ffn_block(x, p, "ffn1", tm, sched)

    h2, u_s5, v = _mix_in(x1, p["mix_norm"], p["w_in"], tm)
    s5_in = (p["s5_lam_re"], p["s5_lam_im"], p["s5_log_dt"].reshape(S5_GROUPS, 1), p["s5_b_re"], p["s5_b_im"])
    abar_re, abar_im, bbar_re, bbar_im = _s5_params(*s5_in)
    a_re, a_im = abar_re.reshape(1, S5_LANES), abar_im.reshape(1, S5_LANES)
    bb_re, bb_im, cc_re, cc_im = _block_diag([bbar_re, bbar_im, p["s5_c_re"], p["s5_c_im"]])
    u_seg = _to_segments(u_s5)
    s_re, s_im, y_lin = carried("s5_forward", _s5_forward, u_seg, a_re, a_im, bb_re, bb_im, cc_re, cc_im, bi)
    y_s5 = _from_segments(_s5_gate(y_lin, u_seg, p["s5_d"], p["s5_w_glu"], p["s5_b_glu"], tm))
    w_dw = jnp.pad(p["conv_w_dw"], ((0, CONV_HALO - CONV_K), (0, 0)))
    heads = jnp.arange(CONV_WIDTH) // CONV_HEAD
    avg = ((heads[:, None] == heads[None, :]).astype(f32) / CONV_HEAD).astype(bf16)
    y_conv, zc = carried("conv_fwd", _conv_fwd, v, w_dw, p["conv_b_dw"], p["conv_ln_g"], p["conv_ln_b"], avg, tm)

    (dx3, grads["final_norm"], loss_terms), saved2 = _ffn_block(
        x1, p, "ffn2", tm, sched, head=(target, p["final_norm"].reshape(1, D_MODEL)), mixed=(y_s5, y_conv, p["w_out"]))
    x2 = saved2[0]
    grads["loss_terms"] = loss_terms

    dx2 = _ffn_block_bwd(dx3, x2, p, "ffn2", saved2, tm, grads, sched)

    dy_s5, dy_conv, dx2b = carried("mix_out_bwd", _mix_out_bwd, dx2, p["w_out"], tm)
    grads["w_out"] = jnp.concatenate([_mm_tn(y_s5, dx2b, bf16, "dw_out_s5"), _mm_tn(y_conv, dx2b, bf16, "dw_out_conv")], axis=0)
    dy_lin, du_skip, dd8, grads["s5_w_glu"], dbg8 = _s5_read_bwd(
        _to_segments(dy_s5), y_lin, u_seg, p["s5_d"], p["s5_w_glu"], p["s5_b_glu"], tm)
    grads["s5_d"] = dd8.sum(axis=0, keepdims=True)
    grads["s5_b_glu"] = dbg8.sum(axis=0, keepdims=True)
    du_seg, da_re8, da_im8, dbb_re, dbb_im, dcc_re, dcc_im = carried(
        "s5_backward", _s5_backward, dy_lin, u_seg, du_skip, s_re, s_im, a_re, -a_im, bb_re, bb_im, cc_re, cc_im, bi)
    d_abar = lambda a8: a8.sum(axis=0).reshape(S5_GROUPS, S5_STATE)
    grads["s5_c_re"], grads["s5_c_im"], d_bbr, d_bbi = _diag_blocks([dcc_re, dcc_im, dbb_re, dbb_im])
    d_lr, d_li, d_ld, d_br, d_bi = _s5_params_bwd(*s5_in, d_abar(da_re8), d_abar(da_im8), d_bbr, d_bbi)
    grads["s5_lam_re"], grads["s5_lam_im"], grads["s5_log_dt"] = d_lr, d_li, d_ld.reshape(1, S5_GROUPS)
    grads["s5_b_re"], grads["s5_b_im"] = d_br, d_bi
    dzc, dlg8, dlb8, dbd8 = _conv_bwd_norm(dy_conv, zc, p["conv_ln_g"], p["conv_ln_b"], avg, tm)
    grads["conv_ln_g"] = dlg8.sum(axis=0, keepdims=True)
    grads["conv_ln_b"] = dlb8.sum(axis=0, keepdims=True)
    grads["conv_b_dw"] = dbd8.sum(axis=0, keepdims=True)
    dv, dw8 = carried("conv_bwd_taps", _conv_bwd_taps, dzc, v, w_dw, tm)
    grads["conv_w_dw"] = dw8.sum(axis=1)[:CONV_K]
    dx1, grads["mix_norm"], dub, dx1h = _mix_in_bwd(dx2, x1, p["mix_norm"], _from_segments(du_seg), dv, p["w_in"], tm)
    grads["w_in"] = _mm_tn(dub, h2, bf16, "dw_in")

    dx0 = _ffn_block_bwd(dx1, x, p, "ffn1", saved1, tm, grads, sched, parts=min(2, L // tm), dxh=dx1h)
    sched.alone("tail")
    return loss_terms, dx0


MESH = pl.DeviceIdType.MESH
ANY = pl.BlockSpec(memory_space=pl.ANY)


def _place():
    return lax.axis_index("x"), lax.axis_index("y"), lax.axis_index("c")


class _Exchange:
    def __init__(self, ins, out_shape, sems, start, finish):
        self.ins, self.out_shape, self.sems, self.start, self.finish = list(ins), list(out_shape), list(sems), start, finish
        self.out = None


def _pallas(body, *, ride=(), **kw):
    if not ride:
        return pl.pallas_call(body, **kw)

    def run(*args):
        out_shape = kw.get("out_shape", [])
        single = not isinstance(out_shape, (list, tuple))
        shapes = [out_shape] if single else list(out_shape)
        out_specs = [kw["out_specs"]] if single else list(kw.get("out_specs", []))
        grid = tuple(kw.get("grid", ()))
        scratch = list(kw.get("scratch_shapes", ()))
        n_in, n_out, n_scr = len(args), len(shapes), len(scratch)
        r_in = [len(e.ins) for e in ride]
        r_out = [len(e.out_shape) for e in ride]
        r_sem = [len(e.sems) for e in ride]

        def wrapped(*refs):
            own_in, refs = refs[:n_in], refs[n_in:]
            ex_in, refs = refs[:sum(r_in)], refs[sum(r_in):]
            own_out, refs = refs[:n_out], refs[n_out:]
            ex_out, refs = refs[:sum(r_out)], refs[sum(r_out):]
            own_scr, ex_sem = refs[:n_scr], refs[n_scr:]
            parts = []
            for e, ni, no, ns in zip(ride, r_in, r_out, r_sem):
                parts.append((e, ex_in[:ni], ex_out[:no], ex_sem[:ns]))
                ex_in, ex_out, ex_sem = ex_in[ni:], ex_out[no:], ex_sem[ns:]

            def at(step):
                def go():
                    for e, i, o, s in parts:
                        getattr(e, step)(i, o, s)
                if grid:
                    ids = [pl.program_id(d) for d in range(len(grid))]
                    when = [i == (0 if step == "start" else g - 1) for i, g in zip(ids, grid)]
                    pl.when(functools.reduce(lambda a, b: a & b, when))(go)
                else:
                    go()

            at("start")
            if body is not None:
                body(*own_in, *own_out, *own_scr)
            at("finish")

        outs = pl.pallas_call(
            wrapped, name=kw["name"], grid=grid,
            in_specs=list(kw.get("in_specs", [])) + [ANY] * sum(r_in),
            out_specs=out_specs + [ANY] * sum(r_out),
            out_shape=shapes + [s for e in ride for s in e.out_shape],
            scratch_shapes=scratch + [s for e in ride for s in e.sems],
            input_output_aliases=kw.get("input_output_aliases", {}),
            compiler_params=_cp(*["arbitrary"] * len(grid)),
        )(*args, *[a for e in ride for a in e.ins])
        own, rest = outs[:n_out], outs[n_out:]
        for e, no in zip(ride, r_out):
            e.out, rest = list(rest[:no]), rest[no:]
        return own[0] if single else own

    return run


def _exchange(ride, name):
    _pallas(None, ride=ride, name=name)()


def _gather(arrs):
    n = len(arrs)

    def copies(ins, outs, sems):
        send_sems, recv_sems, local_sems = sems
        x, y, c = _place()
        me, sibling = (x, y, c), (x, y, 1 - c)
        chips = [(1 - x, y), (x, 1 - y), (1 - x, 1 - y)]

        def place(a, block):
            return outs[a].at[block]

        def copy(a, k, block, to, src=None):
            px, py, pc = block
            dst = place(a, 4 * px + 2 * py + pc)
            return pltpu.make_async_remote_copy(
                src_ref=dst if src is None else src, dst_ref=dst, send_sem=send_sems.at[7 * a + k],
                recv_sem=recv_sems.at[7 * a + k], device_id=to, device_id_type=MESH)

        def own():
            local = [pltpu.make_async_copy(ins[a], place(a, 4 * x + 2 * y + c), local_sems.at[a]) for a in range(n)]
            remote = []
            for a in range(n):
                remote.append(copy(a, 0, me, sibling, src=ins[a]))
                remote += [copy(a, 1 + j, me, (*chip, c), src=ins[a]) for j, chip in enumerate(chips)]
            return local, remote

        return c, me, sibling, chips, copy, own

    def start(ins, outs, sems):
        local, remote = copies(ins, outs, sems)[-1]()
        for cp in local + remote:
            cp.start()

    def finish(ins, outs, sems):
        c, me, sibling, chips, copy, own = copies(ins, outs, sems)
        passed = []
        for j, chip in enumerate(chips):
            for a in range(n):
                copy(a, 1 + j, (*chip, c), me).wait_recv()
                passed.append(copy(a, 4 + j, (*chip, c), sibling))
                passed[-1].start()
        for a in range(n):
            copy(a, 0, sibling, me).wait_recv()
            for j, chip in enumerate(chips):
                copy(a, 4 + j, (*chip, 1 - c), me).wait_recv()
        local, remote = own()
        for cp in remote + passed:
            cp.wait_send()
        for cp in local:
            cp.wait()

    dma = pltpu.SemaphoreType.DMA
    shapes = [S((N_DEV, *a.shape), a.dtype) for a in arrs]
    return _Exchange(arrs, shapes, [dma((7 * n,)), dma((7 * n,)), dma((n,))], start, finish)


def _swap_with_sibling(gs):
    n = len(gs)

    def copies(ins, outs, sems):
        x, y, c = _place()
        return [pltpu.make_async_remote_copy(
            src_ref=ins[a].at[:, 1 - c], dst_ref=outs[a], send_sem=sems[0].at[a], recv_sem=sems[1].at[a],
            device_id=(x, y, 1 - c), device_id_type=MESH) for a in range(n)]

    def start(ins, outs, sems):
        for cp in copies(ins, outs, sems):
            cp.start()

    def finish(ins, outs, sems):
        for cp in copies(ins, outs, sems):
            cp.wait()

    dma = pltpu.SemaphoreType.DMA
    return _Exchange(gs, [S((N_CHIP, *g.shape[2:]), g.dtype) for g in gs], [dma((n,)), dma((n,))], start, finish)


def _swap_with_chips(ps):
    n = len(ps)

    def copies(ins, outs, sems):
        x, y, c = _place()
        q = 2 * x + y
        peers = [(x, 1 - y), (1 - x, y), (1 - x, 1 - y)]

        def copy(a, j, slot_from, slot_to):
            px, py = peers[j]
            return pltpu.make_async_remote_copy(
                src_ref=ins[a].at[slot_from], dst_ref=outs[a].at[slot_to], send_sem=sems[0].at[3 * a + j],
                recv_sem=sems[1].at[3 * a + j], device_id=(px, py, c), device_id_type=MESH)

        sends = lambda: [copy(a, j, 2 * peers[j][0] + peers[j][1], q) for a in range(n) for j in range(3)]
        lands = lambda: [copy(a, j, q, 2 * peers[j][0] + peers[j][1]) for a in range(n) for j in range(3)]
        return sends, lands

    def start(ins, outs, sems):
        for cp in copies(ins, outs, sems)[0]():
            cp.start()

    def finish(ins, outs, sems):
        sends, lands = copies(ins, outs, sems)
        for cp in lands():
            cp.wait_recv()
        for cp in sends():
            cp.wait_send()

    dma = pltpu.SemaphoreType.DMA
    return _Exchange(ps, [S(p.shape, p.dtype) for p in ps], [dma((3 * n,)), dma((3 * n,))], start, finish)


def _row_tile(rows, cols, itemsize):
    t = rows
    while t * cols * itemsize > (1 << 20) and t % 32 == 0:
        t //= 2
    return t


def _add_sibling(g4, st, core, name):
    _, R, C = st.shape
    tr = _row_tile(R, C, 1)

    def body(c_ref, g_ref, s_ref, o_ref):
        o_ref[...] = (g_ref[...].astype(f32) + s_ref[...].astype(f32)).astype(bf16)

    mine = pl.BlockSpec((None, None, tr, C), lambda q, i, c: (q, c[0], i, 0))
    return pl.pallas_call(
        body, name=name,
        grid_spec=pltpu.PrefetchScalarGridSpec(
            num_scalar_prefetch=1, grid=(N_CHIP, R // tr),
            in_specs=[mine,
                      pl.BlockSpec((None, tr, C), lambda q, i, c: (q, i, 0))],
            out_specs=pl.BlockSpec((None, tr, C), lambda q, i, c: (q, i, 0))),
        out_shape=S((N_CHIP, R, C), bf16),
        compiler_params=_cp("parallel", "parallel"),
    )(core, g4, st)


def _adamw(w, g, m, v):
    m = B1 * m + (1.0 - B1) * g
    v = B2 * v + (1.0 - B2) * (g * g)
    m_hat = m / (1.0 - B1 ** STEP)
    v_hat = v / (1.0 - B2 ** STEP)
    return -LR * (m_hat / (jnp.sqrt(v_hat) + ADAM_EPS) + WD * w), m, v


def _adam_sharded(items, slots, name):
    n = len(items)
    R, C = items[0][0].shape
    tr = _row_tile(R, C, 4 * n)

    def body(s_ref, *refs):
        ins, outs = refs[:7 * n], refs[7 * n:]
        for k in range(n):
            w_ref, m_ref, v_ref, p_ref, a_ref, b_ref, c_ref = ins[7 * k:7 * k + 7]
            g = p_ref[...].astype(f32) + a_ref[...].astype(f32) + b_ref[...].astype(f32) + c_ref[...].astype(f32)
            outs[4 * k][...] = g
            outs[4 * k + 1][...], outs[4 * k + 2][...], outs[4 * k + 3][...] = _adamw(w_ref[...], g, m_ref[...], v_ref[...])

    shard = pl.BlockSpec((tr, C), lambda i, s: (i, 0))
    slot = lambda k: pl.BlockSpec((None, tr, C), lambda i, s: (s[k], i, 0))
    out = pl.pallas_call(
        body, name=name,
        grid_spec=pltpu.PrefetchScalarGridSpec(
            num_scalar_prefetch=1, grid=(R // tr,),
            in_specs=[shard, shard, shard, slot(0), slot(1), slot(2), slot(3)] * n,
            out_specs=[shard] * (4 * n)),
        out_shape=[S((R, C), f32)] * (4 * n),
        compiler_params=_cp("parallel"),
    )(slots, *[a for w, m, v, part, got in items for a in (w, m, v, part, got, got, got)])
    return [out[4 * k:4 * k + 4] for k in range(n)]


def _adam_replicated(items, loss_terms, name):
    n = len(items)
    has_loss = loss_terms is not None

    def total(ref):
        g = ref[0]
        for d in range(1, N_DEV):
            g = g + ref[d]
        return g

    def body(*refs):
        ins, outs = refs[:4 * n + has_loss], refs[4 * n + has_loss:]
        for i in range(n):
            w_ref, m_ref, v_ref, g_ref = ins[4 * i:4 * i + 4]
            g = total(g_ref)
            outs[4 * i][...] = g
            outs[4 * i + 1][...], outs[4 * i + 2][...], outs[4 * i + 3][...] = _adamw(w_ref[...], g, m_ref[...], v_ref[...])
        if has_loss:
            outs[-1][...] = jnp.sum(total(ins[-1]), keepdims=True)

    flat = [a for item in items for a in item] + ([loss_terms] if has_loss else [])
    shapes = [S(item[0].shape, f32) for item in items for _ in range(4)] + ([S((1, 1), f32)] if has_loss else [])
    out = pl.pallas_call(body, name=name, out_shape=shapes,
                         compiler_params=pltpu.CompilerParams(vmem_limit_bytes=VMEM_LIMIT))(*flat)
    return [out[4 * i:4 * i + 4] for i in range(n)], (out[-1] if has_loss else None)


WEIGHTS = ["ffn1_norm", "ffn1_w_gate", "ffn1_w_up", "ffn1_w_down", "mix_norm", "w_in", "s5_lam_re", "s5_lam_im", "s5_log_dt",
           "s5_b_re", "s5_b_im", "s5_c_re", "s5_c_im", "s5_d", "s5_w_glu", "s5_b_glu", "conv_w_dw", "conv_b_dw", "conv_ln_g",
           "conv_ln_b", "w_out", "ffn2_norm", "ffn2_w_gate", "ffn2_w_up", "ffn2_w_down", "final_norm"]
SHARDED = ["ffn1_w_gate", "ffn1_w_up", "ffn1_w_down", "w_in", "s5_w_glu", "conv_w_dw", "w_out", "ffn2_w_gate", "ffn2_w_up",
           "ffn2_w_down"]
REPLICATED = [n for n in WEIGHTS if n not in SHARDED]
TRANSPOSED = ["ffn1_w_gate", "ffn1_w_up", "ffn2_w_gate", "ffn2_w_up", "w_in"]


def _shard_to_wire(n, w):
    if n == "conv_w_dw":
        return jnp.pad(w, ((0, CONV_HALO - CONV_K), (0, 0)))
    return w.astype(bf16)


def _to_wire(shards, ride):
    names = list(shards)
    shapes = [jax.eval_shape(functools.partial(_shard_to_wire, n), shards[n]) for n in names]

    def body(*refs):
        for src, dst in zip(refs[:len(names)], refs[len(names):]):
            (r, c), (rp, cp) = src.shape, dst.shape
            dst[:r, :c] = src[...].astype(dst.dtype)
            if cp > c:
                dst[:, c:] = jnp.zeros((rp, cp - c), dst.dtype)
            if rp > r:
                dst[r:, :] = jnp.zeros((rp - r, cp), dst.dtype)

    out = _pallas(body, ride=ride, name="to_wire", out_shape=shapes, in_specs=[pl.BlockSpec(memory_space=pltpu.VMEM)] * len(names),
                  out_specs=[pl.BlockSpec(memory_space=pltpu.VMEM)] * len(names))(*[shards[n] for n in names])
    return dict(zip(names, out))


def _gathered_to_full(n, g):
    if n == "conv_w_dw":
        return g.transpose(1, 0, 2).reshape(CONV_HALO, CONV_WIDTH)[:CONV_K]
    return g.reshape(N_DEV * g.shape[1], g.shape[2])


def _grad_to_blocks(n, g):
    if n == "conv_w_dw":
        g = jnp.pad(g, ((0, CONV_HALO - CONV_K), (0, 0)))
        g = g.reshape(g.shape[0], N_DEV, g.shape[1] // N_DEV).transpose(1, 0, 2)
    else:
        g = g.reshape(N_DEV, g.shape[0] // N_DEV, g.shape[1])
    return g.astype(bf16).reshape(N_CHIP, 2, *g.shape[1:])


REPLICATED_LATE = ["ffn1_norm"]
REPLICATED_HEAD = ["ffn2_norm", "final_norm"]
REPLICATED_MIX = ["mix_norm", "conv_b_dw", "conv_ln_g", "conv_ln_b"]
REPLICATED_S5 = [n for n in REPLICATED if n not in REPLICATED_LATE + REPLICATED_HEAD + REPLICATED_MIX]
REPLICATED_EARLY = REPLICATED_HEAD + REPLICATED_S5 + REPLICATED_MIX

PLAN = {
    "start": [("gather", ["ffn1_w_gate", "ffn1_w_up"])],
    "ffn1_up": [("gather", ["ffn1_w_down", "w_in", "w_out", "s5_w_glu", "conv_w_dw"])],
    "s5_forward": [("gather", ["ffn2_w_gate", "ffn2_w_up"])],
    "conv_fwd": [("gather", ["ffn2_w_down"])],
    "ffn2_dw_up": [("sibling", ["ffn2_w_gate"])],
    "ffn2_dw_down": [("sibling", ["ffn2_w_up"])],
    "mix_out_bwd": [("sibling", ["ffn2_w_down"]), ("replicated", REPLICATED_HEAD)],
    "s5_backward": [("chips", ["ffn2_w_gate", "ffn2_w_up"])],
    "conv_bwd_taps": [("chips", ["ffn2_w_down"]), ("replicated", REPLICATED_S5)],
    "ffn1_dw_down": [("sibling", ["w_in", "s5_w_glu", "conv_w_dw", "w_out"]), ("replicated", REPLICATED_MIX)],
    "ffn1_bwd_act": [("chips", ["w_in", "s5_w_glu", "conv_w_dw", "w_out"]), ("sibling", ["ffn1_w_down"])],
    "ffn1_dw_gate": [("chips", ["ffn1_w_down"])],
    "ffn1_dw_up": [("sibling", ["ffn1_w_gate"])],
    "ffn1_bwd_in_0": [("chips", ["ffn1_w_gate"]), ("sibling", ["ffn1_w_up"])],
    "ffn1_bwd_in_1": [("chips", ["ffn1_w_up"])],
    "tail": [("replicated", REPLICATED_LATE)],
}


class _Schedule:
    def __init__(self, wire, p, grads, core):
        self.wire, self.p, self.grads, self.core = wire, p, grads, core
        self.partial, self.reduced, self.everyone, self.pending = {}, {}, {}, []

    def before(self, point):
        assert not self.pending
        for kind, names in PLAN.get(point, ()):
            if kind == "gather":
                given = [self.wire[n] for n in names]
                ex = _gather(given)
            elif kind == "sibling":
                given = [_grad_to_blocks(n, self.grads[n]) for n in names]
                ex = _swap_with_sibling(given)
            elif kind == "chips":
                given = [self.partial.pop(n) for n in names]
                ex = _swap_with_chips(given)
            else:
                names = names + ["loss_terms"] * (names is REPLICATED_HEAD)
                given = [self.grads[n].reshape(self.p[n].shape) if n in self.p else self.grads[n] for n in names]
                ex = _gather(given)
            self.pending.append((kind, names, given, ex))
        return [ex for _, _, _, ex in self.pending]

    def after(self, point):
        for kind, names, given, ex in self.pending:
            if kind == "gather":
                for n, g in zip(names, ex.out):
                    self.p[n] = _gathered_to_full(n, g)
            elif kind == "sibling":
                for n, blocks, got in zip(names, given, ex.out):
                    self.partial[n] = _add_sibling(blocks, got, self.core, "reduce_add_" + n)
            elif kind == "chips":
                for n, part, got in zip(names, given, ex.out):
                    self.reduced[n] = (part, got)
            else:
                self.everyone.update(zip(names, ex.out))
        self.pending = []

    def alone(self, point):
        _exchange(self.before(point), point)
        self.after(point)


def kernel(x, ffn1_norm, ffn1_w_gate, ffn1_w_up, ffn1_w_down, mix_norm, w_in, s5_lam_re, s5_lam_im, s5_log_dt, s5_b_re, s5_b_im, s5_c_re, s5_c_im, s5_d, s5_w_glu, s5_b_glu, conv_w_dw, conv_b_dw, conv_ln_g, conv_ln_b, w_out, ffn2_norm, ffn2_w_gate, ffn2_w_up, ffn2_w_down, final_norm, loss_target, m_ffn1_norm, m_ffn1_w_gate, m_ffn1_w_up, m_ffn1_w_down, m_mix_norm, m_w_in, m_s5_lam_re, m_s5_lam_im, m_s5_log_dt, m_s5_b_re, m_s5_b_im, m_s5_c_re, m_s5_c_im, m_s5_d, m_s5_w_glu, m_s5_b_glu, m_conv_w_dw, m_conv_b_dw, m_conv_ln_g, m_conv_ln_b, m_w_out, m_ffn2_norm, m_ffn2_w_gate, m_ffn2_w_up, m_ffn2_w_down, m_final_norm, v_ffn1_norm, v_ffn1_w_gate, v_ffn1_w_up, v_ffn1_w_down, v_mix_norm, v_w_in, v_s5_lam_re, v_s5_lam_im, v_s5_log_dt, v_s5_b_re, v_s5_b_im, v_s5_c_re, v_s5_c_im, v_s5_d, v_s5_w_glu, v_s5_b_glu, v_conv_w_dw, v_conv_b_dw, v_conv_ln_g, v_conv_ln_b, v_w_out, v_ffn2_norm, v_ffn2_w_gate, v_ffn2_w_up, v_ffn2_w_down, v_final_norm):
    args = locals()
    w = {n: args[n] for n in WEIGHTS}
    m = {n: args["m_" + n] for n in WEIGHTS}
    v = {n: args["v_" + n] for n in WEIGHTS}
    xq, yq, cq = _place()
    q = 2 * xq + yq
    slots = jnp.stack([q, q ^ 1, q ^ 2, q ^ 3]).astype(jnp.int32)

    def shard2d(n, a):
        a = a.reshape(a.shape[-2:])
        return a.T if n in TRANSPOSED else a

    def view(n, a):
        if n.startswith("s5_b_") and a.ndim == 4:
            return a[0].transpose(0, 2, 1)
        return a[0] if a.ndim >= 3 else a.reshape(1, -1)

    def unview(n, a):
        return (a.transpose(0, 2, 1) if n.startswith("s5_b_") and a.ndim == 3 else a).reshape(w[n].shape)

    p = {n: view(n, w[n]) for n in REPLICATED}
    grads = {}
    first = PLAN["start"][0][1]
    wire = {n: _shard_to_wire(n, shard2d(n, w[n])) for n in first}
    sched = _Schedule(wire, p, grads, jnp.reshape(cq, (1,)).astype(jnp.int32))
    wire.update(_to_wire({n: shard2d(n, w[n]) for n in SHARDED if n not in first}, sched.before("start")))
    sched.after("start")
    _, dx = _local_step(x[0], loss_target[0], p, grads, sched)

    out = {}
    groups = [[n for n in SHARDED if n.startswith(tag)] for tag in ("ffn1", "ffn2")]
    for names in groups + [[n] for n in SHARDED if not n.startswith("ffn")]:
        def fit(n, a):
            a = shard2d(n, a)
            return jnp.pad(a, ((0, sched.reduced[n][1].shape[1] - a.shape[0]), (0, 0)))

        items = [(fit(n, w[n]), fit(n, m[n]), fit(n, v[n]), *sched.reduced[n]) for n in names]
        for n, res in zip(names, _adam_sharded(items, slots, "adam_" + names[0])):
            back = lambda r: r[:shard2d(n, w[n]).shape[0]]
            out[n] = [(back(r).T if n in TRANSPOSED else back(r)).reshape(w[n].shape) for r in res]

    for names in (REPLICATED_EARLY, REPLICATED_LATE):
        items = [(view(n, w[n]), view(n, m[n]), view(n, v[n]), sched.everyone[n]) for n in names]
        res, total = _adam_replicated(items, sched.everyone.get("loss_terms") if names is REPLICATED_EARLY else None,
                                      "adam_" + names[0])
        for n, r in zip(names, res):
            out[n] = [unview(n, a) for a in r]
        if total is not None:
            loss = total.reshape(())

    return (loss, dx.reshape(x.shape), *[out[n][0] for n in WEIGHTS], *[out[n][1] for n in WEIGHTS],
            *[out[n][2] for n in WEIGHTS], *[out[n][3] for n in WEIGHTS])
```

```python
import functools

import jax
import jax.numpy as jnp
from jax import lax
from jax.experimental import pallas as pl
from jax.experimental.pallas import tpu as pltpu

f32 = jnp.float32
bf16 = jnp.bfloat16
S = jax.ShapeDtypeStruct

N_DEV = 8
N_CHIP = 4
D_MODEL = 1024
D_FF = 2816
FF_CHUNKS = [(0, 768), (768, 1536), (1536, 2304), (2304, D_FF)]
S5_WIDTH = 512
S5_GROUPS = 32
S5_GROUP_CH = 16
S5_STATE = 64
S5_LANES = S5_GROUPS * S5_STATE
CONV_WIDTH = 512
CONV_K = 31
CONV_HALO = 32
CONV_HEAD = 64
CONV_ROWS = 32
IN_COLS = S5_WIDTH + 2 * CONV_WIDTH
SEGMENTS = 8
SCAN_LANES = 512
EPS = 1e-6
LR, B1, B2, ADAM_EPS, WD, STEP = 0.001, 0.9, 0.999, 1e-08, 0.01, 10
VMEM_LIMIT = 56 * 1024 * 1024

NN = (((1,), (0,)), ((), ()))
NT = (((1,), (1,)), ((), ()))
TN = (((0,), (0,)), ((), ()))


def _dot(a, b, dims=NN):
    return lax.dot_general(a, b, dims, preferred_element_type=f32)


def _cp(*sem):
    return pltpu.CompilerParams(dimension_semantics=sem, vmem_limit_bytes=VMEM_LIMIT)


def _rms(x, g):
    return x * lax.rsqrt(jnp.mean(x * x, axis=-1, keepdims=True) + EPS) * g


def _rms_bwd(x, g, dh):
    _, vjp = jax.vjp(_rms, x, g)
    return vjp(dh)


def _sigmoid(x):
    return 1.0 / (1.0 + jnp.exp(-x))


def _gelu(x):
    return 0.5 * x * (1.0 + jnp.tanh(0.7978845608028654 * (x + 0.044715 * x * x * x)))


def _rows8(x):
    t, c = x.shape
    return x.reshape(t // 8, 8, c).sum(axis=0)


def _full(shape):
    return pl.BlockSpec(shape, lambda *_: (0,) * len(shape))


def _resident(shape):
    return pl.BlockSpec(shape, lambda *_: (0,) * len(shape), pipeline_mode=pl.Buffered(1))


def _ffn_up(x, g, wg, wu, tm, tag, mixed=None, ride=()):
    L = x.shape[0]

    def body(x_ref, g_ref, wg_ref, wu_ref, *rest):
        h_ref, dadg_ref, dadu_ref, a_ref = rest[-5:-1] if mixed else rest[-4:]
        x = x_ref[...]
        if mixed:
            ys_ref, yc_ref, wo_ref = rest[:3]
            x = x + _dot(ys_ref[...], wo_ref[:S5_WIDTH, :]) + _dot(yc_ref[...], wo_ref[S5_WIDTH:, :])
            rest[-1][...] = x
        h = _rms(x, g_ref[...]).astype(bf16)
        h_ref[...] = h
        for lo, hi in FF_CHUNKS:
            cols = slice(lo, hi)
            gate =_dot(h, wg_ref[cols, :], NT)
            up = _dot(h, wu_ref[cols, :], NT)
            sig = _sigmoid(gate)
            silu = gate * sig
            dadg_ref[:, cols] = (up * (sig + silu * (1.0 - sig))).astype(bf16)
            dadu_ref[:, cols] = silu.astype(bf16)
            a_ref[:, cols] = (silu * up).astype(bf16)

    row = pl.BlockSpec((tm, D_MODEL), lambda i: (i, 0))
    wide = pl.BlockSpec((tm, D_FF), lambda i: (i, 0))
    half = pl.BlockSpec((tm, S5_WIDTH), lambda i: (i, 0))
    return _pallas(
        body, ride=ride, name=tag + "_up", grid=(L // tm,),
        in_specs=[row, _full((1, D_MODEL)), _resident((D_FF, D_MODEL)), _resident((D_FF, D_MODEL))]
        + ([half, half, _resident((D_MODEL, D_MODEL))] if mixed else []),
        out_specs=[row, wide, wide, wide] + [row] * bool(mixed),
        out_shape=[S((L, D_MODEL), bf16)] + [S((L, D_FF), bf16)] * 3 + [S((L, D_MODEL), f32)] * bool(mixed),
        compiler_params=_cp("parallel"),
    )(x, g, wg, wu, *(mixed or ()))


def _ffn_down(x, a, wd, tm, tag, mixer=None, ride=()):
    L = x.shape[0]

    def body(x_ref, a_ref, wd_ref, *rest):
        xo = x_ref[...] + 0.5 * _dot(a_ref[...], wd_ref[...])
        if not mixer:
            rest[0][...] = xo
            return
        g_ref, w_ref, o_ref, h_ref, us_ref, v_ref = rest
        o_ref[...] = xo
        h = _rms(xo, g_ref[...]).astype(bf16)
        h_ref[...] = h
        u = _dot(h, w_ref[...], NT)
        us_ref[...] = u[:, :S5_WIDTH]
        v_ref[...] = u[:, S5_WIDTH:]

    row = lambda c: pl.BlockSpec((tm, c), lambda i: (i, 0))
    extra_in = [_full((1, D_MODEL)), _resident((IN_COLS, D_MODEL))] if mixer else []
    extra_out = [(D_MODEL, bf16), (S5_WIDTH, f32), (2 * CONV_WIDTH, f32)] if mixer else []
    out = _pallas(
        body, ride=ride, name=tag + "_down", grid=(L // tm,),
        in_specs=[row(D_MODEL), row(D_FF), _resident((D_FF, D_MODEL))] + extra_in,
        out_specs=[row(D_MODEL)] + [row(c) for c, _ in extra_out],
        out_shape=[S((L, D_MODEL), f32)] + [S((L, c), t) for c, t in extra_out],
        compiler_params=_cp("parallel"),
    )(x, a, wd, *(mixer or ()))
    return out if mixer else out[0]


def _ffn_down_loss(x, a, wd, target, g, tm, tag):
    L = x.shape[0]

    def body(x_ref, a_ref, wd_ref, t_ref, g_ref, dx_ref, dg_ref, l_ref):
        @pl.when(pl.program_id(0) == 0)
        def _():
            dg_ref[...] = jnp.zeros_like(dg_ref)
            l_ref[...] = jnp.zeros_like(l_ref)

        xo = x_ref[...] + 0.5 * _dot(a_ref[...], wd_ref[...])
        g = g_ref[...]
        e = _rms(xo, g) - t_ref[...]
        l_ref[...] += _rows8(e * e) * (0.5 / D_MODEL)
        dx, dg = _rms_bwd(xo, g, e * (1.0 / D_MODEL))
        dx_ref[...] = dx
        dg_ref[...] += dg

    row = pl.BlockSpec((tm, D_MODEL), lambda i: (i, 0))
    return pl.pallas_call(
        body, name=tag + "_down_loss", grid=(L // tm,),
        in_specs=[row, pl.BlockSpec((tm, D_FF), lambda i: (i, 0)), _resident((D_FF, D_MODEL)), row, _full((1, D_MODEL))],
        out_specs=[row, _full((1, D_MODEL)), _full((8, D_MODEL))],
        out_shape=[S((L, D_MODEL), f32), S((1, D_MODEL), f32), S((8, D_MODEL), f32)],
        compiler_params=_cp("arbitrary"),
    )(x, a, wd, target, g)


def _ffn_bwd_act(dxo, wd, dadg, dadu, tm, tag, ride=()):
    L = dxo.shape[0]

    def body(dx_ref, wd_ref, dadg_ref, dadu_ref, dgate_ref, dup_ref, dxh_ref):
        dxh = (0.5 * dx_ref[...]).astype(bf16)
        dxh_ref[...] = dxh
        for lo, hi in FF_CHUNKS:
            cols = slice(lo, hi)
            da =_dot(dxh, wd_ref[cols, :], NT)
            dgate_ref[:, cols] = (da * dadg_ref[:, cols].astype(f32)).astype(bf16)
            dup_ref[:, cols] = (da * dadu_ref[:, cols].astype(f32)).astype(bf16)

    row = pl.BlockSpec((tm, D_MODEL), lambda i: (i, 0))
    wide = pl.BlockSpec((tm, D_FF), lambda i: (i, 0))
    return _pallas(
        body, ride=ride, name=tag + "_bwd_act", grid=(L // tm,),
        in_specs=[row, _resident((D_FF, D_MODEL)), wide, wide],
        out_specs=[wide, wide, row],
        out_shape=[S((L, D_FF), bf16), S((L, D_FF), bf16), S((L, D_MODEL), bf16)],
        compiler_params=_cp("parallel"),
    )(dxo, wd, dadg, dadu)


def _ffn_bwd_in(dxo, x, g, dgate, dup, wg, wu, tm, name, tiles=None, into=None, ride=()):
    L = x.shape[0]
    first, count = tiles or (0, L // tm)

    def body(dxo_ref, x_ref, g_ref, dgate_ref, dup_ref, wg_ref, wu_ref, *rest):
        dx_ref, dg_ref = rest[-2:]

        @pl.when(pl.program_id(0) == 0)
        def _():
            dg_ref[...] = jnp.zeros_like(dg_ref)

        dh = _dot(dgate_ref[...], wg_ref[...]) + _dot(dup_ref[...], wu_ref[...])
        dx, dg = _rms_bwd(x_ref[...], g_ref[...], dh)
        dx_ref[...] = dxo_ref[...] + dx
        dg_ref[...] += dg

    row = pl.BlockSpec((tm, D_MODEL), lambda i: (first + i, 0))
    wide = pl.BlockSpec((tm, D_FF), lambda i: (first + i, 0))
    return _pallas(
        body, ride=ride, name=name, grid=(count,),
        in_specs=[row, row, _full((1, D_MODEL)), wide, wide, _resident((D_FF, D_MODEL)), _resident((D_FF, D_MODEL))]
        + [ANY] * (into is not None),
        out_specs=[row, _full((1, D_MODEL))],
        out_shape=[S((L, D_MODEL), f32), S((1, D_MODEL), f32)],
        input_output_aliases={7: 0} if into is not None else {},
        compiler_params=_cp("arbitrary"),
    )(dxo, x, g, dgate, dup, wg, wu, *([into] if into is not None else []))


def _mm_tn(a, b, out_dtype, name, tm=512, tn=1024, ride=()):
    L, M = a.shape
    N = b.shape[1]
    tm, tn = min(tm, M), min(tn, N)
    while M % tm:
        tm //= 2
    while N % tn:
        tn //= 2

    def body(a_ref, b_ref, o_ref):
        o_ref[...] = _dot(a_ref[...].astype(bf16), b_ref[...].astype(bf16), TN).astype(out_dtype)

    return _pallas(
        body, ride=ride, name=name, grid=(M // tm, N // tn),
        in_specs=[pl.BlockSpec((L, tm), lambda i, j: (0, i)), pl.BlockSpec((L, tn), lambda i, j: (0, j))],
        out_specs=pl.BlockSpec((tm, tn), lambda i, j: (i, j)),
        out_shape=S((M, N), out_dtype),
        compiler_params=_cp("parallel", "parallel"),
    )(a, b)


def _mix_in_bwd(dxo, x, g, du_s5, dv, w_in, tm):
    L = x.shape[0]

    def body(dxo_ref, x_ref, g_ref, dus_ref, dv_ref, w_ref, dx_ref, dg_ref, dub_ref, dxh_ref):
        @pl.when(pl.program_id(0) == 0)
        def _():
            dg_ref[...] = jnp.zeros_like(dg_ref)

        dus = dus_ref[...].astype(bf16)
        dvb = dv_ref[...].astype(bf16)
        dub_ref[:, :S5_WIDTH] = dus
        dub_ref[:, S5_WIDTH:] = dvb
        dh = _dot(dus, w_ref[:S5_WIDTH, :]) + _dot(dvb, w_ref[S5_WIDTH:, :])
        dx, dg = _rms_bwd(x_ref[...], g_ref[...], dh)
        dx = dxo_ref[...] + dx
        dx_ref[...] = dx
        dxh_ref[...] = (0.5 * dx).astype(bf16)
        dg_ref[...] += dg

    row = lambda c: pl.BlockSpec((tm, c), lambda i: (i, 0))
    return pl.pallas_call(
        body, name="mix_in_bwd", grid=(L // tm,),
        in_specs=[row(D_MODEL), row(D_MODEL), _full((1, D_MODEL)), row(S5_WIDTH), row(2 * CONV_WIDTH),
                  _full((IN_COLS, D_MODEL))],
        out_specs=[row(D_MODEL), _full((1, D_MODEL)), row(IN_COLS), row(D_MODEL)],
        out_shape=[S((L, D_MODEL), f32), S((1, D_MODEL), f32), S((L, IN_COLS), bf16), S((L, D_MODEL), bf16)],
        compiler_params=_cp("arbitrary"),
    )(dxo, x, g, du_s5, dv, w_in)


def _mix_out_bwd(dx, w_out, tm, ride=()):
    L = dx.shape[0]

    def body(dx_ref, w_ref, dys_ref, dyc_ref, dxb_ref):
        dxb = dx_ref[...].astype(bf16)
        dxb_ref[...] = dxb
        dys_ref[...] = _dot(dxb, w_ref[:S5_WIDTH, :], NT)
        dyc_ref[...] = _dot(dxb, w_ref[S5_WIDTH:, :], NT)

    row = lambda c: pl.BlockSpec((tm, c), lambda i: (i, 0))
    return _pallas(
        body, ride=ride, name="mix_out_bwd", grid=(L // tm,),
        in_specs=[row(D_MODEL), _full((D_MODEL, D_MODEL))],
        out_specs=[row(S5_WIDTH), row(CONV_WIDTH), row(D_MODEL)],
        out_shape=[S((L, S5_WIDTH), f32), S((L, CONV_WIDTH), f32), S((L, D_MODEL), bf16)],
        compiler_params=_cp("parallel"),
    )(dx, w_out)


def _s5_discretise(lam_re, lam_im, log_dt, b_re, b_im):
    dt = jnp.exp(log_dt)
    mag = jnp.exp(lam_re * dt)
    abar_re = mag * jnp.cos(lam_im * dt)
    abar_im = mag * jnp.sin(lam_im * dt)
    den = lam_re * lam_re + lam_im * lam_im
    num_re = abar_re - 1.0
    f_re = ((num_re * lam_re + abar_im * lam_im) / den)[:, None, :]
    f_im = ((abar_im * lam_re - num_re * lam_im) / den)[:, None, :]
    return abar_re, abar_im, f_re * b_re - f_im * b_im, f_re * b_im + f_im * b_re


def _s5_params(lam_re, lam_im, log_dt, b_re, b_im):
    def body(lr, li, ld, br, bi, ar_ref, ai_ref, bbr_ref, bbi_ref):
        ar, ai, bbr, bbi = _s5_discretise(lr[...], li[...], ld[...], br[...], bi[...])
        ar_ref[...], ai_ref[...], bbr_ref[...], bbi_ref[...] = ar, ai, bbr, bbi

    gp = S((S5_GROUPS, S5_STATE), f32)
    gcp = S((S5_GROUPS, S5_GROUP_CH, S5_STATE), f32)
    return pl.pallas_call(body, name="s5_params", out_shape=[gp, gp, gcp, gcp])(lam_re, lam_im, log_dt, b_re, b_im)


def _s5_params_bwd(lam_re, lam_im, log_dt, b_re, b_im, d_ar, d_ai, d_bbr, d_bbi):
    def body(lr, li, ld, br, bi, car, cai, cbr, cbi, o_lr, o_li, o_ld, o_br, o_bi):
        _, vjp = jax.vjp(_s5_discretise, lr[...], li[...], ld[...], br[...], bi[...])
        o_lr[...], o_li[...], o_ld[...], o_br[...], o_bi[...] = vjp((car[...], cai[...], cbr[...], cbi[...]))

    gp = S((S5_GROUPS, S5_STATE), f32)
    gcp = S((S5_GROUPS, S5_GROUP_CH, S5_STATE), f32)
    return pl.pallas_call(body, name="s5_params_bwd", out_shape=[gp, gp, S((S5_GROUPS, 1), f32), gcp, gcp])(
        lam_re, lam_im, log_dt, b_re, b_im, d_ar, d_ai, d_bbr, d_bbi)


def _cmul(ar, ai, br, bi):
    return ar * br - ai * bi, ar * bi + ai * br


def _segment_starts(er, ei, ar, ai, steps, reverse):
    pr, pi = ar, ai
    n = 1
    while n < steps:
        pr, pi = _cmul(pr, pi, pr, pi)
        n *= 2
    assert n == steps
    row = lax.broadcasted_iota(jnp.int32, (SEGMENTS, SCAN_LANES), 0)
    hr = jnp.zeros((1, SCAN_LANES), f32)
    hi = jnp.zeros((1, SCAN_LANES), f32)
    out_r = jnp.zeros((SEGMENTS, SCAN_LANES), f32)
    out_i = jnp.zeros((SEGMENTS, SCAN_LANES), f32)
    order = range(SEGMENTS - 1, 0, -1) if reverse else range(0, SEGMENTS - 1)
    for r in order:
        qr, qi = _cmul(pr, pi, hr, hi)
        hr, hi = qr + er[r:r + 1, :], qi + ei[r:r + 1, :]
        nxt = r - 1 if reverse else r + 1
        out_r = jnp.where(row == nxt, hr, out_r)
        out_i = jnp.where(row == nxt, hi, out_i)
    return out_r, out_i


def _s5_read_bwd(dout, y_lin, u, d_skip, w_glu, b_glu, tm):
    L = u.shape[0]

    def body(do_ref, yl_ref, u_ref, d_ref, w_ref, b_ref, dyl_ref, du_ref, dd_ref, dw_ref, db_ref):
        @pl.when(pl.program_id(0) == 0)
        def _():
            dd_ref[...] = jnp.zeros_like(dd_ref)
            dw_ref[...] = jnp.zeros_like(dw_ref)
            db_ref[...] = jnp.zeros_like(db_ref)

        u, d, dout = u_ref[...], d_ref[...], do_ref[...]
        y, gelu_vjp = jax.vjp(_gelu, yl_ref[...] + d * u)
        yb = y.astype(bf16)
        sig = _sigmoid(_dot(yb, w_ref[...]) + b_ref[...])
        dz = dout * y * sig * (1.0 - sig)
        dzb = dz.astype(bf16)
        dy = dout * sig + _dot(dzb, w_ref[...], NT)
        (dyp,) = gelu_vjp(dy)
        dyl_ref[...] = dyp.astype(bf16)
        du_ref[...] = d * dyp
        dd_ref[...] += _rows8(dyp * u)
        db_ref[...] += _rows8(dz)
        dw_ref[...] += _dot(yb, dzb, TN)

    row = pl.BlockSpec((tm, S5_WIDTH), lambda i: (i, 0))
    vec = _full((1, S5_WIDTH))
    part = _full((8, S5_WIDTH))
    return pl.pallas_call(
        body, name="s5_read_bwd", grid=(L // tm,),
        in_specs=[row, row, row, vec, _full((S5_WIDTH, S5_WIDTH)), vec],
        out_specs=[row, row, part, _full((S5_WIDTH, S5_WIDTH)), part],
        out_shape=[S((L, S5_WIDTH), bf16), S((L, S5_WIDTH), f32), S((8, S5_WIDTH), f32),
                   S((S5_WIDTH, S5_WIDTH), f32), S((8, S5_WIDTH), f32)],
        compiler_params=_cp("arbitrary"),
    )(dout, y_lin, u, d_skip, w_glu, b_glu)


S5_CHUNK_CH = SCAN_LANES // S5_STATE * S5_GROUP_CH


def _s5_two_phase(L, bi):
    rows = bi * SEGMENTS
    nb = L // rows
    whole = pltpu.VMEM((L // SEGMENTS, SEGMENTS, SCAN_LANES), f32)
    mat = pl.BlockSpec((S5_CHUNK_CH, SCAN_LANES), lambda c, j: (c, c))
    vec = pl.BlockSpec((1, SCAN_LANES), lambda c, j: (0, c))
    tile = pl.BlockSpec((SEGMENTS, SCAN_LANES), lambda c, j: (0, c))
    return rows, nb, whole, mat, vec, tile


def _s5_forward(u, a_re, a_im, bb_re, bb_im, cc_re, cc_im, bi, ride=()):
    L = u.shape[0]
    rows, nb, whole, mat, vec, _ = _s5_two_phase(L, bi)

    def body(u_ref, ar_ref, ai_ref, br_ref, bi_ref, cr_ref, ci_ref, sr_ref, si_ref, yl_ref, hr_ref, hi_ref, dr_ref, di_ref):
        j = pl.program_id(1)
        ar = jnp.broadcast_to(ar_ref[...], (SEGMENTS, SCAN_LANES))
        ai = jnp.broadcast_to(ai_ref[...], (SEGMENTS, SCAN_LANES))

        @pl.when(j == 0)
        def _():
            hr_ref[...] = jnp.zeros_like(hr_ref)
            hi_ref[...] = jnp.zeros_like(hi_ref)

        @pl.when(j < nb)
        def _():
            base = j * bi
            ub = u_ref[...].astype(bf16)
            dr_ref[pl.ds(base, bi)] = _dot(ub, br_ref[...]).reshape(bi, SEGMENTS, SCAN_LANES)
            di_ref[pl.ds(base, bi)] = _dot(ub, bi_ref[...]).reshape(bi, SEGMENTS, SCAN_LANES)

            def step(i, c):
                pr, pi = _cmul(ar, ai, c[0], c[1])
                return pr + dr_ref[base + i], pi + di_ref[base + i]

            hr_ref[...], hi_ref[...] = lax.fori_loop(0, bi, step, (hr_ref[...], hi_ref[...]), unroll=4)

        @pl.when(j == nb - 1)
        def _():
            hr_ref[...], hi_ref[...] = _segment_starts(hr_ref[...], hi_ref[...], ar_ref[...], ai_ref[...], L // SEGMENTS, False)

        @pl.when(j >= nb)
        def _():
            base = (j - nb) * bi

            def step(i, c):
                pr, pi = _cmul(ar, ai, c[0], c[1])
                nr, nim = pr + dr_ref[base + i], pi + di_ref[base + i]
                dr_ref[base + i] = nr
                di_ref[base + i] = nim
                return nr, nim

            hr_ref[...], hi_ref[...] = lax.fori_loop(0, bi, step, (hr_ref[...], hi_ref[...]), unroll=4)
            sr = dr_ref[pl.ds(base, bi)].reshape(rows, SCAN_LANES).astype(bf16)
            si = di_ref[pl.ds(base, bi)].reshape(rows, SCAN_LANES).astype(bf16)
            sr_ref[...] = sr
            si_ref[...] = si
            yl_ref[...] = _dot(sr, cr_ref[...], NT) - _dot(si, ci_ref[...], NT)

    u_spec = pl.BlockSpec((rows, S5_CHUNK_CH), lambda c, j: (jnp.minimum(j, nb - 1), c))
    late = lambda width: pl.BlockSpec((rows, width), lambda c, j: (jnp.maximum(j - nb, 0), c))
    return _pallas(
        body, ride=ride, name="s5_forward", grid=(S5_LANES // SCAN_LANES, 2 * nb),
        in_specs=[u_spec, vec, vec, mat, mat, mat, mat],
        out_specs=[late(SCAN_LANES), late(SCAN_LANES), late(S5_CHUNK_CH)],
        out_shape=[S((L, S5_LANES), bf16)] * 2 + [S((L, S5_WIDTH), f32)],
        scratch_shapes=[pltpu.VMEM((SEGMENTS, SCAN_LANES), f32)] * 2 + [whole] * 2,
        compiler_params=_cp("parallel", "arbitrary"),
    )(u, a_re, a_im, bb_re, bb_im, cc_re, cc_im)


def _s5_backward(dy, u, du_skip, s_re, s_im, a_re, a_im, bb_re, bb_im, cc_re, cc_im, bi, ride=()):
    L = u.shape[0]
    rows, nb, whole, mat, vec, tile = _s5_two_phase(L, bi)
    per = rows // 16

    def body(dy_ref, u_ref, dus_ref, sr_ref, si_ref, pr_ref, pi_ref, lr_ref, li_ref, ar_ref, ai_ref, br_ref, bi_ref, cr_ref,
             ci_ref, du_ref, dar_ref, dai_ref, dbr_ref, dbi_ref, dcr_ref, dci_ref, hr_ref, hi_ref, gr_ref, gi_ref, fr_ref, fi_ref):
        j = pl.program_id(1)
        ar = jnp.broadcast_to(ar_ref[...], (SEGMENTS, SCAN_LANES))
        ai = jnp.broadcast_to(ai_ref[...], (SEGMENTS, SCAN_LANES))

        @pl.when(j == 0)
        def _():
            for ref in (hr_ref, hi_ref, dar_ref, dai_ref, dbr_ref, dbi_ref, dcr_ref, dci_ref):
                ref[...] = jnp.zeros_like(ref)

        @pl.when(j < nb)
        def _():
            base = (nb - 1 - j) * bi
            dy = dy_ref[...]
            gr_ref[pl.ds(base, bi)] = _dot(dy, cr_ref[...]).reshape(bi, SEGMENTS, SCAN_LANES)
            gi_ref[pl.ds(base, bi)] = (-_dot(dy, ci_ref[...])).reshape(bi, SEGMENTS, SCAN_LANES)

            def step(n, c):
                i = base + bi - 1 - n
                qr, qi = _cmul(ar, ai, c[0], c[1])
                return qr + gr_ref[i], qi + gi_ref[i]

            hr_ref[...], hi_ref[...] = lax.fori_loop(0, bi, step, (hr_ref[...], hi_ref[...]), unroll=4)

        @pl.when(j == nb - 1)
        def _():
            hr_ref[...], hi_ref[...] = _segment_starts(hr_ref[...], hi_ref[...], ar_ref[...], ai_ref[...], L // SEGMENTS, True)

        @pl.when(j >= nb)
        def _():
            blk = 2 * nb - 1 - j
            base = blk * bi
            sr, si = sr_ref[...], si_ref[...]
            fr_ref[...] = sr.astype(f32).reshape(bi, SEGMENTS, SCAN_LANES)
            fi_ref[...] = si.astype(f32).reshape(bi, SEGMENTS, SCAN_LANES)

            def step(n, c):
                i = bi - 1 - n
                gr, gi, accr, acci = c
                qr, qi = _cmul(ar, ai, gr, gi)
                gr, gi = qr + gr_ref[base + i], qi + gi_ref[base + i]
                gr_ref[base + i] = gr
                gi_ref[base + i] = gi
                pr, pi = fr_ref[i - 1], fi_ref[i - 1]
                return gr, gi, accr + (gr * pr + gi * pi), acci + (gi * pr - gr * pi)

            gr, gi, accr, acci = lax.fori_loop(0, bi - 1, step, (hr_ref[...], hi_ref[...], dar_ref[...], dai_ref[...]), unroll=3)
            qr, qi = _cmul(ar, ai, gr, gi)
            gr, gi = qr + gr_ref[base], qi + gi_ref[base]
            gr_ref[base] = gr
            gi_ref[base] = gi
            hr_ref[...], hi_ref[...] = gr, gi
            row = lax.broadcasted_iota(jnp.int32, (SEGMENTS, SCAN_LANES), 0)
            older = lambda ref: ref[...].astype(f32)[SEGMENTS:, :]
            wrap_r = jnp.where(row == 0, 0.0, pltpu.roll(older(lr_ref), 1, 0))
            wrap_i = jnp.where(row == 0, 0.0, pltpu.roll(older(li_ref), 1, 0))
            pr = jnp.where(blk == 0, wrap_r, older(pr_ref))
            pi = jnp.where(blk == 0, wrap_i, older(pi_ref))
            dar_ref[...] = accr + gr * pr + gi * pi
            dai_ref[...] = acci + gi * pr - gr * pi

            g_re = gr_ref[pl.ds(base, bi)].reshape(rows, SCAN_LANES).astype(bf16)
            g_im = gi_ref[pl.ds(base, bi)].reshape(rows, SCAN_LANES).astype(bf16)
            ub = u_ref[...].astype(bf16)
            dy = dy_ref[...]
            du_ref[...] = dus_ref[...] + _dot(g_re, br_ref[...], NT) + _dot(g_im, bi_ref[...], NT)
            dbr_ref[...] += _dot(ub, g_re, TN)
            dbi_ref[...] += _dot(ub, g_im, TN)
            dcr_ref[...] += _dot(dy, sr, TN)
            dci_ref[...] -= _dot(dy, si, TN)

    block = lambda c, j: jnp.where(j < nb, nb - 1 - j, 2 * nb - 1 - j)
    late_block = lambda c, j: jnp.minimum(2 * nb - 1 - j, nb - 1)
    both = pl.BlockSpec((rows, S5_CHUNK_CH), lambda c, j: (block(c, j), c))
    chan = pl.BlockSpec((rows, S5_CHUNK_CH), lambda c, j: (late_block(c, j), c))
    state = pl.BlockSpec((rows, SCAN_LANES), lambda c, j: (late_block(c, j), c))
    prev = pl.BlockSpec((16, SCAN_LANES), lambda c, j: (jnp.maximum(late_block(c, j) * per - 1, 0), c))
    last = pl.BlockSpec((16, SCAN_LANES), lambda c, j: (L // 16 - 1, c))
    grad = pl.BlockSpec((S5_CHUNK_CH, SCAN_LANES), lambda c, j: (c, 0))
    return _pallas(
        body, ride=ride, name="s5_backward", grid=(S5_LANES // SCAN_LANES, 2 * nb),
        in_specs=[both, chan, chan, state, state, prev, prev, last, last, vec, vec, mat, mat, mat, mat],
        out_specs=[chan, tile, tile, grad, grad, grad, grad],
        out_shape=[S((L, S5_WIDTH), f32)] + [S((SEGMENTS, S5_LANES), f32)] * 2 + [S((S5_WIDTH, SCAN_LANES), f32)] * 4,
        scratch_shapes=[pltpu.VMEM((SEGMENTS, SCAN_LANES), f32)] * 2 + [whole] * 2 + [pltpu.VMEM((bi, SEGMENTS, SCAN_LANES), f32)] * 2,
        compiler_params=_cp("parallel", "arbitrary"),
    )(dy, u, du_skip, s_re, s_im, s_re, s_im, s_re, s_im, a_re, a_im, bb_re, bb_im, cc_re, cc_im)


def _s5_gate(y_lin, u, d_skip, w_glu, b_glu, tm, ride=()):
    L = u.shape[0]

    def body(yl_ref, u_ref, d_ref, w_ref, b_ref, o_ref):
        y = _gelu(yl_ref[...] + d_ref[...] * u_ref[...])
        z = _dot(y.astype(bf16), w_ref[...]) + b_ref[...]
        o_ref[...] = (y * _sigmoid(z)).astype(bf16)

    row = pl.BlockSpec((tm, S5_WIDTH), lambda i: (i, 0))
    vec = _full((1, S5_WIDTH))
    return _pallas(
        body, ride=ride, name="s5_gate", grid=(L // tm,),
        in_specs=[row, row, vec, _full((S5_WIDTH, S5_WIDTH)), vec],
        out_specs=row, out_shape=S((L, S5_WIDTH), bf16),
        compiler_params=_cp("parallel"),
    )(y_lin, u, d_skip, w_glu, b_glu)


def _group_mean(x, avg):
    return _dot(x.astype(bf16), avg)


def _conv_act(zn, ln_g, ln_b):
    t = zn * ln_g + ln_b
    return t * _sigmoid(t)


def _glu_padded(v_ref, halo_ref, zpad_ref, tm):
    v = v_ref[...]
    vh = halo_ref[...]
    zh = vh[:, :CONV_WIDTH] * _sigmoid(vh[:, CONV_WIDTH:])
    zpad_ref[:CONV_HALO, :] = jnp.where(pl.program_id(0) > 0, zh, 0.0)
    zpad_ref[CONV_HALO:CONV_HALO + tm, :] = v[:, :CONV_WIDTH] * _sigmoid(v[:, CONV_WIDTH:])
    zpad_ref[CONV_HALO + tm:, :] = jnp.zeros((8, CONV_WIDTH), f32)


def _shifted(pad_ref, sh_ref, tm):
    for b in range(8):
        sh_ref[b] = pad_ref[pl.ds(b, tm + CONV_HALO), :]


def _window(sh_ref, r0, off, rows):
    return sh_ref[off % 8, pl.ds(pl.multiple_of(r0 + 8 * (off // 8), 8), rows), :]


def _tap_sum(w_ref, sh_ref, taps, out_ref, tm, bias):
    def chunk(c, carry):
        r0 = pl.multiple_of(c * CONV_ROWS, CONV_ROWS)
        acc = jnp.zeros((CONV_ROWS, CONV_WIDTH), f32) + bias
        for k, off in taps:
            acc = acc + w_ref[k:k + 1, :] * _window(sh_ref, r0, off, CONV_ROWS)
        out_ref[pl.ds(r0, CONV_ROWS), :] = acc
        return carry

    lax.fori_loop(0, tm // CONV_ROWS, chunk, 0)


FWD_TAPS = [(k, CONV_HALO - (CONV_K - 1) + k) for k in range(CONV_K)]
BWD_TAPS = [(k, CONV_K - 1 - k) for k in range(CONV_K)]


def _conv_specs(tm):
    per = tm // CONV_HALO
    vrow = pl.BlockSpec((tm, 2 * CONV_WIDTH), lambda i: (i, 0))
    vhalo = pl.BlockSpec((CONV_HALO, 2 * CONV_WIDTH), lambda i: (jnp.maximum(i * per - 1, 0), 0))
    return vrow, vhalo


def _conv_scratch(tm):
    return [pltpu.VMEM((tm + CONV_HALO + 8, CONV_WIDTH), f32), pltpu.VMEM((8, tm + CONV_HALO, CONV_WIDTH), f32)]


def _conv_fwd(v, w_dw, b_dw, ln_g, ln_b, avg, tm, ride=()):
    L = v.shape[0]

    def body(v_ref, halo_ref, w_ref, b_ref, g_ref, bb_ref, avg_ref, o_ref, zc_ref, zpad_ref, zs_ref):
        _glu_padded(v_ref, halo_ref, zpad_ref, tm)
        _shifted(zpad_ref, zs_ref, tm)
        _tap_sum(w_ref, zs_ref, FWD_TAPS, zc_ref, tm, b_ref[...])
        zc = zc_ref[...]
        xc = zc - _group_mean(zc, avg_ref[...])
        zn = xc * lax.rsqrt(_group_mean(xc * xc, avg_ref[...]) + EPS)
        o_ref[...] = _conv_act(zn, g_ref[...], bb_ref[...]).astype(bf16)

    vrow, vhalo = _conv_specs(tm)
    vec = _full((1, CONV_WIDTH))
    row = pl.BlockSpec((tm, CONV_WIDTH), lambda i: (i, 0))
    return _pallas(
        body, ride=ride, name="conv_fwd", grid=(L // tm,),
        in_specs=[vrow, vhalo, _full((CONV_HALO, CONV_WIDTH)), vec, vec, vec, _full((CONV_WIDTH, CONV_WIDTH))],
        out_specs=[row, row], out_shape=[S((L, CONV_WIDTH), bf16), S((L, CONV_WIDTH), f32)],
        scratch_shapes=_conv_scratch(tm),
        compiler_params=_cp("arbitrary"),
    )(v, v, w_dw, b_dw, ln_g, ln_b, avg)


def _conv_bwd_norm(dout, zc, ln_g, ln_b, avg, tm):
    L = zc.shape[0]

    def body(do_ref, zc_ref, g_ref, bb_ref, avg_ref, dzc_ref, dg_ref, db_ref, dbd_ref):
        @pl.when(pl.program_id(0) == 0)
        def _():
            dg_ref[...] = jnp.zeros_like(dg_ref)
            db_ref[...] = jnp.zeros_like(db_ref)
            dbd_ref[...] = jnp.zeros_like(dbd_ref)

        avg = avg_ref[...]
        zc = zc_ref[...]
        xc = zc - _group_mean(zc, avg)
        rstd = lax.rsqrt(_group_mean(xc * xc, avg) + EPS)
        xhat = xc * rstd
        _, act_vjp = jax.vjp(_conv_act, xhat, g_ref[...], bb_ref[...])
        dxhat, dg, db = act_vjp(do_ref[...])
        dzc = rstd * (dxhat - _group_mean(dxhat, avg) - xhat * _group_mean(dxhat * xhat, avg))
        dzc_ref[...] = dzc
        dg_ref[0:1, :] += dg
        db_ref[0:1, :] += db
        dbd_ref[...] += _rows8(dzc)

    vec = _full((1, CONV_WIDTH))
    row = pl.BlockSpec((tm, CONV_WIDTH), lambda i: (i, 0))
    part = _full((8, CONV_WIDTH))
    return pl.pallas_call(
        body, name="conv_bwd_norm", grid=(L // tm,),
        in_specs=[row, row, vec, vec, _full((CONV_WIDTH, CONV_WIDTH))],
        out_specs=[row, part, part, part],
        out_shape=[S((L, CONV_WIDTH), f32)] + [S((8, CONV_WIDTH), f32)] * 3,
        compiler_params=_cp("arbitrary"),
    )(dout, zc, ln_g, ln_b, avg)


def _conv_bwd_taps(dzc, v, w_dw, tm, ride=()):
    L = v.shape[0]
    nt = L // tm
    per = tm // CONV_HALO

    def body(d_ref, dn_ref, v_ref, w_ref, dv_ref, dw_ref, dpad_ref, ds_ref, dz_ref, z_ref):
        i = pl.program_id(0)

        @pl.when(i == 0)
        def _():
            dw_ref[...] = jnp.zeros_like(dw_ref)

        v = v_ref[...]
        sig = _sigmoid(v[:, CONV_WIDTH:])
        z_ref[...] = v[:, :CONV_WIDTH] * sig
        dpad_ref[:tm, :] = d_ref[...]
        dpad_ref[tm:tm + CONV_HALO, :] = jnp.where(i < nt - 1, dn_ref[...], 0.0)
        dpad_ref[tm + CONV_HALO:, :] = jnp.zeros((8, CONV_WIDTH), f32)
        _shifted(dpad_ref, ds_ref, tm)
        _tap_sum(w_ref, ds_ref, BWD_TAPS, dz_ref, tm, 0.0)

        for first in range(0, CONV_K, 8):
            taps = BWD_TAPS[first:first + 8]

            def chunk(c, accs, taps=taps):
                r0 = pl.multiple_of(c * 8, 8)
                z = z_ref[pl.ds(r0, 8), :]
                return tuple(acc + z * _window(ds_ref, r0, off, 8) for acc, (_, off) in zip(accs, taps))

            accs = lax.fori_loop(0, tm // 8, chunk, tuple(jnp.zeros((8, CONV_WIDTH), f32) for _ in taps), unroll=2)
            for acc, (k, _) in zip(accs, taps):
                dw_ref[k] += acc

        dz = dz_ref[...]
        dv_ref[:, :CONV_WIDTH] = dz * sig
        dv_ref[:, CONV_WIDTH:] = dz * v[:, :CONV_WIDTH] * sig * (1.0 - sig)

    vrow, _ = _conv_specs(tm)
    row = pl.BlockSpec((tm, CONV_WIDTH), lambda i: (i, 0))
    nxt = pl.BlockSpec((CONV_HALO, CONV_WIDTH), lambda i: (jnp.minimum((i + 1) * per, nt * per - 1), 0))
    return _pallas(
        body, ride=ride, name="conv_bwd_taps", grid=(nt,),
        in_specs=[row, nxt, vrow, _full((CONV_HALO, CONV_WIDTH))],
        out_specs=[vrow, _full((CONV_HALO, 8, CONV_WIDTH))],
        out_shape=[S((L, 2 * CONV_WIDTH), f32), S((CONV_HALO, 8, CONV_WIDTH), f32)],
        scratch_shapes=_conv_scratch(tm) + [pltpu.VMEM((tm, CONV_WIDTH), f32)] * 2,
        compiler_params=_cp("arbitrary"),
    )(dzc, dzc, v, w_dw)


def _to_segments(a):
    L, c = a.shape
    return a.reshape(SEGMENTS, L // SEGMENTS, c).transpose(1, 0, 2).reshape(L, c)


def _from_segments(a):
    L, c = a.shape
    return a.reshape(L // SEGMENTS, SEGMENTS, c).transpose(1, 0, 2).reshape(L, c)


def _block_diag(ms):
    n = len(ms)

    def body(*refs):
        for a in range(n):
            out = refs[n + a]
            out[...] = jnp.zeros_like(out)
            for g in range(S5_GROUPS):
                rows = slice(g * S5_GROUP_CH, (g + 1) * S5_GROUP_CH)
                out[rows, g * S5_STATE:(g + 1) * S5_STATE] = refs[a][rows, :].astype(bf16)

    return pl.pallas_call(body, name="s5_block_diag", out_shape=[S((S5_WIDTH, S5_LANES), bf16)] * n,
                          compiler_params=pltpu.CompilerParams(vmem_limit_bytes=VMEM_LIMIT))(
        *[m.reshape(S5_WIDTH, S5_STATE) for m in ms])


def _diag_blocks(ms):
    n = len(ms)
    per_chunk = SCAN_LANES // S5_STATE

    def body(*refs):
        for a in range(n):
            for g in range(S5_GROUPS):
                rows = slice(g * S5_GROUP_CH, (g + 1) * S5_GROUP_CH)
                at = g % per_chunk * S5_STATE
                refs[n + a][rows, :] = refs[a][rows, at:at + S5_STATE]

    out = pl.pallas_call(body, name="s5_diag_blocks", out_shape=[S((S5_WIDTH, S5_STATE), f32)] * n,
                         compiler_params=pltpu.CompilerParams(vmem_limit_bytes=VMEM_LIMIT))(*ms)
    return [o.reshape(S5_GROUPS, S5_GROUP_CH, S5_STATE) for o in out]


class _NoExchanges:
    def before(self, point):
        return ()

    def after(self, point):
        pass

    def alone(self, point):
        pass


def _ffn_block(x, p, tag, tm, sched, head=None, mixed=None, mixer=None):
    point = tag + "_up"
    h, dadg, dadu, a, *x_in = _ffn_up(x, p[tag + "_norm"], p[tag + "_w_gate"], p[tag + "_w_up"], tm, tag, mixed,
                                      ride=sched.before(point))
    sched.after(point)
    x, = x_in or [x]
    if head is None:
        out = _ffn_down(x, a, p[tag + "_w_down"], tm, tag, mixer and tuple(p[n] for n in mixer), ride=sched.before(tag + "_down"))
        sched.after(tag + "_down")
    else:
        out = _ffn_down_loss(x, a, p[tag + "_w_down"], *head, tm, tag)
    return out, (x, h, dadg, dadu, a)


def _ffn_block_bwd(dxo, x, p, tag, saved, tm, grads, sched, parts=1, dxh=None):
    _, h, dadg, dadu, a = saved

    def weight_grad(which, lhs, rhs):
        point = tag + "_dw_" + which
        grads[tag + "_w_" + which] = _mm_tn(lhs, rhs, bf16, point, ride=sched.before(point))
        sched.after(point)

    if dxh is not None:
        weight_grad("down", a, dxh)
    dgate, dup, own_dxh = _ffn_bwd_act(dxo, p[tag + "_w_down"], dadg, dadu, tm, tag, ride=sched.before(tag + "_bwd_act"))
    sched.after(tag + "_bwd_act")
    weight_grad("gate", dgate, h)
    weight_grad("up", dup, h)
    if dxh is None:
        weight_grad("down", a, own_dxh)
    tiles = x.shape[0] // tm
    dx, dgs = None, []
    for k in range(parts):
        point = tag + "_bwd_in" + ("_%d" % k) * (parts > 1)
        dx, dg = _ffn_bwd_in(dxo, x, p[tag + "_norm"], dgate, dup, p[tag + "_w_gate"], p[tag + "_w_up"], tm, point,
                             tiles=(k * tiles // parts, tiles // parts), into=dx, ride=sched.before(point))
        sched.after(point)
        dgs.append(dg)
    grads[tag + "_norm"] = functools.reduce(jnp.add, dgs)
    return dx


def _local_step(x, target, p, grads, sched):
    L = x.shape[0]
    tm = min(512, L // 2)
    ni = L // SEGMENTS
    bi = min(64, ni)

    def carried(point, fn, *args):
        out = fn(*args, ride=sched.before(point))
        sched.after(point)
        return out

    (x1, h2, u_s5, v), saved1 = _ffn_block(x, p, "ffn1", tm, sched, mixer=("mix_norm", "w_in"))

    s5_in = (p["s5_lam_re"], p["s5_lam_im"], p["s5_log_dt"].reshape(S5_GROUPS, 1), p["s5_b_re"], p["s5_b_im"])
    abar_re, abar_im, bbar_re, bbar_im = _s5_params(*s5_in)
    a_re, a_im = abar_re.reshape(1, S5_LANES), abar_im.reshape(1, S5_LANES)
    bb_re, bb_im, cc_re, cc_im = _block_diag([bbar_re, bbar_im, p["s5_c_re"], p["s5_c_im"]])
    u_seg = _to_segments(u_s5)
    s_re, s_im, y_lin = carried("s5_forward", _s5_forward, u_seg, a_re, a_im, bb_re, bb_im, cc_re, cc_im, bi)
    y_s5 = _from_segments(_s5_gate(y_lin, u_seg, p["s5_d"], p["s5_w_glu"], p["s5_b_glu"], tm))
    w_dw = jnp.pad(p["conv_w_dw"], ((0, CONV_HALO - CONV_K), (0, 0)))
    heads = jnp.arange(CONV_WIDTH) // CONV_HEAD
    avg = ((heads[:, None] == heads[None, :]).astype(f32) / CONV_HEAD).astype(bf16)
    y_conv, zc = carried("conv_fwd", _conv_fwd, v, w_dw, p["conv_b_dw"], p["conv_ln_g"], p["conv_ln_b"], avg, tm)

    (dx3, grads["final_norm"], loss_terms), saved2 = _ffn_block(
        x1, p, "ffn2", tm, sched, head=(target, p["final_norm"].reshape(1, D_MODEL)), mixed=(y_s5, y_conv, p["w_out"]))
    x2 = saved2[0]
    grads["loss_terms"] = loss_terms

    dx2 = _ffn_block_bwd(dx3, x2, p, "ffn2", saved2, tm, grads, sched)

    dy_s5, dy_conv, dx2b = carried("mix_out_bwd", _mix_out_bwd, dx2, p["w_out"], tm)
    grads["w_out"] = jnp.concatenate([_mm_tn(y_s5, dx2b, bf16, "dw_out_s5"), _mm_tn(y_conv, dx2b, bf16, "dw_out_conv")], axis=0)
    dy_lin, du_skip, dd8, grads["s5_w_glu"], dbg8 = _s5_read_bwd(
        _to_segments(dy_s5), y_lin, u_seg, p["s5_d"], p["s5_w_glu"], p["s5_b_glu"], tm)
    grads["s5_d"] = dd8.sum(axis=0, keepdims=True)
    grads["s5_b_glu"] = dbg8.sum(axis=0, keepdims=True)
    du_seg, da_re8, da_im8, dbb_re, dbb_im, dcc_re, dcc_im = carried(
        "s5_backward", _s5_backward, dy_lin, u_seg, du_skip, s_re, s_im, a_re, -a_im, bb_re, bb_im, cc_re, cc_im, bi)
    d_abar = lambda a8: a8.sum(axis=0).reshape(S5_GROUPS, S5_STATE)
    grads["s5_c_re"], grads["s5_c_im"], d_bbr, d_bbi = _diag_blocks([dcc_re, dcc_im, dbb_re, dbb_im])
    d_lr, d_li, d_ld, d_br, d_bi = _s5_params_bwd(*s5_in, d_abar(da_re8), d_abar(da_im8), d_bbr, d_bbi)
    grads["s5_lam_re"], grads["s5_lam_im"], grads["s5_log_dt"] = d_lr, d_li, d_ld.reshape(1, S5_GROUPS)
    grads["s5_b_re"], grads["s5_b_im"] = d_br, d_bi
    dzc, dlg8, dlb8, dbd8 = _conv_bwd_norm(dy_conv, zc, p["conv_ln_g"], p["conv_ln_b"], avg, tm)
    grads["conv_ln_g"] = dlg8.sum(axis=0, keepdims=True)
    grads["conv_ln_b"] = dlb8.sum(axis=0, keepdims=True)
    grads["conv_b_dw"] = dbd8.sum(axis=0, keepdims=True)
    dv, dw8 = carried("conv_bwd_taps", _conv_bwd_taps, dzc, v, w_dw, tm)
    grads["conv_w_dw"] = dw8.sum(axis=1)[:CONV_K]
    dx1, grads["mix_norm"], dub, dx1h = _mix_in_bwd(dx2, x1, p["mix_norm"], _from_segments(du_seg), dv, p["w_in"], tm)
    grads["w_in"] = _mm_tn(dub, h2, bf16, "dw_in")

    dx0 = _ffn_block_bwd(dx1, x, p, "ffn1", saved1, tm, grads, sched, parts=min(2, L // tm), dxh=dx1h)
    sched.alone("tail")
    return loss_terms, dx0


MESH = pl.DeviceIdType.MESH
ANY = pl.BlockSpec(memory_space=pl.ANY)


def _place():
    return lax.axis_index("x"), lax.axis_index("y"), lax.axis_index("c")


class _Exchange:
    def __init__(self, ins, out_shape, sems, start, finish):
        self.ins, self.out_shape, self.sems, self.start, self.finish = list(ins), list(out_shape), list(sems), start, finish
        self.out = None


def _pallas(body, *, ride=(), **kw):
    if not ride:
        return pl.pallas_call(body, **kw)

    def run(*args):
        out_shape = kw.get("out_shape", [])
        single = not isinstance(out_shape, (list, tuple))
        shapes = [out_shape] if single else list(out_shape)
        out_specs = [kw["out_specs"]] if single else list(kw.get("out_specs", []))
        grid = tuple(kw.get("grid", ()))
        scratch = list(kw.get("scratch_shapes", ()))
        n_in, n_out, n_scr = len(args), len(shapes), len(scratch)
        r_in = [len(e.ins) for e in ride]
        r_out = [len(e.out_shape) for e in ride]
        r_sem = [len(e.sems) for e in ride]

        def wrapped(*refs):
            own_in, refs = refs[:n_in], refs[n_in:]
            ex_in, refs = refs[:sum(r_in)], refs[sum(r_in):]
            own_out, refs = refs[:n_out], refs[n_out:]
            ex_out, refs = refs[:sum(r_out)], refs[sum(r_out):]
            own_scr, ex_sem = refs[:n_scr], refs[n_scr:]
            parts = []
            for e, ni, no, ns in zip(ride, r_in, r_out, r_sem):
                parts.append((e, ex_in[:ni], ex_out[:no], ex_sem[:ns]))
                ex_in, ex_out, ex_sem = ex_in[ni:], ex_out[no:], ex_sem[ns:]

            def at(step):
                def go():
                    for e, i, o, s in parts:
                        getattr(e, step)(i, o, s)
                if grid:
                    ids = [pl.program_id(d) for d in range(len(grid))]
                    when = [i == (0 if step == "start" else g - 1) for i, g in zip(ids, grid)]
                    pl.when(functools.reduce(lambda a, b: a & b, when))(go)
                else:
                    go()

            at("start")
            if body is not None:
                body(*own_in, *own_out, *own_scr)
            at("finish")

        outs = pl.pallas_call(
            wrapped, name=kw["name"], grid=grid,
            in_specs=list(kw.get("in_specs", [])) + [ANY] * sum(r_in),
            out_specs=out_specs + [ANY] * sum(r_out),
            out_shape=shapes + [s for e in ride for s in e.out_shape],
            scratch_shapes=scratch + [s for e in ride for s in e.sems],
            input_output_aliases=kw.get("input_output_aliases", {}),
            compiler_params=_cp(*["arbitrary"] * len(grid)),
        )(*args, *[a for e in ride for a in e.ins])
        own, rest = outs[:n_out], outs[n_out:]
        for e, no in zip(ride, r_out):
            e.out, rest = list(rest[:no]), rest[no:]
        return own[0] if single else own

    return run


def _exchange(ride, name):
    _pallas(None, ride=ride, name=name)()


def _gather(arrs):
    n = len(arrs)

    def copies(ins, outs, sems):
        send_sems, recv_sems, local_sems = sems
        x, y, c = _place()
        me, sibling = (x, y, c), (x, y, 1 - c)
        chips = [(1 - x, y), (x, 1 - y), (1 - x, 1 - y)]

        def place(a, block):
            return outs[a].at[block]

        def copy(a, k, block, to, src=None):
            px, py, pc = block
            dst = place(a, 4 * px + 2 * py + pc)
            return pltpu.make_async_remote_copy(
                src_ref=dst if src is None else src, dst_ref=dst, send_sem=send_sems.at[7 * a + k],
                recv_sem=recv_sems.at[7 * a + k], device_id=to, device_id_type=MESH)

        def own():
            local = [pltpu.make_async_copy(ins[a], place(a, 4 * x + 2 * y + c), local_sems.at[a]) for a in range(n)]
            remote = []
            for a in range(n):
                remote.append(copy(a, 0, me, sibling, src=ins[a]))
                remote += [copy(a, 1 + j, me, (*chip, c), src=ins[a]) for j, chip in enumerate(chips)]
            return local, remote

        return c, me, sibling, chips, copy, own

    def start(ins, outs, sems):
        local, remote = copies(ins, outs, sems)[-1]()
        for cp in local + remote:
            cp.start()

    def finish(ins, outs, sems):
        c, me, sibling, chips, copy, own = copies(ins, outs, sems)
        passed = []
        for j, chip in enumerate(chips):
            for a in range(n):
                copy(a, 1 + j, (*chip, c), me).wait_recv()
                passed.append(copy(a, 4 + j, (*chip, c), sibling))
                passed[-1].start()
        for a in range(n):
            copy(a, 0, sibling, me).wait_recv()
            for j, chip in enumerate(chips):
                copy(a, 4 + j, (*chip, 1 - c), me).wait_recv()
        local, remote = own()
        for cp in remote + passed:
            cp.wait_send()
        for cp in local:
            cp.wait()

    dma = pltpu.SemaphoreType.DMA
    shapes = [S((N_DEV, *a.shape), a.dtype) for a in arrs]
    return _Exchange(arrs, shapes, [dma((7 * n,)), dma((7 * n,)), dma((n,))], start, finish)


def _swap_with_sibling(gs):
    n = len(gs)

    def copies(ins, outs, sems):
        x, y, c = _place()
        return [pltpu.make_async_remote_copy(
            src_ref=ins[a].at[:, 1 - c], dst_ref=outs[a], send_sem=sems[0].at[a], recv_sem=sems[1].at[a],
            device_id=(x, y, 1 - c), device_id_type=MESH) for a in range(n)]

    def start(ins, outs, sems):
        for cp in copies(ins, outs, sems):
            cp.start()

    def finish(ins, outs, sems):
        for cp in copies(ins, outs, sems):
            cp.wait()

    dma = pltpu.SemaphoreType.DMA
    return _Exchange(gs, [S((N_CHIP, *g.shape[2:]), g.dtype) for g in gs], [dma((n,)), dma((n,))], start, finish)


def _swap_with_chips(ps):
    n = len(ps)

    def copies(ins, outs, sems):
        x, y, c = _place()
        q = 2 * x + y
        peers = [(x, 1 - y), (1 - x, y), (1 - x, 1 - y)]

        def copy(a, j, slot_from, slot_to):
            px, py = peers[j]
            return pltpu.make_async_remote_copy(
                src_ref=ins[a].at[slot_from], dst_ref=outs[a].at[slot_to], send_sem=sems[0].at[3 * a + j],
                recv_sem=sems[1].at[3 * a + j], device_id=(px, py, c), device_id_type=MESH)

        sends = lambda: [copy(a, j, 2 * peers[j][0] + peers[j][1], q) for a in range(n) for j in range(3)]
        lands = lambda: [copy(a, j, q, 2 * peers[j][0] + peers[j][1]) for a in range(n) for j in range(3)]
        return sends, lands

    def start(ins, outs, sems):
        for cp in copies(ins, outs, sems)[0]():
            cp.start()

    def finish(ins, outs, sems):
        sends, lands = copies(ins, outs, sems)
        for cp in lands():
            cp.wait_recv()
        for cp in sends():
            cp.wait_send()

    dma = pltpu.SemaphoreType.DMA
    return _Exchange(ps, [S(p.shape, p.dtype) for p in ps], [dma((3 * n,)), dma((3 * n,))], start, finish)


def _row_tile(rows, cols, itemsize):
    t = rows
    while t * cols * itemsize > (1 << 20) and t % 32 == 0:
        t //= 2
    return t


def _add_sibling(g4, st, core, name):
    _, R, C = st.shape
    tr = _row_tile(R, C, 1)

    def body(c_ref, g_ref, s_ref, o_ref):
        o_ref[...] = (g_ref[...].astype(f32) + s_ref[...].astype(f32)).astype(bf16)

    mine = pl.BlockSpec((None, None, tr, C), lambda q, i, c: (q, c[0], i, 0))
    return pl.pallas_call(
        body, name=name,
        grid_spec=pltpu.PrefetchScalarGridSpec(
            num_scalar_prefetch=1, grid=(N_CHIP, R // tr),
            in_specs=[mine,
                      pl.BlockSpec((None, tr, C), lambda q, i, c: (q, i, 0))],
            out_specs=pl.BlockSpec((None, tr, C), lambda q, i, c: (q, i, 0))),
        out_shape=S((N_CHIP, R, C), bf16),
        compiler_params=_cp("parallel", "parallel"),
    )(core, g4, st)


SMEM = pl.BlockSpec(memory_space=pltpu.SMEM)
VMEM = pl.BlockSpec(memory_space=pltpu.VMEM)


def _add_sibling_small(items, core, name):
    n = len(items)

    def body(c_ref, *refs):
        c = c_ref[0]
        for k in range(n):
            g_ref, s_ref, o_ref = refs[2 * k], refs[2 * k + 1], refs[2 * n + k]
            for q in range(N_CHIP):
                o_ref[q] = (g_ref[q, c].astype(f32) + s_ref[q].astype(f32)).astype(bf16)

    return pl.pallas_call(body, name=name, in_specs=[SMEM] + [VMEM] * (2 * n), out_specs=[VMEM] * n,
                          out_shape=[S(st.shape, bf16) for _, st in items],
                          compiler_params=pltpu.CompilerParams(vmem_limit_bytes=VMEM_LIMIT))(
        core, *[a for item in items for a in item])


def _adam_small(items, slots, name):
    n = len(items)

    def body(s_ref, *refs):
        ins, outs = refs[:5 * n], refs[5 * n:]
        for k in range(n):
            w_ref, m_ref, v_ref, p_ref, got_ref = ins[5 * k:5 * k + 5]
            g = p_ref[s_ref[0]].astype(f32)
            for j in range(1, N_CHIP):
                g = g + got_ref[s_ref[j]].astype(f32)
            outs[4 * k][...] = g
            outs[4 * k + 1][...], outs[4 * k + 2][...], outs[4 * k + 3][...] = _adamw(w_ref[...], g, m_ref[...], v_ref[...])

    out = pl.pallas_call(body, name=name, in_specs=[SMEM] + [VMEM] * (5 * n), out_specs=[VMEM] * (4 * n),
                         out_shape=[S(item[0].shape, f32) for item in items for _ in range(4)],
                         compiler_params=pltpu.CompilerParams(vmem_limit_bytes=VMEM_LIMIT))(
        slots, *[a for item in items for a in item])
    return [out[4 * k:4 * k + 4] for k in range(n)]


def _adamw(w, g, m, v):
    m = B1 * m + (1.0 - B1) * g
    v = B2 * v + (1.0 - B2) * (g * g)
    m_hat = m / (1.0 - B1 ** STEP)
    v_hat = v / (1.0 - B2 ** STEP)
    return -LR * (m_hat / (jnp.sqrt(v_hat) + ADAM_EPS) + WD * w), m, v


def _adam_sharded(items, slots, name):
    n = len(items)
    R, C = items[0][0].shape
    tr = _row_tile(R, C, 4 * n)

    def body(s_ref, *refs):
        ins, outs = refs[:7 * n], refs[7 * n:]
        for k in range(n):
            w_ref, m_ref, v_ref, p_ref, a_ref, b_ref, c_ref = ins[7 * k:7 * k + 7]
            g = p_ref[...].astype(f32) + a_ref[...].astype(f32) + b_ref[...].astype(f32) + c_ref[...].astype(f32)
            outs[4 * k][...] = g
            outs[4 * k + 1][...], outs[4 * k + 2][...], outs[4 * k + 3][...] = _adamw(w_ref[...], g, m_ref[...], v_ref[...])

    shard = pl.BlockSpec((tr, C), lambda i, s: (i, 0))
    slot = lambda k: pl.BlockSpec((None, tr, C), lambda i, s: (s[k], i, 0))
    out = pl.pallas_call(
        body, name=name,
        grid_spec=pltpu.PrefetchScalarGridSpec(
            num_scalar_prefetch=1, grid=(R // tr,),
            in_specs=[shard, shard, shard, slot(0), slot(1), slot(2), slot(3)] * n,
            out_specs=[shard] * (4 * n)),
        out_shape=[S((R, C), f32)] * (4 * n),
        compiler_params=_cp("parallel"),
    )(slots, *[a for w, m, v, part, got in items for a in (w, m, v, part, got, got, got)])
    return [out[4 * k:4 * k + 4] for k in range(n)]


def _adam_replicated(items, loss_terms, name):
    n = len(items)
    has_loss = loss_terms is not None

    def total(ref):
        g = ref[0]
        for d in range(1, N_DEV):
            g = g + ref[d]
        return g

    def body(*refs):
        ins, outs = refs[:4 * n + has_loss], refs[4 * n + has_loss:]
        for i in range(n):
            w_ref, m_ref, v_ref, g_ref = ins[4 * i:4 * i + 4]
            g = total(g_ref)
            outs[4 * i][...] = g
            outs[4 * i + 1][...], outs[4 * i + 2][...], outs[4 * i + 3][...] = _adamw(w_ref[...], g, m_ref[...], v_ref[...])
        if has_loss:
            outs[-1][...] = jnp.sum(total(ins[-1]), keepdims=True)

    flat = [a for item in items for a in item] + ([loss_terms] if has_loss else [])
    shapes = [S(item[0].shape, f32) for item in items for _ in range(4)] + ([S((1, 1), f32)] if has_loss else [])
    out = pl.pallas_call(body, name=name, out_shape=shapes,
                         compiler_params=pltpu.CompilerParams(vmem_limit_bytes=VMEM_LIMIT))(*flat)
    return [out[4 * i:4 * i + 4] for i in range(n)], (out[-1] if has_loss else None)


WEIGHTS = ["ffn1_norm", "ffn1_w_gate", "ffn1_w_up", "ffn1_w_down", "mix_norm", "w_in", "s5_lam_re", "s5_lam_im", "s5_log_dt",
           "s5_b_re", "s5_b_im", "s5_c_re", "s5_c_im", "s5_d", "s5_w_glu", "s5_b_glu", "conv_w_dw", "conv_b_dw", "conv_ln_g",
           "conv_ln_b", "w_out", "ffn2_norm", "ffn2_w_gate", "ffn2_w_up", "ffn2_w_down", "final_norm"]
SHARDED = ["ffn1_w_gate", "ffn1_w_up", "ffn1_w_down", "w_in", "s5_w_glu", "conv_w_dw", "w_out", "ffn2_w_gate", "ffn2_w_up",
           "ffn2_w_down"]
REPLICATED = [n for n in WEIGHTS if n not in SHARDED]
TRANSPOSED = ["ffn1_w_gate", "ffn1_w_up", "ffn2_w_gate", "ffn2_w_up", "w_in"]


def _shard_to_wire(n, w):
    if n == "conv_w_dw":
        return jnp.pad(w, ((0, CONV_HALO - CONV_K), (0, 0)))
    return w.astype(bf16)


def _to_wire(shards, ride):
    names = list(shards)
    shapes = [jax.eval_shape(functools.partial(_shard_to_wire, n), shards[n]) for n in names]

    def body(*refs):
        for src, dst in zip(refs[:len(names)], refs[len(names):]):
            (r, c), (rp, cp) = src.shape, dst.shape
            dst[:r, :c] = src[...].astype(dst.dtype)
            if cp > c:
                dst[:, c:] = jnp.zeros((rp, cp - c), dst.dtype)
            if rp > r:
                dst[r:, :] = jnp.zeros((rp - r, cp), dst.dtype)

    out = _pallas(body, ride=ride, name="to_wire", out_shape=shapes, in_specs=[pl.BlockSpec(memory_space=pltpu.VMEM)] * len(names),
                  out_specs=[pl.BlockSpec(memory_space=pltpu.VMEM)] * len(names))(*[shards[n] for n in names])
    return dict(zip(names, out))


def _gathered_to_full(n, g):
    if n == "conv_w_dw":
        return g.transpose(1, 0, 2).reshape(CONV_HALO, CONV_WIDTH)[:CONV_K]
    return g.reshape(N_DEV * g.shape[1], g.shape[2])


def _grad_to_blocks(n, g):
    if n == "conv_w_dw":
        g = jnp.pad(g, ((0, CONV_HALO - CONV_K), (0, 0)))
        g = g.reshape(g.shape[0], N_DEV, g.shape[1] // N_DEV).transpose(1, 0, 2)
    else:
        g = g.reshape(N_DEV, g.shape[0] // N_DEV, g.shape[1])
    return g.astype(bf16).reshape(N_CHIP, 2, *g.shape[1:])


REPLICATED_LATE = ["ffn1_norm"]
REPLICATED_HEAD = ["ffn2_norm", "final_norm"]
REPLICATED_MIX = ["mix_norm", "conv_b_dw", "conv_ln_g", "conv_ln_b"]
REPLICATED_S5 = [n for n in REPLICATED if n not in REPLICATED_LATE + REPLICATED_HEAD + REPLICATED_MIX]
REPLICATED_EARLY = REPLICATED_HEAD + REPLICATED_S5 + REPLICATED_MIX

PLAN = {
    "start": [("gather", ["ffn1_w_gate", "ffn1_w_up"])],
    "ffn1_up": [("gather", ["ffn1_w_down", "w_in", "w_out", "s5_w_glu", "conv_w_dw"])],
    "s5_forward": [("gather", ["ffn2_w_gate", "ffn2_w_up"])],
    "conv_fwd": [("gather", ["ffn2_w_down"])],
    "ffn2_dw_up": [("sibling", ["ffn2_w_gate"])],
    "ffn2_dw_down": [("sibling", ["ffn2_w_up"])],
    "mix_out_bwd": [("sibling", ["ffn2_w_down"]), ("replicated", REPLICATED_HEAD)],
    "s5_backward": [("chips", ["ffn2_w_gate", "ffn2_w_up"])],
    "conv_bwd_taps": [("chips", ["ffn2_w_down"]), ("replicated", REPLICATED_S5)],
    "ffn1_dw_down": [("sibling", ["w_in", "s5_w_glu", "conv_w_dw", "w_out"]), ("replicated", REPLICATED_MIX)],
    "ffn1_bwd_act": [("chips", ["w_in", "s5_w_glu", "conv_w_dw", "w_out"]), ("sibling", ["ffn1_w_down"])],
    "ffn1_dw_gate": [("chips", ["ffn1_w_down"])],
    "ffn1_dw_up": [("sibling", ["ffn1_w_gate"])],
    "ffn1_bwd_in_0": [("chips", ["ffn1_w_gate"]), ("sibling", ["ffn1_w_up"])],
    "ffn1_bwd_in_1": [("chips", ["ffn1_w_up"])],
    "tail": [("replicated", REPLICATED_LATE)],
}


class _Schedule:
    def __init__(self, wire, p, grads, core):
        self.wire, self.p, self.grads, self.core = wire, p, grads, core
        self.partial, self.reduced, self.everyone, self.pending = {}, {}, {}, []

    def before(self, point):
        assert not self.pending
        for kind, names in PLAN.get(point, ()):
            if kind == "gather":
                given = [self.wire[n] for n in names]
                ex = _gather(given)
            elif kind == "sibling":
                given = [_grad_to_blocks(n, self.grads[n]) for n in names]
                ex = _swap_with_sibling(given)
            elif kind == "chips":
                given = [self.partial.pop(n) for n in names]
                ex = _swap_with_chips(given)
            else:
                names = names + ["loss_terms"] * (names is REPLICATED_HEAD)
                given = [self.grads[n].reshape(self.p[n].shape) if n in self.p else self.grads[n] for n in names]
                ex = _gather(given)
            self.pending.append((kind, names, given, ex))
        return [ex for _, _, _, ex in self.pending]

    def after(self, point):
        for kind, names, given, ex in self.pending:
            if kind == "gather":
                for n, g in zip(names, ex.out):
                    self.p[n] = _gathered_to_full(n, g)
            elif kind == "sibling":
                if len(names) > 1:
                    sums = _add_sibling_small(list(zip(given, ex.out)), self.core, "reduce_add_" + names[0])
                else:
                    sums = [_add_sibling(given[0], ex.out[0], self.core, "reduce_add_" + names[0])]
                self.partial.update(zip(names, sums))
            elif kind == "chips":
                for n, part, got in zip(names, given, ex.out):
                    self.reduced[n] = (part, got)
            else:
                self.everyone.update(zip(names, ex.out))
        self.pending = []

    def alone(self, point):
        _exchange(self.before(point), point)
        self.after(point)


def kernel(x, ffn1_norm, ffn1_w_gate, ffn1_w_up, ffn1_w_down, mix_norm, w_in, s5_lam_re, s5_lam_im, s5_log_dt, s5_b_re, s5_b_im, s5_c_re, s5_c_im, s5_d, s5_w_glu, s5_b_glu, conv_w_dw, conv_b_dw, conv_ln_g, conv_ln_b, w_out, ffn2_norm, ffn2_w_gate, ffn2_w_up, ffn2_w_down, final_norm, loss_target, m_ffn1_norm, m_ffn1_w_gate, m_ffn1_w_up, m_ffn1_w_down, m_mix_norm, m_w_in, m_s5_lam_re, m_s5_lam_im, m_s5_log_dt, m_s5_b_re, m_s5_b_im, m_s5_c_re, m_s5_c_im, m_s5_d, m_s5_w_glu, m_s5_b_glu, m_conv_w_dw, m_conv_b_dw, m_conv_ln_g, m_conv_ln_b, m_w_out, m_ffn2_norm, m_ffn2_w_gate, m_ffn2_w_up, m_ffn2_w_down, m_final_norm, v_ffn1_norm, v_ffn1_w_gate, v_ffn1_w_up, v_ffn1_w_down, v_mix_norm, v_w_in, v_s5_lam_re, v_s5_lam_im, v_s5_log_dt, v_s5_b_re, v_s5_b_im, v_s5_c_re, v_s5_c_im, v_s5_d, v_s5_w_glu, v_s5_b_glu, v_conv_w_dw, v_conv_b_dw, v_conv_ln_g, v_conv_ln_b, v_w_out, v_ffn2_norm, v_ffn2_w_gate, v_ffn2_w_up, v_ffn2_w_down, v_final_norm):
    args = locals()
    w = {n: args[n] for n in WEIGHTS}
    m = {n: args["m_" + n] for n in WEIGHTS}
    v = {n: args["v_" + n] for n in WEIGHTS}
    xq, yq, cq = _place()
    q = 2 * xq + yq
    slots = jnp.stack([q, q ^ 1, q ^ 2, q ^ 3]).astype(jnp.int32)

    def shard2d(n, a):
        a = a.reshape(a.shape[-2:])
        return a.T if n in TRANSPOSED else a

    def view(n, a):
        if n.startswith("s5_b_") and a.ndim == 4:
            return a[0].transpose(0, 2, 1)
        return a[0] if a.ndim >= 3 else a.reshape(1, -1)

    def unview(n, a):
        return (a.transpose(0, 2, 1) if n.startswith("s5_b_") and a.ndim == 3 else a).reshape(w[n].shape)

    p = {n: view(n, w[n]) for n in REPLICATED}
    grads = {}
    first = PLAN["start"][0][1]
    wire = {n: _shard_to_wire(n, shard2d(n, w[n])) for n in first}
    sched = _Schedule(wire, p, grads, jnp.reshape(cq, (1,)).astype(jnp.int32))
    wire.update(_to_wire({n: shard2d(n, w[n]) for n in SHARDED if n not in first}, sched.before("start")))
    sched.after("start")
    _, dx = _local_step(x[0], loss_target[0], p, grads, sched)

    out = {}
    groups = [[n for n in SHARDED if n.startswith(tag)] for tag in ("ffn1", "ffn2")]
    for names in groups + [[n for n in SHARDED if not n.startswith("ffn")]]:
        def fit(n, a):
            a = shard2d(n, a)
            return jnp.pad(a, ((0, sched.reduced[n][1].shape[1] - a.shape[0]), (0, 0)))

        items = [(fit(n, w[n]), fit(n, m[n]), fit(n, v[n]), *sched.reduced[n]) for n in names]
        update = _adam_sharded if names[0].startswith("ffn") else _adam_small
        for n, res in zip(names, update(items, slots, "adam_" + names[0])):
            back = lambda r: r[:shard2d(n, w[n]).shape[0]]
            out[n] = [(back(r).T if n in TRANSPOSED else back(r)).reshape(w[n].shape) for r in res]

    for names in (REPLICATED_EARLY, REPLICATED_LATE):
        items = [(view(n, w[n]), view(n, m[n]), view(n, v[n]), sched.everyone[n]) for n in names]
        res, total = _adam_replicated(items, sched.everyone.get("loss_terms") if names is REPLICATED_EARLY else None,
                                      "adam_" + names[0])
        for n, r in zip(names, res):
            out[n] = [unview(n, a) for a in r]
        if total is not None:
            loss = total.reshape(())

    return (loss, dx.reshape(x.shape), *[out[n][0] for n in WEIGHTS], *[out[n][1] for n in WEIGHTS],
            *[out[n][2] for n in WEIGHTS], *[out[n][3] for n in WEIGHTS])
```

```python
import functools

import jax
import jax.numpy as jnp
from jax import lax
from jax.experimental import pallas as pl
from jax.experimental.pallas import tpu as pltpu

f32 = jnp.float32
bf16 = jnp.bfloat16
S = jax.ShapeDtypeStruct

N_DEV = 8
N_CHIP = 4
D_MODEL = 1024
D_FF = 2816
FF_CHUNKS = [(0, 768), (768, 1536), (1536, 2304), (2304, D_FF)]
S5_WIDTH = 512
S5_GROUPS = 32
S5_GROUP_CH = 16
S5_STATE = 64
S5_LANES = S5_GROUPS * S5_STATE
CONV_WIDTH = 512
CONV_K = 31
CONV_HALO = 32
CONV_HEAD = 64
CONV_ROWS = 32
IN_COLS = S5_WIDTH + 2 * CONV_WIDTH
SEGMENTS = 8
SCAN_LANES = 512
EPS = 1e-6
LR, B1, B2, ADAM_EPS, WD, STEP = 0.001, 0.9, 0.999, 1e-08, 0.01, 10
VMEM_LIMIT = 56 * 1024 * 1024

NN = (((1,), (0,)), ((), ()))
NT = (((1,), (1,)), ((), ()))
TN = (((0,), (0,)), ((), ()))


def _dot(a, b, dims=NN):
    return lax.dot_general(a, b, dims, preferred_element_type=f32)


def _cp(*sem):
    return pltpu.CompilerParams(dimension_semantics=sem, vmem_limit_bytes=VMEM_LIMIT)


def _rms(x, g):
    return x * lax.rsqrt(jnp.mean(x * x, axis=-1, keepdims=True) + EPS) * g


def _rms_bwd(x, g, dh):
    _, vjp = jax.vjp(_rms, x, g)
    return vjp(dh)


def _sigmoid(x):
    return 1.0 / (1.0 + jnp.exp(-x))


def _gelu(x):
    return 0.5 * x * (1.0 + jnp.tanh(0.7978845608028654 * (x + 0.044715 * x * x * x)))


def _rows8(x):
    t, c = x.shape
    return x.reshape(t // 8, 8, c).sum(axis=0)


def _full(shape):
    return pl.BlockSpec(shape, lambda *_: (0,) * len(shape))


def _resident(shape):
    return pl.BlockSpec(shape, lambda *_: (0,) * len(shape), pipeline_mode=pl.Buffered(1))


def _ffn_up(x, g, wg, wu, tm, tag, mixed=None, ride=()):
    L = x.shape[0]

    def body(x_ref, g_ref, wg_ref, wu_ref, *rest):
        h_ref, dadg_ref, dadu_ref, a_ref = rest[-5:-1] if mixed else rest[-4:]
        x = x_ref[...]
        if mixed:
            ys_ref, yc_ref, wo_ref = rest[:3]
            x = x + _dot(ys_ref[...], wo_ref[:S5_WIDTH, :]) + _dot(yc_ref[...], wo_ref[S5_WIDTH:, :])
            rest[-1][...] = x
        h = _rms(x, g_ref[...]).astype(bf16)
        h_ref[...] = h
        for lo, hi in FF_CHUNKS:
            cols = slice(lo, hi)
            gate =_dot(h, wg_ref[cols, :], NT)
            up = _dot(h, wu_ref[cols, :], NT)
            sig = _sigmoid(gate)
            silu = gate * sig
            dadg_ref[:, cols] = (up * (sig + silu * (1.0 - sig))).astype(bf16)
            dadu_ref[:, cols] = silu.astype(bf16)
            a_ref[:, cols] = (silu * up).astype(bf16)

    row = pl.BlockSpec((tm, D_MODEL), lambda i: (i, 0))
    wide = pl.BlockSpec((tm, D_FF), lambda i: (i, 0))
    half = pl.BlockSpec((tm, S5_WIDTH), lambda i: (i, 0))
    return _pallas(
        body, ride=ride, name=tag + "_up", grid=(L // tm,),
        in_specs=[row, _full((1, D_MODEL)), _resident((D_FF, D_MODEL)), _resident((D_FF, D_MODEL))]
        + ([half, half, _resident((D_MODEL, D_MODEL))] if mixed else []),
        out_specs=[row, wide, wide, wide] + [row] * bool(mixed),
        out_shape=[S((L, D_MODEL), bf16)] + [S((L, D_FF), bf16)] * 3 + [S((L, D_MODEL), f32)] * bool(mixed),
        compiler_params=_cp("parallel"),
    )(x, g, wg, wu, *(mixed or ()))


def _ffn_down(x, a, wd, tm, tag, mixer=None, ride=()):
    L = x.shape[0]

    def body(x_ref, a_ref, wd_ref, *rest):
        xo = x_ref[...] + 0.5 * _dot(a_ref[...], wd_ref[...])
        if not mixer:
            rest[0][...] = xo
            return
        g_ref, w_ref, o_ref, h_ref, us_ref, v_ref = rest
        o_ref[...] = xo
        h = _rms(xo, g_ref[...]).astype(bf16)
        h_ref[...] = h
        u = _dot(h, w_ref[...], NT)
        us_ref[...] = u[:, :S5_WIDTH]
        v_ref[...] = u[:, S5_WIDTH:]

    row = lambda c: pl.BlockSpec((tm, c), lambda i: (i, 0))
    extra_in = [_full((1, D_MODEL)), _resident((IN_COLS, D_MODEL))] if mixer else []
    extra_out = [(D_MODEL, bf16), (S5_WIDTH, f32), (2 * CONV_WIDTH, f32)] if mixer else []
    out = _pallas(
        body, ride=ride, name=tag + "_down", grid=(L // tm,),
        in_specs=[row(D_MODEL), row(D_FF), _resident((D_FF, D_MODEL))] + extra_in,
        out_specs=[row(D_MODEL)] + [row(c) for c, _ in extra_out],
        out_shape=[S((L, D_MODEL), f32)] + [S((L, c), t) for c, t in extra_out],
        compiler_params=_cp("parallel"),
    )(x, a, wd, *(mixer or ()))
    return out if mixer else out[0]


def _ffn_down_loss(x, a, wd, target, g, tm, tag):
    L = x.shape[0]

    def body(x_ref, a_ref, wd_ref, t_ref, g_ref, dx_ref, dg_ref, l_ref):
        @pl.when(pl.program_id(0) == 0)
        def _():
            dg_ref[...] = jnp.zeros_like(dg_ref)
            l_ref[...] = jnp.zeros_like(l_ref)

        xo = x_ref[...] + 0.5 * _dot(a_ref[...], wd_ref[...])
        g = g_ref[...]
        e = _rms(xo, g) - t_ref[...]
        l_ref[...] += _rows8(e * e) * (0.5 / D_MODEL)
        dx, dg = _rms_bwd(xo, g, e * (1.0 / D_MODEL))
        dx_ref[...] = dx
        dg_ref[...] += dg

    row = pl.BlockSpec((tm, D_MODEL), lambda i: (i, 0))
    return pl.pallas_call(
        body, name=tag + "_down_loss", grid=(L // tm,),
        in_specs=[row, pl.BlockSpec((tm, D_FF), lambda i: (i, 0)), _resident((D_FF, D_MODEL)), row, _full((1, D_MODEL))],
        out_specs=[row, _full((1, D_MODEL)), _full((8, D_MODEL))],
        out_shape=[S((L, D_MODEL), f32), S((1, D_MODEL), f32), S((8, D_MODEL), f32)],
        compiler_params=_cp("arbitrary"),
    )(x, a, wd, target, g)


def _ffn_bwd_act(dxo, wd, dadg, dadu, tm, tag, ride=()):
    L = dxo.shape[0]

    def body(dx_ref, wd_ref, dadg_ref, dadu_ref, dgate_ref, dup_ref, dxh_ref):
        dxh = (0.5 * dx_ref[...]).astype(bf16)
        dxh_ref[...] = dxh
        for lo, hi in FF_CHUNKS:
            cols = slice(lo, hi)
            da =_dot(dxh, wd_ref[cols, :], NT)
            dgate_ref[:, cols] = (da * dadg_ref[:, cols].astype(f32)).astype(bf16)
            dup_ref[:, cols] = (da * dadu_ref[:, cols].astype(f32)).astype(bf16)

    row = pl.BlockSpec((tm, D_MODEL), lambda i: (i, 0))
    wide = pl.BlockSpec((tm, D_FF), lambda i: (i, 0))
    return _pallas(
        body, ride=ride, name=tag + "_bwd_act", grid=(L // tm,),
        in_specs=[row, _resident((D_FF, D_MODEL)), wide, wide],
        out_specs=[wide, wide, row],
        out_shape=[S((L, D_FF), bf16), S((L, D_FF), bf16), S((L, D_MODEL), bf16)],
        compiler_params=_cp("parallel"),
    )(dxo, wd, dadg, dadu)


def _ffn_bwd_in(dxo, x, g, dgate, dup, wg, wu, tm, name, tiles=None, into=None, ride=()):
    L = x.shape[0]
    first, count = tiles or (0, L // tm)

    def body(dxo_ref, x_ref, g_ref, dgate_ref, dup_ref, wg_ref, wu_ref, *rest):
        dx_ref, dg_ref = rest[-2:]

        @pl.when(pl.program_id(0) == 0)
        def _():
            dg_ref[...] = jnp.zeros_like(dg_ref)

        dh = _dot(dgate_ref[...], wg_ref[...]) + _dot(dup_ref[...], wu_ref[...])
        dx, dg = _rms_bwd(x_ref[...], g_ref[...], dh)
        dx_ref[...] = dxo_ref[...] + dx
        dg_ref[...] += dg

    row = pl.BlockSpec((tm, D_MODEL), lambda i: (first + i, 0))
    wide = pl.BlockSpec((tm, D_FF), lambda i: (first + i, 0))
    return _pallas(
        body, ride=ride, name=name, grid=(count,),
        in_specs=[row, row, _full((1, D_MODEL)), wide, wide, _resident((D_FF, D_MODEL)), _resident((D_FF, D_MODEL))]
        + [ANY] * (into is not None),
        out_specs=[row, _full((1, D_MODEL))],
        out_shape=[S((L, D_MODEL), f32), S((1, D_MODEL), f32)],
        input_output_aliases={7: 0} if into is not None else {},
        compiler_params=_cp("arbitrary"),
    )(dxo, x, g, dgate, dup, wg, wu, *([into] if into is not None else []))


def _mm_tn(a, b, out_dtype, name, tm=512, tn=1024, ride=()):
    L, M = a.shape
    N = b.shape[1]
    tm, tn = min(tm, M), min(tn, N)
    while M % tm:
        tm //= 2
    while N % tn:
        tn //= 2

    def body(a_ref, b_ref, o_ref):
        o_ref[...] = _dot(a_ref[...].astype(bf16), b_ref[...].astype(bf16), TN).astype(out_dtype)

    return _pallas(
        body, ride=ride, name=name, grid=(M // tm, N // tn),
        in_specs=[pl.BlockSpec((L, tm), lambda i, j: (0, i)), pl.BlockSpec((L, tn), lambda i, j: (0, j))],
        out_specs=pl.BlockSpec((tm, tn), lambda i, j: (i, j)),
        out_shape=S((M, N), out_dtype),
        compiler_params=_cp("parallel", "parallel"),
    )(a, b)


def _mix_in_bwd(dxo, x, g, du_s5, dv, w_in, tm):
    L = x.shape[0]

    def body(dxo_ref, x_ref, g_ref, dus_ref, dv_ref, w_ref, dx_ref, dg_ref, dub_ref, dxh_ref):
        @pl.when(pl.program_id(0) == 0)
        def _():
            dg_ref[...] = jnp.zeros_like(dg_ref)

        dus = dus_ref[...].astype(bf16)
        dvb = dv_ref[...].astype(bf16)
        dub_ref[:, :S5_WIDTH] = dus
        dub_ref[:, S5_WIDTH:] = dvb
        dh = _dot(dus, w_ref[:S5_WIDTH, :]) + _dot(dvb, w_ref[S5_WIDTH:, :])
        dx, dg = _rms_bwd(x_ref[...], g_ref[...], dh)
        dx = dxo_ref[...] + dx
        dx_ref[...] = dx
        dxh_ref[...] = (0.5 * dx).astype(bf16)
        dg_ref[...] += dg

    row = lambda c: pl.BlockSpec((tm, c), lambda i: (i, 0))
    return pl.pallas_call(
        body, name="mix_in_bwd", grid=(L // tm,),
        in_specs=[row(D_MODEL), row(D_MODEL), _full((1, D_MODEL)), row(S5_WIDTH), row(2 * CONV_WIDTH),
                  _full((IN_COLS, D_MODEL))],
        out_specs=[row(D_MODEL), _full((1, D_MODEL)), row(IN_COLS), row(D_MODEL)],
        out_shape=[S((L, D_MODEL), f32), S((1, D_MODEL), f32), S((L, IN_COLS), bf16), S((L, D_MODEL), bf16)],
        compiler_params=_cp("arbitrary"),
    )(dxo, x, g, du_s5, dv, w_in)


def _dw_out(y_s5, y_conv, dxb, tn=512):
    L = dxb.shape[0]

    def body(ys_ref, yc_ref, b_ref, o_ref):
        b = b_ref[...]
        o_ref[:S5_WIDTH, :] = _dot(ys_ref[...], b, TN).astype(bf16)
        o_ref[S5_WIDTH:, :] = _dot(yc_ref[...], b, TN).astype(bf16)

    return pl.pallas_call(
        body, name="dw_out", grid=(D_MODEL // tn,),
        in_specs=[_full((L, S5_WIDTH)), _full((L, CONV_WIDTH)), pl.BlockSpec((L, tn), lambda j: (0, j))],
        out_specs=pl.BlockSpec((S5_WIDTH + CONV_WIDTH, tn), lambda j: (0, j)),
        out_shape=S((S5_WIDTH + CONV_WIDTH, D_MODEL), bf16),
        compiler_params=_cp("parallel"),
    )(y_s5, y_conv, dxb)


def _mix_out_bwd(dx, w_out, tm, ride=()):
    L = dx.shape[0]

    def body(dx_ref, w_ref, dys_ref, dyc_ref, dxb_ref):
        dxb = dx_ref[...].astype(bf16)
        dxb_ref[...] = dxb
        dys_ref[...] = _dot(dxb, w_ref[:S5_WIDTH, :], NT)
        dyc_ref[...] = _dot(dxb, w_ref[S5_WIDTH:, :], NT)

    row = lambda c: pl.BlockSpec((tm, c), lambda i: (i, 0))
    return _pallas(
        body, ride=ride, name="mix_out_bwd", grid=(L // tm,),
        in_specs=[row(D_MODEL), _full((D_MODEL, D_MODEL))],
        out_specs=[row(S5_WIDTH), row(CONV_WIDTH), row(D_MODEL)],
        out_shape=[S((L, S5_WIDTH), f32), S((L, CONV_WIDTH), f32), S((L, D_MODEL), bf16)],
        compiler_params=_cp("parallel"),
    )(dx, w_out)


def _s5_discretise(lam_re, lam_im, log_dt, b_re, b_im):
    dt = jnp.exp(log_dt)
    mag = jnp.exp(lam_re * dt)
    abar_re = mag * jnp.cos(lam_im * dt)
    abar_im = mag * jnp.sin(lam_im * dt)
    den = lam_re * lam_re + lam_im * lam_im
    num_re = abar_re - 1.0
    f_re = ((num_re * lam_re + abar_im * lam_im) / den)[:, None, :]
    f_im = ((abar_im * lam_re - num_re * lam_im) / den)[:, None, :]
    return abar_re, abar_im, f_re * b_re - f_im * b_im, f_re * b_im + f_im * b_re


def _s5_params(lam_re, lam_im, log_dt, b_re, b_im):
    def body(lr, li, ld, br, bi, ar_ref, ai_ref, bbr_ref, bbi_ref):
        ar, ai, bbr, bbi = _s5_discretise(lr[...], li[...], ld[...], br[...], bi[...])
        ar_ref[...], ai_ref[...], bbr_ref[...], bbi_ref[...] = ar, ai, bbr, bbi

    gp = S((S5_GROUPS, S5_STATE), f32)
    gcp = S((S5_GROUPS, S5_GROUP_CH, S5_STATE), f32)
    return pl.pallas_call(body, name="s5_params", out_shape=[gp, gp, gcp, gcp])(lam_re, lam_im, log_dt, b_re, b_im)


def _s5_params_bwd(lam_re, lam_im, log_dt, b_re, b_im, d_ar, d_ai, d_bbr, d_bbi):
    def body(lr, li, ld, br, bi, car, cai, cbr, cbi, o_lr, o_li, o_ld, o_br, o_bi):
        _, vjp = jax.vjp(_s5_discretise, lr[...], li[...], ld[...], br[...], bi[...])
        o_lr[...], o_li[...], o_ld[...], o_br[...], o_bi[...] = vjp((car[...], cai[...], cbr[...], cbi[...]))

    gp = S((S5_GROUPS, S5_STATE), f32)
    gcp = S((S5_GROUPS, S5_GROUP_CH, S5_STATE), f32)
    return pl.pallas_call(body, name="s5_params_bwd", out_shape=[gp, gp, S((S5_GROUPS, 1), f32), gcp, gcp])(
        lam_re, lam_im, log_dt, b_re, b_im, d_ar, d_ai, d_bbr, d_bbi)


def _cmul(ar, ai, br, bi):
    return ar * br - ai * bi, ar * bi + ai * br


def _segment_starts(er, ei, ar, ai, steps, reverse):
    pr, pi = ar, ai
    n = 1
    while n < steps:
        pr, pi = _cmul(pr, pi, pr, pi)
        n *= 2
    assert n == steps
    row = lax.broadcasted_iota(jnp.int32, (SEGMENTS, SCAN_LANES), 0)
    hr = jnp.zeros((1, SCAN_LANES), f32)
    hi = jnp.zeros((1, SCAN_LANES), f32)
    out_r = jnp.zeros((SEGMENTS, SCAN_LANES), f32)
    out_i = jnp.zeros((SEGMENTS, SCAN_LANES), f32)
    order = range(SEGMENTS - 1, 0, -1) if reverse else range(0, SEGMENTS - 1)
    for r in order:
        qr, qi = _cmul(pr, pi, hr, hi)
        hr, hi = qr + er[r:r + 1, :], qi + ei[r:r + 1, :]
        nxt = r - 1 if reverse else r + 1
        out_r = jnp.where(row == nxt, hr, out_r)
        out_i = jnp.where(row == nxt, hi, out_i)
    return out_r, out_i


def _s5_read_bwd(dout, y_lin, u, d_skip, w_glu, b_glu, tm):
    L = u.shape[0]

    def body(do_ref, yl_ref, u_ref, d_ref, w_ref, b_ref, dyl_ref, du_ref, dd_ref, dw_ref, db_ref):
        @pl.when(pl.program_id(0) == 0)
        def _():
            dd_ref[...] = jnp.zeros_like(dd_ref)
            dw_ref[...] = jnp.zeros_like(dw_ref)
            db_ref[...] = jnp.zeros_like(db_ref)

        u, d, dout = u_ref[...], d_ref[...], do_ref[...]
        y, gelu_vjp = jax.vjp(_gelu, yl_ref[...] + d * u)
        yb = y.astype(bf16)
        sig = _sigmoid(_dot(yb, w_ref[...]) + b_ref[...])
        dz = dout * y * sig * (1.0 - sig)
        dzb = dz.astype(bf16)
        dy = dout * sig + _dot(dzb, w_ref[...], NT)
        (dyp,) = gelu_vjp(dy)
        dyl_ref[...] = dyp.astype(bf16)
        du_ref[...] = d * dyp
        dd_ref[...] += _rows8(dyp * u)
        db_ref[...] += _rows8(dz)
        dw_ref[...] += _dot(yb, dzb, TN)

    row = pl.BlockSpec((tm, S5_WIDTH), lambda i: (i, 0))
    vec = _full((1, S5_WIDTH))
    part = _full((8, S5_WIDTH))
    return pl.pallas_call(
        body, name="s5_read_bwd", grid=(L // tm,),
        in_specs=[row, row, row, vec, _full((S5_WIDTH, S5_WIDTH)), vec],
        out_specs=[row, row, part, _full((S5_WIDTH, S5_WIDTH)), part],
        out_shape=[S((L, S5_WIDTH), bf16), S((L, S5_WIDTH), f32), S((8, S5_WIDTH), f32),
                   S((S5_WIDTH, S5_WIDTH), f32), S((8, S5_WIDTH), f32)],
        compiler_params=_cp("arbitrary"),
    )(dout, y_lin, u, d_skip, w_glu, b_glu)


S5_CHUNK_CH = SCAN_LANES // S5_STATE * S5_GROUP_CH


def _s5_two_phase(L, bi):
    rows = bi * SEGMENTS
    nb = L // rows
    whole = pltpu.VMEM((L // SEGMENTS, SEGMENTS, SCAN_LANES), f32)
    mat = pl.BlockSpec((S5_CHUNK_CH, SCAN_LANES), lambda c, j: (c, c))
    vec = pl.BlockSpec((1, SCAN_LANES), lambda c, j: (0, c))
    tile = pl.BlockSpec((SEGMENTS, SCAN_LANES), lambda c, j: (0, c))
    return rows, nb, whole, mat, vec, tile


def _s5_forward(u, a_re, a_im, bb_re, bb_im, cc_re, cc_im, bi, ride=()):
    L = u.shape[0]
    rows, nb, whole, mat, vec, _ = _s5_two_phase(L, bi)

    def body(u_ref, ar_ref, ai_ref, br_ref, bi_ref, cr_ref, ci_ref, sr_ref, si_ref, yl_ref, hr_ref, hi_ref, dr_ref, di_ref):
        j = pl.program_id(1)
        ar = jnp.broadcast_to(ar_ref[...], (SEGMENTS, SCAN_LANES))
        ai = jnp.broadcast_to(ai_ref[...], (SEGMENTS, SCAN_LANES))

        @pl.when(j == 0)
        def _():
            hr_ref[...] = jnp.zeros_like(hr_ref)
            hi_ref[...] = jnp.zeros_like(hi_ref)

        @pl.when(j < nb)
        def _():
            base = j * bi
            ub = u_ref[...].astype(bf16)
            dr_ref[pl.ds(base, bi)] = _dot(ub, br_ref[...]).reshape(bi, SEGMENTS, SCAN_LANES)
            di_ref[pl.ds(base, bi)] = _dot(ub, bi_ref[...]).reshape(bi, SEGMENTS, SCAN_LANES)

            def step(i, c):
                pr, pi = _cmul(ar, ai, c[0], c[1])
                return pr + dr_ref[base + i], pi + di_ref[base + i]

            hr_ref[...], hi_ref[...] = lax.fori_loop(0, bi, step, (hr_ref[...], hi_ref[...]), unroll=True)

        @pl.when(j == nb - 1)
        def _():
            hr_ref[...], hi_ref[...] = _segment_starts(hr_ref[...], hi_ref[...], ar_ref[...], ai_ref[...], L // SEGMENTS, False)

        @pl.when(j >= nb)
        def _():
            base = (j - nb) * bi

            def step(i, c):
                pr, pi = _cmul(ar, ai, c[0], c[1])
                nr, nim = pr + dr_ref[base + i], pi + di_ref[base + i]
                dr_ref[base + i] = nr
                di_ref[base + i] = nim
                return nr, nim

            hr_ref[...], hi_ref[...] = lax.fori_loop(0, bi, step, (hr_ref[...], hi_ref[...]), unroll=True)
            sr = dr_ref[pl.ds(base, bi)].reshape(rows, SCAN_LANES).astype(bf16)
            si = di_ref[pl.ds(base, bi)].reshape(rows, SCAN_LANES).astype(bf16)
            sr_ref[...] = sr
            si_ref[...] = si
            yl_ref[...] = _dot(sr, cr_ref[...], NT) - _dot(si, ci_ref[...], NT)

    u_spec = pl.BlockSpec((rows, S5_CHUNK_CH), lambda c, j: (jnp.minimum(j, nb - 1), c))
    late = lambda width: pl.BlockSpec((rows, width), lambda c, j: (jnp.maximum(j - nb, 0), c))
    return _pallas(
        body, ride=ride, name="s5_forward", grid=(S5_LANES // SCAN_LANES, 2 * nb),
        in_specs=[u_spec, vec, vec, mat, mat, mat, mat],
        out_specs=[late(SCAN_LANES), late(SCAN_LANES), late(S5_CHUNK_CH)],
        out_shape=[S((L, S5_LANES), bf16)] * 2 + [S((L, S5_WIDTH), f32)],
        scratch_shapes=[pltpu.VMEM((SEGMENTS, SCAN_LANES), f32)] * 2 + [whole] * 2,
        compiler_params=_cp("parallel", "arbitrary"),
    )(u, a_re, a_im, bb_re, bb_im, cc_re, cc_im)


def _s5_backward(dy, u, du_skip, s_re, s_im, a_re, a_im, bb_re, bb_im, cc_re, cc_im, bi, ride=()):
    L = u.shape[0]
    rows, nb, whole, mat, vec, tile = _s5_two_phase(L, bi)
    per = rows // 16

    def body(dy_ref, u_ref, dus_ref, sr_ref, si_ref, pr_ref, pi_ref, lr_ref, li_ref, ar_ref, ai_ref, br_ref, bi_ref, cr_ref,
             ci_ref, du_ref, dar_ref, dai_ref, dbr_ref, dbi_ref, dcr_ref, dci_ref, hr_ref, hi_ref, gr_ref, gi_ref, fr_ref, fi_ref):
        j = pl.program_id(1)
        ar = jnp.broadcast_to(ar_ref[...], (SEGMENTS, SCAN_LANES))
        ai = jnp.broadcast_to(ai_ref[...], (SEGMENTS, SCAN_LANES))

        @pl.when(j == 0)
        def _():
            for ref in (hr_ref, hi_ref, dar_ref, dai_ref, dbr_ref, dbi_ref, dcr_ref, dci_ref):
                ref[...] = jnp.zeros_like(ref)

        @pl.when(j < nb)
        def _():
            base = (nb - 1 - j) * bi
            dy = dy_ref[...]
            gr_ref[pl.ds(base, bi)] = _dot(dy, cr_ref[...]).reshape(bi, SEGMENTS, SCAN_LANES)
            gi_ref[pl.ds(base, bi)] = (-_dot(dy, ci_ref[...])).reshape(bi, SEGMENTS, SCAN_LANES)

            def step(n, c):
                i = base + bi - 1 - n
                qr, qi = _cmul(ar, ai, c[0], c[1])
                return qr + gr_ref[i], qi + gi_ref[i]

            hr_ref[...], hi_ref[...] = lax.fori_loop(0, bi, step, (hr_ref[...], hi_ref[...]), unroll=True)

        @pl.when(j == nb - 1)
        def _():
            hr_ref[...], hi_ref[...] = _segment_starts(hr_ref[...], hi_ref[...], ar_ref[...], ai_ref[...], L // SEGMENTS, True)

        @pl.when(j >= nb)
        def _():
            blk = 2 * nb - 1 - j
            base = blk * bi
            sr, si = sr_ref[...], si_ref[...]
            fr_ref[...] = sr.astype(f32).reshape(bi, SEGMENTS, SCAN_LANES)
            fi_ref[...] = si.astype(f32).reshape(bi, SEGMENTS, SCAN_LANES)

            def step(n, c):
                i = bi - 1 - n
                gr, gi, accr, acci = c
                qr, qi = _cmul(ar, ai, gr, gi)
                gr, gi = qr + gr_ref[base + i], qi + gi_ref[base + i]
                gr_ref[base + i] = gr
                gi_ref[base + i] = gi
                pr, pi = fr_ref[i - 1], fi_ref[i - 1]
                return gr, gi, accr + (gr * pr + gi * pi), acci + (gi * pr - gr * pi)

            gr, gi, accr, acci = lax.fori_loop(0, bi - 1, step, (hr_ref[...], hi_ref[...], dar_ref[...], dai_ref[...]), unroll=True)
            qr, qi = _cmul(ar, ai, gr, gi)
            gr, gi = qr + gr_ref[base], qi + gi_ref[base]
            gr_ref[base] = gr
            gi_ref[base] = gi
            hr_ref[...], hi_ref[...] = gr, gi
            row = lax.broadcasted_iota(jnp.int32, (SEGMENTS, SCAN_LANES), 0)
            older = lambda ref: ref[...].astype(f32)[SEGMENTS:, :]
            wrap_r = jnp.where(row == 0, 0.0, pltpu.roll(older(lr_ref), 1, 0))
            wrap_i = jnp.where(row == 0, 0.0, pltpu.roll(older(li_ref), 1, 0))
            pr = jnp.where(blk == 0, wrap_r, older(pr_ref))
            pi = jnp.where(blk == 0, wrap_i, older(pi_ref))
            dar_ref[...] = accr + gr * pr + gi * pi
            dai_ref[...] = acci + gi * pr - gr * pi

            g_re = gr_ref[pl.ds(base, bi)].reshape(rows, SCAN_LANES).astype(bf16)
            g_im = gi_ref[pl.ds(base, bi)].reshape(rows, SCAN_LANES).astype(bf16)
            ub = u_ref[...].astype(bf16)
            dy = dy_ref[...]
            du_ref[...] = dus_ref[...] + _dot(g_re, br_ref[...], NT) + _dot(g_im, bi_ref[...], NT)
            dbr_ref[...] += _dot(ub, g_re, TN)
            dbi_ref[...] += _dot(ub, g_im, TN)
            dcr_ref[...] += _dot(dy, sr, TN)
            dci_ref[...] -= _dot(dy, si, TN)

    block = lambda c, j: jnp.where(j < nb, nb - 1 - j, 2 * nb - 1 - j)
    late_block = lambda c, j: jnp.minimum(2 * nb - 1 - j, nb - 1)
    both = pl.BlockSpec((rows, S5_CHUNK_CH), lambda c, j: (block(c, j), c))
    chan = pl.BlockSpec((rows, S5_CHUNK_CH), lambda c, j: (late_block(c, j), c))
    state = pl.BlockSpec((rows, SCAN_LANES), lambda c, j: (late_block(c, j), c))
    prev = pl.BlockSpec((16, SCAN_LANES), lambda c, j: (jnp.maximum(late_block(c, j) * per - 1, 0), c))
    last = pl.BlockSpec((16, SCAN_LANES), lambda c, j: (L // 16 - 1, c))
    grad = pl.BlockSpec((S5_CHUNK_CH, SCAN_LANES), lambda c, j: (c, 0))
    return _pallas(
        body, ride=ride, name="s5_backward", grid=(S5_LANES // SCAN_LANES, 2 * nb),
        in_specs=[both, chan, chan, state, state, prev, prev, last, last, vec, vec, mat, mat, mat, mat],
        out_specs=[chan, tile, tile, grad, grad, grad, grad],
        out_shape=[S((L, S5_WIDTH), f32)] + [S((SEGMENTS, S5_LANES), f32)] * 2 + [S((S5_WIDTH, SCAN_LANES), f32)] * 4,
        scratch_shapes=[pltpu.VMEM((SEGMENTS, SCAN_LANES), f32)] * 2 + [whole] * 2 + [pltpu.VMEM((bi, SEGMENTS, SCAN_LANES), f32)] * 2,
        compiler_params=_cp("parallel", "arbitrary"),
    )(dy, u, du_skip, s_re, s_im, s_re, s_im, s_re, s_im, a_re, a_im, bb_re, bb_im, cc_re, cc_im)


def _s5_gate(y_lin, u, d_skip, w_glu, b_glu, tm, ride=()):
    L = u.shape[0]

    def body(yl_ref, u_ref, d_ref, w_ref, b_ref, o_ref):
        y = _gelu(yl_ref[...] + d_ref[...] * u_ref[...])
        z = _dot(y.astype(bf16), w_ref[...]) + b_ref[...]
        o_ref[...] = (y * _sigmoid(z)).astype(bf16)

    row = pl.BlockSpec((tm, S5_WIDTH), lambda i: (i, 0))
    vec = _full((1, S5_WIDTH))
    return _pallas(
        body, ride=ride, name="s5_gate", grid=(L // tm,),
        in_specs=[row, row, vec, _full((S5_WIDTH, S5_WIDTH)), vec],
        out_specs=row, out_shape=S((L, S5_WIDTH), bf16),
        compiler_params=_cp("parallel"),
    )(y_lin, u, d_skip, w_glu, b_glu)


def _group_mean(x, avg):
    return _dot(x.astype(bf16), avg)


def _conv_act(zn, ln_g, ln_b):
    t = zn * ln_g + ln_b
    return t * _sigmoid(t)


def _glu_padded(v_ref, halo_ref, zpad_ref, tm):
    v = v_ref[...]
    vh = halo_ref[...]
    zh = vh[:, :CONV_WIDTH] * _sigmoid(vh[:, CONV_WIDTH:])
    zpad_ref[:CONV_HALO, :] = jnp.where(pl.program_id(0) > 0, zh, 0.0)
    zpad_ref[CONV_HALO:CONV_HALO + tm, :] = v[:, :CONV_WIDTH] * _sigmoid(v[:, CONV_WIDTH:])
    zpad_ref[CONV_HALO + tm:, :] = jnp.zeros((8, CONV_WIDTH), f32)


def _shifted(pad_ref, sh_ref, tm):
    for b in range(8):
        sh_ref[b] = pad_ref[pl.ds(b, tm + CONV_HALO), :]


def _window(sh_ref, r0, off, rows):
    return sh_ref[off % 8, pl.ds(pl.multiple_of(r0 + 8 * (off // 8), 8), rows), :]


def _tap_sum(w_ref, sh_ref, taps, out_ref, tm, bias):
    def chunk(c, carry):
        r0 = pl.multiple_of(c * CONV_ROWS, CONV_ROWS)
        acc = jnp.zeros((CONV_ROWS, CONV_WIDTH), f32) + bias
        for k, off in taps:
            acc = acc + w_ref[k:k + 1, :] * _window(sh_ref, r0, off, CONV_ROWS)
        out_ref[pl.ds(r0, CONV_ROWS), :] = acc
        return carry

    lax.fori_loop(0, tm // CONV_ROWS, chunk, 0)


FWD_TAPS = [(k, CONV_HALO - (CONV_K - 1) + k) for k in range(CONV_K)]
BWD_TAPS = [(k, CONV_K - 1 - k) for k in range(CONV_K)]


def _conv_specs(tm):
    per = tm // CONV_HALO
    vrow = pl.BlockSpec((tm, 2 * CONV_WIDTH), lambda i: (i, 0))
    vhalo = pl.BlockSpec((CONV_HALO, 2 * CONV_WIDTH), lambda i: (jnp.maximum(i * per - 1, 0), 0))
    return vrow, vhalo


def _conv_scratch(tm):
    return [pltpu.VMEM((tm + CONV_HALO + 8, CONV_WIDTH), f32), pltpu.VMEM((8, tm + CONV_HALO, CONV_WIDTH), f32)]


def _conv_fwd(v, w_dw, b_dw, ln_g, ln_b, avg, tm, ride=()):
    L = v.shape[0]

    def body(v_ref, halo_ref, w_ref, b_ref, g_ref, bb_ref, avg_ref, o_ref, zc_ref, zpad_ref, zs_ref):
        _glu_padded(v_ref, halo_ref, zpad_ref, tm)
        _shifted(zpad_ref, zs_ref, tm)
        _tap_sum(w_ref, zs_ref, FWD_TAPS, zc_ref, tm, b_ref[...])
        zc = zc_ref[...]
        xc = zc - _group_mean(zc, avg_ref[...])
        zn = xc * lax.rsqrt(_group_mean(xc * xc, avg_ref[...]) + EPS)
        o_ref[...] = _conv_act(zn, g_ref[...], bb_ref[...]).astype(bf16)

    vrow, vhalo = _conv_specs(tm)
    vec = _full((1, CONV_WIDTH))
    row = pl.BlockSpec((tm, CONV_WIDTH), lambda i: (i, 0))
    return _pallas(
        body, ride=ride, name="conv_fwd", grid=(L // tm,),
        in_specs=[vrow, vhalo, _full((CONV_HALO, CONV_WIDTH)), vec, vec, vec, _full((CONV_WIDTH, CONV_WIDTH))],
        out_specs=[row, row], out_shape=[S((L, CONV_WIDTH), bf16), S((L, CONV_WIDTH), f32)],
        scratch_shapes=_conv_scratch(tm),
        compiler_params=_cp("arbitrary"),
    )(v, v, w_dw, b_dw, ln_g, ln_b, avg)


def _conv_bwd_norm(dout, zc, ln_g, ln_b, avg, tm):
    L = zc.shape[0]

    def body(do_ref, zc_ref, g_ref, bb_ref, avg_ref, dzc_ref, dg_ref, db_ref, dbd_ref):
        @pl.when(pl.program_id(0) == 0)
        def _():
            dg_ref[...] = jnp.zeros_like(dg_ref)
            db_ref[...] = jnp.zeros_like(db_ref)
            dbd_ref[...] = jnp.zeros_like(dbd_ref)

        avg = avg_ref[...]
        zc = zc_ref[...]
        xc = zc - _group_mean(zc, avg)
        rstd = lax.rsqrt(_group_mean(xc * xc, avg) + EPS)
        xhat = xc * rstd
        _, act_vjp = jax.vjp(_conv_act, xhat, g_ref[...], bb_ref[...])
        dxhat, dg, db = act_vjp(do_ref[...])
        dzc = rstd * (dxhat - _group_mean(dxhat, avg) - xhat * _group_mean(dxhat * xhat, avg))
        dzc_ref[...] = dzc
        dg_ref[0:1, :] += dg
        db_ref[0:1, :] += db
        dbd_ref[...] += _rows8(dzc)

    vec = _full((1, CONV_WIDTH))
    row = pl.BlockSpec((tm, CONV_WIDTH), lambda i: (i, 0))
    part = _full((8, CONV_WIDTH))
    return pl.pallas_call(
        body, name="conv_bwd_norm", grid=(L // tm,),
        in_specs=[row, row, vec, vec, _full((CONV_WIDTH, CONV_WIDTH))],
        out_specs=[row, part, part, part],
        out_shape=[S((L, CONV_WIDTH), f32)] + [S((8, CONV_WIDTH), f32)] * 3,
        compiler_params=_cp("arbitrary"),
    )(dout, zc, ln_g, ln_b, avg)


def _conv_bwd_taps(dzc, v, w_dw, tm, ride=()):
    L = v.shape[0]
    nt = L // tm
    per = tm // CONV_HALO

    def body(d_ref, dn_ref, v_ref, w_ref, dv_ref, dw_ref, dpad_ref, ds_ref, dz_ref, z_ref):
        i = pl.program_id(0)

        @pl.when(i == 0)
        def _():
            dw_ref[...] = jnp.zeros_like(dw_ref)

        v = v_ref[...]
        sig = _sigmoid(v[:, CONV_WIDTH:])
        z_ref[...] = v[:, :CONV_WIDTH] * sig
        dpad_ref[:tm, :] = d_ref[...]
        dpad_ref[tm:tm + CONV_HALO, :] = jnp.where(i < nt - 1, dn_ref[...], 0.0)
        dpad_ref[tm + CONV_HALO:, :] = jnp.zeros((8, CONV_WIDTH), f32)
        _shifted(dpad_ref, ds_ref, tm)
        _tap_sum(w_ref, ds_ref, BWD_TAPS, dz_ref, tm, 0.0)

        for first in range(0, CONV_K, 8):
            taps = BWD_TAPS[first:first + 8]

            def chunk(c, accs, taps=taps):
                r0 = pl.multiple_of(c * 8, 8)
                z = z_ref[pl.ds(r0, 8), :]
                return tuple(acc + z * _window(ds_ref, r0, off, 8) for acc, (_, off) in zip(accs, taps))

            accs = lax.fori_loop(0, tm // 8, chunk, tuple(jnp.zeros((8, CONV_WIDTH), f32) for _ in taps), unroll=2)
            for acc, (k, _) in zip(accs, taps):
                dw_ref[k] += acc

        dz = dz_ref[...]
        dv_ref[:, :CONV_WIDTH] = dz * sig
        dv_ref[:, CONV_WIDTH:] = dz * v[:, :CONV_WIDTH] * sig * (1.0 - sig)

    vrow, _ = _conv_specs(tm)
    row = pl.BlockSpec((tm, CONV_WIDTH), lambda i: (i, 0))
    nxt = pl.BlockSpec((CONV_HALO, CONV_WIDTH), lambda i: (jnp.minimum((i + 1) * per, nt * per - 1), 0))
    return _pallas(
        body, ride=ride, name="conv_bwd_taps", grid=(nt,),
        in_specs=[row, nxt, vrow, _full((CONV_HALO, CONV_WIDTH))],
        out_specs=[vrow, _full((CONV_HALO, 8, CONV_WIDTH))],
        out_shape=[S((L, 2 * CONV_WIDTH), f32), S((CONV_HALO, 8, CONV_WIDTH), f32)],
        scratch_shapes=_conv_scratch(tm) + [pltpu.VMEM((tm, CONV_WIDTH), f32)] * 2,
        compiler_params=_cp("arbitrary"),
    )(dzc, dzc, v, w_dw)


def _to_segments(a):
    L, c = a.shape
    return a.reshape(SEGMENTS, L // SEGMENTS, c).transpose(1, 0, 2).reshape(L, c)


def _from_segments(a):
    L, c = a.shape
    return a.reshape(L // SEGMENTS, SEGMENTS, c).transpose(1, 0, 2).reshape(L, c)


def _block_diag(ms):
    n = len(ms)

    def body(*refs):
        for a in range(n):
            out = refs[n + a]
            out[...] = jnp.zeros_like(out)
            for g in range(S5_GROUPS):
                rows = slice(g * S5_GROUP_CH, (g + 1) * S5_GROUP_CH)
                out[rows, g * S5_STATE:(g + 1) * S5_STATE] = refs[a][rows, :].astype(bf16)

    return pl.pallas_call(body, name="s5_block_diag", out_shape=[S((S5_WIDTH, S5_LANES), bf16)] * n,
                          compiler_params=pltpu.CompilerParams(vmem_limit_bytes=VMEM_LIMIT))(
        *[m.reshape(S5_WIDTH, S5_STATE) for m in ms])


def _diag_blocks(ms):
    n = len(ms)
    per_chunk = SCAN_LANES // S5_STATE

    def body(*refs):
        for a in range(n):
            for g in range(S5_GROUPS):
                rows = slice(g * S5_GROUP_CH, (g + 1) * S5_GROUP_CH)
                at = g % per_chunk * S5_STATE
                refs[n + a][rows, :] = refs[a][rows, at:at + S5_STATE]

    out = pl.pallas_call(body, name="s5_diag_blocks", out_shape=[S((S5_WIDTH, S5_STATE), f32)] * n,
                         compiler_params=pltpu.CompilerParams(vmem_limit_bytes=VMEM_LIMIT))(*ms)
    return [o.reshape(S5_GROUPS, S5_GROUP_CH, S5_STATE) for o in out]


class _NoExchanges:
    def before(self, point):
        return ()

    def after(self, point):
        pass

    def alone(self, point):
        pass


def _ffn_block(x, p, tag, tm, sched, head=None, mixed=None, mixer=None):
    point = tag + "_up"
    h, dadg, dadu, a, *x_in = _ffn_up(x, p[tag + "_norm"], p[tag + "_w_gate"], p[tag + "_w_up"], tm, tag, mixed,
                                      ride=sched.before(point))
    sched.after(point)
    x, = x_in or [x]
    if head is None:
        out = _ffn_down(x, a, p[tag + "_w_down"], tm, tag, mixer and tuple(p[n] for n in mixer), ride=sched.before(tag + "_down"))
        sched.after(tag + "_down")
    else:
        out = _ffn_down_loss(x, a, p[tag + "_w_down"], *head, tm, tag)
    return out, (x, h, dadg, dadu, a)


def _ffn_block_bwd(dxo, x, p, tag, saved, tm, grads, sched, parts=1, dxh=None):
    _, h, dadg, dadu, a = saved

    def weight_grad(which, lhs, rhs):
        point = tag + "_dw_" + which
        grads[tag + "_w_" + which] = _mm_tn(lhs, rhs, bf16, point, ride=sched.before(point))
        sched.after(point)

    if dxh is not None:
        weight_grad("down", a, dxh)
    dgate, dup, own_dxh = _ffn_bwd_act(dxo, p[tag + "_w_down"], dadg, dadu, tm, tag, ride=sched.before(tag + "_bwd_act"))
    sched.after(tag + "_bwd_act")
    weight_grad("gate", dgate, h)
    weight_grad("up", dup, h)
    if dxh is None:
        weight_grad("down", a, own_dxh)
    tiles = x.shape[0] // tm
    dx, dgs = None, []
    for k in range(parts):
        point = tag + "_bwd_in" + ("_%d" % k) * (parts > 1)
        dx, dg = _ffn_bwd_in(dxo, x, p[tag + "_norm"], dgate, dup, p[tag + "_w_gate"], p[tag + "_w_up"], tm, point,
                             tiles=(k * tiles // parts, tiles // parts), into=dx, ride=sched.before(point))
        sched.after(point)
        dgs.append(dg)
    grads[tag + "_norm"] = functools.reduce(jnp.add, dgs)
    return dx


def _local_step(x, target, p, grads, sched):
    L = x.shape[0]
    tm = min(512, L // 2)
    ni = L // SEGMENTS
    bi = min(64, ni)

    def carried(point, fn, *args):
        out = fn(*args, ride=sched.before(point))
        sched.after(point)
        return out

    (x1, h2, u_s5, v), saved1 = _ffn_block(x, p, "ffn1", tm, sched, mixer=("mix_norm", "w_in"))

    s5_in = (p["s5_lam_re"], p["s5_lam_im"], p["s5_log_dt"].reshape(S5_GROUPS, 1), p["s5_b_re"], p["s5_b_im"])
    abar_re, abar_im, bbar_re, bbar_im = _s5_params(*s5_in)
    a_re, a_im = abar_re.reshape(1, S5_LANES), abar_im.reshape(1, S5_LANES)
    bb_re, bb_im, cc_re, cc_im = _block_diag([bbar_re, bbar_im, p["s5_c_re"], p["s5_c_im"]])
    u_seg = _to_segments(u_s5)
    s_re, s_im, y_lin = carried("s5_forward", _s5_forward, u_seg, a_re, a_im, bb_re, bb_im, cc_re, cc_im, bi)
    y_s5 = _from_segments(_s5_gate(y_lin, u_seg, p["s5_d"], p["s5_w_glu"], p["s5_b_glu"], tm))
    w_dw = jnp.pad(p["conv_w_dw"], ((0, CONV_HALO - CONV_K), (0, 0)))
    heads = jnp.arange(CONV_WIDTH) // CONV_HEAD
    avg = ((heads[:, None] == heads[None, :]).astype(f32) / CONV_HEAD).astype(bf16)
    y_conv, zc = carried("conv_fwd", _conv_fwd, v, w_dw, p["conv_b_dw"], p["conv_ln_g"], p["conv_ln_b"], avg, tm)

    (dx3, grads["final_norm"], loss_terms), saved2 = _ffn_block(
        x1, p, "ffn2", tm, sched, head=(target, p["final_norm"].reshape(1, D_MODEL)), mixed=(y_s5, y_conv, p["w_out"]))
    x2 = saved2[0]
    grads["loss_terms"] = loss_terms

    dx2 = _ffn_block_bwd(dx3, x2, p, "ffn2", saved2, tm, grads, sched)

    dy_s5, dy_conv, dx2b = carried("mix_out_bwd", _mix_out_bwd, dx2, p["w_out"], tm)
    grads["w_out"] = _dw_out(y_s5, y_conv, dx2b)
    dy_lin, du_skip, dd8, grads["s5_w_glu"], dbg8 = _s5_read_bwd(
        _to_segments(dy_s5), y_lin, u_seg, p["s5_d"], p["s5_w_glu"], p["s5_b_glu"], tm)
    grads["s5_d"] = dd8.sum(axis=0, keepdims=True)
    grads["s5_b_glu"] = dbg8.sum(axis=0, keepdims=True)
    du_seg, da_re8, da_im8, dbb_re, dbb_im, dcc_re, dcc_im = carried(
        "s5_backward", _s5_backward, dy_lin, u_seg, du_skip, s_re, s_im, a_re, -a_im, bb_re, bb_im, cc_re, cc_im, bi)
    d_abar = lambda a8: a8.sum(axis=0).reshape(S5_GROUPS, S5_STATE)
    grads["s5_c_re"], grads["s5_c_im"], d_bbr, d_bbi = _diag_blocks([dcc_re, dcc_im, dbb_re, dbb_im])
    d_lr, d_li, d_ld, d_br, d_bi = _s5_params_bwd(*s5_in, d_abar(da_re8), d_abar(da_im8), d_bbr, d_bbi)
    grads["s5_lam_re"], grads["s5_lam_im"], grads["s5_log_dt"] = d_lr, d_li, d_ld.reshape(1, S5_GROUPS)
    grads["s5_b_re"], grads["s5_b_im"] = d_br, d_bi
    dzc, dlg8, dlb8, dbd8 = _conv_bwd_norm(dy_conv, zc, p["conv_ln_g"], p["conv_ln_b"], avg, tm)
    grads["conv_ln_g"] = dlg8.sum(axis=0, keepdims=True)
    grads["conv_ln_b"] = dlb8.sum(axis=0, keepdims=True)
    grads["conv_b_dw"] = dbd8.sum(axis=0, keepdims=True)
    dv, dw8 = carried("conv_bwd_taps", _conv_bwd_taps, dzc, v, w_dw, tm)
    grads["conv_w_dw"] = dw8.sum(axis=1)[:CONV_K]
    dx1, grads["mix_norm"], dub, dx1h = _mix_in_bwd(dx2, x1, p["mix_norm"], _from_segments(du_seg), dv, p["w_in"], tm)
    grads["w_in"] = _mm_tn(dub, h2, bf16, "dw_in")

    dx0 = _ffn_block_bwd(dx1, x, p, "ffn1", saved1, tm, grads, sched, parts=min(2, L // tm), dxh=dx1h)
    sched.alone("tail")
    return loss_terms, dx0


MESH = pl.DeviceIdType.MESH
ANY = pl.BlockSpec(memory_space=pl.ANY)


def _place():
    return lax.axis_index("x"), lax.axis_index("y"), lax.axis_index("c")


class _Exchange:
    def __init__(self, ins, out_shape, sems, start, finish):
        self.ins, self.out_shape, self.sems, self.start, self.finish = list(ins), list(out_shape), list(sems), start, finish
        self.out = None


def _pallas(body, *, ride=(), **kw):
    if not ride:
        return pl.pallas_call(body, **kw)

    def run(*args):
        out_shape = kw.get("out_shape", [])
        single = not isinstance(out_shape, (list, tuple))
        shapes = [out_shape] if single else list(out_shape)
        out_specs = [kw["out_specs"]] if single else list(kw.get("out_specs", []))
        grid = tuple(kw.get("grid", ()))
        scratch = list(kw.get("scratch_shapes", ()))
        n_in, n_out, n_scr = len(args), len(shapes), len(scratch)
        r_in = [len(e.ins) for e in ride]
        r_out = [len(e.out_shape) for e in ride]
        r_sem = [len(e.sems) for e in ride]

        def wrapped(*refs):
            own_in, refs = refs[:n_in], refs[n_in:]
            ex_in, refs = refs[:sum(r_in)], refs[sum(r_in):]
            own_out, refs = refs[:n_out], refs[n_out:]
            ex_out, refs = refs[:sum(r_out)], refs[sum(r_out):]
            own_scr, ex_sem = refs[:n_scr], refs[n_scr:]
            parts = []
            for e, ni, no, ns in zip(ride, r_in, r_out, r_sem):
                parts.append((e, ex_in[:ni], ex_out[:no], ex_sem[:ns]))
                ex_in, ex_out, ex_sem = ex_in[ni:], ex_out[no:], ex_sem[ns:]

            def at(step):
                def go():
                    for e, i, o, s in parts:
                        getattr(e, step)(i, o, s)
                if grid:
                    ids = [pl.program_id(d) for d in range(len(grid))]
                    when = [i == (0 if step == "start" else g - 1) for i, g in zip(ids, grid)]
                    pl.when(functools.reduce(lambda a, b: a & b, when))(go)
                else:
                    go()

            at("start")
            if body is not None:
                body(*own_in, *own_out, *own_scr)
            at("finish")

        outs = pl.pallas_call(
            wrapped, name=kw["name"], grid=grid,
            in_specs=list(kw.get("in_specs", [])) + [ANY] * sum(r_in),
            out_specs=out_specs + [ANY] * sum(r_out),
            out_shape=shapes + [s for e in ride for s in e.out_shape],
            scratch_shapes=scratch + [s for e in ride for s in e.sems],
            input_output_aliases=kw.get("input_output_aliases", {}),
            compiler_params=_cp(*["arbitrary"] * len(grid)),
        )(*args, *[a for e in ride for a in e.ins])
        own, rest = outs[:n_out], outs[n_out:]
        for e, no in zip(ride, r_out):
            e.out, rest = list(rest[:no]), rest[no:]
        return own[0] if single else own

    return run


def _exchange(ride, name):
    _pallas(None, ride=ride, name=name)()


def _gather(arrs):
    n = len(arrs)

    def copies(ins, outs, sems):
        send_sems, recv_sems, local_sems = sems
        x, y, c = _place()
        me, sibling = (x, y, c), (x, y, 1 - c)
        chips = [(1 - x, y), (x, 1 - y), (1 - x, 1 - y)]

        def place(a, block):
            return outs[a].at[block]

        def copy(a, k, block, to, src=None):
            px, py, pc = block
            dst = place(a, 4 * px + 2 * py + pc)
            return pltpu.make_async_remote_copy(
                src_ref=dst if src is None else src, dst_ref=dst, send_sem=send_sems.at[7 * a + k],
                recv_sem=recv_sems.at[7 * a + k], device_id=to, device_id_type=MESH)

        def own():
            local = [pltpu.make_async_copy(ins[a], place(a, 4 * x + 2 * y + c), local_sems.at[a]) for a in range(n)]
            remote = []
            for a in range(n):
                remote.append(copy(a, 0, me, sibling, src=ins[a]))
                remote += [copy(a, 1 + j, me, (*chip, c), src=ins[a]) for j, chip in enumerate(chips)]
            return local, remote

        return c, me, sibling, chips, copy, own

    def start(ins, outs, sems):
        local, remote = copies(ins, outs, sems)[-1]()
        for cp in local + remote:
            cp.start()

    def finish(ins, outs, sems):
        c, me, sibling, chips, copy, own = copies(ins, outs, sems)
        passed = []
        for j, chip in enumerate(chips):
            for a in range(n):
                copy(a, 1 + j, (*chip, c), me).wait_recv()
                passed.append(copy(a, 4 + j, (*chip, c), sibling))
                passed[-1].start()
        for a in range(n):
            copy(a, 0, sibling, me).wait_recv()
            for j, chip in enumerate(chips):
                copy(a, 4 + j, (*chip, 1 - c), me).wait_recv()
        local, remote = own()
        for cp in remote + passed:
            cp.wait_send()
        for cp in local:
            cp.wait()

    dma = pltpu.SemaphoreType.DMA
    shapes = [S((N_DEV, *a.shape), a.dtype) for a in arrs]
    return _Exchange(arrs, shapes, [dma((7 * n,)), dma((7 * n,)), dma((n,))], start, finish)


def _swap_with_sibling(gs):
    n = len(gs)

    def copies(ins, outs, sems):
        x, y, c = _place()
        return [pltpu.make_async_remote_copy(
            src_ref=ins[a].at[:, 1 - c], dst_ref=outs[a], send_sem=sems[0].at[a], recv_sem=sems[1].at[a],
            device_id=(x, y, 1 - c), device_id_type=MESH) for a in range(n)]

    def start(ins, outs, sems):
        for cp in copies(ins, outs, sems):
            cp.start()

    def finish(ins, outs, sems):
        for cp in copies(ins, outs, sems):
            cp.wait()

    dma = pltpu.SemaphoreType.DMA
    return _Exchange(gs, [S((N_CHIP, *g.shape[2:]), g.dtype) for g in gs], [dma((n,)), dma((n,))], start, finish)


def _swap_with_chips(ps):
    n = len(ps)

    def copies(ins, outs, sems):
        x, y, c = _place()
        q = 2 * x + y
        peers = [(x, 1 - y), (1 - x, y), (1 - x, 1 - y)]

        def copy(a, j, slot_from, slot_to):
            px, py = peers[j]
            return pltpu.make_async_remote_copy(
                src_ref=ins[a].at[slot_from], dst_ref=outs[a].at[slot_to], send_sem=sems[0].at[3 * a + j],
                recv_sem=sems[1].at[3 * a + j], device_id=(px, py, c), device_id_type=MESH)

        sends = lambda: [copy(a, j, 2 * peers[j][0] + peers[j][1], q) for a in range(n) for j in range(3)]
        lands = lambda: [copy(a, j, q, 2 * peers[j][0] + peers[j][1]) for a in range(n) for j in range(3)]
        return sends, lands

    def start(ins, outs, sems):
        for cp in copies(ins, outs, sems)[0]():
            cp.start()

    def finish(ins, outs, sems):
        sends, lands = copies(ins, outs, sems)
        for cp in lands():
            cp.wait_recv()
        for cp in sends():
            cp.wait_send()

    dma = pltpu.SemaphoreType.DMA
    return _Exchange(ps, [S(p.shape, p.dtype) for p in ps], [dma((3 * n,)), dma((3 * n,))], start, finish)


def _row_tile(rows, cols, itemsize):
    t = rows
    while t * cols * itemsize > (1 << 20) and t % 32 == 0:
        t //= 2
    return t


def _add_sibling(g4, st, core, name):
    _, R, C = st.shape
    tr = _row_tile(R, C, 1)

    def body(c_ref, g_ref, s_ref, o_ref):
        o_ref[...] = (g_ref[...].astype(f32) + s_ref[...].astype(f32)).astype(bf16)

    mine = pl.BlockSpec((None, None, tr, C), lambda q, i, c: (q, c[0], i, 0))
    return pl.pallas_call(
        body, name=name,
        grid_spec=pltpu.PrefetchScalarGridSpec(
            num_scalar_prefetch=1, grid=(N_CHIP, R // tr),
            in_specs=[mine,
                      pl.BlockSpec((None, tr, C), lambda q, i, c: (q, i, 0))],
            out_specs=pl.BlockSpec((None, tr, C), lambda q, i, c: (q, i, 0))),
        out_shape=S((N_CHIP, R, C), bf16),
        compiler_params=_cp("parallel", "parallel"),
    )(core, g4, st)


SMEM = pl.BlockSpec(memory_space=pltpu.SMEM)
VMEM = pl.BlockSpec(memory_space=pltpu.VMEM)


def _add_sibling_small(items, core, name):
    n = len(items)

    def body(c_ref, *refs):
        c = c_ref[0]
        for k in range(n):
            g_ref, s_ref, o_ref = refs[2 * k], refs[2 * k + 1], refs[2 * n + k]
            for q in range(N_CHIP):
                o_ref[q] = (g_ref[q, c].astype(f32) + s_ref[q].astype(f32)).astype(bf16)

    return pl.pallas_call(body, name=name, in_specs=[SMEM] + [VMEM] * (2 * n), out_specs=[VMEM] * n,
                          out_shape=[S(st.shape, bf16) for _, st in items],
                          compiler_params=pltpu.CompilerParams(vmem_limit_bytes=VMEM_LIMIT))(
        core, *[a for item in items for a in item])


def _adam_small(items, slots, name):
    n = len(items)

    def body(s_ref, *refs):
        ins, outs = refs[:5 * n], refs[5 * n:]
        for k in range(n):
            w_ref, m_ref, v_ref, p_ref, got_ref = ins[5 * k:5 * k + 5]
            g = p_ref[s_ref[0]].astype(f32)
            for j in range(1, N_CHIP):
                g = g + got_ref[s_ref[j]].astype(f32)
            outs[4 * k][...] = g
            outs[4 * k + 1][...], outs[4 * k + 2][...], outs[4 * k + 3][...] = _adamw(w_ref[...], g, m_ref[...], v_ref[...])

    out = pl.pallas_call(body, name=name, in_specs=[SMEM] + [VMEM] * (5 * n), out_specs=[VMEM] * (4 * n),
                         out_shape=[S(item[0].shape, f32) for item in items for _ in range(4)],
                         compiler_params=pltpu.CompilerParams(vmem_limit_bytes=VMEM_LIMIT))(
        slots, *[a for item in items for a in item])
    return [out[4 * k:4 * k + 4] for k in range(n)]


def _adamw(w, g, m, v):
    m = B1 * m + (1.0 - B1) * g
    v = B2 * v + (1.0 - B2) * (g * g)
    m_hat = m / (1.0 - B1 ** STEP)
    v_hat = v / (1.0 - B2 ** STEP)
    return -LR * (m_hat / (jnp.sqrt(v_hat) + ADAM_EPS) + WD * w), m, v


def _adam_sharded(items, slots, name):
    n = len(items)
    R, C = items[0][0].shape
    tr = _row_tile(R, C, 4 * n)

    def body(s_ref, *refs):
        ins, outs = refs[:7 * n], refs[7 * n:]
        for k in range(n):
            w_ref, m_ref, v_ref, p_ref, a_ref, b_ref, c_ref = ins[7 * k:7 * k + 7]
            g = p_ref[...].astype(f32) + a_ref[...].astype(f32) + b_ref[...].astype(f32) + c_ref[...].astype(f32)
            outs[4 * k][...] = g
            outs[4 * k + 1][...], outs[4 * k + 2][...], outs[4 * k + 3][...] = _adamw(w_ref[...], g, m_ref[...], v_ref[...])

    shard = pl.BlockSpec((tr, C), lambda i, s: (i, 0))
    slot = lambda k: pl.BlockSpec((None, tr, C), lambda i, s: (s[k], i, 0))
    out = pl.pallas_call(
        body, name=name,
        grid_spec=pltpu.PrefetchScalarGridSpec(
            num_scalar_prefetch=1, grid=(R // tr,),
            in_specs=[shard, shard, shard, slot(0), slot(1), slot(2), slot(3)] * n,
            out_specs=[shard] * (4 * n)),
        out_shape=[S((R, C), f32)] * (4 * n),
        compiler_params=_cp("parallel"),
    )(slots, *[a for w, m, v, part, got in items for a in (w, m, v, part, got, got, got)])
    return [out[4 * k:4 * k + 4] for k in range(n)]


def _adam_replicated(items, loss_terms, name):
    n = len(items)
    has_loss = loss_terms is not None

    def total(ref):
        g = ref[0]
        for d in range(1, N_DEV):
            g = g + ref[d]
        return g

    def body(*refs):
        ins, outs = refs[:4 * n + has_loss], refs[4 * n + has_loss:]
        for i in range(n):
            w_ref, m_ref, v_ref, g_ref = ins[4 * i:4 * i + 4]
            g = total(g_ref)
            outs[4 * i][...] = g
            outs[4 * i + 1][...], outs[4 * i + 2][...], outs[4 * i + 3][...] = _adamw(w_ref[...], g, m_ref[...], v_ref[...])
        if has_loss:
            outs[-1][...] = jnp.sum(total(ins[-1]), keepdims=True)

    flat = [a for item in items for a in item] + ([loss_terms] if has_loss else [])
    shapes = [S(item[0].shape, f32) for item in items for _ in range(4)] + ([S((1, 1), f32)] if has_loss else [])
    out = pl.pallas_call(body, name=name, out_shape=shapes,
                         compiler_params=pltpu.CompilerParams(vmem_limit_bytes=VMEM_LIMIT))(*flat)
    return [out[4 * i:4 * i + 4] for i in range(n)], (out[-1] if has_loss else None)


WEIGHTS = ["ffn1_norm", "ffn1_w_gate", "ffn1_w_up", "ffn1_w_down", "mix_norm", "w_in", "s5_lam_re", "s5_lam_im", "s5_log_dt",
           "s5_b_re", "s5_b_im", "s5_c_re", "s5_c_im", "s5_d", "s5_w_glu", "s5_b_glu", "conv_w_dw", "conv_b_dw", "conv_ln_g",
           "conv_ln_b", "w_out", "ffn2_norm", "ffn2_w_gate", "ffn2_w_up", "ffn2_w_down", "final_norm"]
SHARDED = ["ffn1_w_gate", "ffn1_w_up", "ffn1_w_down", "w_in", "s5_w_glu", "conv_w_dw", "w_out", "ffn2_w_gate", "ffn2_w_up",
           "ffn2_w_down"]
REPLICATED = [n for n in WEIGHTS if n not in SHARDED]
TRANSPOSED = ["ffn1_w_gate", "ffn1_w_up", "ffn2_w_gate", "ffn2_w_up", "w_in"]


def _shard_to_wire(n, w):
    if n == "conv_w_dw":
        return jnp.pad(w, ((0, CONV_HALO - CONV_K), (0, 0)))
    return w.astype(bf16)


def _to_wire(shards, ride):
    names = list(shards)
    shapes = [jax.eval_shape(functools.partial(_shard_to_wire, n), shards[n]) for n in names]

    def body(*refs):
        for src, dst in zip(refs[:len(names)], refs[len(names):]):
            (r, c), (rp, cp) = src.shape, dst.shape
            dst[:r, :c] = src[...].astype(dst.dtype)
            if cp > c:
                dst[:, c:] = jnp.zeros((rp, cp - c), dst.dtype)
            if rp > r:
                dst[r:, :] = jnp.zeros((rp - r, cp), dst.dtype)

    out = _pallas(body, ride=ride, name="to_wire", out_shape=shapes, in_specs=[pl.BlockSpec(memory_space=pltpu.VMEM)] * len(names),
                  out_specs=[pl.BlockSpec(memory_space=pltpu.VMEM)] * len(names))(*[shards[n] for n in names])
    return dict(zip(names, out))


def _gathered_to_full(n, g):
    if n == "conv_w_dw":
        return g.transpose(1, 0, 2).reshape(CONV_HALO, CONV_WIDTH)[:CONV_K]
    return g.reshape(N_DEV * g.shape[1], g.shape[2])


def _grad_to_blocks(n, g):
    if n == "conv_w_dw":
        g = jnp.pad(g, ((0, CONV_HALO - CONV_K), (0, 0)))
        g = g.reshape(g.shape[0], N_DEV, g.shape[1] // N_DEV).transpose(1, 0, 2)
    else:
        g = g.reshape(N_DEV, g.shape[0] // N_DEV, g.shape[1])
    return g.astype(bf16).reshape(N_CHIP, 2, *g.shape[1:])


REPLICATED_LATE = ["ffn1_norm"]
REPLICATED_HEAD = ["ffn2_norm", "final_norm"]
REPLICATED_MIX = ["mix_norm", "conv_b_dw", "conv_ln_g", "conv_ln_b"]
REPLICATED_S5 = [n for n in REPLICATED if n not in REPLICATED_LATE + REPLICATED_HEAD + REPLICATED_MIX]
REPLICATED_EARLY = REPLICATED_HEAD + REPLICATED_S5 + REPLICATED_MIX

PLAN = {
    "start": [("gather", ["ffn1_w_gate", "ffn1_w_up"])],
    "ffn1_up": [("gather", ["ffn1_w_down", "w_in", "w_out", "s5_w_glu", "conv_w_dw"])],
    "s5_forward": [("gather", ["ffn2_w_gate", "ffn2_w_up"])],
    "conv_fwd": [("gather", ["ffn2_w_down"])],
    "ffn2_dw_up": [("sibling", ["ffn2_w_gate"])],
    "ffn2_dw_down": [("sibling", ["ffn2_w_up"])],
    "mix_out_bwd": [("sibling", ["ffn2_w_down"]), ("replicated", REPLICATED_HEAD)],
    "s5_backward": [("chips", ["ffn2_w_gate", "ffn2_w_up"])],
    "conv_bwd_taps": [("chips", ["ffn2_w_down"]), ("replicated", REPLICATED_S5)],
    "ffn1_dw_down": [("sibling", ["w_in", "s5_w_glu", "conv_w_dw", "w_out"]), ("replicated", REPLICATED_MIX)],
    "ffn1_bwd_act": [("chips", ["w_in", "s5_w_glu", "conv_w_dw", "w_out"]), ("sibling", ["ffn1_w_down"])],
    "ffn1_dw_gate": [("chips", ["ffn1_w_down"])],
    "ffn1_dw_up": [("sibling", ["ffn1_w_gate"])],
    "ffn1_bwd_in_0": [("chips", ["ffn1_w_gate"]), ("sibling", ["ffn1_w_up"])],
    "ffn1_bwd_in_1": [("chips", ["ffn1_w_up"])],
    "tail": [("replicated", REPLICATED_LATE)],
}


class _Schedule:
    def __init__(self, wire, p, grads, core):
        self.wire, self.p, self.grads, self.core = wire, p, grads, core
        self.partial, self.reduced, self.everyone, self.pending = {}, {}, {}, []

    def before(self, point):
        assert not self.pending
        for kind, names in PLAN.get(point, ()):
            if kind == "gather":
                given = [self.wire[n] for n in names]
                ex = _gather(given)
            elif kind == "sibling":
                given = [_grad_to_blocks(n, self.grads[n]) for n in names]
                ex = _swap_with_sibling(given)
            elif kind == "chips":
                given = [self.partial.pop(n) for n in names]
                ex = _swap_with_chips(given)
            else:
                names = names + ["loss_terms"] * (names is REPLICATED_HEAD)
                given = [self.grads[n].reshape(self.p[n].shape) if n in self.p else self.grads[n] for n in names]
                ex = _gather(given)
            self.pending.append((kind, names, given, ex))
        return [ex for _, _, _, ex in self.pending]

    def after(self, point):
        for kind, names, given, ex in self.pending:
            if kind == "gather":
                for n, g in zip(names, ex.out):
                    self.p[n] = _gathered_to_full(n, g)
            elif kind == "sibling":
                if len(names) > 1:
                    sums = _add_sibling_small(list(zip(given, ex.out)), self.core, "reduce_add_" + names[0])
                else:
                    sums = [_add_sibling(given[0], ex.out[0], self.core, "reduce_add_" + names[0])]
                self.partial.update(zip(names, sums))
            elif kind == "chips":
                for n, part, got in zip(names, given, ex.out):
                    self.reduced[n] = (part, got)
            else:
                self.everyone.update(zip(names, ex.out))
        self.pending = []

    def alone(self, point):
        _exchange(self.before(point), point)
        self.after(point)


def kernel(x, ffn1_norm, ffn1_w_gate, ffn1_w_up, ffn1_w_down, mix_norm, w_in, s5_lam_re, s5_lam_im, s5_log_dt, s5_b_re, s5_b_im, s5_c_re, s5_c_im, s5_d, s5_w_glu, s5_b_glu, conv_w_dw, conv_b_dw, conv_ln_g, conv_ln_b, w_out, ffn2_norm, ffn2_w_gate, ffn2_w_up, ffn2_w_down, final_norm, loss_target, m_ffn1_norm, m_ffn1_w_gate, m_ffn1_w_up, m_ffn1_w_down, m_mix_norm, m_w_in, m_s5_lam_re, m_s5_lam_im, m_s5_log_dt, m_s5_b_re, m_s5_b_im, m_s5_c_re, m_s5_c_im, m_s5_d, m_s5_w_glu, m_s5_b_glu, m_conv_w_dw, m_conv_b_dw, m_conv_ln_g, m_conv_ln_b, m_w_out, m_ffn2_norm, m_ffn2_w_gate, m_ffn2_w_up, m_ffn2_w_down, m_final_norm, v_ffn1_norm, v_ffn1_w_gate, v_ffn1_w_up, v_ffn1_w_down, v_mix_norm, v_w_in, v_s5_lam_re, v_s5_lam_im, v_s5_log_dt, v_s5_b_re, v_s5_b_im, v_s5_c_re, v_s5_c_im, v_s5_d, v_s5_w_glu, v_s5_b_glu, v_conv_w_dw, v_conv_b_dw, v_conv_ln_g, v_conv_ln_b, v_w_out, v_ffn2_norm, v_ffn2_w_gate, v_ffn2_w_up, v_ffn2_w_down, v_final_norm):
    args = locals()
    w = {n: args[n] for n in WEIGHTS}
    m = {n: args["m_" + n] for n in WEIGHTS}
    v = {n: args["v_" + n] for n in WEIGHTS}
    xq, yq, cq = _place()
    q = 2 * xq + yq
    slots = jnp.stack([q, q ^ 1, q ^ 2, q ^ 3]).astype(jnp.int32)

    def shard2d(n, a):
        a = a.reshape(a.shape[-2:])
        return a.T if n in TRANSPOSED else a

    def view(n, a):
        if n.startswith("s5_b_") and a.ndim == 4:
            return a[0].transpose(0, 2, 1)
        return a[0] if a.ndim >= 3 else a.reshape(1, -1)

    def unview(n, a):
        return (a.transpose(0, 2, 1) if n.startswith("s5_b_") and a.ndim == 3 else a).reshape(w[n].shape)

    p = {n: view(n, w[n]) for n in REPLICATED}
    grads = {}
    first = PLAN["start"][0][1]
    wire = {n: _shard_to_wire(n, shard2d(n, w[n])) for n in first}
    sched = _Schedule(wire, p, grads, jnp.reshape(cq, (1,)).astype(jnp.int32))
    wire.update(_to_wire({n: shard2d(n, w[n]) for n in SHARDED if n not in first}, sched.before("start")))
    sched.after("start")
    _, dx = _local_step(x[0], loss_target[0], p, grads, sched)

    out = {}
    groups = [[n for n in SHARDED if n.startswith(tag)] for tag in ("ffn1", "ffn2")]
    for names in groups + [[n for n in SHARDED if not n.startswith("ffn")]]:
        def fit(n, a):
            a = shard2d(n, a)
            return jnp.pad(a, ((0, sched.reduced[n][1].shape[1] - a.shape[0]), (0, 0)))

        items = [(fit(n, w[n]), fit(n, m[n]), fit(n, v[n]), *sched.reduced[n]) for n in names]
        update = _adam_sharded if names[0].startswith("ffn") else _adam_small
        for n, res in zip(names, update(items, slots, "adam_" + names[0])):
            back = lambda r: r[:shard2d(n, w[n]).shape[0]]
            out[n] = [(back(r).T if n in TRANSPOSED else back(r)).reshape(w[n].shape) for r in res]

    for names in (REPLICATED_EARLY, REPLICATED_LATE):
        items = [(view(n, w[n]), view(n, m[n]), view(n, v[n]), sched.everyone[n]) for n in names]
        res, total = _adam_replicated(items, sched.everyone.get("loss_terms") if names is REPLICATED_EARLY else None,
                                      "adam_" + names[0])
        for n, r in zip(names, res):
            out[n] = [unview(n, a) for a in r]
        if total is not None:
            loss = total.reshape(())

    return (loss, dx.reshape(x.shape), *[out[n][0] for n in WEIGHTS], *[out[n][1] for n in WEIGHTS],
            *[out[n][2] for n in WEIGHTS], *[out[n][3] for n in WEIGHTS])
```

```python
import functools

import jax
import jax.numpy as jnp
from jax import lax
from jax.experimental import pallas as pl
from jax.experimental.pallas import tpu as pltpu

f32 = jnp.float32
bf16 = jnp.bfloat16
S = jax.ShapeDtypeStruct

N_DEV = 8
N_CHIP = 4
D_MODEL = 1024
D_FF = 2816
FF_CHUNKS = [(0, 768), (768, 1536), (1536, 2304), (2304, D_FF)]
S5_WIDTH = 512
S5_GROUPS = 32
S5_GROUP_CH = 16
S5_STATE = 64
S5_LANES = S5_GROUPS * S5_STATE
CONV_WIDTH = 512
CONV_K = 31
CONV_HALO = 32
CONV_HEAD = 64
CONV_ROWS = 32
IN_COLS = S5_WIDTH + 2 * CONV_WIDTH
SEGMENTS = 8
SCAN_LANES = 512
EPS = 1e-6
LR, B1, B2, ADAM_EPS, WD, STEP = 0.001, 0.9, 0.999, 1e-08, 0.01, 10
VMEM_LIMIT = 56 * 1024 * 1024

NN = (((1,), (0,)), ((), ()))
NT = (((1,), (1,)), ((), ()))
TN = (((0,), (0,)), ((), ()))


def _dot(a, b, dims=NN):
    return lax.dot_general(a, b, dims, preferred_element_type=f32)


def _cp(*sem):
    return pltpu.CompilerParams(dimension_semantics=sem, vmem_limit_bytes=VMEM_LIMIT)


def _rms(x, g):
    return x * lax.rsqrt(jnp.mean(x * x, axis=-1, keepdims=True) + EPS) * g


def _rms_bwd(x, g, dh):
    _, vjp = jax.vjp(_rms, x, g)
    return vjp(dh)


def _sigmoid(x):
    return 1.0 / (1.0 + jnp.exp(-x))


def _gelu(x):
    return 0.5 * x * (1.0 + jnp.tanh(0.7978845608028654 * (x + 0.044715 * x * x * x)))


def _rows8(x):
    t, c = x.shape
    return x.reshape(t // 8, 8, c).sum(axis=0)


def _full(shape):
    return pl.BlockSpec(shape, lambda *_: (0,) * len(shape))


def _resident(shape):
    return pl.BlockSpec(shape, lambda *_: (0,) * len(shape), pipeline_mode=pl.Buffered(1))


def _ffn_up(x, g, wg, wu, tm, tag, mixed=None, ride=()):
    L = x.shape[0]

    def body(x_ref, g_ref, wg_ref, wu_ref, *rest):
        h_ref, dadg_ref, dadu_ref, a_ref = rest[-5:-1] if mixed else rest[-4:]
        x = x_ref[...]
        if mixed:
            ys_ref, yc_ref, wo_ref = rest[:3]
            x = x + _dot(ys_ref[...], wo_ref[:S5_WIDTH, :]) + _dot(yc_ref[...], wo_ref[S5_WIDTH:, :])
            rest[-1][...] = x
        h = _rms(x, g_ref[...]).astype(bf16)
        h_ref[...] = h
        for lo, hi in FF_CHUNKS:
            cols = slice(lo, hi)
            gate =_dot(h, wg_ref[cols, :], NT)
            up = _dot(h, wu_ref[cols, :], NT)
            sig = _sigmoid(gate)
            silu = gate * sig
            dadg_ref[:, cols] = (up * (sig + silu * (1.0 - sig))).astype(bf16)
            dadu_ref[:, cols] = silu.astype(bf16)
            a_ref[:, cols] = (silu * up).astype(bf16)

    row = pl.BlockSpec((tm, D_MODEL), lambda i: (i, 0))
    wide = pl.BlockSpec((tm, D_FF), lambda i: (i, 0))
    half = pl.BlockSpec((tm, S5_WIDTH), lambda i: (i, 0))
    return _pallas(
        body, ride=ride, name=tag + "_up", grid=(L // tm,),
        in_specs=[row, _full((1, D_MODEL)), _resident((D_FF, D_MODEL)), _resident((D_FF, D_MODEL))]
        + ([half, half, _resident((D_MODEL, D_MODEL))] if mixed else []),
        out_specs=[row, wide, wide, wide] + [row] * bool(mixed),
        out_shape=[S((L, D_MODEL), bf16)] + [S((L, D_FF), bf16)] * 3 + [S((L, D_MODEL), f32)] * bool(mixed),
        compiler_params=_cp("parallel"),
    )(x, g, wg, wu, *(mixed or ()))


def _ffn_down(x, a, wd, tm, tag, mixer=None, ride=()):
    L = x.shape[0]

    def body(x_ref, a_ref, wd_ref, *rest):
        xo = x_ref[...] + 0.5 * _dot(a_ref[...], wd_ref[...])
        if not mixer:
            rest[0][...] = xo
            return
        g_ref, w_ref, o_ref, h_ref, us_ref, v_ref = rest
        o_ref[...] = xo
        h = _rms(xo, g_ref[...]).astype(bf16)
        h_ref[...] = h
        u = _dot(h, w_ref[...], NT)
        us_ref[...] = u[:, :S5_WIDTH]
        v_ref[...] = u[:, S5_WIDTH:]

    row = lambda c: pl.BlockSpec((tm, c), lambda i: (i, 0))
    extra_in = [_full((1, D_MODEL)), _resident((IN_COLS, D_MODEL))] if mixer else []
    extra_out = [(D_MODEL, bf16), (S5_WIDTH, f32), (2 * CONV_WIDTH, f32)] if mixer else []
    out = _pallas(
        body, ride=ride, name=tag + "_down", grid=(L // tm,),
        in_specs=[row(D_MODEL), row(D_FF), _resident((D_FF, D_MODEL))] + extra_in,
        out_specs=[row(D_MODEL)] + [row(c) for c, _ in extra_out],
        out_shape=[S((L, D_MODEL), f32)] + [S((L, c), t) for c, t in extra_out],
        compiler_params=_cp("parallel"),
    )(x, a, wd, *(mixer or ()))
    return out if mixer else out[0]


def _ffn_down_loss(x, a, wd, target, g, tm, tag):
    L = x.shape[0]

    def body(x_ref, a_ref, wd_ref, t_ref, g_ref, dx_ref, dg_ref, l_ref):
        @pl.when(pl.program_id(0) == 0)
        def _():
            dg_ref[...] = jnp.zeros_like(dg_ref)
            l_ref[...] = jnp.zeros_like(l_ref)

        xo = x_ref[...] + 0.5 * _dot(a_ref[...], wd_ref[...])
        g = g_ref[...]
        e = _rms(xo, g) - t_ref[...]
        l_ref[...] += _rows8(e * e) * (0.5 / D_MODEL)
        dx, dg = _rms_bwd(xo, g, e * (1.0 / D_MODEL))
        dx_ref[...] = dx
        dg_ref[...] += dg

    row = pl.BlockSpec((tm, D_MODEL), lambda i: (i, 0))
    return pl.pallas_call(
        body, name=tag + "_down_loss", grid=(L // tm,),
        in_specs=[row, pl.BlockSpec((tm, D_FF), lambda i: (i, 0)), _resident((D_FF, D_MODEL)), row, _full((1, D_MODEL))],
        out_specs=[row, _full((1, D_MODEL)), _full((8, D_MODEL))],
        out_shape=[S((L, D_MODEL), f32), S((1, D_MODEL), f32), S((8, D_MODEL), f32)],
        compiler_params=_cp("arbitrary"),
    )(x, a, wd, target, g)


def _ffn_bwd_act(dxo, wd, dadg, dadu, tm, tag, ride=()):
    L = dxo.shape[0]

    def body(dx_ref, wd_ref, dadg_ref, dadu_ref, dgate_ref, dup_ref, dxh_ref):
        dxh = (0.5 * dx_ref[...]).astype(bf16)
        dxh_ref[...] = dxh
        for lo, hi in FF_CHUNKS:
            cols = slice(lo, hi)
            da =_dot(dxh, wd_ref[cols, :], NT)
            dgate_ref[:, cols] = (da * dadg_ref[:, cols].astype(f32)).astype(bf16)
            dup_ref[:, cols] = (da * dadu_ref[:, cols].astype(f32)).astype(bf16)

    row = pl.BlockSpec((tm, D_MODEL), lambda i: (i, 0))
    wide = pl.BlockSpec((tm, D_FF), lambda i: (i, 0))
    return _pallas(
        body, ride=ride, name=tag + "_bwd_act", grid=(L // tm,),
        in_specs=[row, _resident((D_FF, D_MODEL)), wide, wide],
        out_specs=[wide, wide, row],
        out_shape=[S((L, D_FF), bf16), S((L, D_FF), bf16), S((L, D_MODEL), bf16)],
        compiler_params=_cp("parallel"),
    )(dxo, wd, dadg, dadu)


def _ffn_bwd_in(dxo, x, g, dgate, dup, wg, wu, tm, name, tiles=None, into=None, ride=()):
    L = x.shape[0]
    first, count = tiles or (0, L // tm)

    def body(dxo_ref, x_ref, g_ref, dgate_ref, dup_ref, wg_ref, wu_ref, *rest):
        dx_ref, dg_ref = rest[-2:]

        @pl.when(pl.program_id(0) == 0)
        def _():
            dg_ref[...] = jnp.zeros_like(dg_ref)

        dh = _dot(dgate_ref[...], wg_ref[...]) + _dot(dup_ref[...], wu_ref[...])
        dx, dg = _rms_bwd(x_ref[...], g_ref[...], dh)
        dx_ref[...] = dxo_ref[...] + dx
        dg_ref[...] += dg

    row = pl.BlockSpec((tm, D_MODEL), lambda i: (first + i, 0))
    wide = pl.BlockSpec((tm, D_FF), lambda i: (first + i, 0))
    return _pallas(
        body, ride=ride, name=name, grid=(count,),
        in_specs=[row, row, _full((1, D_MODEL)), wide, wide, _resident((D_FF, D_MODEL)), _resident((D_FF, D_MODEL))]
        + [ANY] * (into is not None),
        out_specs=[row, _full((1, D_MODEL))],
        out_shape=[S((L, D_MODEL), f32), S((1, D_MODEL), f32)],
        input_output_aliases={7: 0} if into is not None else {},
        compiler_params=_cp("arbitrary"),
    )(dxo, x, g, dgate, dup, wg, wu, *([into] if into is not None else []))


def _mm_tn(a, b, out_dtype, name, tm=512, tn=1024, ride=()):
    L, M = a.shape
    N = b.shape[1]
    tm, tn = min(tm, M), min(tn, N)
    while M % tm:
        tm //= 2
    while N % tn:
        tn //= 2

    def body(a_ref, b_ref, o_ref):
        o_ref[...] = _dot(a_ref[...].astype(bf16), b_ref[...].astype(bf16), TN).astype(out_dtype)

    return _pallas(
        body, ride=ride, name=name, grid=(M // tm, N // tn),
        in_specs=[pl.BlockSpec((L, tm), lambda i, j: (0, i)), pl.BlockSpec((L, tn), lambda i, j: (0, j))],
        out_specs=pl.BlockSpec((tm, tn), lambda i, j: (i, j)),
        out_shape=S((M, N), out_dtype),
        compiler_params=_cp("parallel", "parallel"),
    )(a, b)


def _mix_in_bwd(dxo, x, g, du_s5, dv, w_in, tm):
    L = x.shape[0]

    def body(dxo_ref, x_ref, g_ref, dus_ref, dv_ref, w_ref, dx_ref, dg_ref, dub_ref, dxh_ref):
        @pl.when(pl.program_id(0) == 0)
        def _():
            dg_ref[...] = jnp.zeros_like(dg_ref)

        dus = dus_ref[...].astype(bf16)
        dvb = dv_ref[...].astype(bf16)
        dub_ref[:, :S5_WIDTH] = dus
        dub_ref[:, S5_WIDTH:] = dvb
        dh = _dot(dus, w_ref[:S5_WIDTH, :]) + _dot(dvb, w_ref[S5_WIDTH:, :])
        dx, dg = _rms_bwd(x_ref[...], g_ref[...], dh)
        dx = dxo_ref[...] + dx
        dx_ref[...] = dx
        dxh_ref[...] = (0.5 * dx).astype(bf16)
        dg_ref[...] += dg

    row = lambda c: pl.BlockSpec((tm, c), lambda i: (i, 0))
    return pl.pallas_call(
        body, name="mix_in_bwd", grid=(L // tm,),
        in_specs=[row(D_MODEL), row(D_MODEL), _full((1, D_MODEL)), row(S5_WIDTH), row(2 * CONV_WIDTH),
                  _full((IN_COLS, D_MODEL))],
        out_specs=[row(D_MODEL), _full((1, D_MODEL)), row(IN_COLS), row(D_MODEL)],
        out_shape=[S((L, D_MODEL), f32), S((1, D_MODEL), f32), S((L, IN_COLS), bf16), S((L, D_MODEL), bf16)],
        compiler_params=_cp("arbitrary"),
    )(dxo, x, g, du_s5, dv, w_in)


def _dw_out(y_s5, y_conv, dxb, tn=512):
    L = dxb.shape[0]

    def body(ys_ref, yc_ref, b_ref, o_ref):
        b = b_ref[...]
        o_ref[:S5_WIDTH, :] = _dot(ys_ref[...], b, TN).astype(bf16)
        o_ref[S5_WIDTH:, :] = _dot(yc_ref[...], b, TN).astype(bf16)

    return pl.pallas_call(
        body, name="dw_out", grid=(D_MODEL // tn,),
        in_specs=[_full((L, S5_WIDTH)), _full((L, CONV_WIDTH)), pl.BlockSpec((L, tn), lambda j: (0, j))],
        out_specs=pl.BlockSpec((S5_WIDTH + CONV_WIDTH, tn), lambda j: (0, j)),
        out_shape=S((S5_WIDTH + CONV_WIDTH, D_MODEL), bf16),
        compiler_params=_cp("parallel"),
    )(y_s5, y_conv, dxb)


def _mix_out_bwd(dx, w_out, tm, ride=()):
    L = dx.shape[0]

    def body(dx_ref, w_ref, dys_ref, dyc_ref, dxb_ref):
        dxb = dx_ref[...].astype(bf16)
        dxb_ref[...] = dxb
        dys_ref[...] = _dot(dxb, w_ref[:S5_WIDTH, :], NT)
        dyc_ref[...] = _dot(dxb, w_ref[S5_WIDTH:, :], NT)

    row = lambda c: pl.BlockSpec((tm, c), lambda i: (i, 0))
    return _pallas(
        body, ride=ride, name="mix_out_bwd", grid=(L // tm,),
        in_specs=[row(D_MODEL), _full((D_MODEL, D_MODEL))],
        out_specs=[row(S5_WIDTH), row(CONV_WIDTH), row(D_MODEL)],
        out_shape=[S((L, S5_WIDTH), f32), S((L, CONV_WIDTH), f32), S((L, D_MODEL), bf16)],
        compiler_params=_cp("parallel"),
    )(dx, w_out)


def _s5_discretise(lam_re, lam_im, log_dt, b_re, b_im):
    dt = jnp.exp(log_dt)
    mag = jnp.exp(lam_re * dt)
    abar_re = mag * jnp.cos(lam_im * dt)
    abar_im = mag * jnp.sin(lam_im * dt)
    den = lam_re * lam_re + lam_im * lam_im
    num_re = abar_re - 1.0
    f_re = ((num_re * lam_re + abar_im * lam_im) / den)[:, None, :]
    f_im = ((abar_im * lam_re - num_re * lam_im) / den)[:, None, :]
    return abar_re, abar_im, f_re * b_re - f_im * b_im, f_re * b_im + f_im * b_re


def _s5_params(lam_re, lam_im, log_dt, b_re, b_im):
    def body(lr, li, ld, br, bi, ar_ref, ai_ref, bbr_ref, bbi_ref):
        ar, ai, bbr, bbi = _s5_discretise(lr[...], li[...], ld[...], br[...], bi[...])
        ar_ref[...], ai_ref[...], bbr_ref[...], bbi_ref[...] = ar, ai, bbr, bbi

    gp = S((S5_GROUPS, S5_STATE), f32)
    gcp = S((S5_GROUPS, S5_GROUP_CH, S5_STATE), f32)
    return pl.pallas_call(body, name="s5_params", out_shape=[gp, gp, gcp, gcp])(lam_re, lam_im, log_dt, b_re, b_im)


def _s5_params_bwd(lam_re, lam_im, log_dt, b_re, b_im, d_ar, d_ai, d_bbr, d_bbi):
    def body(lr, li, ld, br, bi, car, cai, cbr, cbi, o_lr, o_li, o_ld, o_br, o_bi):
        _, vjp = jax.vjp(_s5_discretise, lr[...], li[...], ld[...], br[...], bi[...])
        o_lr[...], o_li[...], o_ld[...], o_br[...], o_bi[...] = vjp((car[...], cai[...], cbr[...], cbi[...]))

    gp = S((S5_GROUPS, S5_STATE), f32)
    gcp = S((S5_GROUPS, S5_GROUP_CH, S5_STATE), f32)
    return pl.pallas_call(body, name="s5_params_bwd", out_shape=[gp, gp, S((S5_GROUPS, 1), f32), gcp, gcp])(
        lam_re, lam_im, log_dt, b_re, b_im, d_ar, d_ai, d_bbr, d_bbi)


def _cmul(ar, ai, br, bi):
    return ar * br - ai * bi, ar * bi + ai * br


def _segment_starts(er, ei, ar, ai, steps, reverse):
    pr, pi = ar, ai
    n = 1
    while n < steps:
        pr, pi = _cmul(pr, pi, pr, pi)
        n *= 2
    assert n == steps
    row = lax.broadcasted_iota(jnp.int32, (SEGMENTS, SCAN_LANES), 0)
    hr = jnp.zeros((1, SCAN_LANES), f32)
    hi = jnp.zeros((1, SCAN_LANES), f32)
    out_r = jnp.zeros((SEGMENTS, SCAN_LANES), f32)
    out_i = jnp.zeros((SEGMENTS, SCAN_LANES), f32)
    order = range(SEGMENTS - 1, 0, -1) if reverse else range(0, SEGMENTS - 1)
    for r in order:
        qr, qi = _cmul(pr, pi, hr, hi)
        hr, hi = qr + er[r:r + 1, :], qi + ei[r:r + 1, :]
        nxt = r - 1 if reverse else r + 1
        out_r = jnp.where(row == nxt, hr, out_r)
        out_i = jnp.where(row == nxt, hi, out_i)
    return out_r, out_i


def _s5_read_bwd(dout, y_lin, u, d_skip, w_glu, b_glu, tm):
    L = u.shape[0]

    def body(do_ref, yl_ref, u_ref, d_ref, w_ref, b_ref, dyl_ref, du_ref, dd_ref, dw_ref, db_ref):
        @pl.when(pl.program_id(0) == 0)
        def _():
            dd_ref[...] = jnp.zeros_like(dd_ref)
            dw_ref[...] = jnp.zeros_like(dw_ref)
            db_ref[...] = jnp.zeros_like(db_ref)

        u, d, dout = u_ref[...], d_ref[...], do_ref[...]
        y, gelu_vjp = jax.vjp(_gelu, yl_ref[...] + d * u)
        yb = y.astype(bf16)
        sig = _sigmoid(_dot(yb, w_ref[...]) + b_ref[...])
        dz = dout * y * sig * (1.0 - sig)
        dzb = dz.astype(bf16)
        dy = dout * sig + _dot(dzb, w_ref[...], NT)
        (dyp,) = gelu_vjp(dy)
        dyl_ref[...] = dyp.astype(bf16)
        du_ref[...] = d * dyp
        dd_ref[...] += _rows8(dyp * u)
        db_ref[...] += _rows8(dz)
        dw_ref[...] += _dot(yb, dzb, TN)

    row = pl.BlockSpec((tm, S5_WIDTH), lambda i: (i, 0))
    vec = _full((1, S5_WIDTH))
    part = _full((8, S5_WIDTH))
    return pl.pallas_call(
        body, name="s5_read_bwd", grid=(L // tm,),
        in_specs=[row, row, row, vec, _full((S5_WIDTH, S5_WIDTH)), vec],
        out_specs=[row, row, part, _full((S5_WIDTH, S5_WIDTH)), part],
        out_shape=[S((L, S5_WIDTH), bf16), S((L, S5_WIDTH), f32), S((8, S5_WIDTH), f32),
                   S((S5_WIDTH, S5_WIDTH), f32), S((8, S5_WIDTH), f32)],
        compiler_params=_cp("arbitrary"),
    )(dout, y_lin, u, d_skip, w_glu, b_glu)


S5_CHUNK_CH = SCAN_LANES // S5_STATE * S5_GROUP_CH


def _s5_two_phase(L, bi):
    rows = bi * SEGMENTS
    nb = L // rows
    whole = pltpu.VMEM((L // SEGMENTS, SEGMENTS, SCAN_LANES), f32)
    mat = pl.BlockSpec((S5_CHUNK_CH, SCAN_LANES), lambda c, j: (c, c))
    vec = pl.BlockSpec((1, SCAN_LANES), lambda c, j: (0, c))
    tile = pl.BlockSpec((SEGMENTS, SCAN_LANES), lambda c, j: (0, c))
    return rows, nb, whole, mat, vec, tile


def _s5_forward(u, a_re, a_im, bb_re, bb_im, cc_re, cc_im, bi, ride=()):
    L = u.shape[0]
    rows, nb, whole, mat, vec, _ = _s5_two_phase(L, bi)

    def body(u_ref, ar_ref, ai_ref, br_ref, bi_ref, cr_ref, ci_ref, sr_ref, si_ref, yl_ref, hr_ref, hi_ref, dr_ref, di_ref):
        j = pl.program_id(1)
        ar = jnp.broadcast_to(ar_ref[...], (SEGMENTS, SCAN_LANES))
        ai = jnp.broadcast_to(ai_ref[...], (SEGMENTS, SCAN_LANES))

        @pl.when(j == 0)
        def _():
            hr_ref[...] = jnp.zeros_like(hr_ref)
            hi_ref[...] = jnp.zeros_like(hi_ref)

        @pl.when(j < nb)
        def _():
            base = j * bi
            ub = u_ref[...].astype(bf16)
            dr_ref[pl.ds(base, bi)] = _dot(ub, br_ref[...]).reshape(bi, SEGMENTS, SCAN_LANES)
            di_ref[pl.ds(base, bi)] = _dot(ub, bi_ref[...]).reshape(bi, SEGMENTS, SCAN_LANES)

            def step(i, c):
                pr, pi = _cmul(ar, ai, c[0], c[1])
                return pr + dr_ref[base + i], pi + di_ref[base + i]

            hr_ref[...], hi_ref[...] = lax.fori_loop(0, bi, step, (hr_ref[...], hi_ref[...]), unroll=True)

        @pl.when(j == nb - 1)
        def _():
            hr_ref[...], hi_ref[...] = _segment_starts(hr_ref[...], hi_ref[...], ar_ref[...], ai_ref[...], L // SEGMENTS, False)

        @pl.when(j >= nb)
        def _():
            base = (j - nb) * bi

            def step(i, c):
                pr, pi = _cmul(ar, ai, c[0], c[1])
                nr, nim = pr + dr_ref[base + i], pi + di_ref[base + i]
                dr_ref[base + i] = nr
                di_ref[base + i] = nim
                return nr, nim

            hr_ref[...], hi_ref[...] = lax.fori_loop(0, bi, step, (hr_ref[...], hi_ref[...]), unroll=True)
            sr = dr_ref[pl.ds(base, bi)].reshape(rows, SCAN_LANES).astype(bf16)
            si = di_ref[pl.ds(base, bi)].reshape(rows, SCAN_LANES).astype(bf16)
            sr_ref[...] = sr
            si_ref[...] = si
            yl_ref[...] = _dot(sr, cr_ref[...], NT) - _dot(si, ci_ref[...], NT)

    u_spec = pl.BlockSpec((rows, S5_CHUNK_CH), lambda c, j: (jnp.minimum(j, nb - 1), c))
    late = lambda width: pl.BlockSpec((rows, width), lambda c, j: (jnp.maximum(j - nb, 0), c))
    return _pallas(
        body, ride=ride, name="s5_forward", grid=(S5_LANES // SCAN_LANES, 2 * nb),
        in_specs=[u_spec, vec, vec, mat, mat, mat, mat],
        out_specs=[late(SCAN_LANES), late(SCAN_LANES), late(S5_CHUNK_CH)],
        out_shape=[S((L, S5_LANES), bf16)] * 2 + [S((L, S5_WIDTH), f32)],
        scratch_shapes=[pltpu.VMEM((SEGMENTS, SCAN_LANES), f32)] * 2 + [whole] * 2,
        compiler_params=_cp("parallel", "arbitrary"),
    )(u, a_re, a_im, bb_re, bb_im, cc_re, cc_im)


def _s5_backward(dy, u, du_skip, s_re, s_im, a_re, a_im, bb_re, bb_im, cc_re, cc_im, bi, ride=()):
    L = u.shape[0]
    rows, nb, whole, mat, vec, tile = _s5_two_phase(L, bi)
    per = rows // 16

    def body(dy_ref, u_ref, dus_ref, sr_ref, si_ref, pr_ref, pi_ref, lr_ref, li_ref, ar_ref, ai_ref, br_ref, bi_ref, cr_ref,
             ci_ref, du_ref, dar_ref, dai_ref, dbr_ref, dbi_ref, dcr_ref, dci_ref, hr_ref, hi_ref, gr_ref, gi_ref, fr_ref, fi_ref):
        j = pl.program_id(1)
        ar = jnp.broadcast_to(ar_ref[...], (SEGMENTS, SCAN_LANES))
        ai = jnp.broadcast_to(ai_ref[...], (SEGMENTS, SCAN_LANES))

        @pl.when(j == 0)
        def _():
            for ref in (hr_ref, hi_ref, dar_ref, dai_ref, dbr_ref, dbi_ref, dcr_ref, dci_ref):
                ref[...] = jnp.zeros_like(ref)

        @pl.when(j < nb)
        def _():
            base = (nb - 1 - j) * bi
            dy = dy_ref[...]
            gr_ref[pl.ds(base, bi)] = _dot(dy, cr_ref[...]).reshape(bi, SEGMENTS, SCAN_LANES)
            gi_ref[pl.ds(base, bi)] = (-_dot(dy, ci_ref[...])).reshape(bi, SEGMENTS, SCAN_LANES)

            def step(n, c):
                i = base + bi - 1 - n
                qr, qi = _cmul(ar, ai, c[0], c[1])
                return qr + gr_ref[i], qi + gi_ref[i]

            hr_ref[...], hi_ref[...] = lax.fori_loop(0, bi, step, (hr_ref[...], hi_ref[...]), unroll=True)

        @pl.when(j == nb - 1)
        def _():
            hr_ref[...], hi_ref[...] = _segment_starts(hr_ref[...], hi_ref[...], ar_ref[...], ai_ref[...], L // SEGMENTS, True)

        @pl.when(j >= nb)
        def _():
            blk = 2 * nb - 1 - j
            base = blk * bi
            sr, si = sr_ref[...], si_ref[...]
            fr_ref[...] = sr.astype(f32).reshape(bi, SEGMENTS, SCAN_LANES)
            fi_ref[...] = si.astype(f32).reshape(bi, SEGMENTS, SCAN_LANES)

            def step(n, c):
                i = bi - 1 - n
                gr, gi, accr, acci = c
                qr, qi = _cmul(ar, ai, gr, gi)
                gr, gi = qr + gr_ref[base + i], qi + gi_ref[base + i]
                gr_ref[base + i] = gr
                gi_ref[base + i] = gi
                pr, pi = fr_ref[i - 1], fi_ref[i - 1]
                return gr, gi, accr + (gr * pr + gi * pi), acci + (gi * pr - gr * pi)

            gr, gi, accr, acci = lax.fori_loop(0, bi - 1, step, (hr_ref[...], hi_ref[...], dar_ref[...], dai_ref[...]), unroll=True)
            qr, qi = _cmul(ar, ai, gr, gi)
            gr, gi = qr + gr_ref[base], qi + gi_ref[base]
            gr_ref[base] = gr
            gi_ref[base] = gi
            hr_ref[...], hi_ref[...] = gr, gi
            row = lax.broadcasted_iota(jnp.int32, (SEGMENTS, SCAN_LANES), 0)
            older = lambda ref: ref[...].astype(f32)[SEGMENTS:, :]
            wrap_r = jnp.where(row == 0, 0.0, pltpu.roll(older(lr_ref), 1, 0))
            wrap_i = jnp.where(row == 0, 0.0, pltpu.roll(older(li_ref), 1, 0))
            pr = jnp.where(blk == 0, wrap_r, older(pr_ref))
            pi = jnp.where(blk == 0, wrap_i, older(pi_ref))
            dar_ref[...] = accr + gr * pr + gi * pi
            dai_ref[...] = acci + gi * pr - gr * pi

            g_re = gr_ref[pl.ds(base, bi)].reshape(rows, SCAN_LANES).astype(bf16)
            g_im = gi_ref[pl.ds(base, bi)].reshape(rows, SCAN_LANES).astype(bf16)
            ub = u_ref[...].astype(bf16)
            dy = dy_ref[...]
            du_ref[...] = dus_ref[...] + _dot(g_re, br_ref[...], NT) + _dot(g_im, bi_ref[...], NT)
            dbr_ref[...] += _dot(ub, g_re, TN)
            dbi_ref[...] += _dot(ub, g_im, TN)
            dcr_ref[...] += _dot(dy, sr, TN)
            dci_ref[...] -= _dot(dy, si, TN)

    block = lambda c, j: jnp.where(j < nb, nb - 1 - j, 2 * nb - 1 - j)
    late_block = lambda c, j: jnp.minimum(2 * nb - 1 - j, nb - 1)
    both = pl.BlockSpec((rows, S5_CHUNK_CH), lambda c, j: (block(c, j), c))
    chan = pl.BlockSpec((rows, S5_CHUNK_CH), lambda c, j: (late_block(c, j), c))
    state = pl.BlockSpec((rows, SCAN_LANES), lambda c, j: (late_block(c, j), c))
    prev = pl.BlockSpec((16, SCAN_LANES), lambda c, j: (jnp.maximum(late_block(c, j) * per - 1, 0), c))
    last = pl.BlockSpec((16, SCAN_LANES), lambda c, j: (L // 16 - 1, c))
    grad = pl.BlockSpec((S5_CHUNK_CH, SCAN_LANES), lambda c, j: (c, 0))
    return _pallas(
        body, ride=ride, name="s5_backward", grid=(S5_LANES // SCAN_LANES, 2 * nb),
        in_specs=[both, chan, chan, state, state, prev, prev, last, last, vec, vec, mat, mat, mat, mat],
        out_specs=[chan, tile, tile, grad, grad, grad, grad],
        out_shape=[S((L, S5_WIDTH), f32)] + [S((SEGMENTS, S5_LANES), f32)] * 2 + [S((S5_WIDTH, SCAN_LANES), f32)] * 4,
        scratch_shapes=[pltpu.VMEM((SEGMENTS, SCAN_LANES), f32)] * 2 + [whole] * 2 + [pltpu.VMEM((bi, SEGMENTS, SCAN_LANES), f32)] * 2,
        compiler_params=_cp("parallel", "arbitrary"),
    )(dy, u, du_skip, s_re, s_im, s_re, s_im, s_re, s_im, a_re, a_im, bb_re, bb_im, cc_re, cc_im)


def _s5_gate(y_lin, u, d_skip, w_glu, b_glu, tm, ride=()):
    L = u.shape[0]

    def body(yl_ref, u_ref, d_ref, w_ref, b_ref, o_ref):
        y = _gelu(yl_ref[...] + d_ref[...] * u_ref[...])
        z = _dot(y.astype(bf16), w_ref[...]) + b_ref[...]
        o_ref[...] = (y * _sigmoid(z)).astype(bf16)

    row = pl.BlockSpec((tm, S5_WIDTH), lambda i: (i, 0))
    vec = _full((1, S5_WIDTH))
    return _pallas(
        body, ride=ride, name="s5_gate", grid=(L // tm,),
        in_specs=[row, row, vec, _full((S5_WIDTH, S5_WIDTH)), vec],
        out_specs=row, out_shape=S((L, S5_WIDTH), bf16),
        compiler_params=_cp("parallel"),
    )(y_lin, u, d_skip, w_glu, b_glu)


def _group_mean(x, avg):
    return _dot(x.astype(bf16), avg)


def _conv_act(zn, ln_g, ln_b):
    t = zn * ln_g + ln_b
    return t * _sigmoid(t)


def _glu_padded(v_ref, halo_ref, zpad_ref, tm):
    v = v_ref[...]
    vh = halo_ref[...]
    zh = vh[:, :CONV_WIDTH] * _sigmoid(vh[:, CONV_WIDTH:])
    zpad_ref[:CONV_HALO, :] = jnp.where(pl.program_id(0) > 0, zh, 0.0)
    zpad_ref[CONV_HALO:CONV_HALO + tm, :] = v[:, :CONV_WIDTH] * _sigmoid(v[:, CONV_WIDTH:])
    zpad_ref[CONV_HALO + tm:, :] = jnp.zeros((8, CONV_WIDTH), f32)


def _shifted(pad_ref, sh_ref, tm):
    for b in range(8):
        sh_ref[b] = pad_ref[pl.ds(b, tm + CONV_HALO), :]


def _window(sh_ref, r0, off, rows):
    start = r0 + 8 * (off // 8)
    return sh_ref[off % 8, pl.ds(start if isinstance(start, int) else pl.multiple_of(start, 8), rows), :]


def _tap_sum(w_ref, sh_ref, taps, out_ref, tm, bias):
    for r0 in range(0, tm, CONV_ROWS):
        acc = jnp.zeros((CONV_ROWS, CONV_WIDTH), f32) + bias
        for k, off in taps:
            acc = acc + w_ref[k:k + 1, :] * _window(sh_ref, r0, off, CONV_ROWS)
        out_ref[r0:r0 + CONV_ROWS, :] = acc


FWD_TAPS = [(k, CONV_HALO - (CONV_K - 1) + k) for k in range(CONV_K)]
BWD_TAPS = [(k, CONV_K - 1 - k) for k in range(CONV_K)]


def _conv_specs(tm):
    per = tm // CONV_HALO
    vrow = pl.BlockSpec((tm, 2 * CONV_WIDTH), lambda i: (i, 0))
    vhalo = pl.BlockSpec((CONV_HALO, 2 * CONV_WIDTH), lambda i: (jnp.maximum(i * per - 1, 0), 0))
    return vrow, vhalo


def _conv_scratch(tm):
    return [pltpu.VMEM((tm + CONV_HALO + 8, CONV_WIDTH), f32), pltpu.VMEM((8, tm + CONV_HALO, CONV_WIDTH), f32)]


def _conv_fwd(v, w_dw, b_dw, ln_g, ln_b, avg, tm, ride=()):
    L = v.shape[0]

    def body(v_ref, halo_ref, w_ref, b_ref, g_ref, bb_ref, avg_ref, o_ref, zc_ref, zpad_ref, zs_ref):
        _glu_padded(v_ref, halo_ref, zpad_ref, tm)
        _shifted(zpad_ref, zs_ref, tm)
        _tap_sum(w_ref, zs_ref, FWD_TAPS, zc_ref, tm, b_ref[...])
        zc = zc_ref[...]
        xc = zc - _group_mean(zc, avg_ref[...])
        zn = xc * lax.rsqrt(_group_mean(xc * xc, avg_ref[...]) + EPS)
        o_ref[...] = _conv_act(zn, g_ref[...], bb_ref[...]).astype(bf16)

    vrow, vhalo = _conv_specs(tm)
    vec = _full((1, CONV_WIDTH))
    row = pl.BlockSpec((tm, CONV_WIDTH), lambda i: (i, 0))
    return _pallas(
        body, ride=ride, name="conv_fwd", grid=(L // tm,),
        in_specs=[vrow, vhalo, _full((CONV_HALO, CONV_WIDTH)), vec, vec, vec, _full((CONV_WIDTH, CONV_WIDTH))],
        out_specs=[row, row], out_shape=[S((L, CONV_WIDTH), bf16), S((L, CONV_WIDTH), f32)],
        scratch_shapes=_conv_scratch(tm),
        compiler_params=_cp("arbitrary"),
    )(v, v, w_dw, b_dw, ln_g, ln_b, avg)


def _conv_bwd_norm(dout, zc, ln_g, ln_b, avg, tm):
    L = zc.shape[0]

    def body(do_ref, zc_ref, g_ref, bb_ref, avg_ref, dzc_ref, dg_ref, db_ref, dbd_ref):
        @pl.when(pl.program_id(0) == 0)
        def _():
            dg_ref[...] = jnp.zeros_like(dg_ref)
            db_ref[...] = jnp.zeros_like(db_ref)
            dbd_ref[...] = jnp.zeros_like(dbd_ref)

        avg = avg_ref[...]
        zc = zc_ref[...]
        xc = zc - _group_mean(zc, avg)
        rstd = lax.rsqrt(_group_mean(xc * xc, avg) + EPS)
        xhat = xc * rstd
        _, act_vjp = jax.vjp(_conv_act, xhat, g_ref[...], bb_ref[...])
        dxhat, dg, db = act_vjp(do_ref[...])
        dzc = rstd * (dxhat - _group_mean(dxhat, avg) - xhat * _group_mean(dxhat * xhat, avg))
        dzc_ref[...] = dzc
        dg_ref[0:1, :] += dg
        db_ref[0:1, :] += db
        dbd_ref[...] += _rows8(dzc)

    vec = _full((1, CONV_WIDTH))
    row = pl.BlockSpec((tm, CONV_WIDTH), lambda i: (i, 0))
    part = _full((8, CONV_WIDTH))
    return pl.pallas_call(
        body, name="conv_bwd_norm", grid=(L // tm,),
        in_specs=[row, row, vec, vec, _full((CONV_WIDTH, CONV_WIDTH))],
        out_specs=[row, part, part, part],
        out_shape=[S((L, CONV_WIDTH), f32)] + [S((8, CONV_WIDTH), f32)] * 3,
        compiler_params=_cp("arbitrary"),
    )(dout, zc, ln_g, ln_b, avg)


def _conv_bwd_taps(dzc, v, w_dw, tm, ride=()):
    L = v.shape[0]
    nt = L // tm
    per = tm // CONV_HALO

    def body(d_ref, dn_ref, v_ref, w_ref, dv_ref, dw_ref, dpad_ref, ds_ref, dz_ref, z_ref):
        i = pl.program_id(0)

        @pl.when(i == 0)
        def _():
            dw_ref[...] = jnp.zeros_like(dw_ref)

        v = v_ref[...]
        sig = _sigmoid(v[:, CONV_WIDTH:])
        z_ref[...] = v[:, :CONV_WIDTH] * sig
        dpad_ref[:tm, :] = d_ref[...]
        dpad_ref[tm:tm + CONV_HALO, :] = jnp.where(i < nt - 1, dn_ref[...], 0.0)
        dpad_ref[tm + CONV_HALO:, :] = jnp.zeros((8, CONV_WIDTH), f32)
        _shifted(dpad_ref, ds_ref, tm)
        _tap_sum(w_ref, ds_ref, BWD_TAPS, dz_ref, tm, 0.0)

        for first in range(0, CONV_K, 8):
            taps = BWD_TAPS[first:first + 8]

            accs = [jnp.zeros((8, CONV_WIDTH), f32) for _ in taps]
            for r0 in range(0, tm, 8):
                z = z_ref[r0:r0 + 8, :]
                accs = [acc + z * _window(ds_ref, r0, off, 8) for acc, (_, off) in zip(accs, taps)]
            for acc, (k, _) in zip(accs, taps):
                dw_ref[k] += acc

        dz = dz_ref[...]
        dv_ref[:, :CONV_WIDTH] = dz * sig
        dv_ref[:, CONV_WIDTH:] = dz * v[:, :CONV_WIDTH] * sig * (1.0 - sig)

    vrow, _ = _conv_specs(tm)
    row = pl.BlockSpec((tm, CONV_WIDTH), lambda i: (i, 0))
    nxt = pl.BlockSpec((CONV_HALO, CONV_WIDTH), lambda i: (jnp.minimum((i + 1) * per, nt * per - 1), 0))
    return _pallas(
        body, ride=ride, name="conv_bwd_taps", grid=(nt,),
        in_specs=[row, nxt, vrow, _full((CONV_HALO, CONV_WIDTH))],
        out_specs=[vrow, _full((CONV_HALO, 8, CONV_WIDTH))],
        out_shape=[S((L, 2 * CONV_WIDTH), f32), S((CONV_HALO, 8, CONV_WIDTH), f32)],
        scratch_shapes=_conv_scratch(tm) + [pltpu.VMEM((tm, CONV_WIDTH), f32)] * 2,
        compiler_params=_cp("arbitrary"),
    )(dzc, dzc, v, w_dw)


def _to_segments(a):
    L, c = a.shape
    return a.reshape(SEGMENTS, L // SEGMENTS, c).transpose(1, 0, 2).reshape(L, c)


def _from_segments(a):
    L, c = a.shape
    return a.reshape(L // SEGMENTS, SEGMENTS, c).transpose(1, 0, 2).reshape(L, c)


def _block_diag(ms):
    n = len(ms)

    def body(*refs):
        for a in range(n):
            out = refs[n + a]
            out[...] = jnp.zeros_like(out)
            for g in range(S5_GROUPS):
                rows = slice(g * S5_GROUP_CH, (g + 1) * S5_GROUP_CH)
                out[rows, g * S5_STATE:(g + 1) * S5_STATE] = refs[a][rows, :].astype(bf16)

    return pl.pallas_call(body, name="s5_block_diag", out_shape=[S((S5_WIDTH, S5_LANES), bf16)] * n,
                          compiler_params=pltpu.CompilerParams(vmem_limit_bytes=VMEM_LIMIT))(
        *[m.reshape(S5_WIDTH, S5_STATE) for m in ms])


def _diag_blocks(ms):
    n = len(ms)
    per_chunk = SCAN_LANES // S5_STATE

    def body(*refs):
        for a in range(n):
            for g in range(S5_GROUPS):
                rows = slice(g * S5_GROUP_CH, (g + 1) * S5_GROUP_CH)
                at = g % per_chunk * S5_STATE
                refs[n + a][rows, :] = refs[a][rows, at:at + S5_STATE]

    out = pl.pallas_call(body, name="s5_diag_blocks", out_shape=[S((S5_WIDTH, S5_STATE), f32)] * n,
                         compiler_params=pltpu.CompilerParams(vmem_limit_bytes=VMEM_LIMIT))(*ms)
    return [o.reshape(S5_GROUPS, S5_GROUP_CH, S5_STATE) for o in out]


class _NoExchanges:
    def before(self, point):
        return ()

    def after(self, point):
        pass

    def alone(self, point):
        pass


def _ffn_block(x, p, tag, tm, sched, head=None, mixed=None, mixer=None):
    point = tag + "_up"
    h, dadg, dadu, a, *x_in = _ffn_up(x, p[tag + "_norm"], p[tag + "_w_gate"], p[tag + "_w_up"], tm, tag, mixed,
                                      ride=sched.before(point))
    sched.after(point)
    x, = x_in or [x]
    if head is None:
        out = _ffn_down(x, a, p[tag + "_w_down"], tm, tag, mixer and tuple(p[n] for n in mixer), ride=sched.before(tag + "_down"))
        sched.after(tag + "_down")
    else:
        out = _ffn_down_loss(x, a, p[tag + "_w_down"], *head, tm, tag)
    return out, (x, h, dadg, dadu, a)


def _ffn_block_bwd(dxo, x, p, tag, saved, tm, grads, sched, parts=1, dxh=None):
    _, h, dadg, dadu, a = saved

    def weight_grad(which, lhs, rhs):
        point = tag + "_dw_" + which
        grads[tag + "_w_" + which] = _mm_tn(lhs, rhs, bf16, point, ride=sched.before(point))
        sched.after(point)

    if dxh is not None:
        weight_grad("down", a, dxh)
    dgate, dup, own_dxh = _ffn_bwd_act(dxo, p[tag + "_w_down"], dadg, dadu, tm, tag, ride=sched.before(tag + "_bwd_act"))
    sched.after(tag + "_bwd_act")
    weight_grad("gate", dgate, h)
    weight_grad("up", dup, h)
    if dxh is None:
        weight_grad("down", a, own_dxh)
    tiles = x.shape[0] // tm
    dx, dgs = None, []
    for k in range(parts):
        point = tag + "_bwd_in" + ("_%d" % k) * (parts > 1)
        dx, dg = _ffn_bwd_in(dxo, x, p[tag + "_norm"], dgate, dup, p[tag + "_w_gate"], p[tag + "_w_up"], tm, point,
                             tiles=(k * tiles // parts, tiles // parts), into=dx, ride=sched.before(point))
        sched.after(point)
        dgs.append(dg)
    grads[tag + "_norm"] = functools.reduce(jnp.add, dgs)
    return dx


def _local_step(x, target, p, grads, sched):
    L = x.shape[0]
    tm = min(512, L // 2)
    ni = L // SEGMENTS
    bi = min(64, ni)

    def carried(point, fn, *args):
        out = fn(*args, ride=sched.before(point))
        sched.after(point)
        return out

    (x1, h2, u_s5, v), saved1 = _ffn_block(x, p, "ffn1", tm, sched, mixer=("mix_norm", "w_in"))

    s5_in = (p["s5_lam_re"], p["s5_lam_im"], p["s5_log_dt"].reshape(S5_GROUPS, 1), p["s5_b_re"], p["s5_b_im"])
    abar_re, abar_im, bbar_re, bbar_im = _s5_params(*s5_in)
    a_re, a_im = abar_re.reshape(1, S5_LANES), abar_im.reshape(1, S5_LANES)
    bb_re, bb_im, cc_re, cc_im = _block_diag([bbar_re, bbar_im, p["s5_c_re"], p["s5_c_im"]])
    u_seg = _to_segments(u_s5)
    s_re, s_im, y_lin = carried("s5_forward", _s5_forward, u_seg, a_re, a_im, bb_re, bb_im, cc_re, cc_im, bi)
    y_s5 = _from_segments(_s5_gate(y_lin, u_seg, p["s5_d"], p["s5_w_glu"], p["s5_b_glu"], tm))
    w_dw = jnp.pad(p["conv_w_dw"], ((0, CONV_HALO - CONV_K), (0, 0)))
    heads = jnp.arange(CONV_WIDTH) // CONV_HEAD
    avg = ((heads[:, None] == heads[None, :]).astype(f32) / CONV_HEAD).astype(bf16)
    y_conv, zc = carried("conv_fwd", _conv_fwd, v, w_dw, p["conv_b_dw"], p["conv_ln_g"], p["conv_ln_b"], avg, tm)

    (dx3, grads["final_norm"], loss_terms), saved2 = _ffn_block(
        x1, p, "ffn2", tm, sched, head=(target, p["final_norm"].reshape(1, D_MODEL)), mixed=(y_s5, y_conv, p["w_out"]))
    x2 = saved2[0]
    grads["loss_terms"] = loss_terms

    dx2 = _ffn_block_bwd(dx3, x2, p, "ffn2", saved2, tm, grads, sched)

    dy_s5, dy_conv, dx2b = carried("mix_out_bwd", _mix_out_bwd, dx2, p["w_out"], tm)
    grads["w_out"] = _dw_out(y_s5, y_conv, dx2b)
    dy_lin, du_skip, dd8, grads["s5_w_glu"], dbg8 = _s5_read_bwd(
        _to_segments(dy_s5), y_lin, u_seg, p["s5_d"], p["s5_w_glu"], p["s5_b_glu"], tm)
    grads["s5_d"] = dd8.sum(axis=0, keepdims=True)
    grads["s5_b_glu"] = dbg8.sum(axis=0, keepdims=True)
    du_seg, da_re8, da_im8, dbb_re, dbb_im, dcc_re, dcc_im = carried(
        "s5_backward", _s5_backward, dy_lin, u_seg, du_skip, s_re, s_im, a_re, -a_im, bb_re, bb_im, cc_re, cc_im, bi)
    d_abar = lambda a8: a8.sum(axis=0).reshape(S5_GROUPS, S5_STATE)
    grads["s5_c_re"], grads["s5_c_im"], d_bbr, d_bbi = _diag_blocks([dcc_re, dcc_im, dbb_re, dbb_im])
    d_lr, d_li, d_ld, d_br, d_bi = _s5_params_bwd(*s5_in, d_abar(da_re8), d_abar(da_im8), d_bbr, d_bbi)
    grads["s5_lam_re"], grads["s5_lam_im"], grads["s5_log_dt"] = d_lr, d_li, d_ld.reshape(1, S5_GROUPS)
    grads["s5_b_re"], grads["s5_b_im"] = d_br, d_bi
    dzc, dlg8, dlb8, dbd8 = _conv_bwd_norm(dy_conv, zc, p["conv_ln_g"], p["conv_ln_b"], avg, tm)
    grads["conv_ln_g"] = dlg8.sum(axis=0, keepdims=True)
    grads["conv_ln_b"] = dlb8.sum(axis=0, keepdims=True)
    grads["conv_b_dw"] = dbd8.sum(axis=0, keepdims=True)
    dv, dw8 = carried("conv_bwd_taps", _conv_bwd_taps, dzc, v, w_dw, tm)
    grads["conv_w_dw"] = dw8.sum(axis=1)[:CONV_K]
    dx1, grads["mix_norm"], dub, dx1h = _mix_in_bwd(dx2, x1, p["mix_norm"], _from_segments(du_seg), dv, p["w_in"], tm)
    grads["w_in"] = _mm_tn(dub, h2, bf16, "dw_in")

    dx0 = _ffn_block_bwd(dx1, x, p, "ffn1", saved1, tm, grads, sched, parts=min(2, L // tm), dxh=dx1h)
    sched.alone("tail")
    return loss_terms, dx0


MESH = pl.DeviceIdType.MESH
ANY = pl.BlockSpec(memory_space=pl.ANY)


def _place():
    return lax.axis_index("x"), lax.axis_index("y"), lax.axis_index("c")


class _Exchange:
    def __init__(self, ins, out_shape, sems, start, finish):
        self.ins, self.out_shape, self.sems, self.start, self.finish = list(ins), list(out_shape), list(sems), start, finish
        self.out = None


def _pallas(body, *, ride=(), **kw):
    if not ride:
        return pl.pallas_call(body, **kw)

    def run(*args):
        out_shape = kw.get("out_shape", [])
        single = not isinstance(out_shape, (list, tuple))
        shapes = [out_shape] if single else list(out_shape)
        out_specs = [kw["out_specs"]] if single else list(kw.get("out_specs", []))
        grid = tuple(kw.get("grid", ()))
        scratch = list(kw.get("scratch_shapes", ()))
        n_in, n_out, n_scr = len(args), len(shapes), len(scratch)
        r_in = [len(e.ins) for e in ride]
        r_out = [len(e.out_shape) for e in ride]
        r_sem = [len(e.sems) for e in ride]

        def wrapped(*refs):
            own_in, refs = refs[:n_in], refs[n_in:]
            ex_in, refs = refs[:sum(r_in)], refs[sum(r_in):]
            own_out, refs = refs[:n_out], refs[n_out:]
            ex_out, refs = refs[:sum(r_out)], refs[sum(r_out):]
            own_scr, ex_sem = refs[:n_scr], refs[n_scr:]
            parts = []
            for e, ni, no, ns in zip(ride, r_in, r_out, r_sem):
                parts.append((e, ex_in[:ni], ex_out[:no], ex_sem[:ns]))
                ex_in, ex_out, ex_sem = ex_in[ni:], ex_out[no:], ex_sem[ns:]

            def at(step):
                def go():
                    for e, i, o, s in parts:
                        getattr(e, step)(i, o, s)
                if grid:
                    ids = [pl.program_id(d) for d in range(len(grid))]
                    when = [i == (0 if step == "start" else g - 1) for i, g in zip(ids, grid)]
                    pl.when(functools.reduce(lambda a, b: a & b, when))(go)
                else:
                    go()

            at("start")
            if body is not None:
                body(*own_in, *own_out, *own_scr)
            at("finish")

        outs = pl.pallas_call(
            wrapped, name=kw["name"], grid=grid,
            in_specs=list(kw.get("in_specs", [])) + [ANY] * sum(r_in),
            out_specs=out_specs + [ANY] * sum(r_out),
            out_shape=shapes + [s for e in ride for s in e.out_shape],
            scratch_shapes=scratch + [s for e in ride for s in e.sems],
            input_output_aliases=kw.get("input_output_aliases", {}),
            compiler_params=_cp(*["arbitrary"] * len(grid)),
        )(*args, *[a for e in ride for a in e.ins])
        own, rest = outs[:n_out], outs[n_out:]
        for e, no in zip(ride, r_out):
            e.out, rest = list(rest[:no]), rest[no:]
        return own[0] if single else own

    return run


def _exchange(ride, name):
    _pallas(None, ride=ride, name=name)()


def _gather(arrs):
    n = len(arrs)

    def copies(ins, outs, sems):
        send_sems, recv_sems, local_sems = sems
        x, y, c = _place()
        me, sibling = (x, y, c), (x, y, 1 - c)
        chips = [(1 - x, y), (x, 1 - y), (1 - x, 1 - y)]

        def place(a, block):
            return outs[a].at[block]

        def copy(a, k, block, to, src=None):
            px, py, pc = block
            dst = place(a, 4 * px + 2 * py + pc)
            return pltpu.make_async_remote_copy(
                src_ref=dst if src is None else src, dst_ref=dst, send_sem=send_sems.at[7 * a + k],
                recv_sem=recv_sems.at[7 * a + k], device_id=to, device_id_type=MESH)

        def own():
            local = [pltpu.make_async_copy(ins[a], place(a, 4 * x + 2 * y + c), local_sems.at[a]) for a in range(n)]
            remote = []
            for a in range(n):
                remote.append(copy(a, 0, me, sibling, src=ins[a]))
                remote += [copy(a, 1 + j, me, (*chip, c), src=ins[a]) for j, chip in enumerate(chips)]
            return local, remote

        return c, me, sibling, chips, copy, own

    def start(ins, outs, sems):
        local, remote = copies(ins, outs, sems)[-1]()
        for cp in local + remote:
            cp.start()

    def finish(ins, outs, sems):
        c, me, sibling, chips, copy, own = copies(ins, outs, sems)
        passed = []
        for j, chip in enumerate(chips):
            for a in range(n):
                copy(a, 1 + j, (*chip, c), me).wait_recv()
                passed.append(copy(a, 4 + j, (*chip, c), sibling))
                passed[-1].start()
        for a in range(n):
            copy(a, 0, sibling, me).wait_recv()
            for j, chip in enumerate(chips):
                copy(a, 4 + j, (*chip, 1 - c), me).wait_recv()
        local, remote = own()
        for cp in remote + passed:
            cp.wait_send()
        for cp in local:
            cp.wait()

    dma = pltpu.SemaphoreType.DMA
    shapes = [S((N_DEV, *a.shape), a.dtype) for a in arrs]
    return _Exchange(arrs, shapes, [dma((7 * n,)), dma((7 * n,)), dma((n,))], start, finish)


def _swap_with_sibling(gs):
    n = len(gs)

    def copies(ins, outs, sems):
        x, y, c = _place()
        return [pltpu.make_async_remote_copy(
            src_ref=ins[a].at[:, 1 - c], dst_ref=outs[a], send_sem=sems[0].at[a], recv_sem=sems[1].at[a],
            device_id=(x, y, 1 - c), device_id_type=MESH) for a in range(n)]

    def start(ins, outs, sems):
        for cp in copies(ins, outs, sems):
            cp.start()

    def finish(ins, outs, sems):
        for cp in copies(ins, outs, sems):
            cp.wait()

    dma = pltpu.SemaphoreType.DMA
    return _Exchange(gs, [S((N_CHIP, *g.shape[2:]), g.dtype) for g in gs], [dma((n,)), dma((n,))], start, finish)


def _swap_with_chips(ps):
    n = len(ps)

    def copies(ins, outs, sems):
        x, y, c = _place()
        q = 2 * x + y
        peers = [(x, 1 - y), (1 - x, y), (1 - x, 1 - y)]

        def copy(a, j, slot_from, slot_to):
            px, py = peers[j]
            return pltpu.make_async_remote_copy(
                src_ref=ins[a].at[slot_from], dst_ref=outs[a].at[slot_to], send_sem=sems[0].at[3 * a + j],
                recv_sem=sems[1].at[3 * a + j], device_id=(px, py, c), device_id_type=MESH)

        sends = lambda: [copy(a, j, 2 * peers[j][0] + peers[j][1], q) for a in range(n) for j in range(3)]
        lands = lambda: [copy(a, j, q, 2 * peers[j][0] + peers[j][1]) for a in range(n) for j in range(3)]
        return sends, lands

    def start(ins, outs, sems):
        for cp in copies(ins, outs, sems)[0]():
            cp.start()

    def finish(ins, outs, sems):
        sends, lands = copies(ins, outs, sems)
        for cp in lands():
            cp.wait_recv()
        for cp in sends():
            cp.wait_send()

    dma = pltpu.SemaphoreType.DMA
    return _Exchange(ps, [S(p.shape, p.dtype) for p in ps], [dma((3 * n,)), dma((3 * n,))], start, finish)


def _row_tile(rows, cols, itemsize):
    t = rows
    while t * cols * itemsize > (1 << 20) and t % 32 == 0:
        t //= 2
    return t


def _add_sibling(g4, st, core, name):
    _, R, C = st.shape
    tr = _row_tile(R, C, 1)

    def body(c_ref, g_ref, s_ref, o_ref):
        o_ref[...] = (g_ref[...].astype(f32) + s_ref[...].astype(f32)).astype(bf16)

    mine = pl.BlockSpec((None, None, tr, C), lambda q, i, c: (q, c[0], i, 0))
    return pl.pallas_call(
        body, name=name,
        grid_spec=pltpu.PrefetchScalarGridSpec(
            num_scalar_prefetch=1, grid=(N_CHIP, R // tr),
            in_specs=[mine,
                      pl.BlockSpec((None, tr, C), lambda q, i, c: (q, i, 0))],
            out_specs=pl.BlockSpec((None, tr, C), lambda q, i, c: (q, i, 0))),
        out_shape=S((N_CHIP, R, C), bf16),
        compiler_params=_cp("parallel", "parallel"),
    )(core, g4, st)


SMEM = pl.BlockSpec(memory_space=pltpu.SMEM)
VMEM = pl.BlockSpec(memory_space=pltpu.VMEM)


def _add_sibling_small(items, core, name):
    n = len(items)

    def body(c_ref, *refs):
        c = c_ref[0]
        for k in range(n):
            g_ref, s_ref, o_ref = refs[2 * k], refs[2 * k + 1], refs[2 * n + k]
            for q in range(N_CHIP):
                o_ref[q] = (g_ref[q, c].astype(f32) + s_ref[q].astype(f32)).astype(bf16)

    return pl.pallas_call(body, name=name, in_specs=[SMEM] + [VMEM] * (2 * n), out_specs=[VMEM] * n,
                          out_shape=[S(st.shape, bf16) for _, st in items],
                          compiler_params=pltpu.CompilerParams(vmem_limit_bytes=VMEM_LIMIT))(
        core, *[a for item in items for a in item])


def _adam_small(items, slots, name):
    n = len(items)

    def body(s_ref, *refs):
        ins, outs = refs[:5 * n], refs[5 * n:]
        for k in range(n):
            w_ref, m_ref, v_ref, p_ref, got_ref = ins[5 * k:5 * k + 5]
            g = p_ref[s_ref[0]].astype(f32)
            for j in range(1, N_CHIP):
                g = g + got_ref[s_ref[j]].astype(f32)
            outs[4 * k][...] = g
            outs[4 * k + 1][...], outs[4 * k + 2][...], outs[4 * k + 3][...] = _adamw(w_ref[...], g, m_ref[...], v_ref[...])

    out = pl.pallas_call(body, name=name, in_specs=[SMEM] + [VMEM] * (5 * n), out_specs=[VMEM] * (4 * n),
                         out_shape=[S(item[0].shape, f32) for item in items for _ in range(4)],
                         compiler_params=pltpu.CompilerParams(vmem_limit_bytes=VMEM_LIMIT))(
        slots, *[a for item in items for a in item])
    return [out[4 * k:4 * k + 4] for k in range(n)]


def _adamw(w, g, m, v):
    m = B1 * m + (1.0 - B1) * g
    v = B2 * v + (1.0 - B2) * (g * g)
    m_hat = m / (1.0 - B1 ** STEP)
    v_hat = v / (1.0 - B2 ** STEP)
    return -LR * (m_hat / (jnp.sqrt(v_hat) + ADAM_EPS) + WD * w), m, v


def _adam_sharded(items, slots, name):
    n = len(items)
    R, C = items[0][0].shape
    tr = _row_tile(R, C, 4 * n)

    def body(s_ref, *refs):
        ins, outs = refs[:7 * n], refs[7 * n:]
        for k in range(n):
            w_ref, m_ref, v_ref, p_ref, a_ref, b_ref, c_ref = ins[7 * k:7 * k + 7]
            g = p_ref[...].astype(f32) + a_ref[...].astype(f32) + b_ref[...].astype(f32) + c_ref[...].astype(f32)
            outs[4 * k][...] = g
            outs[4 * k + 1][...], outs[4 * k + 2][...], outs[4 * k + 3][...] = _adamw(w_ref[...], g, m_ref[...], v_ref[...])

    shard = pl.BlockSpec((tr, C), lambda i, s: (i, 0))
    slot = lambda k: pl.BlockSpec((None, tr, C), lambda i, s: (s[k], i, 0))
    out = pl.pallas_call(
        body, name=name,
        grid_spec=pltpu.PrefetchScalarGridSpec(
            num_scalar_prefetch=1, grid=(R // tr,),
            in_specs=[shard, shard, shard, slot(0), slot(1), slot(2), slot(3)] * n,
            out_specs=[shard] * (4 * n)),
        out_shape=[S((R, C), f32)] * (4 * n),
        compiler_params=_cp("parallel"),
    )(slots, *[a for w, m, v, part, got in items for a in (w, m, v, part, got, got, got)])
    return [out[4 * k:4 * k + 4] for k in range(n)]


def _adam_replicated(items, loss_terms, name):
    n = len(items)
    has_loss = loss_terms is not None

    def total(ref):
        g = ref[0]
        for d in range(1, N_DEV):
            g = g + ref[d]
        return g

    def body(*refs):
        ins, outs = refs[:4 * n + has_loss], refs[4 * n + has_loss:]
        for i in range(n):
            w_ref, m_ref, v_ref, g_ref = ins[4 * i:4 * i + 4]
            g = total(g_ref)
            outs[4 * i][...] = g
            outs[4 * i + 1][...], outs[4 * i + 2][...], outs[4 * i + 3][...] = _adamw(w_ref[...], g, m_ref[...], v_ref[...])
        if has_loss:
            outs[-1][...] = jnp.sum(total(ins[-1]), keepdims=True)

    flat = [a for item in items for a in item] + ([loss_terms] if has_loss else [])
    shapes = [S(item[0].shape, f32) for item in items for _ in range(4)] + ([S((1, 1), f32)] if has_loss else [])
    out = pl.pallas_call(body, name=name, out_shape=shapes,
                         compiler_params=pltpu.CompilerParams(vmem_limit_bytes=VMEM_LIMIT))(*flat)
    return [out[4 * i:4 * i + 4] for i in range(n)], (out[-1] if has_loss else None)


WEIGHTS = ["ffn1_norm", "ffn1_w_gate", "ffn1_w_up", "ffn1_w_down", "mix_norm", "w_in", "s5_lam_re", "s5_lam_im", "s5_log_dt",
           "s5_b_re", "s5_b_im", "s5_c_re", "s5_c_im", "s5_d", "s5_w_glu", "s5_b_glu", "conv_w_dw", "conv_b_dw", "conv_ln_g",
           "conv_ln_b", "w_out", "ffn2_norm", "ffn2_w_gate", "ffn2_w_up", "ffn2_w_down", "final_norm"]
SHARDED = ["ffn1_w_gate", "ffn1_w_up", "ffn1_w_down", "w_in", "s5_w_glu", "conv_w_dw", "w_out", "ffn2_w_gate", "ffn2_w_up",
           "ffn2_w_down"]
REPLICATED = [n for n in WEIGHTS if n not in SHARDED]
TRANSPOSED = ["ffn1_w_gate", "ffn1_w_up", "ffn2_w_gate", "ffn2_w_up", "w_in"]


def _shard_to_wire(n, w):
    if n == "conv_w_dw":
        return jnp.pad(w, ((0, CONV_HALO - CONV_K), (0, 0)))
    return w.astype(bf16)


def _to_wire(shards, ride):
    names = list(shards)
    shapes = [jax.eval_shape(functools.partial(_shard_to_wire, n), shards[n]) for n in names]

    def body(*refs):
        for src, dst in zip(refs[:len(names)], refs[len(names):]):
            (r, c), (rp, cp) = src.shape, dst.shape
            dst[:r, :c] = src[...].astype(dst.dtype)
            if cp > c:
                dst[:, c:] = jnp.zeros((rp, cp - c), dst.dtype)
            if rp > r:
                dst[r:, :] = jnp.zeros((rp - r, cp), dst.dtype)

    out = _pallas(body, ride=ride, name="to_wire", out_shape=shapes, in_specs=[pl.BlockSpec(memory_space=pltpu.VMEM)] * len(names),
                  out_specs=[pl.BlockSpec(memory_space=pltpu.VMEM)] * len(names))(*[shards[n] for n in names])
    return dict(zip(names, out))


def _gathered_to_full(n, g):
    if n == "conv_w_dw":
        return g.transpose(1, 0, 2).reshape(CONV_HALO, CONV_WIDTH)[:CONV_K]
    return g.reshape(N_DEV * g.shape[1], g.shape[2])


def _grad_to_blocks(n, g):
    if n == "conv_w_dw":
        g = jnp.pad(g, ((0, CONV_HALO - CONV_K), (0, 0)))
        g = g.reshape(g.shape[0], N_DEV, g.shape[1] // N_DEV).transpose(1, 0, 2)
    else:
        g = g.reshape(N_DEV, g.shape[0] // N_DEV, g.shape[1])
    return g.astype(bf16).reshape(N_CHIP, 2, *g.shape[1:])


REPLICATED_LATE = ["ffn1_norm"]
REPLICATED_HEAD = ["ffn2_norm", "final_norm"]
REPLICATED_MIX = ["mix_norm", "conv_b_dw", "conv_ln_g", "conv_ln_b"]
REPLICATED_S5 = [n for n in REPLICATED if n not in REPLICATED_LATE + REPLICATED_HEAD + REPLICATED_MIX]
REPLICATED_EARLY = REPLICATED_HEAD + REPLICATED_S5 + REPLICATED_MIX

PLAN = {
    "start": [("gather", ["ffn1_w_gate", "ffn1_w_up"])],
    "ffn1_up": [("gather", ["ffn1_w_down", "w_in", "w_out", "s5_w_glu", "conv_w_dw"])],
    "s5_forward": [("gather", ["ffn2_w_gate", "ffn2_w_up"])],
    "conv_fwd": [("gather", ["ffn2_w_down"])],
    "ffn2_dw_up": [("sibling", ["ffn2_w_gate"])],
    "ffn2_dw_down": [("sibling", ["ffn2_w_up"])],
    "mix_out_bwd": [("sibling", ["ffn2_w_down"]), ("replicated", REPLICATED_HEAD)],
    "s5_backward": [("chips", ["ffn2_w_gate", "ffn2_w_up"])],
    "conv_bwd_taps": [("chips", ["ffn2_w_down"]), ("replicated", REPLICATED_S5)],
    "ffn1_dw_down": [("sibling", ["w_in", "s5_w_glu", "conv_w_dw", "w_out"]), ("replicated", REPLICATED_MIX)],
    "ffn1_bwd_act": [("chips", ["w_in", "s5_w_glu", "conv_w_dw", "w_out"]), ("sibling", ["ffn1_w_down"])],
    "ffn1_dw_gate": [("chips", ["ffn1_w_down"])],
    "ffn1_dw_up": [("sibling", ["ffn1_w_gate"])],
    "ffn1_bwd_in_0": [("chips", ["ffn1_w_gate"]), ("sibling", ["ffn1_w_up"])],
    "ffn1_bwd_in_1": [("chips", ["ffn1_w_up"])],
    "tail": [("replicated", REPLICATED_LATE)],
}


class _Schedule:
    def __init__(self, wire, p, grads, core):
        self.wire, self.p, self.grads, self.core = wire, p, grads, core
        self.partial, self.reduced, self.everyone, self.pending = {}, {}, {}, []

    def before(self, point):
        assert not self.pending
        for kind, names in PLAN.get(point, ()):
            if kind == "gather":
                given = [self.wire[n] for n in names]
                ex = _gather(given)
            elif kind == "sibling":
                given = [_grad_to_blocks(n, self.grads[n]) for n in names]
                ex = _swap_with_sibling(given)
            elif kind == "chips":
                given = [self.partial.pop(n) for n in names]
                ex = _swap_with_chips(given)
            else:
                names = names + ["loss_terms"] * (names is REPLICATED_HEAD)
                given = [self.grads[n].reshape(self.p[n].shape) if n in self.p else self.grads[n] for n in names]
                ex = _gather(given)
            self.pending.append((kind, names, given, ex))
        return [ex for _, _, _, ex in self.pending]

    def after(self, point):
        for kind, names, given, ex in self.pending:
            if kind == "gather":
                for n, g in zip(names, ex.out):
                    self.p[n] = _gathered_to_full(n, g)
            elif kind == "sibling":
                if len(names) > 1:
                    sums = _add_sibling_small(list(zip(given, ex.out)), self.core, "reduce_add_" + names[0])
                else:
                    sums = [_add_sibling(given[0], ex.out[0], self.core, "reduce_add_" + names[0])]
                self.partial.update(zip(names, sums))
            elif kind == "chips":
                for n, part, got in zip(names, given, ex.out):
                    self.reduced[n] = (part, got)
            else:
                self.everyone.update(zip(names, ex.out))
        self.pending = []

    def alone(self, point):
        _exchange(self.before(point), point)
        self.after(point)


def kernel(x, ffn1_norm, ffn1_w_gate, ffn1_w_up, ffn1_w_down, mix_norm, w_in, s5_lam_re, s5_lam_im, s5_log_dt, s5_b_re, s5_b_im, s5_c_re, s5_c_im, s5_d, s5_w_glu, s5_b_glu, conv_w_dw, conv_b_dw, conv_ln_g, conv_ln_b, w_out, ffn2_norm, ffn2_w_gate, ffn2_w_up, ffn2_w_down, final_norm, loss_target, m_ffn1_norm, m_ffn1_w_gate, m_ffn1_w_up, m_ffn1_w_down, m_mix_norm, m_w_in, m_s5_lam_re, m_s5_lam_im, m_s5_log_dt, m_s5_b_re, m_s5_b_im, m_s5_c_re, m_s5_c_im, m_s5_d, m_s5_w_glu, m_s5_b_glu, m_conv_w_dw, m_conv_b_dw, m_conv_ln_g, m_conv_ln_b, m_w_out, m_ffn2_norm, m_ffn2_w_gate, m_ffn2_w_up, m_ffn2_w_down, m_final_norm, v_ffn1_norm, v_ffn1_w_gate, v_ffn1_w_up, v_ffn1_w_down, v_mix_norm, v_w_in, v_s5_lam_re, v_s5_lam_im, v_s5_log_dt, v_s5_b_re, v_s5_b_im, v_s5_c_re, v_s5_c_im, v_s5_d, v_s5_w_glu, v_s5_b_glu, v_conv_w_dw, v_conv_b_dw, v_conv_ln_g, v_conv_ln_b, v_w_out, v_ffn2_norm, v_ffn2_w_gate, v_ffn2_w_up, v_ffn2_w_down, v_final_norm):
    args = locals()
    w = {n: args[n] for n in WEIGHTS}
    m = {n: args["m_" + n] for n in WEIGHTS}
    v = {n: args["v_" + n] for n in WEIGHTS}
    xq, yq, cq = _place()
    q = 2 * xq + yq
    slots = jnp.stack([q, q ^ 1, q ^ 2, q ^ 3]).astype(jnp.int32)

    def shard2d(n, a):
        a = a.reshape(a.shape[-2:])
        return a.T if n in TRANSPOSED else a

    def view(n, a):
        if n.startswith("s5_b_") and a.ndim == 4:
            return a[0].transpose(0, 2, 1)
        return a[0] if a.ndim >= 3 else a.reshape(1, -1)

    def unview(n, a):
        return (a.transpose(0, 2, 1) if n.startswith("s5_b_") and a.ndim == 3 else a).reshape(w[n].shape)

    p = {n: view(n, w[n]) for n in REPLICATED}
    grads = {}
    first = PLAN["start"][0][1]
    wire = {n: _shard_to_wire(n, shard2d(n, w[n])) for n in first}
    sched = _Schedule(wire, p, grads, jnp.reshape(cq, (1,)).astype(jnp.int32))
    wire.update(_to_wire({n: shard2d(n, w[n]) for n in SHARDED if n not in first}, sched.before("start")))
    sched.after("start")
    _, dx = _local_step(x[0], loss_target[0], p, grads, sched)

    out = {}
    groups = [[n for n in SHARDED if n.startswith(tag)] for tag in ("ffn1", "ffn2")]
    for names in groups + [[n for n in SHARDED if not n.startswith("ffn")]]:
        def fit(n, a):
            a = shard2d(n, a)
            return jnp.pad(a, ((0, sched.reduced[n][1].shape[1] - a.shape[0]), (0, 0)))

        items = [(fit(n, w[n]), fit(n, m[n]), fit(n, v[n]), *sched.reduced[n]) for n in names]
        update = _adam_sharded if names[0].startswith("ffn") else _adam_small
        for n, res in zip(names, update(items, slots, "adam_" + names[0])):
            back = lambda r: r[:shard2d(n, w[n]).shape[0]]
            out[n] = [(back(r).T if n in TRANSPOSED else back(r)).reshape(w[n].shape) for r in res]

    for names in (REPLICATED_EARLY, REPLICATED_LATE):
        items = [(view(n, w[n]), view(n, m[n]), view(n, v[n]), sched.everyone[n]) for n in names]
        res, total = _adam_replicated(items, sched.everyone.get("loss_terms") if names is REPLICATED_EARLY else None,
                                      "adam_" + names[0])
        for n, r in zip(names, res):
            out[n] = [unview(n, a) for a in r]
        if total is not None:
            loss = total.reshape(())

    return (loss, dx.reshape(x.shape), *[out[n][0] for n in WEIGHTS], *[out[n][1] for n in WEIGHTS],
            *[out[n][2] for n in WEIGHTS], *[out[n][3] for n in WEIGHTS])
```

```python
import functools

import jax
import jax.numpy as jnp
from jax import lax
from jax.experimental import pallas as pl
from jax.experimental.pallas import tpu as pltpu

f32 = jnp.float32
bf16 = jnp.bfloat16
S = jax.ShapeDtypeStruct

N_DEV = 8
N_CHIP = 4
D_MODEL = 1024
D_FF = 2816
FF_CHUNKS = [(0, 768), (768, 1536), (1536, 2304), (2304, D_FF)]
S5_WIDTH = 512
S5_GROUPS = 32
S5_GROUP_CH = 16
S5_STATE = 64
S5_LANES = S5_GROUPS * S5_STATE
CONV_WIDTH = 512
CONV_K = 31
CONV_HALO = 32
CONV_HEAD = 64
CONV_ROWS = 32
IN_COLS = S5_WIDTH + 2 * CONV_WIDTH
SEGMENTS = 8
SCAN_LANES = 512
EPS = 1e-6
LR, B1, B2, ADAM_EPS, WD, STEP = 0.001, 0.9, 0.999, 1e-08, 0.01, 10
VMEM_LIMIT = 56 * 1024 * 1024

NN = (((1,), (0,)), ((), ()))
NT = (((1,), (1,)), ((), ()))
TN = (((0,), (0,)), ((), ()))


def _dot(a, b, dims=NN):
    return lax.dot_general(a, b, dims, preferred_element_type=f32)


def _cp(*sem):
    return pltpu.CompilerParams(dimension_semantics=sem, vmem_limit_bytes=VMEM_LIMIT)


def _rms(x, g):
    return x * lax.rsqrt(jnp.mean(x * x, axis=-1, keepdims=True) + EPS) * g


def _rms_bwd(x, g, dh):
    _, vjp = jax.vjp(_rms, x, g)
    return vjp(dh)


def _sigmoid(x):
    return 1.0 / (1.0 + jnp.exp(-x))


def _gelu(x):
    return 0.5 * x * (1.0 + jnp.tanh(0.7978845608028654 * (x + 0.044715 * x * x * x)))


def _rows8(x):
    t, c = x.shape
    return x.reshape(t // 8, 8, c).sum(axis=0)


def _full(shape):
    return pl.BlockSpec(shape, lambda *_: (0,) * len(shape))


def _resident(shape):
    return pl.BlockSpec(shape, lambda *_: (0,) * len(shape), pipeline_mode=pl.Buffered(1))


def _ffn_up(x, g, wg, wu, tm, tag, mixed=None, ride=()):
    L = x.shape[0]

    def body(x_ref, g_ref, wg_ref, wu_ref, *rest):
        h_ref, dadg_ref, dadu_ref, a_ref = rest[-5:-1] if mixed else rest[-4:]
        x = x_ref[...]
        if mixed:
            ys_ref, yc_ref, wo_ref = rest[:3]
            x = x + _dot(ys_ref[...], wo_ref[:S5_WIDTH, :]) + _dot(yc_ref[...], wo_ref[S5_WIDTH:, :])
            rest[-1][...] = x
        h = _rms(x, g_ref[...]).astype(bf16)
        h_ref[...] = h
        for lo, hi in FF_CHUNKS:
            cols = slice(lo, hi)
            gate =_dot(h, wg_ref[cols, :], NT)
            up = _dot(h, wu_ref[cols, :], NT)
            sig = _sigmoid(gate)
            silu = gate * sig
            dadg_ref[:, cols] = (up * (sig + silu * (1.0 - sig))).astype(bf16)
            dadu_ref[:, cols] = silu.astype(bf16)
            a_ref[:, cols] = (silu * up).astype(bf16)

    row = pl.BlockSpec((tm, D_MODEL), lambda i: (i, 0))
    wide = pl.BlockSpec((tm, D_FF), lambda i: (i, 0))
    half = pl.BlockSpec((tm, S5_WIDTH), lambda i: (i, 0))
    return _pallas(
        body, ride=ride, name=tag + "_up", grid=(L // tm,),
        in_specs=[row, _full((1, D_MODEL)), _resident((D_FF, D_MODEL)), _resident((D_FF, D_MODEL))]
        + ([half, half, _resident((D_MODEL, D_MODEL))] if mixed else []),
        out_specs=[row, wide, wide, wide] + [row] * bool(mixed),
        out_shape=[S((L, D_MODEL), bf16)] + [S((L, D_FF), bf16)] * 3 + [S((L, D_MODEL), f32)] * bool(mixed),
        compiler_params=_cp("parallel"),
    )(x, g, wg, wu, *(mixed or ()))


def _ffn_down(x, a, wd, tm, tag, mixer=None, ride=()):
    L = x.shape[0]

    def body(x_ref, a_ref, wd_ref, *rest):
        xo = x_ref[...] + 0.5 * _dot(a_ref[...], wd_ref[...])
        if not mixer:
            rest[0][...] = xo
            return
        g_ref, w_ref, o_ref, h_ref, us_ref, v_ref = rest
        o_ref[...] = xo
        h = _rms(xo, g_ref[...]).astype(bf16)
        h_ref[...] = h
        u = _dot(h, w_ref[...], NT)
        us_ref[...] = u[:, :S5_WIDTH]
        v_ref[...] = u[:, S5_WIDTH:]

    row = lambda c: pl.BlockSpec((tm, c), lambda i: (i, 0))
    extra_in = [_full((1, D_MODEL)), _resident((IN_COLS, D_MODEL))] if mixer else []
    extra_out = [(D_MODEL, bf16), (S5_WIDTH, f32), (2 * CONV_WIDTH, f32)] if mixer else []
    out = _pallas(
        body, ride=ride, name=tag + "_down", grid=(L // tm,),
        in_specs=[row(D_MODEL), row(D_FF), _resident((D_FF, D_MODEL))] + extra_in,
        out_specs=[row(D_MODEL)] + [row(c) for c, _ in extra_out],
        out_shape=[S((L, D_MODEL), f32)] + [S((L, c), t) for c, t in extra_out],
        compiler_params=_cp("parallel"),
    )(x, a, wd, *(mixer or ()))
    return out if mixer else out[0]


def _ffn_down_loss(x, a, wd, target, g, tm, tag):
    L = x.shape[0]

    def body(x_ref, a_ref, wd_ref, t_ref, g_ref, dx_ref, dg_ref, l_ref):
        @pl.when(pl.program_id(0) == 0)
        def _():
            dg_ref[...] = jnp.zeros_like(dg_ref)
            l_ref[...] = jnp.zeros_like(l_ref)

        xo = x_ref[...] + 0.5 * _dot(a_ref[...], wd_ref[...])
        g = g_ref[...]
        e = _rms(xo, g) - t_ref[...]
        l_ref[...] += _rows8(e * e) * (0.5 / D_MODEL)
        dx, dg = _rms_bwd(xo, g, e * (1.0 / D_MODEL))
        dx_ref[...] = dx
        dg_ref[...] += dg

    row = pl.BlockSpec((tm, D_MODEL), lambda i: (i, 0))
    return pl.pallas_call(
        body, name=tag + "_down_loss", grid=(L // tm,),
        in_specs=[row, pl.BlockSpec((tm, D_FF), lambda i: (i, 0)), _resident((D_FF, D_MODEL)), row, _full((1, D_MODEL))],
        out_specs=[row, _full((1, D_MODEL)), _full((8, D_MODEL))],
        out_shape=[S((L, D_MODEL), f32), S((1, D_MODEL), f32), S((8, D_MODEL), f32)],
        compiler_params=_cp("arbitrary"),
    )(x, a, wd, target, g)


def _ffn_bwd_act(dxo, wd, dadg, dadu, tm, tag, ride=()):
    L = dxo.shape[0]

    def body(dx_ref, wd_ref, dadg_ref, dadu_ref, dgate_ref, dup_ref, dxh_ref):
        dxh = (0.5 * dx_ref[...]).astype(bf16)
        dxh_ref[...] = dxh
        for lo, hi in FF_CHUNKS:
            cols = slice(lo, hi)
            da =_dot(dxh, wd_ref[cols, :], NT)
            dgate_ref[:, cols] = (da * dadg_ref[:, cols].astype(f32)).astype(bf16)
            dup_ref[:, cols] = (da * dadu_ref[:, cols].astype(f32)).astype(bf16)

    row = pl.BlockSpec((tm, D_MODEL), lambda i: (i, 0))
    wide = pl.BlockSpec((tm, D_FF), lambda i: (i, 0))
    return _pallas(
        body, ride=ride, name=tag + "_bwd_act", grid=(L // tm,),
        in_specs=[row, _resident((D_FF, D_MODEL)), wide, wide],
        out_specs=[wide, wide, row],
        out_shape=[S((L, D_FF), bf16), S((L, D_FF), bf16), S((L, D_MODEL), bf16)],
        compiler_params=_cp("parallel"),
    )(dxo, wd, dadg, dadu)


def _ffn_bwd_in(dxo, x, g, dgate, dup, wg, wu, tm, name, tiles=None, into=None, ride=()):
    L = x.shape[0]
    first, count = tiles or (0, L // tm)

    def body(dxo_ref, x_ref, g_ref, dgate_ref, dup_ref, wg_ref, wu_ref, *rest):
        dx_ref, dg_ref = rest[-2:]

        @pl.when(pl.program_id(0) == 0)
        def _():
            dg_ref[...] = jnp.zeros_like(dg_ref)

        dh = _dot(dgate_ref[...], wg_ref[...]) + _dot(dup_ref[...], wu_ref[...])
        dx, dg = _rms_bwd(x_ref[...], g_ref[...], dh)
        dx_ref[...] = dxo_ref[...] + dx
        dg_ref[...] += dg

    row = pl.BlockSpec((tm, D_MODEL), lambda i: (first + i, 0))
    wide = pl.BlockSpec((tm, D_FF), lambda i: (first + i, 0))
    return _pallas(
        body, ride=ride, name=name, grid=(count,),
        in_specs=[row, row, _full((1, D_MODEL)), wide, wide, _resident((D_FF, D_MODEL)), _resident((D_FF, D_MODEL))]
        + [ANY] * (into is not None),
        out_specs=[row, _full((1, D_MODEL))],
        out_shape=[S((L, D_MODEL), f32), S((1, D_MODEL), f32)],
        input_output_aliases={7: 0} if into is not None else {},
        compiler_params=_cp("arbitrary"),
    )(dxo, x, g, dgate, dup, wg, wu, *([into] if into is not None else []))


def _mm_tn(a, b, out_dtype, name, tm=512, tn=1024, ride=()):
    L, M = a.shape
    N = b.shape[1]
    tm, tn = min(tm, M), min(tn, N)
    while M % tm:
        tm //= 2
    while N % tn:
        tn //= 2

    def body(a_ref, b_ref, o_ref):
        o_ref[...] = _dot(a_ref[...].astype(bf16), b_ref[...].astype(bf16), TN).astype(out_dtype)

    return _pallas(
        body, ride=ride, name=name, grid=(M // tm, N // tn),
        in_specs=[pl.BlockSpec((L, tm), lambda i, j: (0, i)), pl.BlockSpec((L, tn), lambda i, j: (0, j))],
        out_specs=pl.BlockSpec((tm, tn), lambda i, j: (i, j)),
        out_shape=S((M, N), out_dtype),
        compiler_params=_cp("parallel", "parallel"),
    )(a, b)


def _mix_in_bwd(dxo, x, g, du_s5, dv, w_in, tm, ride=()):
    L = x.shape[0]

    def body(dxo_ref, x_ref, g_ref, dus_ref, dv_ref, w_ref, dx_ref, dg_ref, dub_ref, dxh_ref):
        @pl.when(pl.program_id(0) == 0)
        def _():
            dg_ref[...] = jnp.zeros_like(dg_ref)

        dus = dus_ref[...].astype(bf16)
        dvb = dv_ref[...].astype(bf16)
        dub_ref[:, :S5_WIDTH] = dus
        dub_ref[:, S5_WIDTH:] = dvb
        dh = _dot(dus, w_ref[:S5_WIDTH, :]) + _dot(dvb, w_ref[S5_WIDTH:, :])
        dx, dg = _rms_bwd(x_ref[...], g_ref[...], dh)
        dx = dxo_ref[...] + dx
        dx_ref[...] = dx
        dxh_ref[...] = (0.5 * dx).astype(bf16)
        dg_ref[...] += dg

    row = lambda c: pl.BlockSpec((tm, c), lambda i: (i, 0))
    return _pallas(
        body, ride=ride, name="mix_in_bwd", grid=(L // tm,),
        in_specs=[row(D_MODEL), row(D_MODEL), _full((1, D_MODEL)), row(S5_WIDTH), row(2 * CONV_WIDTH),
                  _full((IN_COLS, D_MODEL))],
        out_specs=[row(D_MODEL), _full((1, D_MODEL)), row(IN_COLS), row(D_MODEL)],
        out_shape=[S((L, D_MODEL), f32), S((1, D_MODEL), f32), S((L, IN_COLS), bf16), S((L, D_MODEL), bf16)],
        compiler_params=_cp("arbitrary"),
    )(dxo, x, g, du_s5, dv, w_in)


def _dw_out(y_s5, y_conv, dxb, tn=512):
    L = dxb.shape[0]

    def body(ys_ref, yc_ref, b_ref, o_ref):
        b = b_ref[...]
        o_ref[:S5_WIDTH, :] = _dot(ys_ref[...], b, TN).astype(bf16)
        o_ref[S5_WIDTH:, :] = _dot(yc_ref[...], b, TN).astype(bf16)

    return pl.pallas_call(
        body, name="dw_out", grid=(D_MODEL // tn,),
        in_specs=[_full((L, S5_WIDTH)), _full((L, CONV_WIDTH)), pl.BlockSpec((L, tn), lambda j: (0, j))],
        out_specs=pl.BlockSpec((S5_WIDTH + CONV_WIDTH, tn), lambda j: (0, j)),
        out_shape=S((S5_WIDTH + CONV_WIDTH, D_MODEL), bf16),
        compiler_params=_cp("parallel"),
    )(y_s5, y_conv, dxb)


def _mix_out_bwd(dx, w_out, tm, ride=()):
    L = dx.shape[0]

    def body(dx_ref, w_ref, dys_ref, dyc_ref, dxb_ref):
        dxb = dx_ref[...].astype(bf16)
        dxb_ref[...] = dxb
        dys_ref[...] = _dot(dxb, w_ref[:S5_WIDTH, :], NT)
        dyc_ref[...] = _dot(dxb, w_ref[S5_WIDTH:, :], NT)

    row = lambda c: pl.BlockSpec((tm, c), lambda i: (i, 0))
    return _pallas(
        body, ride=ride, name="mix_out_bwd", grid=(L // tm,),
        in_specs=[row(D_MODEL), _full((D_MODEL, D_MODEL))],
        out_specs=[row(S5_WIDTH), row(CONV_WIDTH), row(D_MODEL)],
        out_shape=[S((L, S5_WIDTH), f32), S((L, CONV_WIDTH), f32), S((L, D_MODEL), bf16)],
        compiler_params=_cp("parallel"),
    )(dx, w_out)


def _s5_discretise(lam_re, lam_im, log_dt, b_re, b_im):
    dt = jnp.exp(log_dt)
    mag = jnp.exp(lam_re * dt)
    abar_re = mag * jnp.cos(lam_im * dt)
    abar_im = mag * jnp.sin(lam_im * dt)
    den = lam_re * lam_re + lam_im * lam_im
    num_re = abar_re - 1.0
    f_re = ((num_re * lam_re + abar_im * lam_im) / den)[:, None, :]
    f_im = ((abar_im * lam_re - num_re * lam_im) / den)[:, None, :]
    return abar_re, abar_im, f_re * b_re - f_im * b_im, f_re * b_im + f_im * b_re


def _s5_params(lam_re, lam_im, log_dt, b_re, b_im):
    def body(lr, li, ld, br, bi, ar_ref, ai_ref, bbr_ref, bbi_ref):
        ar, ai, bbr, bbi = _s5_discretise(lr[...], li[...], ld[...], br[...], bi[...])
        ar_ref[...], ai_ref[...], bbr_ref[...], bbi_ref[...] = ar, ai, bbr, bbi

    gp = S((S5_GROUPS, S5_STATE), f32)
    gcp = S((S5_GROUPS, S5_GROUP_CH, S5_STATE), f32)
    return pl.pallas_call(body, name="s5_params", out_shape=[gp, gp, gcp, gcp])(lam_re, lam_im, log_dt, b_re, b_im)


def _s5_params_bwd(lam_re, lam_im, log_dt, b_re, b_im, d_ar, d_ai, d_bbr, d_bbi):
    def body(lr, li, ld, br, bi, car, cai, cbr, cbi, o_lr, o_li, o_ld, o_br, o_bi):
        _, vjp = jax.vjp(_s5_discretise, lr[...], li[...], ld[...], br[...], bi[...])
        o_lr[...], o_li[...], o_ld[...], o_br[...], o_bi[...] = vjp((car[...], cai[...], cbr[...], cbi[...]))

    gp = S((S5_GROUPS, S5_STATE), f32)
    gcp = S((S5_GROUPS, S5_GROUP_CH, S5_STATE), f32)
    return pl.pallas_call(body, name="s5_params_bwd", out_shape=[gp, gp, S((S5_GROUPS, 1), f32), gcp, gcp])(
        lam_re, lam_im, log_dt, b_re, b_im, d_ar, d_ai, d_bbr, d_bbi)


def _cmul(ar, ai, br, bi):
    return ar * br - ai * bi, ar * bi + ai * br


def _segment_starts(er, ei, ar, ai, steps, reverse):
    pr, pi = ar, ai
    n = 1
    while n < steps:
        pr, pi = _cmul(pr, pi, pr, pi)
        n *= 2
    assert n == steps
    row = lax.broadcasted_iota(jnp.int32, (SEGMENTS, SCAN_LANES), 0)
    hr = jnp.zeros((1, SCAN_LANES), f32)
    hi = jnp.zeros((1, SCAN_LANES), f32)
    out_r = jnp.zeros((SEGMENTS, SCAN_LANES), f32)
    out_i = jnp.zeros((SEGMENTS, SCAN_LANES), f32)
    order = range(SEGMENTS - 1, 0, -1) if reverse else range(0, SEGMENTS - 1)
    for r in order:
        qr, qi = _cmul(pr, pi, hr, hi)
        hr, hi = qr + er[r:r + 1, :], qi + ei[r:r + 1, :]
        nxt = r - 1 if reverse else r + 1
        out_r = jnp.where(row == nxt, hr, out_r)
        out_i = jnp.where(row == nxt, hi, out_i)
    return out_r, out_i


def _s5_read_bwd(dout, y_lin, u, d_skip, w_glu, b_glu, tm):
    L = u.shape[0]

    def body(do_ref, yl_ref, u_ref, d_ref, w_ref, b_ref, dyl_ref, du_ref, dd_ref, dw_ref, db_ref):
        @pl.when(pl.program_id(0) == 0)
        def _():
            dd_ref[...] = jnp.zeros_like(dd_ref)
            dw_ref[...] = jnp.zeros_like(dw_ref)
            db_ref[...] = jnp.zeros_like(db_ref)

        u, d, dout = u_ref[...], d_ref[...], do_ref[...]
        y, gelu_vjp = jax.vjp(_gelu, yl_ref[...] + d * u)
        yb = y.astype(bf16)
        sig = _sigmoid(_dot(yb, w_ref[...]) + b_ref[...])
        dz = dout * y * sig * (1.0 - sig)
        dzb = dz.astype(bf16)
        dy = dout * sig + _dot(dzb, w_ref[...], NT)
        (dyp,) = gelu_vjp(dy)
        dyl_ref[...] = dyp.astype(bf16)
        du_ref[...] = d * dyp
        dd_ref[...] += _rows8(dyp * u)
        db_ref[...] += _rows8(dz)
        dw_ref[...] += _dot(yb, dzb, TN)

    row = pl.BlockSpec((tm, S5_WIDTH), lambda i: (i, 0))
    vec = _full((1, S5_WIDTH))
    part = _full((8, S5_WIDTH))
    return pl.pallas_call(
        body, name="s5_read_bwd", grid=(L // tm,),
        in_specs=[row, row, row, vec, _full((S5_WIDTH, S5_WIDTH)), vec],
        out_specs=[row, row, part, _full((S5_WIDTH, S5_WIDTH)), part],
        out_shape=[S((L, S5_WIDTH), bf16), S((L, S5_WIDTH), f32), S((8, S5_WIDTH), f32),
                   S((S5_WIDTH, S5_WIDTH), f32), S((8, S5_WIDTH), f32)],
        compiler_params=_cp("arbitrary"),
    )(dout, y_lin, u, d_skip, w_glu, b_glu)


S5_CHUNK_CH = SCAN_LANES // S5_STATE * S5_GROUP_CH


def _s5_two_phase(L, bi):
    rows = bi * SEGMENTS
    nb = L // rows
    whole = pltpu.VMEM((L // SEGMENTS, SEGMENTS, SCAN_LANES), f32)
    mat = pl.BlockSpec((S5_CHUNK_CH, SCAN_LANES), lambda c, j: (c, c))
    vec = pl.BlockSpec((1, SCAN_LANES), lambda c, j: (0, c))
    tile = pl.BlockSpec((SEGMENTS, SCAN_LANES), lambda c, j: (0, c))
    return rows, nb, whole, mat, vec, tile


def _s5_forward(u, a_re, a_im, bb_re, bb_im, cc_re, cc_im, bi, ride=()):
    L = u.shape[0]
    rows, nb, whole, mat, vec, _ = _s5_two_phase(L, bi)

    def body(u_ref, ar_ref, ai_ref, br_ref, bi_ref, cr_ref, ci_ref, sr_ref, si_ref, yl_ref, hr_ref, hi_ref, dr_ref, di_ref):
        j = pl.program_id(1)
        ar = jnp.broadcast_to(ar_ref[...], (SEGMENTS, SCAN_LANES))
        ai = jnp.broadcast_to(ai_ref[...], (SEGMENTS, SCAN_LANES))

        @pl.when(j == 0)
        def _():
            hr_ref[...] = jnp.zeros_like(hr_ref)
            hi_ref[...] = jnp.zeros_like(hi_ref)

        @pl.when(j < nb)
        def _():
            base = j * bi
            ub = u_ref[...].astype(bf16)
            dr_ref[pl.ds(base, bi)] = _dot(ub, br_ref[...]).reshape(bi, SEGMENTS, SCAN_LANES)
            di_ref[pl.ds(base, bi)] = _dot(ub, bi_ref[...]).reshape(bi, SEGMENTS, SCAN_LANES)

            def step(i, c):
                pr, pi = _cmul(ar, ai, c[0], c[1])
                return pr + dr_ref[base + i], pi + di_ref[base + i]

            hr_ref[...], hi_ref[...] = lax.fori_loop(0, bi, step, (hr_ref[...], hi_ref[...]), unroll=True)

        @pl.when(j == nb - 1)
        def _():
            hr_ref[...], hi_ref[...] = _segment_starts(hr_ref[...], hi_ref[...], ar_ref[...], ai_ref[...], L // SEGMENTS, False)

        @pl.when(j >= nb)
        def _():
            base = (j - nb) * bi

            def step(i, c):
                pr, pi = _cmul(ar, ai, c[0], c[1])
                nr, nim = pr + dr_ref[base + i], pi + di_ref[base + i]
                dr_ref[base + i] = nr
                di_ref[base + i] = nim
                return nr, nim

            hr_ref[...], hi_ref[...] = lax.fori_loop(0, bi, step, (hr_ref[...], hi_ref[...]), unroll=True)
            sr = dr_ref[pl.ds(base, bi)].reshape(rows, SCAN_LANES).astype(bf16)
            si = di_ref[pl.ds(base, bi)].reshape(rows, SCAN_LANES).astype(bf16)
            sr_ref[...] = sr
            si_ref[...] = si
            yl_ref[...] = _dot(sr, cr_ref[...], NT) - _dot(si, ci_ref[...], NT)

    u_spec = pl.BlockSpec((rows, S5_CHUNK_CH), lambda c, j: (jnp.minimum(j, nb - 1), c))
    late = lambda width: pl.BlockSpec((rows, width), lambda c, j: (jnp.maximum(j - nb, 0), c))
    return _pallas(
        body, ride=ride, name="s5_forward", grid=(S5_LANES // SCAN_LANES, 2 * nb),
        in_specs=[u_spec, vec, vec, mat, mat, mat, mat],
        out_specs=[late(SCAN_LANES), late(SCAN_LANES), late(S5_CHUNK_CH)],
        out_shape=[S((L, S5_LANES), bf16)] * 2 + [S((L, S5_WIDTH), f32)],
        scratch_shapes=[pltpu.VMEM((SEGMENTS, SCAN_LANES), f32)] * 2 + [whole] * 2,
        compiler_params=_cp("parallel", "arbitrary"),
    )(u, a_re, a_im, bb_re, bb_im, cc_re, cc_im)


def _s5_backward(dy, u, du_skip, s_re, s_im, a_re, a_im, bb_re, bb_im, cc_re, cc_im, bi, ride=()):
    L = u.shape[0]
    rows, nb, whole, mat, vec, tile = _s5_two_phase(L, bi)
    per = rows // 16

    def body(dy_ref, u_ref, dus_ref, sr_ref, si_ref, pr_ref, pi_ref, lr_ref, li_ref, ar_ref, ai_ref, br_ref, bi_ref, cr_ref,
             ci_ref, du_ref, dar_ref, dai_ref, dbr_ref, dbi_ref, dcr_ref, dci_ref, hr_ref, hi_ref, gr_ref, gi_ref, fr_ref, fi_ref):
        j = pl.program_id(1)
        ar = jnp.broadcast_to(ar_ref[...], (SEGMENTS, SCAN_LANES))
        ai = jnp.broadcast_to(ai_ref[...], (SEGMENTS, SCAN_LANES))

        @pl.when(j == 0)
        def _():
            for ref in (hr_ref, hi_ref, dar_ref, dai_ref, dbr_ref, dbi_ref, dcr_ref, dci_ref):
                ref[...] = jnp.zeros_like(ref)

        @pl.when(j < nb)
        def _():
            base = (nb - 1 - j) * bi
            dy = dy_ref[...]
            gr_ref[pl.ds(base, bi)] = _dot(dy, cr_ref[...]).reshape(bi, SEGMENTS, SCAN_LANES)
            gi_ref[pl.ds(base, bi)] = (-_dot(dy, ci_ref[...])).reshape(bi, SEGMENTS, SCAN_LANES)

            def step(n, c):
                i = base + bi - 1 - n
                qr, qi = _cmul(ar, ai, c[0], c[1])
                return qr + gr_ref[i], qi + gi_ref[i]

            hr_ref[...], hi_ref[...] = lax.fori_loop(0, bi, step, (hr_ref[...], hi_ref[...]), unroll=True)

        @pl.when(j == nb - 1)
        def _():
            hr_ref[...], hi_ref[...] = _segment_starts(hr_ref[...], hi_ref[...], ar_ref[...], ai_ref[...], L // SEGMENTS, True)

        @pl.when(j >= nb)
        def _():
            blk = 2 * nb - 1 - j
            base = blk * bi
            sr, si = sr_ref[...], si_ref[...]
            fr_ref[...] = sr.astype(f32).reshape(bi, SEGMENTS, SCAN_LANES)
            fi_ref[...] = si.astype(f32).reshape(bi, SEGMENTS, SCAN_LANES)

            def step(n, c):
                i = bi - 1 - n
                gr, gi, accr, acci = c
                qr, qi = _cmul(ar, ai, gr, gi)
                gr, gi = qr + gr_ref[base + i], qi + gi_ref[base + i]
                gr_ref[base + i] = gr
                gi_ref[base + i] = gi
                pr, pi = fr_ref[i - 1], fi_ref[i - 1]
                return gr, gi, accr + (gr * pr + gi * pi), acci + (gi * pr - gr * pi)

            gr, gi, accr, acci = lax.fori_loop(0, bi - 1, step, (hr_ref[...], hi_ref[...], dar_ref[...], dai_ref[...]), unroll=True)
            qr, qi = _cmul(ar, ai, gr, gi)
            gr, gi = qr + gr_ref[base], qi + gi_ref[base]
            gr_ref[base] = gr
            gi_ref[base] = gi
            hr_ref[...], hi_ref[...] = gr, gi
            row = lax.broadcasted_iota(jnp.int32, (SEGMENTS, SCAN_LANES), 0)
            older = lambda ref: ref[...].astype(f32)[SEGMENTS:, :]
            wrap_r = jnp.where(row == 0, 0.0, pltpu.roll(older(lr_ref), 1, 0))
            wrap_i = jnp.where(row == 0, 0.0, pltpu.roll(older(li_ref), 1, 0))
            pr = jnp.where(blk == 0, wrap_r, older(pr_ref))
            pi = jnp.where(blk == 0, wrap_i, older(pi_ref))
            dar_ref[...] = accr + gr * pr + gi * pi
            dai_ref[...] = acci + gi * pr - gr * pi

            g_re = gr_ref[pl.ds(base, bi)].reshape(rows, SCAN_LANES).astype(bf16)
            g_im = gi_ref[pl.ds(base, bi)].reshape(rows, SCAN_LANES).astype(bf16)
            ub = u_ref[...].astype(bf16)
            dy = dy_ref[...]
            du_ref[...] = dus_ref[...] + _dot(g_re, br_ref[...], NT) + _dot(g_im, bi_ref[...], NT)
            dbr_ref[...] += _dot(ub, g_re, TN)
            dbi_ref[...] += _dot(ub, g_im, TN)
            dcr_ref[...] += _dot(dy, sr, TN)
            dci_ref[...] -= _dot(dy, si, TN)

    block = lambda c, j: jnp.where(j < nb, nb - 1 - j, 2 * nb - 1 - j)
    late_block = lambda c, j: jnp.minimum(2 * nb - 1 - j, nb - 1)
    both = pl.BlockSpec((rows, S5_CHUNK_CH), lambda c, j: (block(c, j), c))
    chan = pl.BlockSpec((rows, S5_CHUNK_CH), lambda c, j: (late_block(c, j), c))
    state = pl.BlockSpec((rows, SCAN_LANES), lambda c, j: (late_block(c, j), c))
    prev = pl.BlockSpec((16, SCAN_LANES), lambda c, j: (jnp.maximum(late_block(c, j) * per - 1, 0), c))
    last = pl.BlockSpec((16, SCAN_LANES), lambda c, j: (L // 16 - 1, c))
    grad = pl.BlockSpec((S5_CHUNK_CH, SCAN_LANES), lambda c, j: (c, 0))
    return _pallas(
        body, ride=ride, name="s5_backward", grid=(S5_LANES // SCAN_LANES, 2 * nb),
        in_specs=[both, chan, chan, state, state, prev, prev, last, last, vec, vec, mat, mat, mat, mat],
        out_specs=[chan, tile, tile, grad, grad, grad, grad],
        out_shape=[S((L, S5_WIDTH), f32)] + [S((SEGMENTS, S5_LANES), f32)] * 2 + [S((S5_WIDTH, SCAN_LANES), f32)] * 4,
        scratch_shapes=[pltpu.VMEM((SEGMENTS, SCAN_LANES), f32)] * 2 + [whole] * 2 + [pltpu.VMEM((bi, SEGMENTS, SCAN_LANES), f32)] * 2,
        compiler_params=_cp("parallel", "arbitrary"),
    )(dy, u, du_skip, s_re, s_im, s_re, s_im, s_re, s_im, a_re, a_im, bb_re, bb_im, cc_re, cc_im)


def _s5_gate(y_lin, u, d_skip, w_glu, b_glu, tm, ride=()):
    L = u.shape[0]

    def body(yl_ref, u_ref, d_ref, w_ref, b_ref, o_ref):
        y = _gelu(yl_ref[...] + d_ref[...] * u_ref[...])
        z = _dot(y.astype(bf16), w_ref[...]) + b_ref[...]
        o_ref[...] = (y * _sigmoid(z)).astype(bf16)

    row = pl.BlockSpec((tm, S5_WIDTH), lambda i: (i, 0))
    vec = _full((1, S5_WIDTH))
    return _pallas(
        body, ride=ride, name="s5_gate", grid=(L // tm,),
        in_specs=[row, row, vec, _full((S5_WIDTH, S5_WIDTH)), vec],
        out_specs=row, out_shape=S((L, S5_WIDTH), bf16),
        compiler_params=_cp("parallel"),
    )(y_lin, u, d_skip, w_glu, b_glu)


def _group_mean(x, avg):
    return _dot(x.astype(bf16), avg)


def _conv_act(zn, ln_g, ln_b):
    t = zn * ln_g + ln_b
    return t * _sigmoid(t)


def _glu_padded(v_ref, halo_ref, zpad_ref, tm):
    v = v_ref[...]
    vh = halo_ref[...]
    zh = vh[:, :CONV_WIDTH] * _sigmoid(vh[:, CONV_WIDTH:])
    zpad_ref[:CONV_HALO, :] = jnp.where(pl.program_id(0) > 0, zh, 0.0)
    zpad_ref[CONV_HALO:CONV_HALO + tm, :] = v[:, :CONV_WIDTH] * _sigmoid(v[:, CONV_WIDTH:])
    zpad_ref[CONV_HALO + tm:, :] = jnp.zeros((8, CONV_WIDTH), f32)


def _shifted(pad_ref, sh_ref, tm):
    for b in range(8):
        sh_ref[b] = pad_ref[pl.ds(b, tm + CONV_HALO), :]


def _window(sh_ref, r0, off, rows):
    start = r0 + 8 * (off // 8)
    return sh_ref[off % 8, pl.ds(start if isinstance(start, int) else pl.multiple_of(start, 8), rows), :]


def _tap_sum(w_ref, sh_ref, taps, out_ref, tm, bias):
    for r0 in range(0, tm, CONV_ROWS):
        acc = jnp.zeros((CONV_ROWS, CONV_WIDTH), f32) + bias
        for k, off in taps:
            acc = acc + w_ref[k:k + 1, :] * _window(sh_ref, r0, off, CONV_ROWS)
        out_ref[r0:r0 + CONV_ROWS, :] = acc


FWD_TAPS = [(k, CONV_HALO - (CONV_K - 1) + k) for k in range(CONV_K)]
BWD_TAPS = [(k, CONV_K - 1 - k) for k in range(CONV_K)]


def _conv_specs(tm):
    per = tm // CONV_HALO
    vrow = pl.BlockSpec((tm, 2 * CONV_WIDTH), lambda i: (i, 0))
    vhalo = pl.BlockSpec((CONV_HALO, 2 * CONV_WIDTH), lambda i: (jnp.maximum(i * per - 1, 0), 0))
    return vrow, vhalo


def _conv_scratch(tm):
    return [pltpu.VMEM((tm + CONV_HALO + 8, CONV_WIDTH), f32), pltpu.VMEM((8, tm + CONV_HALO, CONV_WIDTH), f32)]


def _conv_fwd(v, w_dw, b_dw, ln_g, ln_b, avg, tm, ride=()):
    L = v.shape[0]

    def body(v_ref, halo_ref, w_ref, b_ref, g_ref, bb_ref, avg_ref, o_ref, zc_ref, zpad_ref, zs_ref):
        _glu_padded(v_ref, halo_ref, zpad_ref, tm)
        _shifted(zpad_ref, zs_ref, tm)
        _tap_sum(w_ref, zs_ref, FWD_TAPS, zc_ref, tm, b_ref[...])
        zc = zc_ref[...]
        xc = zc - _group_mean(zc, avg_ref[...])
        zn = xc * lax.rsqrt(_group_mean(xc * xc, avg_ref[...]) + EPS)
        o_ref[...] = _conv_act(zn, g_ref[...], bb_ref[...]).astype(bf16)

    vrow, vhalo = _conv_specs(tm)
    vec = _full((1, CONV_WIDTH))
    row = pl.BlockSpec((tm, CONV_WIDTH), lambda i: (i, 0))
    return _pallas(
        body, ride=ride, name="conv_fwd", grid=(L // tm,),
        in_specs=[vrow, vhalo, _full((CONV_HALO, CONV_WIDTH)), vec, vec, vec, _full((CONV_WIDTH, CONV_WIDTH))],
        out_specs=[row, row], out_shape=[S((L, CONV_WIDTH), bf16), S((L, CONV_WIDTH), f32)],
        scratch_shapes=_conv_scratch(tm),
        compiler_params=_cp("arbitrary"),
    )(v, v, w_dw, b_dw, ln_g, ln_b, avg)


def _conv_bwd_norm(dout, zc, ln_g, ln_b, avg, tm):
    L = zc.shape[0]

    def body(do_ref, zc_ref, g_ref, bb_ref, avg_ref, dzc_ref, dg_ref, db_ref, dbd_ref):
        @pl.when(pl.program_id(0) == 0)
        def _():
            dg_ref[...] = jnp.zeros_like(dg_ref)
            db_ref[...] = jnp.zeros_like(db_ref)
            dbd_ref[...] = jnp.zeros_like(dbd_ref)

        avg = avg_ref[...]
        zc = zc_ref[...]
        xc = zc - _group_mean(zc, avg)
        rstd = lax.rsqrt(_group_mean(xc * xc, avg) + EPS)
        xhat = xc * rstd
        _, act_vjp = jax.vjp(_conv_act, xhat, g_ref[...], bb_ref[...])
        dxhat, dg, db = act_vjp(do_ref[...])
        dzc = rstd * (dxhat - _group_mean(dxhat, avg) - xhat * _group_mean(dxhat * xhat, avg))
        dzc_ref[...] = dzc
        dg_ref[0:1, :] += dg
        db_ref[0:1, :] += db
        dbd_ref[...] += _rows8(dzc)

    vec = _full((1, CONV_WIDTH))
    row = pl.BlockSpec((tm, CONV_WIDTH), lambda i: (i, 0))
    part = _full((8, CONV_WIDTH))
    return pl.pallas_call(
        body, name="conv_bwd_norm", grid=(L // tm,),
        in_specs=[row, row, vec, vec, _full((CONV_WIDTH, CONV_WIDTH))],
        out_specs=[row, part, part, part],
        out_shape=[S((L, CONV_WIDTH), f32)] + [S((8, CONV_WIDTH), f32)] * 3,
        compiler_params=_cp("arbitrary"),
    )(dout, zc, ln_g, ln_b, avg)


def _conv_bwd_taps(dzc, v, w_dw, tm, ride=()):
    L = v.shape[0]
    nt = L // tm
    per = tm // CONV_HALO

    def body(d_ref, dn_ref, v_ref, w_ref, dv_ref, dw_ref, dpad_ref, ds_ref, dz_ref, z_ref):
        i = pl.program_id(0)

        @pl.when(i == 0)
        def _():
            dw_ref[...] = jnp.zeros_like(dw_ref)

        v = v_ref[...]
        sig = _sigmoid(v[:, CONV_WIDTH:])
        z_ref[...] = v[:, :CONV_WIDTH] * sig
        dpad_ref[:tm, :] = d_ref[...]
        dpad_ref[tm:tm + CONV_HALO, :] = jnp.where(i < nt - 1, dn_ref[...], 0.0)
        dpad_ref[tm + CONV_HALO:, :] = jnp.zeros((8, CONV_WIDTH), f32)
        _shifted(dpad_ref, ds_ref, tm)
        _tap_sum(w_ref, ds_ref, BWD_TAPS, dz_ref, tm, 0.0)

        for first in range(0, CONV_K, 8):
            taps = BWD_TAPS[first:first + 8]

            accs = [jnp.zeros((8, CONV_WIDTH), f32) for _ in taps]
            for r0 in range(0, tm, 8):
                z = z_ref[r0:r0 + 8, :]
                accs = [acc + z * _window(ds_ref, r0, off, 8) for acc, (_, off) in zip(accs, taps)]
            for acc, (k, _) in zip(accs, taps):
                dw_ref[k] += acc

        dz = dz_ref[...]
        dv_ref[:, :CONV_WIDTH] = dz * sig
        dv_ref[:, CONV_WIDTH:] = dz * v[:, :CONV_WIDTH] * sig * (1.0 - sig)

    vrow, _ = _conv_specs(tm)
    row = pl.BlockSpec((tm, CONV_WIDTH), lambda i: (i, 0))
    nxt = pl.BlockSpec((CONV_HALO, CONV_WIDTH), lambda i: (jnp.minimum((i + 1) * per, nt * per - 1), 0))
    return _pallas(
        body, ride=ride, name="conv_bwd_taps", grid=(nt,),
        in_specs=[row, nxt, vrow, _full((CONV_HALO, CONV_WIDTH))],
        out_specs=[vrow, _full((CONV_HALO, 8, CONV_WIDTH))],
        out_shape=[S((L, 2 * CONV_WIDTH), f32), S((CONV_HALO, 8, CONV_WIDTH), f32)],
        scratch_shapes=_conv_scratch(tm) + [pltpu.VMEM((tm, CONV_WIDTH), f32)] * 2,
        compiler_params=_cp("arbitrary"),
    )(dzc, dzc, v, w_dw)


def _to_segments(a):
    L, c = a.shape
    return a.reshape(SEGMENTS, L // SEGMENTS, c).transpose(1, 0, 2).reshape(L, c)


def _from_segments(a):
    L, c = a.shape
    return a.reshape(L // SEGMENTS, SEGMENTS, c).transpose(1, 0, 2).reshape(L, c)


def _block_diag(ms):
    n = len(ms)

    def body(*refs):
        for a in range(n):
            out = refs[n + a]
            out[...] = jnp.zeros_like(out)
            for g in range(S5_GROUPS):
                rows = slice(g * S5_GROUP_CH, (g + 1) * S5_GROUP_CH)
                out[rows, g * S5_STATE:(g + 1) * S5_STATE] = refs[a][rows, :].astype(bf16)

    return pl.pallas_call(body, name="s5_block_diag", out_shape=[S((S5_WIDTH, S5_LANES), bf16)] * n,
                          compiler_params=pltpu.CompilerParams(vmem_limit_bytes=VMEM_LIMIT))(
        *[m.reshape(S5_WIDTH, S5_STATE) for m in ms])


def _diag_blocks(ms):
    n = len(ms)
    per_chunk = SCAN_LANES // S5_STATE

    def body(*refs):
        for a in range(n):
            for g in range(S5_GROUPS):
                rows = slice(g * S5_GROUP_CH, (g + 1) * S5_GROUP_CH)
                at = g % per_chunk * S5_STATE
                refs[n + a][rows, :] = refs[a][rows, at:at + S5_STATE]

    out = pl.pallas_call(body, name="s5_diag_blocks", out_shape=[S((S5_WIDTH, S5_STATE), f32)] * n,
                         compiler_params=pltpu.CompilerParams(vmem_limit_bytes=VMEM_LIMIT))(*ms)
    return [o.reshape(S5_GROUPS, S5_GROUP_CH, S5_STATE) for o in out]


class _NoExchanges:
    def before(self, point):
        return ()

    def after(self, point):
        pass

    def alone(self, point):
        pass


def _ffn_block(x, p, tag, tm, sched, head=None, mixed=None, mixer=None):
    point = tag + "_up"
    h, dadg, dadu, a, *x_in = _ffn_up(x, p[tag + "_norm"], p[tag + "_w_gate"], p[tag + "_w_up"], tm, tag, mixed,
                                      ride=sched.before(point))
    sched.after(point)
    x, = x_in or [x]
    if head is None:
        out = _ffn_down(x, a, p[tag + "_w_down"], tm, tag, mixer and tuple(p[n] for n in mixer), ride=sched.before(tag + "_down"))
        sched.after(tag + "_down")
    else:
        out = _ffn_down_loss(x, a, p[tag + "_w_down"], *head, tm, tag)
    return out, (x, h, dadg, dadu, a)


def _ffn_block_bwd(dxo, x, p, tag, saved, tm, grads, sched, parts=1, dxh=None):
    _, h, dadg, dadu, a = saved

    def weight_grad(which, lhs, rhs):
        point = tag + "_dw_" + which
        grads[tag + "_w_" + which] = _mm_tn(lhs, rhs, bf16, point, ride=sched.before(point))
        sched.after(point)

    if dxh is not None:
        weight_grad("down", a, dxh)
    dgate, dup, own_dxh = _ffn_bwd_act(dxo, p[tag + "_w_down"], dadg, dadu, tm, tag, ride=sched.before(tag + "_bwd_act"))
    sched.after(tag + "_bwd_act")
    weight_grad("gate", dgate, h)
    weight_grad("up", dup, h)
    if dxh is None:
        weight_grad("down", a, own_dxh)
    tiles = x.shape[0] // tm
    dx, dgs = None, []
    for k in range(parts):
        point = tag + "_bwd_in" + ("_%d" % k) * (parts > 1)
        dx, dg = _ffn_bwd_in(dxo, x, p[tag + "_norm"], dgate, dup, p[tag + "_w_gate"], p[tag + "_w_up"], tm, point,
                             tiles=(k * tiles // parts, tiles // parts), into=dx, ride=sched.before(point))
        sched.after(point)
        dgs.append(dg)
    grads[tag + "_norm"] = functools.reduce(jnp.add, dgs)
    return dx


def _local_step(x, target, p, grads, sched):
    L = x.shape[0]
    tm = min(512, L // 2)
    ni = L // SEGMENTS
    bi = min(64, ni)

    def carried(point, fn, *args):
        out = fn(*args, ride=sched.before(point))
        sched.after(point)
        return out

    (x1, h2, u_s5, v), saved1 = _ffn_block(x, p, "ffn1", tm, sched, mixer=("mix_norm", "w_in"))

    s5_in = (p["s5_lam_re"], p["s5_lam_im"], p["s5_log_dt"].reshape(S5_GROUPS, 1), p["s5_b_re"], p["s5_b_im"])
    abar_re, abar_im, bbar_re, bbar_im = _s5_params(*s5_in)
    a_re, a_im = abar_re.reshape(1, S5_LANES), abar_im.reshape(1, S5_LANES)
    bb_re, bb_im, cc_re, cc_im = _block_diag([bbar_re, bbar_im, p["s5_c_re"], p["s5_c_im"]])
    u_seg = _to_segments(u_s5)
    s_re, s_im, y_lin = carried("s5_forward", _s5_forward, u_seg, a_re, a_im, bb_re, bb_im, cc_re, cc_im, bi)
    y_s5 = _from_segments(_s5_gate(y_lin, u_seg, p["s5_d"], p["s5_w_glu"], p["s5_b_glu"], tm))
    w_dw = jnp.pad(p["conv_w_dw"], ((0, CONV_HALO - CONV_K), (0, 0)))
    heads = jnp.arange(CONV_WIDTH) // CONV_HEAD
    avg = ((heads[:, None] == heads[None, :]).astype(f32) / CONV_HEAD).astype(bf16)
    y_conv, zc = carried("conv_fwd", _conv_fwd, v, w_dw, p["conv_b_dw"], p["conv_ln_g"], p["conv_ln_b"], avg, tm)

    (dx3, grads["final_norm"], loss_terms), saved2 = _ffn_block(
        x1, p, "ffn2", tm, sched, head=(target, p["final_norm"].reshape(1, D_MODEL)), mixed=(y_s5, y_conv, p["w_out"]))
    x2 = saved2[0]
    grads["loss_terms"] = loss_terms

    dx2 = _ffn_block_bwd(dx3, x2, p, "ffn2", saved2, tm, grads, sched)

    dy_s5, dy_conv, dx2b = carried("mix_out_bwd", _mix_out_bwd, dx2, p["w_out"], tm)
    grads["w_out"] = _dw_out(y_s5, y_conv, dx2b)
    dy_lin, du_skip, dd8, grads["s5_w_glu"], dbg8 = _s5_read_bwd(
        _to_segments(dy_s5), y_lin, u_seg, p["s5_d"], p["s5_w_glu"], p["s5_b_glu"], tm)
    grads["s5_d"] = dd8.sum(axis=0, keepdims=True)
    grads["s5_b_glu"] = dbg8.sum(axis=0, keepdims=True)
    du_seg, da_re8, da_im8, dbb_re, dbb_im, dcc_re, dcc_im = carried(
        "s5_backward", _s5_backward, dy_lin, u_seg, du_skip, s_re, s_im, a_re, -a_im, bb_re, bb_im, cc_re, cc_im, bi)
    d_abar = lambda a8: a8.sum(axis=0).reshape(S5_GROUPS, S5_STATE)
    grads["s5_c_re"], grads["s5_c_im"], d_bbr, d_bbi = _diag_blocks([dcc_re, dcc_im, dbb_re, dbb_im])
    d_lr, d_li, d_ld, d_br, d_bi = _s5_params_bwd(*s5_in, d_abar(da_re8), d_abar(da_im8), d_bbr, d_bbi)
    grads["s5_lam_re"], grads["s5_lam_im"], grads["s5_log_dt"] = d_lr, d_li, d_ld.reshape(1, S5_GROUPS)
    grads["s5_b_re"], grads["s5_b_im"] = d_br, d_bi
    dzc, dlg8, dlb8, dbd8 = _conv_bwd_norm(dy_conv, zc, p["conv_ln_g"], p["conv_ln_b"], avg, tm)
    grads["conv_ln_g"] = dlg8.sum(axis=0, keepdims=True)
    grads["conv_ln_b"] = dlb8.sum(axis=0, keepdims=True)
    grads["conv_b_dw"] = dbd8.sum(axis=0, keepdims=True)
    dv, dw8 = carried("conv_bwd_taps", _conv_bwd_taps, dzc, v, w_dw, tm)
    grads["conv_w_dw"] = dw8.sum(axis=1)[:CONV_K]
    dx1, grads["mix_norm"], dub, dx1h = carried("mix_in_bwd", _mix_in_bwd, dx2, x1, p["mix_norm"], _from_segments(du_seg), dv,
                                                p["w_in"], tm)
    grads["w_in"] = _mm_tn(dub, h2, bf16, "dw_in")

    dx0 = _ffn_block_bwd(dx1, x, p, "ffn1", saved1, tm, grads, sched, parts=min(2, L // tm), dxh=dx1h)
    sched.alone("tail")
    return loss_terms, dx0


MESH = pl.DeviceIdType.MESH
ANY = pl.BlockSpec(memory_space=pl.ANY)


def _place():
    return lax.axis_index("x"), lax.axis_index("y"), lax.axis_index("c")


class _Exchange:
    def __init__(self, ins, out_shape, sems, start, finish):
        self.ins, self.out_shape, self.sems, self.start, self.finish = list(ins), list(out_shape), list(sems), start, finish
        self.out = None


def _pallas(body, *, ride=(), **kw):
    if not ride:
        return pl.pallas_call(body, **kw)

    def run(*args):
        out_shape = kw.get("out_shape", [])
        single = not isinstance(out_shape, (list, tuple))
        shapes = [out_shape] if single else list(out_shape)
        out_specs = [kw["out_specs"]] if single else list(kw.get("out_specs", []))
        grid = tuple(kw.get("grid", ()))
        scratch = list(kw.get("scratch_shapes", ()))
        n_in, n_out, n_scr = len(args), len(shapes), len(scratch)
        r_in = [len(e.ins) for e in ride]
        r_out = [len(e.out_shape) for e in ride]
        r_sem = [len(e.sems) for e in ride]

        def wrapped(*refs):
            own_in, refs = refs[:n_in], refs[n_in:]
            ex_in, refs = refs[:sum(r_in)], refs[sum(r_in):]
            own_out, refs = refs[:n_out], refs[n_out:]
            ex_out, refs = refs[:sum(r_out)], refs[sum(r_out):]
            own_scr, ex_sem = refs[:n_scr], refs[n_scr:]
            parts = []
            for e, ni, no, ns in zip(ride, r_in, r_out, r_sem):
                parts.append((e, ex_in[:ni], ex_out[:no], ex_sem[:ns]))
                ex_in, ex_out, ex_sem = ex_in[ni:], ex_out[no:], ex_sem[ns:]

            def at(step):
                def go():
                    for e, i, o, s in parts:
                        getattr(e, step)(i, o, s)
                if grid:
                    ids = [pl.program_id(d) for d in range(len(grid))]
                    when = [i == (0 if step == "start" else g - 1) for i, g in zip(ids, grid)]
                    pl.when(functools.reduce(lambda a, b: a & b, when))(go)
                else:
                    go()

            at("start")
            if body is not None:
                body(*own_in, *own_out, *own_scr)
            at("finish")

        outs = pl.pallas_call(
            wrapped, name=kw["name"], grid=grid,
            in_specs=list(kw.get("in_specs", [])) + [ANY] * sum(r_in),
            out_specs=out_specs + [ANY] * sum(r_out),
            out_shape=shapes + [s for e in ride for s in e.out_shape],
            scratch_shapes=scratch + [s for e in ride for s in e.sems],
            input_output_aliases=kw.get("input_output_aliases", {}),
            compiler_params=_cp(*["arbitrary"] * len(grid)),
        )(*args, *[a for e in ride for a in e.ins])
        own, rest = outs[:n_out], outs[n_out:]
        for e, no in zip(ride, r_out):
            e.out, rest = list(rest[:no]), rest[no:]
        return own[0] if single else own

    return run


def _exchange(ride, name):
    _pallas(None, ride=ride, name=name)()


def _gather(arrs):
    n = len(arrs)

    def copies(ins, outs, sems):
        send_sems, recv_sems, local_sems = sems
        x, y, c = _place()
        me, sibling = (x, y, c), (x, y, 1 - c)
        chips = [(1 - x, y), (x, 1 - y), (1 - x, 1 - y)]

        def place(a, block):
            return outs[a].at[block]

        def copy(a, k, block, to, src=None):
            px, py, pc = block
            dst = place(a, 4 * px + 2 * py + pc)
            return pltpu.make_async_remote_copy(
                src_ref=dst if src is None else src, dst_ref=dst, send_sem=send_sems.at[7 * a + k],
                recv_sem=recv_sems.at[7 * a + k], device_id=to, device_id_type=MESH)

        def own():
            local = [pltpu.make_async_copy(ins[a], place(a, 4 * x + 2 * y + c), local_sems.at[a]) for a in range(n)]
            remote = []
            for a in range(n):
                remote.append(copy(a, 0, me, sibling, src=ins[a]))
                remote += [copy(a, 1 + j, me, (*chip, c), src=ins[a]) for j, chip in enumerate(chips)]
            return local, remote

        return c, me, sibling, chips, copy, own

    def start(ins, outs, sems):
        local, remote = copies(ins, outs, sems)[-1]()
        for cp in local + remote:
            cp.start()

    def finish(ins, outs, sems):
        c, me, sibling, chips, copy, own = copies(ins, outs, sems)
        passed = []
        for j, chip in enumerate(chips):
            for a in range(n):
                copy(a, 1 + j, (*chip, c), me).wait_recv()
                passed.append(copy(a, 4 + j, (*chip, c), sibling))
                passed[-1].start()
        for a in range(n):
            copy(a, 0, sibling, me).wait_recv()
            for j, chip in enumerate(chips):
                copy(a, 4 + j, (*chip, 1 - c), me).wait_recv()
        local, remote = own()
        for cp in remote + passed:
            cp.wait_send()
        for cp in local:
            cp.wait()

    dma = pltpu.SemaphoreType.DMA
    shapes = [S((N_DEV, *a.shape), a.dtype) for a in arrs]
    return _Exchange(arrs, shapes, [dma((7 * n,)), dma((7 * n,)), dma((n,))], start, finish)


def _swap_with_sibling(gs):
    n = len(gs)

    def copies(ins, outs, sems):
        x, y, c = _place()
        return [pltpu.make_async_remote_copy(
            src_ref=ins[a].at[:, 1 - c], dst_ref=outs[a], send_sem=sems[0].at[a], recv_sem=sems[1].at[a],
            device_id=(x, y, 1 - c), device_id_type=MESH) for a in range(n)]

    def start(ins, outs, sems):
        for cp in copies(ins, outs, sems):
            cp.start()

    def finish(ins, outs, sems):
        for cp in copies(ins, outs, sems):
            cp.wait()

    dma = pltpu.SemaphoreType.DMA
    return _Exchange(gs, [S((N_CHIP, *g.shape[2:]), g.dtype) for g in gs], [dma((n,)), dma((n,))], start, finish)


def _swap_with_chips(ps):
    n = len(ps)

    def copies(ins, outs, sems):
        x, y, c = _place()
        q = 2 * x + y
        peers = [(x, 1 - y), (1 - x, y), (1 - x, 1 - y)]

        def copy(a, j, slot_from, slot_to):
            px, py = peers[j]
            return pltpu.make_async_remote_copy(
                src_ref=ins[a].at[slot_from], dst_ref=outs[a].at[slot_to], send_sem=sems[0].at[3 * a + j],
                recv_sem=sems[1].at[3 * a + j], device_id=(px, py, c), device_id_type=MESH)

        sends = lambda: [copy(a, j, 2 * peers[j][0] + peers[j][1], q) for a in range(n) for j in range(3)]
        lands = lambda: [copy(a, j, q, 2 * peers[j][0] + peers[j][1]) for a in range(n) for j in range(3)]
        return sends, lands

    def start(ins, outs, sems):
        for cp in copies(ins, outs, sems)[0]():
            cp.start()

    def finish(ins, outs, sems):
        sends, lands = copies(ins, outs, sems)
        for cp in lands():
            cp.wait_recv()
        for cp in sends():
            cp.wait_send()

    dma = pltpu.SemaphoreType.DMA
    return _Exchange(ps, [S(p.shape, p.dtype) for p in ps], [dma((3 * n,)), dma((3 * n,))], start, finish)


def _row_tile(rows, cols, itemsize):
    t = rows
    while t * cols * itemsize > (1 << 20) and t % 32 == 0:
        t //= 2
    return t


def _add_sibling(g4, st, core, name):
    _, R, C = st.shape
    tr = _row_tile(R, C, 1)

    def body(c_ref, g_ref, s_ref, o_ref):
        o_ref[...] = (g_ref[...].astype(f32) + s_ref[...].astype(f32)).astype(bf16)

    mine = pl.BlockSpec((None, None, tr, C), lambda q, i, c: (q, c[0], i, 0))
    return pl.pallas_call(
        body, name=name,
        grid_spec=pltpu.PrefetchScalarGridSpec(
            num_scalar_prefetch=1, grid=(N_CHIP, R // tr),
            in_specs=[mine,
                      pl.BlockSpec((None, tr, C), lambda q, i, c: (q, i, 0))],
            out_specs=pl.BlockSpec((None, tr, C), lambda q, i, c: (q, i, 0))),
        out_shape=S((N_CHIP, R, C), bf16),
        compiler_params=_cp("parallel", "parallel"),
    )(core, g4, st)


SMEM = pl.BlockSpec(memory_space=pltpu.SMEM)
VMEM = pl.BlockSpec(memory_space=pltpu.VMEM)


def _add_sibling_small(items, core, name):
    n = len(items)

    def body(c_ref, *refs):
        c = c_ref[0]
        for k in range(n):
            g_ref, s_ref, o_ref = refs[2 * k], refs[2 * k + 1], refs[2 * n + k]
            for q in range(N_CHIP):
                o_ref[q] = (g_ref[q, c].astype(f32) + s_ref[q].astype(f32)).astype(bf16)

    return pl.pallas_call(body, name=name, in_specs=[SMEM] + [VMEM] * (2 * n), out_specs=[VMEM] * n,
                          out_shape=[S(st.shape, bf16) for _, st in items],
                          compiler_params=pltpu.CompilerParams(vmem_limit_bytes=VMEM_LIMIT))(
        core, *[a for item in items for a in item])


def _adam_small(items, slots, name):
    n = len(items)

    def body(s_ref, *refs):
        ins, outs = refs[:5 * n], refs[5 * n:]
        for k in range(n):
            w_ref, m_ref, v_ref, p_ref, got_ref = ins[5 * k:5 * k + 5]
            g = p_ref[s_ref[0]].astype(f32)
            for j in range(1, N_CHIP):
                g = g + got_ref[s_ref[j]].astype(f32)
            outs[4 * k][...] = g
            outs[4 * k + 1][...], outs[4 * k + 2][...], outs[4 * k + 3][...] = _adamw(w_ref[...], g, m_ref[...], v_ref[...])

    out = pl.pallas_call(body, name=name, in_specs=[SMEM] + [VMEM] * (5 * n), out_specs=[VMEM] * (4 * n),
                         out_shape=[S(item[0].shape, f32) for item in items for _ in range(4)],
                         compiler_params=pltpu.CompilerParams(vmem_limit_bytes=VMEM_LIMIT))(
        slots, *[a for item in items for a in item])
    return [out[4 * k:4 * k + 4] for k in range(n)]


def _adamw(w, g, m, v):
    m = B1 * m + (1.0 - B1) * g
    v = B2 * v + (1.0 - B2) * (g * g)
    m_hat = m / (1.0 - B1 ** STEP)
    v_hat = v / (1.0 - B2 ** STEP)
    return -LR * (m_hat / (jnp.sqrt(v_hat) + ADAM_EPS) + WD * w), m, v


def _adam_sharded(items, slots, name):
    n = len(items)
    R, C = items[0][0].shape
    tr = _row_tile(R, C, 4 * n)

    def body(s_ref, *refs):
        ins, outs = refs[:7 * n], refs[7 * n:]
        for k in range(n):
            w_ref, m_ref, v_ref, p_ref, a_ref, b_ref, c_ref = ins[7 * k:7 * k + 7]
            g = p_ref[...].astype(f32) + a_ref[...].astype(f32) + b_ref[...].astype(f32) + c_ref[...].astype(f32)
            outs[4 * k][...] = g
            outs[4 * k + 1][...], outs[4 * k + 2][...], outs[4 * k + 3][...] = _adamw(w_ref[...], g, m_ref[...], v_ref[...])

    shard = pl.BlockSpec((tr, C), lambda i, s: (i, 0))
    slot = lambda k: pl.BlockSpec((None, tr, C), lambda i, s: (s[k], i, 0))
    out = pl.pallas_call(
        body, name=name,
        grid_spec=pltpu.PrefetchScalarGridSpec(
            num_scalar_prefetch=1, grid=(R // tr,),
            in_specs=[shard, shard, shard, slot(0), slot(1), slot(2), slot(3)] * n,
            out_specs=[shard] * (4 * n)),
        out_shape=[S((R, C), f32)] * (4 * n),
        compiler_params=_cp("parallel"),
    )(slots, *[a for w, m, v, part, got in items for a in (w, m, v, part, got, got, got)])
    return [out[4 * k:4 * k + 4] for k in range(n)]


def _adam_replicated(items, loss_terms, name):
    n = len(items)
    has_loss = loss_terms is not None

    def total(ref):
        g = ref[0]
        for d in range(1, N_DEV):
            g = g + ref[d]
        return g

    def body(*refs):
        ins, outs = refs[:4 * n + has_loss], refs[4 * n + has_loss:]
        for i in range(n):
            w_ref, m_ref, v_ref, g_ref = ins[4 * i:4 * i + 4]
            g = total(g_ref)
            outs[4 * i][...] = g
            outs[4 * i + 1][...], outs[4 * i + 2][...], outs[4 * i + 3][...] = _adamw(w_ref[...], g, m_ref[...], v_ref[...])
        if has_loss:
            outs[-1][...] = jnp.sum(total(ins[-1]), keepdims=True)

    flat = [a for item in items for a in item] + ([loss_terms] if has_loss else [])
    shapes = [S(item[0].shape, f32) for item in items for _ in range(4)] + ([S((1, 1), f32)] if has_loss else [])
    out = pl.pallas_call(body, name=name, out_shape=shapes,
                         compiler_params=pltpu.CompilerParams(vmem_limit_bytes=VMEM_LIMIT))(*flat)
    return [out[4 * i:4 * i + 4] for i in range(n)], (out[-1] if has_loss else None)


WEIGHTS = ["ffn1_norm", "ffn1_w_gate", "ffn1_w_up", "ffn1_w_down", "mix_norm", "w_in", "s5_lam_re", "s5_lam_im", "s5_log_dt",
           "s5_b_re", "s5_b_im", "s5_c_re", "s5_c_im", "s5_d", "s5_w_glu", "s5_b_glu", "conv_w_dw", "conv_b_dw", "conv_ln_g",
           "conv_ln_b", "w_out", "ffn2_norm", "ffn2_w_gate", "ffn2_w_up", "ffn2_w_down", "final_norm"]
SHARDED = ["ffn1_w_gate", "ffn1_w_up", "ffn1_w_down", "w_in", "s5_w_glu", "conv_w_dw", "w_out", "ffn2_w_gate", "ffn2_w_up",
           "ffn2_w_down"]
REPLICATED = [n for n in WEIGHTS if n not in SHARDED]
TRANSPOSED = ["ffn1_w_gate", "ffn1_w_up", "ffn2_w_gate", "ffn2_w_up", "w_in"]


def _shard_to_wire(n, w):
    if n == "conv_w_dw":
        return jnp.pad(w, ((0, CONV_HALO - CONV_K), (0, 0)))
    return w.astype(bf16)


def _to_wire(shards, ride):
    names = list(shards)
    shapes = [jax.eval_shape(functools.partial(_shard_to_wire, n), shards[n]) for n in names]

    def body(*refs):
        for src, dst in zip(refs[:len(names)], refs[len(names):]):
            (r, c), (rp, cp) = src.shape, dst.shape
            dst[:r, :c] = src[...].astype(dst.dtype)
            if cp > c:
                dst[:, c:] = jnp.zeros((rp, cp - c), dst.dtype)
            if rp > r:
                dst[r:, :] = jnp.zeros((rp - r, cp), dst.dtype)

    out = _pallas(body, ride=ride, name="to_wire", out_shape=shapes, in_specs=[pl.BlockSpec(memory_space=pltpu.VMEM)] * len(names),
                  out_specs=[pl.BlockSpec(memory_space=pltpu.VMEM)] * len(names))(*[shards[n] for n in names])
    return dict(zip(names, out))


def _gathered_to_full(n, g):
    if n == "conv_w_dw":
        return g.transpose(1, 0, 2).reshape(CONV_HALO, CONV_WIDTH)[:CONV_K]
    return g.reshape(N_DEV * g.shape[1], g.shape[2])


def _grad_to_blocks(n, g):
    if n == "conv_w_dw":
        g = jnp.pad(g, ((0, CONV_HALO - CONV_K), (0, 0)))
        g = g.reshape(g.shape[0], N_DEV, g.shape[1] // N_DEV).transpose(1, 0, 2)
    else:
        g = g.reshape(N_DEV, g.shape[0] // N_DEV, g.shape[1])
    return g.astype(bf16).reshape(N_CHIP, 2, *g.shape[1:])


REPLICATED_LATE = ["ffn1_norm"]
REPLICATED_HEAD = ["ffn2_norm", "final_norm"]
REPLICATED_MIX = ["mix_norm", "conv_b_dw", "conv_ln_g", "conv_ln_b"]
REPLICATED_S5 = [n for n in REPLICATED if n not in REPLICATED_LATE + REPLICATED_HEAD + REPLICATED_MIX]
REPLICATED_EARLY = REPLICATED_HEAD + REPLICATED_S5 + REPLICATED_MIX

PLAN = {
    "start": [("gather", ["ffn1_w_gate", "ffn1_w_up"])],
    "ffn1_up": [("gather", ["ffn1_w_down", "w_in", "w_out", "s5_w_glu", "conv_w_dw"])],
    "ffn1_down": [("gather", ["ffn2_w_gate"])],
    "s5_forward": [("gather", ["ffn2_w_up"])],
    "conv_fwd": [("gather", ["ffn2_w_down"])],
    "ffn2_dw_up": [("sibling", ["ffn2_w_gate"])],
    "ffn2_dw_down": [("sibling", ["ffn2_w_up"])],
    "mix_out_bwd": [("sibling", ["ffn2_w_down"]), ("replicated", REPLICATED_HEAD)],
    "s5_backward": [("chips", ["ffn2_w_gate", "ffn2_w_up"])],
    "conv_bwd_taps": [("chips", ["ffn2_w_down"])],
    "mix_in_bwd": [("replicated", REPLICATED_S5)],
    "ffn1_dw_down": [("sibling", ["w_in", "s5_w_glu", "conv_w_dw", "w_out"]), ("replicated", REPLICATED_MIX)],
    "ffn1_bwd_act": [("chips", ["w_in", "s5_w_glu", "conv_w_dw", "w_out"]), ("sibling", ["ffn1_w_down"])],
    "ffn1_dw_gate": [("chips", ["ffn1_w_down"])],
    "ffn1_dw_up": [("sibling", ["ffn1_w_gate"])],
    "ffn1_bwd_in_0": [("chips", ["ffn1_w_gate"]), ("sibling", ["ffn1_w_up"])],
    "ffn1_bwd_in_1": [("chips", ["ffn1_w_up"])],
    "tail": [("replicated", REPLICATED_LATE)],
}


class _Schedule:
    def __init__(self, wire, p, grads, core):
        self.wire, self.p, self.grads, self.core = wire, p, grads, core
        self.partial, self.reduced, self.everyone, self.pending = {}, {}, {}, []

    def before(self, point):
        assert not self.pending
        for kind, names in PLAN.get(point, ()):
            if kind == "gather":
                given = [self.wire[n] for n in names]
                ex = _gather(given)
            elif kind == "sibling":
                given = [_grad_to_blocks(n, self.grads[n]) for n in names]
                ex = _swap_with_sibling(given)
            elif kind == "chips":
                given = [self.partial.pop(n) for n in names]
                ex = _swap_with_chips(given)
            else:
                names = names + ["loss_terms"] * (names is REPLICATED_HEAD)
                given = [self.grads[n].reshape(self.p[n].shape) if n in self.p else self.grads[n] for n in names]
                ex = _gather(given)
            self.pending.append((kind, names, given, ex))
        return [ex for _, _, _, ex in self.pending]

    def after(self, point):
        for kind, names, given, ex in self.pending:
            if kind == "gather":
                for n, g in zip(names, ex.out):
                    self.p[n] = _gathered_to_full(n, g)
            elif kind == "sibling":
                if len(names) > 1:
                    sums = _add_sibling_small(list(zip(given, ex.out)), self.core, "reduce_add_" + names[0])
                else:
                    sums = [_add_sibling(given[0], ex.out[0], self.core, "reduce_add_" + names[0])]
                self.partial.update(zip(names, sums))
            elif kind == "chips":
                for n, part, got in zip(names, given, ex.out):
                    self.reduced[n] = (part, got)
            else:
                self.everyone.update(zip(names, ex.out))
        self.pending = []

    def alone(self, point):
        _exchange(self.before(point), point)
        self.after(point)


def kernel(x, ffn1_norm, ffn1_w_gate, ffn1_w_up, ffn1_w_down, mix_norm, w_in, s5_lam_re, s5_lam_im, s5_log_dt, s5_b_re, s5_b_im, s5_c_re, s5_c_im, s5_d, s5_w_glu, s5_b_glu, conv_w_dw, conv_b_dw, conv_ln_g, conv_ln_b, w_out, ffn2_norm, ffn2_w_gate, ffn2_w_up, ffn2_w_down, final_norm, loss_target, m_ffn1_norm, m_ffn1_w_gate, m_ffn1_w_up, m_ffn1_w_down, m_mix_norm, m_w_in, m_s5_lam_re, m_s5_lam_im, m_s5_log_dt, m_s5_b_re, m_s5_b_im, m_s5_c_re, m_s5_c_im, m_s5_d, m_s5_w_glu, m_s5_b_glu, m_conv_w_dw, m_conv_b_dw, m_conv_ln_g, m_conv_ln_b, m_w_out, m_ffn2_norm, m_ffn2_w_gate, m_ffn2_w_up, m_ffn2_w_down, m_final_norm, v_ffn1_norm, v_ffn1_w_gate, v_ffn1_w_up, v_ffn1_w_down, v_mix_norm, v_w_in, v_s5_lam_re, v_s5_lam_im, v_s5_log_dt, v_s5_b_re, v_s5_b_im, v_s5_c_re, v_s5_c_im, v_s5_d, v_s5_w_glu, v_s5_b_glu, v_conv_w_dw, v_conv_b_dw, v_conv_ln_g, v_conv_ln_b, v_w_out, v_ffn2_norm, v_ffn2_w_gate, v_ffn2_w_up, v_ffn2_w_down, v_final_norm):
    args = locals()
    w = {n: args[n] for n in WEIGHTS}
    m = {n: args["m_" + n] for n in WEIGHTS}
    v = {n: args["v_" + n] for n in WEIGHTS}
    xq, yq, cq = _place()
    q = 2 * xq + yq
    slots = jnp.stack([q, q ^ 1, q ^ 2, q ^ 3]).astype(jnp.int32)

    def shard2d(n, a):
        a = a.reshape(a.shape[-2:])
        return a.T if n in TRANSPOSED else a

    def view(n, a):
        if n.startswith("s5_b_") and a.ndim == 4:
            return a[0].transpose(0, 2, 1)
        return a[0] if a.ndim >= 3 else a.reshape(1, -1)

    def unview(n, a):
        return (a.transpose(0, 2, 1) if n.startswith("s5_b_") and a.ndim == 3 else a).reshape(w[n].shape)

    p = {n: view(n, w[n]) for n in REPLICATED}
    grads = {}
    first = PLAN["start"][0][1]
    wire = {n: _shard_to_wire(n, shard2d(n, w[n])) for n in first}
    sched = _Schedule(wire, p, grads, jnp.reshape(cq, (1,)).astype(jnp.int32))
    wire.update(_to_wire({n: shard2d(n, w[n]) for n in SHARDED if n not in first}, sched.before("start")))
    sched.after("start")
    _, dx = _local_step(x[0], loss_target[0], p, grads, sched)

    out = {}
    groups = [[n for n in SHARDED if n.startswith(tag)] for tag in ("ffn1", "ffn2")]
    for names in groups + [[n for n in SHARDED if not n.startswith("ffn")]]:
        def fit(n, a):
            a = shard2d(n, a)
            return jnp.pad(a, ((0, sched.reduced[n][1].shape[1] - a.shape[0]), (0, 0)))

        items = [(fit(n, w[n]), fit(n, m[n]), fit(n, v[n]), *sched.reduced[n]) for n in names]
        update = _adam_sharded if names[0].startswith("ffn") else _adam_small
        for n, res in zip(names, update(items, slots, "adam_" + names[0])):
            back = lambda r: r[:shard2d(n, w[n]).shape[0]]
            out[n] = [(back(r).T if n in TRANSPOSED else back(r)).reshape(w[n].shape) for r in res]

    for names in (REPLICATED_EARLY, REPLICATED_LATE):
        items = [(view(n, w[n]), view(n, m[n]), view(n, v[n]), sched.everyone[n]) for n in names]
        res, total = _adam_replicated(items, sched.everyone.get("loss_terms") if names is REPLICATED_EARLY else None,
                                      "adam_" + names[0])
        for n, r in zip(names, res):
            out[n] = [unview(n, a) for a in r]
        if total is not None:
            loss = total.reshape(())

    return (loss, dx.reshape(x.shape), *[out[n][0] for n in WEIGHTS], *[out[n][1] for n in WEIGHTS],
            *[out[n][2] for n in WEIGHTS], *[out[n][3] for n in WEIGHTS])
```

```python
import functools

import jax
import jax.numpy as jnp
from jax import lax
from jax.experimental import pallas as pl
from jax.experimental.pallas import tpu as pltpu

f32 = jnp.float32
bf16 = jnp.bfloat16
S = jax.ShapeDtypeStruct

N_DEV = 8
N_CHIP = 4
D_MODEL = 1024
D_FF = 2816
FF_CHUNKS = [(0, 768), (768, 1536), (1536, 2304), (2304, D_FF)]
S5_WIDTH = 512
S5_GROUPS = 32
S5_GROUP_CH = 16
S5_STATE = 64
S5_LANES = S5_GROUPS * S5_STATE
CONV_WIDTH = 512
CONV_K = 31
CONV_HALO = 32
CONV_HEAD = 64
CONV_ROWS = 32
IN_COLS = S5_WIDTH + 2 * CONV_WIDTH
SEGMENTS = 8
SCAN_LANES = 512
EPS = 1e-6
LR, B1, B2, ADAM_EPS, WD, STEP = 0.001, 0.9, 0.999, 1e-08, 0.01, 10
VMEM_LIMIT = 56 * 1024 * 1024

NN = (((1,), (0,)), ((), ()))
NT = (((1,), (1,)), ((), ()))
TN = (((0,), (0,)), ((), ()))


def _dot(a, b, dims=NN):
    return lax.dot_general(a, b, dims, preferred_element_type=f32)


def _cp(*sem):
    return pltpu.CompilerParams(dimension_semantics=sem, vmem_limit_bytes=VMEM_LIMIT)


def _rms(x, g):
    return x * lax.rsqrt(jnp.mean(x * x, axis=-1, keepdims=True) + EPS) * g


def _rms_bwd(x, g, dh):
    _, vjp = jax.vjp(_rms, x, g)
    return vjp(dh)


def _sigmoid(x):
    return 1.0 / (1.0 + jnp.exp(-x))


def _gelu(x):
    return 0.5 * x * (1.0 + jnp.tanh(0.7978845608028654 * (x + 0.044715 * x * x * x)))


def _rows8(x):
    t, c = x.shape
    return x.reshape(t // 8, 8, c).sum(axis=0)


def _full(shape):
    return pl.BlockSpec(shape, lambda *_: (0,) * len(shape))


def _resident(shape):
    return pl.BlockSpec(shape, lambda *_: (0,) * len(shape), pipeline_mode=pl.Buffered(1))


def _ffn_up(x, g, wg, wu, tm, tag, mixed=None, ride=()):
    L = x.shape[0]

    def body(x_ref, g_ref, wg_ref, wu_ref, *rest):
        h_ref, dadg_ref, dadu_ref, a_ref = rest[-5:-1] if mixed else rest[-4:]
        x = x_ref[...]
        if mixed:
            ys_ref, yc_ref, wo_ref = rest[:3]
            x = x + _dot(ys_ref[...], wo_ref[:S5_WIDTH, :]) + _dot(yc_ref[...], wo_ref[S5_WIDTH:, :])
            rest[-1][...] = x
        h = _rms(x, g_ref[...]).astype(bf16)
        h_ref[...] = h
        for lo, hi in FF_CHUNKS:
            cols = slice(lo, hi)
            gate =_dot(h, wg_ref[cols, :], NT)
            up = _dot(h, wu_ref[cols, :], NT)
            sig = _sigmoid(gate)
            silu = gate * sig
            dadg_ref[:, cols] = (up * (sig + silu * (1.0 - sig))).astype(bf16)
            dadu_ref[:, cols] = silu.astype(bf16)
            a_ref[:, cols] = (silu * up).astype(bf16)

    row = pl.BlockSpec((tm, D_MODEL), lambda i: (i, 0))
    wide = pl.BlockSpec((tm, D_FF), lambda i: (i, 0))
    half = pl.BlockSpec((tm, S5_WIDTH), lambda i: (i, 0))
    return _pallas(
        body, ride=ride, name=tag + "_up", grid=(L // tm,),
        in_specs=[row, _full((1, D_MODEL)), _resident((D_FF, D_MODEL)), _resident((D_FF, D_MODEL))]
        + ([half, half, _resident((D_MODEL, D_MODEL))] if mixed else []),
        out_specs=[row, wide, wide, wide] + [row] * bool(mixed),
        out_shape=[S((L, D_MODEL), bf16)] + [S((L, D_FF), bf16)] * 3 + [S((L, D_MODEL), f32)] * bool(mixed),
        compiler_params=_cp("parallel"),
    )(x, g, wg, wu, *(mixed or ()))


def _ffn_down(x, a, wd, tm, tag, mixer=None, ride=()):
    L = x.shape[0]

    def body(x_ref, a_ref, wd_ref, *rest):
        xo = x_ref[...] + 0.5 * _dot(a_ref[...], wd_ref[...])
        if not mixer:
            rest[0][...] = xo
            return
        g_ref, w_ref, o_ref, h_ref, us_ref, v_ref = rest
        o_ref[...] = xo
        h = _rms(xo, g_ref[...]).astype(bf16)
        h_ref[...] = h
        u = _dot(h, w_ref[...], NT)
        us_ref[...] = u[:, :S5_WIDTH]
        v_ref[...] = u[:, S5_WIDTH:]

    row = lambda c: pl.BlockSpec((tm, c), lambda i: (i, 0))
    extra_in = [_full((1, D_MODEL)), _resident((IN_COLS, D_MODEL))] if mixer else []
    extra_out = [(D_MODEL, bf16), (S5_WIDTH, f32), (2 * CONV_WIDTH, f32)] if mixer else []
    out = _pallas(
        body, ride=ride, name=tag + "_down", grid=(L // tm,),
        in_specs=[row(D_MODEL), row(D_FF), _resident((D_FF, D_MODEL))] + extra_in,
        out_specs=[row(D_MODEL)] + [row(c) for c, _ in extra_out],
        out_shape=[S((L, D_MODEL), f32)] + [S((L, c), t) for c, t in extra_out],
        compiler_params=_cp("parallel"),
    )(x, a, wd, *(mixer or ()))
    return out if mixer else out[0]


def _ffn_down_loss(x, a, wd, target, g, tm, tag):
    L = x.shape[0]

    def body(x_ref, a_ref, wd_ref, t_ref, g_ref, dx_ref, dg_ref, l_ref):
        @pl.when(pl.program_id(0) == 0)
        def _():
            dg_ref[...] = jnp.zeros_like(dg_ref)
            l_ref[...] = jnp.zeros_like(l_ref)

        xo = x_ref[...] + 0.5 * _dot(a_ref[...], wd_ref[...])
        g = g_ref[...]
        e = _rms(xo, g) - t_ref[...]
        l_ref[...] += _rows8(e * e) * (0.5 / D_MODEL)
        dx, dg = _rms_bwd(xo, g, e * (1.0 / D_MODEL))
        dx_ref[...] = dx
        dg_ref[...] += dg

    row = pl.BlockSpec((tm, D_MODEL), lambda i: (i, 0))
    return pl.pallas_call(
        body, name=tag + "_down_loss", grid=(L // tm,),
        in_specs=[row, pl.BlockSpec((tm, D_FF), lambda i: (i, 0)), _resident((D_FF, D_MODEL)), row, _full((1, D_MODEL))],
        out_specs=[row, _full((1, D_MODEL)), _full((8, D_MODEL))],
        out_shape=[S((L, D_MODEL), f32), S((1, D_MODEL), f32), S((8, D_MODEL), f32)],
        compiler_params=_cp("arbitrary"),
    )(x, a, wd, target, g)


def _ffn_bwd_act(dxo, wd, dadg, dadu, tm, tag, ride=()):
    L = dxo.shape[0]

    def body(dx_ref, wd_ref, dadg_ref, dadu_ref, dgate_ref, dup_ref, dxh_ref):
        dxh = (0.5 * dx_ref[...]).astype(bf16)
        dxh_ref[...] = dxh
        for lo, hi in FF_CHUNKS:
            cols = slice(lo, hi)
            da =_dot(dxh, wd_ref[cols, :], NT)
            dgate_ref[:, cols] = (da * dadg_ref[:, cols].astype(f32)).astype(bf16)
            dup_ref[:, cols] = (da * dadu_ref[:, cols].astype(f32)).astype(bf16)

    row = pl.BlockSpec((tm, D_MODEL), lambda i: (i, 0))
    wide = pl.BlockSpec((tm, D_FF), lambda i: (i, 0))
    return _pallas(
        body, ride=ride, name=tag + "_bwd_act", grid=(L // tm,),
        in_specs=[row, _resident((D_FF, D_MODEL)), wide, wide],
        out_specs=[wide, wide, row],
        out_shape=[S((L, D_FF), bf16), S((L, D_FF), bf16), S((L, D_MODEL), bf16)],
        compiler_params=_cp("parallel"),
    )(dxo, wd, dadg, dadu)


def _ffn_bwd_in(dxo, x, g, dgate, dup, wg, wu, tm, name, tiles=None, into=None, ride=()):
    L = x.shape[0]
    first, count = tiles or (0, L // tm)

    def body(dxo_ref, x_ref, g_ref, dgate_ref, dup_ref, wg_ref, wu_ref, *rest):
        dx_ref, dg_ref = rest[-2:]

        @pl.when(pl.program_id(0) == 0)
        def _():
            dg_ref[...] = jnp.zeros_like(dg_ref)

        dh = _dot(dgate_ref[...], wg_ref[...]) + _dot(dup_ref[...], wu_ref[...])
        dx, dg = _rms_bwd(x_ref[...], g_ref[...], dh)
        dx_ref[...] = dxo_ref[...] + dx
        dg_ref[...] += dg

    row = pl.BlockSpec((tm, D_MODEL), lambda i: (first + i, 0))
    wide = pl.BlockSpec((tm, D_FF), lambda i: (first + i, 0))
    return _pallas(
        body, ride=ride, name=name, grid=(count,),
        in_specs=[row, row, _full((1, D_MODEL)), wide, wide, _resident((D_FF, D_MODEL)), _resident((D_FF, D_MODEL))]
        + [ANY] * (into is not None),
        out_specs=[row, _full((1, D_MODEL))],
        out_shape=[S((L, D_MODEL), f32), S((1, D_MODEL), f32)],
        input_output_aliases={7: 0} if into is not None else {},
        compiler_params=_cp("arbitrary"),
    )(dxo, x, g, dgate, dup, wg, wu, *([into] if into is not None else []))


def _mm_tn(a, b, out_dtype, name, tm=512, tn=1024, ride=()):
    L, M = a.shape
    N = b.shape[1]
    tm, tn = min(tm, M), min(tn, N)
    while M % tm:
        tm //= 2
    while N % tn:
        tn //= 2

    def body(a_ref, b_ref, o_ref):
        o_ref[...] = _dot(a_ref[...].astype(bf16), b_ref[...].astype(bf16), TN).astype(out_dtype)

    return _pallas(
        body, ride=ride, name=name, grid=(M // tm, N // tn),
        in_specs=[pl.BlockSpec((L, tm), lambda i, j: (0, i)), pl.BlockSpec((L, tn), lambda i, j: (0, j))],
        out_specs=pl.BlockSpec((tm, tn), lambda i, j: (i, j)),
        out_shape=S((M, N), out_dtype),
        compiler_params=_cp("parallel", "parallel"),
    )(a, b)


def _mix_in_bwd(dxo, x, g, du_s5, dv, w_in, tm, ride=()):
    L = x.shape[0]

    def body(dxo_ref, x_ref, g_ref, dus_ref, dv_ref, w_ref, dx_ref, dg_ref, dub_ref, dxh_ref):
        @pl.when(pl.program_id(0) == 0)
        def _():
            dg_ref[...] = jnp.zeros_like(dg_ref)

        dus = dus_ref[...].astype(bf16)
        dvb = dv_ref[...].astype(bf16)
        dub_ref[:, :S5_WIDTH] = dus
        dub_ref[:, S5_WIDTH:] = dvb
        dh = _dot(dus, w_ref[:S5_WIDTH, :]) + _dot(dvb, w_ref[S5_WIDTH:, :])
        dx, dg = _rms_bwd(x_ref[...], g_ref[...], dh)
        dx = dxo_ref[...] + dx
        dx_ref[...] = dx
        dxh_ref[...] = (0.5 * dx).astype(bf16)
        dg_ref[...] += dg

    row = lambda c: pl.BlockSpec((tm, c), lambda i: (i, 0))
    return _pallas(
        body, ride=ride, name="mix_in_bwd", grid=(L // tm,),
        in_specs=[row(D_MODEL), row(D_MODEL), _full((1, D_MODEL)), row(S5_WIDTH), row(2 * CONV_WIDTH),
                  _full((IN_COLS, D_MODEL))],
        out_specs=[row(D_MODEL), _full((1, D_MODEL)), row(IN_COLS), row(D_MODEL)],
        out_shape=[S((L, D_MODEL), f32), S((1, D_MODEL), f32), S((L, IN_COLS), bf16), S((L, D_MODEL), bf16)],
        compiler_params=_cp("arbitrary"),
    )(dxo, x, g, du_s5, dv, w_in)


def _dw_out(y_s5, y_conv, dxb, tn=512):
    L = dxb.shape[0]

    def body(ys_ref, yc_ref, b_ref, o_ref):
        b = b_ref[...]
        o_ref[:S5_WIDTH, :] = _dot(ys_ref[...], b, TN).astype(bf16)
        o_ref[S5_WIDTH:, :] = _dot(yc_ref[...], b, TN).astype(bf16)

    return pl.pallas_call(
        body, name="dw_out", grid=(D_MODEL // tn,),
        in_specs=[_full((L, S5_WIDTH)), _full((L, CONV_WIDTH)), pl.BlockSpec((L, tn), lambda j: (0, j))],
        out_specs=pl.BlockSpec((S5_WIDTH + CONV_WIDTH, tn), lambda j: (0, j)),
        out_shape=S((S5_WIDTH + CONV_WIDTH, D_MODEL), bf16),
        compiler_params=_cp("parallel"),
    )(y_s5, y_conv, dxb)


def _mix_out_bwd(dx, w_out, tm, ride=()):
    L = dx.shape[0]

    def body(dx_ref, w_ref, dys_ref, dyc_ref, dxb_ref):
        dxb = dx_ref[...].astype(bf16)
        dxb_ref[...] = dxb
        dys_ref[...] = _dot(dxb, w_ref[:S5_WIDTH, :], NT)
        dyc_ref[...] = _dot(dxb, w_ref[S5_WIDTH:, :], NT)

    row = lambda c: pl.BlockSpec((tm, c), lambda i: (i, 0))
    return _pallas(
        body, ride=ride, name="mix_out_bwd", grid=(L // tm,),
        in_specs=[row(D_MODEL), _full((D_MODEL, D_MODEL))],
        out_specs=[row(S5_WIDTH), row(CONV_WIDTH), row(D_MODEL)],
        out_shape=[S((L, S5_WIDTH), f32), S((L, CONV_WIDTH), f32), S((L, D_MODEL), bf16)],
        compiler_params=_cp("parallel"),
    )(dx, w_out)


def _s5_discretise(lam_re, lam_im, log_dt, b_re, b_im):
    dt = jnp.exp(log_dt)
    mag = jnp.exp(lam_re * dt)
    abar_re = mag * jnp.cos(lam_im * dt)
    abar_im = mag * jnp.sin(lam_im * dt)
    den = lam_re * lam_re + lam_im * lam_im
    num_re = abar_re - 1.0
    f_re = ((num_re * lam_re + abar_im * lam_im) / den)[:, None, :]
    f_im = ((abar_im * lam_re - num_re * lam_im) / den)[:, None, :]
    return abar_re, abar_im, f_re * b_re - f_im * b_im, f_re * b_im + f_im * b_re


def _s5_params(lam_re, lam_im, log_dt, b_re, b_im):
    def body(lr, li, ld, br, bi, ar_ref, ai_ref, bbr_ref, bbi_ref):
        ar, ai, bbr, bbi = _s5_discretise(lr[...], li[...], ld[...], br[...], bi[...])
        ar_ref[...], ai_ref[...], bbr_ref[...], bbi_ref[...] = ar, ai, bbr, bbi

    gp = S((S5_GROUPS, S5_STATE), f32)
    gcp = S((S5_GROUPS, S5_GROUP_CH, S5_STATE), f32)
    return pl.pallas_call(body, name="s5_params", out_shape=[gp, gp, gcp, gcp])(lam_re, lam_im, log_dt, b_re, b_im)


def _s5_params_bwd(lam_re, lam_im, log_dt, b_re, b_im, d_ar, d_ai, d_bbr, d_bbi):
    def body(lr, li, ld, br, bi, car, cai, cbr, cbi, o_lr, o_li, o_ld, o_br, o_bi):
        _, vjp = jax.vjp(_s5_discretise, lr[...], li[...], ld[...], br[...], bi[...])
        o_lr[...], o_li[...], o_ld[...], o_br[...], o_bi[...] = vjp((car[...], cai[...], cbr[...], cbi[...]))

    gp = S((S5_GROUPS, S5_STATE), f32)
    gcp = S((S5_GROUPS, S5_GROUP_CH, S5_STATE), f32)
    return pl.pallas_call(body, name="s5_params_bwd", out_shape=[gp, gp, S((S5_GROUPS, 1), f32), gcp, gcp])(
        lam_re, lam_im, log_dt, b_re, b_im, d_ar, d_ai, d_bbr, d_bbi)


def _cmul(ar, ai, br, bi):
    return ar * br - ai * bi, ar * bi + ai * br


def _segment_starts(er, ei, ar, ai, steps, reverse):
    pr, pi = ar, ai
    n = 1
    while n < steps:
        pr, pi = _cmul(pr, pi, pr, pi)
        n *= 2
    assert n == steps
    row = lax.broadcasted_iota(jnp.int32, (SEGMENTS, SCAN_LANES), 0)
    hr = jnp.zeros((1, SCAN_LANES), f32)
    hi = jnp.zeros((1, SCAN_LANES), f32)
    out_r = jnp.zeros((SEGMENTS, SCAN_LANES), f32)
    out_i = jnp.zeros((SEGMENTS, SCAN_LANES), f32)
    order = range(SEGMENTS - 1, 0, -1) if reverse else range(0, SEGMENTS - 1)
    for r in order:
        qr, qi = _cmul(pr, pi, hr, hi)
        hr, hi = qr + er[r:r + 1, :], qi + ei[r:r + 1, :]
        nxt = r - 1 if reverse else r + 1
        out_r = jnp.where(row == nxt, hr, out_r)
        out_i = jnp.where(row == nxt, hi, out_i)
    return out_r, out_i


def _s5_read_bwd(dout, y_lin, u, d_skip, w_glu, b_glu, tm):
    L = u.shape[0]

    def body(do_ref, yl_ref, u_ref, d_ref, w_ref, b_ref, dyl_ref, du_ref, dd_ref, dw_ref, db_ref):
        @pl.when(pl.program_id(0) == 0)
        def _():
            dd_ref[...] = jnp.zeros_like(dd_ref)
            dw_ref[...] = jnp.zeros_like(dw_ref)
            db_ref[...] = jnp.zeros_like(db_ref)

        u, d, dout = u_ref[...], d_ref[...], do_ref[...]
        y, gelu_vjp = jax.vjp(_gelu, yl_ref[...] + d * u)
        yb = y.astype(bf16)
        sig = _sigmoid(_dot(yb, w_ref[...]) + b_ref[...])
        dz = dout * y * sig * (1.0 - sig)
        dzb = dz.astype(bf16)
        dy = dout * sig + _dot(dzb, w_ref[...], NT)
        (dyp,) = gelu_vjp(dy)
        dyl_ref[...] = dyp.astype(bf16)
        du_ref[...] = d * dyp
        dd_ref[...] += _rows8(dyp * u)
        db_ref[...] += _rows8(dz)
        dw_ref[...] += _dot(yb, dzb, TN)

    row = pl.BlockSpec((tm, S5_WIDTH), lambda i: (i, 0))
    vec = _full((1, S5_WIDTH))
    part = _full((8, S5_WIDTH))
    return pl.pallas_call(
        body, name="s5_read_bwd", grid=(L // tm,),
        in_specs=[row, row, row, vec, _full((S5_WIDTH, S5_WIDTH)), vec],
        out_specs=[row, row, part, _full((S5_WIDTH, S5_WIDTH)), part],
        out_shape=[S((L, S5_WIDTH), bf16), S((L, S5_WIDTH), f32), S((8, S5_WIDTH), f32),
                   S((S5_WIDTH, S5_WIDTH), f32), S((8, S5_WIDTH), f32)],
        compiler_params=_cp("arbitrary"),
    )(dout, y_lin, u, d_skip, w_glu, b_glu)


S5_CHUNK_CH = SCAN_LANES // S5_STATE * S5_GROUP_CH


def _s5_two_phase(L, bi):
    rows = bi * SEGMENTS
    nb = L // rows
    whole = pltpu.VMEM((L // SEGMENTS, SEGMENTS, SCAN_LANES), f32)
    mat = pl.BlockSpec((S5_CHUNK_CH, SCAN_LANES), lambda c, j: (c, c))
    vec = pl.BlockSpec((1, SCAN_LANES), lambda c, j: (0, c))
    tile = pl.BlockSpec((SEGMENTS, SCAN_LANES), lambda c, j: (0, c))
    return rows, nb, whole, mat, vec, tile


def _s5_forward(u, a_re, a_im, bb_re, bb_im, cc_re, cc_im, bi, ride=()):
    L = u.shape[0]
    rows, nb, whole, mat, vec, _ = _s5_two_phase(L, bi)

    def body(u_ref, ar_ref, ai_ref, br_ref, bi_ref, cr_ref, ci_ref, sr_ref, si_ref, yl_ref, hr_ref, hi_ref, dr_ref, di_ref):
        j = pl.program_id(1)
        ar = jnp.broadcast_to(ar_ref[...], (SEGMENTS, SCAN_LANES))
        ai = jnp.broadcast_to(ai_ref[...], (SEGMENTS, SCAN_LANES))

        @pl.when(j == 0)
        def _():
            hr_ref[...] = jnp.zeros_like(hr_ref)
            hi_ref[...] = jnp.zeros_like(hi_ref)

        @pl.when(j < nb)
        def _():
            base = j * bi
            ub = u_ref[...].astype(bf16)
            dr_ref[pl.ds(base, bi)] = _dot(ub, br_ref[...]).reshape(bi, SEGMENTS, SCAN_LANES)
            di_ref[pl.ds(base, bi)] = _dot(ub, bi_ref[...]).reshape(bi, SEGMENTS, SCAN_LANES)

            def step(i, c):
                pr, pi = _cmul(ar, ai, c[0], c[1])
                return pr + dr_ref[base + i], pi + di_ref[base + i]

            hr_ref[...], hi_ref[...] = lax.fori_loop(0, bi, step, (hr_ref[...], hi_ref[...]), unroll=True)

        @pl.when(j == nb - 1)
        def _():
            hr_ref[...], hi_ref[...] = _segment_starts(hr_ref[...], hi_ref[...], ar_ref[...], ai_ref[...], L // SEGMENTS, False)

        @pl.when(j >= nb)
        def _():
            base = (j - nb) * bi

            def step(i, c):
                pr, pi = _cmul(ar, ai, c[0], c[1])
                nr, nim = pr + dr_ref[base + i], pi + di_ref[base + i]
                dr_ref[base + i] = nr
                di_ref[base + i] = nim
                return nr, nim

            hr_ref[...], hi_ref[...] = lax.fori_loop(0, bi, step, (hr_ref[...], hi_ref[...]), unroll=True)
            sr = dr_ref[pl.ds(base, bi)].reshape(rows, SCAN_LANES).astype(bf16)
            si = di_ref[pl.ds(base, bi)].reshape(rows, SCAN_LANES).astype(bf16)
            sr_ref[...] = sr
            si_ref[...] = si
            yl_ref[...] = _dot(sr, cr_ref[...], NT) - _dot(si, ci_ref[...], NT)

    u_spec = pl.BlockSpec((rows, S5_CHUNK_CH), lambda c, j: (jnp.minimum(j, nb - 1), c))
    late = lambda width: pl.BlockSpec((rows, width), lambda c, j: (jnp.maximum(j - nb, 0), c))
    return _pallas(
        body, ride=ride, name="s5_forward", grid=(S5_LANES // SCAN_LANES, 2 * nb),
        in_specs=[u_spec, vec, vec, mat, mat, mat, mat],
        out_specs=[late(SCAN_LANES), late(SCAN_LANES), late(S5_CHUNK_CH)],
        out_shape=[S((L, S5_LANES), bf16)] * 2 + [S((L, S5_WIDTH), f32)],
        scratch_shapes=[pltpu.VMEM((SEGMENTS, SCAN_LANES), f32)] * 2 + [whole] * 2,
        compiler_params=_cp("parallel", "arbitrary"),
    )(u, a_re, a_im, bb_re, bb_im, cc_re, cc_im)


def _s5_backward(dy, u, du_skip, s_re, s_im, a_re, a_im, bb_re, bb_im, cc_re, cc_im, bi, ride=()):
    L = u.shape[0]
    rows, nb, whole, mat, vec, tile = _s5_two_phase(L, bi)
    per = rows // 16

    def body(dy_ref, u_ref, dus_ref, sr_ref, si_ref, pr_ref, pi_ref, lr_ref, li_ref, ar_ref, ai_ref, br_ref, bi_ref, cr_ref,
             ci_ref, du_ref, dar_ref, dai_ref, dbr_ref, dbi_ref, dcr_ref, dci_ref, hr_ref, hi_ref, gr_ref, gi_ref, fr_ref, fi_ref):
        j = pl.program_id(1)
        ar = jnp.broadcast_to(ar_ref[...], (SEGMENTS, SCAN_LANES))
        ai = jnp.broadcast_to(ai_ref[...], (SEGMENTS, SCAN_LANES))

        @pl.when(j == 0)
        def _():
            for ref in (hr_ref, hi_ref, dar_ref, dai_ref, dbr_ref, dbi_ref, dcr_ref, dci_ref):
                ref[...] = jnp.zeros_like(ref)

        @pl.when(j < nb)
        def _():
            base = (nb - 1 - j) * bi
            dy = dy_ref[...]
            gr_ref[pl.ds(base, bi)] = _dot(dy, cr_ref[...]).reshape(bi, SEGMENTS, SCAN_LANES)
            gi_ref[pl.ds(base, bi)] = (-_dot(dy, ci_ref[...])).reshape(bi, SEGMENTS, SCAN_LANES)

            def step(n, c):
                i = base + bi - 1 - n
                qr, qi = _cmul(ar, ai, c[0], c[1])
                return qr + gr_ref[i], qi + gi_ref[i]

            hr_ref[...], hi_ref[...] = lax.fori_loop(0, bi, step, (hr_ref[...], hi_ref[...]), unroll=True)

        @pl.when(j == nb - 1)
        def _():
            hr_ref[...], hi_ref[...] = _segment_starts(hr_ref[...], hi_ref[...], ar_ref[...], ai_ref[...], L // SEGMENTS, True)

        @pl.when(j >= nb)
        def _():
            blk = 2 * nb - 1 - j
            base = blk * bi
            sr, si = sr_ref[...], si_ref[...]
            fr_ref[...] = sr.astype(f32).reshape(bi, SEGMENTS, SCAN_LANES)
            fi_ref[...] = si.astype(f32).reshape(bi, SEGMENTS, SCAN_LANES)

            def step(n, c):
                i = bi - 1 - n
                gr, gi, accr, acci = c
                qr, qi = _cmul(ar, ai, gr, gi)
                gr, gi = qr + gr_ref[base + i], qi + gi_ref[base + i]
                gr_ref[base + i] = gr
                gi_ref[base + i] = gi
                pr, pi = fr_ref[i - 1], fi_ref[i - 1]
                return gr, gi, accr + (gr * pr + gi * pi), acci + (gi * pr - gr * pi)

            gr, gi, accr, acci = lax.fori_loop(0, bi - 1, step, (hr_ref[...], hi_ref[...], dar_ref[...], dai_ref[...]), unroll=True)
            qr, qi = _cmul(ar, ai, gr, gi)
            gr, gi = qr + gr_ref[base], qi + gi_ref[base]
            gr_ref[base] = gr
            gi_ref[base] = gi
            hr_ref[...], hi_ref[...] = gr, gi
            row = lax.broadcasted_iota(jnp.int32, (SEGMENTS, SCAN_LANES), 0)
            older = lambda ref: ref[...].astype(f32)[SEGMENTS:, :]
            wrap_r = jnp.where(row == 0, 0.0, pltpu.roll(older(lr_ref), 1, 0))
            wrap_i = jnp.where(row == 0, 0.0, pltpu.roll(older(li_ref), 1, 0))
            pr = jnp.where(blk == 0, wrap_r, older(pr_ref))
            pi = jnp.where(blk == 0, wrap_i, older(pi_ref))
            dar_ref[...] = accr + gr * pr + gi * pi
            dai_ref[...] = acci + gi * pr - gr * pi

            g_re = gr_ref[pl.ds(base, bi)].reshape(rows, SCAN_LANES).astype(bf16)
            g_im = gi_ref[pl.ds(base, bi)].reshape(rows, SCAN_LANES).astype(bf16)
            ub = u_ref[...].astype(bf16)
            dy = dy_ref[...]
            du_ref[...] = dus_ref[...] + _dot(g_re, br_ref[...], NT) + _dot(g_im, bi_ref[...], NT)
            dbr_ref[...] += _dot(ub, g_re, TN)
            dbi_ref[...] += _dot(ub, g_im, TN)
            dcr_ref[...] += _dot(dy, sr, TN)
            dci_ref[...] -= _dot(dy, si, TN)

    block = lambda c, j: jnp.where(j < nb, nb - 1 - j, 2 * nb - 1 - j)
    late_block = lambda c, j: jnp.minimum(2 * nb - 1 - j, nb - 1)
    both = pl.BlockSpec((rows, S5_CHUNK_CH), lambda c, j: (block(c, j), c))
    chan = pl.BlockSpec((rows, S5_CHUNK_CH), lambda c, j: (late_block(c, j), c))
    state = pl.BlockSpec((rows, SCAN_LANES), lambda c, j: (late_block(c, j), c))
    prev = pl.BlockSpec((16, SCAN_LANES), lambda c, j: (jnp.maximum(late_block(c, j) * per - 1, 0), c))
    last = pl.BlockSpec((16, SCAN_LANES), lambda c, j: (L // 16 - 1, c))
    grad = pl.BlockSpec((S5_CHUNK_CH, SCAN_LANES), lambda c, j: (c, 0))
    return _pallas(
        body, ride=ride, name="s5_backward", grid=(S5_LANES // SCAN_LANES, 2 * nb),
        in_specs=[both, chan, chan, state, state, prev, prev, last, last, vec, vec, mat, mat, mat, mat],
        out_specs=[chan, tile, tile, grad, grad, grad, grad],
        out_shape=[S((L, S5_WIDTH), f32)] + [S((SEGMENTS, S5_LANES), f32)] * 2 + [S((S5_WIDTH, SCAN_LANES), f32)] * 4,
        scratch_shapes=[pltpu.VMEM((SEGMENTS, SCAN_LANES), f32)] * 2 + [whole] * 2 + [pltpu.VMEM((bi, SEGMENTS, SCAN_LANES), f32)] * 2,
        compiler_params=_cp("parallel", "arbitrary"),
    )(dy, u, du_skip, s_re, s_im, s_re, s_im, s_re, s_im, a_re, a_im, bb_re, bb_im, cc_re, cc_im)


def _s5_gate(y_lin, u, d_skip, w_glu, b_glu, tm, ride=()):
    L = u.shape[0]

    def body(yl_ref, u_ref, d_ref, w_ref, b_ref, o_ref):
        y = _gelu(yl_ref[...] + d_ref[...] * u_ref[...])
        z = _dot(y.astype(bf16), w_ref[...]) + b_ref[...]
        o_ref[...] = (y * _sigmoid(z)).astype(bf16)

    row = pl.BlockSpec((tm, S5_WIDTH), lambda i: (i, 0))
    vec = _full((1, S5_WIDTH))
    return _pallas(
        body, ride=ride, name="s5_gate", grid=(L // tm,),
        in_specs=[row, row, vec, _full((S5_WIDTH, S5_WIDTH)), vec],
        out_specs=row, out_shape=S((L, S5_WIDTH), bf16),
        compiler_params=_cp("parallel"),
    )(y_lin, u, d_skip, w_glu, b_glu)


def _group_mean(x, avg):
    return _dot(x.astype(bf16), avg)


def _conv_act(zn, ln_g, ln_b):
    t = zn * ln_g + ln_b
    return t * _sigmoid(t)


def _glu_padded(v_ref, halo_ref, zpad_ref, tm):
    v = v_ref[...]
    vh = halo_ref[...]
    zh = vh[:, :CONV_WIDTH] * _sigmoid(vh[:, CONV_WIDTH:])
    zpad_ref[:CONV_HALO, :] = jnp.where(pl.program_id(0) > 0, zh, 0.0)
    zpad_ref[CONV_HALO:CONV_HALO + tm, :] = v[:, :CONV_WIDTH] * _sigmoid(v[:, CONV_WIDTH:])
    zpad_ref[CONV_HALO + tm:, :] = jnp.zeros((8, CONV_WIDTH), f32)


def _shifted(pad_ref, sh_ref, tm):
    for b in range(8):
        sh_ref[b] = pad_ref[pl.ds(b, tm + CONV_HALO), :]


def _window(sh_ref, r0, off, rows):
    start = r0 + 8 * (off // 8)
    return sh_ref[off % 8, pl.ds(start if isinstance(start, int) else pl.multiple_of(start, 8), rows), :]


def _tap_sum(w_ref, sh_ref, taps, out_ref, tm, bias):
    for r0 in range(0, tm, CONV_ROWS):
        acc = jnp.zeros((CONV_ROWS, CONV_WIDTH), f32) + bias
        for k, off in taps:
            acc = acc + w_ref[k:k + 1, :] * _window(sh_ref, r0, off, CONV_ROWS)
        out_ref[r0:r0 + CONV_ROWS, :] = acc


FWD_TAPS = [(k, CONV_HALO - (CONV_K - 1) + k) for k in range(CONV_K)]
BWD_TAPS = [(k, CONV_K - 1 - k) for k in range(CONV_K)]


def _conv_specs(tm):
    per = tm // CONV_HALO
    vrow = pl.BlockSpec((tm, 2 * CONV_WIDTH), lambda i: (i, 0))
    vhalo = pl.BlockSpec((CONV_HALO, 2 * CONV_WIDTH), lambda i: (jnp.maximum(i * per - 1, 0), 0))
    return vrow, vhalo


def _conv_scratch(tm):
    return [pltpu.VMEM((tm + CONV_HALO + 8, CONV_WIDTH), f32), pltpu.VMEM((8, tm + CONV_HALO, CONV_WIDTH), f32)]


def _conv_fwd(v, w_dw, b_dw, ln_g, ln_b, avg, tm, ride=()):
    L = v.shape[0]

    def body(v_ref, halo_ref, w_ref, b_ref, g_ref, bb_ref, avg_ref, o_ref, zc_ref, zpad_ref, zs_ref):
        _glu_padded(v_ref, halo_ref, zpad_ref, tm)
        _shifted(zpad_ref, zs_ref, tm)
        _tap_sum(w_ref, zs_ref, FWD_TAPS, zc_ref, tm, b_ref[...])
        zc = zc_ref[...]
        xc = zc - _group_mean(zc, avg_ref[...])
        zn = xc * lax.rsqrt(_group_mean(xc * xc, avg_ref[...]) + EPS)
        o_ref[...] = _conv_act(zn, g_ref[...], bb_ref[...]).astype(bf16)

    vrow, vhalo = _conv_specs(tm)
    vec = _full((1, CONV_WIDTH))
    row = pl.BlockSpec((tm, CONV_WIDTH), lambda i: (i, 0))
    return _pallas(
        body, ride=ride, name="conv_fwd", grid=(L // tm,),
        in_specs=[vrow, vhalo, _full((CONV_HALO, CONV_WIDTH)), vec, vec, vec, _full((CONV_WIDTH, CONV_WIDTH))],
        out_specs=[row, row], out_shape=[S((L, CONV_WIDTH), bf16), S((L, CONV_WIDTH), f32)],
        scratch_shapes=_conv_scratch(tm),
        compiler_params=_cp("arbitrary"),
    )(v, v, w_dw, b_dw, ln_g, ln_b, avg)


def _conv_bwd_norm(dout, zc, ln_g, ln_b, avg, tm):
    L = zc.shape[0]

    def body(do_ref, zc_ref, g_ref, bb_ref, avg_ref, dzc_ref, dg_ref, db_ref, dbd_ref):
        @pl.when(pl.program_id(0) == 0)
        def _():
            dg_ref[...] = jnp.zeros_like(dg_ref)
            db_ref[...] = jnp.zeros_like(db_ref)
            dbd_ref[...] = jnp.zeros_like(dbd_ref)

        avg = avg_ref[...]
        zc = zc_ref[...]
        xc = zc - _group_mean(zc, avg)
        rstd = lax.rsqrt(_group_mean(xc * xc, avg) + EPS)
        xhat = xc * rstd
        _, act_vjp = jax.vjp(_conv_act, xhat, g_ref[...], bb_ref[...])
        dxhat, dg, db = act_vjp(do_ref[...])
        dzc = rstd * (dxhat - _group_mean(dxhat, avg) - xhat * _group_mean(dxhat * xhat, avg))
        dzc_ref[...] = dzc
        dg_ref[0:1, :] += dg
        db_ref[0:1, :] += db
        dbd_ref[...] += _rows8(dzc)

    vec = _full((1, CONV_WIDTH))
    row = pl.BlockSpec((tm, CONV_WIDTH), lambda i: (i, 0))
    part = _full((8, CONV_WIDTH))
    return pl.pallas_call(
        body, name="conv_bwd_norm", grid=(L // tm,),
        in_specs=[row, row, vec, vec, _full((CONV_WIDTH, CONV_WIDTH))],
        out_specs=[row, part, part, part],
        out_shape=[S((L, CONV_WIDTH), f32)] + [S((8, CONV_WIDTH), f32)] * 3,
        compiler_params=_cp("arbitrary"),
    )(dout, zc, ln_g, ln_b, avg)


def _conv_bwd_taps(dzc, v, w_dw, tm, ride=()):
    L = v.shape[0]
    nt = L // tm
    per = tm // CONV_HALO

    def body(d_ref, dn_ref, v_ref, w_ref, dv_ref, dw_ref, dpad_ref, ds_ref, dz_ref, z_ref):
        i = pl.program_id(0)

        @pl.when(i == 0)
        def _():
            dw_ref[...] = jnp.zeros_like(dw_ref)

        v = v_ref[...]
        sig = _sigmoid(v[:, CONV_WIDTH:])
        z_ref[...] = v[:, :CONV_WIDTH] * sig
        dpad_ref[:tm, :] = d_ref[...]
        dpad_ref[tm:tm + CONV_HALO, :] = jnp.where(i < nt - 1, dn_ref[...], 0.0)
        dpad_ref[tm + CONV_HALO:, :] = jnp.zeros((8, CONV_WIDTH), f32)
        _shifted(dpad_ref, ds_ref, tm)
        _tap_sum(w_ref, ds_ref, BWD_TAPS, dz_ref, tm, 0.0)

        for first in range(0, CONV_K, 8):
            taps = BWD_TAPS[first:first + 8]

            accs = [jnp.zeros((8, CONV_WIDTH), f32) for _ in taps]
            for r0 in range(0, tm, 8):
                z = z_ref[r0:r0 + 8, :]
                accs = [acc + z * _window(ds_ref, r0, off, 8) for acc, (_, off) in zip(accs, taps)]
            for acc, (k, _) in zip(accs, taps):
                dw_ref[k] += acc

        dz = dz_ref[...]
        dv_ref[:, :CONV_WIDTH] = dz * sig
        dv_ref[:, CONV_WIDTH:] = dz * v[:, :CONV_WIDTH] * sig * (1.0 - sig)

    vrow, _ = _conv_specs(tm)
    row = pl.BlockSpec((tm, CONV_WIDTH), lambda i: (i, 0))
    nxt = pl.BlockSpec((CONV_HALO, CONV_WIDTH), lambda i: (jnp.minimum((i + 1) * per, nt * per - 1), 0))
    return _pallas(
        body, ride=ride, name="conv_bwd_taps", grid=(nt,),
        in_specs=[row, nxt, vrow, _full((CONV_HALO, CONV_WIDTH))],
        out_specs=[vrow, _full((CONV_HALO, 8, CONV_WIDTH))],
        out_shape=[S((L, 2 * CONV_WIDTH), f32), S((CONV_HALO, 8, CONV_WIDTH), f32)],
        scratch_shapes=_conv_scratch(tm) + [pltpu.VMEM((tm, CONV_WIDTH), f32)] * 2,
        compiler_params=_cp("arbitrary"),
    )(dzc, dzc, v, w_dw)


def _to_segments(a):
    L, c = a.shape
    return a.reshape(SEGMENTS, L // SEGMENTS, c).transpose(1, 0, 2).reshape(L, c)


def _from_segments(a):
    L, c = a.shape
    return a.reshape(L // SEGMENTS, SEGMENTS, c).transpose(1, 0, 2).reshape(L, c)


def _block_diag(ms):
    n = len(ms)

    def body(*refs):
        for a in range(n):
            out = refs[n + a]
            out[...] = jnp.zeros_like(out)
            for g in range(S5_GROUPS):
                rows = slice(g * S5_GROUP_CH, (g + 1) * S5_GROUP_CH)
                out[rows, g * S5_STATE:(g + 1) * S5_STATE] = refs[a][rows, :].astype(bf16)

    return pl.pallas_call(body, name="s5_block_diag", out_shape=[S((S5_WIDTH, S5_LANES), bf16)] * n,
                          compiler_params=pltpu.CompilerParams(vmem_limit_bytes=VMEM_LIMIT))(
        *[m.reshape(S5_WIDTH, S5_STATE) for m in ms])


def _diag_blocks(ms):
    n = len(ms)
    per_chunk = SCAN_LANES // S5_STATE

    def body(*refs):
        for a in range(n):
            for g in range(S5_GROUPS):
                rows = slice(g * S5_GROUP_CH, (g + 1) * S5_GROUP_CH)
                at = g % per_chunk * S5_STATE
                refs[n + a][rows, :] = refs[a][rows, at:at + S5_STATE]

    out = pl.pallas_call(body, name="s5_diag_blocks", out_shape=[S((S5_WIDTH, S5_STATE), f32)] * n,
                         compiler_params=pltpu.CompilerParams(vmem_limit_bytes=VMEM_LIMIT))(*ms)
    return [o.reshape(S5_GROUPS, S5_GROUP_CH, S5_STATE) for o in out]


class _NoExchanges:
    def before(self, point):
        return ()

    def after(self, point):
        pass

    def alone(self, point):
        pass


def _ffn_block(x, p, tag, tm, sched, head=None, mixed=None, mixer=None):
    point = tag + "_up"
    h, dadg, dadu, a, *x_in = _ffn_up(x, p[tag + "_norm"], p[tag + "_w_gate"], p[tag + "_w_up"], tm, tag, mixed,
                                      ride=sched.before(point))
    sched.after(point)
    x, = x_in or [x]
    if head is None:
        out = _ffn_down(x, a, p[tag + "_w_down"], tm, tag, mixer and tuple(p[n] for n in mixer), ride=sched.before(tag + "_down"))
        sched.after(tag + "_down")
    else:
        out = _ffn_down_loss(x, a, p[tag + "_w_down"], *head, tm, tag)
    return out, (x, h, dadg, dadu, a)


def _ffn_block_bwd(dxo, x, p, tag, saved, tm, grads, sched, parts=1, dxh=None):
    _, h, dadg, dadu, a = saved

    def weight_grad(which, lhs, rhs):
        point = tag + "_dw_" + which
        grads[tag + "_w_" + which] = _mm_tn(lhs, rhs, bf16, point, ride=sched.before(point))
        sched.after(point)

    if dxh is not None:
        weight_grad("down", a, dxh)
    dgate, dup, own_dxh = _ffn_bwd_act(dxo, p[tag + "_w_down"], dadg, dadu, tm, tag, ride=sched.before(tag + "_bwd_act"))
    sched.after(tag + "_bwd_act")
    weight_grad("gate", dgate, h)
    weight_grad("up", dup, h)
    if dxh is None:
        weight_grad("down", a, own_dxh)
    tiles = x.shape[0] // tm
    dx, dgs = None, []
    for k in range(parts):
        point = tag + "_bwd_in" + ("_%d" % k) * (parts > 1)
        dx, dg = _ffn_bwd_in(dxo, x, p[tag + "_norm"], dgate, dup, p[tag + "_w_gate"], p[tag + "_w_up"], tm, point,
                             tiles=(k * tiles // parts, tiles // parts), into=dx, ride=sched.before(point))
        sched.after(point)
        dgs.append(dg)
    grads[tag + "_norm"] = functools.reduce(jnp.add, dgs)
    return dx


def _local_step(x, target, p, grads, sched):
    L = x.shape[0]
    tm = min(512, L // 2)
    ni = L // SEGMENTS
    bi = min(64, ni)

    def carried(point, fn, *args):
        out = fn(*args, ride=sched.before(point))
        sched.after(point)
        return out

    (x1, h2, u_s5, v), saved1 = _ffn_block(x, p, "ffn1", tm, sched, mixer=("mix_norm", "w_in"))

    s5_in = (p["s5_lam_re"], p["s5_lam_im"], p["s5_log_dt"].reshape(S5_GROUPS, 1), p["s5_b_re"], p["s5_b_im"])
    abar_re, abar_im, bbar_re, bbar_im = _s5_params(*s5_in)
    a_re, a_im = abar_re.reshape(1, S5_LANES), abar_im.reshape(1, S5_LANES)
    bb_re, bb_im, cc_re, cc_im = _block_diag([bbar_re, bbar_im, p["s5_c_re"], p["s5_c_im"]])
    u_seg = _to_segments(u_s5)
    s_re, s_im, y_lin = carried("s5_forward", _s5_forward, u_seg, a_re, a_im, bb_re, bb_im, cc_re, cc_im, bi)
    y_s5 = _from_segments(_s5_gate(y_lin, u_seg, p["s5_d"], p["s5_w_glu"], p["s5_b_glu"], tm))
    w_dw = jnp.pad(p["conv_w_dw"], ((0, CONV_HALO - CONV_K), (0, 0)))
    heads = jnp.arange(CONV_WIDTH) // CONV_HEAD
    avg = ((heads[:, None] == heads[None, :]).astype(f32) / CONV_HEAD).astype(bf16)
    y_conv, zc = carried("conv_fwd", _conv_fwd, v, w_dw, p["conv_b_dw"], p["conv_ln_g"], p["conv_ln_b"], avg, tm)

    (dx3, grads["final_norm"], loss_terms), saved2 = _ffn_block(
        x1, p, "ffn2", tm, sched, head=(target, p["final_norm"].reshape(1, D_MODEL)), mixed=(y_s5, y_conv, p["w_out"]))
    x2 = saved2[0]
    grads["loss_terms"] = loss_terms

    dx2 = _ffn_block_bwd(dx3, x2, p, "ffn2", saved2, tm, grads, sched)

    dy_s5, dy_conv, dx2b = carried("mix_out_bwd", _mix_out_bwd, dx2, p["w_out"], tm)
    grads["w_out"] = _dw_out(y_s5, y_conv, dx2b)
    dy_lin, du_skip, dd8, grads["s5_w_glu"], dbg8 = _s5_read_bwd(
        _to_segments(dy_s5), y_lin, u_seg, p["s5_d"], p["s5_w_glu"], p["s5_b_glu"], tm)
    grads["s5_d"] = dd8.sum(axis=0, keepdims=True)
    grads["s5_b_glu"] = dbg8.sum(axis=0, keepdims=True)
    du_seg, da_re8, da_im8, dbb_re, dbb_im, dcc_re, dcc_im = carried(
        "s5_backward", _s5_backward, dy_lin, u_seg, du_skip, s_re, s_im, a_re, -a_im, bb_re, bb_im, cc_re, cc_im, bi)
    d_abar = lambda a8: a8.sum(axis=0).reshape(S5_GROUPS, S5_STATE)
    grads["s5_c_re"], grads["s5_c_im"], d_bbr, d_bbi = _diag_blocks([dcc_re, dcc_im, dbb_re, dbb_im])
    d_lr, d_li, d_ld, d_br, d_bi = _s5_params_bwd(*s5_in, d_abar(da_re8), d_abar(da_im8), d_bbr, d_bbi)
    grads["s5_lam_re"], grads["s5_lam_im"], grads["s5_log_dt"] = d_lr, d_li, d_ld.reshape(1, S5_GROUPS)
    grads["s5_b_re"], grads["s5_b_im"] = d_br, d_bi
    dzc, dlg8, dlb8, dbd8 = _conv_bwd_norm(dy_conv, zc, p["conv_ln_g"], p["conv_ln_b"], avg, tm)
    grads["conv_ln_g"] = dlg8.sum(axis=0, keepdims=True)
    grads["conv_ln_b"] = dlb8.sum(axis=0, keepdims=True)
    grads["conv_b_dw"] = dbd8.sum(axis=0, keepdims=True)
    dv, dw8 = carried("conv_bwd_taps", _conv_bwd_taps, dzc, v, w_dw, tm)
    grads["conv_w_dw"] = dw8.sum(axis=1)[:CONV_K]
    dx1, grads["mix_norm"], dub, dx1h = carried("mix_in_bwd", _mix_in_bwd, dx2, x1, p["mix_norm"], _from_segments(du_seg), dv,
                                                p["w_in"], tm)
    grads["w_in"] = _mm_tn(dub, h2, bf16, "dw_in")

    dx0 = _ffn_block_bwd(dx1, x, p, "ffn1", saved1, tm, grads, sched, parts=min(2, L // tm), dxh=dx1h)
    sched.alone("tail")
    return loss_terms, dx0


MESH = pl.DeviceIdType.MESH
ANY = pl.BlockSpec(memory_space=pl.ANY)


def _place():
    return lax.axis_index("x"), lax.axis_index("y"), lax.axis_index("c")


class _Exchange:
    def __init__(self, ins, out_shape, sems, start, finish):
        self.ins, self.out_shape, self.sems, self.start, self.finish = list(ins), list(out_shape), list(sems), start, finish
        self.out = None


def _pallas(body, *, ride=(), **kw):
    if not ride:
        return pl.pallas_call(body, **kw)

    def run(*args):
        out_shape = kw.get("out_shape", [])
        single = not isinstance(out_shape, (list, tuple))
        shapes = [out_shape] if single else list(out_shape)
        out_specs = [kw["out_specs"]] if single else list(kw.get("out_specs", []))
        grid = tuple(kw.get("grid", ()))
        scratch = list(kw.get("scratch_shapes", ()))
        n_in, n_out, n_scr = len(args), len(shapes), len(scratch)
        r_in = [len(e.ins) for e in ride]
        r_out = [len(e.out_shape) for e in ride]
        r_sem = [len(e.sems) for e in ride]

        def wrapped(*refs):
            own_in, refs = refs[:n_in], refs[n_in:]
            ex_in, refs = refs[:sum(r_in)], refs[sum(r_in):]
            own_out, refs = refs[:n_out], refs[n_out:]
            ex_out, refs = refs[:sum(r_out)], refs[sum(r_out):]
            own_scr, ex_sem = refs[:n_scr], refs[n_scr:]
            parts = []
            for e, ni, no, ns in zip(ride, r_in, r_out, r_sem):
                parts.append((e, ex_in[:ni], ex_out[:no], ex_sem[:ns]))
                ex_in, ex_out, ex_sem = ex_in[ni:], ex_out[no:], ex_sem[ns:]

            def at(step):
                def go():
                    for e, i, o, s in parts:
                        getattr(e, step)(i, o, s)
                if grid:
                    ids = [pl.program_id(d) for d in range(len(grid))]
                    when = [i == (0 if step == "start" else g - 1) for i, g in zip(ids, grid)]
                    pl.when(functools.reduce(lambda a, b: a & b, when))(go)
                else:
                    go()

            at("start")
            if body is not None:
                body(*own_in, *own_out, *own_scr)
            at("finish")

        outs = pl.pallas_call(
            wrapped, name=kw["name"], grid=grid,
            in_specs=list(kw.get("in_specs", [])) + [ANY] * sum(r_in),
            out_specs=out_specs + [ANY] * sum(r_out),
            out_shape=shapes + [s for e in ride for s in e.out_shape],
            scratch_shapes=scratch + [s for e in ride for s in e.sems],
            input_output_aliases=kw.get("input_output_aliases", {}),
            compiler_params=_cp(*["arbitrary"] * len(grid)),
        )(*args, *[a for e in ride for a in e.ins])
        own, rest = outs[:n_out], outs[n_out:]
        for e, no in zip(ride, r_out):
            e.out, rest = list(rest[:no]), rest[no:]
        return own[0] if single else own

    return run


def _exchange(ride, name):
    _pallas(None, ride=ride, name=name)()


def _gather(arrs):
    n = len(arrs)

    def copies(ins, outs, sems):
        send_sems, recv_sems, local_sems = sems
        x, y, c = _place()
        me, sibling = (x, y, c), (x, y, 1 - c)
        chips = [(1 - x, y), (x, 1 - y), (1 - x, 1 - y)]

        def place(a, block):
            return outs[a].at[block]

        def copy(a, k, block, to, src=None):
            px, py, pc = block
            dst = place(a, 4 * px + 2 * py + pc)
            return pltpu.make_async_remote_copy(
                src_ref=dst if src is None else src, dst_ref=dst, send_sem=send_sems.at[7 * a + k],
                recv_sem=recv_sems.at[7 * a + k], device_id=to, device_id_type=MESH)

        def own():
            local = [pltpu.make_async_copy(ins[a], place(a, 4 * x + 2 * y + c), local_sems.at[a]) for a in range(n)]
            remote = []
            for a in range(n):
                remote.append(copy(a, 0, me, sibling, src=ins[a]))
                remote += [copy(a, 1 + j, me, (*chip, c), src=ins[a]) for j, chip in enumerate(chips)]
            return local, remote

        return c, me, sibling, chips, copy, own

    def start(ins, outs, sems):
        local, remote = copies(ins, outs, sems)[-1]()
        for cp in local + remote:
            cp.start()

    def finish(ins, outs, sems):
        c, me, sibling, chips, copy, own = copies(ins, outs, sems)
        passed = []
        for j, chip in enumerate(chips):
            for a in range(n):
                copy(a, 1 + j, (*chip, c), me).wait_recv()
                passed.append(copy(a, 4 + j, (*chip, c), sibling))
                passed[-1].start()
        for a in range(n):
            copy(a, 0, sibling, me).wait_recv()
            for j, chip in enumerate(chips):
                copy(a, 4 + j, (*chip, 1 - c), me).wait_recv()
        local, remote = own()
        for cp in remote + passed:
            cp.wait_send()
        for cp in local:
            cp.wait()

    dma = pltpu.SemaphoreType.DMA
    shapes = [S((N_DEV, *a.shape), a.dtype) for a in arrs]
    return _Exchange(arrs, shapes, [dma((7 * n,)), dma((7 * n,)), dma((n,))], start, finish)


def _swap_with_sibling(gs):
    n = len(gs)

    def copies(ins, outs, sems):
        x, y, c = _place()
        return [pltpu.make_async_remote_copy(
            src_ref=ins[a].at[:, 1 - c], dst_ref=outs[a], send_sem=sems[0].at[a], recv_sem=sems[1].at[a],
            device_id=(x, y, 1 - c), device_id_type=MESH) for a in range(n)]

    def start(ins, outs, sems):
        for cp in copies(ins, outs, sems):
            cp.start()

    def finish(ins, outs, sems):
        for cp in copies(ins, outs, sems):
            cp.wait()

    dma = pltpu.SemaphoreType.DMA
    return _Exchange(gs, [S((N_CHIP, *g.shape[2:]), g.dtype) for g in gs], [dma((n,)), dma((n,))], start, finish)


def _swap_with_chips(ps):
    n = len(ps)

    def copies(ins, outs, sems):
        x, y, c = _place()
        q = 2 * x + y
        peers = [(x, 1 - y), (1 - x, y), (1 - x, 1 - y)]

        def copy(a, j, slot_from, slot_to):
            px, py = peers[j]
            return pltpu.make_async_remote_copy(
                src_ref=ins[a].at[slot_from], dst_ref=outs[a].at[slot_to], send_sem=sems[0].at[3 * a + j],
                recv_sem=sems[1].at[3 * a + j], device_id=(px, py, c), device_id_type=MESH)

        sends = lambda: [copy(a, j, 2 * peers[j][0] + peers[j][1], q) for a in range(n) for j in range(3)]
        lands = lambda: [copy(a, j, q, 2 * peers[j][0] + peers[j][1]) for a in range(n) for j in range(3)]
        return sends, lands

    def start(ins, outs, sems):
        for cp in copies(ins, outs, sems)[0]():
            cp.start()

    def finish(ins, outs, sems):
        sends, lands = copies(ins, outs, sems)
        for cp in lands():
            cp.wait_recv()
        for cp in sends():
            cp.wait_send()

    dma = pltpu.SemaphoreType.DMA
    return _Exchange(ps, [S(p.shape, p.dtype) for p in ps], [dma((3 * n,)), dma((3 * n,))], start, finish)


def _row_tile(rows, cols, itemsize):
    t = rows
    while t * cols * itemsize > (1 << 20) and t % 32 == 0:
        t //= 2
    return t


def _add_sibling(g4, st, core, name):
    _, R, C = st.shape
    tr = _row_tile(R, C, 1)

    def body(c_ref, g_ref, s_ref, o_ref):
        o_ref[...] = (g_ref[...].astype(f32) + s_ref[...].astype(f32)).astype(bf16)

    mine = pl.BlockSpec((None, None, tr, C), lambda q, i, c: (q, c[0], i, 0))
    return pl.pallas_call(
        body, name=name,
        grid_spec=pltpu.PrefetchScalarGridSpec(
            num_scalar_prefetch=1, grid=(N_CHIP, R // tr),
            in_specs=[mine,
                      pl.BlockSpec((None, tr, C), lambda q, i, c: (q, i, 0))],
            out_specs=pl.BlockSpec((None, tr, C), lambda q, i, c: (q, i, 0))),
        out_shape=S((N_CHIP, R, C), bf16),
        compiler_params=_cp("parallel", "parallel"),
    )(core, g4, st)


SMEM = pl.BlockSpec(memory_space=pltpu.SMEM)
VMEM = pl.BlockSpec(memory_space=pltpu.VMEM)


def _add_sibling_small(items, core, name):
    n = len(items)

    def body(c_ref, *refs):
        c = c_ref[0]
        for k in range(n):
            g_ref, s_ref, o_ref = refs[2 * k], refs[2 * k + 1], refs[2 * n + k]
            for q in range(N_CHIP):
                o_ref[q] = (g_ref[q, c].astype(f32) + s_ref[q].astype(f32)).astype(bf16)

    return pl.pallas_call(body, name=name, in_specs=[SMEM] + [VMEM] * (2 * n), out_specs=[VMEM] * n,
                          out_shape=[S(st.shape, bf16) for _, st in items],
                          compiler_params=pltpu.CompilerParams(vmem_limit_bytes=VMEM_LIMIT))(
        core, *[a for item in items for a in item])


def _adam_small(items, slots, name):
    n = len(items)

    def body(s_ref, *refs):
        ins, outs = refs[:5 * n], refs[5 * n:]
        for k in range(n):
            w_ref, m_ref, v_ref, p_ref, got_ref = ins[5 * k:5 * k + 5]
            g = p_ref[s_ref[0]].astype(f32)
            for j in range(1, N_CHIP):
                g = g + got_ref[s_ref[j]].astype(f32)
            outs[4 * k][...] = g
            outs[4 * k + 1][...], outs[4 * k + 2][...], outs[4 * k + 3][...] = _adamw(w_ref[...], g, m_ref[...], v_ref[...])

    out = pl.pallas_call(body, name=name, in_specs=[SMEM] + [VMEM] * (5 * n), out_specs=[VMEM] * (4 * n),
                         out_shape=[S(item[0].shape, f32) for item in items for _ in range(4)],
                         compiler_params=pltpu.CompilerParams(vmem_limit_bytes=VMEM_LIMIT))(
        slots, *[a for item in items for a in item])
    return [out[4 * k:4 * k + 4] for k in range(n)]


def _adamw(w, g, m, v):
    m = B1 * m + (1.0 - B1) * g
    v = B2 * v + (1.0 - B2) * (g * g)
    m_hat = m / (1.0 - B1 ** STEP)
    v_hat = v / (1.0 - B2 ** STEP)
    return -LR * (m_hat / (jnp.sqrt(v_hat) + ADAM_EPS) + WD * w), m, v


def _adam_sharded(items, slots, name):
    n = len(items)
    R, C = items[0][0].shape
    tr = _row_tile(R, C, 4 * n)

    def body(s_ref, *refs):
        ins, outs = refs[:7 * n], refs[7 * n:]
        for k in range(n):
            w_ref, m_ref, v_ref, p_ref, a_ref, b_ref, c_ref = ins[7 * k:7 * k + 7]
            g = p_ref[...].astype(f32) + a_ref[...].astype(f32) + b_ref[...].astype(f32) + c_ref[...].astype(f32)
            outs[4 * k][...] = g
            outs[4 * k + 1][...], outs[4 * k + 2][...], outs[4 * k + 3][...] = _adamw(w_ref[...], g, m_ref[...], v_ref[...])

    shard = pl.BlockSpec((tr, C), lambda i, s: (i, 0))
    slot = lambda k: pl.BlockSpec((None, tr, C), lambda i, s: (s[k], i, 0))
    out = pl.pallas_call(
        body, name=name,
        grid_spec=pltpu.PrefetchScalarGridSpec(
            num_scalar_prefetch=1, grid=(R // tr,),
            in_specs=[shard, shard, shard, slot(0), slot(1), slot(2), slot(3)] * n,
            out_specs=[shard] * (4 * n)),
        out_shape=[S((R, C), f32)] * (4 * n),
        compiler_params=_cp("parallel"),
    )(slots, *[a for w, m, v, part, got in items for a in (w, m, v, part, got, got, got)])
    return [out[4 * k:4 * k + 4] for k in range(n)]


def _adam_replicated(items, stacks, loss_at, name):
    n, ns = len(items), len(stacks)

    def body(*refs):
        stack_refs, ins, outs = refs[:ns], refs[ns:ns + 3 * n], refs[ns + 3 * n:]
        totals = []
        for ref in stack_refs:
            g = ref[0]
            for d in range(1, N_DEV):
                g = g + ref[d]
            totals.append(g)
        for i, (_, _, _, (s, k)) in enumerate(items):
            w_ref, m_ref, v_ref = ins[3 * i:3 * i + 3]
            g = totals[s] if k is None else totals[s][k]
            outs[4 * i][...] = g
            outs[4 * i + 1][...], outs[4 * i + 2][...], outs[4 * i + 3][...] = _adamw(w_ref[...], g, m_ref[...], v_ref[...])
        if loss_at is not None:
            outs[-1][...] = jnp.sum(totals[loss_at[0]], keepdims=True)

    flat = list(stacks) + [a for item in items for a in item[:3]]
    shapes = [S(item[0].shape, f32) for item in items for _ in range(4)] + ([S((1, 1), f32)] if loss_at is not None else [])
    out = pl.pallas_call(body, name=name, out_shape=shapes,
                         compiler_params=pltpu.CompilerParams(vmem_limit_bytes=VMEM_LIMIT))(*flat)
    return [out[4 * i:4 * i + 4] for i in range(n)], (out[-1] if loss_at is not None else None)


WEIGHTS = ["ffn1_norm", "ffn1_w_gate", "ffn1_w_up", "ffn1_w_down", "mix_norm", "w_in", "s5_lam_re", "s5_lam_im", "s5_log_dt",
           "s5_b_re", "s5_b_im", "s5_c_re", "s5_c_im", "s5_d", "s5_w_glu", "s5_b_glu", "conv_w_dw", "conv_b_dw", "conv_ln_g",
           "conv_ln_b", "w_out", "ffn2_norm", "ffn2_w_gate", "ffn2_w_up", "ffn2_w_down", "final_norm"]
SHARDED = ["ffn1_w_gate", "ffn1_w_up", "ffn1_w_down", "w_in", "s5_w_glu", "conv_w_dw", "w_out", "ffn2_w_gate", "ffn2_w_up",
           "ffn2_w_down"]
REPLICATED = [n for n in WEIGHTS if n not in SHARDED]
TRANSPOSED = ["ffn1_w_gate", "ffn1_w_up", "ffn2_w_gate", "ffn2_w_up", "w_in"]


def _shard_to_wire(n, w):
    if n == "conv_w_dw":
        return jnp.pad(w, ((0, CONV_HALO - CONV_K), (0, 0)))
    return w.astype(bf16)


def _to_wire(shards, ride):
    names = list(shards)
    shapes = [jax.eval_shape(functools.partial(_shard_to_wire, n), shards[n]) for n in names]

    def body(*refs):
        for src, dst in zip(refs[:len(names)], refs[len(names):]):
            (r, c), (rp, cp) = src.shape, dst.shape
            dst[:r, :c] = src[...].astype(dst.dtype)
            if cp > c:
                dst[:, c:] = jnp.zeros((rp, cp - c), dst.dtype)
            if rp > r:
                dst[r:, :] = jnp.zeros((rp - r, cp), dst.dtype)

    out = _pallas(body, ride=ride, name="to_wire", out_shape=shapes, in_specs=[pl.BlockSpec(memory_space=pltpu.VMEM)] * len(names),
                  out_specs=[pl.BlockSpec(memory_space=pltpu.VMEM)] * len(names))(*[shards[n] for n in names])
    return dict(zip(names, out))


def _gathered_to_full(n, g):
    if n == "conv_w_dw":
        return g.transpose(1, 0, 2).reshape(CONV_HALO, CONV_WIDTH)[:CONV_K]
    return g.reshape(N_DEV * g.shape[1], g.shape[2])


def _grad_to_blocks(n, g):
    if n == "conv_w_dw":
        g = jnp.pad(g, ((0, CONV_HALO - CONV_K), (0, 0)))
        g = g.reshape(g.shape[0], N_DEV, g.shape[1] // N_DEV).transpose(1, 0, 2)
    else:
        g = g.reshape(N_DEV, g.shape[0] // N_DEV, g.shape[1])
    return g.astype(bf16).reshape(N_CHIP, 2, *g.shape[1:])


REPLICATED_LATE = ["ffn1_norm"]
REPLICATED_HEAD = ["ffn2_norm", "final_norm"]
REPLICATED_MIX = ["mix_norm", "conv_b_dw", "conv_ln_g", "conv_ln_b"]
REPLICATED_S5 = [n for n in REPLICATED if n not in REPLICATED_LATE + REPLICATED_HEAD + REPLICATED_MIX]
REPLICATED_EARLY = REPLICATED_HEAD + REPLICATED_S5 + REPLICATED_MIX

PLAN = {
    "start": [("gather", ["ffn1_w_gate", "ffn1_w_up"])],
    "ffn1_up": [("gather", ["ffn1_w_down", "w_in", "w_out", "s5_w_glu", "conv_w_dw"])],
    "ffn1_down": [("gather", ["ffn2_w_gate"])],
    "s5_forward": [("gather", ["ffn2_w_up"])],
    "conv_fwd": [("gather", ["ffn2_w_down"])],
    "ffn2_dw_up": [("sibling", ["ffn2_w_gate"])],
    "ffn2_dw_down": [("sibling", ["ffn2_w_up"])],
    "mix_out_bwd": [("sibling", ["ffn2_w_down"]), ("replicated", REPLICATED_HEAD)],
    "s5_backward": [("chips", ["ffn2_w_gate", "ffn2_w_up"])],
    "conv_bwd_taps": [("chips", ["ffn2_w_down"])],
    "mix_in_bwd": [("replicated", REPLICATED_S5)],
    "ffn1_dw_down": [("sibling", ["w_in", "s5_w_glu", "conv_w_dw", "w_out"]), ("replicated", REPLICATED_MIX)],
    "ffn1_bwd_act": [("chips", ["w_in", "s5_w_glu", "conv_w_dw", "w_out"]), ("sibling", ["ffn1_w_down"])],
    "ffn1_dw_gate": [("chips", ["ffn1_w_down"])],
    "ffn1_dw_up": [("sibling", ["ffn1_w_gate"])],
    "ffn1_bwd_in_0": [("chips", ["ffn1_w_gate"]), ("sibling", ["ffn1_w_up"])],
    "ffn1_bwd_in_1": [("chips", ["ffn1_w_up"])],
    "tail": [("replicated", REPLICATED_LATE)],
}


class _Schedule:
    def __init__(self, wire, p, grads, core):
        self.wire, self.p, self.grads, self.core = wire, p, grads, core
        self.partial, self.reduced, self.pending = {}, {}, []
        self.stacks, self.everyone = [], {}

    def before(self, point):
        assert not self.pending
        for kind, names in PLAN.get(point, ()):
            if kind == "gather":
                given = [self.wire[n] for n in names]
                ex = _gather(given)
            elif kind == "sibling":
                given = [_grad_to_blocks(n, self.grads[n]) for n in names]
                ex = _swap_with_sibling(given)
            elif kind == "chips":
                given = [self.partial.pop(n) for n in names]
                ex = _swap_with_chips(given)
            else:
                names = names + ["loss_terms"] * (names is REPLICATED_HEAD)
                by_shape = {}
                for n in names:
                    by_shape.setdefault(self.p[n].shape if n in self.p else None, []).append(n)
                given = []
                for shape, members in by_shape.items():
                    for k, n in enumerate(members):
                        self.everyone[n] = (len(self.stacks) + len(given), k if len(members) > 1 else None)
                    parts = [self.grads[n].reshape(shape) if shape else self.grads[n] for n in members]
                    given.append(jnp.stack(parts) if len(members) > 1 else parts[0])
                ex = _gather(given)
            self.pending.append((kind, names, given, ex))
        return [ex for _, _, _, ex in self.pending]

    def after(self, point):
        for kind, names, given, ex in self.pending:
            if kind == "gather":
                for n, g in zip(names, ex.out):
                    self.p[n] = _gathered_to_full(n, g)
            elif kind == "sibling":
                if len(names) > 1:
                    sums = _add_sibling_small(list(zip(given, ex.out)), self.core, "reduce_add_" + names[0])
                else:
                    sums = [_add_sibling(given[0], ex.out[0], self.core, "reduce_add_" + names[0])]
                self.partial.update(zip(names, sums))
            elif kind == "chips":
                for n, part, got in zip(names, given, ex.out):
                    self.reduced[n] = (part, got)
            else:
                self.stacks += ex.out
        self.pending = []

    def alone(self, point):
        _exchange(self.before(point), point)
        self.after(point)


def kernel(x, ffn1_norm, ffn1_w_gate, ffn1_w_up, ffn1_w_down, mix_norm, w_in, s5_lam_re, s5_lam_im, s5_log_dt, s5_b_re, s5_b_im, s5_c_re, s5_c_im, s5_d, s5_w_glu, s5_b_glu, conv_w_dw, conv_b_dw, conv_ln_g, conv_ln_b, w_out, ffn2_norm, ffn2_w_gate, ffn2_w_up, ffn2_w_down, final_norm, loss_target, m_ffn1_norm, m_ffn1_w_gate, m_ffn1_w_up, m_ffn1_w_down, m_mix_norm, m_w_in, m_s5_lam_re, m_s5_lam_im, m_s5_log_dt, m_s5_b_re, m_s5_b_im, m_s5_c_re, m_s5_c_im, m_s5_d, m_s5_w_glu, m_s5_b_glu, m_conv_w_dw, m_conv_b_dw, m_conv_ln_g, m_conv_ln_b, m_w_out, m_ffn2_norm, m_ffn2_w_gate, m_ffn2_w_up, m_ffn2_w_down, m_final_norm, v_ffn1_norm, v_ffn1_w_gate, v_ffn1_w_up, v_ffn1_w_down, v_mix_norm, v_w_in, v_s5_lam_re, v_s5_lam_im, v_s5_log_dt, v_s5_b_re, v_s5_b_im, v_s5_c_re, v_s5_c_im, v_s5_d, v_s5_w_glu, v_s5_b_glu, v_conv_w_dw, v_conv_b_dw, v_conv_ln_g, v_conv_ln_b, v_w_out, v_ffn2_norm, v_ffn2_w_gate, v_ffn2_w_up, v_ffn2_w_down, v_final_norm):
    args = locals()
    w = {n: args[n] for n in WEIGHTS}
    m = {n: args["m_" + n] for n in WEIGHTS}
    v = {n: args["v_" + n] for n in WEIGHTS}
    xq, yq, cq = _place()
    q = 2 * xq + yq
    slots = jnp.stack([q, q ^ 1, q ^ 2, q ^ 3]).astype(jnp.int32)

    def shard2d(n, a):
        a = a.reshape(a.shape[-2:])
        return a.T if n in TRANSPOSED else a

    def view(n, a):
        if n.startswith("s5_b_") and a.ndim == 4:
            return a[0].transpose(0, 2, 1)
        return a[0] if a.ndim >= 3 else a.reshape(1, -1)

    def unview(n, a):
        return (a.transpose(0, 2, 1) if n.startswith("s5_b_") and a.ndim == 3 else a).reshape(w[n].shape)

    p = {n: view(n, w[n]) for n in REPLICATED}
    grads = {}
    first = PLAN["start"][0][1]
    wire = {n: _shard_to_wire(n, shard2d(n, w[n])) for n in first}
    sched = _Schedule(wire, p, grads, jnp.reshape(cq, (1,)).astype(jnp.int32))
    wire.update(_to_wire({n: shard2d(n, w[n]) for n in SHARDED if n not in first}, sched.before("start")))
    sched.after("start")
    _, dx = _local_step(x[0], loss_target[0], p, grads, sched)

    out = {}
    groups = [[n for n in SHARDED if n.startswith(tag)] for tag in ("ffn1", "ffn2")]
    for names in groups + [[n for n in SHARDED if not n.startswith("ffn")]]:
        def fit(n, a):
            a = shard2d(n, a)
            return jnp.pad(a, ((0, sched.reduced[n][1].shape[1] - a.shape[0]), (0, 0)))

        items = [(fit(n, w[n]), fit(n, m[n]), fit(n, v[n]), *sched.reduced[n]) for n in names]
        update = _adam_sharded if names[0].startswith("ffn") else _adam_small
        for n, res in zip(names, update(items, slots, "adam_" + names[0])):
            back = lambda r: r[:shard2d(n, w[n]).shape[0]]
            out[n] = [(back(r).T if n in TRANSPOSED else back(r)).reshape(w[n].shape) for r in res]

    for names in (REPLICATED_EARLY, REPLICATED_LATE):
        loss_at = sched.everyone["loss_terms"] if names is REPLICATED_EARLY else None
        used = sorted({sched.everyone[n][0] for n in names} | ({loss_at[0]} if loss_at else set()))
        at = lambda where: (used.index(where[0]), where[1])
        items = [(view(n, w[n]), view(n, m[n]), view(n, v[n]), at(sched.everyone[n])) for n in names]
        res, total = _adam_replicated(items, [sched.stacks[s] for s in used], loss_at and at(loss_at), "adam_" + names[0])
        for n, r in zip(names, res):
            out[n] = [unview(n, a) for a in r]
        if total is not None:
            loss = total.reshape(())

    return (loss, dx.reshape(x.shape), *[out[n][0] for n in WEIGHTS], *[out[n][1] for n in WEIGHTS],
            *[out[n][2] for n in WEIGHTS], *[out[n][3] for n in WEIGHTS])
```

```python
import functools

import jax
import jax.numpy as jnp
from jax import lax
from jax.experimental import pallas as pl
from jax.experimental.pallas import tpu as pltpu

f32 = jnp.float32
bf16 = jnp.bfloat16
S = jax.ShapeDtypeStruct

N_DEV = 8
N_CHIP = 4
D_MODEL = 1024
D_FF = 2816
FF_CHUNKS = [(0, 768), (768, 1536), (1536, 2304), (2304, D_FF)]
S5_WIDTH = 512
S5_GROUPS = 32
S5_GROUP_CH = 16
S5_STATE = 64
S5_LANES = S5_GROUPS * S5_STATE
CONV_WIDTH = 512
CONV_K = 31
CONV_HALO = 32
CONV_HEAD = 64
CONV_ROWS = 32
IN_COLS = S5_WIDTH + 2 * CONV_WIDTH
SEGMENTS = 8
SCAN_LANES = 512
EPS = 1e-6
LR, B1, B2, ADAM_EPS, WD, STEP = 0.001, 0.9, 0.999, 1e-08, 0.01, 10
VMEM_LIMIT = 56 * 1024 * 1024

NN = (((1,), (0,)), ((), ()))
NT = (((1,), (1,)), ((), ()))
TN = (((0,), (0,)), ((), ()))


def _dot(a, b, dims=NN):
    return lax.dot_general(a, b, dims, preferred_element_type=f32)


def _cp(*sem):
    return pltpu.CompilerParams(dimension_semantics=sem, vmem_limit_bytes=VMEM_LIMIT)


def _rms(x, g):
    return x * lax.rsqrt(jnp.mean(x * x, axis=-1, keepdims=True) + EPS) * g


def _rms_bwd(x, g, dh):
    _, vjp = jax.vjp(_rms, x, g)
    return vjp(dh)


def _sigmoid(x):
    return 1.0 / (1.0 + jnp.exp(-x))


def _gelu(x):
    return 0.5 * x * (1.0 + jnp.tanh(0.7978845608028654 * (x + 0.044715 * x * x * x)))


def _rows8(x):
    t, c = x.shape
    return x.reshape(t // 8, 8, c).sum(axis=0)


def _full(shape):
    return pl.BlockSpec(shape, lambda *_: (0,) * len(shape))


def _resident(shape):
    return pl.BlockSpec(shape, lambda *_: (0,) * len(shape), pipeline_mode=pl.Buffered(1))


def _ffn_up(x, g, wg, wu, tm, tag, mixed=None, ride=()):
    L = x.shape[0]

    def body(x_ref, g_ref, wg_ref, wu_ref, *rest):
        h_ref, dadg_ref, dadu_ref, a_ref = rest[-5:-1] if mixed else rest[-4:]
        x = x_ref[...]
        if mixed:
            ys_ref, yc_ref, wo_ref = rest[:3]
            x = x + _dot(ys_ref[...], wo_ref[:S5_WIDTH, :]) + _dot(yc_ref[...], wo_ref[S5_WIDTH:, :])
            rest[-1][...] = x
        h = _rms(x, g_ref[...]).astype(bf16)
        h_ref[...] = h
        for lo, hi in FF_CHUNKS:
            cols = slice(lo, hi)
            gate =_dot(h, wg_ref[cols, :], NT)
            up = _dot(h, wu_ref[cols, :], NT)
            sig = _sigmoid(gate)
            silu = gate * sig
            dadg_ref[:, cols] = (up * (sig + silu * (1.0 - sig))).astype(bf16)
            dadu_ref[:, cols] = silu.astype(bf16)
            a_ref[:, cols] = (silu * up).astype(bf16)

    row = pl.BlockSpec((tm, D_MODEL), lambda i: (i, 0))
    wide = pl.BlockSpec((tm, D_FF), lambda i: (i, 0))
    half = pl.BlockSpec((tm, S5_WIDTH), lambda i: (i, 0))
    return _pallas(
        body, ride=ride, name=tag + "_up", grid=(L // tm,),
        in_specs=[row, _full((1, D_MODEL)), _resident((D_FF, D_MODEL)), _resident((D_FF, D_MODEL))]
        + ([half, half, _resident((D_MODEL, D_MODEL))] if mixed else []),
        out_specs=[row, wide, wide, wide] + [row] * bool(mixed),
        out_shape=[S((L, D_MODEL), bf16)] + [S((L, D_FF), bf16)] * 3 + [S((L, D_MODEL), f32)] * bool(mixed),
        compiler_params=_cp("parallel"),
    )(x, g, wg, wu, *(mixed or ()))


def _ffn_down(x, a, wd, tm, tag, mixer=None, ride=()):
    L = x.shape[0]

    def body(x_ref, a_ref, wd_ref, *rest):
        xo = x_ref[...] + 0.5 * _dot(a_ref[...], wd_ref[...])
        if not mixer:
            rest[0][...] = xo
            return
        g_ref, w_ref, o_ref, h_ref, us_ref, v_ref = rest
        o_ref[...] = xo
        h = _rms(xo, g_ref[...]).astype(bf16)
        h_ref[...] = h
        u = _dot(h, w_ref[...], NT)
        us_ref[...] = u[:, :S5_WIDTH]
        v_ref[...] = u[:, S5_WIDTH:]

    row = lambda c: pl.BlockSpec((tm, c), lambda i: (i, 0))
    extra_in = [_full((1, D_MODEL)), _resident((IN_COLS, D_MODEL))] if mixer else []
    extra_out = [(D_MODEL, bf16), (S5_WIDTH, f32), (2 * CONV_WIDTH, f32)] if mixer else []
    out = _pallas(
        body, ride=ride, name=tag + "_down", grid=(L // tm,),
        in_specs=[row(D_MODEL), row(D_FF), _resident((D_FF, D_MODEL))] + extra_in,
        out_specs=[row(D_MODEL)] + [row(c) for c, _ in extra_out],
        out_shape=[S((L, D_MODEL), f32)] + [S((L, c), t) for c, t in extra_out],
        compiler_params=_cp("parallel"),
    )(x, a, wd, *(mixer or ()))
    return out if mixer else out[0]


def _ffn_down_loss(x, a, wd, target, g, tm, tag):
    L = x.shape[0]

    def body(x_ref, a_ref, wd_ref, t_ref, g_ref, dx_ref, dg_ref, l_ref):
        @pl.when(pl.program_id(0) == 0)
        def _():
            dg_ref[...] = jnp.zeros_like(dg_ref)
            l_ref[...] = jnp.zeros_like(l_ref)

        xo = x_ref[...] + 0.5 * _dot(a_ref[...], wd_ref[...])
        g = g_ref[...]
        e = _rms(xo, g) - t_ref[...]
        l_ref[...] += _rows8(e * e) * (0.5 / D_MODEL)
        dx, dg = _rms_bwd(xo, g, e * (1.0 / D_MODEL))
        dx_ref[...] = dx
        dg_ref[...] += dg

    row = pl.BlockSpec((tm, D_MODEL), lambda i: (i, 0))
    return pl.pallas_call(
        body, name=tag + "_down_loss", grid=(L // tm,),
        in_specs=[row, pl.BlockSpec((tm, D_FF), lambda i: (i, 0)), _resident((D_FF, D_MODEL)), row, _full((1, D_MODEL))],
        out_specs=[row, _full((1, D_MODEL)), _full((8, D_MODEL))],
        out_shape=[S((L, D_MODEL), f32), S((1, D_MODEL), f32), S((8, D_MODEL), f32)],
        compiler_params=_cp("arbitrary"),
    )(x, a, wd, target, g)


def _ffn_bwd_act(dxo, wd, dadg, dadu, tm, tag, ride=()):
    L = dxo.shape[0]

    def body(dx_ref, wd_ref, dadg_ref, dadu_ref, dgate_ref, dup_ref, dxh_ref):
        dxh = (0.5 * dx_ref[...]).astype(bf16)
        dxh_ref[...] = dxh
        for lo, hi in FF_CHUNKS:
            cols = slice(lo, hi)
            da =_dot(dxh, wd_ref[cols, :], NT)
            dgate_ref[:, cols] = (da * dadg_ref[:, cols].astype(f32)).astype(bf16)
            dup_ref[:, cols] = (da * dadu_ref[:, cols].astype(f32)).astype(bf16)

    row = pl.BlockSpec((tm, D_MODEL), lambda i: (i, 0))
    wide = pl.BlockSpec((tm, D_FF), lambda i: (i, 0))
    return _pallas(
        body, ride=ride, name=tag + "_bwd_act", grid=(L // tm,),
        in_specs=[row, _resident((D_FF, D_MODEL)), wide, wide],
        out_specs=[wide, wide, row],
        out_shape=[S((L, D_FF), bf16), S((L, D_FF), bf16), S((L, D_MODEL), bf16)],
        compiler_params=_cp("parallel"),
    )(dxo, wd, dadg, dadu)


def _ffn_bwd_in(dxo, x, g, dgate, dup, wg, wu, tm, name, tiles=None, into=None, ride=()):
    L = x.shape[0]
    first, count = tiles or (0, L // tm)

    def body(dxo_ref, x_ref, g_ref, dgate_ref, dup_ref, wg_ref, wu_ref, *rest):
        dx_ref, dg_ref = rest[-2:]

        @pl.when(pl.program_id(0) == 0)
        def _():
            dg_ref[...] = jnp.zeros_like(dg_ref)

        dh = _dot(dgate_ref[...], wg_ref[...]) + _dot(dup_ref[...], wu_ref[...])
        dx, dg = _rms_bwd(x_ref[...], g_ref[...], dh)
        dx_ref[...] = dxo_ref[...] + dx
        dg_ref[...] += dg

    row = pl.BlockSpec((tm, D_MODEL), lambda i: (first + i, 0))
    wide = pl.BlockSpec((tm, D_FF), lambda i: (first + i, 0))
    return _pallas(
        body, ride=ride, name=name, grid=(count,),
        in_specs=[row, row, _full((1, D_MODEL)), wide, wide, _resident((D_FF, D_MODEL)), _resident((D_FF, D_MODEL))]
        + [ANY] * (into is not None),
        out_specs=[row, _full((1, D_MODEL))],
        out_shape=[S((L, D_MODEL), f32), S((1, D_MODEL), f32)],
        input_output_aliases={7: 0} if into is not None else {},
        compiler_params=_cp("arbitrary"),
    )(dxo, x, g, dgate, dup, wg, wu, *([into] if into is not None else []))


def _mm_tn(a, b, out_dtype, name, tm=512, tn=1024, ride=()):
    L, M = a.shape
    N = b.shape[1]
    tm, tn = min(tm, M), min(tn, N)
    while M % tm:
        tm //= 2
    while N % tn:
        tn //= 2

    def body(a_ref, b_ref, o_ref):
        o_ref[...] = _dot(a_ref[...].astype(bf16), b_ref[...].astype(bf16), TN).astype(out_dtype)

    return _pallas(
        body, ride=ride, name=name, grid=(M // tm, N // tn),
        in_specs=[pl.BlockSpec((L, tm), lambda i, j: (0, i)), pl.BlockSpec((L, tn), lambda i, j: (0, j))],
        out_specs=pl.BlockSpec((tm, tn), lambda i, j: (i, j)),
        out_shape=S((M, N), out_dtype),
        compiler_params=_cp("parallel", "parallel"),
    )(a, b)


def _mix_in_bwd(dxo, x, g, du_s5, dv, w_in, tm, ride=()):
    L = x.shape[0]

    def body(dxo_ref, x_ref, g_ref, dus_ref, dv_ref, w_ref, dx_ref, dg_ref, dub_ref, dxh_ref):
        @pl.when(pl.program_id(0) == 0)
        def _():
            dg_ref[...] = jnp.zeros_like(dg_ref)

        dus = dus_ref[...].astype(bf16)
        dvb = dv_ref[...].astype(bf16)
        dub_ref[:, :S5_WIDTH] = dus
        dub_ref[:, S5_WIDTH:] = dvb
        dh = _dot(dus, w_ref[:S5_WIDTH, :]) + _dot(dvb, w_ref[S5_WIDTH:, :])
        dx, dg = _rms_bwd(x_ref[...], g_ref[...], dh)
        dx = dxo_ref[...] + dx
        dx_ref[...] = dx
        dxh_ref[...] = (0.5 * dx).astype(bf16)
        dg_ref[...] += dg

    row = lambda c: pl.BlockSpec((tm, c), lambda i: (i, 0))
    return _pallas(
        body, ride=ride, name="mix_in_bwd", grid=(L // tm,),
        in_specs=[row(D_MODEL), row(D_MODEL), _full((1, D_MODEL)), row(S5_WIDTH), row(2 * CONV_WIDTH),
                  _full((IN_COLS, D_MODEL))],
        out_specs=[row(D_MODEL), _full((1, D_MODEL)), row(IN_COLS), row(D_MODEL)],
        out_shape=[S((L, D_MODEL), f32), S((1, D_MODEL), f32), S((L, IN_COLS), bf16), S((L, D_MODEL), bf16)],
        compiler_params=_cp("arbitrary"),
    )(dxo, x, g, du_s5, dv, w_in)


def _dw_out(y_s5, y_conv, dxb, tn=512):
    L = dxb.shape[0]

    def body(ys_ref, yc_ref, b_ref, o_ref):
        b = b_ref[...]
        o_ref[:S5_WIDTH, :] = _dot(ys_ref[...], b, TN).astype(bf16)
        o_ref[S5_WIDTH:, :] = _dot(yc_ref[...], b, TN).astype(bf16)

    return pl.pallas_call(
        body, name="dw_out", grid=(D_MODEL // tn,),
        in_specs=[_full((L, S5_WIDTH)), _full((L, CONV_WIDTH)), pl.BlockSpec((L, tn), lambda j: (0, j))],
        out_specs=pl.BlockSpec((S5_WIDTH + CONV_WIDTH, tn), lambda j: (0, j)),
        out_shape=S((S5_WIDTH + CONV_WIDTH, D_MODEL), bf16),
        compiler_params=_cp("parallel"),
    )(y_s5, y_conv, dxb)


def _mix_out_bwd(dx, w_out, tm, ride=()):
    L = dx.shape[0]

    def body(dx_ref, w_ref, dys_ref, dyc_ref, dxb_ref):
        dxb = dx_ref[...].astype(bf16)
        dxb_ref[...] = dxb
        dys_ref[...] = _dot(dxb, w_ref[:S5_WIDTH, :], NT)
        dyc_ref[...] = _dot(dxb, w_ref[S5_WIDTH:, :], NT)

    row = lambda c: pl.BlockSpec((tm, c), lambda i: (i, 0))
    return _pallas(
        body, ride=ride, name="mix_out_bwd", grid=(L // tm,),
        in_specs=[row(D_MODEL), _full((D_MODEL, D_MODEL))],
        out_specs=[row(S5_WIDTH), row(CONV_WIDTH), row(D_MODEL)],
        out_shape=[S((L, S5_WIDTH), f32), S((L, CONV_WIDTH), f32), S((L, D_MODEL), bf16)],
        compiler_params=_cp("parallel"),
    )(dx, w_out)


def _s5_discretise(lam_re, lam_im, log_dt, b_re, b_im):
    dt = jnp.exp(log_dt)
    mag = jnp.exp(lam_re * dt)
    abar_re = mag * jnp.cos(lam_im * dt)
    abar_im = mag * jnp.sin(lam_im * dt)
    den = lam_re * lam_re + lam_im * lam_im
    num_re = abar_re - 1.0
    f_re = ((num_re * lam_re + abar_im * lam_im) / den)[:, None, :]
    f_im = ((abar_im * lam_re - num_re * lam_im) / den)[:, None, :]
    return abar_re, abar_im, f_re * b_re - f_im * b_im, f_re * b_im + f_im * b_re


def _s5_params(lam_re, lam_im, log_dt, b_re, b_im):
    def body(lr, li, ld, br, bi, ar_ref, ai_ref, bbr_ref, bbi_ref):
        ar, ai, bbr, bbi = _s5_discretise(lr[...], li[...], ld[...], br[...], bi[...])
        ar_ref[...], ai_ref[...], bbr_ref[...], bbi_ref[...] = ar, ai, bbr, bbi

    gp = S((S5_GROUPS, S5_STATE), f32)
    gcp = S((S5_GROUPS, S5_GROUP_CH, S5_STATE), f32)
    return pl.pallas_call(body, name="s5_params", out_shape=[gp, gp, gcp, gcp])(lam_re, lam_im, log_dt, b_re, b_im)


def _s5_params_bwd(lam_re, lam_im, log_dt, b_re, b_im, d_ar, d_ai, d_bbr, d_bbi):
    def body(lr, li, ld, br, bi, car, cai, cbr, cbi, o_lr, o_li, o_ld, o_br, o_bi):
        _, vjp = jax.vjp(_s5_discretise, lr[...], li[...], ld[...], br[...], bi[...])
        o_lr[...], o_li[...], o_ld[...], o_br[...], o_bi[...] = vjp((car[...], cai[...], cbr[...], cbi[...]))

    gp = S((S5_GROUPS, S5_STATE), f32)
    gcp = S((S5_GROUPS, S5_GROUP_CH, S5_STATE), f32)
    return pl.pallas_call(body, name="s5_params_bwd", out_shape=[gp, gp, S((S5_GROUPS, 1), f32), gcp, gcp])(
        lam_re, lam_im, log_dt, b_re, b_im, d_ar, d_ai, d_bbr, d_bbi)


def _cmul(ar, ai, br, bi):
    return ar * br - ai * bi, ar * bi + ai * br


def _segment_starts(er, ei, ar, ai, steps, reverse):
    pr, pi = ar, ai
    n = 1
    while n < steps:
        pr, pi = _cmul(pr, pi, pr, pi)
        n *= 2
    assert n == steps
    row = lax.broadcasted_iota(jnp.int32, (SEGMENTS, SCAN_LANES), 0)
    hr = jnp.zeros((1, SCAN_LANES), f32)
    hi = jnp.zeros((1, SCAN_LANES), f32)
    out_r = jnp.zeros((SEGMENTS, SCAN_LANES), f32)
    out_i = jnp.zeros((SEGMENTS, SCAN_LANES), f32)
    order = range(SEGMENTS - 1, 0, -1) if reverse else range(0, SEGMENTS - 1)
    for r in order:
        qr, qi = _cmul(pr, pi, hr, hi)
        hr, hi = qr + er[r:r + 1, :], qi + ei[r:r + 1, :]
        nxt = r - 1 if reverse else r + 1
        out_r = jnp.where(row == nxt, hr, out_r)
        out_i = jnp.where(row == nxt, hi, out_i)
    return out_r, out_i


def _s5_read_bwd(dout, y_lin, u, d_skip, w_glu, b_glu, tm):
    L = u.shape[0]

    def body(do_ref, yl_ref, u_ref, d_ref, w_ref, b_ref, dyl_ref, du_ref, dd_ref, dw_ref, db_ref):
        @pl.when(pl.program_id(0) == 0)
        def _():
            dd_ref[...] = jnp.zeros_like(dd_ref)
            dw_ref[...] = jnp.zeros_like(dw_ref)
            db_ref[...] = jnp.zeros_like(db_ref)

        u, d, dout = u_ref[...], d_ref[...], do_ref[...]
        y, gelu_vjp = jax.vjp(_gelu, yl_ref[...] + d * u)
        yb = y.astype(bf16)
        sig = _sigmoid(_dot(yb, w_ref[...]) + b_ref[...])
        dz = dout * y * sig * (1.0 - sig)
        dzb = dz.astype(bf16)
        dy = dout * sig + _dot(dzb, w_ref[...], NT)
        (dyp,) = gelu_vjp(dy)
        dyl_ref[...] = dyp.astype(bf16)
        du_ref[...] = d * dyp
        dd_ref[...] += _rows8(dyp * u)
        db_ref[...] += _rows8(dz)
        dw_ref[...] += _dot(yb, dzb, TN)

    row = pl.BlockSpec((tm, S5_WIDTH), lambda i: (i, 0))
    vec = _full((1, S5_WIDTH))
    part = _full((8, S5_WIDTH))
    return pl.pallas_call(
        body, name="s5_read_bwd", grid=(L // tm,),
        in_specs=[row, row, row, vec, _full((S5_WIDTH, S5_WIDTH)), vec],
        out_specs=[row, row, part, _full((S5_WIDTH, S5_WIDTH)), part],
        out_shape=[S((L, S5_WIDTH), bf16), S((L, S5_WIDTH), f32), S((8, S5_WIDTH), f32),
                   S((S5_WIDTH, S5_WIDTH), f32), S((8, S5_WIDTH), f32)],
        compiler_params=_cp("arbitrary"),
    )(dout, y_lin, u, d_skip, w_glu, b_glu)


S5_CHUNK_CH = SCAN_LANES // S5_STATE * S5_GROUP_CH


def _s5_two_phase(L, bi):
    rows = bi * SEGMENTS
    nb = L // rows
    whole = pltpu.VMEM((L // SEGMENTS, SEGMENTS, SCAN_LANES), f32)
    mat = pl.BlockSpec((S5_CHUNK_CH, SCAN_LANES), lambda c, j: (c, c))
    vec = pl.BlockSpec((1, SCAN_LANES), lambda c, j: (0, c))
    tile = pl.BlockSpec((SEGMENTS, SCAN_LANES), lambda c, j: (0, c))
    return rows, nb, whole, mat, vec, tile


def _s5_forward(u, a_re, a_im, bb_re, bb_im, cc_re, cc_im, bi, ride=()):
    L = u.shape[0]
    rows, nb, whole, mat, vec, _ = _s5_two_phase(L, bi)

    def body(u_ref, ar_ref, ai_ref, br_ref, bi_ref, cr_ref, ci_ref, sr_ref, si_ref, yl_ref, hr_ref, hi_ref, dr_ref, di_ref):
        j = pl.program_id(1)
        ar = jnp.broadcast_to(ar_ref[...], (SEGMENTS, SCAN_LANES))
        ai = jnp.broadcast_to(ai_ref[...], (SEGMENTS, SCAN_LANES))

        @pl.when(j == 0)
        def _():
            hr_ref[...] = jnp.zeros_like(hr_ref)
            hi_ref[...] = jnp.zeros_like(hi_ref)

        @pl.when(j < nb)
        def _():
            base = j * bi
            ub = u_ref[...].astype(bf16)
            dr_ref[pl.ds(base, bi)] = _dot(ub, br_ref[...]).reshape(bi, SEGMENTS, SCAN_LANES)
            di_ref[pl.ds(base, bi)] = _dot(ub, bi_ref[...]).reshape(bi, SEGMENTS, SCAN_LANES)

            def step(i, c):
                pr, pi = _cmul(ar, ai, c[0], c[1])
                return pr + dr_ref[base + i], pi + di_ref[base + i]

            hr_ref[...], hi_ref[...] = lax.fori_loop(0, bi, step, (hr_ref[...], hi_ref[...]), unroll=True)

        @pl.when(j == nb - 1)
        def _():
            hr_ref[...], hi_ref[...] = _segment_starts(hr_ref[...], hi_ref[...], ar_ref[...], ai_ref[...], L // SEGMENTS, False)

        @pl.when(j >= nb)
        def _():
            base = (j - nb) * bi

            def step(i, c):
                pr, pi = _cmul(ar, ai, c[0], c[1])
                nr, nim = pr + dr_ref[base + i], pi + di_ref[base + i]
                dr_ref[base + i] = nr
                di_ref[base + i] = nim
                return nr, nim

            hr_ref[...], hi_ref[...] = lax.fori_loop(0, bi, step, (hr_ref[...], hi_ref[...]), unroll=True)
            sr = dr_ref[pl.ds(base, bi)].reshape(rows, SCAN_LANES).astype(bf16)
            si = di_ref[pl.ds(base, bi)].reshape(rows, SCAN_LANES).astype(bf16)
            sr_ref[...] = sr
            si_ref[...] = si
            yl_ref[...] = _dot(sr, cr_ref[...], NT) - _dot(si, ci_ref[...], NT)

    u_spec = pl.BlockSpec((rows, S5_CHUNK_CH), lambda c, j: (jnp.minimum(j, nb - 1), c))
    late = lambda width: pl.BlockSpec((rows, width), lambda c, j: (jnp.maximum(j - nb, 0), c))
    return _pallas(
        body, ride=ride, name="s5_forward", grid=(S5_LANES // SCAN_LANES, 2 * nb),
        in_specs=[u_spec, vec, vec, mat, mat, mat, mat],
        out_specs=[late(SCAN_LANES), late(SCAN_LANES), late(S5_CHUNK_CH)],
        out_shape=[S((L, S5_LANES), bf16)] * 2 + [S((L, S5_WIDTH), f32)],
        scratch_shapes=[pltpu.VMEM((SEGMENTS, SCAN_LANES), f32)] * 2 + [whole] * 2,
        compiler_params=_cp("parallel", "arbitrary"),
    )(u, a_re, a_im, bb_re, bb_im, cc_re, cc_im)


def _s5_backward(dy, u, du_skip, s_re, s_im, a_re, a_im, bb_re, bb_im, cc_re, cc_im, bi, ride=()):
    L = u.shape[0]
    rows, nb, whole, mat, vec, tile = _s5_two_phase(L, bi)
    per = rows // 16

    def body(dy_ref, u_ref, dus_ref, sr_ref, si_ref, pr_ref, pi_ref, lr_ref, li_ref, ar_ref, ai_ref, br_ref, bi_ref, cr_ref,
             ci_ref, du_ref, dar_ref, dai_ref, dbr_ref, dbi_ref, dcr_ref, dci_ref, hr_ref, hi_ref, gr_ref, gi_ref, fr_ref, fi_ref):
        j = pl.program_id(1)
        ar = jnp.broadcast_to(ar_ref[...], (SEGMENTS, SCAN_LANES))
        ai = jnp.broadcast_to(ai_ref[...], (SEGMENTS, SCAN_LANES))

        @pl.when(j == 0)
        def _():
            for ref in (hr_ref, hi_ref, dar_ref, dai_ref, dbr_ref, dbi_ref, dcr_ref, dci_ref):
                ref[...] = jnp.zeros_like(ref)

        @pl.when(j < nb)
        def _():
            base = (nb - 1 - j) * bi
            dy = dy_ref[...]
            gr_ref[pl.ds(base, bi)] = _dot(dy, cr_ref[...]).reshape(bi, SEGMENTS, SCAN_LANES)
            gi_ref[pl.ds(base, bi)] = (-_dot(dy, ci_ref[...])).reshape(bi, SEGMENTS, SCAN_LANES)

            def step(n, c):
                i = base + bi - 1 - n
                qr, qi = _cmul(ar, ai, c[0], c[1])
                return qr + gr_ref[i], qi + gi_ref[i]

            hr_ref[...], hi_ref[...] = lax.fori_loop(0, bi, step, (hr_ref[...], hi_ref[...]), unroll=True)

        @pl.when(j == nb - 1)
        def _():
            hr_ref[...], hi_ref[...] = _segment_starts(hr_ref[...], hi_ref[...], ar_ref[...], ai_ref[...], L // SEGMENTS, True)

        @pl.when(j >= nb)
        def _():
            blk = 2 * nb - 1 - j
            base = blk * bi
            sr, si = sr_ref[...], si_ref[...]
            fr_ref[...] = sr.astype(f32).reshape(bi, SEGMENTS, SCAN_LANES)
            fi_ref[...] = si.astype(f32).reshape(bi, SEGMENTS, SCAN_LANES)

            def step(n, c):
                i = bi - 1 - n
                gr, gi, accr, acci = c
                qr, qi = _cmul(ar, ai, gr, gi)
                gr, gi = qr + gr_ref[base + i], qi + gi_ref[base + i]
                gr_ref[base + i] = gr
                gi_ref[base + i] = gi
                pr, pi = fr_ref[i - 1], fi_ref[i - 1]
                return gr, gi, accr + (gr * pr + gi * pi), acci + (gi * pr - gr * pi)

            gr, gi, accr, acci = lax.fori_loop(0, bi - 1, step, (hr_ref[...], hi_ref[...], dar_ref[...], dai_ref[...]), unroll=True)
            qr, qi = _cmul(ar, ai, gr, gi)
            gr, gi = qr + gr_ref[base], qi + gi_ref[base]
            gr_ref[base] = gr
            gi_ref[base] = gi
            hr_ref[...], hi_ref[...] = gr, gi
            row = lax.broadcasted_iota(jnp.int32, (SEGMENTS, SCAN_LANES), 0)
            older = lambda ref: ref[...].astype(f32)[SEGMENTS:, :]
            wrap_r = jnp.where(row == 0, 0.0, pltpu.roll(older(lr_ref), 1, 0))
            wrap_i = jnp.where(row == 0, 0.0, pltpu.roll(older(li_ref), 1, 0))
            pr = jnp.where(blk == 0, wrap_r, older(pr_ref))
            pi = jnp.where(blk == 0, wrap_i, older(pi_ref))
            dar_ref[...] = accr + gr * pr + gi * pi
            dai_ref[...] = acci + gi * pr - gr * pi

            g_re = gr_ref[pl.ds(base, bi)].reshape(rows, SCAN_LANES).astype(bf16)
            g_im = gi_ref[pl.ds(base, bi)].reshape(rows, SCAN_LANES).astype(bf16)
            ub = u_ref[...].astype(bf16)
            dy = dy_ref[...]
            du_ref[...] = dus_ref[...] + _dot(g_re, br_ref[...], NT) + _dot(g_im, bi_ref[...], NT)
            dbr_ref[...] += _dot(ub, g_re, TN)
            dbi_ref[...] += _dot(ub, g_im, TN)
            dcr_ref[...] += _dot(dy, sr, TN)
            dci_ref[...] -= _dot(dy, si, TN)

    block = lambda c, j: jnp.where(j < nb, nb - 1 - j, 2 * nb - 1 - j)
    late_block = lambda c, j: jnp.minimum(2 * nb - 1 - j, nb - 1)
    both = pl.BlockSpec((rows, S5_CHUNK_CH), lambda c, j: (block(c, j), c))
    chan = pl.BlockSpec((rows, S5_CHUNK_CH), lambda c, j: (late_block(c, j), c))
    state = pl.BlockSpec((rows, SCAN_LANES), lambda c, j: (late_block(c, j), c))
    prev = pl.BlockSpec((16, SCAN_LANES), lambda c, j: (jnp.maximum(late_block(c, j) * per - 1, 0), c))
    last = pl.BlockSpec((16, SCAN_LANES), lambda c, j: (L // 16 - 1, c))
    grad = pl.BlockSpec((S5_CHUNK_CH, SCAN_LANES), lambda c, j: (c, 0))
    return _pallas(
        body, ride=ride, name="s5_backward", grid=(S5_LANES // SCAN_LANES, 2 * nb),
        in_specs=[both, chan, chan, state, state, prev, prev, last, last, vec, vec, mat, mat, mat, mat],
        out_specs=[chan, tile, tile, grad, grad, grad, grad],
        out_shape=[S((L, S5_WIDTH), f32)] + [S((SEGMENTS, S5_LANES), f32)] * 2 + [S((S5_WIDTH, SCAN_LANES), f32)] * 4,
        scratch_shapes=[pltpu.VMEM((SEGMENTS, SCAN_LANES), f32)] * 2 + [whole] * 2 + [pltpu.VMEM((bi, SEGMENTS, SCAN_LANES), f32)] * 2,
        compiler_params=_cp("parallel", "arbitrary"),
    )(dy, u, du_skip, s_re, s_im, s_re, s_im, s_re, s_im, a_re, a_im, bb_re, bb_im, cc_re, cc_im)


def _s5_gate(y_lin, u, d_skip, w_glu, b_glu, tm, ride=()):
    L = u.shape[0]

    def body(yl_ref, u_ref, d_ref, w_ref, b_ref, o_ref):
        y = _gelu(yl_ref[...] + d_ref[...] * u_ref[...])
        z = _dot(y.astype(bf16), w_ref[...]) + b_ref[...]
        o_ref[...] = (y * _sigmoid(z)).astype(bf16)

    row = pl.BlockSpec((tm, S5_WIDTH), lambda i: (i, 0))
    vec = _full((1, S5_WIDTH))
    return _pallas(
        body, ride=ride, name="s5_gate", grid=(L // tm,),
        in_specs=[row, row, vec, _full((S5_WIDTH, S5_WIDTH)), vec],
        out_specs=row, out_shape=S((L, S5_WIDTH), bf16),
        compiler_params=_cp("parallel"),
    )(y_lin, u, d_skip, w_glu, b_glu)


def _group_mean(x, avg):
    return _dot(x.astype(bf16), avg)


def _conv_act(zn, ln_g, ln_b):
    t = zn * ln_g + ln_b
    return t * _sigmoid(t)


def _glu_padded(v_ref, halo_ref, zpad_ref, tm):
    v = v_ref[...]
    vh = halo_ref[...]
    zh = vh[:, :CONV_WIDTH] * _sigmoid(vh[:, CONV_WIDTH:])
    zpad_ref[:CONV_HALO, :] = jnp.where(pl.program_id(0) > 0, zh, 0.0)
    zpad_ref[CONV_HALO:CONV_HALO + tm, :] = v[:, :CONV_WIDTH] * _sigmoid(v[:, CONV_WIDTH:])
    zpad_ref[CONV_HALO + tm:, :] = jnp.zeros((8, CONV_WIDTH), f32)


def _shifted(pad_ref, sh_ref, tm):
    for b in range(8):
        sh_ref[b] = pad_ref[pl.ds(b, tm + CONV_HALO), :]


def _window(sh_ref, r0, off, rows):
    start = r0 + 8 * (off // 8)
    return sh_ref[off % 8, pl.ds(start if isinstance(start, int) else pl.multiple_of(start, 8), rows), :]


def _tap_sum(w_ref, sh_ref, taps, out_ref, tm, bias):
    for r0 in range(0, tm, CONV_ROWS):
        acc = jnp.zeros((CONV_ROWS, CONV_WIDTH), f32) + bias
        for k, off in taps:
            acc = acc + w_ref[k:k + 1, :] * _window(sh_ref, r0, off, CONV_ROWS)
        out_ref[r0:r0 + CONV_ROWS, :] = acc


FWD_TAPS = [(k, CONV_HALO - (CONV_K - 1) + k) for k in range(CONV_K)]
BWD_TAPS = [(k, CONV_K - 1 - k) for k in range(CONV_K)]


def _conv_specs(tm):
    per = tm // CONV_HALO
    vrow = pl.BlockSpec((tm, 2 * CONV_WIDTH), lambda i: (i, 0))
    vhalo = pl.BlockSpec((CONV_HALO, 2 * CONV_WIDTH), lambda i: (jnp.maximum(i * per - 1, 0), 0))
    return vrow, vhalo


def _conv_scratch(tm):
    return [pltpu.VMEM((tm + CONV_HALO + 8, CONV_WIDTH), f32), pltpu.VMEM((8, tm + CONV_HALO, CONV_WIDTH), f32)]


def _conv_fwd(v, w_dw, b_dw, ln_g, ln_b, avg, tm, ride=()):
    L = v.shape[0]

    def body(v_ref, halo_ref, w_ref, b_ref, g_ref, bb_ref, avg_ref, o_ref, zc_ref, zpad_ref, zs_ref):
        _glu_padded(v_ref, halo_ref, zpad_ref, tm)
        _shifted(zpad_ref, zs_ref, tm)
        _tap_sum(w_ref, zs_ref, FWD_TAPS, zc_ref, tm, b_ref[...])
        zc = zc_ref[...]
        xc = zc - _group_mean(zc, avg_ref[...])
        zn = xc * lax.rsqrt(_group_mean(xc * xc, avg_ref[...]) + EPS)
        o_ref[...] = _conv_act(zn, g_ref[...], bb_ref[...]).astype(bf16)

    vrow, vhalo = _conv_specs(tm)
    vec = _full((1, CONV_WIDTH))
    row = pl.BlockSpec((tm, CONV_WIDTH), lambda i: (i, 0))
    return _pallas(
        body, ride=ride, name="conv_fwd", grid=(L // tm,),
        in_specs=[vrow, vhalo, _full((CONV_HALO, CONV_WIDTH)), vec, vec, vec, _full((CONV_WIDTH, CONV_WIDTH))],
        out_specs=[row, row], out_shape=[S((L, CONV_WIDTH), bf16), S((L, CONV_WIDTH), f32)],
        scratch_shapes=_conv_scratch(tm),
        compiler_params=_cp("arbitrary"),
    )(v, v, w_dw, b_dw, ln_g, ln_b, avg)


def _conv_bwd_norm(dout, zc, ln_g, ln_b, avg, tm):
    L = zc.shape[0]

    def body(do_ref, zc_ref, g_ref, bb_ref, avg_ref, dzc_ref, dg_ref, db_ref, dbd_ref):
        @pl.when(pl.program_id(0) == 0)
        def _():
            dg_ref[...] = jnp.zeros_like(dg_ref)
            db_ref[...] = jnp.zeros_like(db_ref)
            dbd_ref[...] = jnp.zeros_like(dbd_ref)

        avg = avg_ref[...]
        zc = zc_ref[...]
        xc = zc - _group_mean(zc, avg)
        rstd = lax.rsqrt(_group_mean(xc * xc, avg) + EPS)
        xhat = xc * rstd
        _, act_vjp = jax.vjp(_conv_act, xhat, g_ref[...], bb_ref[...])
        dxhat, dg, db = act_vjp(do_ref[...])
        dzc = rstd * (dxhat - _group_mean(dxhat, avg) - xhat * _group_mean(dxhat * xhat, avg))
        dzc_ref[...] = dzc
        dg_ref[0:1, :] += dg
        db_ref[0:1, :] += db
        dbd_ref[...] += _rows8(dzc)

    vec = _full((1, CONV_WIDTH))
    row = pl.BlockSpec((tm, CONV_WIDTH), lambda i: (i, 0))
    part = _full((8, CONV_WIDTH))
    return pl.pallas_call(
        body, name="conv_bwd_norm", grid=(L // tm,),
        in_specs=[row, row, vec, vec, _full((CONV_WIDTH, CONV_WIDTH))],
        out_specs=[row, part, part, part],
        out_shape=[S((L, CONV_WIDTH), f32)] + [S((8, CONV_WIDTH), f32)] * 3,
        compiler_params=_cp("arbitrary"),
    )(dout, zc, ln_g, ln_b, avg)


def _conv_bwd_taps(dzc, v, w_dw, tm, ride=()):
    L = v.shape[0]
    nt = L // tm
    per = tm // CONV_HALO

    def body(d_ref, dn_ref, v_ref, w_ref, dv_ref, dw_ref, dpad_ref, ds_ref, dz_ref, z_ref):
        i = pl.program_id(0)

        @pl.when(i == 0)
        def _():
            dw_ref[...] = jnp.zeros_like(dw_ref)

        v = v_ref[...]
        sig = _sigmoid(v[:, CONV_WIDTH:])
        z_ref[...] = v[:, :CONV_WIDTH] * sig
        dpad_ref[:tm, :] = d_ref[...]
        dpad_ref[tm:tm + CONV_HALO, :] = jnp.where(i < nt - 1, dn_ref[...], 0.0)
        dpad_ref[tm + CONV_HALO:, :] = jnp.zeros((8, CONV_WIDTH), f32)
        _shifted(dpad_ref, ds_ref, tm)
        _tap_sum(w_ref, ds_ref, BWD_TAPS, dz_ref, tm, 0.0)

        for first in range(0, CONV_K, 8):
            taps = BWD_TAPS[first:first + 8]

            accs = [jnp.zeros((8, CONV_WIDTH), f32) for _ in taps]
            for r0 in range(0, tm, 8):
                z = z_ref[r0:r0 + 8, :]
                accs = [acc + z * _window(ds_ref, r0, off, 8) for acc, (_, off) in zip(accs, taps)]
            for acc, (k, _) in zip(accs, taps):
                dw_ref[k] += acc

        dz = dz_ref[...]
        dv_ref[:, :CONV_WIDTH] = dz * sig
        dv_ref[:, CONV_WIDTH:] = dz * v[:, :CONV_WIDTH] * sig * (1.0 - sig)

    vrow, _ = _conv_specs(tm)
    row = pl.BlockSpec((tm, CONV_WIDTH), lambda i: (i, 0))
    nxt = pl.BlockSpec((CONV_HALO, CONV_WIDTH), lambda i: (jnp.minimum((i + 1) * per, nt * per - 1), 0))
    return _pallas(
        body, ride=ride, name="conv_bwd_taps", grid=(nt,),
        in_specs=[row, nxt, vrow, _full((CONV_HALO, CONV_WIDTH))],
        out_specs=[vrow, _full((CONV_HALO, 8, CONV_WIDTH))],
        out_shape=[S((L, 2 * CONV_WIDTH), f32), S((CONV_HALO, 8, CONV_WIDTH), f32)],
        scratch_shapes=_conv_scratch(tm) + [pltpu.VMEM((tm, CONV_WIDTH), f32)] * 2,
        compiler_params=_cp("arbitrary"),
    )(dzc, dzc, v, w_dw)


def _to_segments(a):
    L, c = a.shape
    return a.reshape(SEGMENTS, L // SEGMENTS, c).transpose(1, 0, 2).reshape(L, c)


def _from_segments(a):
    L, c = a.shape
    return a.reshape(L // SEGMENTS, SEGMENTS, c).transpose(1, 0, 2).reshape(L, c)


def _block_diag(ms):
    n = len(ms)

    def body(*refs):
        for a in range(n):
            out = refs[n + a]
            out[...] = jnp.zeros_like(out)
            for g in range(S5_GROUPS):
                rows = slice(g * S5_GROUP_CH, (g + 1) * S5_GROUP_CH)
                out[rows, g * S5_STATE:(g + 1) * S5_STATE] = refs[a][rows, :].astype(bf16)

    return pl.pallas_call(body, name="s5_block_diag", out_shape=[S((S5_WIDTH, S5_LANES), bf16)] * n,
                          compiler_params=pltpu.CompilerParams(vmem_limit_bytes=VMEM_LIMIT))(
        *[m.reshape(S5_WIDTH, S5_STATE) for m in ms])


def _diag_blocks(ms):
    n = len(ms)
    per_chunk = SCAN_LANES // S5_STATE

    def body(*refs):
        for a in range(n):
            for g in range(S5_GROUPS):
                rows = slice(g * S5_GROUP_CH, (g + 1) * S5_GROUP_CH)
                at = g % per_chunk * S5_STATE
                refs[n + a][rows, :] = refs[a][rows, at:at + S5_STATE]

    out = pl.pallas_call(body, name="s5_diag_blocks", out_shape=[S((S5_WIDTH, S5_STATE), f32)] * n,
                         compiler_params=pltpu.CompilerParams(vmem_limit_bytes=VMEM_LIMIT))(*ms)
    return [o.reshape(S5_GROUPS, S5_GROUP_CH, S5_STATE) for o in out]


class _NoExchanges:
    def before(self, point):
        return ()

    def after(self, point):
        pass

    def alone(self, point):
        pass


def _ffn_block(x, p, tag, tm, sched, head=None, mixed=None, mixer=None):
    point = tag + "_up"
    h, dadg, dadu, a, *x_in = _ffn_up(x, p[tag + "_norm"], p[tag + "_w_gate"], p[tag + "_w_up"], tm, tag, mixed,
                                      ride=sched.before(point))
    sched.after(point)
    x, = x_in or [x]
    if head is None:
        out = _ffn_down(x, a, p[tag + "_w_down"], tm, tag, mixer and tuple(p[n] for n in mixer), ride=sched.before(tag + "_down"))
        sched.after(tag + "_down")
    else:
        out = _ffn_down_loss(x, a, p[tag + "_w_down"], *head, tm, tag)
    return out, (x, h, dadg, dadu, a)


def _ffn_block_bwd(dxo, x, p, tag, saved, tm, grads, sched, parts=1, dxh=None):
    _, h, dadg, dadu, a = saved

    def weight_grad(which, lhs, rhs):
        point = tag + "_dw_" + which
        grads[tag + "_w_" + which] = _mm_tn(lhs, rhs, bf16, point, ride=sched.before(point))
        sched.after(point)

    if dxh is not None:
        weight_grad("down", a, dxh)
    dgate, dup, own_dxh = _ffn_bwd_act(dxo, p[tag + "_w_down"], dadg, dadu, tm, tag, ride=sched.before(tag + "_bwd_act"))
    sched.after(tag + "_bwd_act")
    weight_grad("gate", dgate, h)
    weight_grad("up", dup, h)
    if dxh is None:
        weight_grad("down", a, own_dxh)
    tiles = x.shape[0] // tm
    dx, dgs = None, []
    for k in range(parts):
        point = tag + "_bwd_in" + ("_%d" % k) * (parts > 1)
        dx, dg = _ffn_bwd_in(dxo, x, p[tag + "_norm"], dgate, dup, p[tag + "_w_gate"], p[tag + "_w_up"], tm, point,
                             tiles=(k * tiles // parts, tiles // parts), into=dx, ride=sched.before(point))
        sched.after(point)
        dgs.append(dg)
    grads[tag + "_norm"] = functools.reduce(jnp.add, dgs)
    return dx


def _local_step(x, target, p, grads, sched):
    L = x.shape[0]
    tm = min(512, L // 2)
    ni = L // SEGMENTS
    bi = min(64, ni)

    def carried(point, fn, *args):
        out = fn(*args, ride=sched.before(point))
        sched.after(point)
        return out

    (x1, h2, u_s5, v), saved1 = _ffn_block(x, p, "ffn1", tm, sched, mixer=("mix_norm", "w_in"))

    s5_in = (p["s5_lam_re"], p["s5_lam_im"], p["s5_log_dt"].reshape(S5_GROUPS, 1), p["s5_b_re"], p["s5_b_im"])
    abar_re, abar_im, bbar_re, bbar_im = _s5_params(*s5_in)
    a_re, a_im = abar_re.reshape(1, S5_LANES), abar_im.reshape(1, S5_LANES)
    bb_re, bb_im, cc_re, cc_im = _block_diag([bbar_re, bbar_im, p["s5_c_re"], p["s5_c_im"]])
    u_seg = _to_segments(u_s5)
    s_re, s_im, y_lin = carried("s5_forward", _s5_forward, u_seg, a_re, a_im, bb_re, bb_im, cc_re, cc_im, bi)
    y_s5 = _from_segments(_s5_gate(y_lin, u_seg, p["s5_d"], p["s5_w_glu"], p["s5_b_glu"], tm))
    w_dw = jnp.pad(p["conv_w_dw"], ((0, CONV_HALO - CONV_K), (0, 0)))
    heads = jnp.arange(CONV_WIDTH) // CONV_HEAD
    avg = ((heads[:, None] == heads[None, :]).astype(f32) / CONV_HEAD).astype(bf16)
    y_conv, zc = carried("conv_fwd", _conv_fwd, v, w_dw, p["conv_b_dw"], p["conv_ln_g"], p["conv_ln_b"], avg, tm)

    (dx3, grads["final_norm"], loss_terms), saved2 = _ffn_block(
        x1, p, "ffn2", tm, sched, head=(target, p["final_norm"].reshape(1, D_MODEL)), mixed=(y_s5, y_conv, p["w_out"]))
    x2 = saved2[0]
    grads["loss_terms"] = loss_terms

    dx2 = _ffn_block_bwd(dx3, x2, p, "ffn2", saved2, tm, grads, sched)

    dy_s5, dy_conv, dx2b = carried("mix_out_bwd", _mix_out_bwd, dx2, p["w_out"], tm)
    grads["w_out"] = _dw_out(y_s5, y_conv, dx2b)
    dy_lin, du_skip, dd8, grads["s5_w_glu"], dbg8 = _s5_read_bwd(
        _to_segments(dy_s5), y_lin, u_seg, p["s5_d"], p["s5_w_glu"], p["s5_b_glu"], tm)
    grads["s5_d"] = dd8.sum(axis=0, keepdims=True)
    grads["s5_b_glu"] = dbg8.sum(axis=0, keepdims=True)
    du_seg, da_re8, da_im8, dbb_re, dbb_im, dcc_re, dcc_im = carried(
        "s5_backward", _s5_backward, dy_lin, u_seg, du_skip, s_re, s_im, a_re, -a_im, bb_re, bb_im, cc_re, cc_im, bi)
    d_abar = lambda a8: a8.sum(axis=0).reshape(S5_GROUPS, S5_STATE)
    grads["s5_c_re"], grads["s5_c_im"], d_bbr, d_bbi = _diag_blocks([dcc_re, dcc_im, dbb_re, dbb_im])
    d_lr, d_li, d_ld, d_br, d_bi = _s5_params_bwd(*s5_in, d_abar(da_re8), d_abar(da_im8), d_bbr, d_bbi)
    grads["s5_lam_re"], grads["s5_lam_im"], grads["s5_log_dt"] = d_lr, d_li, d_ld.reshape(1, S5_GROUPS)
    grads["s5_b_re"], grads["s5_b_im"] = d_br, d_bi
    dzc, dlg8, dlb8, dbd8 = _conv_bwd_norm(dy_conv, zc, p["conv_ln_g"], p["conv_ln_b"], avg, tm)
    grads["conv_ln_g"] = dlg8.sum(axis=0, keepdims=True)
    grads["conv_ln_b"] = dlb8.sum(axis=0, keepdims=True)
    grads["conv_b_dw"] = dbd8.sum(axis=0, keepdims=True)
    dv, dw8 = carried("conv_bwd_taps", _conv_bwd_taps, dzc, v, w_dw, tm)
    grads["conv_w_dw"] = dw8.sum(axis=1)[:CONV_K]
    dx1, grads["mix_norm"], dub, dx1h = carried("mix_in_bwd", _mix_in_bwd, dx2, x1, p["mix_norm"], _from_segments(du_seg), dv,
                                                p["w_in"], tm)
    grads["w_in"] = _mm_tn(dub, h2, bf16, "dw_in")

    dx0 = _ffn_block_bwd(dx1, x, p, "ffn1", saved1, tm, grads, sched, parts=min(2, L // tm), dxh=dx1h)
    sched.alone("tail")
    return loss_terms, dx0


MESH = pl.DeviceIdType.MESH
ANY = pl.BlockSpec(memory_space=pl.ANY)


def _place():
    return lax.axis_index("x"), lax.axis_index("y"), lax.axis_index("c")


class _Exchange:
    def __init__(self, ins, out_shape, sems, start, finish):
        self.ins, self.out_shape, self.sems, self.start, self.finish = list(ins), list(out_shape), list(sems), start, finish
        self.out = None


def _pallas(body, *, ride=(), **kw):
    if not ride:
        return pl.pallas_call(body, **kw)

    def run(*args):
        out_shape = kw.get("out_shape", [])
        single = not isinstance(out_shape, (list, tuple))
        shapes = [out_shape] if single else list(out_shape)
        out_specs = [kw["out_specs"]] if single else list(kw.get("out_specs", []))
        grid = tuple(kw.get("grid", ()))
        scratch = list(kw.get("scratch_shapes", ()))
        n_in, n_out, n_scr = len(args), len(shapes), len(scratch)
        r_in = [len(e.ins) for e in ride]
        r_out = [len(e.out_shape) for e in ride]
        r_sem = [len(e.sems) for e in ride]

        def wrapped(*refs):
            own_in, refs = refs[:n_in], refs[n_in:]
            ex_in, refs = refs[:sum(r_in)], refs[sum(r_in):]
            own_out, refs = refs[:n_out], refs[n_out:]
            ex_out, refs = refs[:sum(r_out)], refs[sum(r_out):]
            own_scr, ex_sem = refs[:n_scr], refs[n_scr:]
            parts = []
            for e, ni, no, ns in zip(ride, r_in, r_out, r_sem):
                parts.append((e, ex_in[:ni], ex_out[:no], ex_sem[:ns]))
                ex_in, ex_out, ex_sem = ex_in[ni:], ex_out[no:], ex_sem[ns:]

            def at(step):
                def go():
                    for e, i, o, s in parts:
                        getattr(e, step)(i, o, s)
                if grid:
                    ids = [pl.program_id(d) for d in range(len(grid))]
                    when = [i == (0 if step == "start" else g - 1) for i, g in zip(ids, grid)]
                    pl.when(functools.reduce(lambda a, b: a & b, when))(go)
                else:
                    go()

            at("start")
            if body is not None:
                body(*own_in, *own_out, *own_scr)
            at("finish")

        outs = pl.pallas_call(
            wrapped, name=kw["name"], grid=grid,
            in_specs=list(kw.get("in_specs", [])) + [ANY] * sum(r_in),
            out_specs=out_specs + [ANY] * sum(r_out),
            out_shape=shapes + [s for e in ride for s in e.out_shape],
            scratch_shapes=scratch + [s for e in ride for s in e.sems],
            input_output_aliases=kw.get("input_output_aliases", {}),
            compiler_params=_cp(*["arbitrary"] * len(grid)),
        )(*args, *[a for e in ride for a in e.ins])
        own, rest = outs[:n_out], outs[n_out:]
        for e, no in zip(ride, r_out):
            e.out, rest = list(rest[:no]), rest[no:]
        return own[0] if single else own

    return run


def _exchange(ride, name):
    _pallas(None, ride=ride, name=name)()


def _gather(arrs):
    n = len(arrs)

    def copies(ins, outs, sems):
        send_sems, recv_sems, local_sems = sems
        x, y, c = _place()
        me, sibling = (x, y, c), (x, y, 1 - c)
        chips = [(1 - x, y), (x, 1 - y), (1 - x, 1 - y)]

        def place(a, block):
            return outs[a].at[block]

        def copy(a, k, block, to, src=None):
            px, py, pc = block
            dst = place(a, 4 * px + 2 * py + pc)
            return pltpu.make_async_remote_copy(
                src_ref=dst if src is None else src, dst_ref=dst, send_sem=send_sems.at[7 * a + k],
                recv_sem=recv_sems.at[7 * a + k], device_id=to, device_id_type=MESH)

        def own():
            local = [pltpu.make_async_copy(ins[a], place(a, 4 * x + 2 * y + c), local_sems.at[a]) for a in range(n)]
            remote = []
            for a in range(n):
                remote.append(copy(a, 0, me, sibling, src=ins[a]))
                remote += [copy(a, 1 + j, me, (*chip, c), src=ins[a]) for j, chip in enumerate(chips)]
            return local, remote

        return c, me, sibling, chips, copy, own

    def start(ins, outs, sems):
        local, remote = copies(ins, outs, sems)[-1]()
        for cp in local + remote:
            cp.start()

    def finish(ins, outs, sems):
        c, me, sibling, chips, copy, own = copies(ins, outs, sems)
        passed = []
        for j, chip in enumerate(chips):
            for a in range(n):
                copy(a, 1 + j, (*chip, c), me).wait_recv()
                passed.append(copy(a, 4 + j, (*chip, c), sibling))
                passed[-1].start()
        for a in range(n):
            copy(a, 0, sibling, me).wait_recv()
            for j, chip in enumerate(chips):
                copy(a, 4 + j, (*chip, 1 - c), me).wait_recv()
        local, remote = own()
        for cp in remote + passed:
            cp.wait_send()
        for cp in local:
            cp.wait()

    dma = pltpu.SemaphoreType.DMA
    shapes = [S((N_DEV, *a.shape), a.dtype) for a in arrs]
    return _Exchange(arrs, shapes, [dma((7 * n,)), dma((7 * n,)), dma((n,))], start, finish)


def _swap_with_sibling(gs):
    n = len(gs)

    def copies(ins, outs, sems):
        x, y, c = _place()
        return [pltpu.make_async_remote_copy(
            src_ref=ins[a].at[:, 1 - c], dst_ref=outs[a], send_sem=sems[0].at[a], recv_sem=sems[1].at[a],
            device_id=(x, y, 1 - c), device_id_type=MESH) for a in range(n)]

    def start(ins, outs, sems):
        for cp in copies(ins, outs, sems):
            cp.start()

    def finish(ins, outs, sems):
        for cp in copies(ins, outs, sems):
            cp.wait()

    dma = pltpu.SemaphoreType.DMA
    return _Exchange(gs, [S((N_CHIP, *g.shape[2:]), g.dtype) for g in gs], [dma((n,)), dma((n,))], start, finish)


def _swap_with_chips(ps):
    n = len(ps)

    def copies(ins, outs, sems):
        x, y, c = _place()
        q = 2 * x + y
        peers = [(x, 1 - y), (1 - x, y), (1 - x, 1 - y)]

        def copy(a, j, slot_from, slot_to):
            px, py = peers[j]
            return pltpu.make_async_remote_copy(
                src_ref=ins[a].at[slot_from], dst_ref=outs[a].at[slot_to], send_sem=sems[0].at[3 * a + j],
                recv_sem=sems[1].at[3 * a + j], device_id=(px, py, c), device_id_type=MESH)

        sends = lambda: [copy(a, j, 2 * peers[j][0] + peers[j][1], q) for a in range(n) for j in range(3)]
        lands = lambda: [copy(a, j, q, 2 * peers[j][0] + peers[j][1]) for a in range(n) for j in range(3)]
        return sends, lands

    def start(ins, outs, sems):
        for cp in copies(ins, outs, sems)[0]():
            cp.start()

    def finish(ins, outs, sems):
        sends, lands = copies(ins, outs, sems)
        for cp in lands():
            cp.wait_recv()
        for cp in sends():
            cp.wait_send()

    dma = pltpu.SemaphoreType.DMA
    return _Exchange(ps, [S(p.shape, p.dtype) for p in ps], [dma((3 * n,)), dma((3 * n,))], start, finish)


def _row_tile(rows, cols, itemsize):
    t = rows
    while t * cols * itemsize > (1 << 20) and t % 32 == 0:
        t //= 2
    return t


def _add_sibling(g4, st, core, name):
    _, R, C = st.shape
    tr = _row_tile(R, C, 1)

    def body(c_ref, g_ref, s_ref, o_ref):
        o_ref[...] = (g_ref[...].astype(f32) + s_ref[...].astype(f32)).astype(bf16)

    mine = pl.BlockSpec((None, None, tr, C), lambda q, i, c: (q, c[0], i, 0))
    return pl.pallas_call(
        body, name=name,
        grid_spec=pltpu.PrefetchScalarGridSpec(
            num_scalar_prefetch=1, grid=(N_CHIP, R // tr),
            in_specs=[mine,
                      pl.BlockSpec((None, tr, C), lambda q, i, c: (q, i, 0))],
            out_specs=pl.BlockSpec((None, tr, C), lambda q, i, c: (q, i, 0))),
        out_shape=S((N_CHIP, R, C), bf16),
        compiler_params=_cp("parallel", "parallel"),
    )(core, g4, st)


SMEM = pl.BlockSpec(memory_space=pltpu.SMEM)
VMEM = pl.BlockSpec(memory_space=pltpu.VMEM)


def _add_sibling_small(items, core, name):
    n = len(items)

    def body(c_ref, *refs):
        c = c_ref[0]
        for k in range(n):
            g_ref, s_ref, o_ref = refs[2 * k], refs[2 * k + 1], refs[2 * n + k]
            for q in range(N_CHIP):
                o_ref[q] = (g_ref[q, c].astype(f32) + s_ref[q].astype(f32)).astype(bf16)

    return pl.pallas_call(body, name=name, in_specs=[SMEM] + [VMEM] * (2 * n), out_specs=[VMEM] * n,
                          out_shape=[S(st.shape, bf16) for _, st in items],
                          compiler_params=pltpu.CompilerParams(vmem_limit_bytes=VMEM_LIMIT))(
        core, *[a for item in items for a in item])


def _adam_small(items, slots, name):
    n = len(items)

    def body(s_ref, *refs):
        ins, outs = refs[:5 * n], refs[5 * n:]
        for k in range(n):
            w_ref, m_ref, v_ref, p_ref, got_ref = ins[5 * k:5 * k + 5]
            g = p_ref[s_ref[0]].astype(f32)
            for j in range(1, N_CHIP):
                g = g + got_ref[s_ref[j]].astype(f32)
            outs[4 * k][...] = g
            outs[4 * k + 1][...], outs[4 * k + 2][...], outs[4 * k + 3][...] = _adamw(w_ref[...], g, m_ref[...], v_ref[...])

    out = pl.pallas_call(body, name=name, in_specs=[SMEM] + [VMEM] * (5 * n), out_specs=[VMEM] * (4 * n),
                         out_shape=[S(item[0].shape, f32) for item in items for _ in range(4)],
                         compiler_params=pltpu.CompilerParams(vmem_limit_bytes=VMEM_LIMIT))(
        slots, *[a for item in items for a in item])
    return [out[4 * k:4 * k + 4] for k in range(n)]


def _adamw(w, g, m, v):
    m = B1 * m + (1.0 - B1) * g
    v = B2 * v + (1.0 - B2) * (g * g)
    m_hat = m / (1.0 - B1 ** STEP)
    v_hat = v / (1.0 - B2 ** STEP)
    return -LR * (m_hat / (jnp.sqrt(v_hat) + ADAM_EPS) + WD * w), m, v


def _adam_sharded(items, slots, name):
    n = len(items)
    R, C = items[0][0].shape
    tr = _row_tile(R, C, 4 * n)

    def body(s_ref, *refs):
        ins, outs = refs[:7 * n], refs[7 * n:]
        for k in range(n):
            w_ref, m_ref, v_ref, p_ref, a_ref, b_ref, c_ref = ins[7 * k:7 * k + 7]
            g = p_ref[...].astype(f32) + a_ref[...].astype(f32) + b_ref[...].astype(f32) + c_ref[...].astype(f32)
            outs[4 * k][...] = g
            outs[4 * k + 1][...], outs[4 * k + 2][...], outs[4 * k + 3][...] = _adamw(w_ref[...], g, m_ref[...], v_ref[...])

    shard = pl.BlockSpec((tr, C), lambda i, s: (i, 0))
    slot = lambda k: pl.BlockSpec((None, tr, C), lambda i, s: (s[k], i, 0))
    out = pl.pallas_call(
        body, name=name,
        grid_spec=pltpu.PrefetchScalarGridSpec(
            num_scalar_prefetch=1, grid=(R // tr,),
            in_specs=[shard, shard, shard, slot(0), slot(1), slot(2), slot(3)] * n,
            out_specs=[shard] * (4 * n)),
        out_shape=[S((R, C), f32)] * (4 * n),
        compiler_params=_cp("parallel"),
    )(slots, *[a for w, m, v, part, got in items for a in (w, m, v, part, got, got, got)])
    return [out[4 * k:4 * k + 4] for k in range(n)]


def _adam_replicated(items, stacks, loss_at, name):
    n, ns = len(items), len(stacks)

    def body(*refs):
        stack_refs, ins, outs = refs[:ns], refs[ns:ns + 3 * n], refs[ns + 3 * n:]
        totals = []
        for ref in stack_refs:
            g = ref[0]
            for d in range(1, N_DEV):
                g = g + ref[d]
            totals.append(g)
        for i, (_, _, _, (s, k)) in enumerate(items):
            w_ref, m_ref, v_ref = ins[3 * i:3 * i + 3]
            g = totals[s] if k is None else totals[s][k]
            outs[4 * i][...] = g
            outs[4 * i + 1][...], outs[4 * i + 2][...], outs[4 * i + 3][...] = _adamw(w_ref[...], g, m_ref[...], v_ref[...])
        if loss_at is not None:
            outs[-1][...] = jnp.sum(totals[loss_at[0]], keepdims=True)

    flat = list(stacks) + [a for item in items for a in item[:3]]
    shapes = [S(item[0].shape, f32) for item in items for _ in range(4)] + ([S((1, 1), f32)] if loss_at is not None else [])
    out = pl.pallas_call(body, name=name, out_shape=shapes,
                         compiler_params=pltpu.CompilerParams(vmem_limit_bytes=VMEM_LIMIT))(*flat)
    return [out[4 * i:4 * i + 4] for i in range(n)], (out[-1] if loss_at is not None else None)


WEIGHTS = ["ffn1_norm", "ffn1_w_gate", "ffn1_w_up", "ffn1_w_down", "mix_norm", "w_in", "s5_lam_re", "s5_lam_im", "s5_log_dt",
           "s5_b_re", "s5_b_im", "s5_c_re", "s5_c_im", "s5_d", "s5_w_glu", "s5_b_glu", "conv_w_dw", "conv_b_dw", "conv_ln_g",
           "conv_ln_b", "w_out", "ffn2_norm", "ffn2_w_gate", "ffn2_w_up", "ffn2_w_down", "final_norm"]
SHARDED = ["ffn1_w_gate", "ffn1_w_up", "ffn1_w_down", "w_in", "s5_w_glu", "conv_w_dw", "w_out", "ffn2_w_gate", "ffn2_w_up",
           "ffn2_w_down"]
REPLICATED = [n for n in WEIGHTS if n not in SHARDED]
TRANSPOSED = ["ffn1_w_gate", "ffn1_w_up", "ffn2_w_gate", "ffn2_w_up", "w_in"]


def _shard_to_wire(n, w):
    if n == "conv_w_dw":
        return jnp.pad(w, ((0, CONV_HALO - CONV_K), (0, 0)))
    return w.astype(bf16)


def _to_wire(shards, ride):
    names = list(shards)
    shapes = [jax.eval_shape(functools.partial(_shard_to_wire, n), shards[n]) for n in names]

    def body(*refs):
        for src, dst in zip(refs[:len(names)], refs[len(names):]):
            (r, c), (rp, cp) = src.shape, dst.shape
            dst[:r, :c] = src[...].astype(dst.dtype)
            if cp > c:
                dst[:, c:] = jnp.zeros((rp, cp - c), dst.dtype)
            if rp > r:
                dst[r:, :] = jnp.zeros((rp - r, cp), dst.dtype)

    out = _pallas(body, ride=ride, name="to_wire", out_shape=shapes, in_specs=[pl.BlockSpec(memory_space=pltpu.VMEM)] * len(names),
                  out_specs=[pl.BlockSpec(memory_space=pltpu.VMEM)] * len(names))(*[shards[n] for n in names])
    return dict(zip(names, out))


def _gathered_to_full(n, g):
    if n == "conv_w_dw":
        return g.transpose(1, 0, 2).reshape(CONV_HALO, CONV_WIDTH)[:CONV_K]
    return g.reshape(N_DEV * g.shape[1], g.shape[2])


def _grad_to_blocks(n, g):
    if n == "conv_w_dw":
        g = jnp.pad(g, ((0, CONV_HALO - CONV_K), (0, 0)))
        g = g.reshape(g.shape[0], N_DEV, g.shape[1] // N_DEV).transpose(1, 0, 2)
    else:
        g = g.reshape(N_DEV, g.shape[0] // N_DEV, g.shape[1])
    return g.astype(bf16).reshape(N_CHIP, 2, *g.shape[1:])


REPLICATED_LATE = ["ffn1_norm"]
REPLICATED_HEAD = ["ffn2_norm", "final_norm"]
REPLICATED_MIX = ["mix_norm", "conv_b_dw", "conv_ln_g", "conv_ln_b"]
REPLICATED_S5 = [n for n in REPLICATED if n not in REPLICATED_LATE + REPLICATED_HEAD + REPLICATED_MIX]
REPLICATED_EARLY = REPLICATED_HEAD + REPLICATED_S5 + REPLICATED_MIX

PLAN = {
    "start": [("gather", ["ffn1_w_gate", "ffn1_w_up"])],
    "ffn1_up": [("gather", ["ffn1_w_down", "w_in", "w_out", "s5_w_glu", "conv_w_dw"])],
    "ffn1_down": [("gather", ["ffn2_w_gate"])],
    "s5_forward": [("gather", ["ffn2_w_up"])],
    "conv_fwd": [("gather", ["ffn2_w_down"])],
    "ffn2_dw_up": [("sibling", ["ffn2_w_gate"])],
    "ffn2_dw_down": [("sibling", ["ffn2_w_up"])],
    "mix_out_bwd": [("sibling", ["ffn2_w_down"]), ("replicated", REPLICATED_HEAD)],
    "s5_backward": [("chips", ["ffn2_w_gate", "ffn2_w_up"])],
    "conv_bwd_taps": [("chips", ["ffn2_w_down"])],
    "mix_in_bwd": [("replicated", REPLICATED_S5)],
    "ffn1_dw_down": [("sibling", ["w_in", "s5_w_glu", "conv_w_dw", "w_out"]), ("replicated", REPLICATED_MIX)],
    "ffn1_bwd_act": [("chips", ["w_in", "s5_w_glu", "conv_w_dw", "w_out"]), ("sibling", ["ffn1_w_down"])],
    "ffn1_dw_gate": [("chips", ["ffn1_w_down"])],
    "ffn1_dw_up": [("sibling", ["ffn1_w_gate"])],
    "ffn1_bwd_in_0": [("chips", ["ffn1_w_gate"]), ("sibling", ["ffn1_w_up"])],
    "ffn1_bwd_in_1": [("chips", ["ffn1_w_up"])],
    "tail": [("replicated", REPLICATED_LATE)],
}


class _Schedule:
    def __init__(self, wire, p, grads, core):
        self.wire, self.p, self.grads, self.core = wire, p, grads, core
        self.partial, self.reduced, self.pending = {}, {}, []
        self.stacks, self.shapes, self.everyone = [], [], {}

    def before(self, point):
        assert not self.pending
        for kind, names in PLAN.get(point, ()):
            if kind == "gather":
                given = [self.wire[n] for n in names]
                ex = _gather(given)
            elif kind == "sibling":
                given = [_grad_to_blocks(n, self.grads[n]) for n in names]
                ex = _swap_with_sibling(given)
            elif kind == "chips":
                given = [self.partial.pop(n) for n in names]
                ex = _swap_with_chips(given)
            else:
                names = names + ["loss_terms"] * (names is REPLICATED_HEAD)
                by_shape = {}
                for n in names:
                    by_shape.setdefault(self.p[n].shape if n in self.p else None, []).append(n)
                given = []
                for shape, members in by_shape.items():
                    for k, n in enumerate(members):
                        self.everyone[n] = (len(self.stacks) + len(given), k if len(members) > 1 else None)
                    parts = [self.grads[n].reshape(shape) if shape else self.grads[n] for n in members]
                    whole = jnp.stack(parts) if len(members) > 1 else parts[0]
                    self.shapes.append(whole.shape)
                    if whole.ndim >= 3 and whole.shape[-1] < 128:
                        whole = whole.reshape(*whole.shape[:-2], -1)
                    given.append(whole)
                ex = _gather(given)
            self.pending.append((kind, names, given, ex))
        return [ex for _, _, _, ex in self.pending]

    def after(self, point):
        for kind, names, given, ex in self.pending:
            if kind == "gather":
                for n, g in zip(names, ex.out):
                    self.p[n] = _gathered_to_full(n, g)
            elif kind == "sibling":
                if len(names) > 1:
                    sums = _add_sibling_small(list(zip(given, ex.out)), self.core, "reduce_add_" + names[0])
                else:
                    sums = [_add_sibling(given[0], ex.out[0], self.core, "reduce_add_" + names[0])]
                self.partial.update(zip(names, sums))
            elif kind == "chips":
                for n, part, got in zip(names, given, ex.out):
                    self.reduced[n] = (part, got)
            else:
                self.stacks += [g.reshape(N_DEV, *self.shapes[len(self.stacks) + i]) for i, g in enumerate(ex.out)]
        self.pending = []

    def alone(self, point):
        _exchange(self.before(point), point)
        self.after(point)


def kernel(x, ffn1_norm, ffn1_w_gate, ffn1_w_up, ffn1_w_down, mix_norm, w_in, s5_lam_re, s5_lam_im, s5_log_dt, s5_b_re, s5_b_im, s5_c_re, s5_c_im, s5_d, s5_w_glu, s5_b_glu, conv_w_dw, conv_b_dw, conv_ln_g, conv_ln_b, w_out, ffn2_norm, ffn2_w_gate, ffn2_w_up, ffn2_w_down, final_norm, loss_target, m_ffn1_norm, m_ffn1_w_gate, m_ffn1_w_up, m_ffn1_w_down, m_mix_norm, m_w_in, m_s5_lam_re, m_s5_lam_im, m_s5_log_dt, m_s5_b_re, m_s5_b_im, m_s5_c_re, m_s5_c_im, m_s5_d, m_s5_w_glu, m_s5_b_glu, m_conv_w_dw, m_conv_b_dw, m_conv_ln_g, m_conv_ln_b, m_w_out, m_ffn2_norm, m_ffn2_w_gate, m_ffn2_w_up, m_ffn2_w_down, m_final_norm, v_ffn1_norm, v_ffn1_w_gate, v_ffn1_w_up, v_ffn1_w_down, v_mix_norm, v_w_in, v_s5_lam_re, v_s5_lam_im, v_s5_log_dt, v_s5_b_re, v_s5_b_im, v_s5_c_re, v_s5_c_im, v_s5_d, v_s5_w_glu, v_s5_b_glu, v_conv_w_dw, v_conv_b_dw, v_conv_ln_g, v_conv_ln_b, v_w_out, v_ffn2_norm, v_ffn2_w_gate, v_ffn2_w_up, v_ffn2_w_down, v_final_norm):
    args = locals()
    w = {n: args[n] for n in WEIGHTS}
    m = {n: args["m_" + n] for n in WEIGHTS}
    v = {n: args["v_" + n] for n in WEIGHTS}
    xq, yq, cq = _place()
    q = 2 * xq + yq
    slots = jnp.stack([q, q ^ 1, q ^ 2, q ^ 3]).astype(jnp.int32)

    def shard2d(n, a):
        a = a.reshape(a.shape[-2:])
        return a.T if n in TRANSPOSED else a

    def view(n, a):
        if n.startswith("s5_b_") and a.ndim == 4:
            return a[0].transpose(0, 2, 1)
        return a[0] if a.ndim >= 3 else a.reshape(1, -1)

    def unview(n, a):
        return (a.transpose(0, 2, 1) if n.startswith("s5_b_") and a.ndim == 3 else a).reshape(w[n].shape)

    p = {n: view(n, w[n]) for n in REPLICATED}
    grads = {}
    first = PLAN["start"][0][1]
    wire = {n: _shard_to_wire(n, shard2d(n, w[n])) for n in first}
    sched = _Schedule(wire, p, grads, jnp.reshape(cq, (1,)).astype(jnp.int32))
    wire.update(_to_wire({n: shard2d(n, w[n]) for n in SHARDED if n not in first}, sched.before("start")))
    sched.after("start")
    _, dx = _local_step(x[0], loss_target[0], p, grads, sched)

    out = {}
    groups = [[n for n in SHARDED if n.startswith(tag)] for tag in ("ffn1", "ffn2")]
    for names in groups + [[n for n in SHARDED if not n.startswith("ffn")]]:
        def fit(n, a):
            a = shard2d(n, a)
            return jnp.pad(a, ((0, sched.reduced[n][1].shape[1] - a.shape[0]), (0, 0)))

        items = [(fit(n, w[n]), fit(n, m[n]), fit(n, v[n]), *sched.reduced[n]) for n in names]
        update = _adam_sharded if names[0].startswith("ffn") else _adam_small
        for n, res in zip(names, update(items, slots, "adam_" + names[0])):
            back = lambda r: r[:shard2d(n, w[n]).shape[0]]
            out[n] = [(back(r).T if n in TRANSPOSED else back(r)).reshape(w[n].shape) for r in res]

    for names in (REPLICATED_EARLY, REPLICATED_LATE):
        loss_at = sched.everyone["loss_terms"] if names is REPLICATED_EARLY else None
        used = sorted({sched.everyone[n][0] for n in names} | ({loss_at[0]} if loss_at else set()))
        at = lambda where: (used.index(where[0]), where[1])
        items = [(view(n, w[n]), view(n, m[n]), view(n, v[n]), at(sched.everyone[n])) for n in names]
        res, total = _adam_replicated(items, [sched.stacks[s] for s in used], loss_at and at(loss_at), "adam_" + names[0])
        for n, r in zip(names, res):
            out[n] = [unview(n, a) for a in r]
        if total is not None:
            loss = total.reshape(())

    return (loss, dx.reshape(x.shape), *[out[n][0] for n in WEIGHTS], *[out[n][1] for n in WEIGHTS],
            *[out[n][2] for n in WEIGHTS], *[out[n][3] for n in WEIGHTS])
```

```python
import functools

import jax
import jax.numpy as jnp
from jax import lax
from jax.experimental import pallas as pl
from jax.experimental.pallas import tpu as pltpu

f32 = jnp.float32
bf16 = jnp.bfloat16
S = jax.ShapeDtypeStruct

N_DEV = 8
N_CHIP = 4
D_MODEL = 1024
D_FF = 2816
FF_CHUNKS = [(0, 768), (768, 1536), (1536, 2304), (2304, D_FF)]
S5_WIDTH = 512
S5_GROUPS = 32
S5_GROUP_CH = 16
S5_STATE = 64
S5_LANES = S5_GROUPS * S5_STATE
CONV_WIDTH = 512
CONV_K = 31
CONV_HALO = 32
CONV_HEAD = 64
CONV_ROWS = 32
IN_COLS = S5_WIDTH + 2 * CONV_WIDTH
SEGMENTS = 8
SCAN_LANES = 512
EPS = 1e-6
LR, B1, B2, ADAM_EPS, WD, STEP = 0.001, 0.9, 0.999, 1e-08, 0.01, 10
VMEM_LIMIT = 56 * 1024 * 1024

NN = (((1,), (0,)), ((), ()))
NT = (((1,), (1,)), ((), ()))
TN = (((0,), (0,)), ((), ()))


def _dot(a, b, dims=NN):
    return lax.dot_general(a, b, dims, preferred_element_type=f32)


def _cp(*sem):
    return pltpu.CompilerParams(dimension_semantics=sem, vmem_limit_bytes=VMEM_LIMIT)


def _rms(x, g):
    return x * lax.rsqrt(jnp.mean(x * x, axis=-1, keepdims=True) + EPS) * g


def _rms_bwd(x, g, dh):
    _, vjp = jax.vjp(_rms, x, g)
    return vjp(dh)


def _sigmoid(x):
    return 1.0 / (1.0 + jnp.exp(-x))


def _gelu(x):
    return 0.5 * x * (1.0 + jnp.tanh(0.7978845608028654 * (x + 0.044715 * x * x * x)))


def _rows8(x):
    t, c = x.shape
    return x.reshape(t // 8, 8, c).sum(axis=0)


def _full(shape):
    return pl.BlockSpec(shape, lambda *_: (0,) * len(shape))


def _resident(shape):
    return pl.BlockSpec(shape, lambda *_: (0,) * len(shape), pipeline_mode=pl.Buffered(1))


def _ffn_up(x, g, wg, wu, tm, tag, mixed=None, ride=()):
    L = x.shape[0]

    def body(x_ref, g_ref, wg_ref, wu_ref, *rest):
        h_ref, dadg_ref, dadu_ref, a_ref = rest[-5:-1] if mixed else rest[-4:]
        x = x_ref[...]
        if mixed:
            ys_ref, yc_ref, wo_ref = rest[:3]
            x = x + _dot(ys_ref[...], wo_ref[:S5_WIDTH, :]) + _dot(yc_ref[...], wo_ref[S5_WIDTH:, :])
            rest[-1][...] = x
        h = _rms(x, g_ref[...]).astype(bf16)
        h_ref[...] = h
        for lo, hi in FF_CHUNKS:
            cols = slice(lo, hi)
            gate =_dot(h, wg_ref[cols, :], NT)
            up = _dot(h, wu_ref[cols, :], NT)
            sig = _sigmoid(gate)
            silu = gate * sig
            dadg_ref[:, cols] = (up * (sig + silu * (1.0 - sig))).astype(bf16)
            dadu_ref[:, cols] = silu.astype(bf16)
            a_ref[:, cols] = (silu * up).astype(bf16)

    row = pl.BlockSpec((tm, D_MODEL), lambda i: (i, 0))
    wide = pl.BlockSpec((tm, D_FF), lambda i: (i, 0))
    half = pl.BlockSpec((tm, S5_WIDTH), lambda i: (i, 0))
    return _pallas(
        body, ride=ride, name=tag + "_up", grid=(L // tm,),
        in_specs=[row, _full((1, D_MODEL)), _resident((D_FF, D_MODEL)), _resident((D_FF, D_MODEL))]
        + ([half, half, _resident((D_MODEL, D_MODEL))] if mixed else []),
        out_specs=[row, wide, wide, wide] + [row] * bool(mixed),
        out_shape=[S((L, D_MODEL), bf16)] + [S((L, D_FF), bf16)] * 3 + [S((L, D_MODEL), f32)] * bool(mixed),
        compiler_params=_cp("parallel"),
    )(x, g, wg, wu, *(mixed or ()))


def _ffn_down(x, a, wd, tm, tag, mixer=None, ride=()):
    L = x.shape[0]

    def body(x_ref, a_ref, wd_ref, *rest):
        xo = x_ref[...] + 0.5 * _dot(a_ref[...], wd_ref[...])
        if not mixer:
            rest[0][...] = xo
            return
        g_ref, w_ref, o_ref, h_ref, us_ref, v_ref = rest
        o_ref[...] = xo
        h = _rms(xo, g_ref[...]).astype(bf16)
        h_ref[...] = h
        u = _dot(h, w_ref[...], NT)
        us_ref[...] = u[:, :S5_WIDTH]
        v_ref[...] = u[:, S5_WIDTH:]

    row = lambda c: pl.BlockSpec((tm, c), lambda i: (i, 0))
    extra_in = [_full((1, D_MODEL)), _resident((IN_COLS, D_MODEL))] if mixer else []
    extra_out = [(D_MODEL, bf16), (S5_WIDTH, f32), (2 * CONV_WIDTH, f32)] if mixer else []
    out = _pallas(
        body, ride=ride, name=tag + "_down", grid=(L // tm,),
        in_specs=[row(D_MODEL), row(D_FF), _resident((D_FF, D_MODEL))] + extra_in,
        out_specs=[row(D_MODEL)] + [row(c) for c, _ in extra_out],
        out_shape=[S((L, D_MODEL), f32)] + [S((L, c), t) for c, t in extra_out],
        compiler_params=_cp("parallel"),
    )(x, a, wd, *(mixer or ()))
    return out if mixer else out[0]


def _ffn_down_loss(x, a, wd, target, g, tm, tag):
    L = x.shape[0]

    def body(x_ref, a_ref, wd_ref, t_ref, g_ref, dx_ref, dg_ref, l_ref):
        @pl.when(pl.program_id(0) == 0)
        def _():
            dg_ref[...] = jnp.zeros_like(dg_ref)
            l_ref[...] = jnp.zeros_like(l_ref)

        xo = x_ref[...] + 0.5 * _dot(a_ref[...], wd_ref[...])
        g = g_ref[...]
        e = _rms(xo, g) - t_ref[...]
        l_ref[...] += _rows8(e * e) * (0.5 / D_MODEL)
        dx, dg = _rms_bwd(xo, g, e * (1.0 / D_MODEL))
        dx_ref[...] = dx
        dg_ref[...] += dg

    row = pl.BlockSpec((tm, D_MODEL), lambda i: (i, 0))
    return pl.pallas_call(
        body, name=tag + "_down_loss", grid=(L // tm,),
        in_specs=[row, pl.BlockSpec((tm, D_FF), lambda i: (i, 0)), _resident((D_FF, D_MODEL)), row, _full((1, D_MODEL))],
        out_specs=[row, _full((1, D_MODEL)), _full((8, D_MODEL))],
        out_shape=[S((L, D_MODEL), f32), S((1, D_MODEL), f32), S((8, D_MODEL), f32)],
        compiler_params=_cp("arbitrary"),
    )(x, a, wd, target, g)


def _ffn_bwd_act(dxo, wd, dadg, dadu, tm, tag, ride=()):
    L = dxo.shape[0]

    def body(dx_ref, wd_ref, dadg_ref, dadu_ref, dgate_ref, dup_ref, dxh_ref):
        dxh = (0.5 * dx_ref[...]).astype(bf16)
        dxh_ref[...] = dxh
        for lo, hi in FF_CHUNKS:
            cols = slice(lo, hi)
            da =_dot(dxh, wd_ref[cols, :], NT)
            dgate_ref[:, cols] = (da * dadg_ref[:, cols].astype(f32)).astype(bf16)
            dup_ref[:, cols] = (da * dadu_ref[:, cols].astype(f32)).astype(bf16)

    row = pl.BlockSpec((tm, D_MODEL), lambda i: (i, 0))
    wide = pl.BlockSpec((tm, D_FF), lambda i: (i, 0))
    return _pallas(
        body, ride=ride, name=tag + "_bwd_act", grid=(L // tm,),
        in_specs=[row, _resident((D_FF, D_MODEL)), wide, wide],
        out_specs=[wide, wide, row],
        out_shape=[S((L, D_FF), bf16), S((L, D_FF), bf16), S((L, D_MODEL), bf16)],
        compiler_params=_cp("parallel"),
    )(dxo, wd, dadg, dadu)


def _ffn_bwd_in(dxo, x, g, dgate, dup, wg, wu, tm, name, tiles=None, into=None, ride=()):
    L = x.shape[0]
    first, count = tiles or (0, L // tm)

    def body(dxo_ref, x_ref, g_ref, dgate_ref, dup_ref, wg_ref, wu_ref, *rest):
        dx_ref, dg_ref = rest[-2:]

        @pl.when(pl.program_id(0) == 0)
        def _():
            dg_ref[...] = jnp.zeros_like(dg_ref)

        dh = _dot(dgate_ref[...], wg_ref[...]) + _dot(dup_ref[...], wu_ref[...])
        dx, dg = _rms_bwd(x_ref[...], g_ref[...], dh)
        dx_ref[...] = dxo_ref[...] + dx
        dg_ref[...] += dg

    row = pl.BlockSpec((tm, D_MODEL), lambda i: (first + i, 0))
    wide = pl.BlockSpec((tm, D_FF), lambda i: (first + i, 0))
    return _pallas(
        body, ride=ride, name=name, grid=(count,),
        in_specs=[row, row, _full((1, D_MODEL)), wide, wide, _resident((D_FF, D_MODEL)), _resident((D_FF, D_MODEL))]
        + [ANY] * (into is not None),
        out_specs=[row, _full((1, D_MODEL))],
        out_shape=[S((L, D_MODEL), f32), S((1, D_MODEL), f32)],
        input_output_aliases={7: 0} if into is not None else {},
        compiler_params=_cp("arbitrary"),
    )(dxo, x, g, dgate, dup, wg, wu, *([into] if into is not None else []))


def _mm_tn(a, b, out_dtype, name, tm=512, tn=1024, ride=()):
    L, M = a.shape
    N = b.shape[1]
    tm, tn = min(tm, M), min(tn, N)
    while M % tm:
        tm //= 2
    while N % tn:
        tn //= 2

    def body(a_ref, b_ref, o_ref):
        o_ref[...] = _dot(a_ref[...].astype(bf16), b_ref[...].astype(bf16), TN).astype(out_dtype)

    return _pallas(
        body, ride=ride, name=name, grid=(M // tm, N // tn),
        in_specs=[pl.BlockSpec((L, tm), lambda i, j: (0, i)), pl.BlockSpec((L, tn), lambda i, j: (0, j))],
        out_specs=pl.BlockSpec((tm, tn), lambda i, j: (i, j)),
        out_shape=S((M, N), out_dtype),
        compiler_params=_cp("parallel", "parallel"),
    )(a, b)


def _mix_in_bwd(dxo, x, g, du_s5, dv, w_in, tm, ride=()):
    L = x.shape[0]

    def body(dxo_ref, x_ref, g_ref, dus_ref, dv_ref, w_ref, dx_ref, dg_ref, dub_ref, dxh_ref):
        @pl.when(pl.program_id(0) == 0)
        def _():
            dg_ref[...] = jnp.zeros_like(dg_ref)

        dus = dus_ref[...].astype(bf16)
        dvb = dv_ref[...].astype(bf16)
        dub_ref[:, :S5_WIDTH] = dus
        dub_ref[:, S5_WIDTH:] = dvb
        dh = _dot(dus, w_ref[:S5_WIDTH, :]) + _dot(dvb, w_ref[S5_WIDTH:, :])
        dx, dg = _rms_bwd(x_ref[...], g_ref[...], dh)
        dx = dxo_ref[...] + dx
        dx_ref[...] = dx
        dxh_ref[...] = (0.5 * dx).astype(bf16)
        dg_ref[...] += dg

    row = lambda c: pl.BlockSpec((tm, c), lambda i: (i, 0))
    return _pallas(
        body, ride=ride, name="mix_in_bwd", grid=(L // tm,),
        in_specs=[row(D_MODEL), row(D_MODEL), _full((1, D_MODEL)), row(S5_WIDTH), row(2 * CONV_WIDTH),
                  _full((IN_COLS, D_MODEL))],
        out_specs=[row(D_MODEL), _full((1, D_MODEL)), row(IN_COLS), row(D_MODEL)],
        out_shape=[S((L, D_MODEL), f32), S((1, D_MODEL), f32), S((L, IN_COLS), bf16), S((L, D_MODEL), bf16)],
        compiler_params=_cp("arbitrary"),
    )(dxo, x, g, du_s5, dv, w_in)


def _dw_out(y_s5, y_conv, dxb, tn=512):
    L = dxb.shape[0]

    def body(ys_ref, yc_ref, b_ref, o_ref):
        b = b_ref[...]
        o_ref[:S5_WIDTH, :] = _dot(ys_ref[...], b, TN).astype(bf16)
        o_ref[S5_WIDTH:, :] = _dot(yc_ref[...], b, TN).astype(bf16)

    return pl.pallas_call(
        body, name="dw_out", grid=(D_MODEL // tn,),
        in_specs=[_full((L, S5_WIDTH)), _full((L, CONV_WIDTH)), pl.BlockSpec((L, tn), lambda j: (0, j))],
        out_specs=pl.BlockSpec((S5_WIDTH + CONV_WIDTH, tn), lambda j: (0, j)),
        out_shape=S((S5_WIDTH + CONV_WIDTH, D_MODEL), bf16),
        compiler_params=_cp("parallel"),
    )(y_s5, y_conv, dxb)


def _mix_out_bwd(dx, w_out, tm, ride=()):
    L = dx.shape[0]

    def body(dx_ref, w_ref, dys_ref, dyc_ref, dxb_ref):
        dxb = dx_ref[...].astype(bf16)
        dxb_ref[...] = dxb
        dys_ref[...] = _dot(dxb, w_ref[:S5_WIDTH, :], NT)
        dyc_ref[...] = _dot(dxb, w_ref[S5_WIDTH:, :], NT)

    row = lambda c: pl.BlockSpec((tm, c), lambda i: (i, 0))
    return _pallas(
        body, ride=ride, name="mix_out_bwd", grid=(L // tm,),
        in_specs=[row(D_MODEL), _full((D_MODEL, D_MODEL))],
        out_specs=[row(S5_WIDTH), row(CONV_WIDTH), row(D_MODEL)],
        out_shape=[S((L, S5_WIDTH), f32), S((L, CONV_WIDTH), f32), S((L, D_MODEL), bf16)],
        compiler_params=_cp("parallel"),
    )(dx, w_out)


def _s5_discretise(lam_re, lam_im, log_dt, b_re, b_im):
    dt = jnp.exp(log_dt)
    mag = jnp.exp(lam_re * dt)
    abar_re = mag * jnp.cos(lam_im * dt)
    abar_im = mag * jnp.sin(lam_im * dt)
    den = lam_re * lam_re + lam_im * lam_im
    num_re = abar_re - 1.0
    f_re = ((num_re * lam_re + abar_im * lam_im) / den)[:, None, :]
    f_im = ((abar_im * lam_re - num_re * lam_im) / den)[:, None, :]
    return abar_re, abar_im, f_re * b_re - f_im * b_im, f_re * b_im + f_im * b_re


def _s5_params(lam_re, lam_im, log_dt, b_re, b_im):
    def body(lr, li, ld, br, bi, ar_ref, ai_ref, bbr_ref, bbi_ref):
        ar, ai, bbr, bbi = _s5_discretise(lr[...], li[...], ld[...], br[...], bi[...])
        ar_ref[...], ai_ref[...], bbr_ref[...], bbi_ref[...] = ar, ai, bbr, bbi

    gp = S((S5_GROUPS, S5_STATE), f32)
    gcp = S((S5_GROUPS, S5_GROUP_CH, S5_STATE), f32)
    return pl.pallas_call(body, name="s5_params", out_shape=[gp, gp, gcp, gcp])(lam_re, lam_im, log_dt, b_re, b_im)


def _s5_params_bwd(lam_re, lam_im, log_dt, b_re, b_im, d_ar, d_ai, d_bbr, d_bbi):
    def body(lr, li, ld, br, bi, car, cai, cbr, cbi, o_lr, o_li, o_ld, o_br, o_bi):
        _, vjp = jax.vjp(_s5_discretise, lr[...], li[...], ld[...], br[...], bi[...])
        o_lr[...], o_li[...], o_ld[...], o_br[...], o_bi[...] = vjp((car[...], cai[...], cbr[...], cbi[...]))

    gp = S((S5_GROUPS, S5_STATE), f32)
    gcp = S((S5_GROUPS, S5_GROUP_CH, S5_STATE), f32)
    return pl.pallas_call(body, name="s5_params_bwd", out_shape=[gp, gp, S((S5_GROUPS, 1), f32), gcp, gcp])(
        lam_re, lam_im, log_dt, b_re, b_im, d_ar, d_ai, d_bbr, d_bbi)


def _cmul(ar, ai, br, bi):
    return ar * br - ai * bi, ar * bi + ai * br


def _segment_starts(er, ei, ar, ai, steps, reverse):
    pr, pi = ar, ai
    n = 1
    while n < steps:
        pr, pi = _cmul(pr, pi, pr, pi)
        n *= 2
    assert n == steps
    row = lax.broadcasted_iota(jnp.int32, (SEGMENTS, SCAN_LANES), 0)
    hr = jnp.zeros((1, SCAN_LANES), f32)
    hi = jnp.zeros((1, SCAN_LANES), f32)
    out_r = jnp.zeros((SEGMENTS, SCAN_LANES), f32)
    out_i = jnp.zeros((SEGMENTS, SCAN_LANES), f32)
    order = range(SEGMENTS - 1, 0, -1) if reverse else range(0, SEGMENTS - 1)
    for r in order:
        qr, qi = _cmul(pr, pi, hr, hi)
        hr, hi = qr + er[r:r + 1, :], qi + ei[r:r + 1, :]
        nxt = r - 1 if reverse else r + 1
        out_r = jnp.where(row == nxt, hr, out_r)
        out_i = jnp.where(row == nxt, hi, out_i)
    return out_r, out_i


def _s5_read_bwd(dout, y_lin, u, d_skip, w_glu, b_glu, tm):
    L = u.shape[0]

    def body(do_ref, yl_ref, u_ref, d_ref, w_ref, b_ref, dyl_ref, du_ref, dd_ref, dw_ref, db_ref):
        @pl.when(pl.program_id(0) == 0)
        def _():
            dd_ref[...] = jnp.zeros_like(dd_ref)
            dw_ref[...] = jnp.zeros_like(dw_ref)
            db_ref[...] = jnp.zeros_like(db_ref)

        u, d, dout = u_ref[...], d_ref[...], do_ref[...]
        y, gelu_vjp = jax.vjp(_gelu, yl_ref[...] + d * u)
        yb = y.astype(bf16)
        sig = _sigmoid(_dot(yb, w_ref[...]) + b_ref[...])
        dz = dout * y * sig * (1.0 - sig)
        dzb = dz.astype(bf16)
        dy = dout * sig + _dot(dzb, w_ref[...], NT)
        (dyp,) = gelu_vjp(dy)
        dyl_ref[...] = dyp.astype(bf16)
        du_ref[...] = d * dyp
        dd_ref[...] += _rows8(dyp * u)
        db_ref[...] += _rows8(dz)
        dw_ref[...] += _dot(yb, dzb, TN)

    row = pl.BlockSpec((tm, S5_WIDTH), lambda i: (i, 0))
    vec = _full((1, S5_WIDTH))
    part = _full((8, S5_WIDTH))
    return pl.pallas_call(
        body, name="s5_read_bwd", grid=(L // tm,),
        in_specs=[row, row, row, vec, _full((S5_WIDTH, S5_WIDTH)), vec],
        out_specs=[row, row, part, _full((S5_WIDTH, S5_WIDTH)), part],
        out_shape=[S((L, S5_WIDTH), bf16), S((L, S5_WIDTH), f32), S((8, S5_WIDTH), f32),
                   S((S5_WIDTH, S5_WIDTH), f32), S((8, S5_WIDTH), f32)],
        compiler_params=_cp("arbitrary"),
    )(dout, y_lin, u, d_skip, w_glu, b_glu)


S5_CHUNK_CH = SCAN_LANES // S5_STATE * S5_GROUP_CH


def _s5_two_phase(L, bi):
    rows = bi * SEGMENTS
    nb = L // rows
    whole = pltpu.VMEM((L // SEGMENTS, SEGMENTS, SCAN_LANES), f32)
    mat = pl.BlockSpec((S5_CHUNK_CH, SCAN_LANES), lambda c, j: (c, c))
    vec = pl.BlockSpec((1, SCAN_LANES), lambda c, j: (0, c))
    tile = pl.BlockSpec((SEGMENTS, SCAN_LANES), lambda c, j: (0, c))
    return rows, nb, whole, mat, vec, tile


def _s5_forward(u, a_re, a_im, bb_re, bb_im, cc_re, cc_im, bi, ride=()):
    L = u.shape[0]
    rows, nb, whole, mat, vec, _ = _s5_two_phase(L, bi)

    def body(u_ref, ar_ref, ai_ref, br_ref, bi_ref, cr_ref, ci_ref, sr_ref, si_ref, yl_ref, hr_ref, hi_ref, dr_ref, di_ref):
        j = pl.program_id(1)
        ar = jnp.broadcast_to(ar_ref[...], (SEGMENTS, SCAN_LANES))
        ai = jnp.broadcast_to(ai_ref[...], (SEGMENTS, SCAN_LANES))

        @pl.when(j == 0)
        def _():
            hr_ref[...] = jnp.zeros_like(hr_ref)
            hi_ref[...] = jnp.zeros_like(hi_ref)

        @pl.when(j < nb)
        def _():
            base = j * bi
            ub = u_ref[...].astype(bf16)
            dr_ref[pl.ds(base, bi)] = _dot(ub, br_ref[...]).reshape(bi, SEGMENTS, SCAN_LANES)
            di_ref[pl.ds(base, bi)] = _dot(ub, bi_ref[...]).reshape(bi, SEGMENTS, SCAN_LANES)

            def step(i, c):
                pr, pi = _cmul(ar, ai, c[0], c[1])
                return pr + dr_ref[base + i], pi + di_ref[base + i]

            hr_ref[...], hi_ref[...] = lax.fori_loop(0, bi, step, (hr_ref[...], hi_ref[...]), unroll=True)

        @pl.when(j == nb - 1)
        def _():
            hr_ref[...], hi_ref[...] = _segment_starts(hr_ref[...], hi_ref[...], ar_ref[...], ai_ref[...], L // SEGMENTS, False)

        @pl.when(j >= nb)
        def _():
            base = (j - nb) * bi

            def step(i, c):
                pr, pi = _cmul(ar, ai, c[0], c[1])
                nr, nim = pr + dr_ref[base + i], pi + di_ref[base + i]
                dr_ref[base + i] = nr
                di_ref[base + i] = nim
                return nr, nim

            hr_ref[...], hi_ref[...] = lax.fori_loop(0, bi, step, (hr_ref[...], hi_ref[...]), unroll=True)
            sr = dr_ref[pl.ds(base, bi)].reshape(rows, SCAN_LANES).astype(bf16)
            si = di_ref[pl.ds(base, bi)].reshape(rows, SCAN_LANES).astype(bf16)
            sr_ref[...] = sr
            si_ref[...] = si
            yl_ref[...] = _dot(sr, cr_ref[...], NT) - _dot(si, ci_ref[...], NT)

    u_spec = pl.BlockSpec((rows, S5_CHUNK_CH), lambda c, j: (jnp.minimum(j, nb - 1), c))
    late = lambda width: pl.BlockSpec((rows, width), lambda c, j: (jnp.maximum(j - nb, 0), c))
    return _pallas(
        body, ride=ride, name="s5_forward", grid=(S5_LANES // SCAN_LANES, 2 * nb),
        in_specs=[u_spec, vec, vec, mat, mat, mat, mat],
        out_specs=[late(SCAN_LANES), late(SCAN_LANES), late(S5_CHUNK_CH)],
        out_shape=[S((L, S5_LANES), bf16)] * 2 + [S((L, S5_WIDTH), f32)],
        scratch_shapes=[pltpu.VMEM((SEGMENTS, SCAN_LANES), f32)] * 2 + [whole] * 2,
        compiler_params=_cp("parallel", "arbitrary"),
    )(u, a_re, a_im, bb_re, bb_im, cc_re, cc_im)


def _s5_backward(dy, u, du_skip, s_re, s_im, a_re, a_im, bb_re, bb_im, cc_re, cc_im, bi, ride=()):
    L = u.shape[0]
    rows, nb, whole, mat, vec, tile = _s5_two_phase(L, bi)
    per = rows // 16

    def body(dy_ref, u_ref, dus_ref, sr_ref, si_ref, pr_ref, pi_ref, lr_ref, li_ref, ar_ref, ai_ref, br_ref, bi_ref, cr_ref,
             ci_ref, du_ref, dar_ref, dai_ref, dbr_ref, dbi_ref, dcr_ref, dci_ref, hr_ref, hi_ref, gr_ref, gi_ref, fr_ref, fi_ref):
        j = pl.program_id(1)
        ar = jnp.broadcast_to(ar_ref[...], (SEGMENTS, SCAN_LANES))
        ai = jnp.broadcast_to(ai_ref[...], (SEGMENTS, SCAN_LANES))

        @pl.when(j == 0)
        def _():
            for ref in (hr_ref, hi_ref, dar_ref, dai_ref, dbr_ref, dbi_ref, dcr_ref, dci_ref):
                ref[...] = jnp.zeros_like(ref)

        @pl.when(j < nb)
        def _():
            base = (nb - 1 - j) * bi
            dy = dy_ref[...]
            gr_ref[pl.ds(base, bi)] = _dot(dy, cr_ref[...]).reshape(bi, SEGMENTS, SCAN_LANES)
            gi_ref[pl.ds(base, bi)] = (-_dot(dy, ci_ref[...])).reshape(bi, SEGMENTS, SCAN_LANES)

            def step(n, c):
                i = base + bi - 1 - n
                qr, qi = _cmul(ar, ai, c[0], c[1])
                return qr + gr_ref[i], qi + gi_ref[i]

            hr_ref[...], hi_ref[...] = lax.fori_loop(0, bi, step, (hr_ref[...], hi_ref[...]), unroll=True)

        @pl.when(j == nb - 1)
        def _():
            hr_ref[...], hi_ref[...] = _segment_starts(hr_ref[...], hi_ref[...], ar_ref[...], ai_ref[...], L // SEGMENTS, True)

        @pl.when(j >= nb)
        def _():
            blk = 2 * nb - 1 - j
            base = blk * bi
            sr, si = sr_ref[...], si_ref[...]
            fr_ref[...] = sr.astype(f32).reshape(bi, SEGMENTS, SCAN_LANES)
            fi_ref[...] = si.astype(f32).reshape(bi, SEGMENTS, SCAN_LANES)

            def step(n, c):
                i = bi - 1 - n
                gr, gi, accr, acci = c
                qr, qi = _cmul(ar, ai, gr, gi)
                gr, gi = qr + gr_ref[base + i], qi + gi_ref[base + i]
                gr_ref[base + i] = gr
                gi_ref[base + i] = gi
                pr, pi = fr_ref[i - 1], fi_ref[i - 1]
                return gr, gi, accr + (gr * pr + gi * pi), acci + (gi * pr - gr * pi)

            gr, gi, accr, acci = lax.fori_loop(0, bi - 1, step, (hr_ref[...], hi_ref[...], dar_ref[...], dai_ref[...]), unroll=True)
            qr, qi = _cmul(ar, ai, gr, gi)
            gr, gi = qr + gr_ref[base], qi + gi_ref[base]
            gr_ref[base] = gr
            gi_ref[base] = gi
            hr_ref[...], hi_ref[...] = gr, gi
            row = lax.broadcasted_iota(jnp.int32, (SEGMENTS, SCAN_LANES), 0)
            older = lambda ref: ref[...].astype(f32)[SEGMENTS:, :]
            wrap_r = jnp.where(row == 0, 0.0, pltpu.roll(older(lr_ref), 1, 0))
            wrap_i = jnp.where(row == 0, 0.0, pltpu.roll(older(li_ref), 1, 0))
            pr = jnp.where(blk == 0, wrap_r, older(pr_ref))
            pi = jnp.where(blk == 0, wrap_i, older(pi_ref))
            dar_ref[...] = accr + gr * pr + gi * pi
            dai_ref[...] = acci + gi * pr - gr * pi

            g_re = gr_ref[pl.ds(base, bi)].reshape(rows, SCAN_LANES).astype(bf16)
            g_im = gi_ref[pl.ds(base, bi)].reshape(rows, SCAN_LANES).astype(bf16)
            ub = u_ref[...].astype(bf16)
            dy = dy_ref[...]
            du_ref[...] = dus_ref[...] + _dot(g_re, br_ref[...], NT) + _dot(g_im, bi_ref[...], NT)
            dbr_ref[...] += _dot(ub, g_re, TN)
            dbi_ref[...] += _dot(ub, g_im, TN)
            dcr_ref[...] += _dot(dy, sr, TN)
            dci_ref[...] -= _dot(dy, si, TN)

    block = lambda c, j: jnp.where(j < nb, nb - 1 - j, 2 * nb - 1 - j)
    late_block = lambda c, j: jnp.minimum(2 * nb - 1 - j, nb - 1)
    both = pl.BlockSpec((rows, S5_CHUNK_CH), lambda c, j: (block(c, j), c))
    chan = pl.BlockSpec((rows, S5_CHUNK_CH), lambda c, j: (late_block(c, j), c))
    state = pl.BlockSpec((rows, SCAN_LANES), lambda c, j: (late_block(c, j), c))
    prev = pl.BlockSpec((16, SCAN_LANES), lambda c, j: (jnp.maximum(late_block(c, j) * per - 1, 0), c))
    last = pl.BlockSpec((16, SCAN_LANES), lambda c, j: (L // 16 - 1, c))
    grad = pl.BlockSpec((S5_CHUNK_CH, SCAN_LANES), lambda c, j: (c, 0))
    return _pallas(
        body, ride=ride, name="s5_backward", grid=(S5_LANES // SCAN_LANES, 2 * nb),
        in_specs=[both, chan, chan, state, state, prev, prev, last, last, vec, vec, mat, mat, mat, mat],
        out_specs=[chan, tile, tile, grad, grad, grad, grad],
        out_shape=[S((L, S5_WIDTH), f32)] + [S((SEGMENTS, S5_LANES), f32)] * 2 + [S((S5_WIDTH, SCAN_LANES), f32)] * 4,
        scratch_shapes=[pltpu.VMEM((SEGMENTS, SCAN_LANES), f32)] * 2 + [whole] * 2 + [pltpu.VMEM((bi, SEGMENTS, SCAN_LANES), f32)] * 2,
        compiler_params=_cp("parallel", "arbitrary"),
    )(dy, u, du_skip, s_re, s_im, s_re, s_im, s_re, s_im, a_re, a_im, bb_re, bb_im, cc_re, cc_im)


def _s5_gate(y_lin, u, d_skip, w_glu, b_glu, tm, ride=()):
    L = u.shape[0]

    def body(yl_ref, u_ref, d_ref, w_ref, b_ref, o_ref):
        y = _gelu(yl_ref[...] + d_ref[...] * u_ref[...])
        z = _dot(y.astype(bf16), w_ref[...]) + b_ref[...]
        o_ref[...] = (y * _sigmoid(z)).astype(bf16)

    row = pl.BlockSpec((tm, S5_WIDTH), lambda i: (i, 0))
    vec = _full((1, S5_WIDTH))
    return _pallas(
        body, ride=ride, name="s5_gate", grid=(L // tm,),
        in_specs=[row, row, vec, _full((S5_WIDTH, S5_WIDTH)), vec],
        out_specs=row, out_shape=S((L, S5_WIDTH), bf16),
        compiler_params=_cp("parallel"),
    )(y_lin, u, d_skip, w_glu, b_glu)


def _group_mean(x, avg):
    return _dot(x.astype(bf16), avg)


def _conv_act(zn, ln_g, ln_b):
    t = zn * ln_g + ln_b
    return t * _sigmoid(t)


def _glu_padded(v_ref, halo_ref, zpad_ref, tm):
    v = v_ref[...]
    vh = halo_ref[...]
    zh = vh[:, :CONV_WIDTH] * _sigmoid(vh[:, CONV_WIDTH:])
    zpad_ref[:CONV_HALO, :] = jnp.where(pl.program_id(0) > 0, zh, 0.0)
    zpad_ref[CONV_HALO:CONV_HALO + tm, :] = v[:, :CONV_WIDTH] * _sigmoid(v[:, CONV_WIDTH:])
    zpad_ref[CONV_HALO + tm:, :] = jnp.zeros((8, CONV_WIDTH), f32)


def _shifted(pad_ref, sh_ref, tm):
    for b in range(8):
        sh_ref[b] = pad_ref[pl.ds(b, tm + CONV_HALO), :]


def _window(sh_ref, r0, off, rows):
    start = r0 + 8 * (off // 8)
    return sh_ref[off % 8, pl.ds(start if isinstance(start, int) else pl.multiple_of(start, 8), rows), :]


def _tap_sum(w_ref, sh_ref, taps, out_ref, tm, bias):
    for r0 in range(0, tm, CONV_ROWS):
        acc = jnp.zeros((CONV_ROWS, CONV_WIDTH), f32) + bias
        for k, off in taps:
            acc = acc + w_ref[k:k + 1, :] * _window(sh_ref, r0, off, CONV_ROWS)
        out_ref[r0:r0 + CONV_ROWS, :] = acc


FWD_TAPS = [(k, CONV_HALO - (CONV_K - 1) + k) for k in range(CONV_K)]
BWD_TAPS = [(k, CONV_K - 1 - k) for k in range(CONV_K)]


def _conv_specs(tm):
    per = tm // CONV_HALO
    vrow = pl.BlockSpec((tm, 2 * CONV_WIDTH), lambda i: (i, 0))
    vhalo = pl.BlockSpec((CONV_HALO, 2 * CONV_WIDTH), lambda i: (jnp.maximum(i * per - 1, 0), 0))
    return vrow, vhalo


def _conv_scratch(tm):
    return [pltpu.VMEM((tm + CONV_HALO + 8, CONV_WIDTH), f32), pltpu.VMEM((8, tm + CONV_HALO, CONV_WIDTH), f32)]


def _conv_fwd(v, w_dw, b_dw, ln_g, ln_b, avg, tm, ride=()):
    L = v.shape[0]

    def body(v_ref, halo_ref, w_ref, b_ref, g_ref, bb_ref, avg_ref, o_ref, zc_ref, zpad_ref, zs_ref):
        _glu_padded(v_ref, halo_ref, zpad_ref, tm)
        _shifted(zpad_ref, zs_ref, tm)
        _tap_sum(w_ref, zs_ref, FWD_TAPS, zc_ref, tm, b_ref[...])
        zc = zc_ref[...]
        xc = zc - _group_mean(zc, avg_ref[...])
        zn = xc * lax.rsqrt(_group_mean(xc * xc, avg_ref[...]) + EPS)
        o_ref[...] = _conv_act(zn, g_ref[...], bb_ref[...]).astype(bf16)

    vrow, vhalo = _conv_specs(tm)
    vec = _full((1, CONV_WIDTH))
    row = pl.BlockSpec((tm, CONV_WIDTH), lambda i: (i, 0))
    return _pallas(
        body, ride=ride, name="conv_fwd", grid=(L // tm,),
        in_specs=[vrow, vhalo, _full((CONV_HALO, CONV_WIDTH)), vec, vec, vec, _full((CONV_WIDTH, CONV_WIDTH))],
        out_specs=[row, row], out_shape=[S((L, CONV_WIDTH), bf16), S((L, CONV_WIDTH), f32)],
        scratch_shapes=_conv_scratch(tm),
        compiler_params=_cp("arbitrary"),
    )(v, v, w_dw, b_dw, ln_g, ln_b, avg)


def _conv_bwd_norm(dout, zc, ln_g, ln_b, avg, tm):
    L = zc.shape[0]

    def body(do_ref, zc_ref, g_ref, bb_ref, avg_ref, dzc_ref, dg_ref, db_ref, dbd_ref):
        @pl.when(pl.program_id(0) == 0)
        def _():
            dg_ref[...] = jnp.zeros_like(dg_ref)
            db_ref[...] = jnp.zeros_like(db_ref)
            dbd_ref[...] = jnp.zeros_like(dbd_ref)

        avg = avg_ref[...]
        zc = zc_ref[...]
        xc = zc - _group_mean(zc, avg)
        rstd = lax.rsqrt(_group_mean(xc * xc, avg) + EPS)
        xhat = xc * rstd
        _, act_vjp = jax.vjp(_conv_act, xhat, g_ref[...], bb_ref[...])
        dxhat, dg, db = act_vjp(do_ref[...])
        dzc = rstd * (dxhat - _group_mean(dxhat, avg) - xhat * _group_mean(dxhat * xhat, avg))
        dzc_ref[...] = dzc
        dg_ref[0:1, :] += dg
        db_ref[0:1, :] += db
        dbd_ref[...] += _rows8(dzc)

    vec = _full((1, CONV_WIDTH))
    row = pl.BlockSpec((tm, CONV_WIDTH), lambda i: (i, 0))
    part = _full((8, CONV_WIDTH))
    return pl.pallas_call(
        body, name="conv_bwd_norm", grid=(L // tm,),
        in_specs=[row, row, vec, vec, _full((CONV_WIDTH, CONV_WIDTH))],
        out_specs=[row, part, part, part],
        out_shape=[S((L, CONV_WIDTH), f32)] + [S((8, CONV_WIDTH), f32)] * 3,
        compiler_params=_cp("arbitrary"),
    )(dout, zc, ln_g, ln_b, avg)


def _conv_bwd_taps(dzc, v, w_dw, tm, ride=()):
    L = v.shape[0]
    nt = L // tm
    per = tm // CONV_HALO

    def body(d_ref, dn_ref, v_ref, w_ref, dv_ref, dw_ref, dpad_ref, ds_ref, dz_ref, z_ref):
        i = pl.program_id(0)

        @pl.when(i == 0)
        def _():
            dw_ref[...] = jnp.zeros_like(dw_ref)

        v = v_ref[...]
        sig = _sigmoid(v[:, CONV_WIDTH:])
        z_ref[...] = v[:, :CONV_WIDTH] * sig
        dpad_ref[:tm, :] = d_ref[...]
        dpad_ref[tm:tm + CONV_HALO, :] = jnp.where(i < nt - 1, dn_ref[...], 0.0)
        dpad_ref[tm + CONV_HALO:, :] = jnp.zeros((8, CONV_WIDTH), f32)
        _shifted(dpad_ref, ds_ref, tm)
        _tap_sum(w_ref, ds_ref, BWD_TAPS, dz_ref, tm, 0.0)

        for first in range(0, CONV_K, 8):
            taps = BWD_TAPS[first:first + 8]

            accs = [jnp.zeros((8, CONV_WIDTH), f32) for _ in taps]
            for r0 in range(0, tm, 8):
                z = z_ref[r0:r0 + 8, :]
                accs = [acc + z * _window(ds_ref, r0, off, 8) for acc, (_, off) in zip(accs, taps)]
            for acc, (k, _) in zip(accs, taps):
                dw_ref[k] += acc

        dz = dz_ref[...]
        dv_ref[:, :CONV_WIDTH] = dz * sig
        dv_ref[:, CONV_WIDTH:] = dz * v[:, :CONV_WIDTH] * sig * (1.0 - sig)

    vrow, _ = _conv_specs(tm)
    row = pl.BlockSpec((tm, CONV_WIDTH), lambda i: (i, 0))
    nxt = pl.BlockSpec((CONV_HALO, CONV_WIDTH), lambda i: (jnp.minimum((i + 1) * per, nt * per - 1), 0))
    return _pallas(
        body, ride=ride, name="conv_bwd_taps", grid=(nt,),
        in_specs=[row, nxt, vrow, _full((CONV_HALO, CONV_WIDTH))],
        out_specs=[vrow, _full((CONV_HALO, 8, CONV_WIDTH))],
        out_shape=[S((L, 2 * CONV_WIDTH), f32), S((CONV_HALO, 8, CONV_WIDTH), f32)],
        scratch_shapes=_conv_scratch(tm) + [pltpu.VMEM((tm, CONV_WIDTH), f32)] * 2,
        compiler_params=_cp("arbitrary"),
    )(dzc, dzc, v, w_dw)


def _to_segments(a):
    L, c = a.shape
    return a.reshape(SEGMENTS, L // SEGMENTS, c).transpose(1, 0, 2).reshape(L, c)


def _from_segments(a):
    L, c = a.shape
    return a.reshape(L // SEGMENTS, SEGMENTS, c).transpose(1, 0, 2).reshape(L, c)


def _block_diag(ms):
    n = len(ms)

    def body(*refs):
        for a in range(n):
            out = refs[n + a]
            out[...] = jnp.zeros_like(out)
            for g in range(S5_GROUPS):
                rows = slice(g * S5_GROUP_CH, (g + 1) * S5_GROUP_CH)
                out[rows, g * S5_STATE:(g + 1) * S5_STATE] = refs[a][rows, :].astype(bf16)

    return pl.pallas_call(body, name="s5_block_diag", out_shape=[S((S5_WIDTH, S5_LANES), bf16)] * n,
                          compiler_params=pltpu.CompilerParams(vmem_limit_bytes=VMEM_LIMIT))(
        *[m.reshape(S5_WIDTH, S5_STATE) for m in ms])


def _diag_blocks(ms):
    n = len(ms)
    per_chunk = SCAN_LANES // S5_STATE

    def body(*refs):
        for a in range(n):
            for g in range(S5_GROUPS):
                rows = slice(g * S5_GROUP_CH, (g + 1) * S5_GROUP_CH)
                at = g % per_chunk * S5_STATE
                refs[n + a][rows, :] = refs[a][rows, at:at + S5_STATE]

    out = pl.pallas_call(body, name="s5_diag_blocks", out_shape=[S((S5_WIDTH, S5_STATE), f32)] * n,
                         compiler_params=pltpu.CompilerParams(vmem_limit_bytes=VMEM_LIMIT))(*ms)
    return [o.reshape(S5_GROUPS, S5_GROUP_CH, S5_STATE) for o in out]


class _NoExchanges:
    def before(self, point):
        return ()

    def after(self, point):
        pass

    def alone(self, point):
        pass


def _ffn_block(x, p, tag, tm, sched, head=None, mixed=None, mixer=None):
    point = tag + "_up"
    h, dadg, dadu, a, *x_in = _ffn_up(x, p[tag + "_norm"], p[tag + "_w_gate"], p[tag + "_w_up"], tm, tag, mixed,
                                      ride=sched.before(point))
    sched.after(point)
    x, = x_in or [x]
    if head is None:
        out = _ffn_down(x, a, p[tag + "_w_down"], tm, tag, mixer and tuple(p[n] for n in mixer), ride=sched.before(tag + "_down"))
        sched.after(tag + "_down")
    else:
        out = _ffn_down_loss(x, a, p[tag + "_w_down"], *head, tm, tag)
    return out, (x, h, dadg, dadu, a)


def _ffn_block_bwd(dxo, x, p, tag, saved, tm, grads, sched, parts=1, dxh=None):
    _, h, dadg, dadu, a = saved

    def weight_grad(which, lhs, rhs):
        point = tag + "_dw_" + which
        grads[tag + "_w_" + which] = _mm_tn(lhs, rhs, bf16, point, ride=sched.before(point))
        sched.after(point)

    if dxh is not None:
        weight_grad("down", a, dxh)
    dgate, dup, own_dxh = _ffn_bwd_act(dxo, p[tag + "_w_down"], dadg, dadu, tm, tag, ride=sched.before(tag + "_bwd_act"))
    sched.after(tag + "_bwd_act")
    weight_grad("gate", dgate, h)
    weight_grad("up", dup, h)
    if dxh is None:
        weight_grad("down", a, own_dxh)
    tiles = x.shape[0] // tm
    dx, dgs = None, []
    for k in range(parts):
        point = tag + "_bwd_in" + ("_%d" % k) * (parts > 1)
        dx, dg = _ffn_bwd_in(dxo, x, p[tag + "_norm"], dgate, dup, p[tag + "_w_gate"], p[tag + "_w_up"], tm, point,
                             tiles=(k * tiles // parts, tiles // parts), into=dx, ride=sched.before(point))
        sched.after(point)
        dgs.append(dg)
    grads[tag + "_norm"] = functools.reduce(jnp.add, dgs)
    return dx


def _local_step(x, target, p, grads, sched):
    L = x.shape[0]
    tm = min(512, L // 2)
    ni = L // SEGMENTS
    bi = min(128, ni)

    def carried(point, fn, *args):
        out = fn(*args, ride=sched.before(point))
        sched.after(point)
        return out

    (x1, h2, u_s5, v), saved1 = _ffn_block(x, p, "ffn1", tm, sched, mixer=("mix_norm", "w_in"))

    s5_in = (p["s5_lam_re"], p["s5_lam_im"], p["s5_log_dt"].reshape(S5_GROUPS, 1), p["s5_b_re"], p["s5_b_im"])
    abar_re, abar_im, bbar_re, bbar_im = _s5_params(*s5_in)
    a_re, a_im = abar_re.reshape(1, S5_LANES), abar_im.reshape(1, S5_LANES)
    bb_re, bb_im, cc_re, cc_im = _block_diag([bbar_re, bbar_im, p["s5_c_re"], p["s5_c_im"]])
    u_seg = _to_segments(u_s5)
    s_re, s_im, y_lin = carried("s5_forward", _s5_forward, u_seg, a_re, a_im, bb_re, bb_im, cc_re, cc_im, bi)
    y_s5 = _from_segments(_s5_gate(y_lin, u_seg, p["s5_d"], p["s5_w_glu"], p["s5_b_glu"], tm))
    w_dw = jnp.pad(p["conv_w_dw"], ((0, CONV_HALO - CONV_K), (0, 0)))
    heads = jnp.arange(CONV_WIDTH) // CONV_HEAD
    avg = ((heads[:, None] == heads[None, :]).astype(f32) / CONV_HEAD).astype(bf16)
    y_conv, zc = carried("conv_fwd", _conv_fwd, v, w_dw, p["conv_b_dw"], p["conv_ln_g"], p["conv_ln_b"], avg, tm)

    (dx3, grads["final_norm"], loss_terms), saved2 = _ffn_block(
        x1, p, "ffn2", tm, sched, head=(target, p["final_norm"].reshape(1, D_MODEL)), mixed=(y_s5, y_conv, p["w_out"]))
    x2 = saved2[0]
    grads["loss_terms"] = loss_terms

    dx2 = _ffn_block_bwd(dx3, x2, p, "ffn2", saved2, tm, grads, sched)

    dy_s5, dy_conv, dx2b = carried("mix_out_bwd", _mix_out_bwd, dx2, p["w_out"], tm)
    grads["w_out"] = _dw_out(y_s5, y_conv, dx2b)
    dy_lin, du_skip, dd8, grads["s5_w_glu"], dbg8 = _s5_read_bwd(
        _to_segments(dy_s5), y_lin, u_seg, p["s5_d"], p["s5_w_glu"], p["s5_b_glu"], tm)
    grads["s5_d"] = dd8.sum(axis=0, keepdims=True)
    grads["s5_b_glu"] = dbg8.sum(axis=0, keepdims=True)
    du_seg, da_re8, da_im8, dbb_re, dbb_im, dcc_re, dcc_im = carried(
        "s5_backward", _s5_backward, dy_lin, u_seg, du_skip, s_re, s_im, a_re, -a_im, bb_re, bb_im, cc_re, cc_im, bi)
    d_abar = lambda a8: a8.sum(axis=0).reshape(S5_GROUPS, S5_STATE)
    grads["s5_c_re"], grads["s5_c_im"], d_bbr, d_bbi = _diag_blocks([dcc_re, dcc_im, dbb_re, dbb_im])
    d_lr, d_li, d_ld, d_br, d_bi = _s5_params_bwd(*s5_in, d_abar(da_re8), d_abar(da_im8), d_bbr, d_bbi)
    grads["s5_lam_re"], grads["s5_lam_im"], grads["s5_log_dt"] = d_lr, d_li, d_ld.reshape(1, S5_GROUPS)
    grads["s5_b_re"], grads["s5_b_im"] = d_br, d_bi
    dzc, dlg8, dlb8, dbd8 = _conv_bwd_norm(dy_conv, zc, p["conv_ln_g"], p["conv_ln_b"], avg, tm)
    grads["conv_ln_g"] = dlg8.sum(axis=0, keepdims=True)
    grads["conv_ln_b"] = dlb8.sum(axis=0, keepdims=True)
    grads["conv_b_dw"] = dbd8.sum(axis=0, keepdims=True)
    dv, dw8 = carried("conv_bwd_taps", _conv_bwd_taps, dzc, v, w_dw, tm)
    grads["conv_w_dw"] = dw8.sum(axis=1)[:CONV_K]
    dx1, grads["mix_norm"], dub, dx1h = carried("mix_in_bwd", _mix_in_bwd, dx2, x1, p["mix_norm"], _from_segments(du_seg), dv,
                                                p["w_in"], tm)
    grads["w_in"] = _mm_tn(dub, h2, bf16, "dw_in")

    dx0 = _ffn_block_bwd(dx1, x, p, "ffn1", saved1, tm, grads, sched, parts=min(2, L // tm), dxh=dx1h)
    sched.alone("tail")
    return loss_terms, dx0


MESH = pl.DeviceIdType.MESH
ANY = pl.BlockSpec(memory_space=pl.ANY)


def _place():
    return lax.axis_index("x"), lax.axis_index("y"), lax.axis_index("c")


class _Exchange:
    def __init__(self, ins, out_shape, sems, start, finish):
        self.ins, self.out_shape, self.sems, self.start, self.finish = list(ins), list(out_shape), list(sems), start, finish
        self.out = None


def _pallas(body, *, ride=(), **kw):
    if not ride:
        return pl.pallas_call(body, **kw)

    def run(*args):
        out_shape = kw.get("out_shape", [])
        single = not isinstance(out_shape, (list, tuple))
        shapes = [out_shape] if single else list(out_shape)
        out_specs = [kw["out_specs"]] if single else list(kw.get("out_specs", []))
        grid = tuple(kw.get("grid", ()))
        scratch = list(kw.get("scratch_shapes", ()))
        n_in, n_out, n_scr = len(args), len(shapes), len(scratch)
        r_in = [len(e.ins) for e in ride]
        r_out = [len(e.out_shape) for e in ride]
        r_sem = [len(e.sems) for e in ride]

        def wrapped(*refs):
            own_in, refs = refs[:n_in], refs[n_in:]
            ex_in, refs = refs[:sum(r_in)], refs[sum(r_in):]
            own_out, refs = refs[:n_out], refs[n_out:]
            ex_out, refs = refs[:sum(r_out)], refs[sum(r_out):]
            own_scr, ex_sem = refs[:n_scr], refs[n_scr:]
            parts = []
            for e, ni, no, ns in zip(ride, r_in, r_out, r_sem):
                parts.append((e, ex_in[:ni], ex_out[:no], ex_sem[:ns]))
                ex_in, ex_out, ex_sem = ex_in[ni:], ex_out[no:], ex_sem[ns:]

            def at(step):
                def go():
                    for e, i, o, s in parts:
                        getattr(e, step)(i, o, s)
                if grid:
                    ids = [pl.program_id(d) for d in range(len(grid))]
                    when = [i == (0 if step == "start" else g - 1) for i, g in zip(ids, grid)]
                    pl.when(functools.reduce(lambda a, b: a & b, when))(go)
                else:
                    go()

            at("start")
            if body is not None:
                body(*own_in, *own_out, *own_scr)
            at("finish")

        outs = pl.pallas_call(
            wrapped, name=kw["name"], grid=grid,
            in_specs=list(kw.get("in_specs", [])) + [ANY] * sum(r_in),
            out_specs=out_specs + [ANY] * sum(r_out),
            out_shape=shapes + [s for e in ride for s in e.out_shape],
            scratch_shapes=scratch + [s for e in ride for s in e.sems],
            input_output_aliases=kw.get("input_output_aliases", {}),
            compiler_params=_cp(*["arbitrary"] * len(grid)),
        )(*args, *[a for e in ride for a in e.ins])
        own, rest = outs[:n_out], outs[n_out:]
        for e, no in zip(ride, r_out):
            e.out, rest = list(rest[:no]), rest[no:]
        return own[0] if single else own

    return run


def _exchange(ride, name):
    _pallas(None, ride=ride, name=name)()


def _gather(arrs):
    n = len(arrs)

    def copies(ins, outs, sems):
        send_sems, recv_sems, local_sems = sems
        x, y, c = _place()
        me, sibling = (x, y, c), (x, y, 1 - c)
        chips = [(1 - x, y), (x, 1 - y), (1 - x, 1 - y)]

        def place(a, block):
            return outs[a].at[block]

        def copy(a, k, block, to, src=None):
            px, py, pc = block
            dst = place(a, 4 * px + 2 * py + pc)
            return pltpu.make_async_remote_copy(
                src_ref=dst if src is None else src, dst_ref=dst, send_sem=send_sems.at[7 * a + k],
                recv_sem=recv_sems.at[7 * a + k], device_id=to, device_id_type=MESH)

        def own():
            local = [pltpu.make_async_copy(ins[a], place(a, 4 * x + 2 * y + c), local_sems.at[a]) for a in range(n)]
            remote = []
            for a in range(n):
                remote.append(copy(a, 0, me, sibling, src=ins[a]))
                remote += [copy(a, 1 + j, me, (*chip, c), src=ins[a]) for j, chip in enumerate(chips)]
            return local, remote

        return c, me, sibling, chips, copy, own

    def start(ins, outs, sems):
        local, remote = copies(ins, outs, sems)[-1]()
        for cp in local + remote:
            cp.start()

    def finish(ins, outs, sems):
        c, me, sibling, chips, copy, own = copies(ins, outs, sems)
        passed = []
        for j, chip in enumerate(chips):
            for a in range(n):
                copy(a, 1 + j, (*chip, c), me).wait_recv()
                passed.append(copy(a, 4 + j, (*chip, c), sibling))
                passed[-1].start()
        for a in range(n):
            copy(a, 0, sibling, me).wait_recv()
            for j, chip in enumerate(chips):
                copy(a, 4 + j, (*chip, 1 - c), me).wait_recv()
        local, remote = own()
        for cp in remote + passed:
            cp.wait_send()
        for cp in local:
            cp.wait()

    dma = pltpu.SemaphoreType.DMA
    shapes = [S((N_DEV, *a.shape), a.dtype) for a in arrs]
    return _Exchange(arrs, shapes, [dma((7 * n,)), dma((7 * n,)), dma((n,))], start, finish)


def _swap_with_sibling(gs):
    n = len(gs)

    def copies(ins, outs, sems):
        x, y, c = _place()
        return [pltpu.make_async_remote_copy(
            src_ref=ins[a].at[:, 1 - c], dst_ref=outs[a], send_sem=sems[0].at[a], recv_sem=sems[1].at[a],
            device_id=(x, y, 1 - c), device_id_type=MESH) for a in range(n)]

    def start(ins, outs, sems):
        for cp in copies(ins, outs, sems):
            cp.start()

    def finish(ins, outs, sems):
        for cp in copies(ins, outs, sems):
            cp.wait()

    dma = pltpu.SemaphoreType.DMA
    return _Exchange(gs, [S((N_CHIP, *g.shape[2:]), g.dtype) for g in gs], [dma((n,)), dma((n,))], start, finish)


def _swap_with_chips(ps):
    n = len(ps)

    def copies(ins, outs, sems):
        x, y, c = _place()
        q = 2 * x + y
        peers = [(x, 1 - y), (1 - x, y), (1 - x, 1 - y)]

        def copy(a, j, slot_from, slot_to):
            px, py = peers[j]
            return pltpu.make_async_remote_copy(
                src_ref=ins[a].at[slot_from], dst_ref=outs[a].at[slot_to], send_sem=sems[0].at[3 * a + j],
                recv_sem=sems[1].at[3 * a + j], device_id=(px, py, c), device_id_type=MESH)

        sends = lambda: [copy(a, j, 2 * peers[j][0] + peers[j][1], q) for a in range(n) for j in range(3)]
        lands = lambda: [copy(a, j, q, 2 * peers[j][0] + peers[j][1]) for a in range(n) for j in range(3)]
        return sends, lands

    def start(ins, outs, sems):
        for cp in copies(ins, outs, sems)[0]():
            cp.start()

    def finish(ins, outs, sems):
        sends, lands = copies(ins, outs, sems)
        for cp in lands():
            cp.wait_recv()
        for cp in sends():
            cp.wait_send()

    dma = pltpu.SemaphoreType.DMA
    return _Exchange(ps, [S(p.shape, p.dtype) for p in ps], [dma((3 * n,)), dma((3 * n,))], start, finish)


def _row_tile(rows, cols, itemsize):
    t = rows
    while t * cols * itemsize > (1 << 20) and t % 32 == 0:
        t //= 2
    return t


def _add_sibling(g4, st, core, name):
    _, R, C = st.shape
    tr = _row_tile(R, C, 1)

    def body(c_ref, g_ref, s_ref, o_ref):
        o_ref[...] = (g_ref[...].astype(f32) + s_ref[...].astype(f32)).astype(bf16)

    mine = pl.BlockSpec((None, None, tr, C), lambda q, i, c: (q, c[0], i, 0))
    return pl.pallas_call(
        body, name=name,
        grid_spec=pltpu.PrefetchScalarGridSpec(
            num_scalar_prefetch=1, grid=(N_CHIP, R // tr),
            in_specs=[mine,
                      pl.BlockSpec((None, tr, C), lambda q, i, c: (q, i, 0))],
            out_specs=pl.BlockSpec((None, tr, C), lambda q, i, c: (q, i, 0))),
        out_shape=S((N_CHIP, R, C), bf16),
        compiler_params=_cp("parallel", "parallel"),
    )(core, g4, st)


SMEM = pl.BlockSpec(memory_space=pltpu.SMEM)
VMEM = pl.BlockSpec(memory_space=pltpu.VMEM)


def _add_sibling_small(items, core, name):
    n = len(items)

    def body(c_ref, *refs):
        c = c_ref[0]
        for k in range(n):
            g_ref, s_ref, o_ref = refs[2 * k], refs[2 * k + 1], refs[2 * n + k]
            for q in range(N_CHIP):
                o_ref[q] = (g_ref[q, c].astype(f32) + s_ref[q].astype(f32)).astype(bf16)

    return pl.pallas_call(body, name=name, in_specs=[SMEM] + [VMEM] * (2 * n), out_specs=[VMEM] * n,
                          out_shape=[S(st.shape, bf16) for _, st in items],
                          compiler_params=pltpu.CompilerParams(vmem_limit_bytes=VMEM_LIMIT))(
        core, *[a for item in items for a in item])


def _adam_small(items, slots, name):
    n = len(items)

    def body(s_ref, *refs):
        ins, outs = refs[:5 * n], refs[5 * n:]
        for k in range(n):
            w_ref, m_ref, v_ref, p_ref, got_ref = ins[5 * k:5 * k + 5]
            g = p_ref[s_ref[0]].astype(f32)
            for j in range(1, N_CHIP):
                g = g + got_ref[s_ref[j]].astype(f32)
            outs[4 * k][...] = g
            outs[4 * k + 1][...], outs[4 * k + 2][...], outs[4 * k + 3][...] = _adamw(w_ref[...], g, m_ref[...], v_ref[...])

    out = pl.pallas_call(body, name=name, in_specs=[SMEM] + [VMEM] * (5 * n), out_specs=[VMEM] * (4 * n),
                         out_shape=[S(item[0].shape, f32) for item in items for _ in range(4)],
                         compiler_params=pltpu.CompilerParams(vmem_limit_bytes=VMEM_LIMIT))(
        slots, *[a for item in items for a in item])
    return [out[4 * k:4 * k + 4] for k in range(n)]


def _adamw(w, g, m, v):
    m = B1 * m + (1.0 - B1) * g
    v = B2 * v + (1.0 - B2) * (g * g)
    m_hat = m / (1.0 - B1 ** STEP)
    v_hat = v / (1.0 - B2 ** STEP)
    return -LR * (m_hat / (jnp.sqrt(v_hat) + ADAM_EPS) + WD * w), m, v


def _adam_sharded(items, slots, name):
    n = len(items)
    R, C = items[0][0].shape
    tr = _row_tile(R, C, 4 * n)

    def body(s_ref, *refs):
        ins, outs = refs[:7 * n], refs[7 * n:]
        for k in range(n):
            w_ref, m_ref, v_ref, p_ref, a_ref, b_ref, c_ref = ins[7 * k:7 * k + 7]
            g = p_ref[...].astype(f32) + a_ref[...].astype(f32) + b_ref[...].astype(f32) + c_ref[...].astype(f32)
            outs[4 * k][...] = g
            outs[4 * k + 1][...], outs[4 * k + 2][...], outs[4 * k + 3][...] = _adamw(w_ref[...], g, m_ref[...], v_ref[...])

    shard = pl.BlockSpec((tr, C), lambda i, s: (i, 0))
    slot = lambda k: pl.BlockSpec((None, tr, C), lambda i, s: (s[k], i, 0))
    out = pl.pallas_call(
        body, name=name,
        grid_spec=pltpu.PrefetchScalarGridSpec(
            num_scalar_prefetch=1, grid=(R // tr,),
            in_specs=[shard, shard, shard, slot(0), slot(1), slot(2), slot(3)] * n,
            out_specs=[shard] * (4 * n)),
        out_shape=[S((R, C), f32)] * (4 * n),
        compiler_params=_cp("parallel"),
    )(slots, *[a for w, m, v, part, got in items for a in (w, m, v, part, got, got, got)])
    return [out[4 * k:4 * k + 4] for k in range(n)]


def _adam_replicated(items, stacks, loss_at, name):
    n, ns = len(items), len(stacks)

    def body(*refs):
        stack_refs, ins, outs = refs[:ns], refs[ns:ns + 3 * n], refs[ns + 3 * n:]
        totals = []
        for ref in stack_refs:
            g = ref[0]
            for d in range(1, N_DEV):
                g = g + ref[d]
            totals.append(g)
        for i, (_, _, _, (s, k)) in enumerate(items):
            w_ref, m_ref, v_ref = ins[3 * i:3 * i + 3]
            g = totals[s] if k is None else totals[s][k]
            outs[4 * i][...] = g
            outs[4 * i + 1][...], outs[4 * i + 2][...], outs[4 * i + 3][...] = _adamw(w_ref[...], g, m_ref[...], v_ref[...])
        if loss_at is not None:
            outs[-1][...] = jnp.sum(totals[loss_at[0]], keepdims=True)

    flat = list(stacks) + [a for item in items for a in item[:3]]
    shapes = [S(item[0].shape, f32) for item in items for _ in range(4)] + ([S((1, 1), f32)] if loss_at is not None else [])
    out = pl.pallas_call(body, name=name, out_shape=shapes,
                         compiler_params=pltpu.CompilerParams(vmem_limit_bytes=VMEM_LIMIT))(*flat)
    return [out[4 * i:4 * i + 4] for i in range(n)], (out[-1] if loss_at is not None else None)


WEIGHTS = ["ffn1_norm", "ffn1_w_gate", "ffn1_w_up", "ffn1_w_down", "mix_norm", "w_in", "s5_lam_re", "s5_lam_im", "s5_log_dt",
           "s5_b_re", "s5_b_im", "s5_c_re", "s5_c_im", "s5_d", "s5_w_glu", "s5_b_glu", "conv_w_dw", "conv_b_dw", "conv_ln_g",
           "conv_ln_b", "w_out", "ffn2_norm", "ffn2_w_gate", "ffn2_w_up", "ffn2_w_down", "final_norm"]
SHARDED = ["ffn1_w_gate", "ffn1_w_up", "ffn1_w_down", "w_in", "s5_w_glu", "conv_w_dw", "w_out", "ffn2_w_gate", "ffn2_w_up",
           "ffn2_w_down"]
REPLICATED = [n for n in WEIGHTS if n not in SHARDED]
TRANSPOSED = ["ffn1_w_gate", "ffn1_w_up", "ffn2_w_gate", "ffn2_w_up", "w_in"]


def _shard_to_wire(n, w):
    if n == "conv_w_dw":
        return jnp.pad(w, ((0, CONV_HALO - CONV_K), (0, 0)))
    return w.astype(bf16)


def _to_wire(shards, ride):
    names = list(shards)
    shapes = [jax.eval_shape(functools.partial(_shard_to_wire, n), shards[n]) for n in names]

    def body(*refs):
        for src, dst in zip(refs[:len(names)], refs[len(names):]):
            (r, c), (rp, cp) = src.shape, dst.shape
            dst[:r, :c] = src[...].astype(dst.dtype)
            if cp > c:
                dst[:, c:] = jnp.zeros((rp, cp - c), dst.dtype)
            if rp > r:
                dst[r:, :] = jnp.zeros((rp - r, cp), dst.dtype)

    out = _pallas(body, ride=ride, name="to_wire", out_shape=shapes, in_specs=[pl.BlockSpec(memory_space=pltpu.VMEM)] * len(names),
                  out_specs=[pl.BlockSpec(memory_space=pltpu.VMEM)] * len(names))(*[shards[n] for n in names])
    return dict(zip(names, out))


def _gathered_to_full(n, g):
    if n == "conv_w_dw":
        return g.transpose(1, 0, 2).reshape(CONV_HALO, CONV_WIDTH)[:CONV_K]
    return g.reshape(N_DEV * g.shape[1], g.shape[2])


def _grad_to_blocks(n, g):
    if n == "conv_w_dw":
        g = jnp.pad(g, ((0, CONV_HALO - CONV_K), (0, 0)))
        g = g.reshape(g.shape[0], N_DEV, g.shape[1] // N_DEV).transpose(1, 0, 2)
    else:
        g = g.reshape(N_DEV, g.shape[0] // N_DEV, g.shape[1])
    return g.astype(bf16).reshape(N_CHIP, 2, *g.shape[1:])


REPLICATED_LATE = ["ffn1_norm"]
REPLICATED_HEAD = ["ffn2_norm", "final_norm"]
REPLICATED_MIX = ["mix_norm", "conv_b_dw", "conv_ln_g", "conv_ln_b"]
REPLICATED_S5 = [n for n in REPLICATED if n not in REPLICATED_LATE + REPLICATED_HEAD + REPLICATED_MIX]
REPLICATED_EARLY = REPLICATED_HEAD + REPLICATED_S5 + REPLICATED_MIX

PLAN = {
    "start": [("gather", ["ffn1_w_gate", "ffn1_w_up"])],
    "ffn1_up": [("gather", ["ffn1_w_down", "w_in", "w_out", "s5_w_glu", "conv_w_dw"])],
    "ffn1_down": [("gather", ["ffn2_w_gate"])],
    "s5_forward": [("gather", ["ffn2_w_up"])],
    "conv_fwd": [("gather", ["ffn2_w_down"])],
    "ffn2_dw_up": [("sibling", ["ffn2_w_gate"])],
    "ffn2_dw_down": [("sibling", ["ffn2_w_up"])],
    "mix_out_bwd": [("sibling", ["ffn2_w_down"]), ("replicated", REPLICATED_HEAD)],
    "s5_backward": [("chips", ["ffn2_w_gate", "ffn2_w_up"])],
    "conv_bwd_taps": [("chips", ["ffn2_w_down"])],
    "mix_in_bwd": [("replicated", REPLICATED_S5)],
    "ffn1_dw_down": [("sibling", ["w_in", "s5_w_glu", "conv_w_dw", "w_out"]), ("replicated", REPLICATED_MIX)],
    "ffn1_bwd_act": [("chips", ["w_in", "s5_w_glu", "conv_w_dw", "w_out"]), ("sibling", ["ffn1_w_down"])],
    "ffn1_dw_gate": [("chips", ["ffn1_w_down"])],
    "ffn1_dw_up": [("sibling", ["ffn1_w_gate"])],
    "ffn1_bwd_in_0": [("chips", ["ffn1_w_gate"]), ("sibling", ["ffn1_w_up"])],
    "ffn1_bwd_in_1": [("chips", ["ffn1_w_up"])],
    "tail": [("replicated", REPLICATED_LATE)],
}


class _Schedule:
    def __init__(self, wire, p, grads, core):
        self.wire, self.p, self.grads, self.core = wire, p, grads, core
        self.partial, self.reduced, self.pending = {}, {}, []
        self.stacks, self.shapes, self.everyone = [], [], {}

    def before(self, point):
        assert not self.pending
        for kind, names in PLAN.get(point, ()):
            if kind == "gather":
                given = [self.wire[n] for n in names]
                ex = _gather(given)
            elif kind == "sibling":
                given = [_grad_to_blocks(n, self.grads[n]) for n in names]
                ex = _swap_with_sibling(given)
            elif kind == "chips":
                given = [self.partial.pop(n) for n in names]
                ex = _swap_with_chips(given)
            else:
                names = names + ["loss_terms"] * (names is REPLICATED_HEAD)
                by_shape = {}
                for n in names:
                    by_shape.setdefault(self.p[n].shape if n in self.p else None, []).append(n)
                given = []
                for shape, members in by_shape.items():
                    for k, n in enumerate(members):
                        self.everyone[n] = (len(self.stacks) + len(given), k if len(members) > 1 else None)
                    parts = [self.grads[n].reshape(shape) if shape else self.grads[n] for n in members]
                    whole = jnp.stack(parts) if len(members) > 1 else parts[0]
                    self.shapes.append(whole.shape)
                    if whole.ndim >= 3 and whole.shape[-1] < 128:
                        whole = whole.reshape(*whole.shape[:-2], -1)
                    given.append(whole)
                ex = _gather(given)
            self.pending.append((kind, names, given, ex))
        return [ex for _, _, _, ex in self.pending]

    def after(self, point):
        for kind, names, given, ex in self.pending:
            if kind == "gather":
                for n, g in zip(names, ex.out):
                    self.p[n] = _gathered_to_full(n, g)
            elif kind == "sibling":
                if len(names) > 1:
                    sums = _add_sibling_small(list(zip(given, ex.out)), self.core, "reduce_add_" + names[0])
                else:
                    sums = [_add_sibling(given[0], ex.out[0], self.core, "reduce_add_" + names[0])]
                self.partial.update(zip(names, sums))
            elif kind == "chips":
                for n, part, got in zip(names, given, ex.out):
                    self.reduced[n] = (part, got)
            else:
                self.stacks += [g.reshape(N_DEV, *self.shapes[len(self.stacks) + i]) for i, g in enumerate(ex.out)]
        self.pending = []

    def alone(self, point):
        _exchange(self.before(point), point)
        self.after(point)


def kernel(x, ffn1_norm, ffn1_w_gate, ffn1_w_up, ffn1_w_down, mix_norm, w_in, s5_lam_re, s5_lam_im, s5_log_dt, s5_b_re, s5_b_im, s5_c_re, s5_c_im, s5_d, s5_w_glu, s5_b_glu, conv_w_dw, conv_b_dw, conv_ln_g, conv_ln_b, w_out, ffn2_norm, ffn2_w_gate, ffn2_w_up, ffn2_w_down, final_norm, loss_target, m_ffn1_norm, m_ffn1_w_gate, m_ffn1_w_up, m_ffn1_w_down, m_mix_norm, m_w_in, m_s5_lam_re, m_s5_lam_im, m_s5_log_dt, m_s5_b_re, m_s5_b_im, m_s5_c_re, m_s5_c_im, m_s5_d, m_s5_w_glu, m_s5_b_glu, m_conv_w_dw, m_conv_b_dw, m_conv_ln_g, m_conv_ln_b, m_w_out, m_ffn2_norm, m_ffn2_w_gate, m_ffn2_w_up, m_ffn2_w_down, m_final_norm, v_ffn1_norm, v_ffn1_w_gate, v_ffn1_w_up, v_ffn1_w_down, v_mix_norm, v_w_in, v_s5_lam_re, v_s5_lam_im, v_s5_log_dt, v_s5_b_re, v_s5_b_im, v_s5_c_re, v_s5_c_im, v_s5_d, v_s5_w_glu, v_s5_b_glu, v_conv_w_dw, v_conv_b_dw, v_conv_ln_g, v_conv_ln_b, v_w_out, v_ffn2_norm, v_ffn2_w_gate, v_ffn2_w_up, v_ffn2_w_down, v_final_norm):
    args = locals()
    w = {n: args[n] for n in WEIGHTS}
    m = {n: args["m_" + n] for n in WEIGHTS}
    v = {n: args["v_" + n] for n in WEIGHTS}
    xq, yq, cq = _place()
    q = 2 * xq + yq
    slots = jnp.stack([q, q ^ 1, q ^ 2, q ^ 3]).astype(jnp.int32)

    def shard2d(n, a):
        a = a.reshape(a.shape[-2:])
        return a.T if n in TRANSPOSED else a

    def view(n, a):
        if n.startswith("s5_b_") and a.ndim == 4:
            return a[0].transpose(0, 2, 1)
        return a[0] if a.ndim >= 3 else a.reshape(1, -1)

    def unview(n, a):
        return (a.transpose(0, 2, 1) if n.startswith("s5_b_") and a.ndim == 3 else a).reshape(w[n].shape)

    p = {n: view(n, w[n]) for n in REPLICATED}
    grads = {}
    first = PLAN["start"][0][1]
    wire = {n: _shard_to_wire(n, shard2d(n, w[n])) for n in first}
    sched = _Schedule(wire, p, grads, jnp.reshape(cq, (1,)).astype(jnp.int32))
    wire.update(_to_wire({n: shard2d(n, w[n]) for n in SHARDED if n not in first}, sched.before("start")))
    sched.after("start")
    _, dx = _local_step(x[0], loss_target[0], p, grads, sched)

    out = {}
    groups = [[n for n in SHARDED if n.startswith(tag)] for tag in ("ffn1", "ffn2")]
    for names in groups + [[n for n in SHARDED if not n.startswith("ffn")]]:
        def fit(n, a):
            a = shard2d(n, a)
            return jnp.pad(a, ((0, sched.reduced[n][1].shape[1] - a.shape[0]), (0, 0)))

        items = [(fit(n, w[n]), fit(n, m[n]), fit(n, v[n]), *sched.reduced[n]) for n in names]
        update = _adam_sharded if names[0].startswith("ffn") else _adam_small
        for n, res in zip(names, update(items, slots, "adam_" + names[0])):
            back = lambda r: r[:shard2d(n, w[n]).shape[0]]
            out[n] = [(back(r).T if n in TRANSPOSED else back(r)).reshape(w[n].shape) for r in res]

    for names in (REPLICATED_EARLY, REPLICATED_LATE):
        loss_at = sched.everyone["loss_terms"] if names is REPLICATED_EARLY else None
        used = sorted({sched.everyone[n][0] for n in names} | ({loss_at[0]} if loss_at else set()))
        at = lambda where: (used.index(where[0]), where[1])
        items = [(view(n, w[n]), view(n, m[n]), view(n, v[n]), at(sched.everyone[n])) for n in names]
        res, total = _adam_replicated(items, [sched.stacks[s] for s in used], loss_at and at(loss_at), "adam_" + names[0])
        for n, r in zip(names, res):
            out[n] = [unview(n, a) for a in r]
        if total is not None:
            loss = total.reshape(())

    return (loss, dx.reshape(x.shape), *[out[n][0] for n in WEIGHTS], *[out[n][1] for n in WEIGHTS],
            *[out[n][2] for n in WEIGHTS], *[out[n][3] for n in WEIGHTS])
```

```python
import functools

import jax
import jax.numpy as jnp
from jax import lax
from jax.experimental import pallas as pl
from jax.experimental.pallas import tpu as pltpu

f32 = jnp.float32
bf16 = jnp.bfloat16
S = jax.ShapeDtypeStruct

N_DEV = 8
N_CHIP = 4
D_MODEL = 1024
D_FF = 2816
FF_CHUNKS = [(0, 768), (768, 1536), (1536, 2304), (2304, D_FF)]
S5_WIDTH = 512
S5_GROUPS = 32
S5_GROUP_CH = 16
S5_STATE = 64
S5_LANES = S5_GROUPS * S5_STATE
CONV_WIDTH = 512
CONV_K = 31
CONV_HALO = 32
CONV_HEAD = 64
CONV_ROWS = 32
IN_COLS = S5_WIDTH + 2 * CONV_WIDTH
SEGMENTS = 8
SCAN_LANES = 512
EPS = 1e-6
LR, B1, B2, ADAM_EPS, WD, STEP = 0.001, 0.9, 0.999, 1e-08, 0.01, 10
VMEM_LIMIT = 56 * 1024 * 1024

NN = (((1,), (0,)), ((), ()))
NT = (((1,), (1,)), ((), ()))
TN = (((0,), (0,)), ((), ()))


def _dot(a, b, dims=NN):
    return lax.dot_general(a, b, dims, preferred_element_type=f32)


def _cp(*sem):
    return pltpu.CompilerParams(dimension_semantics=sem, vmem_limit_bytes=VMEM_LIMIT)


def _rms(x, g):
    return x * lax.rsqrt(jnp.mean(x * x, axis=-1, keepdims=True) + EPS) * g


def _rms_bwd(x, g, dh):
    _, vjp = jax.vjp(_rms, x, g)
    return vjp(dh)


def _sigmoid(x):
    return 1.0 / (1.0 + jnp.exp(-x))


def _gelu(x):
    return 0.5 * x * (1.0 + jnp.tanh(0.7978845608028654 * (x + 0.044715 * x * x * x)))


def _rows8(x):
    t, c = x.shape
    return x.reshape(t // 8, 8, c).sum(axis=0)


def _full(shape):
    return pl.BlockSpec(shape, lambda *_: (0,) * len(shape))


def _resident(shape):
    return pl.BlockSpec(shape, lambda *_: (0,) * len(shape), pipeline_mode=pl.Buffered(1))


def _ffn_up(x, g, wg, wu, tm, tag, mixed=None, ride=()):
    L = x.shape[0]

    def body(x_ref, g_ref, wg_ref, wu_ref, *rest):
        h_ref, dadg_ref, dadu_ref, a_ref = rest[-5:-1] if mixed else rest[-4:]
        x = x_ref[...]
        if mixed:
            ys_ref, yc_ref, wo_ref = rest[:3]
            x = x + _dot(ys_ref[...], wo_ref[:S5_WIDTH, :]) + _dot(yc_ref[...], wo_ref[S5_WIDTH:, :])
            rest[-1][...] = x
        h = _rms(x, g_ref[...]).astype(bf16)
        h_ref[...] = h
        for lo, hi in FF_CHUNKS:
            cols = slice(lo, hi)
            gate =_dot(h, wg_ref[cols, :], NT)
            up = _dot(h, wu_ref[cols, :], NT)
            sig = _sigmoid(gate)
            silu = gate * sig
            dadg_ref[:, cols] = (up * (sig + silu * (1.0 - sig))).astype(bf16)
            dadu_ref[:, cols] = silu.astype(bf16)
            a_ref[:, cols] = (silu * up).astype(bf16)

    row = pl.BlockSpec((tm, D_MODEL), lambda i: (i, 0))
    wide = pl.BlockSpec((tm, D_FF), lambda i: (i, 0))
    half = pl.BlockSpec((tm, S5_WIDTH), lambda i: (i, 0))
    return _pallas(
        body, ride=ride, name=tag + "_up", grid=(L // tm,),
        in_specs=[row, _full((1, D_MODEL)), _resident((D_FF, D_MODEL)), _resident((D_FF, D_MODEL))]
        + ([half, half, _resident((D_MODEL, D_MODEL))] if mixed else []),
        out_specs=[row, wide, wide, wide] + [row] * bool(mixed),
        out_shape=[S((L, D_MODEL), bf16)] + [S((L, D_FF), bf16)] * 3 + [S((L, D_MODEL), f32)] * bool(mixed),
        compiler_params=_cp("parallel"),
    )(x, g, wg, wu, *(mixed or ()))


def _ffn_down(x, a, wd, tm, tag, mixer=None, ride=()):
    L = x.shape[0]

    def body(x_ref, a_ref, wd_ref, *rest):
        xo = x_ref[...] + 0.5 * _dot(a_ref[...], wd_ref[...])
        if not mixer:
            rest[0][...] = xo
            return
        g_ref, w_ref, o_ref, h_ref, us_ref, v_ref = rest
        o_ref[...] = xo
        h = _rms(xo, g_ref[...]).astype(bf16)
        h_ref[...] = h
        u = _dot(h, w_ref[...], NT)
        us_ref[...] = u[:, :S5_WIDTH]
        v_ref[...] = u[:, S5_WIDTH:]

    row = lambda c: pl.BlockSpec((tm, c), lambda i: (i, 0))
    extra_in = [_full((1, D_MODEL)), _resident((IN_COLS, D_MODEL))] if mixer else []
    extra_out = [(D_MODEL, bf16), (S5_WIDTH, f32), (2 * CONV_WIDTH, f32)] if mixer else []
    out = _pallas(
        body, ride=ride, name=tag + "_down", grid=(L // tm,),
        in_specs=[row(D_MODEL), row(D_FF), _resident((D_FF, D_MODEL))] + extra_in,
        out_specs=[row(D_MODEL)] + [row(c) for c, _ in extra_out],
        out_shape=[S((L, D_MODEL), f32)] + [S((L, c), t) for c, t in extra_out],
        compiler_params=_cp("parallel"),
    )(x, a, wd, *(mixer or ()))
    return out if mixer else out[0]


def _ffn_down_loss(x, a, wd, target, g, tm, tag):
    L = x.shape[0]

    def body(x_ref, a_ref, wd_ref, t_ref, g_ref, dx_ref, dg_ref, l_ref):
        @pl.when(pl.program_id(0) == 0)
        def _():
            dg_ref[...] = jnp.zeros_like(dg_ref)
            l_ref[...] = jnp.zeros_like(l_ref)

        xo = x_ref[...] + 0.5 * _dot(a_ref[...], wd_ref[...])
        g = g_ref[...]
        e = _rms(xo, g) - t_ref[...]
        l_ref[...] += _rows8(e * e) * (0.5 / D_MODEL)
        dx, dg = _rms_bwd(xo, g, e * (1.0 / D_MODEL))
        dx_ref[...] = dx
        dg_ref[...] += dg

    row = pl.BlockSpec((tm, D_MODEL), lambda i: (i, 0))
    return pl.pallas_call(
        body, name=tag + "_down_loss", grid=(L // tm,),
        in_specs=[row, pl.BlockSpec((tm, D_FF), lambda i: (i, 0)), _resident((D_FF, D_MODEL)), row, _full((1, D_MODEL))],
        out_specs=[row, _full((1, D_MODEL)), _full((8, D_MODEL))],
        out_shape=[S((L, D_MODEL), f32), S((1, D_MODEL), f32), S((8, D_MODEL), f32)],
        compiler_params=_cp("arbitrary"),
    )(x, a, wd, target, g)


def _ffn_bwd_act(dxo, wd, dadg, dadu, tm, tag, ride=()):
    L = dxo.shape[0]

    DEPTH = 3
    n_tiles = L // tm

    def body(dx_ref, wd_ref, dadg_hbm, dadu_hbm, dgate_ref, dup_ref, dxh_ref, g_ring, u_ring, sems):
        i = pl.program_id(0)

        def fetch(tile, slot):
            rows = pl.ds(pl.multiple_of(tile * tm, tm), tm)
            return (pltpu.make_async_copy(dadg_hbm.at[rows, :], g_ring.at[slot], sems.at[0, slot]),
                    pltpu.make_async_copy(dadu_hbm.at[rows, :], u_ring.at[slot], sems.at[1, slot]))

        @pl.when(i == 0)
        def _():
            for t in range(min(DEPTH - 1, n_tiles)):
                for c in fetch(t, t):
                    c.start()

        ahead = i + (DEPTH - 1)

        @pl.when(ahead < n_tiles)
        def _():
            for c in fetch(ahead, ahead % DEPTH):
                c.start()

        slot = i % DEPTH
        for c in fetch(i, slot):
            c.wait()
        dxh = (0.5 * dx_ref[...]).astype(bf16)
        dxh_ref[...] = dxh
        for lo, hi in FF_CHUNKS:
            cols = slice(lo, hi)
            da =_dot(dxh, wd_ref[cols, :], NT)
            dgate_ref[:, cols] = (da * g_ring[slot, :, cols].astype(f32)).astype(bf16)
            dup_ref[:, cols] = (da * u_ring[slot, :, cols].astype(f32)).astype(bf16)

    row = pl.BlockSpec((tm, D_MODEL), lambda i: (i, 0))
    wide = pl.BlockSpec((tm, D_FF), lambda i: (i, 0))
    return _pallas(
        body, ride=ride, name=tag + "_bwd_act", grid=(n_tiles,),
        in_specs=[row, _resident((D_FF, D_MODEL)), ANY, ANY],
        out_specs=[wide, wide, row],
        out_shape=[S((L, D_FF), bf16), S((L, D_FF), bf16), S((L, D_MODEL), bf16)],
        scratch_shapes=[pltpu.VMEM((DEPTH, tm, D_FF), bf16), pltpu.VMEM((DEPTH, tm, D_FF), bf16),
                        pltpu.SemaphoreType.DMA((2, DEPTH))],
        compiler_params=_cp("arbitrary"),
    )(dxo, wd, dadg, dadu)


def _ffn_bwd_in(dxo, x, g, dgate, dup, wg, wu, tm, name, tiles=None, into=None, ride=()):
    L = x.shape[0]
    first, count = tiles or (0, L // tm)

    def body(dxo_ref, x_ref, g_ref, dgate_ref, dup_ref, wg_ref, wu_ref, *rest):
        dx_ref, dg_ref = rest[-2:]

        @pl.when(pl.program_id(0) == 0)
        def _():
            dg_ref[...] = jnp.zeros_like(dg_ref)

        dh = _dot(dgate_ref[...], wg_ref[...]) + _dot(dup_ref[...], wu_ref[...])
        dx, dg = _rms_bwd(x_ref[...], g_ref[...], dh)
        dx_ref[...] = dxo_ref[...] + dx
        dg_ref[...] += dg

    row = pl.BlockSpec((tm, D_MODEL), lambda i: (first + i, 0))
    wide = pl.BlockSpec((tm, D_FF), lambda i: (first + i, 0))
    return _pallas(
        body, ride=ride, name=name, grid=(count,),
        in_specs=[row, row, _full((1, D_MODEL)), wide, wide, _resident((D_FF, D_MODEL)), _resident((D_FF, D_MODEL))]
        + [ANY] * (into is not None),
        out_specs=[row, _full((1, D_MODEL))],
        out_shape=[S((L, D_MODEL), f32), S((1, D_MODEL), f32)],
        input_output_aliases={7: 0} if into is not None else {},
        compiler_params=_cp("arbitrary"),
    )(dxo, x, g, dgate, dup, wg, wu, *([into] if into is not None else []))


def _mm_tn(a, b, out_dtype, name, tm=512, tn=1024, ride=()):
    L, M = a.shape
    N = b.shape[1]
    tm, tn = min(tm, M), min(tn, N)
    while M % tm:
        tm //= 2
    while N % tn:
        tn //= 2

    def body(a_ref, b_ref, o_ref):
        o_ref[...] = _dot(a_ref[...].astype(bf16), b_ref[...].astype(bf16), TN).astype(out_dtype)

    return _pallas(
        body, ride=ride, name=name, grid=(M // tm, N // tn),
        in_specs=[pl.BlockSpec((L, tm), lambda i, j: (0, i)), pl.BlockSpec((L, tn), lambda i, j: (0, j))],
        out_specs=pl.BlockSpec((tm, tn), lambda i, j: (i, j)),
        out_shape=S((M, N), out_dtype),
        compiler_params=_cp("parallel", "parallel"),
    )(a, b)


def _mix_in_bwd(dxo, x, g, du_s5, dv, w_in, tm, ride=()):
    L = x.shape[0]

    def body(dxo_ref, x_ref, g_ref, dus_ref, dv_ref, w_ref, dx_ref, dg_ref, dub_ref, dxh_ref):
        @pl.when(pl.program_id(0) == 0)
        def _():
            dg_ref[...] = jnp.zeros_like(dg_ref)

        dus = dus_ref[...].astype(bf16)
        dvb = dv_ref[...].astype(bf16)
        dub_ref[:, :S5_WIDTH] = dus
        dub_ref[:, S5_WIDTH:] = dvb
        dh = _dot(dus, w_ref[:S5_WIDTH, :]) + _dot(dvb, w_ref[S5_WIDTH:, :])
        dx, dg = _rms_bwd(x_ref[...], g_ref[...], dh)
        dx = dxo_ref[...] + dx
        dx_ref[...] = dx
        dxh_ref[...] = (0.5 * dx).astype(bf16)
        dg_ref[...] += dg

    row = lambda c: pl.BlockSpec((tm, c), lambda i: (i, 0))
    return _pallas(
        body, ride=ride, name="mix_in_bwd", grid=(L // tm,),
        in_specs=[row(D_MODEL), row(D_MODEL), _full((1, D_MODEL)), row(S5_WIDTH), row(2 * CONV_WIDTH),
                  _full((IN_COLS, D_MODEL))],
        out_specs=[row(D_MODEL), _full((1, D_MODEL)), row(IN_COLS), row(D_MODEL)],
        out_shape=[S((L, D_MODEL), f32), S((1, D_MODEL), f32), S((L, IN_COLS), bf16), S((L, D_MODEL), bf16)],
        compiler_params=_cp("arbitrary"),
    )(dxo, x, g, du_s5, dv, w_in)


def _dw_out(y_s5, y_conv, dxb, tn=512):
    L = dxb.shape[0]

    def body(ys_ref, yc_ref, b_ref, o_ref):
        b = b_ref[...]
        o_ref[:S5_WIDTH, :] = _dot(ys_ref[...], b, TN).astype(bf16)
        o_ref[S5_WIDTH:, :] = _dot(yc_ref[...], b, TN).astype(bf16)

    return pl.pallas_call(
        body, name="dw_out", grid=(D_MODEL // tn,),
        in_specs=[_full((L, S5_WIDTH)), _full((L, CONV_WIDTH)), pl.BlockSpec((L, tn), lambda j: (0, j))],
        out_specs=pl.BlockSpec((S5_WIDTH + CONV_WIDTH, tn), lambda j: (0, j)),
        out_shape=S((S5_WIDTH + CONV_WIDTH, D_MODEL), bf16),
        compiler_params=_cp("parallel"),
    )(y_s5, y_conv, dxb)


def _mix_out_bwd(dx, w_out, tm, ride=()):
    L = dx.shape[0]

    def body(dx_ref, w_ref, dys_ref, dyc_ref, dxb_ref):
        dxb = dx_ref[...].astype(bf16)
        dxb_ref[...] = dxb
        dys_ref[...] = _dot(dxb, w_ref[:S5_WIDTH, :], NT)
        dyc_ref[...] = _dot(dxb, w_ref[S5_WIDTH:, :], NT)

    row = lambda c: pl.BlockSpec((tm, c), lambda i: (i, 0))
    return _pallas(
        body, ride=ride, name="mix_out_bwd", grid=(L // tm,),
        in_specs=[row(D_MODEL), _full((D_MODEL, D_MODEL))],
        out_specs=[row(S5_WIDTH), row(CONV_WIDTH), row(D_MODEL)],
        out_shape=[S((L, S5_WIDTH), f32), S((L, CONV_WIDTH), f32), S((L, D_MODEL), bf16)],
        compiler_params=_cp("parallel"),
    )(dx, w_out)


def _s5_discretise(lam_re, lam_im, log_dt, b_re, b_im):
    dt = jnp.exp(log_dt)
    mag = jnp.exp(lam_re * dt)
    abar_re = mag * jnp.cos(lam_im * dt)
    abar_im = mag * jnp.sin(lam_im * dt)
    den = lam_re * lam_re + lam_im * lam_im
    num_re = abar_re - 1.0
    f_re = ((num_re * lam_re + abar_im * lam_im) / den)[:, None, :]
    f_im = ((abar_im * lam_re - num_re * lam_im) / den)[:, None, :]
    return abar_re, abar_im, f_re * b_re - f_im * b_im, f_re * b_im + f_im * b_re


def _s5_params(lam_re, lam_im, log_dt, b_re, b_im):
    def body(lr, li, ld, br, bi, ar_ref, ai_ref, bbr_ref, bbi_ref):
        ar, ai, bbr, bbi = _s5_discretise(lr[...], li[...], ld[...], br[...], bi[...])
        ar_ref[...], ai_ref[...], bbr_ref[...], bbi_ref[...] = ar, ai, bbr, bbi

    gp = S((S5_GROUPS, S5_STATE), f32)
    gcp = S((S5_GROUPS, S5_GROUP_CH, S5_STATE), f32)
    return pl.pallas_call(body, name="s5_params", out_shape=[gp, gp, gcp, gcp])(lam_re, lam_im, log_dt, b_re, b_im)


def _s5_params_bwd(lam_re, lam_im, log_dt, b_re, b_im, d_ar, d_ai, d_bbr, d_bbi):
    def body(lr, li, ld, br, bi, car, cai, cbr, cbi, o_lr, o_li, o_ld, o_br, o_bi):
        _, vjp = jax.vjp(_s5_discretise, lr[...], li[...], ld[...], br[...], bi[...])
        o_lr[...], o_li[...], o_ld[...], o_br[...], o_bi[...] = vjp((car[...], cai[...], cbr[...], cbi[...]))

    gp = S((S5_GROUPS, S5_STATE), f32)
    gcp = S((S5_GROUPS, S5_GROUP_CH, S5_STATE), f32)
    return pl.pallas_call(body, name="s5_params_bwd", out_shape=[gp, gp, S((S5_GROUPS, 1), f32), gcp, gcp])(
        lam_re, lam_im, log_dt, b_re, b_im, d_ar, d_ai, d_bbr, d_bbi)


def _cmul(ar, ai, br, bi):
    return ar * br - ai * bi, ar * bi + ai * br


def _segment_starts(er, ei, ar, ai, steps, reverse):
    pr, pi = ar, ai
    n = 1
    while n < steps:
        pr, pi = _cmul(pr, pi, pr, pi)
        n *= 2
    assert n == steps
    row = lax.broadcasted_iota(jnp.int32, (SEGMENTS, SCAN_LANES), 0)
    hr = jnp.zeros((1, SCAN_LANES), f32)
    hi = jnp.zeros((1, SCAN_LANES), f32)
    out_r = jnp.zeros((SEGMENTS, SCAN_LANES), f32)
    out_i = jnp.zeros((SEGMENTS, SCAN_LANES), f32)
    order = range(SEGMENTS - 1, 0, -1) if reverse else range(0, SEGMENTS - 1)
    for r in order:
        qr, qi = _cmul(pr, pi, hr, hi)
        hr, hi = qr + er[r:r + 1, :], qi + ei[r:r + 1, :]
        nxt = r - 1 if reverse else r + 1
        out_r = jnp.where(row == nxt, hr, out_r)
        out_i = jnp.where(row == nxt, hi, out_i)
    return out_r, out_i


def _s5_read_bwd(dout, y_lin, u, d_skip, w_glu, b_glu, tm):
    L = u.shape[0]

    def body(do_ref, yl_ref, u_ref, d_ref, w_ref, b_ref, dyl_ref, du_ref, dd_ref, dw_ref, db_ref):
        @pl.when(pl.program_id(0) == 0)
        def _():
            dd_ref[...] = jnp.zeros_like(dd_ref)
            dw_ref[...] = jnp.zeros_like(dw_ref)
            db_ref[...] = jnp.zeros_like(db_ref)

        u, d, dout = u_ref[...], d_ref[...], do_ref[...]
        y, gelu_vjp = jax.vjp(_gelu, yl_ref[...] + d * u)
        yb = y.astype(bf16)
        sig = _sigmoid(_dot(yb, w_ref[...]) + b_ref[...])
        dz = dout * y * sig * (1.0 - sig)
        dzb = dz.astype(bf16)
        dy = dout * sig + _dot(dzb, w_ref[...], NT)
        (dyp,) = gelu_vjp(dy)
        dyl_ref[...] = dyp.astype(bf16)
        du_ref[...] = d * dyp
        dd_ref[...] += _rows8(dyp * u)
        db_ref[...] += _rows8(dz)
        dw_ref[...] += _dot(yb, dzb, TN)

    row = pl.BlockSpec((tm, S5_WIDTH), lambda i: (i, 0))
    vec = _full((1, S5_WIDTH))
    part = _full((8, S5_WIDTH))
    return pl.pallas_call(
        body, name="s5_read_bwd", grid=(L // tm,),
        in_specs=[row, row, row, vec, _full((S5_WIDTH, S5_WIDTH)), vec],
        out_specs=[row, row, part, _full((S5_WIDTH, S5_WIDTH)), part],
        out_shape=[S((L, S5_WIDTH), bf16), S((L, S5_WIDTH), f32), S((8, S5_WIDTH), f32),
                   S((S5_WIDTH, S5_WIDTH), f32), S((8, S5_WIDTH), f32)],
        compiler_params=_cp("arbitrary"),
    )(dout, y_lin, u, d_skip, w_glu, b_glu)


S5_CHUNK_CH = SCAN_LANES // S5_STATE * S5_GROUP_CH


def _s5_two_phase(L, bi):
    rows = bi * SEGMENTS
    nb = L // rows
    whole = pltpu.VMEM((L // SEGMENTS, SEGMENTS, SCAN_LANES), f32)
    mat = pl.BlockSpec((S5_CHUNK_CH, SCAN_LANES), lambda c, j: (c, c))
    vec = pl.BlockSpec((1, SCAN_LANES), lambda c, j: (0, c))
    tile = pl.BlockSpec((SEGMENTS, SCAN_LANES), lambda c, j: (0, c))
    return rows, nb, whole, mat, vec, tile


def _s5_forward(u, a_re, a_im, bb_re, bb_im, cc_re, cc_im, bi, ride=()):
    L = u.shape[0]
    rows, nb, whole, mat, vec, _ = _s5_two_phase(L, bi)

    def body(u_ref, ar_ref, ai_ref, br_ref, bi_ref, cr_ref, ci_ref, sr_ref, si_ref, yl_ref, hr_ref, hi_ref, dr_ref, di_ref):
        j = pl.program_id(1)
        ar = jnp.broadcast_to(ar_ref[...], (SEGMENTS, SCAN_LANES))
        ai = jnp.broadcast_to(ai_ref[...], (SEGMENTS, SCAN_LANES))

        @pl.when(j == 0)
        def _():
            hr_ref[...] = jnp.zeros_like(hr_ref)
            hi_ref[...] = jnp.zeros_like(hi_ref)

        @pl.when(j < nb)
        def _():
            base = j * bi
            ub = u_ref[...].astype(bf16)
            dr_ref[pl.ds(base, bi)] = _dot(ub, br_ref[...]).reshape(bi, SEGMENTS, SCAN_LANES)
            di_ref[pl.ds(base, bi)] = _dot(ub, bi_ref[...]).reshape(bi, SEGMENTS, SCAN_LANES)

            def step(i, c):
                pr, pi = _cmul(ar, ai, c[0], c[1])
                return pr + dr_ref[base + i], pi + di_ref[base + i]

            hr_ref[...], hi_ref[...] = lax.fori_loop(0, bi, step, (hr_ref[...], hi_ref[...]), unroll=True)

        @pl.when(j == nb - 1)
        def _():
            hr_ref[...], hi_ref[...] = _segment_starts(hr_ref[...], hi_ref[...], ar_ref[...], ai_ref[...], L // SEGMENTS, False)

        @pl.when(j >= nb)
        def _():
            base = (j - nb) * bi

            def step(i, c):
                pr, pi = _cmul(ar, ai, c[0], c[1])
                nr, nim = pr + dr_ref[base + i], pi + di_ref[base + i]
                dr_ref[base + i] = nr
                di_ref[base + i] = nim
                return nr, nim

            hr_ref[...], hi_ref[...] = lax.fori_loop(0, bi, step, (hr_ref[...], hi_ref[...]), unroll=True)
            sr = dr_ref[pl.ds(base, bi)].reshape(rows, SCAN_LANES).astype(bf16)
            si = di_ref[pl.ds(base, bi)].reshape(rows, SCAN_LANES).astype(bf16)
            sr_ref[...] = sr
            si_ref[...] = si
            yl_ref[...] = _dot(sr, cr_ref[...], NT) - _dot(si, ci_ref[...], NT)

    u_spec = pl.BlockSpec((rows, S5_CHUNK_CH), lambda c, j: (jnp.minimum(j, nb - 1), c))
    late = lambda width: pl.BlockSpec((rows, width), lambda c, j: (jnp.maximum(j - nb, 0), c))
    return _pallas(
        body, ride=ride, name="s5_forward", grid=(S5_LANES // SCAN_LANES, 2 * nb),
        in_specs=[u_spec, vec, vec, mat, mat, mat, mat],
        out_specs=[late(SCAN_LANES), late(SCAN_LANES), late(S5_CHUNK_CH)],
        out_shape=[S((L, S5_LANES), bf16)] * 2 + [S((L, S5_WIDTH), f32)],
        scratch_shapes=[pltpu.VMEM((SEGMENTS, SCAN_LANES), f32)] * 2 + [whole] * 2,
        compiler_params=_cp("parallel", "arbitrary"),
    )(u, a_re, a_im, bb_re, bb_im, cc_re, cc_im)


def _s5_backward(dy, u, du_skip, s_re, s_im, a_re, a_im, bb_re, bb_im, cc_re, cc_im, bi, ride=()):
    L = u.shape[0]
    rows, nb, whole, mat, vec, tile = _s5_two_phase(L, bi)
    per = rows // 16

    def body(dy_ref, u_ref, dus_ref, sr_ref, si_ref, pr_ref, pi_ref, lr_ref, li_ref, ar_ref, ai_ref, br_ref, bi_ref, cr_ref,
             ci_ref, du_ref, dar_ref, dai_ref, dbr_ref, dbi_ref, dcr_ref, dci_ref, hr_ref, hi_ref, gr_ref, gi_ref, fr_ref, fi_ref):
        j = pl.program_id(1)
        ar = jnp.broadcast_to(ar_ref[...], (SEGMENTS, SCAN_LANES))
        ai = jnp.broadcast_to(ai_ref[...], (SEGMENTS, SCAN_LANES))

        @pl.when(j == 0)
        def _():
            for ref in (hr_ref, hi_ref, dar_ref, dai_ref, dbr_ref, dbi_ref, dcr_ref, dci_ref):
                ref[...] = jnp.zeros_like(ref)

        @pl.when(j < nb)
        def _():
            base = (nb - 1 - j) * bi
            dy = dy_ref[...]
            gr_ref[pl.ds(base, bi)] = _dot(dy, cr_ref[...]).reshape(bi, SEGMENTS, SCAN_LANES)
            gi_ref[pl.ds(base, bi)] = (-_dot(dy, ci_ref[...])).reshape(bi, SEGMENTS, SCAN_LANES)

            def step(n, c):
                i = base + bi - 1 - n
                qr, qi = _cmul(ar, ai, c[0], c[1])
                return qr + gr_ref[i], qi + gi_ref[i]

            hr_ref[...], hi_ref[...] = lax.fori_loop(0, bi, step, (hr_ref[...], hi_ref[...]), unroll=True)

        @pl.when(j == nb - 1)
        def _():
            hr_ref[...], hi_ref[...] = _segment_starts(hr_ref[...], hi_ref[...], ar_ref[...], ai_ref[...], L // SEGMENTS, True)

        @pl.when(j >= nb)
        def _():
            blk = 2 * nb - 1 - j
            base = blk * bi
            sr, si = sr_ref[...], si_ref[...]
            fr_ref[...] = sr.astype(f32).reshape(bi, SEGMENTS, SCAN_LANES)
            fi_ref[...] = si.astype(f32).reshape(bi, SEGMENTS, SCAN_LANES)

            def step(n, c):
                i = bi - 1 - n
                gr, gi, accr, acci = c
                qr, qi = _cmul(ar, ai, gr, gi)
                gr, gi = qr + gr_ref[base + i], qi + gi_ref[base + i]
                gr_ref[base + i] = gr
                gi_ref[base + i] = gi
                pr, pi = fr_ref[i - 1], fi_ref[i - 1]
                return gr, gi, accr + (gr * pr + gi * pi), acci + (gi * pr - gr * pi)

            gr, gi, accr, acci = lax.fori_loop(0, bi - 1, step, (hr_ref[...], hi_ref[...], dar_ref[...], dai_ref[...]), unroll=True)
            qr, qi = _cmul(ar, ai, gr, gi)
            gr, gi = qr + gr_ref[base], qi + gi_ref[base]
            gr_ref[base] = gr
            gi_ref[base] = gi
            hr_ref[...], hi_ref[...] = gr, gi
            row = lax.broadcasted_iota(jnp.int32, (SEGMENTS, SCAN_LANES), 0)
            older = lambda ref: ref[...].astype(f32)[SEGMENTS:, :]
            wrap_r = jnp.where(row == 0, 0.0, pltpu.roll(older(lr_ref), 1, 0))
            wrap_i = jnp.where(row == 0, 0.0, pltpu.roll(older(li_ref), 1, 0))
            pr = jnp.where(blk == 0, wrap_r, older(pr_ref))
            pi = jnp.where(blk == 0, wrap_i, older(pi_ref))
            dar_ref[...] = accr + gr * pr + gi * pi
            dai_ref[...] = acci + gi * pr - gr * pi

            g_re = gr_ref[pl.ds(base, bi)].reshape(rows, SCAN_LANES).astype(bf16)
            g_im = gi_ref[pl.ds(base, bi)].reshape(rows, SCAN_LANES).astype(bf16)
            ub = u_ref[...].astype(bf16)
            dy = dy_ref[...]
            du_ref[...] = dus_ref[...] + _dot(g_re, br_ref[...], NT) + _dot(g_im, bi_ref[...], NT)
            dbr_ref[...] += _dot(ub, g_re, TN)
            dbi_ref[...] += _dot(ub, g_im, TN)
            dcr_ref[...] += _dot(dy, sr, TN)
            dci_ref[...] -= _dot(dy, si, TN)

    block = lambda c, j: jnp.where(j < nb, nb - 1 - j, 2 * nb - 1 - j)
    late_block = lambda c, j: jnp.minimum(2 * nb - 1 - j, nb - 1)
    both = pl.BlockSpec((rows, S5_CHUNK_CH), lambda c, j: (block(c, j), c))
    chan = pl.BlockSpec((rows, S5_CHUNK_CH), lambda c, j: (late_block(c, j), c))
    state = pl.BlockSpec((rows, SCAN_LANES), lambda c, j: (late_block(c, j), c))
    prev = pl.BlockSpec((16, SCAN_LANES), lambda c, j: (jnp.maximum(late_block(c, j) * per - 1, 0), c))
    last = pl.BlockSpec((16, SCAN_LANES), lambda c, j: (L // 16 - 1, c))
    grad = pl.BlockSpec((S5_CHUNK_CH, SCAN_LANES), lambda c, j: (c, 0))
    return _pallas(
        body, ride=ride, name="s5_backward", grid=(S5_LANES // SCAN_LANES, 2 * nb),
        in_specs=[both, chan, chan, state, state, prev, prev, last, last, vec, vec, mat, mat, mat, mat],
        out_specs=[chan, tile, tile, grad, grad, grad, grad],
        out_shape=[S((L, S5_WIDTH), f32)] + [S((SEGMENTS, S5_LANES), f32)] * 2 + [S((S5_WIDTH, SCAN_LANES), f32)] * 4,
        scratch_shapes=[pltpu.VMEM((SEGMENTS, SCAN_LANES), f32)] * 2 + [whole] * 2 + [pltpu.VMEM((bi, SEGMENTS, SCAN_LANES), f32)] * 2,
        compiler_params=_cp("parallel", "arbitrary"),
    )(dy, u, du_skip, s_re, s_im, s_re, s_im, s_re, s_im, a_re, a_im, bb_re, bb_im, cc_re, cc_im)


def _s5_gate(y_lin, u, d_skip, w_glu, b_glu, tm, ride=()):
    L = u.shape[0]

    def body(yl_ref, u_ref, d_ref, w_ref, b_ref, o_ref):
        y = _gelu(yl_ref[...] + d_ref[...] * u_ref[...])
        z = _dot(y.astype(bf16), w_ref[...]) + b_ref[...]
        o_ref[...] = (y * _sigmoid(z)).astype(bf16)

    row = pl.BlockSpec((tm, S5_WIDTH), lambda i: (i, 0))
    vec = _full((1, S5_WIDTH))
    return _pallas(
        body, ride=ride, name="s5_gate", grid=(L // tm,),
        in_specs=[row, row, vec, _full((S5_WIDTH, S5_WIDTH)), vec],
        out_specs=row, out_shape=S((L, S5_WIDTH), bf16),
        compiler_params=_cp("parallel"),
    )(y_lin, u, d_skip, w_glu, b_glu)


def _group_mean(x, avg):
    return _dot(x.astype(bf16), avg)


def _conv_act(zn, ln_g, ln_b):
    t = zn * ln_g + ln_b
    return t * _sigmoid(t)


def _glu_padded(v_ref, halo_ref, zpad_ref, tm):
    v = v_ref[...]
    vh = halo_ref[...]
    zh = vh[:, :CONV_WIDTH] * _sigmoid(vh[:, CONV_WIDTH:])
    zpad_ref[:CONV_HALO, :] = jnp.where(pl.program_id(0) > 0, zh, 0.0)
    zpad_ref[CONV_HALO:CONV_HALO + tm, :] = v[:, :CONV_WIDTH] * _sigmoid(v[:, CONV_WIDTH:])
    zpad_ref[CONV_HALO + tm:, :] = jnp.zeros((8, CONV_WIDTH), f32)


def _shifted(pad_ref, sh_ref, tm):
    for b in range(8):
        sh_ref[b] = pad_ref[pl.ds(b, tm + CONV_HALO), :]


def _window(sh_ref, r0, off, rows):
    start = r0 + 8 * (off // 8)
    return sh_ref[off % 8, pl.ds(start if isinstance(start, int) else pl.multiple_of(start, 8), rows), :]


def _tap_sum(w_ref, sh_ref, taps, out_ref, tm, bias):
    for r0 in range(0, tm, CONV_ROWS):
        acc = jnp.zeros((CONV_ROWS, CONV_WIDTH), f32) + bias
        for k, off in taps:
            acc = acc + w_ref[k:k + 1, :] * _window(sh_ref, r0, off, CONV_ROWS)
        out_ref[r0:r0 + CONV_ROWS, :] = acc


FWD_TAPS = [(k, CONV_HALO - (CONV_K - 1) + k) for k in range(CONV_K)]
BWD_TAPS = [(k, CONV_K - 1 - k) for k in range(CONV_K)]


def _conv_specs(tm):
    per = tm // CONV_HALO
    vrow = pl.BlockSpec((tm, 2 * CONV_WIDTH), lambda i: (i, 0))
    vhalo = pl.BlockSpec((CONV_HALO, 2 * CONV_WIDTH), lambda i: (jnp.maximum(i * per - 1, 0), 0))
    return vrow, vhalo


def _conv_scratch(tm):
    return [pltpu.VMEM((tm + CONV_HALO + 8, CONV_WIDTH), f32), pltpu.VMEM((8, tm + CONV_HALO, CONV_WIDTH), f32)]


def _conv_fwd(v, w_dw, b_dw, ln_g, ln_b, avg, tm, ride=()):
    L = v.shape[0]

    def body(v_ref, halo_ref, w_ref, b_ref, g_ref, bb_ref, avg_ref, o_ref, zc_ref, zpad_ref, zs_ref):
        _glu_padded(v_ref, halo_ref, zpad_ref, tm)
        _shifted(zpad_ref, zs_ref, tm)
        _tap_sum(w_ref, zs_ref, FWD_TAPS, zc_ref, tm, b_ref[...])
        zc = zc_ref[...]
        xc = zc - _group_mean(zc, avg_ref[...])
        zn = xc * lax.rsqrt(_group_mean(xc * xc, avg_ref[...]) + EPS)
        o_ref[...] = _conv_act(zn, g_ref[...], bb_ref[...]).astype(bf16)

    vrow, vhalo = _conv_specs(tm)
    vec = _full((1, CONV_WIDTH))
    row = pl.BlockSpec((tm, CONV_WIDTH), lambda i: (i, 0))
    return _pallas(
        body, ride=ride, name="conv_fwd", grid=(L // tm,),
        in_specs=[vrow, vhalo, _full((CONV_HALO, CONV_WIDTH)), vec, vec, vec, _full((CONV_WIDTH, CONV_WIDTH))],
        out_specs=[row, row], out_shape=[S((L, CONV_WIDTH), bf16), S((L, CONV_WIDTH), f32)],
        scratch_shapes=_conv_scratch(tm),
        compiler_params=_cp("arbitrary"),
    )(v, v, w_dw, b_dw, ln_g, ln_b, avg)


def _conv_bwd_norm(dout, zc, ln_g, ln_b, avg, tm):
    L = zc.shape[0]

    def body(do_ref, zc_ref, g_ref, bb_ref, avg_ref, dzc_ref, dg_ref, db_ref, dbd_ref):
        @pl.when(pl.program_id(0) == 0)
        def _():
            dg_ref[...] = jnp.zeros_like(dg_ref)
            db_ref[...] = jnp.zeros_like(db_ref)
            dbd_ref[...] = jnp.zeros_like(dbd_ref)

        avg = avg_ref[...]
        zc = zc_ref[...]
        xc = zc - _group_mean(zc, avg)
        rstd = lax.rsqrt(_group_mean(xc * xc, avg) + EPS)
        xhat = xc * rstd
        _, act_vjp = jax.vjp(_conv_act, xhat, g_ref[...], bb_ref[...])
        dxhat, dg, db = act_vjp(do_ref[...])
        dzc = rstd * (dxhat - _group_mean(dxhat, avg) - xhat * _group_mean(dxhat * xhat, avg))
        dzc_ref[...] = dzc
        dg_ref[0:1, :] += dg
        db_ref[0:1, :] += db
        dbd_ref[...] += _rows8(dzc)

    vec = _full((1, CONV_WIDTH))
    row = pl.BlockSpec((tm, CONV_WIDTH), lambda i: (i, 0))
    part = _full((8, CONV_WIDTH))
    return pl.pallas_call(
        body, name="conv_bwd_norm", grid=(L // tm,),
        in_specs=[row, row, vec, vec, _full((CONV_WIDTH, CONV_WIDTH))],
        out_specs=[row, part, part, part],
        out_shape=[S((L, CONV_WIDTH), f32)] + [S((8, CONV_WIDTH), f32)] * 3,
        compiler_params=_cp("arbitrary"),
    )(dout, zc, ln_g, ln_b, avg)


def _conv_bwd_taps(dzc, v, w_dw, tm, ride=()):
    L = v.shape[0]
    nt = L // tm
    per = tm // CONV_HALO

    def body(d_ref, dn_ref, v_ref, w_ref, dv_ref, dw_ref, dpad_ref, ds_ref, dz_ref, z_ref):
        i = pl.program_id(0)

        @pl.when(i == 0)
        def _():
            dw_ref[...] = jnp.zeros_like(dw_ref)

        v = v_ref[...]
        sig = _sigmoid(v[:, CONV_WIDTH:])
        z_ref[...] = v[:, :CONV_WIDTH] * sig
        dpad_ref[:tm, :] = d_ref[...]
        dpad_ref[tm:tm + CONV_HALO, :] = jnp.where(i < nt - 1, dn_ref[...], 0.0)
        dpad_ref[tm + CONV_HALO:, :] = jnp.zeros((8, CONV_WIDTH), f32)
        _shifted(dpad_ref, ds_ref, tm)
        _tap_sum(w_ref, ds_ref, BWD_TAPS, dz_ref, tm, 0.0)

        for first in range(0, CONV_K, 8):
            taps = BWD_TAPS[first:first + 8]

            accs = [jnp.zeros((8, CONV_WIDTH), f32) for _ in taps]
            for r0 in range(0, tm, 8):
                z = z_ref[r0:r0 + 8, :]
                accs = [acc + z * _window(ds_ref, r0, off, 8) for acc, (_, off) in zip(accs, taps)]
            for acc, (k, _) in zip(accs, taps):
                dw_ref[k] += acc

        dz = dz_ref[...]
        dv_ref[:, :CONV_WIDTH] = dz * sig
        dv_ref[:, CONV_WIDTH:] = dz * v[:, :CONV_WIDTH] * sig * (1.0 - sig)

    vrow, _ = _conv_specs(tm)
    row = pl.BlockSpec((tm, CONV_WIDTH), lambda i: (i, 0))
    nxt = pl.BlockSpec((CONV_HALO, CONV_WIDTH), lambda i: (jnp.minimum((i + 1) * per, nt * per - 1), 0))
    return _pallas(
        body, ride=ride, name="conv_bwd_taps", grid=(nt,),
        in_specs=[row, nxt, vrow, _full((CONV_HALO, CONV_WIDTH))],
        out_specs=[vrow, _full((CONV_HALO, 8, CONV_WIDTH))],
        out_shape=[S((L, 2 * CONV_WIDTH), f32), S((CONV_HALO, 8, CONV_WIDTH), f32)],
        scratch_shapes=_conv_scratch(tm) + [pltpu.VMEM((tm, CONV_WIDTH), f32)] * 2,
        compiler_params=_cp("arbitrary"),
    )(dzc, dzc, v, w_dw)


def _to_segments(a):
    L, c = a.shape
    return a.reshape(SEGMENTS, L // SEGMENTS, c).transpose(1, 0, 2).reshape(L, c)


def _from_segments(a):
    L, c = a.shape
    return a.reshape(L // SEGMENTS, SEGMENTS, c).transpose(1, 0, 2).reshape(L, c)


def _block_diag(ms):
    n = len(ms)

    def body(*refs):
        for a in range(n):
            out = refs[n + a]
            out[...] = jnp.zeros_like(out)
            for g in range(S5_GROUPS):
                rows = slice(g * S5_GROUP_CH, (g + 1) * S5_GROUP_CH)
                out[rows, g * S5_STATE:(g + 1) * S5_STATE] = refs[a][rows, :].astype(bf16)

    return pl.pallas_call(body, name="s5_block_diag", out_shape=[S((S5_WIDTH, S5_LANES), bf16)] * n,
                          compiler_params=pltpu.CompilerParams(vmem_limit_bytes=VMEM_LIMIT))(
        *[m.reshape(S5_WIDTH, S5_STATE) for m in ms])


def _diag_blocks(ms):
    n = len(ms)
    per_chunk = SCAN_LANES // S5_STATE

    def body(*refs):
        for a in range(n):
            for g in range(S5_GROUPS):
                rows = slice(g * S5_GROUP_CH, (g + 1) * S5_GROUP_CH)
                at = g % per_chunk * S5_STATE
                refs[n + a][rows, :] = refs[a][rows, at:at + S5_STATE]

    out = pl.pallas_call(body, name="s5_diag_blocks", out_shape=[S((S5_WIDTH, S5_STATE), f32)] * n,
                         compiler_params=pltpu.CompilerParams(vmem_limit_bytes=VMEM_LIMIT))(*ms)
    return [o.reshape(S5_GROUPS, S5_GROUP_CH, S5_STATE) for o in out]


class _NoExchanges:
    def before(self, point):
        return ()

    def after(self, point):
        pass

    def alone(self, point):
        pass


def _ffn_block(x, p, tag, tm, sched, head=None, mixed=None, mixer=None):
    point = tag + "_up"
    h, dadg, dadu, a, *x_in = _ffn_up(x, p[tag + "_norm"], p[tag + "_w_gate"], p[tag + "_w_up"], tm, tag, mixed,
                                      ride=sched.before(point))
    sched.after(point)
    x, = x_in or [x]
    if head is None:
        out = _ffn_down(x, a, p[tag + "_w_down"], tm, tag, mixer and tuple(p[n] for n in mixer), ride=sched.before(tag + "_down"))
        sched.after(tag + "_down")
    else:
        out = _ffn_down_loss(x, a, p[tag + "_w_down"], *head, tm, tag)
    return out, (x, h, dadg, dadu, a)


def _ffn_block_bwd(dxo, x, p, tag, saved, tm, grads, sched, parts=1, dxh=None):
    _, h, dadg, dadu, a = saved

    def weight_grad(which, lhs, rhs):
        point = tag + "_dw_" + which
        grads[tag + "_w_" + which] = _mm_tn(lhs, rhs, bf16, point, ride=sched.before(point))
        sched.after(point)

    if dxh is not None:
        weight_grad("down", a, dxh)
    dgate, dup, own_dxh = _ffn_bwd_act(dxo, p[tag + "_w_down"], dadg, dadu, tm, tag, ride=sched.before(tag + "_bwd_act"))
    sched.after(tag + "_bwd_act")
    weight_grad("gate", dgate, h)
    weight_grad("up", dup, h)
    if dxh is None:
        weight_grad("down", a, own_dxh)
    tiles = x.shape[0] // tm
    dx, dgs = None, []
    for k in range(parts):
        point = tag + "_bwd_in" + ("_%d" % k) * (parts > 1)
        dx, dg = _ffn_bwd_in(dxo, x, p[tag + "_norm"], dgate, dup, p[tag + "_w_gate"], p[tag + "_w_up"], tm, point,
                             tiles=(k * tiles // parts, tiles // parts), into=dx, ride=sched.before(point))
        sched.after(point)
        dgs.append(dg)
    grads[tag + "_norm"] = functools.reduce(jnp.add, dgs)
    return dx


def _local_step(x, target, p, grads, sched):
    L = x.shape[0]
    tm = min(512, L // 2)
    ni = L // SEGMENTS
    bi = min(128, ni)

    def carried(point, fn, *args):
        out = fn(*args, ride=sched.before(point))
        sched.after(point)
        return out

    (x1, h2, u_s5, v), saved1 = _ffn_block(x, p, "ffn1", tm, sched, mixer=("mix_norm", "w_in"))

    s5_in = (p["s5_lam_re"], p["s5_lam_im"], p["s5_log_dt"].reshape(S5_GROUPS, 1), p["s5_b_re"], p["s5_b_im"])
    abar_re, abar_im, bbar_re, bbar_im = _s5_params(*s5_in)
    a_re, a_im = abar_re.reshape(1, S5_LANES), abar_im.reshape(1, S5_LANES)
    bb_re, bb_im, cc_re, cc_im = _block_diag([bbar_re, bbar_im, p["s5_c_re"], p["s5_c_im"]])
    u_seg = _to_segments(u_s5)
    s_re, s_im, y_lin = carried("s5_forward", _s5_forward, u_seg, a_re, a_im, bb_re, bb_im, cc_re, cc_im, bi)
    y_s5 = _from_segments(_s5_gate(y_lin, u_seg, p["s5_d"], p["s5_w_glu"], p["s5_b_glu"], tm))
    w_dw = jnp.pad(p["conv_w_dw"], ((0, CONV_HALO - CONV_K), (0, 0)))
    heads = jnp.arange(CONV_WIDTH) // CONV_HEAD
    avg = ((heads[:, None] == heads[None, :]).astype(f32) / CONV_HEAD).astype(bf16)
    y_conv, zc = carried("conv_fwd", _conv_fwd, v, w_dw, p["conv_b_dw"], p["conv_ln_g"], p["conv_ln_b"], avg, tm)

    (dx3, grads["final_norm"], loss_terms), saved2 = _ffn_block(
        x1, p, "ffn2", tm, sched, head=(target, p["final_norm"].reshape(1, D_MODEL)), mixed=(y_s5, y_conv, p["w_out"]))
    x2 = saved2[0]
    grads["loss_terms"] = loss_terms

    dx2 = _ffn_block_bwd(dx3, x2, p, "ffn2", saved2, tm, grads, sched)

    dy_s5, dy_conv, dx2b = carried("mix_out_bwd", _mix_out_bwd, dx2, p["w_out"], tm)
    grads["w_out"] = _dw_out(y_s5, y_conv, dx2b)
    dy_lin, du_skip, dd8, grads["s5_w_glu"], dbg8 = _s5_read_bwd(
        _to_segments(dy_s5), y_lin, u_seg, p["s5_d"], p["s5_w_glu"], p["s5_b_glu"], tm)
    grads["s5_d"] = dd8.sum(axis=0, keepdims=True)
    grads["s5_b_glu"] = dbg8.sum(axis=0, keepdims=True)
    du_seg, da_re8, da_im8, dbb_re, dbb_im, dcc_re, dcc_im = carried(
        "s5_backward", _s5_backward, dy_lin, u_seg, du_skip, s_re, s_im, a_re, -a_im, bb_re, bb_im, cc_re, cc_im, bi)
    d_abar = lambda a8: a8.sum(axis=0).reshape(S5_GROUPS, S5_STATE)
    grads["s5_c_re"], grads["s5_c_im"], d_bbr, d_bbi = _diag_blocks([dcc_re, dcc_im, dbb_re, dbb_im])
    d_lr, d_li, d_ld, d_br, d_bi = _s5_params_bwd(*s5_in, d_abar(da_re8), d_abar(da_im8), d_bbr, d_bbi)
    grads["s5_lam_re"], grads["s5_lam_im"], grads["s5_log_dt"] = d_lr, d_li, d_ld.reshape(1, S5_GROUPS)
    grads["s5_b_re"], grads["s5_b_im"] = d_br, d_bi
    dzc, dlg8, dlb8, dbd8 = _conv_bwd_norm(dy_conv, zc, p["conv_ln_g"], p["conv_ln_b"], avg, tm)
    grads["conv_ln_g"] = dlg8.sum(axis=0, keepdims=True)
    grads["conv_ln_b"] = dlb8.sum(axis=0, keepdims=True)
    grads["conv_b_dw"] = dbd8.sum(axis=0, keepdims=True)
    dv, dw8 = carried("conv_bwd_taps", _conv_bwd_taps, dzc, v, w_dw, tm)
    grads["conv_w_dw"] = dw8.sum(axis=1)[:CONV_K]
    dx1, grads["mix_norm"], dub, dx1h = carried("mix_in_bwd", _mix_in_bwd, dx2, x1, p["mix_norm"], _from_segments(du_seg), dv,
                                                p["w_in"], tm)
    grads["w_in"] = _mm_tn(dub, h2, bf16, "dw_in")

    dx0 = _ffn_block_bwd(dx1, x, p, "ffn1", saved1, tm, grads, sched, parts=min(2, L // tm), dxh=dx1h)
    sched.alone("tail")
    return loss_terms, dx0


MESH = pl.DeviceIdType.MESH
ANY = pl.BlockSpec(memory_space=pl.ANY)


def _place():
    return lax.axis_index("x"), lax.axis_index("y"), lax.axis_index("c")


class _Exchange:
    def __init__(self, ins, out_shape, sems, start, finish):
        self.ins, self.out_shape, self.sems, self.start, self.finish = list(ins), list(out_shape), list(sems), start, finish
        self.out = None


def _pallas(body, *, ride=(), **kw):
    if not ride:
        return pl.pallas_call(body, **kw)

    def run(*args):
        out_shape = kw.get("out_shape", [])
        single = not isinstance(out_shape, (list, tuple))
        shapes = [out_shape] if single else list(out_shape)
        out_specs = [kw["out_specs"]] if single else list(kw.get("out_specs", []))
        grid = tuple(kw.get("grid", ()))
        scratch = list(kw.get("scratch_shapes", ()))
        n_in, n_out, n_scr = len(args), len(shapes), len(scratch)
        r_in = [len(e.ins) for e in ride]
        r_out = [len(e.out_shape) for e in ride]
        r_sem = [len(e.sems) for e in ride]

        def wrapped(*refs):
            own_in, refs = refs[:n_in], refs[n_in:]
            ex_in, refs = refs[:sum(r_in)], refs[sum(r_in):]
            own_out, refs = refs[:n_out], refs[n_out:]
            ex_out, refs = refs[:sum(r_out)], refs[sum(r_out):]
            own_scr, ex_sem = refs[:n_scr], refs[n_scr:]
            parts = []
            for e, ni, no, ns in zip(ride, r_in, r_out, r_sem):
                parts.append((e, ex_in[:ni], ex_out[:no], ex_sem[:ns]))
                ex_in, ex_out, ex_sem = ex_in[ni:], ex_out[no:], ex_sem[ns:]

            def at(step):
                def go():
                    for e, i, o, s in parts:
                        getattr(e, step)(i, o, s)
                if grid:
                    ids = [pl.program_id(d) for d in range(len(grid))]
                    when = [i == (0 if step == "start" else g - 1) for i, g in zip(ids, grid)]
                    pl.when(functools.reduce(lambda a, b: a & b, when))(go)
                else:
                    go()

            at("start")
            if body is not None:
                body(*own_in, *own_out, *own_scr)
            at("finish")

        outs = pl.pallas_call(
            wrapped, name=kw["name"], grid=grid,
            in_specs=list(kw.get("in_specs", [])) + [ANY] * sum(r_in),
            out_specs=out_specs + [ANY] * sum(r_out),
            out_shape=shapes + [s for e in ride for s in e.out_shape],
            scratch_shapes=scratch + [s for e in ride for s in e.sems],
            input_output_aliases=kw.get("input_output_aliases", {}),
            compiler_params=_cp(*["arbitrary"] * len(grid)),
        )(*args, *[a for e in ride for a in e.ins])
        own, rest = outs[:n_out], outs[n_out:]
        for e, no in zip(ride, r_out):
            e.out, rest = list(rest[:no]), rest[no:]
        return own[0] if single else own

    return run


def _exchange(ride, name):
    _pallas(None, ride=ride, name=name)()


def _gather(arrs):
    n = len(arrs)

    def copies(ins, outs, sems):
        send_sems, recv_sems, local_sems = sems
        x, y, c = _place()
        me, sibling = (x, y, c), (x, y, 1 - c)
        chips = [(1 - x, y), (x, 1 - y), (1 - x, 1 - y)]

        def place(a, block):
            return outs[a].at[block]

        def copy(a, k, block, to, src=None):
            px, py, pc = block
            dst = place(a, 4 * px + 2 * py + pc)
            return pltpu.make_async_remote_copy(
                src_ref=dst if src is None else src, dst_ref=dst, send_sem=send_sems.at[7 * a + k],
                recv_sem=recv_sems.at[7 * a + k], device_id=to, device_id_type=MESH)

        def own():
            local = [pltpu.make_async_copy(ins[a], place(a, 4 * x + 2 * y + c), local_sems.at[a]) for a in range(n)]
            remote = []
            for a in range(n):
                remote.append(copy(a, 0, me, sibling, src=ins[a]))
                remote += [copy(a, 1 + j, me, (*chip, c), src=ins[a]) for j, chip in enumerate(chips)]
            return local, remote

        return c, me, sibling, chips, copy, own

    def start(ins, outs, sems):
        local, remote = copies(ins, outs, sems)[-1]()
        for cp in local + remote:
            cp.start()

    def finish(ins, outs, sems):
        c, me, sibling, chips, copy, own = copies(ins, outs, sems)
        passed = []
        for j, chip in enumerate(chips):
            for a in range(n):
                copy(a, 1 + j, (*chip, c), me).wait_recv()
                passed.append(copy(a, 4 + j, (*chip, c), sibling))
                passed[-1].start()
        for a in range(n):
            copy(a, 0, sibling, me).wait_recv()
            for j, chip in enumerate(chips):
                copy(a, 4 + j, (*chip, 1 - c), me).wait_recv()
        local, remote = own()
        for cp in remote + passed:
            cp.wait_send()
        for cp in local:
            cp.wait()

    dma = pltpu.SemaphoreType.DMA
    shapes = [S((N_DEV, *a.shape), a.dtype) for a in arrs]
    return _Exchange(arrs, shapes, [dma((7 * n,)), dma((7 * n,)), dma((n,))], start, finish)


def _swap_with_sibling(gs):
    n = len(gs)

    def copies(ins, outs, sems):
        x, y, c = _place()
        return [pltpu.make_async_remote_copy(
            src_ref=ins[a].at[:, 1 - c], dst_ref=outs[a], send_sem=sems[0].at[a], recv_sem=sems[1].at[a],
            device_id=(x, y, 1 - c), device_id_type=MESH) for a in range(n)]

    def start(ins, outs, sems):
        for cp in copies(ins, outs, sems):
            cp.start()

    def finish(ins, outs, sems):
        for cp in copies(ins, outs, sems):
            cp.wait()

    dma = pltpu.SemaphoreType.DMA
    return _Exchange(gs, [S((N_CHIP, *g.shape[2:]), g.dtype) for g in gs], [dma((n,)), dma((n,))], start, finish)


def _swap_with_chips(ps):
    n = len(ps)

    def copies(ins, outs, sems):
        x, y, c = _place()
        q = 2 * x + y
        peers = [(x, 1 - y), (1 - x, y), (1 - x, 1 - y)]

        def copy(a, j, slot_from, slot_to):
            px, py = peers[j]
            return pltpu.make_async_remote_copy(
                src_ref=ins[a].at[slot_from], dst_ref=outs[a].at[slot_to], send_sem=sems[0].at[3 * a + j],
                recv_sem=sems[1].at[3 * a + j], device_id=(px, py, c), device_id_type=MESH)

        sends = lambda: [copy(a, j, 2 * peers[j][0] + peers[j][1], q) for a in range(n) for j in range(3)]
        lands = lambda: [copy(a, j, q, 2 * peers[j][0] + peers[j][1]) for a in range(n) for j in range(3)]
        return sends, lands

    def start(ins, outs, sems):
        for cp in copies(ins, outs, sems)[0]():
            cp.start()

    def finish(ins, outs, sems):
        sends, lands = copies(ins, outs, sems)
        for cp in lands():
            cp.wait_recv()
        for cp in sends():
            cp.wait_send()

    dma = pltpu.SemaphoreType.DMA
    return _Exchange(ps, [S(p.shape, p.dtype) for p in ps], [dma((3 * n,)), dma((3 * n,))], start, finish)


def _row_tile(rows, cols, itemsize):
    t = rows
    while t * cols * itemsize > (1 << 20) and t % 32 == 0:
        t //= 2
    return t


def _add_sibling(g4, st, core, name):
    _, R, C = st.shape
    tr = _row_tile(R, C, 1)

    def body(c_ref, g_ref, s_ref, o_ref):
        o_ref[...] = (g_ref[...].astype(f32) + s_ref[...].astype(f32)).astype(bf16)

    mine = pl.BlockSpec((None, None, tr, C), lambda q, i, c: (q, c[0], i, 0))
    return pl.pallas_call(
        body, name=name,
        grid_spec=pltpu.PrefetchScalarGridSpec(
            num_scalar_prefetch=1, grid=(N_CHIP, R // tr),
            in_specs=[mine,
                      pl.BlockSpec((None, tr, C), lambda q, i, c: (q, i, 0))],
            out_specs=pl.BlockSpec((None, tr, C), lambda q, i, c: (q, i, 0))),
        out_shape=S((N_CHIP, R, C), bf16),
        compiler_params=_cp("parallel", "parallel"),
    )(core, g4, st)


SMEM = pl.BlockSpec(memory_space=pltpu.SMEM)
VMEM = pl.BlockSpec(memory_space=pltpu.VMEM)


def _add_sibling_small(items, core, name):
    n = len(items)

    def body(c_ref, *refs):
        c = c_ref[0]
        for k in range(n):
            g_ref, s_ref, o_ref = refs[2 * k], refs[2 * k + 1], refs[2 * n + k]
            for q in range(N_CHIP):
                o_ref[q] = (g_ref[q, c].astype(f32) + s_ref[q].astype(f32)).astype(bf16)

    return pl.pallas_call(body, name=name, in_specs=[SMEM] + [VMEM] * (2 * n), out_specs=[VMEM] * n,
                          out_shape=[S(st.shape, bf16) for _, st in items],
                          compiler_params=pltpu.CompilerParams(vmem_limit_bytes=VMEM_LIMIT))(
        core, *[a for item in items for a in item])


def _adam_small(items, slots, name):
    n = len(items)

    def body(s_ref, *refs):
        ins, outs = refs[:5 * n], refs[5 * n:]
        for k in range(n):
            w_ref, m_ref, v_ref, p_ref, got_ref = ins[5 * k:5 * k + 5]
            g = p_ref[s_ref[0]].astype(f32)
            for j in range(1, N_CHIP):
                g = g + got_ref[s_ref[j]].astype(f32)
            outs[4 * k][...] = g
            outs[4 * k + 1][...], outs[4 * k + 2][...], outs[4 * k + 3][...] = _adamw(w_ref[...], g, m_ref[...], v_ref[...])

    out = pl.pallas_call(body, name=name, in_specs=[SMEM] + [VMEM] * (5 * n), out_specs=[VMEM] * (4 * n),
                         out_shape=[S(item[0].shape, f32) for item in items for _ in range(4)],
                         compiler_params=pltpu.CompilerParams(vmem_limit_bytes=VMEM_LIMIT))(
        slots, *[a for item in items for a in item])
    return [out[4 * k:4 * k + 4] for k in range(n)]


def _adamw(w, g, m, v):
    m = B1 * m + (1.0 - B1) * g
    v = B2 * v + (1.0 - B2) * (g * g)
    m_hat = m / (1.0 - B1 ** STEP)
    v_hat = v / (1.0 - B2 ** STEP)
    return -LR * (m_hat / (jnp.sqrt(v_hat) + ADAM_EPS) + WD * w), m, v


def _adam_sharded(items, slots, name):
    n = len(items)
    R, C = items[0][0].shape
    tr = _row_tile(R, C, 4 * n)

    def body(s_ref, *refs):
        ins, outs = refs[:7 * n], refs[7 * n:]
        for k in range(n):
            w_ref, m_ref, v_ref, p_ref, a_ref, b_ref, c_ref = ins[7 * k:7 * k + 7]
            g = p_ref[...].astype(f32) + a_ref[...].astype(f32) + b_ref[...].astype(f32) + c_ref[...].astype(f32)
            outs[4 * k][...] = g
            outs[4 * k + 1][...], outs[4 * k + 2][...], outs[4 * k + 3][...] = _adamw(w_ref[...], g, m_ref[...], v_ref[...])

    shard = pl.BlockSpec((tr, C), lambda i, s: (i, 0))
    slot = lambda k: pl.BlockSpec((None, tr, C), lambda i, s: (s[k], i, 0))
    out = pl.pallas_call(
        body, name=name,
        grid_spec=pltpu.PrefetchScalarGridSpec(
            num_scalar_prefetch=1, grid=(R // tr,),
            in_specs=[shard, shard, shard, slot(0), slot(1), slot(2), slot(3)] * n,
            out_specs=[shard] * (4 * n)),
        out_shape=[S((R, C), f32)] * (4 * n),
        compiler_params=_cp("parallel"),
    )(slots, *[a for w, m, v, part, got in items for a in (w, m, v, part, got, got, got)])
    return [out[4 * k:4 * k + 4] for k in range(n)]


def _adam_replicated(items, stacks, loss_at, name):
    n, ns = len(items), len(stacks)

    def body(*refs):
        stack_refs, ins, outs = refs[:ns], refs[ns:ns + 3 * n], refs[ns + 3 * n:]
        totals = []
        for ref in stack_refs:
            g = ref[0]
            for d in range(1, N_DEV):
                g = g + ref[d]
            totals.append(g)
        for i, (_, _, _, (s, k)) in enumerate(items):
            w_ref, m_ref, v_ref = ins[3 * i:3 * i + 3]
            g = totals[s] if k is None else totals[s][k]
            outs[4 * i][...] = g
            outs[4 * i + 1][...], outs[4 * i + 2][...], outs[4 * i + 3][...] = _adamw(w_ref[...], g, m_ref[...], v_ref[...])
        if loss_at is not None:
            outs[-1][...] = jnp.sum(totals[loss_at[0]], keepdims=True)

    flat = list(stacks) + [a for item in items for a in item[:3]]
    shapes = [S(item[0].shape, f32) for item in items for _ in range(4)] + ([S((1, 1), f32)] if loss_at is not None else [])
    out = pl.pallas_call(body, name=name, out_shape=shapes,
                         compiler_params=pltpu.CompilerParams(vmem_limit_bytes=VMEM_LIMIT))(*flat)
    return [out[4 * i:4 * i + 4] for i in range(n)], (out[-1] if loss_at is not None else None)


WEIGHTS = ["ffn1_norm", "ffn1_w_gate", "ffn1_w_up", "ffn1_w_down", "mix_norm", "w_in", "s5_lam_re", "s5_lam_im", "s5_log_dt",
           "s5_b_re", "s5_b_im", "s5_c_re", "s5_c_im", "s5_d", "s5_w_glu", "s5_b_glu", "conv_w_dw", "conv_b_dw", "conv_ln_g",
           "conv_ln_b", "w_out", "ffn2_norm", "ffn2_w_gate", "ffn2_w_up", "ffn2_w_down", "final_norm"]
SHARDED = ["ffn1_w_gate", "ffn1_w_up", "ffn1_w_down", "w_in", "s5_w_glu", "conv_w_dw", "w_out", "ffn2_w_gate", "ffn2_w_up",
           "ffn2_w_down"]
REPLICATED = [n for n in WEIGHTS if n not in SHARDED]
TRANSPOSED = ["ffn1_w_gate", "ffn1_w_up", "ffn2_w_gate", "ffn2_w_up", "w_in"]


def _shard_to_wire(n, w):
    if n == "conv_w_dw":
        return jnp.pad(w, ((0, CONV_HALO - CONV_K), (0, 0)))
    return w.astype(bf16)


def _to_wire(shards, ride):
    names = list(shards)
    shapes = [jax.eval_shape(functools.partial(_shard_to_wire, n), shards[n]) for n in names]

    def body(*refs):
        for src, dst in zip(refs[:len(names)], refs[len(names):]):
            (r, c), (rp, cp) = src.shape, dst.shape
            dst[:r, :c] = src[...].astype(dst.dtype)
            if cp > c:
                dst[:, c:] = jnp.zeros((rp, cp - c), dst.dtype)
            if rp > r:
                dst[r:, :] = jnp.zeros((rp - r, cp), dst.dtype)

    out = _pallas(body, ride=ride, name="to_wire", out_shape=shapes, in_specs=[pl.BlockSpec(memory_space=pltpu.VMEM)] * len(names),
                  out_specs=[pl.BlockSpec(memory_space=pltpu.VMEM)] * len(names))(*[shards[n] for n in names])
    return dict(zip(names, out))


def _gathered_to_full(n, g):
    if n == "conv_w_dw":
        return g.transpose(1, 0, 2).reshape(CONV_HALO, CONV_WIDTH)[:CONV_K]
    return g.reshape(N_DEV * g.shape[1], g.shape[2])


def _grad_to_blocks(n, g):
    if n == "conv_w_dw":
        g = jnp.pad(g, ((0, CONV_HALO - CONV_K), (0, 0)))
        g = g.reshape(g.shape[0], N_DEV, g.shape[1] // N_DEV).transpose(1, 0, 2)
    else:
        g = g.reshape(N_DEV, g.shape[0] // N_DEV, g.shape[1])
    return g.astype(bf16).reshape(N_CHIP, 2, *g.shape[1:])


REPLICATED_LATE = ["ffn1_norm"]
REPLICATED_HEAD = ["ffn2_norm", "final_norm"]
REPLICATED_MIX = ["mix_norm", "conv_b_dw", "conv_ln_g", "conv_ln_b"]
REPLICATED_S5 = [n for n in REPLICATED if n not in REPLICATED_LATE + REPLICATED_HEAD + REPLICATED_MIX]
REPLICATED_EARLY = REPLICATED_HEAD + REPLICATED_S5 + REPLICATED_MIX

PLAN = {
    "start": [("gather", ["ffn1_w_gate", "ffn1_w_up"])],
    "ffn1_up": [("gather", ["ffn1_w_down", "w_in", "w_out", "s5_w_glu", "conv_w_dw"])],
    "ffn1_down": [("gather", ["ffn2_w_gate"])],
    "s5_forward": [("gather", ["ffn2_w_up"])],
    "conv_fwd": [("gather", ["ffn2_w_down"])],
    "ffn2_dw_up": [("sibling", ["ffn2_w_gate"])],
    "ffn2_dw_down": [("sibling", ["ffn2_w_up"])],
    "mix_out_bwd": [("sibling", ["ffn2_w_down"]), ("replicated", REPLICATED_HEAD)],
    "s5_backward": [("chips", ["ffn2_w_gate", "ffn2_w_up"])],
    "conv_bwd_taps": [("chips", ["ffn2_w_down"])],
    "mix_in_bwd": [("replicated", REPLICATED_S5)],
    "ffn1_dw_down": [("sibling", ["w_in", "s5_w_glu", "conv_w_dw", "w_out"]), ("replicated", REPLICATED_MIX)],
    "ffn1_bwd_act": [("chips", ["w_in", "s5_w_glu", "conv_w_dw", "w_out"]), ("sibling", ["ffn1_w_down"])],
    "ffn1_dw_gate": [("chips", ["ffn1_w_down"])],
    "ffn1_dw_up": [("sibling", ["ffn1_w_gate"])],
    "ffn1_bwd_in_0": [("chips", ["ffn1_w_gate"]), ("sibling", ["ffn1_w_up"])],
    "ffn1_bwd_in_1": [("chips", ["ffn1_w_up"])],
    "tail": [("replicated", REPLICATED_LATE)],
}


class _Schedule:
    def __init__(self, wire, p, grads, core):
        self.wire, self.p, self.grads, self.core = wire, p, grads, core
        self.partial, self.reduced, self.pending = {}, {}, []
        self.stacks, self.shapes, self.everyone = [], [], {}

    def before(self, point):
        assert not self.pending
        for kind, names in PLAN.get(point, ()):
            if kind == "gather":
                given = [self.wire[n] for n in names]
                ex = _gather(given)
            elif kind == "sibling":
                given = [_grad_to_blocks(n, self.grads[n]) for n in names]
                ex = _swap_with_sibling(given)
            elif kind == "chips":
                given = [self.partial.pop(n) for n in names]
                ex = _swap_with_chips(given)
            else:
                names = names + ["loss_terms"] * (names is REPLICATED_HEAD)
                by_shape = {}
                for n in names:
                    by_shape.setdefault(self.p[n].shape if n in self.p else None, []).append(n)
                given = []
                for shape, members in by_shape.items():
                    for k, n in enumerate(members):
                        self.everyone[n] = (len(self.stacks) + len(given), k if len(members) > 1 else None)
                    parts = [self.grads[n].reshape(shape) if shape else self.grads[n] for n in members]
                    whole = jnp.stack(parts) if len(members) > 1 else parts[0]
                    self.shapes.append(whole.shape)
                    if whole.ndim >= 3 and whole.shape[-1] < 128:
                        whole = whole.reshape(*whole.shape[:-2], -1)
                    given.append(whole)
                ex = _gather(given)
            self.pending.append((kind, names, given, ex))
        return [ex for _, _, _, ex in self.pending]

    def after(self, point):
        for kind, names, given, ex in self.pending:
            if kind == "gather":
                for n, g in zip(names, ex.out):
                    self.p[n] = _gathered_to_full(n, g)
            elif kind == "sibling":
                if len(names) > 1:
                    sums = _add_sibling_small(list(zip(given, ex.out)), self.core, "reduce_add_" + names[0])
                else:
                    sums = [_add_sibling(given[0], ex.out[0], self.core, "reduce_add_" + names[0])]
                self.partial.update(zip(names, sums))
            elif kind == "chips":
                for n, part, got in zip(names, given, ex.out):
                    self.reduced[n] = (part, got)
            else:
                self.stacks += [g.reshape(N_DEV, *self.shapes[len(self.stacks) + i]) for i, g in enumerate(ex.out)]
        self.pending = []

    def alone(self, point):
        _exchange(self.before(point), point)
        self.after(point)


def kernel(x, ffn1_norm, ffn1_w_gate, ffn1_w_up, ffn1_w_down, mix_norm, w_in, s5_lam_re, s5_lam_im, s5_log_dt, s5_b_re, s5_b_im, s5_c_re, s5_c_im, s5_d, s5_w_glu, s5_b_glu, conv_w_dw, conv_b_dw, conv_ln_g, conv_ln_b, w_out, ffn2_norm, ffn2_w_gate, ffn2_w_up, ffn2_w_down, final_norm, loss_target, m_ffn1_norm, m_ffn1_w_gate, m_ffn1_w_up, m_ffn1_w_down, m_mix_norm, m_w_in, m_s5_lam_re, m_s5_lam_im, m_s5_log_dt, m_s5_b_re, m_s5_b_im, m_s5_c_re, m_s5_c_im, m_s5_d, m_s5_w_glu, m_s5_b_glu, m_conv_w_dw, m_conv_b_dw, m_conv_ln_g, m_conv_ln_b, m_w_out, m_ffn2_norm, m_ffn2_w_gate, m_ffn2_w_up, m_ffn2_w_down, m_final_norm, v_ffn1_norm, v_ffn1_w_gate, v_ffn1_w_up, v_ffn1_w_down, v_mix_norm, v_w_in, v_s5_lam_re, v_s5_lam_im, v_s5_log_dt, v_s5_b_re, v_s5_b_im, v_s5_c_re, v_s5_c_im, v_s5_d, v_s5_w_glu, v_s5_b_glu, v_conv_w_dw, v_conv_b_dw, v_conv_ln_g, v_conv_ln_b, v_w_out, v_ffn2_norm, v_ffn2_w_gate, v_ffn2_w_up, v_ffn2_w_down, v_final_norm):
    args = locals()
    w = {n: args[n] for n in WEIGHTS}
    m = {n: args["m_" + n] for n in WEIGHTS}
    v = {n: args["v_" + n] for n in WEIGHTS}
    xq, yq, cq = _place()
    q = 2 * xq + yq
    slots = jnp.stack([q, q ^ 1, q ^ 2, q ^ 3]).astype(jnp.int32)

    def shard2d(n, a):
        a = a.reshape(a.shape[-2:])
        return a.T if n in TRANSPOSED else a

    def view(n, a):
        if n.startswith("s5_b_") and a.ndim == 4:
            return a[0].transpose(0, 2, 1)
        return a[0] if a.ndim >= 3 else a.reshape(1, -1)

    def unview(n, a):
        return (a.transpose(0, 2, 1) if n.startswith("s5_b_") and a.ndim == 3 else a).reshape(w[n].shape)

    p = {n: view(n, w[n]) for n in REPLICATED}
    grads = {}
    first = PLAN["start"][0][1]
    wire = {n: _shard_to_wire(n, shard2d(n, w[n])) for n in first}
    sched = _Schedule(wire, p, grads, jnp.reshape(cq, (1,)).astype(jnp.int32))
    wire.update(_to_wire({n: shard2d(n, w[n]) for n in SHARDED if n not in first}, sched.before("start")))
    sched.after("start")
    _, dx = _local_step(x[0], loss_target[0], p, grads, sched)

    out = {}
    groups = [[n for n in SHARDED if n.startswith(tag)] for tag in ("ffn1", "ffn2")]
    for names in groups + [[n for n in SHARDED if not n.startswith("ffn")]]:
        def fit(n, a):
            a = shard2d(n, a)
            return jnp.pad(a, ((0, sched.reduced[n][1].shape[1] - a.shape[0]), (0, 0)))

        items = [(fit(n, w[n]), fit(n, m[n]), fit(n, v[n]), *sched.reduced[n]) for n in names]
        update = _adam_sharded if names[0].startswith("ffn") else _adam_small
        for n, res in zip(names, update(items, slots, "adam_" + names[0])):
            back = lambda r: r[:shard2d(n, w[n]).shape[0]]
            out[n] = [(back(r).T if n in TRANSPOSED else back(r)).reshape(w[n].shape) for r in res]

    for names in (REPLICATED_EARLY, REPLICATED_LATE):
        loss_at = sched.everyone["loss_terms"] if names is REPLICATED_EARLY else None
        used = sorted({sched.everyone[n][0] for n in names} | ({loss_at[0]} if loss_at else set()))
        at = lambda where: (used.index(where[0]), where[1])
        items = [(view(n, w[n]), view(n, m[n]), view(n, v[n]), at(sched.everyone[n])) for n in names]
        res, total = _adam_replicated(items, [sched.stacks[s] for s in used], loss_at and at(loss_at), "adam_" + names[0])
        for n, r in zip(names, res):
            out[n] = [unview(n, a) for a in r]
        if total is not None:
            loss = total.reshape(())

    return (loss, dx.reshape(x.shape), *[out[n][0] for n in WEIGHTS], *[out[n][1] for n in WEIGHTS],
            *[out[n][2] for n in WEIGHTS], *[out[n][3] for n in WEIGHTS])
```
